```python
import math
import jax
import jax.numpy as jnp
from jax import lax
import numpy as np

D_MODEL = 1024
BATCH = 8
SEQ = 2048
DEPTH = 1
DEC_BATCH = 128
DEC_SEQ = 8
PAST_LEN = 16384
PAGE_SIZE = 128

MIX_WIDTH = 2 * D_MODEL
SSD_WIDTH = MIX_WIDTH // 2
RWKV_WIDTH = MIX_WIDTH - SSD_WIDTH
SSD_HEAD_DIM = 64
SSD_HEADS = SSD_WIDTH // SSD_HEAD_DIM
SSD_GROUPS = 2
SSD_HPG = SSD_HEADS // SSD_GROUPS
SSD_STATE = 128
SSD_CONV = 4
SSD_CHUNK = 128
SSD_CONV_DIM = SSD_WIDTH + 2 * SSD_GROUPS * SSD_STATE
RWKV_HEAD_DIM = 64
RWKV_HEADS = RWKV_WIDTH // RWKV_HEAD_DIM
DECAY_LORA = 64
AAA_LORA = 64
GATE_LORA = 128
RWKV_PROJ = 3 * RWKV_WIDTH + DECAY_LORA + AAA_LORA + GATE_LORA
IN_PROJ = SSD_WIDTH + SSD_CONV_DIM + SSD_HEADS + RWKV_PROJ
IN_SPLITS = (SSD_WIDTH, SSD_WIDTH + SSD_CONV_DIM, SSD_WIDTH + SSD_CONV_DIM + SSD_HEADS)
XBC_SPLITS = (SSD_WIDTH, SSD_WIDTH + SSD_GROUPS * SSD_STATE)
RWKV_SPLITS = (RWKV_WIDTH, 2 * RWKV_WIDTH, 3 * RWKV_WIDTH, 3 * RWKV_WIDTH + DECAY_LORA,
               3 * RWKV_WIDTH + DECAY_LORA + AAA_LORA)
D_FF = -(-8 * D_MODEL // (3 * 256)) * 256
PLE_DIM = 256
NORM_EPS = 1e-6
GN_EPS = 64e-5

kernel_name = 'hymba_ssd_rwkv7_ple_step'


def rmsnorm(x, g):
    xf = x.astype(jnp.float32)
    y = xf * lax.rsqrt(jnp.mean(xf * xf, axis=-1, keepdims=True) + NORM_EPS)
    return (y * g.astype(jnp.float32)).astype(x.dtype)


def ssd_chunked(x, dt, a_head, bm, cm, h0):
    b, l, g, e, p = x.shape
    n = bm.shape[-1]
    q = SSD_CHUNK if l % SSD_CHUNK == 0 else l
    c = l // q
    f32 = jnp.float32
    x = x.astype(f32).reshape(b, c, q, g, e, p)
    dt = dt.astype(f32).reshape(b, c, q, g, e)
    bm = bm.astype(f32).reshape(b, c, q, g, n)
    cm = cm.astype(f32).reshape(b, c, q, g, n)
    a_cum = jnp.cumsum(dt * a_head.astype(f32), axis=2)
    seg = a_cum[:, :, :, None] - a_cum[:, :, None, :]
    causal = jnp.tril(jnp.ones((q, q), dtype=bool))[None, None, :, :, None, None]
    decay_qs = jnp.exp(jnp.where(causal, seg, -jnp.inf))
    cb = jnp.einsum('bcqgn,bcsgn->bcqsg', cm, bm)
    w_qs = cb[..., None] * decay_qs * dt[:, :, None]
    y_diag = jnp.einsum('bcqsge,bcsgep->bcqgep', w_qs, x)
    decay_end = jnp.exp(a_cum[:, :, -1:] - a_cum) * dt
    chunk_states = jnp.einsum('bcsgn,bcsge,bcsgep->bcgepn', bm, decay_end, x)
    chunk_decay = jnp.exp(a_cum[:, :, -1])

    def step(h, inp):
        s_c, d_c = inp
        return h * d_c[..., None, None] + s_c, h

    h_fin, h_in = lax.scan(step, h0.astype(f32),
                           (jnp.moveaxis(chunk_states, 1, 0), jnp.moveaxis(chunk_decay, 1, 0)))
    h_in = jnp.moveaxis(h_in, 0, 1)
    y_off = jnp.einsum('bcqgn,bcgepn->bcqgep', cm, h_in) * jnp.exp(a_cum)[..., None]
    return (y_diag + y_off).reshape(b, l, g, e, p), h_fin


def wkv_scan(r, decay, k, v, kk, a, s0):
    f32 = jnp.float32
    seq = tuple(jnp.moveaxis(t.astype(f32), 1, 0) for t in (r, decay, k, v, kk, a))

    def step(s, inp):
        r_t, w_t, k_t, v_t, kk_t, a_t = inp
        s_kk = jnp.einsum('bhvk,bhk->bhv', s, kk_t)
        s = (s * w_t[:, :, None, :] - s_kk[..., None] * (kk_t * a_t)[:, :, None, :]
             + v_t[..., None] * k_t[:, :, None, :])
        return s, jnp.einsum('bhvk,bhk->bhv', s, r_t)

    s_fin, out = lax.scan(step, s0.astype(f32), seq)
    return jnp.moveaxis(out, 0, 1), s_fin


def token_mixers(hn, conv_buf, shift_buf, ssm0, wkv0, w):
    b, l, _ = hn.shape
    dtype = hn.dtype
    f32 = jnp.float32
    proj = hn @ w['w_in']
    z, xbc, dt_raw, rw = jnp.split(proj, IN_SPLITS, axis=-1)

    xbc_full = jnp.concatenate([conv_buf.astype(xbc.dtype), xbc], axis=1)
    conv = lax.conv_general_dilated(
        xbc_full, w['conv_w'][:, None, :].astype(xbc.dtype), (1,), 'VALID',
        dimension_numbers=('NWC', 'WIO', 'NWC'), feature_group_count=SSD_CONV_DIM) + w['conv_b'].astype(xbc.dtype)
    conv_new = xbc_full[:, -(SSD_CONV - 1):]
    xbc_act = jax.nn.silu(conv)
    xs, bm, cm = jnp.split(xbc_act, XBC_SPLITS, axis=-1)
    xs = xs.reshape(b, l, SSD_GROUPS, SSD_HPG, SSD_HEAD_DIM)
    bm = bm.reshape(b, l, SSD_GROUPS, SSD_STATE)
    cm = cm.reshape(b, l, SSD_GROUPS, SSD_STATE)
    dt = jax.nn.softplus(dt_raw.astype(f32) + w['dt_bias'].astype(f32)).reshape(b, l, SSD_GROUPS, SSD_HPG)
    a_head = -jnp.exp(w['a_log'].astype(f32)).reshape(SSD_GROUPS, SSD_HPG)
    y, ssm_fin = ssd_chunked(xs, dt, a_head, bm, cm,
                             ssm0.reshape(b, SSD_GROUPS, SSD_HPG, SSD_HEAD_DIM, SSD_STATE))
    y = y + w['d_skip'].astype(f32).reshape(SSD_GROUPS, SSD_HPG, 1) * xs.astype(f32)
    yg = (y.reshape(b, l, SSD_WIDTH) * jax.nn.silu(z.astype(f32))).reshape(b, l, SSD_GROUPS, SSD_WIDTH // SSD_GROUPS)
    yg = yg * lax.rsqrt(jnp.mean(yg * yg, axis=-1, keepdims=True) + NORM_EPS)
    y_ssd = yg.reshape(b, l, SSD_WIDTH) * w['ssd_norm'].astype(f32)

    prev = jnp.concatenate([shift_buf.astype(rw.dtype), rw], axis=1)[:, :-1]
    shift_new = rw[:, -1:]
    u = rw + (prev - rw) * w['shift_mu']
    r, k, v, w_lo, a_lo, g_lo = jnp.split(u, RWKV_SPLITS, axis=-1)
    w_log = -jax.nn.softplus(-(w['w0'] + jnp.tanh(w_lo) @ w['w2']).astype(f32)) - 0.5
    decay = jnp.exp(-jnp.exp(w_log))
    a = jax.nn.sigmoid((w['a0'] + a_lo @ w['a2']).astype(f32))
    gate = (jax.nn.sigmoid(g_lo) @ w['g2']).astype(f32)

    def heads(t):
        return t.reshape(b, l, RWKV_HEADS, RWKV_HEAD_DIM)

    kk = heads((k * w['k_k']).astype(f32))
    kk = kk / jnp.maximum(jnp.sqrt(jnp.sum(kk * kk, axis=-1, keepdims=True)), 1e-12)
    k = k.astype(f32) * (1.0 + (a - 1.0) * w['k_a'].astype(f32))
    rh, kh, vh = heads(r.astype(f32)), heads(k), heads(v.astype(f32))
    o, wkv_fin = wkv_scan(rh, heads(decay), kh, vh, kk, heads(a), wkv0)
    mu = jnp.mean(o, axis=-1, keepdims=True)
    var = jnp.mean(jnp.square(o - mu), axis=-1, keepdims=True)
    on = ((o - mu) * lax.rsqrt(var + GN_EPS)).reshape(b, l, RWKV_WIDTH)
    on = on * w['ln_x_w'].astype(f32) + w['ln_x_b'].astype(f32)
    r_k = w['r_k'].astype(f32).reshape(RWKV_HEADS, RWKV_HEAD_DIM)
    bonus = jnp.sum(rh * kh * r_k, axis=-1, keepdims=True) * vh
    y_rwkv = (on + bonus.reshape(b, l, RWKV_WIDTH)) * gate

    out = jnp.concatenate([y_ssd, y_rwkv], axis=-1).astype(dtype) @ w['w_out']
    ssm_new = ssm_fin.reshape(b, SSD_HEADS, SSD_HEAD_DIM, SSD_STATE).astype(dtype)
    return out, conv_new.astype(dtype), shift_new.astype(dtype), ssm_new, wkv_fin.astype(dtype)


def layer(h, p, conv_buf, shift_buf, ssm0, wkv0, w):
    mix, conv_new, shift_new, ssm_new, wkv_new = token_mixers(
        rmsnorm(h, w['norm_mix']), conv_buf, shift_buf, ssm0, wkv0, w)
    h = h + mix
    hf = rmsnorm(h, w['norm_ffn'])
    h = h + (jax.nn.silu(hf @ w['w_gate']) * (hf @ w['w_up'])) @ w['w_down']
    gate = jax.nn.sigmoid(rmsnorm(h, w['norm_ple']) @ w['w_ple_gate'])
    h = h + gate * (p.astype(h.dtype) @ w['w_ple_proj'])
    return h, ssm_new, conv_new, wkv_new, shift_new


def setup_inputs(seed: int = 0) -> dict:
    key = jax.random.key(seed)
    ks = list(jax.random.split(key, 40))

    def nrm(shape, scale):
        return scale * jax.random.normal(ks.pop(), shape, jnp.float32)

    def unif(shape, lo, hi):
        return jax.random.uniform(ks.pop(), shape, jnp.float32, lo, hi)

    L = (DEPTH,)
    dt0 = jnp.exp(unif(L + (SSD_HEADS,), math.log(1e-3), math.log(1e-1)))
    return {
        'x_prompt': nrm((BATCH, SEQ, D_MODEL), 1.0),
        'x_sample': nrm((DEC_BATCH, DEC_SEQ, D_MODEL), 1.0),
        'state_ssm': nrm(L + (DEC_BATCH, SSD_HEADS, SSD_HEAD_DIM, SSD_STATE), 0.1),
        'state_conv': nrm(L + (DEC_BATCH, SSD_CONV - 1, SSD_CONV_DIM), 1.0),
        'state_wkv': nrm(L + (DEC_BATCH, RWKV_HEADS, RWKV_HEAD_DIM, RWKV_HEAD_DIM), 0.1),
        'state_shift': nrm(L + (DEC_BATCH, 1, RWKV_PROJ), 1.0),
        'p_prompt': nrm((DEPTH, BATCH, SEQ, PLE_DIM), 1.0),
        'p_sample': nrm((DEPTH, DEC_BATCH, DEC_SEQ, PLE_DIM), 1.0),
        'norm_mix': 1.0 + nrm(L + (D_MODEL,), 0.05),
        'w_in': nrm(L + (D_MODEL, IN_PROJ), D_MODEL ** -0.5),
        'conv_w': nrm(L + (SSD_CONV, SSD_CONV_DIM), SSD_CONV ** -0.5),
        'conv_b': nrm(L + (SSD_CONV_DIM,), 0.01),
        'dt_bias': dt0 + jnp.log(-jnp.expm1(-dt0)),
        'a_log': jnp.log(unif(L + (SSD_HEADS,), 1.0, 16.0)),
        'd_skip': 1.0 + nrm(L + (SSD_HEADS,), 0.1),
        'ssd_norm': 1.0 + nrm(L + (SSD_WIDTH,), 0.05),
        'shift_mu': unif(L + (RWKV_PROJ,), 0.0, 1.0),
        'w0': -2.5 + nrm(L + (RWKV_WIDTH,), 0.5),
        'w2': nrm(L + (DECAY_LORA, RWKV_WIDTH), 0.1 * DECAY_LORA ** -0.5),
        'a0': nrm(L + (RWKV_WIDTH,), 0.1),
        'a2': nrm(L + (AAA_LORA, RWKV_WIDTH), AAA_LORA ** -0.5),
        'g2': nrm(L + (GATE_LORA, RWKV_WIDTH), GATE_LORA ** -0.5),
        'k_k': 0.85 + nrm(L + (RWKV_WIDTH,), 0.05),
        'k_a': 1.0 + nrm(L + (RWKV_WIDTH,), 0.05),
        'r_k': nrm(L + (RWKV_WIDTH,), 0.1),
        'ln_x_w': 1.0 + nrm(L + (RWKV_WIDTH,), 0.05),
        'ln_x_b': nrm(L + (RWKV_WIDTH,), 0.01),
        'w_out': nrm(L + (MIX_WIDTH, D_MODEL), MIX_WIDTH ** -0.5),
        'norm_ffn': 1.0 + nrm(L + (D_MODEL,), 0.05),
        'w_gate': nrm(L + (D_MODEL, D_FF), D_MODEL ** -0.5),
        'w_up': nrm(L + (D_MODEL, D_FF), D_MODEL ** -0.5),
        'w_down': nrm(L + (D_FF, D_MODEL), D_FF ** -0.5),
        'norm_ple': 1.0 + nrm(L + (D_MODEL,), 0.05),
        'w_ple_gate': nrm(L + (D_MODEL, D_MODEL), D_MODEL ** -0.5),
        'w_ple_proj': nrm(L + (PLE_DIM, D_MODEL), PLE_DIM ** -0.5),
        'norm_final': 1.0 + nrm((D_MODEL,), 0.05),
    }


def reference(x_prompt, x_sample, state_ssm, state_conv, state_wkv, state_shift, p_prompt, p_sample,
              norm_mix, w_in, conv_w, conv_b, dt_bias, a_log, d_skip, ssd_norm, shift_mu, w0, w2, a0, a2,
              g2, k_k, k_a, r_k, ln_x_w, ln_x_b, w_out, norm_ffn, w_gate, w_up, w_down, norm_ple,
              w_ple_gate, w_ple_proj, norm_final):
    bp = x_prompt.shape[0]
    dtype = x_prompt.dtype
    hp, hs = x_prompt, x_sample
    ssm_p, conv_p, wkv_p, shift_p = [], [], [], []
    ssm_s, conv_s, wkv_s, shift_s = [], [], [], []
    for i in range(DEPTH):
        w = {
            'norm_mix': norm_mix[i], 'w_in': w_in[i], 'conv_w': conv_w[i], 'conv_b': conv_b[i],
            'dt_bias': dt_bias[i], 'a_log': a_log[i], 'd_skip': d_skip[i], 'ssd_norm': ssd_norm[i],
            'shift_mu': shift_mu[i], 'w0': w0[i], 'w2': w2[i], 'a0': a0[i], 'a2': a2[i], 'g2': g2[i],
            'k_k': k_k[i], 'k_a': k_a[i], 'r_k': r_k[i], 'ln_x_w': ln_x_w[i], 'ln_x_b': ln_x_b[i],
            'w_out': w_out[i], 'norm_ffn': norm_ffn[i], 'w_gate': w_gate[i], 'w_up': w_up[i],
            'w_down': w_down[i], 'norm_ple': norm_ple[i], 'w_ple_gate': w_ple_gate[i],
            'w_ple_proj': w_ple_proj[i],
        }
        hp, s1, c1, k1, t1 = layer(
            hp, p_prompt[i],
            jnp.zeros((bp, SSD_CONV - 1, SSD_CONV_DIM), dtype),
            jnp.zeros((bp, 1, RWKV_PROJ), dtype),
            jnp.zeros((bp, SSD_HEADS, SSD_HEAD_DIM, SSD_STATE), dtype),
            jnp.zeros((bp, RWKV_HEADS, RWKV_HEAD_DIM, RWKV_HEAD_DIM), dtype), w)
        ssm_p.append(s1); conv_p.append(c1); wkv_p.append(k1); shift_p.append(t1)
        hs, s2, c2, k2, t2 = layer(hs, p_sample[i], state_conv[i], state_shift[i], state_ssm[i], state_wkv[i], w)
        ssm_s.append(s2); conv_s.append(c2); wkv_s.append(k2); shift_s.append(t2)
    y_prompt = rmsnorm(hp, norm_final)
    y_sample = rmsnorm(hs, norm_final)
    return (y_prompt, y_sample,
            jnp.stack(ssm_p), jnp.stack(conv_p), jnp.stack(wkv_p), jnp.stack(shift_p),
            jnp.stack(ssm_s), jnp.stack(conv_s), jnp.stack(wkv_s), jnp.stack(shift_s))
```

```python
import functools

import jax
import jax.numpy as jnp
from jax import lax
from jax.experimental import pallas as pl
from jax.experimental.pallas import tpu as pltpu

F32 = jnp.float32
BF16 = jnp.bfloat16
HIGHEST = lax.Precision.HIGHEST

D_MODEL = 1024
SSD_WIDTH = 1024
SSD_HEADS = 16
SSD_HEAD_DIM = 64
SSD_GROUPS = 2
SSD_GROUP_WIDTH = SSD_WIDTH // SSD_GROUPS
SSD_STATE = 128
SSD_CONV = 4
SSD_CHUNK = 128
SSD_BC = SSD_GROUPS * SSD_STATE
SSD_CONV_DIM = SSD_WIDTH + 2 * SSD_BC
RWKV_WIDTH = 1024
RWKV_HEADS = 16
RWKV_HEAD_DIM = 64
DECAY_LORA = 64
AAA_LORA = 64
GATE_LORA = 128
RWKV_PROJ = 3 * RWKV_WIDTH + DECAY_LORA + AAA_LORA + GATE_LORA
D_FF = 2816
PLE_DIM = 256
NORM_EPS = 1e-6
GN_EPS = 64e-5

WKV_BATCH_BLOCK = 8
V7X_VMEM_LIMIT = 56 * 1024 * 1024
CONV_PAD = 8


def _rms(x, g):
    return x * lax.rsqrt(jnp.mean(x * x, axis=-1, keepdims=True) + NORM_EPS) * g


def _sigmoid(x):
    return 1.0 / (1.0 + jnp.exp(-x))


def _silu(x):
    return x * _sigmoid(x)


def _softplus(x):
    return jnp.maximum(x, 0.0) + jnp.log1p(jnp.exp(-jnp.abs(x)))


def _bdot(a, b):
    return jnp.dot(a.astype(BF16), b.astype(BF16), preferred_element_type=F32)


def _const_spec(shape):
    return pl.BlockSpec(shape, lambda *_: (0,) * len(shape), pipeline_mode=pl.Buffered(1))


def _head_expand(rows):
    h = lax.broadcasted_iota(jnp.int32, (rows, SSD_WIDTH), 0)
    c = lax.broadcasted_iota(jnp.int32, (rows, SSD_WIDTH), 1)
    return (c // SSD_HEAD_DIM == h).astype(F32)


def _proj_kernel(x_ref, g_ref, wz_ref, wx_ref, wr_ref, wdt_ref, z_ref, xbc_ref, rw_ref, dt_ref):
    u = _rms(x_ref[...], g_ref[...]).astype(BF16)
    z_ref[...] = jnp.dot(u, wz_ref[...], preferred_element_type=F32)
    xbc_ref[...] = jnp.dot(u, wx_ref[...], preferred_element_type=F32)
    rw_ref[...] = jnp.dot(u, wr_ref[...], preferred_element_type=F32)
    dt_ref[...] = jnp.dot(u, wdt_ref[...], preferred_element_type=F32)


def in_projection(x, g, wz, wx, wr, wdt, *, tm):
    n = x.shape[0]
    row = lambda w: pl.BlockSpec((tm, w), lambda i: (i, 0))
    return pl.pallas_call(
        _proj_kernel,
        grid=(n // tm,),
        in_specs=[row(D_MODEL), _const_spec((1, D_MODEL)), _const_spec(wz.shape),
                  _const_spec(wx.shape), _const_spec(wr.shape), _const_spec(wdt.shape)],
        out_specs=[row(SSD_WIDTH), row(SSD_CONV_DIM), row(RWKV_PROJ), row(SSD_HEADS)],
        out_shape=[jax.ShapeDtypeStruct((n, SSD_WIDTH), F32),
                   jax.ShapeDtypeStruct((n, SSD_CONV_DIM), F32),
                   jax.ShapeDtypeStruct((n, RWKV_PROJ), F32),
                   jax.ShapeDtypeStruct((n, SSD_HEADS), F32)],
        compiler_params=pltpu.CompilerParams(
            dimension_semantics=("parallel",), vmem_limit_bytes=V7X_VMEM_LIMIT),
        name="in_projection",
    )(x, g, wz, wx, wr, wdt)


def _ssd_kernel(z_ref, xbc_ref, dt_ref, dtt_ref, hist_ref, h0_ref, cw_ref, cb_ref,
                dtb_ref, dtbt_ref, alog_ref, alogt_ref, dsk_ref, nrm_ref,
                y_ref, hfin_ref, cnew_ref, xfull_scr, h_scr, *, q):
    c = pl.program_id(1)
    last = pl.num_programs(1) - 1
    gw = SSD_GROUP_WIDTH

    @pl.when(c == 0)
    def _():
        xfull_scr[CONV_PAD - 3:CONV_PAD, :] = hist_ref[0]
        for g in range(SSD_GROUPS):
            h_scr[g] = h0_ref[0, g * 8:(g + 1) * 8].reshape(gw, SSD_STATE).T

    @pl.when(c > 0)
    def _():
        xfull_scr[CONV_PAD - 3:CONV_PAD, :] = xfull_scr[CONV_PAD + q - 3:CONV_PAD + q, :]

    xfull_scr[CONV_PAD:CONV_PAD + q, :] = xbc_ref[0]

    conv = cb_ref[...]
    for j in range(SSD_CONV):
        lo = CONV_PAD - 3 + j
        conv = conv + xfull_scr[lo:lo + q, :] * cw_ref[j:j + 1, :]
    act = _silu(conv)
    xs = act[:, :SSD_WIDTH]

    dt = _softplus(dt_ref[0] + dtb_ref[...])
    dtt = _softplus(dtt_ref[0] + dtbt_ref[...])
    da = dt * -jnp.exp(alog_ref[...])
    dat = dtt * -jnp.exp(alogt_ref[...])
    row = lax.broadcasted_iota(jnp.int32, (q, q), 0)
    col = lax.broadcasted_iota(jnp.int32, (q, q), 1)
    causal = row >= col
    a_cum = jnp.dot(causal.astype(F32), da, precision=HIGHEST, preferred_element_type=F32)
    a_cumt = jnp.dot(dat, (row <= col).astype(F32), precision=HIGHEST, preferred_element_type=F32)
    a_end = a_cum[q - 1:q, :]

    expand = _head_expand(SSD_HEADS)
    ex = lambda t: jnp.dot(t, expand, precision=HIGHEST, preferred_element_type=F32)
    decay_end_x = ex(jnp.exp(a_end - a_cum) * dt)
    decay_in_x = ex(jnp.exp(a_cum))
    chunk_decay_x = ex(jnp.exp(a_end))
    xd = xs * decay_end_x

    ys = []
    for g in range(SSD_GROUPS):
        bm = act[:, SSD_WIDTH + g * SSD_STATE:SSD_WIDTH + (g + 1) * SSD_STATE]
        cm = act[:, SSD_WIDTH + SSD_BC + g * SSD_STATE:SSD_WIDTH + SSD_BC + (g + 1) * SSD_STATE]
        cb = lax.dot_general(cm.astype(BF16), bm.astype(BF16), (((1,), (1,)), ((), ())),
                             preferred_element_type=F32)
        y_heads = []
        for e in range(8):
            h = g * 8 + e
            seg = a_cum[:, h:h + 1] - a_cumt[h:h + 1, :]
            lmat = jnp.where(causal, jnp.exp(jnp.where(causal, seg, 0.0)), 0.0)
            w_qs = cb * lmat * dtt[h:h + 1, :]
            y_heads.append(_bdot(w_qs, xs[:, h * SSD_HEAD_DIM:(h + 1) * SSD_HEAD_DIM]))
        y_diag = jnp.concatenate(y_heads, axis=1)
        h_in = h_scr[g]
        y_off = _bdot(cm, h_in)
        sl = slice(g * gw, (g + 1) * gw)
        ys.append(y_diag + y_off * decay_in_x[:, sl])
        upd = lax.dot_general(bm.astype(BF16), xd[:, sl].astype(BF16), (((0,), (0,)), ((), ())),
                              preferred_element_type=F32)
        h_scr[g] = h_in * chunk_decay_x[:, sl] + upd

    y = jnp.concatenate(ys, axis=1) + dsk_ref[...] * xs
    yg = y * _silu(z_ref[0])
    outs = []
    for g in range(SSD_GROUPS):
        t = yg[:, g * gw:(g + 1) * gw]
        outs.append(t * lax.rsqrt(jnp.mean(t * t, axis=-1, keepdims=True) + NORM_EPS))
    y_ref[0] = jnp.concatenate(outs, axis=1) * nrm_ref[...]

    @pl.when(c == last)
    def _():
        cnew_ref[0] = xfull_scr[CONV_PAD + q - 3:CONV_PAD + q, :]
        for g in range(SSD_GROUPS):
            hfin_ref[0, g * 8:(g + 1) * 8] = h_scr[g].T.reshape(8, SSD_HEAD_DIM, SSD_STATE)


def ssd_mixer(z, xbc, dt, conv0, ssm0, w, *, q):
    b, l, _ = z.shape
    dtt = jnp.swapaxes(dt, 1, 2)
    seq = lambda wd: pl.BlockSpec((1, q, wd), lambda i, c: (i, c, 0))
    per_b3 = lambda s: pl.BlockSpec((1,) + s, lambda i, c: (i,) + (0,) * len(s))
    col = lambda t: t.reshape(-1, 1)
    rowv = lambda t: t.reshape(1, -1)
    consts = [w["conv_w"], rowv(w["conv_b"]), rowv(w["dt_bias"]), col(w["dt_bias"]),
              rowv(w["a_log"]), col(w["a_log"]),
              rowv(jnp.repeat(w["d_skip"], SSD_HEAD_DIM)), rowv(w["ssd_norm"])]
    return pl.pallas_call(
        functools.partial(_ssd_kernel, q=q),
        grid=(b, l // q),
        in_specs=[seq(SSD_WIDTH), seq(SSD_CONV_DIM), seq(SSD_HEADS),
                  pl.BlockSpec((1, SSD_HEADS, q), lambda i, c: (i, 0, c)),
                  per_b3((SSD_CONV - 1, SSD_CONV_DIM)),
                  per_b3((SSD_HEADS, SSD_HEAD_DIM, SSD_STATE))]
                 + [_const_spec(t.shape) for t in consts],
        out_specs=[seq(SSD_WIDTH), per_b3((SSD_HEADS, SSD_HEAD_DIM, SSD_STATE)),
                   per_b3((SSD_CONV - 1, SSD_CONV_DIM))],
        out_shape=[jax.ShapeDtypeStruct((b, l, SSD_WIDTH), F32),
                   jax.ShapeDtypeStruct((b, SSD_HEADS, SSD_HEAD_DIM, SSD_STATE), F32),
                   jax.ShapeDtypeStruct((b, SSD_CONV - 1, SSD_CONV_DIM), F32)],
        scratch_shapes=[pltpu.VMEM((CONV_PAD + q, SSD_CONV_DIM), F32),
                        pltpu.VMEM((SSD_GROUPS, SSD_STATE, SSD_GROUP_WIDTH), F32)],
        compiler_params=pltpu.CompilerParams(
            dimension_semantics=("parallel", "arbitrary"), vmem_limit_bytes=V7X_VMEM_LIMIT),
        name="ssd_mixer",
    )(z, xbc, dt, dtt, conv0, ssm0, *consts)


def _head_sums(t):
    pieces = []
    for h in range(RWKV_HEADS):
        s = jnp.sum(t[:, h * RWKV_HEAD_DIM:(h + 1) * RWKV_HEAD_DIM], axis=-1, keepdims=True)
        pieces.append(jnp.broadcast_to(s, (t.shape[0], RWKV_HEAD_DIM)))
    return jnp.concatenate(pieces, axis=1)


def _rwkv_prep_kernel(rw_ref, sh0_ref, mu_ref, w0_ref, w2_ref, a0_ref, a2_ref, g2_ref,
                      kk_ref, ka_ref, rk_ref,
                      r_out, w_out, k_out, v_out, kkn_out, kka_out, gate_out, bonus_out, shn_ref,
                      full_scr, *, tt):
    c = pl.program_id(1)
    wd = RWKV_WIDTH

    @pl.when(c == 0)
    def _():
        full_scr[CONV_PAD - 1:CONV_PAD, :] = sh0_ref[0]

    @pl.when(c > 0)
    def _():
        full_scr[CONV_PAD - 1:CONV_PAD, :] = full_scr[CONV_PAD + tt - 1:CONV_PAD + tt, :]

    rw = rw_ref[0]
    full_scr[CONV_PAD:CONV_PAD + tt, :] = rw
    prev = full_scr[CONV_PAD - 1:CONV_PAD - 1 + tt, :]
    u = rw + (prev - rw) * mu_ref[...]
    r = u[:, :wd]
    k = u[:, wd:2 * wd]
    v = u[:, 2 * wd:3 * wd]
    w_lo = u[:, 3 * wd:3 * wd + DECAY_LORA]
    a_lo = u[:, 3 * wd + DECAY_LORA:3 * wd + DECAY_LORA + AAA_LORA]
    g_lo = u[:, 3 * wd + DECAY_LORA + AAA_LORA:]

    w_log = -_softplus(-(w0_ref[...] + _bdot(jnp.tanh(w_lo), w2_ref[...]))) - 0.5
    decay = jnp.exp(-jnp.exp(w_log))
    a = _sigmoid(a0_ref[...] + _bdot(a_lo, a2_ref[...]))
    gate = _bdot(_sigmoid(g_lo), g2_ref[...])

    kk = k * kk_ref[...]
    kk = kk / jnp.maximum(jnp.sqrt(_head_sums(kk * kk)), 1e-12)
    kf = k * (1.0 + (a - 1.0) * ka_ref[...])
    bonus = _head_sums(r * kf * rk_ref[...]) * v

    r_out[0] = r
    w_out[0] = decay
    k_out[0] = kf
    v_out[0] = v
    kkn_out[0] = kk
    kka_out[0] = kk * a
    gate_out[0] = gate
    bonus_out[0] = bonus

    @pl.when(c == pl.num_programs(1) - 1)
    def _():
        shn_ref[0] = full_scr[CONV_PAD + tt - 1:CONV_PAD + tt, :]


def rwkv_prep(rw, shift0, w, *, tt):
    b, l, _ = rw.shape
    rowv = lambda t: t.reshape(1, -1)
    consts = [rowv(w["shift_mu"]), rowv(w["w0"]), w["w2"].astype(BF16), rowv(w["a0"]),
              w["a2"].astype(BF16), w["g2"].astype(BF16), rowv(w["k_k"]), rowv(w["k_a"]),
              rowv(w["r_k"])]
    seq = lambda wd: pl.BlockSpec((1, tt, wd), lambda i, c: (i, c, 0))
    one = pl.BlockSpec((1, 1, RWKV_PROJ), lambda i, c: (i, 0, 0))
    outs = pl.pallas_call(
        functools.partial(_rwkv_prep_kernel, tt=tt),
        grid=(b, l // tt),
        in_specs=[seq(RWKV_PROJ), one] + [_const_spec(t.shape) for t in consts],
        out_specs=[seq(RWKV_WIDTH)] * 8 + [one],
        out_shape=[jax.ShapeDtypeStruct((b, l, RWKV_WIDTH), F32)] * 8
                  + [jax.ShapeDtypeStruct((b, 1, RWKV_PROJ), F32)],
        scratch_shapes=[pltpu.VMEM((CONV_PAD + tt, RWKV_PROJ), F32)],
        compiler_params=pltpu.CompilerParams(
            dimension_semantics=("parallel", "arbitrary"), vmem_limit_bytes=V7X_VMEM_LIMIT),
        name="rwkv_prep",
    )(rw, shift0, *consts)
    return outs


def _wkv_kernel(r_ref, w_ref, k_ref, v_ref, kk_ref, kka_ref, s0_ref,
                o_ref, sfin_ref, s_scr, vt_scr, ot_scr, *, steps):
    c = pl.program_id(1)
    n = RWKV_HEAD_DIM
    lanes = WKV_BATCH_BLOCK * RWKV_HEADS

    @pl.when(c == 0)
    def _():
        s_scr[...] = s0_ref[...].reshape(lanes, n * n).T.reshape(n, n, lanes)

    def to_pairs(ref, t):
        return ref[:, t].reshape(lanes, n).T

    def step(t, carry):
        r_t = to_pairs(r_ref, t)
        w_t = to_pairs(w_ref, t)
        k_t = to_pairs(k_ref, t)
        kk_t = to_pairs(kk_ref, t)
        kka_t = to_pairs(kka_ref, t)
        vt_scr[...] = to_pairs(v_ref, t)

        def per_value(vi, carry2):
            s_v = s_scr[vi]
            skk = jnp.sum(s_v * kk_t, axis=0, keepdims=True)
            v_row = vt_scr[pl.ds(vi, 1), :]
            s_new = s_v * w_t - skk * kka_t + v_row * k_t
            s_scr[vi] = s_new
            ot_scr[pl.ds(vi, 1), :] = jnp.sum(s_new * r_t, axis=0, keepdims=True)
            return carry2

        lax.fori_loop(0, n, per_value, 0, unroll=4)
        o_ref[:, t] = ot_scr[...].T.reshape(WKV_BATCH_BLOCK, RWKV_HEADS, n)
        return carry

    lax.fori_loop(0, steps, step, 0)

    @pl.when(c == pl.num_programs(1) - 1)
    def _():
        sfin_ref[...] = s_scr[...].reshape(n * n, lanes).T.reshape(
            WKV_BATCH_BLOCK, RWKV_HEADS, n, n)


def wkv_scan(r, w, k, v, kk, kka, s0, *, steps):
    b, l, _ = r.shape
    h, n = RWKV_HEADS, RWKV_HEAD_DIM
    assert b % WKV_BATCH_BLOCK == 0 and l % steps == 0
    ops = [t.reshape(b, l, h, n) for t in (r, w, k, v, kk, kka)]
    seq_spec = pl.BlockSpec((WKV_BATCH_BLOCK, steps, h, n), lambda g, c: (g, c, 0, 0))
    st_spec = pl.BlockSpec((WKV_BATCH_BLOCK, h, n, n), lambda g, c: (g, 0, 0, 0))
    o, s_fin = pl.pallas_call(
        functools.partial(_wkv_kernel, steps=steps),
        grid=(b // WKV_BATCH_BLOCK, l // steps),
        in_specs=[seq_spec] * 6 + [st_spec],
        out_specs=[seq_spec, st_spec],
        out_shape=[jax.ShapeDtypeStruct((b, l, h, n), F32),
                   jax.ShapeDtypeStruct((b, h, n, n), F32)],
        scratch_shapes=[pltpu.VMEM((n, n, WKV_BATCH_BLOCK * h), F32),
                        pltpu.VMEM((n, WKV_BATCH_BLOCK * h), F32),
                        pltpu.VMEM((n, WKV_BATCH_BLOCK * h), F32)],
        compiler_params=pltpu.CompilerParams(
            dimension_semantics=("parallel", "arbitrary"), vmem_limit_bytes=V7X_VMEM_LIMIT),
        name="wkv_scan",
    )(*ops, s0)
    return o.reshape(b, l, h * n), s_fin


def _mix_out_kernel(x_ref, ys_ref, o_ref, gate_ref, bonus_ref, lnw_ref, lnb_ref,
                    woa_ref, wob_ref, h_ref):
    o = o_ref[...]
    inv_n = 1.0 / RWKV_HEAD_DIM
    mu = _head_sums(o) * inv_n
    d = o - mu
    var = _head_sums(d * d) * inv_n
    on = d * lax.rsqrt(var + GN_EPS) * lnw_ref[...] + lnb_ref[...]
    y_rwkv = (on + bonus_ref[...]) * gate_ref[...]
    mix = _bdot(ys_ref[...], woa_ref[...]) + _bdot(y_rwkv, wob_ref[...])
    h_ref[...] = x_ref[...] + mix


def mix_out(x, y_ssd, o, gate, bonus, lnw, lnb, woa, wob, *, tm):
    n = x.shape[0]
    row = pl.BlockSpec((tm, D_MODEL), lambda i: (i, 0))
    consts = [lnw, lnb, woa, wob]
    return pl.pallas_call(
        _mix_out_kernel,
        grid=(n // tm,),
        in_specs=[row] * 5 + [_const_spec(t.shape) for t in consts],
        out_specs=row,
        out_shape=jax.ShapeDtypeStruct((n, D_MODEL), F32),
        compiler_params=pltpu.CompilerParams(
            dimension_semantics=("parallel",), vmem_limit_bytes=V7X_VMEM_LIMIT),
        name="mix_out",
    )(x, y_ssd, o, gate, bonus, *consts)


def _ffn_kernel(h_ref, p_ref, nf_ref, wg_ref, wu_ref, wd_ref, np_ref, wpg_ref, wpp_ref,
                nl_ref, y_ref):
    h = h_ref[...]
    hf = _rms(h, nf_ref[...]).astype(BF16)
    gate = jnp.dot(hf, wg_ref[...], preferred_element_type=F32)
    up = jnp.dot(hf, wu_ref[...], preferred_element_type=F32)
    h = h + _bdot(_silu(gate) * up, wd_ref[...])
    pg = _sigmoid(_bdot(_rms(h, np_ref[...]), wpg_ref[...]))
    h = h + pg * _bdot(p_ref[...], wpp_ref[...])
    y_ref[...] = _rms(h, nl_ref[...])


def ffn_ple(h, p, nf, wg, wu, wd, npl, wpg, wpp, nl, *, tm):
    n = h.shape[0]
    consts = [nf, wg, wu, wd, npl, wpg, wpp, nl]
    return pl.pallas_call(
        _ffn_kernel,
        grid=(n // tm,),
        in_specs=[pl.BlockSpec((tm, D_MODEL), lambda i: (i, 0)),
                  pl.BlockSpec((tm, PLE_DIM), lambda i: (i, 0))]
                 + [_const_spec(t.shape) for t in consts],
        out_specs=pl.BlockSpec((tm, D_MODEL), lambda i: (i, 0)),
        out_shape=jax.ShapeDtypeStruct((n, D_MODEL), F32),
        compiler_params=pltpu.CompilerParams(
            dimension_semantics=("parallel",), vmem_limit_bytes=V7X_VMEM_LIMIT),
        name="ffn_ple",
    )(h, p, *consts)


def _prepare_weights(w):
    c0, c1, c2 = SSD_WIDTH, SSD_WIDTH + SSD_CONV_DIM, SSD_WIDTH + SSD_CONV_DIM + SSD_HEADS
    w_in = w["w_in"]
    rowv = lambda t: t.reshape(1, -1)
    return dict(
        w,
        wz=w_in[:, :c0].astype(BF16), wx=w_in[:, c0:c1].astype(BF16),
        wdt=w_in[:, c1:c2].astype(BF16), wr=w_in[:, c2:].astype(BF16),
        woa=w["w_out"][:SSD_WIDTH].astype(BF16), wob=w["w_out"][SSD_WIDTH:].astype(BF16),
        wg=w["w_gate"].astype(BF16), wu=w["w_up"].astype(BF16), wd=w["w_down"].astype(BF16),
        wpg=w["w_ple_gate"].astype(BF16), wpp=w["w_ple_proj"].astype(BF16),
        norm_mix_r=rowv(w["norm_mix"]), norm_ffn_r=rowv(w["norm_ffn"]),
        norm_ple_r=rowv(w["norm_ple"]), norm_final_r=rowv(w["norm_final"]),
        ln_x_w_r=rowv(w["ln_x_w"]), ln_x_b_r=rowv(w["ln_x_b"]),
    )


def layer_forward(x, p, conv0, shift0, ssm0, wkv0, w, *, tm, ssd_q, prep_tt, wkv_steps):
    b, l, _ = x.shape
    n = b * l
    x2 = x.reshape(n, D_MODEL)
    z, xbc, rw, dt = in_projection(x2, w["norm_mix_r"], w["wz"], w["wx"], w["wr"], w["wdt"], tm=tm)
    y_ssd, ssm_new, conv_new = ssd_mixer(
        z.reshape(b, l, -1), xbc.reshape(b, l, -1), dt.reshape(b, l, -1), conv0, ssm0, w, q=ssd_q)
    r, dec, kf, v, kk, kka, gate, bonus, shift_new = rwkv_prep(rw.reshape(b, l, -1), shift0, w, tt=prep_tt)
    o, wkv_new = wkv_scan(r, dec, kf, v, kk, kka, wkv0, steps=wkv_steps)
    flat = lambda t: t.reshape(n, -1)
    h = mix_out(x2, flat(y_ssd), flat(o), flat(gate), flat(bonus), w["ln_x_w_r"], w["ln_x_b_r"],
                w["woa"], w["wob"], tm=tm)
    y = ffn_ple(h, p.reshape(n, PLE_DIM), w["norm_ffn_r"], w["wg"], w["wu"], w["wd"],
                w["norm_ple_r"], w["wpg"], w["wpp"], w["norm_final_r"], tm=tm)
    return y.reshape(b, l, D_MODEL), ssm_new, conv_new, wkv_new, shift_new


def kernel(x_prompt, x_sample, state_ssm, state_conv, state_wkv, state_shift, p_prompt, p_sample, norm_mix, w_in, conv_w, conv_b, dt_bias, a_log, d_skip, ssd_norm, shift_mu, w0, w2, a0, a2, g2, k_k, k_a, r_k, ln_x_w, ln_x_b, w_out, norm_ffn, w_gate, w_up, w_down, norm_ple, w_ple_gate, w_ple_proj, norm_final):
    w = _prepare_weights(dict(
        norm_mix=norm_mix[0], w_in=w_in[0], conv_w=conv_w[0], conv_b=conv_b[0], dt_bias=dt_bias[0],
        a_log=a_log[0], d_skip=d_skip[0], ssd_norm=ssd_norm[0], shift_mu=shift_mu[0], w0=w0[0],
        w2=w2[0], a0=a0[0], a2=a2[0], g2=g2[0], k_k=k_k[0], k_a=k_a[0], r_k=r_k[0],
        ln_x_w=ln_x_w[0], ln_x_b=ln_x_b[0], w_out=w_out[0], norm_ffn=norm_ffn[0],
        w_gate=w_gate[0], w_up=w_up[0], w_down=w_down[0], norm_ple=norm_ple[0],
        w_ple_gate=w_ple_gate[0], w_ple_proj=w_ple_proj[0], norm_final=norm_final))
    bp = x_prompt.shape[0]
    zeros = lambda *s: jnp.zeros(s, F32)
    yp, s1, c1, k1, t1 = layer_forward(
        x_prompt, p_prompt[0], zeros(bp, SSD_CONV - 1, SSD_CONV_DIM), zeros(bp, 1, RWKV_PROJ),
        zeros(bp, SSD_HEADS, SSD_HEAD_DIM, SSD_STATE),
        zeros(bp, RWKV_HEADS, RWKV_HEAD_DIM, RWKV_HEAD_DIM), w,
        tm=256, ssd_q=min(SSD_CHUNK, x_prompt.shape[1]), prep_tt=min(128, x_prompt.shape[1]),
        wkv_steps=min(16, x_prompt.shape[1]))
    ys, s2, c2, k2, t2 = layer_forward(
        x_sample, p_sample[0], state_conv[0], state_shift[0], state_ssm[0], state_wkv[0], w,
        tm=256, ssd_q=x_sample.shape[1], prep_tt=x_sample.shape[1], wkv_steps=x_sample.shape[1])
    return (yp, ys, s1[None], c1[None], k1[None], t1[None], s2[None], c2[None], k2[None], t2[None])
```

```python
import functools

import jax
import jax.numpy as jnp
from jax import lax
from jax.experimental import pallas as pl
from jax.experimental.pallas import tpu as pltpu

F32 = jnp.float32
BF16 = jnp.bfloat16
HIGHEST = lax.Precision.HIGHEST

D_MODEL = 1024
SSD_WIDTH = 1024
SSD_HEADS = 16
SSD_HEAD_DIM = 64
SSD_GROUPS = 2
SSD_GROUP_WIDTH = SSD_WIDTH // SSD_GROUPS
SSD_STATE = 128
SSD_CONV = 4
SSD_CHUNK = 128
SSD_BC = SSD_GROUPS * SSD_STATE
SSD_CONV_DIM = SSD_WIDTH + 2 * SSD_BC
RWKV_WIDTH = 1024
RWKV_HEADS = 16
RWKV_HEAD_DIM = 64
DECAY_LORA = 64
AAA_LORA = 64
GATE_LORA = 128
RWKV_PROJ = 3 * RWKV_WIDTH + DECAY_LORA + AAA_LORA + GATE_LORA
D_FF = 2816
PLE_DIM = 256
NORM_EPS = 1e-6
GN_EPS = 64e-5

WKV_BATCH_BLOCK = 8
V7X_VMEM_LIMIT = 56 * 1024 * 1024
CONV_PAD = 8


def _rms(x, g):
    return x * lax.rsqrt(jnp.mean(x * x, axis=-1, keepdims=True) + NORM_EPS) * g


def _sigmoid(x):
    return 1.0 / (1.0 + jnp.exp(-x))


def _silu(x):
    return x * _sigmoid(x)


def _softplus(x):
    return jnp.maximum(x, 0.0) + jnp.log1p(jnp.exp(-jnp.abs(x)))


def _bdot(a, b):
    return jnp.dot(a.astype(BF16), b.astype(BF16), preferred_element_type=F32)


def _const_spec(shape):
    return pl.BlockSpec(shape, lambda *_: (0,) * len(shape), pipeline_mode=pl.Buffered(1))


def _head_expand(rows):
    h = lax.broadcasted_iota(jnp.int32, (rows, SSD_WIDTH), 0)
    c = lax.broadcasted_iota(jnp.int32, (rows, SSD_WIDTH), 1)
    return (c // SSD_HEAD_DIM == h).astype(F32)


def _proj_kernel(x_ref, g_ref, wz_ref, wx_ref, wr_ref, wdt_ref, z_ref, xbc_ref, rw_ref, dt_ref):
    u = _rms(x_ref[...], g_ref[...]).astype(BF16)
    z_ref[...] = jnp.dot(u, wz_ref[...], preferred_element_type=F32)
    xbc_ref[...] = jnp.dot(u, wx_ref[...], preferred_element_type=F32)
    rw_ref[...] = jnp.dot(u, wr_ref[...], preferred_element_type=F32)
    dt_ref[...] = jnp.dot(u, wdt_ref[...], preferred_element_type=F32)


def in_projection(x, g, wz, wx, wr, wdt, *, tm):
    n = x.shape[0]
    row = lambda w: pl.BlockSpec((tm, w), lambda i: (i, 0))
    return pl.pallas_call(
        _proj_kernel,
        grid=(n // tm,),
        in_specs=[row(D_MODEL), _const_spec((1, D_MODEL)), _const_spec(wz.shape),
                  _const_spec(wx.shape), _const_spec(wr.shape), _const_spec(wdt.shape)],
        out_specs=[row(SSD_WIDTH), row(SSD_CONV_DIM), row(RWKV_PROJ), row(SSD_HEADS)],
        out_shape=[jax.ShapeDtypeStruct((n, SSD_WIDTH), F32),
                   jax.ShapeDtypeStruct((n, SSD_CONV_DIM), F32),
                   jax.ShapeDtypeStruct((n, RWKV_PROJ), F32),
                   jax.ShapeDtypeStruct((n, SSD_HEADS), F32)],
        compiler_params=pltpu.CompilerParams(
            dimension_semantics=("parallel",), vmem_limit_bytes=V7X_VMEM_LIMIT),
        name="in_projection",
    )(x, g, wz, wx, wr, wdt)


def _ssd_kernel(z_ref, xbc_ref, dt_ref, dtt_ref, hist_ref, h0_ref, cw_ref, cb_ref,
                dtb_ref, dtbt_ref, alog_ref, alogt_ref, dsk_ref, nrm_ref,
                y_ref, hfin_ref, cnew_ref, xfull_scr, h_scr, *, q):
    c = pl.program_id(1)
    last = pl.num_programs(1) - 1
    gw = SSD_GROUP_WIDTH

    @pl.when(c == 0)
    def _():
        xfull_scr[CONV_PAD - 3:CONV_PAD, :] = hist_ref[0]
        for g in range(SSD_GROUPS):
            h_scr[g] = h0_ref[0, g * 8:(g + 1) * 8].reshape(gw, SSD_STATE).T

    @pl.when(c > 0)
    def _():
        xfull_scr[CONV_PAD - 3:CONV_PAD, :] = xfull_scr[CONV_PAD + q - 3:CONV_PAD + q, :]

    xfull_scr[CONV_PAD:CONV_PAD + q, :] = xbc_ref[0]

    conv = cb_ref[...]
    for j in range(SSD_CONV):
        lo = CONV_PAD - 3 + j
        conv = conv + xfull_scr[lo:lo + q, :] * cw_ref[j:j + 1, :]
    act = _silu(conv)
    xs = act[:, :SSD_WIDTH]

    dt = _softplus(dt_ref[0] + dtb_ref[...])
    dtt = _softplus(dtt_ref[0] + dtbt_ref[...])
    da = dt * -jnp.exp(alog_ref[...])
    dat = dtt * -jnp.exp(alogt_ref[...])
    row = lax.broadcasted_iota(jnp.int32, (q, q), 0)
    col = lax.broadcasted_iota(jnp.int32, (q, q), 1)
    causal = row >= col
    a_cum = jnp.dot(causal.astype(F32), da, precision=HIGHEST, preferred_element_type=F32)
    a_cumt = jnp.dot(dat, (row <= col).astype(F32), precision=HIGHEST, preferred_element_type=F32)
    a_end = a_cum[q - 1:q, :]

    expand = _head_expand(SSD_HEADS)
    ex = lambda t: jnp.dot(t, expand, precision=HIGHEST, preferred_element_type=F32)
    decay_end_x = ex(jnp.exp(a_end - a_cum) * dt)
    decay_in_x = ex(jnp.exp(a_cum))
    chunk_decay_x = ex(jnp.exp(a_end))
    xd = xs * decay_end_x

    ys = []
    for g in range(SSD_GROUPS):
        bm = act[:, SSD_WIDTH + g * SSD_STATE:SSD_WIDTH + (g + 1) * SSD_STATE]
        cm = act[:, SSD_WIDTH + SSD_BC + g * SSD_STATE:SSD_WIDTH + SSD_BC + (g + 1) * SSD_STATE]
        cb = lax.dot_general(cm.astype(BF16), bm.astype(BF16), (((1,), (1,)), ((), ())),
                             preferred_element_type=F32)
        y_heads = []
        for e in range(8):
            h = g * 8 + e
            seg = a_cum[:, h:h + 1] - a_cumt[h:h + 1, :]
            lmat = jnp.where(causal, jnp.exp(jnp.where(causal, seg, 0.0)), 0.0)
            w_qs = cb * lmat * dtt[h:h + 1, :]
            y_heads.append(_bdot(w_qs, xs[:, h * SSD_HEAD_DIM:(h + 1) * SSD_HEAD_DIM]))
        y_diag = jnp.concatenate(y_heads, axis=1)
        h_in = h_scr[g]
        y_off = _bdot(cm, h_in)
        sl = slice(g * gw, (g + 1) * gw)
        ys.append(y_diag + y_off * decay_in_x[:, sl])
        upd = lax.dot_general(bm.astype(BF16), xd[:, sl].astype(BF16), (((0,), (0,)), ((), ())),
                              preferred_element_type=F32)
        h_scr[g] = h_in * chunk_decay_x[:, sl] + upd

    y = jnp.concatenate(ys, axis=1) + dsk_ref[...] * xs
    yg = y * _silu(z_ref[0])
    outs = []
    for g in range(SSD_GROUPS):
        t = yg[:, g * gw:(g + 1) * gw]
        outs.append(t * lax.rsqrt(jnp.mean(t * t, axis=-1, keepdims=True) + NORM_EPS))
    y_ref[0] = jnp.concatenate(outs, axis=1) * nrm_ref[...]

    @pl.when(c == last)
    def _():
        cnew_ref[0] = xfull_scr[CONV_PAD + q - 3:CONV_PAD + q, :]
        for g in range(SSD_GROUPS):
            hfin_ref[0, g * 8:(g + 1) * 8] = h_scr[g].T.reshape(8, SSD_HEAD_DIM, SSD_STATE)


def ssd_mixer(z, xbc, dt, conv0, ssm0, w, *, q):
    b, l, _ = z.shape
    dtt = jnp.swapaxes(dt, 1, 2)
    seq = lambda wd: pl.BlockSpec((1, q, wd), lambda i, c: (i, c, 0))
    per_b3 = lambda s: pl.BlockSpec((1,) + s, lambda i, c: (i,) + (0,) * len(s))
    col = lambda t: t.reshape(-1, 1)
    rowv = lambda t: t.reshape(1, -1)
    consts = [w["conv_w"], rowv(w["conv_b"]), rowv(w["dt_bias"]), col(w["dt_bias"]),
              rowv(w["a_log"]), col(w["a_log"]),
              rowv(jnp.repeat(w["d_skip"], SSD_HEAD_DIM)), rowv(w["ssd_norm"])]
    return pl.pallas_call(
        functools.partial(_ssd_kernel, q=q),
        grid=(b, l // q),
        in_specs=[seq(SSD_WIDTH), seq(SSD_CONV_DIM), seq(SSD_HEADS),
                  pl.BlockSpec((1, SSD_HEADS, q), lambda i, c: (i, 0, c)),
                  per_b3((SSD_CONV - 1, SSD_CONV_DIM)),
                  per_b3((SSD_HEADS, SSD_HEAD_DIM, SSD_STATE))]
                 + [_const_spec(t.shape) for t in consts],
        out_specs=[seq(SSD_WIDTH), per_b3((SSD_HEADS, SSD_HEAD_DIM, SSD_STATE)),
                   per_b3((SSD_CONV - 1, SSD_CONV_DIM))],
        out_shape=[jax.ShapeDtypeStruct((b, l, SSD_WIDTH), F32),
                   jax.ShapeDtypeStruct((b, SSD_HEADS, SSD_HEAD_DIM, SSD_STATE), F32),
                   jax.ShapeDtypeStruct((b, SSD_CONV - 1, SSD_CONV_DIM), F32)],
        scratch_shapes=[pltpu.VMEM((CONV_PAD + q, SSD_CONV_DIM), F32),
                        pltpu.VMEM((SSD_GROUPS, SSD_STATE, SSD_GROUP_WIDTH), F32)],
        compiler_params=pltpu.CompilerParams(
            dimension_semantics=("parallel", "arbitrary"), vmem_limit_bytes=V7X_VMEM_LIMIT),
        name="ssd_mixer",
    )(z, xbc, dt, dtt, conv0, ssm0, *consts)


def _head_sums(t):
    pieces = []
    for h in range(RWKV_HEADS):
        s = jnp.sum(t[:, h * RWKV_HEAD_DIM:(h + 1) * RWKV_HEAD_DIM], axis=-1, keepdims=True)
        pieces.append(jnp.broadcast_to(s, (t.shape[0], RWKV_HEAD_DIM)))
    return jnp.concatenate(pieces, axis=1)


def _rwkv_prep_kernel(rw_ref, sh0_ref, mu_ref, w0_ref, w2_ref, a0_ref, a2_ref, g2_ref,
                      kk_ref, ka_ref, rk_ref,
                      r_out, w_out, k_out, v_out, kkn_out, kka_out, gate_out, bonus_out, shn_ref,
                      full_scr, *, tt, log_decay):
    c = pl.program_id(1)
    wd = RWKV_WIDTH

    @pl.when(c == 0)
    def _():
        full_scr[CONV_PAD - 1:CONV_PAD, :] = sh0_ref[0]

    @pl.when(c > 0)
    def _():
        full_scr[CONV_PAD - 1:CONV_PAD, :] = full_scr[CONV_PAD + tt - 1:CONV_PAD + tt, :]

    rw = rw_ref[0]
    full_scr[CONV_PAD:CONV_PAD + tt, :] = rw
    prev = full_scr[CONV_PAD - 1:CONV_PAD - 1 + tt, :]
    u = rw + (prev - rw) * mu_ref[...]
    r = u[:, :wd]
    k = u[:, wd:2 * wd]
    v = u[:, 2 * wd:3 * wd]
    w_lo = u[:, 3 * wd:3 * wd + DECAY_LORA]
    a_lo = u[:, 3 * wd + DECAY_LORA:3 * wd + DECAY_LORA + AAA_LORA]
    g_lo = u[:, 3 * wd + DECAY_LORA + AAA_LORA:]

    w_log = -_softplus(-(w0_ref[...] + _bdot(jnp.tanh(w_lo), w2_ref[...]))) - 0.5
    decay = -jnp.exp(w_log)
    if not log_decay:
        decay = jnp.exp(decay)
    a = _sigmoid(a0_ref[...] + _bdot(a_lo, a2_ref[...]))
    gate = _bdot(_sigmoid(g_lo), g2_ref[...])

    kk = k * kk_ref[...]
    kk = kk / jnp.maximum(jnp.sqrt(_head_sums(kk * kk)), 1e-12)
    kf = k * (1.0 + (a - 1.0) * ka_ref[...])
    bonus = _head_sums(r * kf * rk_ref[...]) * v

    r_out[0] = r
    w_out[0] = decay
    k_out[0] = kf
    v_out[0] = v
    kkn_out[0] = kk
    kka_out[0] = kk * a
    gate_out[0] = gate
    bonus_out[0] = bonus

    @pl.when(c == pl.num_programs(1) - 1)
    def _():
        shn_ref[0] = full_scr[CONV_PAD + tt - 1:CONV_PAD + tt, :]


def rwkv_prep(rw, shift0, w, *, tt, log_decay):
    b, l, _ = rw.shape
    rowv = lambda t: t.reshape(1, -1)
    consts = [rowv(w["shift_mu"]), rowv(w["w0"]), w["w2"].astype(BF16), rowv(w["a0"]),
              w["a2"].astype(BF16), w["g2"].astype(BF16), rowv(w["k_k"]), rowv(w["k_a"]),
              rowv(w["r_k"])]
    seq = lambda wd: pl.BlockSpec((1, tt, wd), lambda i, c: (i, c, 0))
    one = pl.BlockSpec((1, 1, RWKV_PROJ), lambda i, c: (i, 0, 0))
    outs = pl.pallas_call(
        functools.partial(_rwkv_prep_kernel, tt=tt, log_decay=log_decay),
        grid=(b, l // tt),
        in_specs=[seq(RWKV_PROJ), one] + [_const_spec(t.shape) for t in consts],
        out_specs=[seq(RWKV_WIDTH)] * 8 + [one],
        out_shape=[jax.ShapeDtypeStruct((b, l, RWKV_WIDTH), F32)] * 8
                  + [jax.ShapeDtypeStruct((b, 1, RWKV_PROJ), F32)],
        scratch_shapes=[pltpu.VMEM((CONV_PAD + tt, RWKV_PROJ), F32)],
        compiler_params=pltpu.CompilerParams(
            dimension_semantics=("parallel", "arbitrary"), vmem_limit_bytes=V7X_VMEM_LIMIT),
        name="rwkv_prep",
    )(rw, shift0, *consts)
    return outs


def _wkv_kernel(r_ref, w_ref, k_ref, v_ref, kk_ref, kka_ref, s0_ref,
                o_ref, sfin_ref, s_scr, vt_scr, ot_scr, *, steps):
    c = pl.program_id(1)
    n = RWKV_HEAD_DIM
    lanes = WKV_BATCH_BLOCK * RWKV_HEADS

    @pl.when(c == 0)
    def _():
        s_scr[...] = s0_ref[...].reshape(lanes, n * n).T.reshape(n, n, lanes)

    def to_pairs(ref, t):
        return ref[:, t].reshape(lanes, n).T

    def step(t, carry):
        r_t = to_pairs(r_ref, t)
        w_t = to_pairs(w_ref, t)
        k_t = to_pairs(k_ref, t)
        kk_t = to_pairs(kk_ref, t)
        kka_t = to_pairs(kka_ref, t)
        vt_scr[...] = to_pairs(v_ref, t)

        def per_value(vi, carry2):
            s_v = s_scr[vi]
            skk = jnp.sum(s_v * kk_t, axis=0, keepdims=True)
            v_row = vt_scr[pl.ds(vi, 1), :]
            s_new = s_v * w_t - skk * kka_t + v_row * k_t
            s_scr[vi] = s_new
            ot_scr[pl.ds(vi, 1), :] = jnp.sum(s_new * r_t, axis=0, keepdims=True)
            return carry2

        lax.fori_loop(0, n, per_value, 0, unroll=4)
        o_ref[:, t] = ot_scr[...].T.reshape(WKV_BATCH_BLOCK, RWKV_HEADS, n)
        return carry

    lax.fori_loop(0, steps, step, 0)

    @pl.when(c == pl.num_programs(1) - 1)
    def _():
        sfin_ref[...] = s_scr[...].reshape(n * n, lanes).T.reshape(
            WKV_BATCH_BLOCK, RWKV_HEADS, n, n)


def wkv_scan(r, w, k, v, kk, kka, s0, *, steps):
    b, l, _ = r.shape
    h, n = RWKV_HEADS, RWKV_HEAD_DIM
    assert b % WKV_BATCH_BLOCK == 0 and l % steps == 0
    ops = [t.reshape(b, l, h, n) for t in (r, w, k, v, kk, kka)]
    seq_spec = pl.BlockSpec((WKV_BATCH_BLOCK, steps, h, n), lambda g, c: (g, c, 0, 0))
    st_spec = pl.BlockSpec((WKV_BATCH_BLOCK, h, n, n), lambda g, c: (g, 0, 0, 0))
    o, s_fin = pl.pallas_call(
        functools.partial(_wkv_kernel, steps=steps),
        grid=(b // WKV_BATCH_BLOCK, l // steps),
        in_specs=[seq_spec] * 6 + [st_spec],
        out_specs=[seq_spec, st_spec],
        out_shape=[jax.ShapeDtypeStruct((b, l, h, n), F32),
                   jax.ShapeDtypeStruct((b, h, n, n), F32)],
        scratch_shapes=[pltpu.VMEM((n, n, WKV_BATCH_BLOCK * h), F32),
                        pltpu.VMEM((n, WKV_BATCH_BLOCK * h), F32),
                        pltpu.VMEM((n, WKV_BATCH_BLOCK * h), F32)],
        compiler_params=pltpu.CompilerParams(
            dimension_semantics=("parallel", "arbitrary"), vmem_limit_bytes=V7X_VMEM_LIMIT),
        name="wkv_scan",
    )(*ops, s0)
    return o.reshape(b, l, h * n), s_fin


WKV_CHUNK = 64
WKV_PAIRS = RWKV_HEADS // 2
WKV_ROW_STRIDE = WKV_CHUNK + 8
WKV_SOLVE_ROWS = 4


def _pair_masks():
    c = WKV_CHUNK
    row = lax.broadcasted_iota(jnp.int32, (2 * c, 2 * c), 0)
    col = lax.broadcasted_iota(jnp.int32, (2 * c, 2 * c), 1)
    t, i = row % c, col % c
    keep = i <= t - jnp.where(row < c, 1, 0)
    sign = jnp.where(row >= c, jnp.where(col < c, -1.0, 1.0), 1.0)
    block_diag = row // c == col // c
    return keep, sign, block_diag


def _wkv_prepare_kernel(r_ref, lw_ref, kf_ref, v_ref, kk_ref, kb_ref,
                        kkt_ref, rt_ref, kfh_ref, nbh_ref, vb_ref, lo_ref, rhs0_ref, gend_ref, abt_ref,
                        abs_scr, rhs_scr, top_scr):
    c = WKV_CHUNK
    keep, sign, _ = _pair_masks()
    lane = lax.broadcasted_iota(jnp.int32, (1, 2 * c), 1)
    head0 = lane < RWKV_HEAD_DIM
    tri = (lax.broadcasted_iota(jnp.int32, (c, c), 0) >= lax.broadcasted_iota(jnp.int32, (c, c), 1)).astype(F32)
    zeros = jnp.zeros((c, 2 * c), BF16)

    def per_batch(b, carry):
        lw = lw_ref[b]
        cum = jnp.dot(tri, lw, precision=HIGHEST, preferred_element_type=F32)
        cum_end = cum[c - 1:c, :]
        g_inv = jnp.exp(-cum)
        g_tail = jnp.exp(cum_end - cum)
        kb, kf = kb_ref[b], kf_ref[b]
        kkt_ref[b] = (kk_ref[b] * jnp.exp(cum - lw)).astype(BF16)
        rt_ref[b] = (r_ref[b] * jnp.exp(cum)).astype(BF16)
        kfh_ref[b] = (kf * g_tail).astype(BF16)
        nbh_ref[b] = (-kb * g_tail).astype(BF16)
        vb_ref[b] = v_ref[b].astype(BF16)
        gend_ref[b] = jnp.broadcast_to(jnp.exp(cum_end), (8, RWKV_WIDTH))
        rhs_scr[:c] = (kb * g_inv).astype(BF16)
        rhs_scr[c:] = (kf * g_inv).astype(BF16)

        for j in range(WKV_PAIRS):
            sl = slice(j * 2 * RWKV_HEAD_DIM, (j + 1) * 2 * RWKV_HEAD_DIM)
            lhs = jnp.concatenate([kkt_ref[b, :, sl], rt_ref[b, :, sl]], axis=0)
            rhs = rhs_scr[:, sl]
            for h2 in range(2):
                sel = head0 if h2 == 0 else lane >= RWKV_HEAD_DIM
                a = lax.dot_general(jnp.where(sel, lhs, jnp.zeros_like(lhs)), rhs,
                                    (((1,), (1,)), ((), ())), preferred_element_type=F32)
                a = jnp.where(keep, a, 0.0) * sign
                inst = h2 * (WKV_BATCH_BLOCK * WKV_PAIRS) + b * WKV_PAIRS + j
                abs_scr[pl.ds(inst * WKV_ROW_STRIDE, c), :] = a[:c, :c]
                top_scr[j, :, h2 * 2 * c:(h2 + 1) * 2 * c] = a[:c].astype(BF16)
                lo_ref[b, :, (2 * j + h2) * 2 * c:(2 * j + h2 + 1) * 2 * c] = a[c:].astype(BF16)

        for j in range(WKV_PAIRS):
            sl = slice(j * 2 * RWKV_HEAD_DIM, (j + 1) * 2 * RWKV_HEAD_DIM)
            v = vb_ref[b, :, sl]
            v0 = jnp.where(head0, v, jnp.zeros_like(v))
            w_akf = jnp.concatenate([zeros, v0, zeros, v - v0], axis=0)
            rhs0_ref[b, :, sl] = jnp.dot(top_scr[j], w_akf, preferred_element_type=F32)
        return carry

    lax.fori_loop(0, WKV_BATCH_BLOCK, per_batch, 0)

    def to_lanes(t, carry):
        abt_ref[0, t] = abs_scr[pl.ds(t, 2 * WKV_BATCH_BLOCK * WKV_PAIRS, stride=WKV_ROW_STRIDE), :].T
        return carry

    lax.fori_loop(0, c, to_lanes, 0, unroll=8)


def _wkv_apply_kernel(kkt_ref, rt_ref, kfh_ref, nbh_ref, vb_ref, lo_ref, rhs0_ref, gend_ref, abt_ref, s0_ref,
                      o_ref, sfin_ref, x_scr, rs_scr, rt_scr):
    c = WKV_CHUNK
    n = RWKV_HEAD_DIM
    ch = pl.program_id(1)
    _, _, block_diag = _pair_masks()
    lane = lax.broadcasted_iota(jnp.int32, (1, 2 * c), 1)
    head0 = lane < n
    eye2 = (lax.broadcasted_iota(jnp.int32, (n, 2 * n), 0)
            == lax.broadcasted_iota(jnp.int32, (n, 2 * n), 1) % n).astype(F32)
    n_pair_rows = WKV_BATCH_BLOCK * WKV_PAIRS

    @pl.when(ch == 0)
    def _():
        def init(b, carry):
            for j in range(WKV_PAIRS):
                sp = s0_ref[b, 2 * j:2 * j + 2].reshape(2 * n, n)
                dup = jnp.dot(sp, eye2, precision=HIGHEST, preferred_element_type=F32)
                x_scr[b, j] = jnp.where(block_diag, dup, 0.0)
            return carry
        lax.fori_loop(0, WKV_BATCH_BLOCK, init, 0)

    def before_solve(b, carry):
        for j in range(WKV_PAIRS):
            sl = slice(j * 2 * n, (j + 1) * 2 * n)
            lhs = jnp.concatenate([kkt_ref[b, :, sl], rt_ref[b, :, sl]], axis=0)
            kx = lax.dot_general(lhs, x_scr[b, j].astype(BF16), (((1,), (1,)), ((), ())),
                                 preferred_element_type=F32)
            row0 = (b * WKV_PAIRS + j) * WKV_ROW_STRIDE
            rs_scr[pl.ds(row0, c), :] = kx[:c] + rhs0_ref[b, :, sl]
            o_ref[b, :, sl] = kx[c:]
        return carry

    lax.fori_loop(0, WKV_BATCH_BLOCK, before_solve, 0)

    def to_lanes(t, carry):
        m = rs_scr[pl.ds(t, n_pair_rows, stride=WKV_ROW_STRIDE), :].T
        rt_scr[t] = jnp.concatenate([m[:n], m[n:]], axis=1)
        return carry

    lax.fori_loop(0, c, to_lanes, 0, unroll=8)

    def solve_rows(tb, carry):
        t0 = tb * WKV_SOLVE_ROWS
        rows = range(WKV_SOLVE_ROWS)

        def sub(ib, accs):
            for di in rows:
                i = ib * WKV_SOLVE_ROWS + di
                p = rt_scr[i]
                accs = tuple(accs[r] - abt_ref[0, t0 + r, pl.ds(i, 1), :] * p for r in rows)
            return accs

        accs = list(lax.fori_loop(0, tb, sub, tuple(rt_scr[t0 + r] for r in rows)))
        for r in rows:
            for r2 in range(r):
                accs[r] = accs[r] - abt_ref[0, t0 + r, pl.ds(t0 + r2, 1), :] * accs[r2]
            rt_scr[t0 + r] = accs[r]
        return carry

    lax.fori_loop(0, c // WKV_SOLVE_ROWS, solve_rows, 0)

    def from_lanes(t, carry):
        m = rt_scr[t].T
        rs_scr[pl.ds(t, n_pair_rows, stride=WKV_ROW_STRIDE), :] = jnp.concatenate(
            [m[:n_pair_rows], m[n_pair_rows:]], axis=1)
        return carry

    lax.fori_loop(0, c, from_lanes, 0, unroll=8)

    def after_solve(b, carry):
        for j in range(WKV_PAIRS):
            sl = slice(j * 2 * n, (j + 1) * 2 * n)
            row0 = (b * WKV_PAIRS + j) * WKV_ROW_STRIDE
            p = rs_scr[pl.ds(row0, c), :]
            v = vb_ref[b, :, sl].astype(F32)
            p0, v0 = jnp.where(head0, p, 0.0), jnp.where(head0, v, 0.0)
            w_o = jnp.concatenate([p0, v0, p - p0, v - v0], axis=0).astype(BF16)
            o_ref[b, :, sl] = o_ref[b, :, sl] + jnp.dot(
                lo_ref[b, :, j * 4 * c:(j + 1) * 4 * c], w_o, preferred_element_type=F32)
            vp = jnp.concatenate([v, p], axis=0).astype(BF16)
            kb = jnp.concatenate([kfh_ref[b, :, sl], nbh_ref[b, :, sl]], axis=0)
            upd = lax.dot_general(vp, kb, (((0,), (0,)), ((), ())), preferred_element_type=F32)
            x_scr[b, j] = jnp.where(block_diag, x_scr[b, j] * gend_ref[b, 0:1, sl] + upd, 0.0)
        return carry

    lax.fori_loop(0, WKV_BATCH_BLOCK, after_solve, 0)

    @pl.when(ch == pl.num_programs(1) - 1)
    def _():
        def fin(b, carry):
            for j in range(WKV_PAIRS):
                sp = lax.dot_general(x_scr[b, j], eye2, (((1,), (1,)), ((), ())),
                                     precision=HIGHEST, preferred_element_type=F32)
                sfin_ref[b, 2 * j:2 * j + 2] = sp.reshape(2, n, n)
            return carry
        lax.fori_loop(0, WKV_BATCH_BLOCK, fin, 0)


def wkv_chunked(r, lw, kf, v, kk, kb, s0):
    b, l, wd = r.shape
    c = WKV_CHUNK
    assert b % WKV_BATCH_BLOCK == 0 and l % c == 0
    gb, nc = b // WKV_BATCH_BLOCK, l // c
    lanes = 2 * WKV_BATCH_BLOCK * WKV_PAIRS
    seq = lambda w_: pl.BlockSpec((WKV_BATCH_BLOCK, c, w_), lambda g, i: (g, i, 0))
    gend_spec = pl.BlockSpec((WKV_BATCH_BLOCK, 8, wd), lambda g, i: (g * nc + i, 0, 0))
    abt_spec = pl.BlockSpec((1, c, c, lanes), lambda g, i: (g * nc + i, 0, 0, 0))
    sds = jax.ShapeDtypeStruct
    prepared = pl.pallas_call(
        _wkv_prepare_kernel,
        grid=(gb, nc),
        in_specs=[seq(wd)] * 6,
        out_specs=[seq(wd)] * 5 + [seq(2 * wd), seq(wd), gend_spec, abt_spec],
        out_shape=[sds((b, l, wd), BF16)] * 5 + [sds((b, l, 2 * wd), BF16), sds((b, l, wd), F32),
                                                 sds((gb * nc * WKV_BATCH_BLOCK, 8, wd), F32),
                                                 sds((gb * nc, c, c, lanes), F32)],
        scratch_shapes=[pltpu.VMEM((lanes * WKV_ROW_STRIDE, c), F32),
                        pltpu.VMEM((2 * c, wd), BF16),
                        pltpu.VMEM((WKV_PAIRS, c, 4 * c), BF16)],
        compiler_params=pltpu.CompilerParams(
            dimension_semantics=("parallel", "parallel"), vmem_limit_bytes=V7X_VMEM_LIMIT),
        name="wkv_prepare",
    )(r, lw, kf, v, kk, kb)
    st_spec = pl.BlockSpec((WKV_BATCH_BLOCK, RWKV_HEADS, RWKV_HEAD_DIM, RWKV_HEAD_DIM),
                           lambda g, i: (g, 0, 0, 0))
    o, s_fin = pl.pallas_call(
        _wkv_apply_kernel,
        grid=(gb, nc),
        in_specs=[seq(wd)] * 5 + [seq(2 * wd), seq(wd), gend_spec, abt_spec, st_spec],
        out_specs=[seq(wd), st_spec],
        out_shape=[sds((b, l, wd), F32), sds(s0.shape, F32)],
        scratch_shapes=[pltpu.VMEM((WKV_BATCH_BLOCK, WKV_PAIRS, 2 * RWKV_HEAD_DIM, 2 * RWKV_HEAD_DIM), F32),
                        pltpu.VMEM((WKV_BATCH_BLOCK * WKV_PAIRS * WKV_ROW_STRIDE, 2 * RWKV_HEAD_DIM), F32),
                        pltpu.VMEM((c, RWKV_HEAD_DIM, lanes), F32)],
        compiler_params=pltpu.CompilerParams(
            dimension_semantics=("parallel", "arbitrary"), vmem_limit_bytes=V7X_VMEM_LIMIT),
        name="wkv_apply",
    )(*prepared, s0)
    return o, s_fin


def _mix_out_kernel(x_ref, ys_ref, o_ref, gate_ref, bonus_ref, lnw_ref, lnb_ref,
                    woa_ref, wob_ref, h_ref):
    o = o_ref[...]
    inv_n = 1.0 / RWKV_HEAD_DIM
    mu = _head_sums(o) * inv_n
    d = o - mu
    var = _head_sums(d * d) * inv_n
    on = d * lax.rsqrt(var + GN_EPS) * lnw_ref[...] + lnb_ref[...]
    y_rwkv = (on + bonus_ref[...]) * gate_ref[...]
    mix = _bdot(ys_ref[...], woa_ref[...]) + _bdot(y_rwkv, wob_ref[...])
    h_ref[...] = x_ref[...] + mix


def mix_out(x, y_ssd, o, gate, bonus, lnw, lnb, woa, wob, *, tm):
    n = x.shape[0]
    row = pl.BlockSpec((tm, D_MODEL), lambda i: (i, 0))
    consts = [lnw, lnb, woa, wob]
    return pl.pallas_call(
        _mix_out_kernel,
        grid=(n // tm,),
        in_specs=[row] * 5 + [_const_spec(t.shape) for t in consts],
        out_specs=row,
        out_shape=jax.ShapeDtypeStruct((n, D_MODEL), F32),
        compiler_params=pltpu.CompilerParams(
            dimension_semantics=("parallel",), vmem_limit_bytes=V7X_VMEM_LIMIT),
        name="mix_out",
    )(x, y_ssd, o, gate, bonus, *consts)


def _ffn_kernel(h_ref, p_ref, nf_ref, wg_ref, wu_ref, wd_ref, np_ref, wpg_ref, wpp_ref,
                nl_ref, y_ref):
    h = h_ref[...]
    hf = _rms(h, nf_ref[...]).astype(BF16)
    gate = jnp.dot(hf, wg_ref[...], preferred_element_type=F32)
    up = jnp.dot(hf, wu_ref[...], preferred_element_type=F32)
    h = h + _bdot(_silu(gate) * up, wd_ref[...])
    pg = _sigmoid(_bdot(_rms(h, np_ref[...]), wpg_ref[...]))
    h = h + pg * _bdot(p_ref[...], wpp_ref[...])
    y_ref[...] = _rms(h, nl_ref[...])


def ffn_ple(h, p, nf, wg, wu, wd, npl, wpg, wpp, nl, *, tm):
    n = h.shape[0]
    consts = [nf, wg, wu, wd, npl, wpg, wpp, nl]
    return pl.pallas_call(
        _ffn_kernel,
        grid=(n // tm,),
        in_specs=[pl.BlockSpec((tm, D_MODEL), lambda i: (i, 0)),
                  pl.BlockSpec((tm, PLE_DIM), lambda i: (i, 0))]
                 + [_const_spec(t.shape) for t in consts],
        out_specs=pl.BlockSpec((tm, D_MODEL), lambda i: (i, 0)),
        out_shape=jax.ShapeDtypeStruct((n, D_MODEL), F32),
        compiler_params=pltpu.CompilerParams(
            dimension_semantics=("parallel",), vmem_limit_bytes=V7X_VMEM_LIMIT),
        name="ffn_ple",
    )(h, p, *consts)


def _prepare_weights(w):
    c0, c1, c2 = SSD_WIDTH, SSD_WIDTH + SSD_CONV_DIM, SSD_WIDTH + SSD_CONV_DIM + SSD_HEADS
    w_in = w["w_in"]
    rowv = lambda t: t.reshape(1, -1)
    return dict(
        w,
        wz=w_in[:, :c0].astype(BF16), wx=w_in[:, c0:c1].astype(BF16),
        wdt=w_in[:, c1:c2].astype(BF16), wr=w_in[:, c2:].astype(BF16),
        woa=w["w_out"][:SSD_WIDTH].astype(BF16), wob=w["w_out"][SSD_WIDTH:].astype(BF16),
        wg=w["w_gate"].astype(BF16), wu=w["w_up"].astype(BF16), wd=w["w_down"].astype(BF16),
        wpg=w["w_ple_gate"].astype(BF16), wpp=w["w_ple_proj"].astype(BF16),
        norm_mix_r=rowv(w["norm_mix"]), norm_ffn_r=rowv(w["norm_ffn"]),
        norm_ple_r=rowv(w["norm_ple"]), norm_final_r=rowv(w["norm_final"]),
        ln_x_w_r=rowv(w["ln_x_w"]), ln_x_b_r=rowv(w["ln_x_b"]),
    )


def layer_forward(x, p, conv0, shift0, ssm0, wkv0, w, *, tm, ssd_q, prep_tt, wkv_steps):
    b, l, _ = x.shape
    n = b * l
    x2 = x.reshape(n, D_MODEL)
    z, xbc, rw, dt = in_projection(x2, w["norm_mix_r"], w["wz"], w["wx"], w["wr"], w["wdt"], tm=tm)
    y_ssd, ssm_new, conv_new = ssd_mixer(
        z.reshape(b, l, -1), xbc.reshape(b, l, -1), dt.reshape(b, l, -1), conv0, ssm0, w, q=ssd_q)
    chunked = l % WKV_CHUNK == 0
    r, dec, kf, v, kk, kka, gate, bonus, shift_new = rwkv_prep(
        rw.reshape(b, l, -1), shift0, w, tt=prep_tt, log_decay=chunked)
    if chunked:
        o, wkv_new = wkv_chunked(r, dec, kf, v, kk, kka, wkv0)
    else:
        o, wkv_new = wkv_scan(r, dec, kf, v, kk, kka, wkv0, steps=wkv_steps)
    flat = lambda t: t.reshape(n, -1)
    h = mix_out(x2, flat(y_ssd), flat(o), flat(gate), flat(bonus), w["ln_x_w_r"], w["ln_x_b_r"],
                w["woa"], w["wob"], tm=tm)
    y = ffn_ple(h, p.reshape(n, PLE_DIM), w["norm_ffn_r"], w["wg"], w["wu"], w["wd"],
                w["norm_ple_r"], w["wpg"], w["wpp"], w["norm_final_r"], tm=tm)
    return y.reshape(b, l, D_MODEL), ssm_new, conv_new, wkv_new, shift_new


def kernel(x_prompt, x_sample, state_ssm, state_conv, state_wkv, state_shift, p_prompt, p_sample, norm_mix, w_in, conv_w, conv_b, dt_bias, a_log, d_skip, ssd_norm, shift_mu, w0, w2, a0, a2, g2, k_k, k_a, r_k, ln_x_w, ln_x_b, w_out, norm_ffn, w_gate, w_up, w_down, norm_ple, w_ple_gate, w_ple_proj, norm_final):
    w = _prepare_weights(dict(
        norm_mix=norm_mix[0], w_in=w_in[0], conv_w=conv_w[0], conv_b=conv_b[0], dt_bias=dt_bias[0],
        a_log=a_log[0], d_skip=d_skip[0], ssd_norm=ssd_norm[0], shift_mu=shift_mu[0], w0=w0[0],
        w2=w2[0], a0=a0[0], a2=a2[0], g2=g2[0], k_k=k_k[0], k_a=k_a[0], r_k=r_k[0],
        ln_x_w=ln_x_w[0], ln_x_b=ln_x_b[0], w_out=w_out[0], norm_ffn=norm_ffn[0],
        w_gate=w_gate[0], w_up=w_up[0], w_down=w_down[0], norm_ple=norm_ple[0],
        w_ple_gate=w_ple_gate[0], w_ple_proj=w_ple_proj[0], norm_final=norm_final))
    bp = x_prompt.shape[0]
    zeros = lambda *s: jnp.zeros(s, F32)
    yp, s1, c1, k1, t1 = layer_forward(
        x_prompt, p_prompt[0], zeros(bp, SSD_CONV - 1, SSD_CONV_DIM), zeros(bp, 1, RWKV_PROJ),
        zeros(bp, SSD_HEADS, SSD_HEAD_DIM, SSD_STATE),
        zeros(bp, RWKV_HEADS, RWKV_HEAD_DIM, RWKV_HEAD_DIM), w,
        tm=256, ssd_q=min(SSD_CHUNK, x_prompt.shape[1]), prep_tt=min(128, x_prompt.shape[1]),
        wkv_steps=min(16, x_prompt.shape[1]))
    ys, s2, c2, k2, t2 = layer_forward(
        x_sample, p_sample[0], state_conv[0], state_shift[0], state_ssm[0], state_wkv[0], w,
        tm=256, ssd_q=x_sample.shape[1], prep_tt=x_sample.shape[1], wkv_steps=x_sample.shape[1])
    return (yp, ys, s1[None], c1[None], k1[None], t1[None], s2[None], c2[None], k2[None], t2[None])
```

```python
import functools

import jax
import jax.numpy as jnp
from jax import lax
from jax.experimental import pallas as pl
from jax.experimental.pallas import tpu as pltpu

F32 = jnp.float32
BF16 = jnp.bfloat16
HIGHEST = lax.Precision.HIGHEST

D_MODEL = 1024
SSD_WIDTH = 1024
SSD_HEADS = 16
SSD_HEAD_DIM = 64
SSD_GROUPS = 2
SSD_GROUP_WIDTH = SSD_WIDTH // SSD_GROUPS
SSD_STATE = 128
SSD_CONV = 4
SSD_CHUNK = 128
SSD_BC = SSD_GROUPS * SSD_STATE
SSD_CONV_DIM = SSD_WIDTH + 2 * SSD_BC
RWKV_WIDTH = 1024
RWKV_HEADS = 16
RWKV_HEAD_DIM = 64
DECAY_LORA = 64
AAA_LORA = 64
GATE_LORA = 128
RWKV_PROJ = 3 * RWKV_WIDTH + DECAY_LORA + AAA_LORA + GATE_LORA
D_FF = 2816
PLE_DIM = 256
NORM_EPS = 1e-6
GN_EPS = 64e-5

WKV_BATCH_BLOCK = 8
V7X_VMEM_LIMIT = 56 * 1024 * 1024
CONV_PAD = 8


def _rms(x, g):
    return x * lax.rsqrt(jnp.mean(x * x, axis=-1, keepdims=True) + NORM_EPS) * g


def _sigmoid(x):
    return 1.0 / (1.0 + jnp.exp(-x))


def _silu(x):
    return x * _sigmoid(x)


def _softplus(x):
    return jnp.maximum(x, 0.0) + jnp.log(1.0 + jnp.exp(-jnp.abs(x)))


def _bdot(a, b):
    return jnp.dot(a.astype(BF16), b.astype(BF16), preferred_element_type=F32)


def _split3(t):
    hi = t.astype(BF16)
    r1 = t - hi.astype(F32)
    mid = r1.astype(BF16)
    lo = (r1 - mid.astype(F32)).astype(BF16)
    return hi, mid, lo


def _dot01(a, b, *, exact_side):
    if exact_side == "lhs":
        m = b.astype(BF16)
        return sum(jnp.dot(p, m, preferred_element_type=F32) for p in _split3(a))
    m = a.astype(BF16)
    return sum(jnp.dot(m, p, preferred_element_type=F32) for p in _split3(b))


def _const_spec(shape):
    return pl.BlockSpec(shape, lambda *_: (0,) * len(shape), pipeline_mode=pl.Buffered(1))


def _head_expand(rows):
    h = lax.broadcasted_iota(jnp.int32, (rows, SSD_WIDTH), 0)
    c = lax.broadcasted_iota(jnp.int32, (rows, SSD_WIDTH), 1)
    return (c // SSD_HEAD_DIM == h).astype(F32)


def _proj_kernel(x_ref, g_ref, wz_ref, wx_ref, wr_ref, wdt_ref, z_ref, xbc_ref, rw_ref, dt_ref):
    u = _rms(x_ref[...], g_ref[...]).astype(BF16)
    z_ref[...] = jnp.dot(u, wz_ref[...], preferred_element_type=F32)
    xbc_ref[...] = jnp.dot(u, wx_ref[...], preferred_element_type=F32)
    rw_ref[...] = jnp.dot(u, wr_ref[...], preferred_element_type=F32)
    dt_ref[...] = jnp.dot(u, wdt_ref[...], preferred_element_type=F32)


def in_projection(x, g, wz, wx, wr, wdt, *, tm):
    n = x.shape[0]
    row = lambda w: pl.BlockSpec((tm, w), lambda i: (i, 0))
    return pl.pallas_call(
        _proj_kernel,
        grid=(n // tm,),
        in_specs=[row(D_MODEL), _const_spec((1, D_MODEL)), _const_spec(wz.shape),
                  _const_spec(wx.shape), _const_spec(wr.shape), _const_spec(wdt.shape)],
        out_specs=[row(SSD_WIDTH), row(SSD_CONV_DIM), row(RWKV_PROJ), row(SSD_HEADS)],
        out_shape=[jax.ShapeDtypeStruct((n, SSD_WIDTH), F32),
                   jax.ShapeDtypeStruct((n, SSD_CONV_DIM), F32),
                   jax.ShapeDtypeStruct((n, RWKV_PROJ), F32),
                   jax.ShapeDtypeStruct((n, SSD_HEADS), F32)],
        compiler_params=pltpu.CompilerParams(
            dimension_semantics=("parallel",), vmem_limit_bytes=V7X_VMEM_LIMIT),
        name="in_projection",
    )(x, g, wz, wx, wr, wdt)


def _ssd_kernel(z_ref, xbc_ref, dt_ref, dtt_ref, hist_ref, h0_ref, cw_ref, cb_ref,
                dtb_ref, dtbt_ref, alog_ref, alogt_ref, dsk_ref, nrm_ref,
                y_ref, hfin_ref, cnew_ref, xfull_scr, h_scr, *, q):
    c = pl.program_id(1)
    last = pl.num_programs(1) - 1
    gw = SSD_GROUP_WIDTH

    @pl.when(c == 0)
    def _():
        xfull_scr[CONV_PAD - 3:CONV_PAD, :] = hist_ref[0]
        for g in range(SSD_GROUPS):
            h_scr[g] = h0_ref[0, g * 8:(g + 1) * 8].reshape(gw, SSD_STATE).T

    @pl.when(c > 0)
    def _():
        xfull_scr[CONV_PAD - 3:CONV_PAD, :] = xfull_scr[CONV_PAD + q - 3:CONV_PAD + q, :]

    xfull_scr[CONV_PAD:CONV_PAD + q, :] = xbc_ref[0]

    conv = cb_ref[...]
    for j in range(SSD_CONV):
        lo = CONV_PAD - 3 + j
        conv = conv + xfull_scr[lo:lo + q, :] * cw_ref[j:j + 1, :]
    act = _silu(conv)
    xs = act[:, :SSD_WIDTH]

    dt = _softplus(dt_ref[0] + dtb_ref[...])
    dtt = _softplus(dtt_ref[0] + dtbt_ref[...])
    da = dt * -jnp.exp(alog_ref[...])
    dat = dtt * -jnp.exp(alogt_ref[...])
    row = lax.broadcasted_iota(jnp.int32, (q, q), 0)
    col = lax.broadcasted_iota(jnp.int32, (q, q), 1)
    causal = row >= col
    a_cum = _dot01(causal.astype(F32), da, exact_side="rhs")
    a_cumt = _dot01(dat, (row <= col).astype(F32), exact_side="lhs")

    expand = _head_expand(SSD_HEADS)
    a_cum_x = _dot01(a_cum, expand, exact_side="lhs")
    dt_x = _dot01(dt, expand, exact_side="lhs")
    a_end_x = a_cum_x[q - 1:q, :]
    decay_in_x = jnp.exp(a_cum_x)
    chunk_decay_x = jnp.exp(a_end_x)
    xd = xs * (jnp.exp(a_end_x - a_cum_x) * dt_x)

    ys = []
    for g in range(SSD_GROUPS):
        bm = act[:, SSD_WIDTH + g * SSD_STATE:SSD_WIDTH + (g + 1) * SSD_STATE]
        cm = act[:, SSD_WIDTH + SSD_BC + g * SSD_STATE:SSD_WIDTH + SSD_BC + (g + 1) * SSD_STATE]
        cb = lax.dot_general(cm.astype(BF16), bm.astype(BF16), (((1,), (1,)), ((), ())),
                             preferred_element_type=F32)
        y_heads = []
        for e in range(8):
            h = g * 8 + e
            seg = a_cum[:, h:h + 1] - a_cumt[h:h + 1, :]
            lmat = jnp.where(causal, jnp.exp(jnp.where(causal, seg, 0.0)), 0.0)
            w_qs = cb * lmat * dtt[h:h + 1, :]
            y_heads.append(_bdot(w_qs, xs[:, h * SSD_HEAD_DIM:(h + 1) * SSD_HEAD_DIM]))
        y_diag = jnp.concatenate(y_heads, axis=1)
        h_in = h_scr[g]
        y_off = _bdot(cm, h_in)
        sl = slice(g * gw, (g + 1) * gw)
        ys.append(y_diag + y_off * decay_in_x[:, sl])
        upd = lax.dot_general(bm.astype(BF16), xd[:, sl].astype(BF16), (((0,), (0,)), ((), ())),
                              preferred_element_type=F32)
        h_scr[g] = h_in * chunk_decay_x[:, sl] + upd

    y = jnp.concatenate(ys, axis=1) + dsk_ref[...] * xs
    yg = y * _silu(z_ref[0])
    outs = []
    for g in range(SSD_GROUPS):
        t = yg[:, g * gw:(g + 1) * gw]
        outs.append(t * lax.rsqrt(jnp.mean(t * t, axis=-1, keepdims=True) + NORM_EPS))
    y_ref[0] = jnp.concatenate(outs, axis=1) * nrm_ref[...]

    @pl.when(c == last)
    def _():
        cnew_ref[0] = xfull_scr[CONV_PAD + q - 3:CONV_PAD + q, :]
        for g in range(SSD_GROUPS):
            hfin_ref[0, g * 8:(g + 1) * 8] = h_scr[g].T.reshape(8, SSD_HEAD_DIM, SSD_STATE)


def ssd_mixer(z, xbc, dt, conv0, ssm0, w, *, q):
    b, l, _ = z.shape
    dtt = jnp.swapaxes(dt, 1, 2)
    seq = lambda wd: pl.BlockSpec((1, q, wd), lambda i, c: (i, c, 0))
    per_b3 = lambda s: pl.BlockSpec((1,) + s, lambda i, c: (i,) + (0,) * len(s))
    col = lambda t: t.reshape(-1, 1)
    rowv = lambda t: t.reshape(1, -1)
    consts = [w["conv_w"], rowv(w["conv_b"]), rowv(w["dt_bias"]), col(w["dt_bias"]),
              rowv(w["a_log"]), col(w["a_log"]),
              rowv(jnp.repeat(w["d_skip"], SSD_HEAD_DIM)), rowv(w["ssd_norm"])]
    return pl.pallas_call(
        functools.partial(_ssd_kernel, q=q),
        grid=(b, l // q),
        in_specs=[seq(SSD_WIDTH), seq(SSD_CONV_DIM), seq(SSD_HEADS),
                  pl.BlockSpec((1, SSD_HEADS, q), lambda i, c: (i, 0, c)),
                  per_b3((SSD_CONV - 1, SSD_CONV_DIM)),
                  per_b3((SSD_HEADS, SSD_HEAD_DIM, SSD_STATE))]
                 + [_const_spec(t.shape) for t in consts],
        out_specs=[seq(SSD_WIDTH), per_b3((SSD_HEADS, SSD_HEAD_DIM, SSD_STATE)),
                   per_b3((SSD_CONV - 1, SSD_CONV_DIM))],
        out_shape=[jax.ShapeDtypeStruct((b, l, SSD_WIDTH), F32),
                   jax.ShapeDtypeStruct((b, SSD_HEADS, SSD_HEAD_DIM, SSD_STATE), F32),
                   jax.ShapeDtypeStruct((b, SSD_CONV - 1, SSD_CONV_DIM), F32)],
        scratch_shapes=[pltpu.VMEM((CONV_PAD + q, SSD_CONV_DIM), F32),
                        pltpu.VMEM((SSD_GROUPS, SSD_STATE, SSD_GROUP_WIDTH), F32)],
        compiler_params=pltpu.CompilerParams(
            dimension_semantics=("parallel", "arbitrary"), vmem_limit_bytes=V7X_VMEM_LIMIT),
        name="ssd_mixer",
    )(z, xbc, dt, dtt, conv0, ssm0, *consts)


def _head_sums(t):
    pair = 2 * RWKV_HEAD_DIM
    first = lax.broadcasted_iota(jnp.int32, (1, pair), 1) < RWKV_HEAD_DIM
    pieces = []
    for j in range(RWKV_HEADS // 2):
        x = t[:, j * pair:(j + 1) * pair]
        x0 = jnp.where(first, x, 0.0)
        s0 = jnp.sum(x0, axis=-1, keepdims=True)
        s1 = jnp.sum(x - x0, axis=-1, keepdims=True)
        pieces.append(jnp.where(first, s0, s1))
    return jnp.concatenate(pieces, axis=1)


def _rwkv_prep_kernel(rw_ref, sh0_ref, mu_ref, w0_ref, w2_ref, a0_ref, a2_ref, g2_ref,
                      kk_ref, ka_ref, rk_ref,
                      r_out, w_out, k_out, v_out, kkn_out, kka_out, gate_out, bonus_out, shn_ref,
                      full_scr, *, tt, seqs, log_decay):
    c = pl.program_id(1)
    wd = RWKV_WIDTH
    l = tt // seqs

    if seqs == 1:
        @pl.when(c == 0)
        def _():
            full_scr[0, CONV_PAD - 1:CONV_PAD, :] = sh0_ref[0]

        @pl.when(c > 0)
        def _():
            full_scr[0, CONV_PAD - 1:CONV_PAD, :] = full_scr[0, CONV_PAD + tt - 1:CONV_PAD + tt, :]

        rw = rw_ref[0]
        full_scr[0, CONV_PAD:CONV_PAD + tt, :] = rw
        prev = full_scr[0, CONV_PAD - 1:CONV_PAD - 1 + tt, :]
    else:
        full_scr[:, CONV_PAD - 1:CONV_PAD, :] = sh0_ref[...]
        full_scr[:, CONV_PAD:CONV_PAD + l, :] = rw_ref[...]
        rw = rw_ref[...].reshape(tt, RWKV_PROJ)
        prev = full_scr[:, CONV_PAD - 1:CONV_PAD - 1 + l, :].reshape(tt, RWKV_PROJ)
    u = rw + (prev - rw) * mu_ref[...]
    r = u[:, :wd]
    k = u[:, wd:2 * wd]
    v = u[:, 2 * wd:3 * wd]
    w_lo = u[:, 3 * wd:3 * wd + DECAY_LORA]
    a_lo = u[:, 3 * wd + DECAY_LORA:3 * wd + DECAY_LORA + AAA_LORA]
    g_lo = u[:, 3 * wd + DECAY_LORA + AAA_LORA:]

    w_log = -_softplus(-(w0_ref[...] + _bdot(jnp.tanh(w_lo), w2_ref[...]))) - 0.5
    decay = -jnp.exp(w_log)
    if not log_decay:
        decay = jnp.exp(decay)
    a = _sigmoid(a0_ref[...] + _bdot(a_lo, a2_ref[...]))
    gate = _bdot(_sigmoid(g_lo), g2_ref[...])

    kk = k * kk_ref[...]
    kk = kk / jnp.maximum(jnp.sqrt(_head_sums(kk * kk)), 1e-12)
    kf = k * (1.0 + (a - 1.0) * ka_ref[...])
    bonus = _head_sums(r * kf * rk_ref[...]) * v

    blk = r_out.shape
    for ref, val in ((r_out, r), (w_out, decay), (k_out, kf), (v_out, v), (kkn_out, kk),
                     (kka_out, kk * a), (gate_out, gate), (bonus_out, bonus)):
        ref[...] = val.reshape(blk)

    if seqs == 1:
        @pl.when(c == pl.num_programs(1) - 1)
        def _():
            shn_ref[0] = full_scr[0, CONV_PAD + tt - 1:CONV_PAD + tt, :]
    else:
        shn_ref[...] = rw_ref[:, l - 1:l, :]


def rwkv_prep(rw, shift0, w, *, tt, log_decay):
    b, l, _ = rw.shape
    seqs = max(1, tt // l)
    rowv = lambda t: t.reshape(1, -1)
    consts = [rowv(w["shift_mu"]), rowv(w["w0"]), w["w2"].astype(BF16), rowv(w["a0"]),
              w["a2"].astype(BF16), w["g2"].astype(BF16), rowv(w["k_k"]), rowv(w["k_a"]),
              rowv(w["r_k"])]
    rows = tt // seqs
    grid = (b // seqs, l // rows)
    seq = lambda wd: pl.BlockSpec((seqs, rows, wd), lambda i, c: (i, c, 0))
    one = pl.BlockSpec((seqs, 1, RWKV_PROJ), lambda i, c: (i, 0, 0))
    outs = pl.pallas_call(
        functools.partial(_rwkv_prep_kernel, tt=tt, seqs=seqs, log_decay=log_decay),
        grid=grid,
        in_specs=[seq(RWKV_PROJ), one] + [_const_spec(t.shape) for t in consts],
        out_specs=[seq(RWKV_WIDTH)] * 8 + [one],
        out_shape=[jax.ShapeDtypeStruct((b, l, RWKV_WIDTH), F32)] * 8
                  + [jax.ShapeDtypeStruct((b, 1, RWKV_PROJ), F32)],
        scratch_shapes=[pltpu.VMEM((seqs, CONV_PAD + rows, RWKV_PROJ), F32)],
        compiler_params=pltpu.CompilerParams(
            dimension_semantics=("parallel", "arbitrary"), vmem_limit_bytes=V7X_VMEM_LIMIT),
        name="rwkv_prep",
    )(rw, shift0, *consts)
    return outs


def _wkv_kernel(r_ref, w_ref, k_ref, v_ref, kk_ref, kka_ref, s0_ref,
                o_ref, sfin_ref, s_scr, vt_scr, ot_scr, *, steps):
    c = pl.program_id(1)
    n = RWKV_HEAD_DIM
    lanes = WKV_BATCH_BLOCK * RWKV_HEADS

    @pl.when(c == 0)
    def _():
        s_scr[...] = s0_ref[...].reshape(lanes, n * n).T.reshape(n, n, lanes)

    def to_pairs(ref, t):
        return ref[:, t].reshape(lanes, n).T

    def step(t, carry):
        r_t = to_pairs(r_ref, t)
        w_t = to_pairs(w_ref, t)
        k_t = to_pairs(k_ref, t)
        kk_t = to_pairs(kk_ref, t)
        kka_t = to_pairs(kka_ref, t)
        vt_scr[...] = to_pairs(v_ref, t)

        def per_value(vi, carry2):
            s_v = s_scr[vi]
            skk = jnp.sum(s_v * kk_t, axis=0, keepdims=True)
            v_row = vt_scr[pl.ds(vi, 1), :]
            s_new = s_v * w_t - skk * kka_t + v_row * k_t
            s_scr[vi] = s_new
            ot_scr[pl.ds(vi, 1), :] = jnp.sum(s_new * r_t, axis=0, keepdims=True)
            return carry2

        lax.fori_loop(0, n, per_value, 0, unroll=4)
        o_ref[:, t] = ot_scr[...].T.reshape(WKV_BATCH_BLOCK, RWKV_HEADS, n)
        return carry

    lax.fori_loop(0, steps, step, 0)

    @pl.when(c == pl.num_programs(1) - 1)
    def _():
        sfin_ref[...] = s_scr[...].reshape(n * n, lanes).T.reshape(
            WKV_BATCH_BLOCK, RWKV_HEADS, n, n)


def wkv_scan(r, w, k, v, kk, kka, s0, *, steps):
    b, l, _ = r.shape
    h, n = RWKV_HEADS, RWKV_HEAD_DIM
    assert b % WKV_BATCH_BLOCK == 0 and l % steps == 0
    ops = [t.reshape(b, l, h, n) for t in (r, w, k, v, kk, kka)]
    seq_spec = pl.BlockSpec((WKV_BATCH_BLOCK, steps, h, n), lambda g, c: (g, c, 0, 0))
    st_spec = pl.BlockSpec((WKV_BATCH_BLOCK, h, n, n), lambda g, c: (g, 0, 0, 0))
    o, s_fin = pl.pallas_call(
        functools.partial(_wkv_kernel, steps=steps),
        grid=(b // WKV_BATCH_BLOCK, l // steps),
        in_specs=[seq_spec] * 6 + [st_spec],
        out_specs=[seq_spec, st_spec],
        out_shape=[jax.ShapeDtypeStruct((b, l, h, n), F32),
                   jax.ShapeDtypeStruct((b, h, n, n), F32)],
        scratch_shapes=[pltpu.VMEM((n, n, WKV_BATCH_BLOCK * h), F32),
                        pltpu.VMEM((n, WKV_BATCH_BLOCK * h), F32),
                        pltpu.VMEM((n, WKV_BATCH_BLOCK * h), F32)],
        compiler_params=pltpu.CompilerParams(
            dimension_semantics=("parallel", "arbitrary"), vmem_limit_bytes=V7X_VMEM_LIMIT),
        name="wkv_scan",
    )(*ops, s0)
    return o.reshape(b, l, h * n), s_fin


WKV_CHUNK = 64
WKV_PAIRS = RWKV_HEADS // 2
WKV_ROW_STRIDE = WKV_CHUNK + 8
WKV_SOLVE_ROWS = 4


def _pair_masks():
    c = WKV_CHUNK
    row = lax.broadcasted_iota(jnp.int32, (2 * c, 2 * c), 0)
    col = lax.broadcasted_iota(jnp.int32, (2 * c, 2 * c), 1)
    t, i = row % c, col % c
    keep = i <= t - jnp.where(row < c, 1, 0)
    sign = jnp.where(row >= c, jnp.where(col < c, -1.0, 1.0), 1.0)
    block_diag = row // c == col // c
    return keep, sign, block_diag


def _wkv_prepare_kernel(r_ref, lw_ref, kf_ref, v_ref, kk_ref, kb_ref,
                        kkt_ref, rt_ref, kfh_ref, nbh_ref, vb_ref, lo_ref, rhs0_ref, gend_ref, abt_ref,
                        abs_scr, rhs_scr, top_scr):
    c = WKV_CHUNK
    keep, sign, _ = _pair_masks()
    lane = lax.broadcasted_iota(jnp.int32, (1, 2 * c), 1)
    head0 = lane < RWKV_HEAD_DIM
    tri = (lax.broadcasted_iota(jnp.int32, (c, c), 0) >= lax.broadcasted_iota(jnp.int32, (c, c), 1)).astype(F32)
    zeros = jnp.zeros((c, 2 * c), BF16)

    def per_batch(b, carry):
        lw = lw_ref[b]
        cum = _dot01(tri, lw, exact_side="rhs")
        cum_end = cum[c - 1:c, :]
        g_inv = jnp.exp(-cum)
        g_tail = jnp.exp(cum_end - cum)
        kb, kf = kb_ref[b], kf_ref[b]
        kkt_ref[b] = (kk_ref[b] * jnp.exp(cum - lw)).astype(BF16)
        rt_ref[b] = (r_ref[b] * jnp.exp(cum)).astype(BF16)
        kfh_ref[b] = (kf * g_tail).astype(BF16)
        nbh_ref[b] = (-kb * g_tail).astype(BF16)
        vb_ref[b] = v_ref[b].astype(BF16)
        gend_ref[b] = jnp.broadcast_to(jnp.exp(cum_end), (8, RWKV_WIDTH))
        rhs_scr[:c] = (kb * g_inv).astype(BF16)
        rhs_scr[c:] = (kf * g_inv).astype(BF16)

        for j in range(WKV_PAIRS):
            sl = slice(j * 2 * RWKV_HEAD_DIM, (j + 1) * 2 * RWKV_HEAD_DIM)
            lhs = jnp.concatenate([kkt_ref[b, :, sl], rt_ref[b, :, sl]], axis=0)
            rhs = rhs_scr[:, sl]
            for h2 in range(2):
                sel = head0 if h2 == 0 else lane >= RWKV_HEAD_DIM
                a = lax.dot_general(jnp.where(sel, lhs, jnp.zeros_like(lhs)), rhs,
                                    (((1,), (1,)), ((), ())), preferred_element_type=F32)
                a = jnp.where(keep, a, 0.0) * sign
                inst = h2 * (WKV_BATCH_BLOCK * WKV_PAIRS) + b * WKV_PAIRS + j
                abs_scr[pl.ds(inst * WKV_ROW_STRIDE, c), :] = a[:c, :c]
                top_scr[j, :, h2 * 2 * c:(h2 + 1) * 2 * c] = a[:c].astype(BF16)
                lo_ref[b, :, (2 * j + h2) * 2 * c:(2 * j + h2 + 1) * 2 * c] = a[c:].astype(BF16)

        for j in range(WKV_PAIRS):
            sl = slice(j * 2 * RWKV_HEAD_DIM, (j + 1) * 2 * RWKV_HEAD_DIM)
            v = vb_ref[b, :, sl]
            v0 = jnp.where(head0, v, jnp.zeros_like(v))
            w_akf = jnp.concatenate([zeros, v0, zeros, v - v0], axis=0)
            rhs0_ref[b, :, sl] = jnp.dot(top_scr[j], w_akf, preferred_element_type=F32)
        return carry

    lax.fori_loop(0, WKV_BATCH_BLOCK, per_batch, 0)

    def to_lanes(t, carry):
        abt_ref[0, t] = abs_scr[pl.ds(t, 2 * WKV_BATCH_BLOCK * WKV_PAIRS, stride=WKV_ROW_STRIDE), :].T
        return carry

    lax.fori_loop(0, c, to_lanes, 0, unroll=8)


def _wkv_apply_kernel(kkt_ref, rt_ref, kfh_ref, nbh_ref, vb_ref, lo_ref, rhs0_ref, gend_ref, abt_ref, s0_ref,
                      o_ref, sfin_ref, x_scr, rs_scr, rt_scr):
    c = WKV_CHUNK
    n = RWKV_HEAD_DIM
    ch = pl.program_id(1)
    _, _, block_diag = _pair_masks()
    lane = lax.broadcasted_iota(jnp.int32, (1, 2 * c), 1)
    head0 = lane < n
    eye2 = (lax.broadcasted_iota(jnp.int32, (n, 2 * n), 0)
            == lax.broadcasted_iota(jnp.int32, (n, 2 * n), 1) % n).astype(F32)
    n_pair_rows = WKV_BATCH_BLOCK * WKV_PAIRS

    @pl.when(ch == 0)
    def _():
        def init(b, carry):
            for j in range(WKV_PAIRS):
                sp = s0_ref[b, 2 * j:2 * j + 2].reshape(2 * n, n)
                dup = jnp.dot(sp, eye2, precision=HIGHEST, preferred_element_type=F32)
                x_scr[b, j] = jnp.where(block_diag, dup, 0.0)
            return carry
        lax.fori_loop(0, WKV_BATCH_BLOCK, init, 0)

    def before_solve(b, carry):
        for j in range(WKV_PAIRS):
            sl = slice(j * 2 * n, (j + 1) * 2 * n)
            lhs = jnp.concatenate([kkt_ref[b, :, sl], rt_ref[b, :, sl]], axis=0)
            kx = lax.dot_general(lhs, x_scr[b, j].astype(BF16), (((1,), (1,)), ((), ())),
                                 preferred_element_type=F32)
            row0 = (b * WKV_PAIRS + j) * WKV_ROW_STRIDE
            rs_scr[pl.ds(row0, c), :] = kx[:c] + rhs0_ref[b, :, sl]
            o_ref[b, :, sl] = kx[c:]
        return carry

    lax.fori_loop(0, WKV_BATCH_BLOCK, before_solve, 0)

    def to_lanes(t, carry):
        m = rs_scr[pl.ds(t, n_pair_rows, stride=WKV_ROW_STRIDE), :].T
        rt_scr[t] = jnp.concatenate([m[:n], m[n:]], axis=1)
        return carry

    lax.fori_loop(0, c, to_lanes, 0, unroll=8)

    def solve_rows(tb, carry):
        t0 = tb * WKV_SOLVE_ROWS
        rows = range(WKV_SOLVE_ROWS)

        def sub(ib, accs):
            for di in rows:
                i = ib * WKV_SOLVE_ROWS + di
                p = rt_scr[i]
                accs = tuple(accs[r] - abt_ref[0, t0 + r, pl.ds(i, 1), :] * p for r in rows)
            return accs

        accs = list(lax.fori_loop(0, tb, sub, tuple(rt_scr[t0 + r] for r in rows)))
        for r in rows:
            for r2 in range(r):
                accs[r] = accs[r] - abt_ref[0, t0 + r, pl.ds(t0 + r2, 1), :] * accs[r2]
            rt_scr[t0 + r] = accs[r]
        return carry

    lax.fori_loop(0, c // WKV_SOLVE_ROWS, solve_rows, 0)

    def from_lanes(t, carry):
        m = rt_scr[t].T
        rs_scr[pl.ds(t, n_pair_rows, stride=WKV_ROW_STRIDE), :] = jnp.concatenate(
            [m[:n_pair_rows], m[n_pair_rows:]], axis=1)
        return carry

    lax.fori_loop(0, c, from_lanes, 0, unroll=8)

    def after_solve(b, carry):
        for j in range(WKV_PAIRS):
            sl = slice(j * 2 * n, (j + 1) * 2 * n)
            row0 = (b * WKV_PAIRS + j) * WKV_ROW_STRIDE
            p = rs_scr[pl.ds(row0, c), :]
            v = vb_ref[b, :, sl].astype(F32)
            p0, v0 = jnp.where(head0, p, 0.0), jnp.where(head0, v, 0.0)
            w_o = jnp.concatenate([p0, v0, p - p0, v - v0], axis=0).astype(BF16)
            o_ref[b, :, sl] = o_ref[b, :, sl] + jnp.dot(
                lo_ref[b, :, j * 4 * c:(j + 1) * 4 * c], w_o, preferred_element_type=F32)
            vp = jnp.concatenate([v, p], axis=0).astype(BF16)
            kb = jnp.concatenate([kfh_ref[b, :, sl], nbh_ref[b, :, sl]], axis=0)
            upd = lax.dot_general(vp, kb, (((0,), (0,)), ((), ())), preferred_element_type=F32)
            x_scr[b, j] = jnp.where(block_diag, x_scr[b, j] * gend_ref[b, 0:1, sl] + upd, 0.0)
        return carry

    lax.fori_loop(0, WKV_BATCH_BLOCK, after_solve, 0)

    @pl.when(ch == pl.num_programs(1) - 1)
    def _():
        def fin(b, carry):
            for j in range(WKV_PAIRS):
                sp = lax.dot_general(x_scr[b, j], eye2, (((1,), (1,)), ((), ())),
                                     precision=HIGHEST, preferred_element_type=F32)
                sfin_ref[b, 2 * j:2 * j + 2] = sp.reshape(2, n, n)
            return carry
        lax.fori_loop(0, WKV_BATCH_BLOCK, fin, 0)


def wkv_chunked(r, lw, kf, v, kk, kb, s0):
    b, l, wd = r.shape
    c = WKV_CHUNK
    assert b % WKV_BATCH_BLOCK == 0 and l % c == 0
    gb, nc = b // WKV_BATCH_BLOCK, l // c
    lanes = 2 * WKV_BATCH_BLOCK * WKV_PAIRS
    seq = lambda w_: pl.BlockSpec((WKV_BATCH_BLOCK, c, w_), lambda g, i: (g, i, 0))
    gend_spec = pl.BlockSpec((WKV_BATCH_BLOCK, 8, wd), lambda g, i: (g * nc + i, 0, 0))
    abt_spec = pl.BlockSpec((1, c, c, lanes), lambda g, i: (g * nc + i, 0, 0, 0))
    sds = jax.ShapeDtypeStruct
    prepared = pl.pallas_call(
        _wkv_prepare_kernel,
        grid=(gb, nc),
        in_specs=[seq(wd)] * 6,
        out_specs=[seq(wd)] * 5 + [seq(2 * wd), seq(wd), gend_spec, abt_spec],
        out_shape=[sds((b, l, wd), BF16)] * 5 + [sds((b, l, 2 * wd), BF16), sds((b, l, wd), F32),
                                                 sds((gb * nc * WKV_BATCH_BLOCK, 8, wd), F32),
                                                 sds((gb * nc, c, c, lanes), F32)],
        scratch_shapes=[pltpu.VMEM((lanes * WKV_ROW_STRIDE, c), F32),
                        pltpu.VMEM((2 * c, wd), BF16),
                        pltpu.VMEM((WKV_PAIRS, c, 4 * c), BF16)],
        compiler_params=pltpu.CompilerParams(
            dimension_semantics=("parallel", "parallel"), vmem_limit_bytes=V7X_VMEM_LIMIT),
        name="wkv_prepare",
    )(r, lw, kf, v, kk, kb)
    st_spec = pl.BlockSpec((WKV_BATCH_BLOCK, RWKV_HEADS, RWKV_HEAD_DIM, RWKV_HEAD_DIM),
                           lambda g, i: (g, 0, 0, 0))
    o, s_fin = pl.pallas_call(
        _wkv_apply_kernel,
        grid=(gb, nc),
        in_specs=[seq(wd)] * 5 + [seq(2 * wd), seq(wd), gend_spec, abt_spec, st_spec],
        out_specs=[seq(wd), st_spec],
        out_shape=[sds((b, l, wd), F32), sds(s0.shape, F32)],
        scratch_shapes=[pltpu.VMEM((WKV_BATCH_BLOCK, WKV_PAIRS, 2 * RWKV_HEAD_DIM, 2 * RWKV_HEAD_DIM), F32),
                        pltpu.VMEM((WKV_BATCH_BLOCK * WKV_PAIRS * WKV_ROW_STRIDE, 2 * RWKV_HEAD_DIM), F32),
                        pltpu.VMEM((c, RWKV_HEAD_DIM, lanes), F32)],
        compiler_params=pltpu.CompilerParams(
            dimension_semantics=("parallel", "arbitrary"), vmem_limit_bytes=V7X_VMEM_LIMIT),
        name="wkv_apply",
    )(*prepared, s0)
    return o, s_fin


def _mix_out_kernel(x_ref, ys_ref, o_ref, gate_ref, bonus_ref, lnw_ref, lnb_ref,
                    woa_ref, wob_ref, h_ref):
    o = o_ref[...]
    inv_n = 1.0 / RWKV_HEAD_DIM
    mu = _head_sums(o) * inv_n
    d = o - mu
    var = _head_sums(d * d) * inv_n
    on = d * lax.rsqrt(var + GN_EPS) * lnw_ref[...] + lnb_ref[...]
    y_rwkv = (on + bonus_ref[...]) * gate_ref[...]
    mix = _bdot(ys_ref[...], woa_ref[...]) + _bdot(y_rwkv, wob_ref[...])
    h_ref[...] = x_ref[...] + mix


def mix_out(x, y_ssd, o, gate, bonus, lnw, lnb, woa, wob, *, tm):
    n = x.shape[0]
    row = pl.BlockSpec((tm, D_MODEL), lambda i: (i, 0))
    consts = [lnw, lnb, woa, wob]
    return pl.pallas_call(
        _mix_out_kernel,
        grid=(n // tm,),
        in_specs=[row] * 5 + [_const_spec(t.shape) for t in consts],
        out_specs=row,
        out_shape=jax.ShapeDtypeStruct((n, D_MODEL), F32),
        compiler_params=pltpu.CompilerParams(
            dimension_semantics=("parallel",), vmem_limit_bytes=V7X_VMEM_LIMIT),
        name="mix_out",
    )(x, y_ssd, o, gate, bonus, *consts)


def _ffn_kernel(h_ref, p_ref, nf_ref, wg_ref, wu_ref, wd_ref, np_ref, wpg_ref, wpp_ref,
                nl_ref, y_ref):
    h = h_ref[...]
    hf = _rms(h, nf_ref[...]).astype(BF16)
    gate = jnp.dot(hf, wg_ref[...], preferred_element_type=F32)
    up = jnp.dot(hf, wu_ref[...], preferred_element_type=F32)
    h = h + _bdot(_silu(gate) * up, wd_ref[...])
    pg = _sigmoid(_bdot(_rms(h, np_ref[...]), wpg_ref[...]))
    h = h + pg * _bdot(p_ref[...], wpp_ref[...])
    y_ref[...] = _rms(h, nl_ref[...])


def ffn_ple(h, p, nf, wg, wu, wd, npl, wpg, wpp, nl, *, tm):
    n = h.shape[0]
    consts = [nf, wg, wu, wd, npl, wpg, wpp, nl]
    return pl.pallas_call(
        _ffn_kernel,
        grid=(n // tm,),
        in_specs=[pl.BlockSpec((tm, D_MODEL), lambda i: (i, 0)),
                  pl.BlockSpec((tm, PLE_DIM), lambda i: (i, 0))]
                 + [_const_spec(t.shape) for t in consts],
        out_specs=pl.BlockSpec((tm, D_MODEL), lambda i: (i, 0)),
        out_shape=jax.ShapeDtypeStruct((n, D_MODEL), F32),
        compiler_params=pltpu.CompilerParams(
            dimension_semantics=("parallel",), vmem_limit_bytes=V7X_VMEM_LIMIT),
        name="ffn_ple",
    )(h, p, *consts)


def _prepare_weights(w):
    c0, c1, c2 = SSD_WIDTH, SSD_WIDTH + SSD_CONV_DIM, SSD_WIDTH + SSD_CONV_DIM + SSD_HEADS
    w_in = w["w_in"]
    rowv = lambda t: t.reshape(1, -1)
    return dict(
        w,
        wz=w_in[:, :c0].astype(BF16), wx=w_in[:, c0:c1].astype(BF16),
        wdt=w_in[:, c1:c2].astype(BF16), wr=w_in[:, c2:].astype(BF16),
        woa=w["w_out"][:SSD_WIDTH].astype(BF16), wob=w["w_out"][SSD_WIDTH:].astype(BF16),
        wg=w["w_gate"].astype(BF16), wu=w["w_up"].astype(BF16), wd=w["w_down"].astype(BF16),
        wpg=w["w_ple_gate"].astype(BF16), wpp=w["w_ple_proj"].astype(BF16),
        norm_mix_r=rowv(w["norm_mix"]), norm_ffn_r=rowv(w["norm_ffn"]),
        norm_ple_r=rowv(w["norm_ple"]), norm_final_r=rowv(w["norm_final"]),
        ln_x_w_r=rowv(w["ln_x_w"]), ln_x_b_r=rowv(w["ln_x_b"]),
    )


def layer_forward(x, p, conv0, shift0, ssm0, wkv0, w, *, tm, ssd_q, prep_tt, wkv_steps):
    b, l, _ = x.shape
    n = b * l
    x2 = x.reshape(n, D_MODEL)
    z, xbc, rw, dt = in_projection(x2, w["norm_mix_r"], w["wz"], w["wx"], w["wr"], w["wdt"], tm=tm)
    y_ssd, ssm_new, conv_new = ssd_mixer(
        z.reshape(b, l, -1), xbc.reshape(b, l, -1), dt.reshape(b, l, -1), conv0, ssm0, w, q=ssd_q)
    chunked = l % WKV_CHUNK == 0
    r, dec, kf, v, kk, kka, gate, bonus, shift_new = rwkv_prep(
        rw.reshape(b, l, -1), shift0, w, tt=prep_tt, log_decay=chunked)
    if chunked:
        o, wkv_new = wkv_chunked(r, dec, kf, v, kk, kka, wkv0)
    else:
        o, wkv_new = wkv_scan(r, dec, kf, v, kk, kka, wkv0, steps=wkv_steps)
    flat = lambda t: t.reshape(n, -1)
    h = mix_out(x2, flat(y_ssd), flat(o), flat(gate), flat(bonus), w["ln_x_w_r"], w["ln_x_b_r"],
                w["woa"], w["wob"], tm=tm)
    y = ffn_ple(h, p.reshape(n, PLE_DIM), w["norm_ffn_r"], w["wg"], w["wu"], w["wd"],
                w["norm_ple_r"], w["wpg"], w["wpp"], w["norm_final_r"], tm=tm)
    return y.reshape(b, l, D_MODEL), ssm_new, conv_new, wkv_new, shift_new


def kernel(x_prompt, x_sample, state_ssm, state_conv, state_wkv, state_shift, p_prompt, p_sample, norm_mix, w_in, conv_w, conv_b, dt_bias, a_log, d_skip, ssd_norm, shift_mu, w0, w2, a0, a2, g2, k_k, k_a, r_k, ln_x_w, ln_x_b, w_out, norm_ffn, w_gate, w_up, w_down, norm_ple, w_ple_gate, w_ple_proj, norm_final):
    w = _prepare_weights(dict(
        norm_mix=norm_mix[0], w_in=w_in[0], conv_w=conv_w[0], conv_b=conv_b[0], dt_bias=dt_bias[0],
        a_log=a_log[0], d_skip=d_skip[0], ssd_norm=ssd_norm[0], shift_mu=shift_mu[0], w0=w0[0],
        w2=w2[0], a0=a0[0], a2=a2[0], g2=g2[0], k_k=k_k[0], k_a=k_a[0], r_k=r_k[0],
        ln_x_w=ln_x_w[0], ln_x_b=ln_x_b[0], w_out=w_out[0], norm_ffn=norm_ffn[0],
        w_gate=w_gate[0], w_up=w_up[0], w_down=w_down[0], norm_ple=norm_ple[0],
        w_ple_gate=w_ple_gate[0], w_ple_proj=w_ple_proj[0], norm_final=norm_final))
    bp = x_prompt.shape[0]
    zeros = lambda *s: jnp.zeros(s, F32)
    yp, s1, c1, k1, t1 = layer_forward(
        x_prompt, p_prompt[0], zeros(bp, SSD_CONV - 1, SSD_CONV_DIM), zeros(bp, 1, RWKV_PROJ),
        zeros(bp, SSD_HEADS, SSD_HEAD_DIM, SSD_STATE),
        zeros(bp, RWKV_HEADS, RWKV_HEAD_DIM, RWKV_HEAD_DIM), w,
        tm=256, ssd_q=min(SSD_CHUNK, x_prompt.shape[1]), prep_tt=min(128, x_prompt.shape[1]),
        wkv_steps=min(16, x_prompt.shape[1]))
    ys, s2, c2, k2, t2 = layer_forward(
        x_sample, p_sample[0], state_conv[0], state_shift[0], state_ssm[0], state_wkv[0], w,
        tm=256, ssd_q=x_sample.shape[1], prep_tt=128, wkv_steps=x_sample.shape[1])
    return (yp, ys, s1[None], c1[None], k1[None], t1[None], s2[None], c2[None], k2[None], t2[None])
```

```python
import functools

import jax
import jax.numpy as jnp
from jax import lax
from jax.experimental import pallas as pl
from jax.experimental.pallas import tpu as pltpu

F32 = jnp.float32
BF16 = jnp.bfloat16
HIGHEST = lax.Precision.HIGHEST

D_MODEL = 1024
SSD_WIDTH = 1024
SSD_HEADS = 16
SSD_HEAD_DIM = 64
SSD_GROUPS = 2
SSD_GROUP_WIDTH = SSD_WIDTH // SSD_GROUPS
SSD_STATE = 128
SSD_CONV = 4
SSD_CHUNK = 128
SSD_BC = SSD_GROUPS * SSD_STATE
SSD_CONV_DIM = SSD_WIDTH + 2 * SSD_BC
RWKV_WIDTH = 1024
RWKV_HEADS = 16
RWKV_HEAD_DIM = 64
DECAY_LORA = 64
AAA_LORA = 64
GATE_LORA = 128
RWKV_PROJ = 3 * RWKV_WIDTH + DECAY_LORA + AAA_LORA + GATE_LORA
D_FF = 2816
PLE_DIM = 256
NORM_EPS = 1e-6
GN_EPS = 64e-5

WKV_BATCH_BLOCK = 8
V7X_VMEM_LIMIT = 56 * 1024 * 1024
CONV_PAD = 8
SSD_SEQS_PER_STEP = 4


def _rms(x, g):
    return x * lax.rsqrt(jnp.mean(x * x, axis=-1, keepdims=True) + NORM_EPS) * g


def _sigmoid(x):
    return 1.0 / (1.0 + jnp.exp(-x))


def _silu(x):
    return x * _sigmoid(x)


def _softplus(x):
    return jnp.maximum(x, 0.0) + jnp.log(1.0 + jnp.exp(-jnp.abs(x)))


def _bdot(a, b):
    return jnp.dot(a.astype(BF16), b.astype(BF16), preferred_element_type=F32)


def _split3(t):
    hi = t.astype(BF16)
    r1 = t - hi.astype(F32)
    mid = r1.astype(BF16)
    lo = (r1 - mid.astype(F32)).astype(BF16)
    return hi, mid, lo


def _dot01(a, b, *, exact_side):
    if exact_side == "lhs":
        m = b.astype(BF16)
        return sum(jnp.dot(p, m, preferred_element_type=F32) for p in _split3(a))
    m = a.astype(BF16)
    return sum(jnp.dot(m, p, preferred_element_type=F32) for p in _split3(b))


def _const_spec(shape):
    return pl.BlockSpec(shape, lambda *_: (0,) * len(shape), pipeline_mode=pl.Buffered(1))


def _head_expand(rows):
    h = lax.broadcasted_iota(jnp.int32, (rows, SSD_WIDTH), 0)
    c = lax.broadcasted_iota(jnp.int32, (rows, SSD_WIDTH), 1)
    return (c // SSD_HEAD_DIM == h).astype(F32)


def _proj_kernel(x_ref, g_ref, wz_ref, wx_ref, wr_ref, wdt_ref, wdtt_ref,
                 z_ref, xbc_ref, rw_ref, dt_ref, dtt_ref):
    u = _rms(x_ref[...], g_ref[...]).astype(BF16)
    z_ref[...] = jnp.dot(u, wz_ref[...], preferred_element_type=F32)
    xbc_ref[...] = jnp.dot(u, wx_ref[...], preferred_element_type=F32)
    rw_ref[...] = jnp.dot(u, wr_ref[...], preferred_element_type=F32)
    dt_ref[...] = jnp.dot(u, wdt_ref[...], preferred_element_type=F32)
    dtt_ref[...] = lax.dot_general(wdtt_ref[...], u, (((1,), (1,)), ((), ())), preferred_element_type=F32)


def in_projection(x, g, wz, wx, wr, wdt, *, tm):
    n = x.shape[0]
    row = lambda w: pl.BlockSpec((tm, w), lambda i: (i, 0))
    wdtt = wdt.T
    return pl.pallas_call(
        _proj_kernel,
        grid=(n // tm,),
        in_specs=[row(D_MODEL), _const_spec((1, D_MODEL)), _const_spec(wz.shape),
                  _const_spec(wx.shape), _const_spec(wr.shape), _const_spec(wdt.shape),
                  _const_spec(wdtt.shape)],
        out_specs=[row(SSD_WIDTH), row(SSD_CONV_DIM), row(RWKV_PROJ), row(SSD_HEADS),
                   pl.BlockSpec((SSD_HEADS, tm), lambda i: (0, i))],
        out_shape=[jax.ShapeDtypeStruct((n, SSD_WIDTH), F32),
                   jax.ShapeDtypeStruct((n, SSD_CONV_DIM), F32),
                   jax.ShapeDtypeStruct((n, RWKV_PROJ), F32),
                   jax.ShapeDtypeStruct((n, SSD_HEADS), F32),
                   jax.ShapeDtypeStruct((SSD_HEADS, n), F32)],
        compiler_params=pltpu.CompilerParams(
            dimension_semantics=("parallel",), vmem_limit_bytes=V7X_VMEM_LIMIT),
        name="in_projection",
    )(x, g, wz, wx, wr, wdt, wdtt)


def _ssd_kernel(z_ref, xbc_ref, dt_ref, dtt_ref, hist_ref, h0_ref, cw_ref, cb_ref,
                dtb_ref, dtbt_ref, alog_ref, alogt_ref, dsk_ref, nrm_ref,
                y_ref, hfin_ref, cnew_ref, xfull_scr, h_scr, *, q, nseq, single_chunk):
    refs = (z_ref, xbc_ref, dt_ref, dtt_ref, hist_ref, h0_ref, cw_ref, cb_ref, dtb_ref, dtbt_ref, alog_ref,
            alogt_ref, dsk_ref, nrm_ref, y_ref, hfin_ref, cnew_ref, xfull_scr, h_scr)
    for s in range(nseq):
        _ssd_sequence(s, *refs, q=q, single_chunk=single_chunk)


def _ssd_sequence(s, z_ref, xbc_ref, dt_ref, dtt_ref, hist_ref, h0_ref, cw_ref, cb_ref,
                  dtb_ref, dtbt_ref, alog_ref, alogt_ref, dsk_ref, nrm_ref,
                  y_ref, hfin_ref, cnew_ref, xfull_scr, h_scr, *, q, single_chunk):
    c = pl.program_id(1)
    last = pl.num_programs(1) - 1
    gw = SSD_GROUP_WIDTH

    @pl.when(c == 0)
    def _():
        xfull_scr[s,CONV_PAD - 3:CONV_PAD, :] = hist_ref[s]
        if not single_chunk:
            for g in range(SSD_GROUPS):
                h_scr[s * SSD_GROUPS + g] = h0_ref[s,g * 8:(g + 1) * 8].reshape(gw, SSD_STATE).T

    @pl.when(c > 0)
    def _():
        xfull_scr[s,CONV_PAD - 3:CONV_PAD, :] = xfull_scr[s,CONV_PAD + q - 3:CONV_PAD + q, :]

    xfull_scr[s,CONV_PAD:CONV_PAD + q, :] = xbc_ref[s]

    conv = cb_ref[...]
    for j in range(SSD_CONV):
        lo = CONV_PAD - 3 + j
        conv = conv + xfull_scr[s,lo:lo + q, :] * cw_ref[j:j + 1, :]
    act = _silu(conv)
    xs = act[:, :SSD_WIDTH]

    dt = _softplus(dt_ref[s] + dtb_ref[...])
    dtt_raw = dtt_ref[...] if len(dtt_ref.shape) == 2 else dtt_ref[s]
    dtt = _softplus(dtt_raw + dtbt_ref[...])
    da = dt * -jnp.exp(alog_ref[...])
    dat = dtt * -jnp.exp(alogt_ref[...])
    row = lax.broadcasted_iota(jnp.int32, (q, q), 0)
    col = lax.broadcasted_iota(jnp.int32, (q, q), 1)
    causal = row >= col
    a_cum = _dot01(causal.astype(F32), da, exact_side="rhs")
    a_cumt = _dot01(dat, (row <= col).astype(F32), exact_side="lhs")

    expand = _head_expand(SSD_HEADS)
    a_cum_x = _dot01(a_cum, expand, exact_side="lhs")
    dt_x = _dot01(dt, expand, exact_side="lhs")
    a_end_x = a_cum_x[q - 1:q, :]
    decay_in_x = jnp.exp(a_cum_x)
    chunk_decay_x = jnp.exp(a_end_x)
    xd = xs * (jnp.exp(a_end_x - a_cum_x) * dt_x)

    ys = []
    for g in range(SSD_GROUPS):
        bm = act[:, SSD_WIDTH + g * SSD_STATE:SSD_WIDTH + (g + 1) * SSD_STATE]
        cm = act[:, SSD_WIDTH + SSD_BC + g * SSD_STATE:SSD_WIDTH + SSD_BC + (g + 1) * SSD_STATE]
        cb = lax.dot_general(cm.astype(BF16), bm.astype(BF16), (((1,), (1,)), ((), ())),
                             preferred_element_type=F32)
        y_heads = []
        for e in range(8):
            h = g * 8 + e
            seg = a_cum[:, h:h + 1] - a_cumt[h:h + 1, :]
            lmat = jnp.where(causal, jnp.exp(jnp.where(causal, seg, 0.0)), 0.0)
            w_qs = cb * lmat * dtt[h:h + 1, :]
            y_heads.append(_bdot(w_qs, xs[:, h * SSD_HEAD_DIM:(h + 1) * SSD_HEAD_DIM]))
        y_diag = jnp.concatenate(y_heads, axis=1)
        sl = slice(g * gw, (g + 1) * gw)
        if single_chunk:
            h_in = h0_ref[s,g * 8:(g + 1) * 8].reshape(gw, SSD_STATE)
            y_off = lax.dot_general(cm.astype(BF16), h_in.astype(BF16), (((1,), (1,)), ((), ())),
                                    preferred_element_type=F32)
            upd = lax.dot_general(xd[:, sl].astype(BF16), bm.astype(BF16), (((0,), (0,)), ((), ())),
                                  preferred_element_type=F32)
            head_decay = jnp.broadcast_to(jnp.exp(a_cumt[:, q - 1:q]), (SSD_HEADS, SSD_STATE))
            for e in range(8):
                h = g * 8 + e
                hfin_ref[s,h] = (h0_ref[s,h] * head_decay[h:h + 1, :]
                                  + upd[e * SSD_HEAD_DIM:(e + 1) * SSD_HEAD_DIM, :])
        else:
            h_in = h_scr[s * SSD_GROUPS + g]
            y_off = _bdot(cm, h_in)
            upd = lax.dot_general(bm.astype(BF16), xd[:, sl].astype(BF16), (((0,), (0,)), ((), ())),
                                  preferred_element_type=F32)
            h_scr[s * SSD_GROUPS + g] = h_in * chunk_decay_x[:, sl] + upd
        ys.append(y_diag + y_off * decay_in_x[:, sl])

    y = jnp.concatenate(ys, axis=1) + dsk_ref[...] * xs
    yg = y * _silu(z_ref[s])
    outs = []
    for g in range(SSD_GROUPS):
        t = yg[:, g * gw:(g + 1) * gw]
        outs.append(t * lax.rsqrt(jnp.mean(t * t, axis=-1, keepdims=True) + NORM_EPS))
    y_ref[s] = jnp.concatenate(outs, axis=1) * nrm_ref[...]

    @pl.when(c == last)
    def _():
        cnew_ref[s] = xfull_scr[s,CONV_PAD + q - 3:CONV_PAD + q, :]
        if not single_chunk:
            for g in range(SSD_GROUPS):
                hfin_ref[s,g * 8:(g + 1) * 8] = h_scr[s * SSD_GROUPS + g].T.reshape(8, SSD_HEAD_DIM, SSD_STATE)


def ssd_mixer(z, xbc, dt, dtt_flat, conv0, ssm0, w, *, q):
    b, l, _ = z.shape
    single_chunk = l == q
    nseq = SSD_SEQS_PER_STEP if (single_chunk and b % SSD_SEQS_PER_STEP == 0) else 1
    if q % 128 == 0 and nseq == 1:
        dtt = dtt_flat
        dtt_spec = pl.BlockSpec((SSD_HEADS, q), lambda i, c: (0, i * (l // q) + c))
    else:
        dtt = jnp.swapaxes(dt, 1, 2)
        dtt_spec = pl.BlockSpec((nseq, SSD_HEADS, q), lambda i, c: (i, 0, c))
    seq = lambda wd: pl.BlockSpec((nseq, q, wd), lambda i, c: (i, c, 0))
    per_b3 = lambda s: pl.BlockSpec((nseq,) + s, lambda i, c: (i,) + (0,) * len(s))
    col = lambda t: t.reshape(-1, 1)
    rowv = lambda t: t.reshape(1, -1)
    consts = [w["conv_w"], rowv(w["conv_b"]), rowv(w["dt_bias"]), col(w["dt_bias"]),
              rowv(w["a_log"]), col(w["a_log"]),
              rowv(jnp.repeat(w["d_skip"], SSD_HEAD_DIM)), rowv(w["ssd_norm"])]
    return pl.pallas_call(
        functools.partial(_ssd_kernel, q=q, nseq=nseq, single_chunk=single_chunk),
        grid=(b // nseq, l // q),
        in_specs=[seq(SSD_WIDTH), seq(SSD_CONV_DIM), seq(SSD_HEADS),
                  dtt_spec,
                  per_b3((SSD_CONV - 1, SSD_CONV_DIM)),
                  per_b3((SSD_HEADS, SSD_HEAD_DIM, SSD_STATE))]
                 + [_const_spec(t.shape) for t in consts],
        out_specs=[seq(SSD_WIDTH), per_b3((SSD_HEADS, SSD_HEAD_DIM, SSD_STATE)),
                   per_b3((SSD_CONV - 1, SSD_CONV_DIM))],
        out_shape=[jax.ShapeDtypeStruct((b, l, SSD_WIDTH), F32),
                   jax.ShapeDtypeStruct((b, SSD_HEADS, SSD_HEAD_DIM, SSD_STATE), F32),
                   jax.ShapeDtypeStruct((b, SSD_CONV - 1, SSD_CONV_DIM), F32)],
        scratch_shapes=[pltpu.VMEM((nseq, CONV_PAD + q, SSD_CONV_DIM), F32),
                        pltpu.VMEM((nseq * SSD_GROUPS, SSD_STATE, SSD_GROUP_WIDTH), F32)],
        compiler_params=pltpu.CompilerParams(
            dimension_semantics=("parallel", "arbitrary"), vmem_limit_bytes=V7X_VMEM_LIMIT),
        name="ssd_mixer",
    )(z, xbc, dt, dtt, conv0, ssm0, *consts)


def _head_sums(t):
    pair = 2 * RWKV_HEAD_DIM
    first = lax.broadcasted_iota(jnp.int32, (1, pair), 1) < RWKV_HEAD_DIM
    pieces = []
    for j in range(RWKV_HEADS // 2):
        x = t[:, j * pair:(j + 1) * pair]
        x0 = jnp.where(first, x, 0.0)
        s0 = jnp.sum(x0, axis=-1, keepdims=True)
        s1 = jnp.sum(x - x0, axis=-1, keepdims=True)
        pieces.append(jnp.where(first, s0, s1))
    return jnp.concatenate(pieces, axis=1)


def _rwkv_prep_kernel(rw_ref, sh0_ref, mu_ref, w0_ref, w2_ref, a0_ref, a2_ref, g2_ref,
                      kk_ref, ka_ref, rk_ref,
                      *rest, tt, seqs, chunk):
    outs, full_scr = rest[:-1], rest[-1]
    gate_out, bonus_out, shn_ref = outs[-3:]
    c = pl.program_id(1)
    wd = RWKV_WIDTH
    l = tt // seqs

    if seqs == 1:
        @pl.when(c == 0)
        def _():
            full_scr[0, CONV_PAD - 1:CONV_PAD, :] = sh0_ref[0]

        @pl.when(c > 0)
        def _():
            full_scr[0, CONV_PAD - 1:CONV_PAD, :] = full_scr[0, CONV_PAD + tt - 1:CONV_PAD + tt, :]

        rw = rw_ref[0]
        full_scr[0, CONV_PAD:CONV_PAD + tt, :] = rw
        prev = full_scr[0, CONV_PAD - 1:CONV_PAD - 1 + tt, :]
    else:
        full_scr[:, CONV_PAD - 1:CONV_PAD, :] = sh0_ref[...]
        full_scr[:, CONV_PAD:CONV_PAD + l, :] = rw_ref[...]
        rw = rw_ref[...].reshape(tt, RWKV_PROJ)
        prev = full_scr[:, CONV_PAD - 1:CONV_PAD - 1 + l, :].reshape(tt, RWKV_PROJ)
    u = rw + (prev - rw) * mu_ref[...]
    r = u[:, :wd]
    k = u[:, wd:2 * wd]
    v = u[:, 2 * wd:3 * wd]
    w_lo = u[:, 3 * wd:3 * wd + DECAY_LORA]
    a_lo = u[:, 3 * wd + DECAY_LORA:3 * wd + DECAY_LORA + AAA_LORA]
    g_lo = u[:, 3 * wd + DECAY_LORA + AAA_LORA:]

    w_log = -_softplus(-(w0_ref[...] + _bdot(jnp.tanh(w_lo), w2_ref[...]))) - 0.5
    lw = -jnp.exp(w_log)
    a = _sigmoid(a0_ref[...] + _bdot(a_lo, a2_ref[...]))
    gate = _bdot(_sigmoid(g_lo), g2_ref[...])

    kk = k * kk_ref[...]
    kk = kk / jnp.maximum(jnp.sqrt(_head_sums(kk * kk)), 1e-12)
    kf = k * (1.0 + (a - 1.0) * ka_ref[...])
    kb = kk * a
    bonus = _head_sums(r * kf * rk_ref[...]) * v

    blk = gate_out.shape
    gate_out[...] = gate.reshape(blk)
    bonus_out[...] = bonus.reshape(blk)
    if chunk == 0:
        for ref, val in zip(outs, (r, jnp.exp(lw), kf, v, kk, kb)):
            ref[...] = val.reshape(blk)
    else:
        kkt_out, rt_out, kfh_out, nbh_out, vb_out, kbg_out, kfg_out, gend_out = outs[:8]
        tri = (lax.broadcasted_iota(jnp.int32, (chunk, chunk), 0)
               >= lax.broadcasted_iota(jnp.int32, (chunk, chunk), 1)).astype(F32)
        for ci in range(tt // chunk):
            rs = slice(ci * chunk, (ci + 1) * chunk)
            lw_c = lw[rs]
            cum = _dot01(tri, lw_c, exact_side="rhs")
            cum_end = cum[chunk - 1:chunk, :]
            g_inv = jnp.exp(-cum)
            g_tail = jnp.exp(cum_end - cum)
            kkt_out[0, rs, :] = (kk[rs] * jnp.exp(cum - lw_c)).astype(BF16)
            rt_out[0, rs, :] = (r[rs] * jnp.exp(cum)).astype(BF16)
            kfh_out[0, rs, :] = (kf[rs] * g_tail).astype(BF16)
            nbh_out[0, rs, :] = (-kb[rs] * g_tail).astype(BF16)
            vb_out[0, rs, :] = v[rs].astype(BF16)
            kbg_out[0, rs, :] = (kb[rs] * g_inv).astype(BF16)
            kfg_out[0, rs, :] = (kf[rs] * g_inv).astype(BF16)
            gend_out[0, ci] = jnp.broadcast_to(jnp.exp(cum_end), (8, wd))

    if seqs == 1:
        @pl.when(c == pl.num_programs(1) - 1)
        def _():
            shn_ref[0] = full_scr[0, CONV_PAD + tt - 1:CONV_PAD + tt, :]
    else:
        shn_ref[...] = rw_ref[:, l - 1:l, :]


def rwkv_prep(rw, shift0, w, *, tt, chunk):
    b, l, _ = rw.shape
    seqs = max(1, tt // l)
    assert chunk == 0 or (seqs == 1 and tt % chunk == 0)
    rowv = lambda t: t.reshape(1, -1)
    consts = [rowv(w["shift_mu"]), rowv(w["w0"]), w["w2"].astype(BF16), rowv(w["a0"]),
              w["a2"].astype(BF16), w["g2"].astype(BF16), rowv(w["k_k"]), rowv(w["k_a"]),
              rowv(w["r_k"])]
    rows = tt // seqs
    grid = (b // seqs, l // rows)
    seq = lambda wd: pl.BlockSpec((seqs, rows, wd), lambda i, c: (i, c, 0))
    one = pl.BlockSpec((seqs, 1, RWKV_PROJ), lambda i, c: (i, 0, 0))
    sds = jax.ShapeDtypeStruct
    f32_seq = sds((b, l, RWKV_WIDTH), F32)
    if chunk == 0:
        op_specs = [seq(RWKV_WIDTH)] * 6
        op_shapes = [f32_seq] * 6
    else:
        per_tile = tt // chunk
        op_specs = [seq(RWKV_WIDTH)] * 7 + [
            pl.BlockSpec((1, per_tile, 8, RWKV_WIDTH), lambda i, c: (i, c, 0, 0))]
        op_shapes = [sds((b, l, RWKV_WIDTH), BF16)] * 7 + [sds((b, l // chunk, 8, RWKV_WIDTH), F32)]
    outs = pl.pallas_call(
        functools.partial(_rwkv_prep_kernel, tt=tt, seqs=seqs, chunk=chunk),
        grid=grid,
        in_specs=[seq(RWKV_PROJ), one] + [_const_spec(t.shape) for t in consts],
        out_specs=op_specs + [seq(RWKV_WIDTH)] * 2 + [one],
        out_shape=op_shapes + [f32_seq] * 2 + [sds((b, 1, RWKV_PROJ), F32)],
        scratch_shapes=[pltpu.VMEM((seqs, CONV_PAD + rows, RWKV_PROJ), F32)],
        compiler_params=pltpu.CompilerParams(
            dimension_semantics=("parallel", "arbitrary"), vmem_limit_bytes=V7X_VMEM_LIMIT),
        name="rwkv_prep",
    )(rw, shift0, *consts)
    return outs


def _wkv_kernel(r_ref, w_ref, k_ref, v_ref, kk_ref, kka_ref, s0_ref,
                o_ref, sfin_ref, s_scr, vt_scr, ot_scr, *, steps):
    c = pl.program_id(1)
    n = RWKV_HEAD_DIM
    lanes = WKV_BATCH_BLOCK * RWKV_HEADS

    @pl.when(c == 0)
    def _():
        s_scr[...] = s0_ref[...].reshape(lanes, n * n).T.reshape(n, n, lanes)

    def to_pairs(ref, t):
        return ref[:, t].reshape(lanes, n).T

    def step(t, carry):
        r_t = to_pairs(r_ref, t)
        w_t = to_pairs(w_ref, t)
        k_t = to_pairs(k_ref, t)
        kk_t = to_pairs(kk_ref, t)
        kka_t = to_pairs(kka_ref, t)
        vt_scr[...] = to_pairs(v_ref, t)

        def per_value(vi, carry2):
            s_v = s_scr[vi]
            skk = jnp.sum(s_v * kk_t, axis=0, keepdims=True)
            v_row = vt_scr[pl.ds(vi, 1), :]
            s_new = s_v * w_t - skk * kka_t + v_row * k_t
            s_scr[vi] = s_new
            ot_scr[pl.ds(vi, 1), :] = jnp.sum(s_new * r_t, axis=0, keepdims=True)
            return carry2

        lax.fori_loop(0, n, per_value, 0, unroll=4)
        o_ref[:, t] = ot_scr[...].T.reshape(WKV_BATCH_BLOCK, RWKV_HEADS, n)
        return carry

    lax.fori_loop(0, steps, step, 0)

    @pl.when(c == pl.num_programs(1) - 1)
    def _():
        sfin_ref[...] = s_scr[...].reshape(n * n, lanes).T.reshape(
            WKV_BATCH_BLOCK, RWKV_HEADS, n, n)


def wkv_scan(r, w, k, v, kk, kka, s0, *, steps):
    b, l, _ = r.shape
    h, n = RWKV_HEADS, RWKV_HEAD_DIM
    assert b % WKV_BATCH_BLOCK == 0 and l % steps == 0
    ops = [t.reshape(b, l, h, n) for t in (r, w, k, v, kk, kka)]
    seq_spec = pl.BlockSpec((WKV_BATCH_BLOCK, steps, h, n), lambda g, c: (g, c, 0, 0))
    st_spec = pl.BlockSpec((WKV_BATCH_BLOCK, h, n, n), lambda g, c: (g, 0, 0, 0))
    o, s_fin = pl.pallas_call(
        functools.partial(_wkv_kernel, steps=steps),
        grid=(b // WKV_BATCH_BLOCK, l // steps),
        in_specs=[seq_spec] * 6 + [st_spec],
        out_specs=[seq_spec, st_spec],
        out_shape=[jax.ShapeDtypeStruct((b, l, h, n), F32),
                   jax.ShapeDtypeStruct((b, h, n, n), F32)],
        scratch_shapes=[pltpu.VMEM((n, n, WKV_BATCH_BLOCK * h), F32),
                        pltpu.VMEM((n, WKV_BATCH_BLOCK * h), F32),
                        pltpu.VMEM((n, WKV_BATCH_BLOCK * h), F32)],
        compiler_params=pltpu.CompilerParams(
            dimension_semantics=("parallel", "arbitrary"), vmem_limit_bytes=V7X_VMEM_LIMIT),
        name="wkv_scan",
    )(*ops, s0)
    return o.reshape(b, l, h * n), s_fin


WKV_CHUNK = 64
WKV_PAIRS = RWKV_HEADS // 2
WKV_ROW_STRIDE = WKV_CHUNK + 8
WKV_SOLVE_ROWS = 4


def _pair_masks():
    c = WKV_CHUNK
    row = lax.broadcasted_iota(jnp.int32, (2 * c, 2 * c), 0)
    col = lax.broadcasted_iota(jnp.int32, (2 * c, 2 * c), 1)
    t, i = row % c, col % c
    keep = i <= t - jnp.where(row < c, 1, 0)
    sign = jnp.where(row >= c, jnp.where(col < c, -1.0, 1.0), 1.0)
    block_diag = row // c == col // c
    return keep, sign, block_diag


def _wkv_prepare_kernel(kkt_ref, rt_ref, kbg_ref, kfg_ref, vb_ref,
                        lo_ref, rhs0_ref, abt_ref, abs_scr, top_scr):
    c = WKV_CHUNK
    keep, sign, _ = _pair_masks()
    lane = lax.broadcasted_iota(jnp.int32, (1, 2 * c), 1)
    head0 = lane < RWKV_HEAD_DIM
    zeros = jnp.zeros((c, 2 * c), BF16)

    def per_batch(b, carry):
        for j in range(WKV_PAIRS):
            sl = slice(j * 2 * RWKV_HEAD_DIM, (j + 1) * 2 * RWKV_HEAD_DIM)
            lhs = jnp.concatenate([kkt_ref[b, :, sl], rt_ref[b, :, sl]], axis=0)
            rhs = jnp.concatenate([kbg_ref[b, :, sl], kfg_ref[b, :, sl]], axis=0)
            for h2 in range(2):
                sel = head0 if h2 == 0 else lane >= RWKV_HEAD_DIM
                a = lax.dot_general(jnp.where(sel, lhs, jnp.zeros_like(lhs)), rhs,
                                    (((1,), (1,)), ((), ())), preferred_element_type=F32)
                a = jnp.where(keep, a, 0.0) * sign
                inst = h2 * (WKV_BATCH_BLOCK * WKV_PAIRS) + b * WKV_PAIRS + j
                abs_scr[pl.ds(inst * WKV_ROW_STRIDE, c), :] = a[:c, :c]
                top_scr[j, :, h2 * 2 * c:(h2 + 1) * 2 * c] = a[:c].astype(BF16)
                lo_ref[b, :, (2 * j + h2) * 2 * c:(2 * j + h2 + 1) * 2 * c] = a[c:].astype(BF16)

        for j in range(WKV_PAIRS):
            sl = slice(j * 2 * RWKV_HEAD_DIM, (j + 1) * 2 * RWKV_HEAD_DIM)
            v = vb_ref[b, :, sl]
            v0 = jnp.where(head0, v, jnp.zeros_like(v))
            w_akf = jnp.concatenate([zeros, v0, zeros, v - v0], axis=0)
            rhs0_ref[b, :, sl] = jnp.dot(top_scr[j], w_akf, preferred_element_type=F32)
        return carry

    lax.fori_loop(0, WKV_BATCH_BLOCK, per_batch, 0)

    def to_lanes(t, carry):
        abt_ref[0, t] = abs_scr[pl.ds(t, 2 * WKV_BATCH_BLOCK * WKV_PAIRS, stride=WKV_ROW_STRIDE), :].T
        return carry

    lax.fori_loop(0, c, to_lanes, 0, unroll=8)


def _wkv_apply_kernel(kkt_ref, rt_ref, kfh_ref, nbh_ref, vb_ref, lo_ref, rhs0_ref, gend_ref, abt_ref, s0_ref,
                      o_ref, sfin_ref, x_scr, rs_scr, rt_scr):
    c = WKV_CHUNK
    n = RWKV_HEAD_DIM
    ch = pl.program_id(1)
    _, _, block_diag = _pair_masks()
    lane = lax.broadcasted_iota(jnp.int32, (1, 2 * c), 1)
    head0 = lane < n
    eye2 = (lax.broadcasted_iota(jnp.int32, (n, 2 * n), 0)
            == lax.broadcasted_iota(jnp.int32, (n, 2 * n), 1) % n).astype(F32)
    n_pair_rows = WKV_BATCH_BLOCK * WKV_PAIRS

    @pl.when(ch == 0)
    def _():
        def init(b, carry):
            for j in range(WKV_PAIRS):
                sp = s0_ref[b, 2 * j:2 * j + 2].reshape(2 * n, n)
                dup = jnp.dot(sp, eye2, precision=HIGHEST, preferred_element_type=F32)
                x_scr[b, j] = jnp.where(block_diag, dup, 0.0)
            return carry
        lax.fori_loop(0, WKV_BATCH_BLOCK, init, 0)

    def before_solve(b, carry):
        for j in range(WKV_PAIRS):
            sl = slice(j * 2 * n, (j + 1) * 2 * n)
            lhs = jnp.concatenate([kkt_ref[b, :, sl], rt_ref[b, :, sl]], axis=0)
            kx = lax.dot_general(lhs, x_scr[b, j].astype(BF16), (((1,), (1,)), ((), ())),
                                 preferred_element_type=F32)
            row0 = (b * WKV_PAIRS + j) * WKV_ROW_STRIDE
            rs_scr[pl.ds(row0, c), :] = kx[:c] + rhs0_ref[b, :, sl]
            o_ref[b, :, sl] = kx[c:]
        return carry

    lax.fori_loop(0, WKV_BATCH_BLOCK, before_solve, 0)

    def to_lanes(t, carry):
        m = rs_scr[pl.ds(t, n_pair_rows, stride=WKV_ROW_STRIDE), :].T
        rt_scr[t] = jnp.concatenate([m[:n], m[n:]], axis=1)
        return carry

    lax.fori_loop(0, c, to_lanes, 0, unroll=8)

    def solve_rows(tb, carry):
        t0 = tb * WKV_SOLVE_ROWS
        rows = range(WKV_SOLVE_ROWS)

        def sub(ib, accs):
            for di in rows:
                i = ib * WKV_SOLVE_ROWS + di
                p = rt_scr[i]
                accs = tuple(accs[r] - abt_ref[0, t0 + r, pl.ds(i, 1), :] * p for r in rows)
            return accs

        accs = list(lax.fori_loop(0, tb, sub, tuple(rt_scr[t0 + r] for r in rows)))
        for r in rows:
            for r2 in range(r):
                accs[r] = accs[r] - abt_ref[0, t0 + r, pl.ds(t0 + r2, 1), :] * accs[r2]
            rt_scr[t0 + r] = accs[r]
        return carry

    lax.fori_loop(0, c // WKV_SOLVE_ROWS, solve_rows, 0)

    def from_lanes(t, carry):
        m = rt_scr[t].T
        rs_scr[pl.ds(t, n_pair_rows, stride=WKV_ROW_STRIDE), :] = jnp.concatenate(
            [m[:n_pair_rows], m[n_pair_rows:]], axis=1)
        return carry

    lax.fori_loop(0, c, from_lanes, 0, unroll=8)

    def after_solve(b, carry):
        for j in range(WKV_PAIRS):
            sl = slice(j * 2 * n, (j + 1) * 2 * n)
            row0 = (b * WKV_PAIRS + j) * WKV_ROW_STRIDE
            p = rs_scr[pl.ds(row0, c), :]
            v = vb_ref[b, :, sl].astype(F32)
            p0, v0 = jnp.where(head0, p, 0.0), jnp.where(head0, v, 0.0)
            w_o = jnp.concatenate([p0, v0, p - p0, v - v0], axis=0).astype(BF16)
            o_ref[b, :, sl] = o_ref[b, :, sl] + jnp.dot(
                lo_ref[b, :, j * 4 * c:(j + 1) * 4 * c], w_o, preferred_element_type=F32)
            vp = jnp.concatenate([v, p], axis=0).astype(BF16)
            kb = jnp.concatenate([kfh_ref[b, :, sl], nbh_ref[b, :, sl]], axis=0)
            upd = lax.dot_general(vp, kb, (((0,), (0,)), ((), ())), preferred_element_type=F32)
            x_scr[b, j] = jnp.where(block_diag, x_scr[b, j] * gend_ref[b, 0, 0:1, sl] + upd, 0.0)
        return carry

    lax.fori_loop(0, WKV_BATCH_BLOCK, after_solve, 0)

    @pl.when(ch == pl.num_programs(1) - 1)
    def _():
        def fin(b, carry):
            for j in range(WKV_PAIRS):
                sp = lax.dot_general(x_scr[b, j], eye2, (((1,), (1,)), ((), ())),
                                     precision=HIGHEST, preferred_element_type=F32)
                sfin_ref[b, 2 * j:2 * j + 2] = sp.reshape(2, n, n)
            return carry
        lax.fori_loop(0, WKV_BATCH_BLOCK, fin, 0)


def wkv_chunked(kkt, rt, kfh, nbh, vb, kbg, kfg, gend, s0):
    b, l, wd = kkt.shape
    c = WKV_CHUNK
    assert b % WKV_BATCH_BLOCK == 0 and l % c == 0
    gb, nc = b // WKV_BATCH_BLOCK, l // c
    lanes = 2 * WKV_BATCH_BLOCK * WKV_PAIRS
    seq = lambda w_: pl.BlockSpec((WKV_BATCH_BLOCK, c, w_), lambda g, i: (g, i, 0))
    gend_spec = pl.BlockSpec((WKV_BATCH_BLOCK, 1, 8, wd), lambda g, i: (g, i, 0, 0))
    abt_spec = pl.BlockSpec((1, c, c, lanes), lambda g, i: (g * nc + i, 0, 0, 0))
    sds = jax.ShapeDtypeStruct
    lo, rhs0, abt = pl.pallas_call(
        _wkv_prepare_kernel,
        grid=(gb, nc),
        in_specs=[seq(wd)] * 5,
        out_specs=[seq(2 * wd), seq(wd), abt_spec],
        out_shape=[sds((b, l, 2 * wd), BF16), sds((b, l, wd), F32), sds((gb * nc, c, c, lanes), F32)],
        scratch_shapes=[pltpu.VMEM((lanes * WKV_ROW_STRIDE, c), F32),
                        pltpu.VMEM((WKV_PAIRS, c, 4 * c), BF16)],
        compiler_params=pltpu.CompilerParams(
            dimension_semantics=("parallel", "parallel"), vmem_limit_bytes=V7X_VMEM_LIMIT),
        name="wkv_prepare",
    )(kkt, rt, kbg, kfg, vb)
    st_spec = pl.BlockSpec((WKV_BATCH_BLOCK, RWKV_HEADS, RWKV_HEAD_DIM, RWKV_HEAD_DIM),
                           lambda g, i: (g, 0, 0, 0))
    o, s_fin = pl.pallas_call(
        _wkv_apply_kernel,
        grid=(gb, nc),
        in_specs=[seq(wd)] * 5 + [seq(2 * wd), seq(wd), gend_spec, abt_spec, st_spec],
        out_specs=[seq(wd), st_spec],
        out_shape=[sds((b, l, wd), F32), sds(s0.shape, F32)],
        scratch_shapes=[pltpu.VMEM((WKV_BATCH_BLOCK, WKV_PAIRS, 2 * RWKV_HEAD_DIM, 2 * RWKV_HEAD_DIM), F32),
                        pltpu.VMEM((WKV_BATCH_BLOCK * WKV_PAIRS * WKV_ROW_STRIDE, 2 * RWKV_HEAD_DIM), F32),
                        pltpu.VMEM((c, RWKV_HEAD_DIM, lanes), F32)],
        compiler_params=pltpu.CompilerParams(
            dimension_semantics=("parallel", "arbitrary"), vmem_limit_bytes=V7X_VMEM_LIMIT),
        name="wkv_apply",
    )(kkt, rt, kfh, nbh, vb, lo, rhs0, gend, abt, s0)
    return o, s_fin


def _mix_out_kernel(x_ref, ys_ref, o_ref, gate_ref, bonus_ref, lnw_ref, lnb_ref,
                    woa_ref, wob_ref, h_ref):
    o = o_ref[...]
    inv_n = 1.0 / RWKV_HEAD_DIM
    mu = _head_sums(o) * inv_n
    d = o - mu
    var = _head_sums(d * d) * inv_n
    on = d * lax.rsqrt(var + GN_EPS) * lnw_ref[...] + lnb_ref[...]
    y_rwkv = (on + bonus_ref[...]) * gate_ref[...]
    mix = _bdot(ys_ref[...], woa_ref[...]) + _bdot(y_rwkv, wob_ref[...])
    h_ref[...] = x_ref[...] + mix


def mix_out(x, y_ssd, o, gate, bonus, lnw, lnb, woa, wob, *, tm):
    n = x.shape[0]
    row = pl.BlockSpec((tm, D_MODEL), lambda i: (i, 0))
    consts = [lnw, lnb, woa, wob]
    return pl.pallas_call(
        _mix_out_kernel,
        grid=(n // tm,),
        in_specs=[row] * 5 + [_const_spec(t.shape) for t in consts],
        out_specs=row,
        out_shape=jax.ShapeDtypeStruct((n, D_MODEL), F32),
        compiler_params=pltpu.CompilerParams(
            dimension_semantics=("parallel",), vmem_limit_bytes=V7X_VMEM_LIMIT),
        name="mix_out",
    )(x, y_ssd, o, gate, bonus, *consts)


def _ffn_kernel(h_ref, p_ref, nf_ref, wg_ref, wu_ref, wd_ref, np_ref, wpg_ref, wpp_ref,
                nl_ref, y_ref):
    h = h_ref[...]
    hf = _rms(h, nf_ref[...]).astype(BF16)
    gate = jnp.dot(hf, wg_ref[...], preferred_element_type=F32)
    up = jnp.dot(hf, wu_ref[...], preferred_element_type=F32)
    h = h + _bdot(_silu(gate) * up, wd_ref[...])
    pg = _sigmoid(_bdot(_rms(h, np_ref[...]), wpg_ref[...]))
    h = h + pg * _bdot(p_ref[...], wpp_ref[...])
    y_ref[...] = _rms(h, nl_ref[...])


def ffn_ple(h, p, nf, wg, wu, wd, npl, wpg, wpp, nl, *, tm):
    n = h.shape[0]
    consts = [nf, wg, wu, wd, npl, wpg, wpp, nl]
    return pl.pallas_call(
        _ffn_kernel,
        grid=(n // tm,),
        in_specs=[pl.BlockSpec((tm, D_MODEL), lambda i: (i, 0)),
                  pl.BlockSpec((tm, PLE_DIM), lambda i: (i, 0))]
                 + [_const_spec(t.shape) for t in consts],
        out_specs=pl.BlockSpec((tm, D_MODEL), lambda i: (i, 0)),
        out_shape=jax.ShapeDtypeStruct((n, D_MODEL), F32),
        compiler_params=pltpu.CompilerParams(
            dimension_semantics=("parallel",), vmem_limit_bytes=V7X_VMEM_LIMIT),
        name="ffn_ple",
    )(h, p, *consts)


def _prepare_weights(w):
    c0, c1, c2 = SSD_WIDTH, SSD_WIDTH + SSD_CONV_DIM, SSD_WIDTH + SSD_CONV_DIM + SSD_HEADS
    w_in = w["w_in"]
    rowv = lambda t: t.reshape(1, -1)
    return dict(
        w,
        wz=w_in[:, :c0].astype(BF16), wx=w_in[:, c0:c1].astype(BF16),
        wdt=w_in[:, c1:c2].astype(BF16), wr=w_in[:, c2:].astype(BF16),
        woa=w["w_out"][:SSD_WIDTH].astype(BF16), wob=w["w_out"][SSD_WIDTH:].astype(BF16),
        wg=w["w_gate"].astype(BF16), wu=w["w_up"].astype(BF16), wd=w["w_down"].astype(BF16),
        wpg=w["w_ple_gate"].astype(BF16), wpp=w["w_ple_proj"].astype(BF16),
        norm_mix_r=rowv(w["norm_mix"]), norm_ffn_r=rowv(w["norm_ffn"]),
        norm_ple_r=rowv(w["norm_ple"]), norm_final_r=rowv(w["norm_final"]),
        ln_x_w_r=rowv(w["ln_x_w"]), ln_x_b_r=rowv(w["ln_x_b"]),
    )


def layer_forward(x, p, conv0, shift0, ssm0, wkv0, w, *, tm, ssd_q, prep_tt, wkv_steps):
    b, l, _ = x.shape
    n = b * l
    x2 = x.reshape(n, D_MODEL)
    z, xbc, rw, dt, dtt = in_projection(x2, w["norm_mix_r"], w["wz"], w["wx"], w["wr"], w["wdt"], tm=tm)
    y_ssd, ssm_new, conv_new = ssd_mixer(
        z.reshape(b, l, -1), xbc.reshape(b, l, -1), dt.reshape(b, l, -1), dtt, conv0, ssm0, w, q=ssd_q)
    chunked = l % WKV_CHUNK == 0
    *ops, gate, bonus, shift_new = rwkv_prep(
        rw.reshape(b, l, -1), shift0, w, tt=prep_tt, chunk=WKV_CHUNK if chunked else 0)
    if chunked:
        o, wkv_new = wkv_chunked(*ops, wkv0)
    else:
        o, wkv_new = wkv_scan(*ops, wkv0, steps=wkv_steps)
    flat = lambda t: t.reshape(n, -1)
    h = mix_out(x2, flat(y_ssd), flat(o), flat(gate), flat(bonus), w["ln_x_w_r"], w["ln_x_b_r"],
                w["woa"], w["wob"], tm=tm)
    y = ffn_ple(h, p.reshape(n, PLE_DIM), w["norm_ffn_r"], w["wg"], w["wu"], w["wd"],
                w["norm_ple_r"], w["wpg"], w["wpp"], w["norm_final_r"], tm=tm)
    return y.reshape(b, l, D_MODEL), ssm_new, conv_new, wkv_new, shift_new


def kernel(x_prompt, x_sample, state_ssm, state_conv, state_wkv, state_shift, p_prompt, p_sample, norm_mix, w_in, conv_w, conv_b, dt_bias, a_log, d_skip, ssd_norm, shift_mu, w0, w2, a0, a2, g2, k_k, k_a, r_k, ln_x_w, ln_x_b, w_out, norm_ffn, w_gate, w_up, w_down, norm_ple, w_ple_gate, w_ple_proj, norm_final):
    w = _prepare_weights(dict(
        norm_mix=norm_mix[0], w_in=w_in[0], conv_w=conv_w[0], conv_b=conv_b[0], dt_bias=dt_bias[0],
        a_log=a_log[0], d_skip=d_skip[0], ssd_norm=ssd_norm[0], shift_mu=shift_mu[0], w0=w0[0],
        w2=w2[0], a0=a0[0], a2=a2[0], g2=g2[0], k_k=k_k[0], k_a=k_a[0], r_k=r_k[0],
        ln_x_w=ln_x_w[0], ln_x_b=ln_x_b[0], w_out=w_out[0], norm_ffn=norm_ffn[0],
        w_gate=w_gate[0], w_up=w_up[0], w_down=w_down[0], norm_ple=norm_ple[0],
        w_ple_gate=w_ple_gate[0], w_ple_proj=w_ple_proj[0], norm_final=norm_final))
    bp = x_prompt.shape[0]
    zeros = lambda *s: jnp.zeros(s, F32)
    yp, s1, c1, k1, t1 = layer_forward(
        x_prompt, p_prompt[0], zeros(bp, SSD_CONV - 1, SSD_CONV_DIM), zeros(bp, 1, RWKV_PROJ),
        zeros(bp, SSD_HEADS, SSD_HEAD_DIM, SSD_STATE),
        zeros(bp, RWKV_HEADS, RWKV_HEAD_DIM, RWKV_HEAD_DIM), w,
        tm=256, ssd_q=min(SSD_CHUNK, x_prompt.shape[1]), prep_tt=min(128, x_prompt.shape[1]),
        wkv_steps=min(16, x_prompt.shape[1]))
    ys, s2, c2, k2, t2 = layer_forward(
        x_sample, p_sample[0], state_conv[0], state_shift[0], state_ssm[0], state_wkv[0], w,
        tm=256, ssd_q=x_sample.shape[1], prep_tt=128, wkv_steps=x_sample.shape[1])
    return (yp, ys, s1[None], c1[None], k1[None], t1[None], s2[None], c2[None], k2[None], t2[None])
```

```python
import functools

import jax
import jax.numpy as jnp
from jax import lax
from jax.experimental import pallas as pl
from jax.experimental.pallas import tpu as pltpu

F32 = jnp.float32
BF16 = jnp.bfloat16
HIGHEST = lax.Precision.HIGHEST

D_MODEL = 1024
SSD_WIDTH = 1024
SSD_HEADS = 16
SSD_HEAD_DIM = 64
SSD_GROUPS = 2
SSD_GROUP_WIDTH = SSD_WIDTH // SSD_GROUPS
SSD_STATE = 128
SSD_CONV = 4
SSD_CHUNK = 128
SSD_BC = SSD_GROUPS * SSD_STATE
SSD_CONV_DIM = SSD_WIDTH + 2 * SSD_BC
RWKV_WIDTH = 1024
RWKV_HEADS = 16
RWKV_HEAD_DIM = 64
DECAY_LORA = 64
AAA_LORA = 64
GATE_LORA = 128
RWKV_PROJ = 3 * RWKV_WIDTH + DECAY_LORA + AAA_LORA + GATE_LORA
D_FF = 2816
PLE_DIM = 256
NORM_EPS = 1e-6
GN_EPS = 64e-5

WKV_BATCH_BLOCK = 8
V7X_VMEM_LIMIT = 56 * 1024 * 1024
CONV_PAD = 8
SSD_SEQS_PER_STEP = 4


def _rms(x, g):
    return x * lax.rsqrt(jnp.mean(x * x, axis=-1, keepdims=True) + NORM_EPS) * g


def _sigmoid(x):
    return 1.0 / (1.0 + jnp.exp(-x))


def _silu(x):
    return x * _sigmoid(x)


def _softplus(x):
    return jnp.maximum(x, 0.0) + jnp.log(1.0 + jnp.exp(-jnp.abs(x)))


def _bdot(a, b):
    return jnp.dot(a.astype(BF16), b.astype(BF16), preferred_element_type=F32)


def _split3(t):
    hi = t.astype(BF16)
    r1 = t - hi.astype(F32)
    mid = r1.astype(BF16)
    lo = (r1 - mid.astype(F32)).astype(BF16)
    return hi, mid, lo


def _dot01(a, b, *, exact_side):
    if exact_side == "lhs":
        m = b.astype(BF16)
        return sum(jnp.dot(p, m, preferred_element_type=F32) for p in _split3(a))
    m = a.astype(BF16)
    return sum(jnp.dot(m, p, preferred_element_type=F32) for p in _split3(b))


def _const_spec(shape):
    return pl.BlockSpec(shape, lambda *_: (0,) * len(shape), pipeline_mode=pl.Buffered(1))


def _head_expand(rows):
    h = lax.broadcasted_iota(jnp.int32, (rows, SSD_WIDTH), 0)
    c = lax.broadcasted_iota(jnp.int32, (rows, SSD_WIDTH), 1)
    return (c // SSD_HEAD_DIM == h).astype(F32)


def _proj_kernel(x_ref, g_ref, wz_ref, wx_ref, wr_ref, wdt_ref, wdtt_ref,
                 z_ref, xbc_ref, rw_ref, dt_ref, dtt_ref):
    u = _rms(x_ref[...], g_ref[...]).astype(BF16)
    z_ref[...] = jnp.dot(u, wz_ref[...], preferred_element_type=F32)
    xbc_ref[...] = jnp.dot(u, wx_ref[...], preferred_element_type=F32)
    rw_ref[...] = jnp.dot(u, wr_ref[...], preferred_element_type=F32)
    dt_ref[...] = jnp.dot(u, wdt_ref[...], preferred_element_type=F32)
    dtt_ref[...] = lax.dot_general(wdtt_ref[...], u, (((1,), (1,)), ((), ())), preferred_element_type=F32)


def in_projection(x, g, wz, wx, wr, wdt, *, tm):
    n = x.shape[0]
    row = lambda w: pl.BlockSpec((tm, w), lambda i: (i, 0))
    wdtt = wdt.T
    return pl.pallas_call(
        _proj_kernel,
        grid=(n // tm,),
        in_specs=[row(D_MODEL), _const_spec((1, D_MODEL)), _const_spec(wz.shape),
                  _const_spec(wx.shape), _const_spec(wr.shape), _const_spec(wdt.shape),
                  _const_spec(wdtt.shape)],
        out_specs=[row(SSD_WIDTH), row(SSD_CONV_DIM), row(RWKV_PROJ), row(SSD_HEADS),
                   pl.BlockSpec((SSD_HEADS, tm), lambda i: (0, i))],
        out_shape=[jax.ShapeDtypeStruct((n, SSD_WIDTH), F32),
                   jax.ShapeDtypeStruct((n, SSD_CONV_DIM), F32),
                   jax.ShapeDtypeStruct((n, RWKV_PROJ), F32),
                   jax.ShapeDtypeStruct((n, SSD_HEADS), F32),
                   jax.ShapeDtypeStruct((SSD_HEADS, n), F32)],
        compiler_params=pltpu.CompilerParams(
            dimension_semantics=("parallel",), vmem_limit_bytes=V7X_VMEM_LIMIT),
        name="in_projection",
    )(x, g, wz, wx, wr, wdt, wdtt)


def _ssd_kernel(z_ref, xbc_ref, dt_ref, dtt_ref, hist_ref, h0_ref, cw_ref, cb_ref,
                dtb_ref, dtbt_ref, alog_ref, alogt_ref, dsk_ref, nrm_ref,
                y_ref, hfin_ref, cnew_ref, xfull_scr, h_scr, *, q, nseq, single_chunk):
    refs = (z_ref, xbc_ref, dt_ref, dtt_ref, hist_ref, h0_ref, cw_ref, cb_ref, dtb_ref, dtbt_ref, alog_ref,
            alogt_ref, dsk_ref, nrm_ref, y_ref, hfin_ref, cnew_ref, xfull_scr, h_scr)
    for s in range(nseq):
        _ssd_sequence(s, *refs, q=q, single_chunk=single_chunk)


def _ssd_sequence(s, z_ref, xbc_ref, dt_ref, dtt_ref, hist_ref, h0_ref, cw_ref, cb_ref,
                  dtb_ref, dtbt_ref, alog_ref, alogt_ref, dsk_ref, nrm_ref,
                  y_ref, hfin_ref, cnew_ref, xfull_scr, h_scr, *, q, single_chunk):
    c = pl.program_id(1)
    last = pl.num_programs(1) - 1
    gw = SSD_GROUP_WIDTH

    @pl.when(c == 0)
    def _():
        xfull_scr[s,CONV_PAD - 3:CONV_PAD, :] = hist_ref[s]
        if not single_chunk:
            for g in range(SSD_GROUPS):
                h_scr[s * SSD_GROUPS + g] = h0_ref[s,g * 8:(g + 1) * 8].reshape(gw, SSD_STATE).T

    @pl.when(c > 0)
    def _():
        xfull_scr[s,CONV_PAD - 3:CONV_PAD, :] = xfull_scr[s,CONV_PAD + q - 3:CONV_PAD + q, :]

    xfull_scr[s,CONV_PAD:CONV_PAD + q, :] = xbc_ref[s]

    conv = cb_ref[...]
    for j in range(SSD_CONV):
        lo = CONV_PAD - 3 + j
        conv = conv + xfull_scr[s,lo:lo + q, :] * cw_ref[j:j + 1, :]
    act = _silu(conv)
    xs = act[:, :SSD_WIDTH]

    dt = _softplus(dt_ref[s] + dtb_ref[...])
    dtt_raw = dtt_ref[...] if len(dtt_ref.shape) == 2 else dtt_ref[s]
    dtt = _softplus(dtt_raw + dtbt_ref[...])
    da = dt * -jnp.exp(alog_ref[...])
    dat = dtt * -jnp.exp(alogt_ref[...])
    row = lax.broadcasted_iota(jnp.int32, (q, q), 0)
    col = lax.broadcasted_iota(jnp.int32, (q, q), 1)
    causal = row >= col
    a_cum = _dot01(causal.astype(F32), da, exact_side="rhs")
    a_cumt = _dot01(dat, (row <= col).astype(F32), exact_side="lhs")

    expand = _head_expand(SSD_HEADS)
    a_cum_x = _dot01(a_cum, expand, exact_side="lhs")
    dt_x = _dot01(dt, expand, exact_side="lhs")
    a_end_x = a_cum_x[q - 1:q, :]
    decay_in_x = jnp.exp(a_cum_x)
    chunk_decay_x = jnp.exp(a_end_x)
    xd = xs * (jnp.exp(a_end_x - a_cum_x) * dt_x)

    ys = []
    for g in range(SSD_GROUPS):
        bm = act[:, SSD_WIDTH + g * SSD_STATE:SSD_WIDTH + (g + 1) * SSD_STATE]
        cm = act[:, SSD_WIDTH + SSD_BC + g * SSD_STATE:SSD_WIDTH + SSD_BC + (g + 1) * SSD_STATE]
        cb = lax.dot_general(cm.astype(BF16), bm.astype(BF16), (((1,), (1,)), ((), ())),
                             preferred_element_type=F32)
        y_heads = []
        for e in range(8):
            h = g * 8 + e
            seg = a_cum[:, h:h + 1] - a_cumt[h:h + 1, :]
            lmat = jnp.where(causal, jnp.exp(jnp.where(causal, seg, 0.0)), 0.0)
            w_qs = cb * lmat * dtt[h:h + 1, :]
            y_heads.append(_bdot(w_qs, xs[:, h * SSD_HEAD_DIM:(h + 1) * SSD_HEAD_DIM]))
        y_diag = jnp.concatenate(y_heads, axis=1)
        sl = slice(g * gw, (g + 1) * gw)
        if single_chunk:
            h_in = h0_ref[s,g * 8:(g + 1) * 8].reshape(gw, SSD_STATE)
            y_off = lax.dot_general(cm.astype(BF16), h_in.astype(BF16), (((1,), (1,)), ((), ())),
                                    preferred_element_type=F32)
            upd = lax.dot_general(xd[:, sl].astype(BF16), bm.astype(BF16), (((0,), (0,)), ((), ())),
                                  preferred_element_type=F32)
            head_decay = jnp.broadcast_to(jnp.exp(a_cumt[:, q - 1:q]), (SSD_HEADS, SSD_STATE))
            for e in range(8):
                h = g * 8 + e
                hfin_ref[s,h] = (h0_ref[s,h] * head_decay[h:h + 1, :]
                                  + upd[e * SSD_HEAD_DIM:(e + 1) * SSD_HEAD_DIM, :])
        else:
            h_in = h_scr[s * SSD_GROUPS + g]
            y_off = _bdot(cm, h_in)
            upd = lax.dot_general(bm.astype(BF16), xd[:, sl].astype(BF16), (((0,), (0,)), ((), ())),
                                  preferred_element_type=F32)
            h_scr[s * SSD_GROUPS + g] = h_in * chunk_decay_x[:, sl] + upd
        ys.append(y_diag + y_off * decay_in_x[:, sl])

    y = jnp.concatenate(ys, axis=1) + dsk_ref[...] * xs
    yg = y * _silu(z_ref[s])
    outs = []
    for g in range(SSD_GROUPS):
        t = yg[:, g * gw:(g + 1) * gw]
        outs.append(t * lax.rsqrt(jnp.mean(t * t, axis=-1, keepdims=True) + NORM_EPS))
    y_ref[s] = jnp.concatenate(outs, axis=1) * nrm_ref[...]

    @pl.when(c == last)
    def _():
        cnew_ref[s] = xfull_scr[s,CONV_PAD + q - 3:CONV_PAD + q, :]
        if not single_chunk:
            for g in range(SSD_GROUPS):
                hfin_ref[s,g * 8:(g + 1) * 8] = h_scr[s * SSD_GROUPS + g].T.reshape(8, SSD_HEAD_DIM, SSD_STATE)


def ssd_mixer(z, xbc, dt, dtt_flat, conv0, ssm0, w, *, q):
    b, l, _ = z.shape
    single_chunk = l == q
    nseq = SSD_SEQS_PER_STEP if (single_chunk and b % SSD_SEQS_PER_STEP == 0) else 1
    if q % 128 == 0 and nseq == 1:
        dtt = dtt_flat
        dtt_spec = pl.BlockSpec((SSD_HEADS, q), lambda i, c: (0, i * (l // q) + c))
    else:
        dtt = jnp.swapaxes(dt, 1, 2)
        dtt_spec = pl.BlockSpec((nseq, SSD_HEADS, q), lambda i, c: (i, 0, c))
    seq = lambda wd: pl.BlockSpec((nseq, q, wd), lambda i, c: (i, c, 0))
    per_b3 = lambda s: pl.BlockSpec((nseq,) + s, lambda i, c: (i,) + (0,) * len(s))
    col = lambda t: t.reshape(-1, 1)
    rowv = lambda t: t.reshape(1, -1)
    consts = [w["conv_w"], rowv(w["conv_b"]), rowv(w["dt_bias"]), col(w["dt_bias"]),
              rowv(w["a_log"]), col(w["a_log"]),
              rowv(jnp.repeat(w["d_skip"], SSD_HEAD_DIM)), rowv(w["ssd_norm"])]
    return pl.pallas_call(
        functools.partial(_ssd_kernel, q=q, nseq=nseq, single_chunk=single_chunk),
        grid=(b // nseq, l // q),
        in_specs=[seq(SSD_WIDTH), seq(SSD_CONV_DIM), seq(SSD_HEADS),
                  dtt_spec,
                  per_b3((SSD_CONV - 1, SSD_CONV_DIM)),
                  per_b3((SSD_HEADS, SSD_HEAD_DIM, SSD_STATE))]
                 + [_const_spec(t.shape) for t in consts],
        out_specs=[seq(SSD_WIDTH), per_b3((SSD_HEADS, SSD_HEAD_DIM, SSD_STATE)),
                   per_b3((SSD_CONV - 1, SSD_CONV_DIM))],
        out_shape=[jax.ShapeDtypeStruct((b, l, SSD_WIDTH), F32),
                   jax.ShapeDtypeStruct((b, SSD_HEADS, SSD_HEAD_DIM, SSD_STATE), F32),
                   jax.ShapeDtypeStruct((b, SSD_CONV - 1, SSD_CONV_DIM), F32)],
        scratch_shapes=[pltpu.VMEM((nseq, CONV_PAD + q, SSD_CONV_DIM), F32),
                        pltpu.VMEM((nseq * SSD_GROUPS, SSD_STATE, SSD_GROUP_WIDTH), F32)],
        compiler_params=pltpu.CompilerParams(
            dimension_semantics=("parallel", "arbitrary"), vmem_limit_bytes=V7X_VMEM_LIMIT),
        name="ssd_mixer",
    )(z, xbc, dt, dtt, conv0, ssm0, *consts)


def _head_sums(t):
    pair = 2 * RWKV_HEAD_DIM
    first = lax.broadcasted_iota(jnp.int32, (1, pair), 1) < RWKV_HEAD_DIM
    pieces = []
    for j in range(RWKV_HEADS // 2):
        x = t[:, j * pair:(j + 1) * pair]
        x0 = jnp.where(first, x, 0.0)
        s0 = jnp.sum(x0, axis=-1, keepdims=True)
        s1 = jnp.sum(x - x0, axis=-1, keepdims=True)
        pieces.append(jnp.where(first, s0, s1))
    return jnp.concatenate(pieces, axis=1)


def _rwkv_prep_kernel(rw_ref, sh0_ref, mu_ref, w0_ref, w2_ref, a0_ref, a2_ref, g2_ref,
                      kk_ref, ka_ref, rk_ref,
                      *rest, tt, seqs, chunk):
    outs, full_scr = rest[:-1], rest[-1]
    gate_out, bonus_out, shn_ref = outs[-3:]
    c = pl.program_id(1)
    wd = RWKV_WIDTH
    l = tt // seqs

    if seqs == 1:
        @pl.when(c == 0)
        def _():
            full_scr[0, CONV_PAD - 1:CONV_PAD, :] = sh0_ref[0]

        @pl.when(c > 0)
        def _():
            full_scr[0, CONV_PAD - 1:CONV_PAD, :] = full_scr[0, CONV_PAD + tt - 1:CONV_PAD + tt, :]

        rw = rw_ref[0]
        full_scr[0, CONV_PAD:CONV_PAD + tt, :] = rw
        prev = full_scr[0, CONV_PAD - 1:CONV_PAD - 1 + tt, :]
    else:
        full_scr[:, CONV_PAD - 1:CONV_PAD, :] = sh0_ref[...]
        full_scr[:, CONV_PAD:CONV_PAD + l, :] = rw_ref[...]
        rw = rw_ref[...].reshape(tt, RWKV_PROJ)
        prev = full_scr[:, CONV_PAD - 1:CONV_PAD - 1 + l, :].reshape(tt, RWKV_PROJ)
    u = rw + (prev - rw) * mu_ref[...]
    r = u[:, :wd]
    k = u[:, wd:2 * wd]
    v = u[:, 2 * wd:3 * wd]
    w_lo = u[:, 3 * wd:3 * wd + DECAY_LORA]
    a_lo = u[:, 3 * wd + DECAY_LORA:3 * wd + DECAY_LORA + AAA_LORA]
    g_lo = u[:, 3 * wd + DECAY_LORA + AAA_LORA:]

    w_log = -_softplus(-(w0_ref[...] + _bdot(jnp.tanh(w_lo), w2_ref[...]))) - 0.5
    lw = -jnp.exp(w_log)
    a = _sigmoid(a0_ref[...] + _bdot(a_lo, a2_ref[...]))
    gate = _bdot(_sigmoid(g_lo), g2_ref[...])

    kk = k * kk_ref[...]
    kk = kk / jnp.maximum(jnp.sqrt(_head_sums(kk * kk)), 1e-12)
    kf = k * (1.0 + (a - 1.0) * ka_ref[...])
    kb = kk * a
    bonus = _head_sums(r * kf * rk_ref[...]) * v

    blk = gate_out.shape
    gate_out[...] = gate.reshape(blk)
    bonus_out[...] = bonus.reshape(blk)
    if chunk == 0:
        for ref, val in zip(outs, (r, jnp.exp(lw), kf, v, kk, kb)):
            ref[...] = val.reshape(blk)
    else:
        kkt_out, rt_out, kfh_out, nbh_out, vb_out, kbg_out, kfg_out, gend_out = outs[:8]
        tri = (lax.broadcasted_iota(jnp.int32, (chunk, chunk), 0)
               >= lax.broadcasted_iota(jnp.int32, (chunk, chunk), 1)).astype(F32)
        for ci in range(tt // chunk):
            rs = slice(ci * chunk, (ci + 1) * chunk)
            lw_c = lw[rs]
            cum = _dot01(tri, lw_c, exact_side="rhs")
            cum_end = cum[chunk - 1:chunk, :]
            g_inv = jnp.exp(-cum)
            g_tail = jnp.exp(cum_end - cum)
            kkt_out[0, rs, :] = (kk[rs] * jnp.exp(cum - lw_c)).astype(BF16)
            rt_out[0, rs, :] = (r[rs] * jnp.exp(cum)).astype(BF16)
            kfh_out[0, rs, :] = (kf[rs] * g_tail).astype(BF16)
            nbh_out[0, rs, :] = (-kb[rs] * g_tail).astype(BF16)
            vb_out[0, rs, :] = v[rs].astype(BF16)
            kbg_out[0, rs, :] = (kb[rs] * g_inv).astype(BF16)
            kfg_out[0, rs, :] = (kf[rs] * g_inv).astype(BF16)
            gend_out[0, ci] = jnp.broadcast_to(jnp.exp(cum_end), (8, wd))

    if seqs == 1:
        @pl.when(c == pl.num_programs(1) - 1)
        def _():
            shn_ref[0] = full_scr[0, CONV_PAD + tt - 1:CONV_PAD + tt, :]
    else:
        shn_ref[...] = rw_ref[:, l - 1:l, :]


def rwkv_prep(rw, shift0, w, *, tt, chunk):
    b, l, _ = rw.shape
    seqs = max(1, tt // l)
    assert chunk == 0 or (seqs == 1 and tt % chunk == 0)
    rowv = lambda t: t.reshape(1, -1)
    consts = [rowv(w["shift_mu"]), rowv(w["w0"]), w["w2"].astype(BF16), rowv(w["a0"]),
              w["a2"].astype(BF16), w["g2"].astype(BF16), rowv(w["k_k"]), rowv(w["k_a"]),
              rowv(w["r_k"])]
    rows = tt // seqs
    grid = (b // seqs, l // rows)
    seq = lambda wd: pl.BlockSpec((seqs, rows, wd), lambda i, c: (i, c, 0))
    one = pl.BlockSpec((seqs, 1, RWKV_PROJ), lambda i, c: (i, 0, 0))
    sds = jax.ShapeDtypeStruct
    f32_seq = sds((b, l, RWKV_WIDTH), F32)
    if chunk == 0:
        op_specs = [seq(RWKV_WIDTH)] * 6
        op_shapes = [f32_seq] * 6
    else:
        per_tile = tt // chunk
        op_specs = [seq(RWKV_WIDTH)] * 7 + [
            pl.BlockSpec((1, per_tile, 8, RWKV_WIDTH), lambda i, c: (i, c, 0, 0))]
        op_shapes = [sds((b, l, RWKV_WIDTH), BF16)] * 7 + [sds((b, l // chunk, 8, RWKV_WIDTH), F32)]
    outs = pl.pallas_call(
        functools.partial(_rwkv_prep_kernel, tt=tt, seqs=seqs, chunk=chunk),
        grid=grid,
        in_specs=[seq(RWKV_PROJ), one] + [_const_spec(t.shape) for t in consts],
        out_specs=op_specs + [seq(RWKV_WIDTH)] * 2 + [one],
        out_shape=op_shapes + [f32_seq] * 2 + [sds((b, 1, RWKV_PROJ), F32)],
        scratch_shapes=[pltpu.VMEM((seqs, CONV_PAD + rows, RWKV_PROJ), F32)],
        compiler_params=pltpu.CompilerParams(
            dimension_semantics=("parallel", "arbitrary"), vmem_limit_bytes=V7X_VMEM_LIMIT),
        name="rwkv_prep",
    )(rw, shift0, *consts)
    return outs


def _wkv_kernel(r_ref, w_ref, k_ref, v_ref, kk_ref, kka_ref, s0_ref,
                o_ref, sfin_ref, s_scr, vt_scr, ot_scr, *, steps):
    c = pl.program_id(1)
    n = RWKV_HEAD_DIM
    lanes = WKV_BATCH_BLOCK * RWKV_HEADS

    @pl.when(c == 0)
    def _():
        s_scr[...] = s0_ref[...].reshape(lanes, n * n).T.reshape(n, n, lanes)

    def to_pairs(ref, t):
        return ref[:, t].reshape(lanes, n).T

    def step(t, carry):
        r_t = to_pairs(r_ref, t)
        w_t = to_pairs(w_ref, t)
        k_t = to_pairs(k_ref, t)
        kk_t = to_pairs(kk_ref, t)
        kka_t = to_pairs(kka_ref, t)
        vt_scr[...] = to_pairs(v_ref, t)

        def per_value(vi, carry2):
            s_v = s_scr[vi]
            skk = jnp.sum(s_v * kk_t, axis=0, keepdims=True)
            v_row = vt_scr[pl.ds(vi, 1), :]
            s_new = s_v * w_t - skk * kka_t + v_row * k_t
            s_scr[vi] = s_new
            ot_scr[pl.ds(vi, 1), :] = jnp.sum(s_new * r_t, axis=0, keepdims=True)
            return carry2

        lax.fori_loop(0, n, per_value, 0, unroll=4)
        o_ref[:, t] = ot_scr[...].T.reshape(WKV_BATCH_BLOCK, RWKV_HEADS, n)
        return carry

    lax.fori_loop(0, steps, step, 0)

    @pl.when(c == pl.num_programs(1) - 1)
    def _():
        sfin_ref[...] = s_scr[...].reshape(n * n, lanes).T.reshape(
            WKV_BATCH_BLOCK, RWKV_HEADS, n, n)


def wkv_scan(r, w, k, v, kk, kka, s0, *, steps):
    b, l, _ = r.shape
    h, n = RWKV_HEADS, RWKV_HEAD_DIM
    assert b % WKV_BATCH_BLOCK == 0 and l % steps == 0
    ops = [t.reshape(b, l, h, n) for t in (r, w, k, v, kk, kka)]
    seq_spec = pl.BlockSpec((WKV_BATCH_BLOCK, steps, h, n), lambda g, c: (g, c, 0, 0))
    st_spec = pl.BlockSpec((WKV_BATCH_BLOCK, h, n, n), lambda g, c: (g, 0, 0, 0))
    o, s_fin = pl.pallas_call(
        functools.partial(_wkv_kernel, steps=steps),
        grid=(b // WKV_BATCH_BLOCK, l // steps),
        in_specs=[seq_spec] * 6 + [st_spec],
        out_specs=[seq_spec, st_spec],
        out_shape=[jax.ShapeDtypeStruct((b, l, h, n), F32),
                   jax.ShapeDtypeStruct((b, h, n, n), F32)],
        scratch_shapes=[pltpu.VMEM((n, n, WKV_BATCH_BLOCK * h), F32),
                        pltpu.VMEM((n, WKV_BATCH_BLOCK * h), F32),
                        pltpu.VMEM((n, WKV_BATCH_BLOCK * h), F32)],
        compiler_params=pltpu.CompilerParams(
            dimension_semantics=("parallel", "arbitrary"), vmem_limit_bytes=V7X_VMEM_LIMIT),
        name="wkv_scan",
    )(*ops, s0)
    return o.reshape(b, l, h * n), s_fin


WKV_LANE_BATCH = 128


def _wkv_batch_lanes_kernel(r_ref, w_ref, k_ref, v_ref, kk_ref, kka_ref, s0_ref,
                            o_ref, sfin_ref, op_scr, ot_scr, *, steps):
    n = RWKV_HEAD_DIM
    nb = WKV_LANE_BATCH
    sfin_ref[...] = s0_ref[...]

    def step(t, carry):
        rows = pl.ds(t, nb, stride=steps)
        for i, ref in enumerate((r_ref, w_ref, k_ref, kk_ref, kka_ref, v_ref)):
            op_scr[i] = ref[rows, :].T
        for h2 in range(2):
            ch = slice(h2 * n, (h2 + 1) * n)

            def per_value(vi, carry2):
                s_v = sfin_ref[h2, vi]
                skk = jnp.sum(s_v * op_scr[3, ch, :], axis=0, keepdims=True)
                v_row = op_scr[5, pl.ds(h2 * n + vi, 1), :]
                s_new = s_v * op_scr[1, ch, :] - skk * op_scr[4, ch, :] + v_row * op_scr[2, ch, :]
                sfin_ref[h2, vi] = s_new
                ot_scr[pl.ds(h2 * n + vi, 1), :] = jnp.sum(s_new * op_scr[0, ch, :], axis=0, keepdims=True)
                return carry2

            lax.fori_loop(0, n, per_value, 0, unroll=4)
        o_ref[rows, :] = ot_scr[...].T
        return carry

    lax.fori_loop(0, steps, step, 0)


def wkv_scan_batch_lanes(r, w, k, v, kk, kka, s0):
    b, l, wd = r.shape
    h, n, nb = RWKV_HEADS, RWKV_HEAD_DIM, WKV_LANE_BATCH
    assert b % nb == 0
    ops = [t.reshape(b * l, wd) for t in (r, w, k, v, kk, kka)]
    s0t = jnp.transpose(s0, (1, 2, 3, 0))
    seq_spec = pl.BlockSpec((nb * l, 2 * n), lambda g, j: (g, j))
    st_spec = pl.BlockSpec((2, n, n, nb), lambda g, j: (j, 0, 0, g))
    o, s_fin = pl.pallas_call(
        functools.partial(_wkv_batch_lanes_kernel, steps=l),
        grid=(b // nb, h // 2),
        in_specs=[seq_spec] * 6 + [st_spec],
        out_specs=[seq_spec, st_spec],
        out_shape=[jax.ShapeDtypeStruct((b * l, wd), F32), jax.ShapeDtypeStruct((h, n, n, b), F32)],
        scratch_shapes=[pltpu.VMEM((6, 2 * n, nb), F32), pltpu.VMEM((2 * n, nb), F32)],
        compiler_params=pltpu.CompilerParams(
            dimension_semantics=("parallel", "parallel"), vmem_limit_bytes=V7X_VMEM_LIMIT),
        name="wkv_scan_batch_lanes",
    )(*ops, s0t)
    return o.reshape(b, l, wd), jnp.transpose(s_fin, (3, 0, 1, 2))


WKV_CHUNK = 64
WKV_PAIRS = RWKV_HEADS // 2
WKV_ROW_STRIDE = WKV_CHUNK + 8
WKV_SOLVE_ROWS = 4


def _pair_masks():
    c = WKV_CHUNK
    row = lax.broadcasted_iota(jnp.int32, (2 * c, 2 * c), 0)
    col = lax.broadcasted_iota(jnp.int32, (2 * c, 2 * c), 1)
    t, i = row % c, col % c
    keep = i <= t - jnp.where(row < c, 1, 0)
    sign = jnp.where(row >= c, jnp.where(col < c, -1.0, 1.0), 1.0)
    block_diag = row // c == col // c
    return keep, sign, block_diag


def _wkv_prepare_kernel(kkt_ref, rt_ref, kbg_ref, kfg_ref, vb_ref,
                        lo_ref, rhs0_ref, abt_ref, abs_scr, top_scr):
    c = WKV_CHUNK
    keep, sign, _ = _pair_masks()
    lane = lax.broadcasted_iota(jnp.int32, (1, 2 * c), 1)
    head0 = lane < RWKV_HEAD_DIM
    zeros = jnp.zeros((c, 2 * c), BF16)

    def per_batch(b, carry):
        for j in range(WKV_PAIRS):
            sl = slice(j * 2 * RWKV_HEAD_DIM, (j + 1) * 2 * RWKV_HEAD_DIM)
            lhs = jnp.concatenate([kkt_ref[b, :, sl], rt_ref[b, :, sl]], axis=0)
            rhs = jnp.concatenate([kbg_ref[b, :, sl], kfg_ref[b, :, sl]], axis=0)
            for h2 in range(2):
                sel = head0 if h2 == 0 else lane >= RWKV_HEAD_DIM
                a = lax.dot_general(jnp.where(sel, lhs, jnp.zeros_like(lhs)), rhs,
                                    (((1,), (1,)), ((), ())), preferred_element_type=F32)
                a = jnp.where(keep, a, 0.0) * sign
                inst = h2 * (WKV_BATCH_BLOCK * WKV_PAIRS) + b * WKV_PAIRS + j
                abs_scr[pl.ds(inst * WKV_ROW_STRIDE, c), :] = a[:c, :c]
                top_scr[j, :, h2 * 2 * c:(h2 + 1) * 2 * c] = a[:c].astype(BF16)
                lo_ref[b, :, (2 * j + h2) * 2 * c:(2 * j + h2 + 1) * 2 * c] = a[c:].astype(BF16)

        for j in range(WKV_PAIRS):
            sl = slice(j * 2 * RWKV_HEAD_DIM, (j + 1) * 2 * RWKV_HEAD_DIM)
            v = vb_ref[b, :, sl]
            v0 = jnp.where(head0, v, jnp.zeros_like(v))
            w_akf = jnp.concatenate([zeros, v0, zeros, v - v0], axis=0)
            rhs0_ref[b, :, sl] = jnp.dot(top_scr[j], w_akf, preferred_element_type=F32)
        return carry

    lax.fori_loop(0, WKV_BATCH_BLOCK, per_batch, 0)

    def to_lanes(t, carry):
        abt_ref[0, t] = abs_scr[pl.ds(t, 2 * WKV_BATCH_BLOCK * WKV_PAIRS, stride=WKV_ROW_STRIDE), :].T
        return carry

    lax.fori_loop(0, c, to_lanes, 0, unroll=8)


def _wkv_apply_kernel(kkt_ref, rt_ref, kfh_ref, nbh_ref, vb_ref, lo_ref, rhs0_ref, gend_ref, abt_ref, s0_ref,
                      o_ref, sfin_ref, x_scr, rs_scr, rt_scr):
    c = WKV_CHUNK
    n = RWKV_HEAD_DIM
    ch = pl.program_id(1)
    _, _, block_diag = _pair_masks()
    lane = lax.broadcasted_iota(jnp.int32, (1, 2 * c), 1)
    head0 = lane < n
    eye2 = (lax.broadcasted_iota(jnp.int32, (n, 2 * n), 0)
            == lax.broadcasted_iota(jnp.int32, (n, 2 * n), 1) % n).astype(F32)
    n_pair_rows = WKV_BATCH_BLOCK * WKV_PAIRS

    @pl.when(ch == 0)
    def _():
        def init(b, carry):
            for j in range(WKV_PAIRS):
                sp = s0_ref[b, 2 * j:2 * j + 2].reshape(2 * n, n)
                dup = jnp.dot(sp, eye2, precision=HIGHEST, preferred_element_type=F32)
                x_scr[b, j] = jnp.where(block_diag, dup, 0.0)
            return carry
        lax.fori_loop(0, WKV_BATCH_BLOCK, init, 0)

    def before_solve(b, carry):
        for j in range(WKV_PAIRS):
            sl = slice(j * 2 * n, (j + 1) * 2 * n)
            lhs = jnp.concatenate([kkt_ref[b, :, sl], rt_ref[b, :, sl]], axis=0)
            kx = lax.dot_general(lhs, x_scr[b, j].astype(BF16), (((1,), (1,)), ((), ())),
                                 preferred_element_type=F32)
            row0 = (b * WKV_PAIRS + j) * WKV_ROW_STRIDE
            rs_scr[pl.ds(row0, c), :] = kx[:c] + rhs0_ref[b, :, sl]
            o_ref[b, :, sl] = kx[c:]
        return carry

    lax.fori_loop(0, WKV_BATCH_BLOCK, before_solve, 0)

    def to_lanes(t, carry):
        m = rs_scr[pl.ds(t, n_pair_rows, stride=WKV_ROW_STRIDE), :].T
        rt_scr[t] = jnp.concatenate([m[:n], m[n:]], axis=1)
        return carry

    lax.fori_loop(0, c, to_lanes, 0, unroll=8)

    def solve_rows(tb, carry):
        t0 = tb * WKV_SOLVE_ROWS
        rows = range(WKV_SOLVE_ROWS)

        def sub(ib, accs):
            for di in rows:
                i = ib * WKV_SOLVE_ROWS + di
                p = rt_scr[i]
                accs = tuple(accs[r] - abt_ref[0, t0 + r, pl.ds(i, 1), :] * p for r in rows)
            return accs

        accs = list(lax.fori_loop(0, tb, sub, tuple(rt_scr[t0 + r] for r in rows)))
        for r in rows:
            for r2 in range(r):
                accs[r] = accs[r] - abt_ref[0, t0 + r, pl.ds(t0 + r2, 1), :] * accs[r2]
            rt_scr[t0 + r] = accs[r]
        return carry

    lax.fori_loop(0, c // WKV_SOLVE_ROWS, solve_rows, 0)

    def from_lanes(t, carry):
        m = rt_scr[t].T
        rs_scr[pl.ds(t, n_pair_rows, stride=WKV_ROW_STRIDE), :] = jnp.concatenate(
            [m[:n_pair_rows], m[n_pair_rows:]], axis=1)
        return carry

    lax.fori_loop(0, c, from_lanes, 0, unroll=8)

    def after_solve(b, carry):
        for j in range(WKV_PAIRS):
            sl = slice(j * 2 * n, (j + 1) * 2 * n)
            row0 = (b * WKV_PAIRS + j) * WKV_ROW_STRIDE
            p = rs_scr[pl.ds(row0, c), :]
            v = vb_ref[b, :, sl].astype(F32)
            p0, v0 = jnp.where(head0, p, 0.0), jnp.where(head0, v, 0.0)
            w_o = jnp.concatenate([p0, v0, p - p0, v - v0], axis=0).astype(BF16)
            o_ref[b, :, sl] = o_ref[b, :, sl] + jnp.dot(
                lo_ref[b, :, j * 4 * c:(j + 1) * 4 * c], w_o, preferred_element_type=F32)
            vp = jnp.concatenate([v, p], axis=0).astype(BF16)
            kb = jnp.concatenate([kfh_ref[b, :, sl], nbh_ref[b, :, sl]], axis=0)
            upd = lax.dot_general(vp, kb, (((0,), (0,)), ((), ())), preferred_element_type=F32)
            x_scr[b, j] = jnp.where(block_diag, x_scr[b, j] * gend_ref[b, 0, 0:1, sl] + upd, 0.0)
        return carry

    lax.fori_loop(0, WKV_BATCH_BLOCK, after_solve, 0)

    @pl.when(ch == pl.num_programs(1) - 1)
    def _():
        def fin(b, carry):
            for j in range(WKV_PAIRS):
                sp = lax.dot_general(x_scr[b, j], eye2, (((1,), (1,)), ((), ())),
                                     precision=HIGHEST, preferred_element_type=F32)
                sfin_ref[b, 2 * j:2 * j + 2] = sp.reshape(2, n, n)
            return carry
        lax.fori_loop(0, WKV_BATCH_BLOCK, fin, 0)


def wkv_chunked(kkt, rt, kfh, nbh, vb, kbg, kfg, gend, s0):
    b, l, wd = kkt.shape
    c = WKV_CHUNK
    assert b % WKV_BATCH_BLOCK == 0 and l % c == 0
    gb, nc = b // WKV_BATCH_BLOCK, l // c
    lanes = 2 * WKV_BATCH_BLOCK * WKV_PAIRS
    seq = lambda w_: pl.BlockSpec((WKV_BATCH_BLOCK, c, w_), lambda g, i: (g, i, 0))
    gend_spec = pl.BlockSpec((WKV_BATCH_BLOCK, 1, 8, wd), lambda g, i: (g, i, 0, 0))
    abt_spec = pl.BlockSpec((1, c, c, lanes), lambda g, i: (g * nc + i, 0, 0, 0))
    sds = jax.ShapeDtypeStruct
    lo, rhs0, abt = pl.pallas_call(
        _wkv_prepare_kernel,
        grid=(gb, nc),
        in_specs=[seq(wd)] * 5,
        out_specs=[seq(2 * wd), seq(wd), abt_spec],
        out_shape=[sds((b, l, 2 * wd), BF16), sds((b, l, wd), F32), sds((gb * nc, c, c, lanes), F32)],
        scratch_shapes=[pltpu.VMEM((lanes * WKV_ROW_STRIDE, c), F32),
                        pltpu.VMEM((WKV_PAIRS, c, 4 * c), BF16)],
        compiler_params=pltpu.CompilerParams(
            dimension_semantics=("parallel", "parallel"), vmem_limit_bytes=V7X_VMEM_LIMIT),
        name="wkv_prepare",
    )(kkt, rt, kbg, kfg, vb)
    st_spec = pl.BlockSpec((WKV_BATCH_BLOCK, RWKV_HEADS, RWKV_HEAD_DIM, RWKV_HEAD_DIM),
                           lambda g, i: (g, 0, 0, 0))
    o, s_fin = pl.pallas_call(
        _wkv_apply_kernel,
        grid=(gb, nc),
        in_specs=[seq(wd)] * 5 + [seq(2 * wd), seq(wd), gend_spec, abt_spec, st_spec],
        out_specs=[seq(wd), st_spec],
        out_shape=[sds((b, l, wd), F32), sds(s0.shape, F32)],
        scratch_shapes=[pltpu.VMEM((WKV_BATCH_BLOCK, WKV_PAIRS, 2 * RWKV_HEAD_DIM, 2 * RWKV_HEAD_DIM), F32),
                        pltpu.VMEM((WKV_BATCH_BLOCK * WKV_PAIRS * WKV_ROW_STRIDE, 2 * RWKV_HEAD_DIM), F32),
                        pltpu.VMEM((c, RWKV_HEAD_DIM, lanes), F32)],
        compiler_params=pltpu.CompilerParams(
            dimension_semantics=("parallel", "arbitrary"), vmem_limit_bytes=V7X_VMEM_LIMIT),
        name="wkv_apply",
    )(kkt, rt, kfh, nbh, vb, lo, rhs0, gend, abt, s0)
    return o, s_fin


def _mix_out_kernel(x_ref, ys_ref, o_ref, gate_ref, bonus_ref, lnw_ref, lnb_ref,
                    woa_ref, wob_ref, h_ref):
    o = o_ref[...]
    inv_n = 1.0 / RWKV_HEAD_DIM
    mu = _head_sums(o) * inv_n
    d = o - mu
    var = _head_sums(d * d) * inv_n
    on = d * lax.rsqrt(var + GN_EPS) * lnw_ref[...] + lnb_ref[...]
    y_rwkv = (on + bonus_ref[...]) * gate_ref[...]
    mix = _bdot(ys_ref[...], woa_ref[...]) + _bdot(y_rwkv, wob_ref[...])
    h_ref[...] = x_ref[...] + mix


def mix_out(x, y_ssd, o, gate, bonus, lnw, lnb, woa, wob, *, tm):
    n = x.shape[0]
    row = pl.BlockSpec((tm, D_MODEL), lambda i: (i, 0))
    consts = [lnw, lnb, woa, wob]
    return pl.pallas_call(
        _mix_out_kernel,
        grid=(n // tm,),
        in_specs=[row] * 5 + [_const_spec(t.shape) for t in consts],
        out_specs=row,
        out_shape=jax.ShapeDtypeStruct((n, D_MODEL), F32),
        compiler_params=pltpu.CompilerParams(
            dimension_semantics=("parallel",), vmem_limit_bytes=V7X_VMEM_LIMIT),
        name="mix_out",
    )(x, y_ssd, o, gate, bonus, *consts)


def _ffn_kernel(h_ref, p_ref, nf_ref, wg_ref, wu_ref, wd_ref, np_ref, wpg_ref, wpp_ref,
                nl_ref, y_ref):
    h = h_ref[...]
    hf = _rms(h, nf_ref[...]).astype(BF16)
    gate = jnp.dot(hf, wg_ref[...], preferred_element_type=F32)
    up = jnp.dot(hf, wu_ref[...], preferred_element_type=F32)
    h = h + _bdot(_silu(gate) * up, wd_ref[...])
    pg = _sigmoid(_bdot(_rms(h, np_ref[...]), wpg_ref[...]))
    h = h + pg * _bdot(p_ref[...], wpp_ref[...])
    y_ref[...] = _rms(h, nl_ref[...])


def ffn_ple(h, p, nf, wg, wu, wd, npl, wpg, wpp, nl, *, tm):
    n = h.shape[0]
    consts = [nf, wg, wu, wd, npl, wpg, wpp, nl]
    return pl.pallas_call(
        _ffn_kernel,
        grid=(n // tm,),
        in_specs=[pl.BlockSpec((tm, D_MODEL), lambda i: (i, 0)),
                  pl.BlockSpec((tm, PLE_DIM), lambda i: (i, 0))]
                 + [_const_spec(t.shape) for t in consts],
        out_specs=pl.BlockSpec((tm, D_MODEL), lambda i: (i, 0)),
        out_shape=jax.ShapeDtypeStruct((n, D_MODEL), F32),
        compiler_params=pltpu.CompilerParams(
            dimension_semantics=("parallel",), vmem_limit_bytes=V7X_VMEM_LIMIT),
        name="ffn_ple",
    )(h, p, *consts)


def _prepare_weights(w):
    c0, c1, c2 = SSD_WIDTH, SSD_WIDTH + SSD_CONV_DIM, SSD_WIDTH + SSD_CONV_DIM + SSD_HEADS
    w_in = w["w_in"]
    rowv = lambda t: t.reshape(1, -1)
    return dict(
        w,
        wz=w_in[:, :c0].astype(BF16), wx=w_in[:, c0:c1].astype(BF16),
        wdt=w_in[:, c1:c2].astype(BF16), wr=w_in[:, c2:].astype(BF16),
        woa=w["w_out"][:SSD_WIDTH].astype(BF16), wob=w["w_out"][SSD_WIDTH:].astype(BF16),
        wg=w["w_gate"].astype(BF16), wu=w["w_up"].astype(BF16), wd=w["w_down"].astype(BF16),
        wpg=w["w_ple_gate"].astype(BF16), wpp=w["w_ple_proj"].astype(BF16),
        norm_mix_r=rowv(w["norm_mix"]), norm_ffn_r=rowv(w["norm_ffn"]),
        norm_ple_r=rowv(w["norm_ple"]), norm_final_r=rowv(w["norm_final"]),
        ln_x_w_r=rowv(w["ln_x_w"]), ln_x_b_r=rowv(w["ln_x_b"]),
    )


def layer_forward(x, p, conv0, shift0, ssm0, wkv0, w, *, tm, ssd_q, prep_tt, wkv_steps):
    b, l, _ = x.shape
    n = b * l
    x2 = x.reshape(n, D_MODEL)
    z, xbc, rw, dt, dtt = in_projection(x2, w["norm_mix_r"], w["wz"], w["wx"], w["wr"], w["wdt"], tm=tm)
    y_ssd, ssm_new, conv_new = ssd_mixer(
        z.reshape(b, l, -1), xbc.reshape(b, l, -1), dt.reshape(b, l, -1), dtt, conv0, ssm0, w, q=ssd_q)
    chunked = l % WKV_CHUNK == 0
    *ops, gate, bonus, shift_new = rwkv_prep(
        rw.reshape(b, l, -1), shift0, w, tt=prep_tt, chunk=WKV_CHUNK if chunked else 0)
    if chunked:
        o, wkv_new = wkv_chunked(*ops, wkv0)
    elif b % WKV_LANE_BATCH == 0:
        o, wkv_new = wkv_scan_batch_lanes(*ops, wkv0)
    else:
        o, wkv_new = wkv_scan(*ops, wkv0, steps=wkv_steps)
    flat = lambda t: t.reshape(n, -1)
    h = mix_out(x2, flat(y_ssd), flat(o), flat(gate), flat(bonus), w["ln_x_w_r"], w["ln_x_b_r"],
                w["woa"], w["wob"], tm=tm)
    y = ffn_ple(h, p.reshape(n, PLE_DIM), w["norm_ffn_r"], w["wg"], w["wu"], w["wd"],
                w["norm_ple_r"], w["wpg"], w["wpp"], w["norm_final_r"], tm=tm)
    return y.reshape(b, l, D_MODEL), ssm_new, conv_new, wkv_new, shift_new


def kernel(x_prompt, x_sample, state_ssm, state_conv, state_wkv, state_shift, p_prompt, p_sample, norm_mix, w_in, conv_w, conv_b, dt_bias, a_log, d_skip, ssd_norm, shift_mu, w0, w2, a0, a2, g2, k_k, k_a, r_k, ln_x_w, ln_x_b, w_out, norm_ffn, w_gate, w_up, w_down, norm_ple, w_ple_gate, w_ple_proj, norm_final):
    w = _prepare_weights(dict(
        norm_mix=norm_mix[0], w_in=w_in[0], conv_w=conv_w[0], conv_b=conv_b[0], dt_bias=dt_bias[0],
        a_log=a_log[0], d_skip=d_skip[0], ssd_norm=ssd_norm[0], shift_mu=shift_mu[0], w0=w0[0],
        w2=w2[0], a0=a0[0], a2=a2[0], g2=g2[0], k_k=k_k[0], k_a=k_a[0], r_k=r_k[0],
        ln_x_w=ln_x_w[0], ln_x_b=ln_x_b[0], w_out=w_out[0], norm_ffn=norm_ffn[0],
        w_gate=w_gate[0], w_up=w_up[0], w_down=w_down[0], norm_ple=norm_ple[0],
        w_ple_gate=w_ple_gate[0], w_ple_proj=w_ple_proj[0], norm_final=norm_final))
    bp = x_prompt.shape[0]
    zeros = lambda *s: jnp.zeros(s, F32)
    yp, s1, c1, k1, t1 = layer_forward(
        x_prompt, p_prompt[0], zeros(bp, SSD_CONV - 1, SSD_CONV_DIM), zeros(bp, 1, RWKV_PROJ),
        zeros(bp, SSD_HEADS, SSD_HEAD_DIM, SSD_STATE),
        zeros(bp, RWKV_HEADS, RWKV_HEAD_DIM, RWKV_HEAD_DIM), w,
        tm=256, ssd_q=min(SSD_CHUNK, x_prompt.shape[1]), prep_tt=min(128, x_prompt.shape[1]),
        wkv_steps=min(16, x_prompt.shape[1]))
    ys, s2, c2, k2, t2 = layer_forward(
        x_sample, p_sample[0], state_conv[0], state_shift[0], state_ssm[0], state_wkv[0], w,
        tm=256, ssd_q=x_sample.shape[1], prep_tt=128, wkv_steps=x_sample.shape[1])
    return (yp, ys, s1[None], c1[None], k1[None], t1[None], s2[None], c2[None], k2[None], t2[None])
```

```python
import functools

import jax
import jax.numpy as jnp
from jax import lax
from jax.experimental import pallas as pl
from jax.experimental.pallas import tpu as pltpu

F32 = jnp.float32
BF16 = jnp.bfloat16
HIGHEST = lax.Precision.HIGHEST

D_MODEL = 1024
SSD_WIDTH = 1024
SSD_HEADS = 16
SSD_HEAD_DIM = 64
SSD_GROUPS = 2
SSD_GROUP_WIDTH = SSD_WIDTH // SSD_GROUPS
SSD_STATE = 128
SSD_CONV = 4
SSD_CHUNK = 128
SSD_BC = SSD_GROUPS * SSD_STATE
SSD_CONV_DIM = SSD_WIDTH + 2 * SSD_BC
RWKV_WIDTH = 1024
RWKV_HEADS = 16
RWKV_HEAD_DIM = 64
DECAY_LORA = 64
AAA_LORA = 64
GATE_LORA = 128
RWKV_PROJ = 3 * RWKV_WIDTH + DECAY_LORA + AAA_LORA + GATE_LORA
D_FF = 2816
PLE_DIM = 256
NORM_EPS = 1e-6
GN_EPS = 64e-5

WKV_BATCH_BLOCK = 8
V7X_VMEM_LIMIT = 56 * 1024 * 1024
CONV_PAD = 8
SSD_SEQS_PER_STEP = 4


def _rms(x, g):
    return x * lax.rsqrt(jnp.mean(x * x, axis=-1, keepdims=True) + NORM_EPS) * g


def _sigmoid(x):
    return 1.0 / (1.0 + jnp.exp(-x))


def _silu(x):
    return x * _sigmoid(x)


def _softplus(x):
    return jnp.maximum(x, 0.0) + jnp.log(1.0 + jnp.exp(-jnp.abs(x)))


def _bdot(a, b):
    return jnp.dot(a.astype(BF16), b.astype(BF16), preferred_element_type=F32)


def _split3(t):
    hi = t.astype(BF16)
    r1 = t - hi.astype(F32)
    mid = r1.astype(BF16)
    lo = (r1 - mid.astype(F32)).astype(BF16)
    return hi, mid, lo


def _dot01(a, b, *, exact_side):
    if exact_side == "lhs":
        m = b.astype(BF16)
        return sum(jnp.dot(p, m, preferred_element_type=F32) for p in _split3(a))
    m = a.astype(BF16)
    return sum(jnp.dot(m, p, preferred_element_type=F32) for p in _split3(b))


def _const_spec(shape):
    return pl.BlockSpec(shape, lambda *_: (0,) * len(shape), pipeline_mode=pl.Buffered(1))


def _head_expand(rows):
    h = lax.broadcasted_iota(jnp.int32, (rows, SSD_WIDTH), 0)
    c = lax.broadcasted_iota(jnp.int32, (rows, SSD_WIDTH), 1)
    return (c // SSD_HEAD_DIM == h).astype(F32)


def _proj_kernel(x_ref, g_ref, wz_ref, wx_ref, wr_ref, wdt_ref, wdtt_ref,
                 z_ref, xbc_ref, rw_ref, dt_ref, dtt_ref):
    u = _rms(x_ref[...], g_ref[...]).astype(BF16)
    z_ref[...] = jnp.dot(u, wz_ref[...], preferred_element_type=F32)
    xbc_ref[...] = jnp.dot(u, wx_ref[...], preferred_element_type=F32)
    rw_ref[...] = jnp.dot(u, wr_ref[...], preferred_element_type=F32)
    dt_ref[...] = jnp.dot(u, wdt_ref[...], preferred_element_type=F32)
    dtt_ref[...] = lax.dot_general(wdtt_ref[...], u, (((1,), (1,)), ((), ())), preferred_element_type=F32)


def in_projection(x, g, wz, wx, wr, wdt, *, tm):
    n = x.shape[0]
    row = lambda w: pl.BlockSpec((tm, w), lambda i: (i, 0))
    wdtt = wdt.T
    return pl.pallas_call(
        _proj_kernel,
        grid=(n // tm,),
        in_specs=[row(D_MODEL), _const_spec((1, D_MODEL)), _const_spec(wz.shape),
                  _const_spec(wx.shape), _const_spec(wr.shape), _const_spec(wdt.shape),
                  _const_spec(wdtt.shape)],
        out_specs=[row(SSD_WIDTH), row(SSD_CONV_DIM), row(RWKV_PROJ), row(SSD_HEADS),
                   pl.BlockSpec((SSD_HEADS, tm), lambda i: (0, i))],
        out_shape=[jax.ShapeDtypeStruct((n, SSD_WIDTH), F32),
                   jax.ShapeDtypeStruct((n, SSD_CONV_DIM), F32),
                   jax.ShapeDtypeStruct((n, RWKV_PROJ), F32),
                   jax.ShapeDtypeStruct((n, SSD_HEADS), F32),
                   jax.ShapeDtypeStruct((SSD_HEADS, n), F32)],
        compiler_params=pltpu.CompilerParams(
            dimension_semantics=("parallel",), vmem_limit_bytes=V7X_VMEM_LIMIT),
        name="in_projection",
    )(x, g, wz, wx, wr, wdt, wdtt)


def _ssd_kernel(z_ref, xbc_ref, dt_ref, dtt_ref, hist_ref, h0_ref, cw_ref, cb_ref,
                dtb_ref, dtbt_ref, alog_ref, alogt_ref, dsk_ref, nrm_ref,
                y_ref, hfin_ref, cnew_ref, xfull_scr, h_scr, *, q, nseq, single_chunk):
    refs = (z_ref, xbc_ref, dt_ref, dtt_ref, hist_ref, h0_ref, cw_ref, cb_ref, dtb_ref, dtbt_ref, alog_ref,
            alogt_ref, dsk_ref, nrm_ref, y_ref, hfin_ref, cnew_ref, xfull_scr, h_scr)
    for s in range(nseq):
        _ssd_sequence(s, *refs, q=q, single_chunk=single_chunk)


def _ssd_sequence(s, z_ref, xbc_ref, dt_ref, dtt_ref, hist_ref, h0_ref, cw_ref, cb_ref,
                  dtb_ref, dtbt_ref, alog_ref, alogt_ref, dsk_ref, nrm_ref,
                  y_ref, hfin_ref, cnew_ref, xfull_scr, h_scr, *, q, single_chunk):
    c = pl.program_id(1)
    last = pl.num_programs(1) - 1
    gw = SSD_GROUP_WIDTH

    @pl.when(c == 0)
    def _():
        xfull_scr[s,CONV_PAD - 3:CONV_PAD, :] = hist_ref[s]
        if not single_chunk:
            for g in range(SSD_GROUPS):
                h_scr[s * SSD_GROUPS + g] = h0_ref[s,g * 8:(g + 1) * 8].reshape(gw, SSD_STATE).T

    @pl.when(c > 0)
    def _():
        xfull_scr[s,CONV_PAD - 3:CONV_PAD, :] = xfull_scr[s,CONV_PAD + q - 3:CONV_PAD + q, :]

    xfull_scr[s,CONV_PAD:CONV_PAD + q, :] = xbc_ref[s]

    conv = cb_ref[...]
    for j in range(SSD_CONV):
        lo = CONV_PAD - 3 + j
        conv = conv + xfull_scr[s,lo:lo + q, :] * cw_ref[j:j + 1, :]
    act = _silu(conv)
    xs = act[:, :SSD_WIDTH]

    dt = _softplus(dt_ref[s] + dtb_ref[...])
    dtt_raw = dtt_ref[...] if len(dtt_ref.shape) == 2 else dtt_ref[s]
    dtt = _softplus(dtt_raw + dtbt_ref[...])
    da = dt * -jnp.exp(alog_ref[...])
    dat = dtt * -jnp.exp(alogt_ref[...])
    row = lax.broadcasted_iota(jnp.int32, (q, q), 0)
    col = lax.broadcasted_iota(jnp.int32, (q, q), 1)
    causal = row >= col
    a_cum = _dot01(causal.astype(F32), da, exact_side="rhs")
    a_cumt = _dot01(dat, (row <= col).astype(F32), exact_side="lhs")

    expand = _head_expand(SSD_HEADS)
    a_cum_x = _dot01(a_cum, expand, exact_side="lhs")
    dt_x = _dot01(dt, expand, exact_side="lhs")
    a_end_x = a_cum_x[q - 1:q, :]
    decay_in_x = jnp.exp(a_cum_x)
    chunk_decay_x = jnp.exp(a_end_x)
    xd = xs * (jnp.exp(a_end_x - a_cum_x) * dt_x)

    ys = []
    for g in range(SSD_GROUPS):
        bm = act[:, SSD_WIDTH + g * SSD_STATE:SSD_WIDTH + (g + 1) * SSD_STATE]
        cm = act[:, SSD_WIDTH + SSD_BC + g * SSD_STATE:SSD_WIDTH + SSD_BC + (g + 1) * SSD_STATE]
        cb = lax.dot_general(cm.astype(BF16), bm.astype(BF16), (((1,), (1,)), ((), ())),
                             preferred_element_type=F32)
        y_heads = []
        for e in range(8):
            h = g * 8 + e
            seg = a_cum[:, h:h + 1] - a_cumt[h:h + 1, :]
            lmat = jnp.where(causal, jnp.exp(jnp.where(causal, seg, 0.0)), 0.0)
            w_qs = cb * lmat * dtt[h:h + 1, :]
            y_heads.append(_bdot(w_qs, xs[:, h * SSD_HEAD_DIM:(h + 1) * SSD_HEAD_DIM]))
        y_diag = jnp.concatenate(y_heads, axis=1)
        sl = slice(g * gw, (g + 1) * gw)
        if single_chunk:
            h_in = h0_ref[s,g * 8:(g + 1) * 8].reshape(gw, SSD_STATE)
            y_off = lax.dot_general(cm.astype(BF16), h_in.astype(BF16), (((1,), (1,)), ((), ())),
                                    preferred_element_type=F32)
            upd = lax.dot_general(xd[:, sl].astype(BF16), bm.astype(BF16), (((0,), (0,)), ((), ())),
                                  preferred_element_type=F32)
            head_decay = jnp.broadcast_to(jnp.exp(a_cumt[:, q - 1:q]), (SSD_HEADS, SSD_STATE))
            for e in range(8):
                h = g * 8 + e
                hfin_ref[s,h] = (h0_ref[s,h] * head_decay[h:h + 1, :]
                                  + upd[e * SSD_HEAD_DIM:(e + 1) * SSD_HEAD_DIM, :])
        else:
            h_in = h_scr[s * SSD_GROUPS + g]
            y_off = _bdot(cm, h_in)
            upd = lax.dot_general(bm.astype(BF16), xd[:, sl].astype(BF16), (((0,), (0,)), ((), ())),
                                  preferred_element_type=F32)
            h_scr[s * SSD_GROUPS + g] = h_in * chunk_decay_x[:, sl] + upd
        ys.append(y_diag + y_off * decay_in_x[:, sl])

    y = jnp.concatenate(ys, axis=1) + dsk_ref[...] * xs
    yg = y * _silu(z_ref[s])
    outs = []
    for g in range(SSD_GROUPS):
        t = yg[:, g * gw:(g + 1) * gw]
        outs.append(t * lax.rsqrt(jnp.mean(t * t, axis=-1, keepdims=True) + NORM_EPS))
    y_ref[s] = jnp.concatenate(outs, axis=1) * nrm_ref[...]

    @pl.when(c == last)
    def _():
        cnew_ref[s] = xfull_scr[s,CONV_PAD + q - 3:CONV_PAD + q, :]
        if not single_chunk:
            for g in range(SSD_GROUPS):
                hfin_ref[s,g * 8:(g + 1) * 8] = h_scr[s * SSD_GROUPS + g].T.reshape(8, SSD_HEAD_DIM, SSD_STATE)


def ssd_mixer(z, xbc, dt, dtt_flat, conv0, ssm0, w, *, q):
    b, l, _ = z.shape
    single_chunk = l == q
    nseq = SSD_SEQS_PER_STEP if (single_chunk and b % SSD_SEQS_PER_STEP == 0) else 1
    if q % 128 == 0 and nseq == 1:
        dtt = dtt_flat
        dtt_spec = pl.BlockSpec((SSD_HEADS, q), lambda i, c: (0, i * (l // q) + c))
    else:
        dtt = jnp.swapaxes(dt, 1, 2)
        dtt_spec = pl.BlockSpec((nseq, SSD_HEADS, q), lambda i, c: (i, 0, c))
    seq = lambda wd: pl.BlockSpec((nseq, q, wd), lambda i, c: (i, c, 0))
    per_b3 = lambda s: pl.BlockSpec((nseq,) + s, lambda i, c: (i,) + (0,) * len(s))
    col = lambda t: t.reshape(-1, 1)
    rowv = lambda t: t.reshape(1, -1)
    consts = [w["conv_w"], rowv(w["conv_b"]), rowv(w["dt_bias"]), col(w["dt_bias"]),
              rowv(w["a_log"]), col(w["a_log"]),
              rowv(jnp.repeat(w["d_skip"], SSD_HEAD_DIM)), rowv(w["ssd_norm"])]
    return pl.pallas_call(
        functools.partial(_ssd_kernel, q=q, nseq=nseq, single_chunk=single_chunk),
        grid=(b // nseq, l // q),
        in_specs=[seq(SSD_WIDTH), seq(SSD_CONV_DIM), seq(SSD_HEADS),
                  dtt_spec,
                  per_b3((SSD_CONV - 1, SSD_CONV_DIM)),
                  per_b3((SSD_HEADS, SSD_HEAD_DIM, SSD_STATE))]
                 + [_const_spec(t.shape) for t in consts],
        out_specs=[seq(SSD_WIDTH), per_b3((SSD_HEADS, SSD_HEAD_DIM, SSD_STATE)),
                   per_b3((SSD_CONV - 1, SSD_CONV_DIM))],
        out_shape=[jax.ShapeDtypeStruct((b, l, SSD_WIDTH), F32),
                   jax.ShapeDtypeStruct((b, SSD_HEADS, SSD_HEAD_DIM, SSD_STATE), F32),
                   jax.ShapeDtypeStruct((b, SSD_CONV - 1, SSD_CONV_DIM), F32)],
        scratch_shapes=[pltpu.VMEM((nseq, CONV_PAD + q, SSD_CONV_DIM), F32),
                        pltpu.VMEM((nseq * SSD_GROUPS, SSD_STATE, SSD_GROUP_WIDTH), F32)],
        compiler_params=pltpu.CompilerParams(
            dimension_semantics=("parallel", "arbitrary"), vmem_limit_bytes=V7X_VMEM_LIMIT),
        name="ssd_mixer",
    )(z, xbc, dt, dtt, conv0, ssm0, *consts)


def _head_sums(t):
    pair = 2 * RWKV_HEAD_DIM
    first = lax.broadcasted_iota(jnp.int32, (1, pair), 1) < RWKV_HEAD_DIM
    pieces = []
    for j in range(RWKV_HEADS // 2):
        x = t[:, j * pair:(j + 1) * pair]
        x0 = jnp.where(first, x, 0.0)
        s0 = jnp.sum(x0, axis=-1, keepdims=True)
        s1 = jnp.sum(x - x0, axis=-1, keepdims=True)
        pieces.append(jnp.where(first, s0, s1))
    return jnp.concatenate(pieces, axis=1)


def _rwkv_prep_kernel(rw_ref, sh0_ref, mu_ref, w0_ref, w2_ref, a0_ref, a2_ref, g2_ref,
                      kk_ref, ka_ref, rk_ref,
                      *rest, tt, seqs, chunk):
    outs, full_scr = rest[:-1], rest[-1]
    gate_out, bonus_out, shn_ref = outs[-3:]
    c = pl.program_id(1)
    wd = RWKV_WIDTH
    l = tt // seqs

    if seqs == 1:
        @pl.when(c == 0)
        def _():
            full_scr[0, CONV_PAD - 1:CONV_PAD, :] = sh0_ref[0]

        @pl.when(c > 0)
        def _():
            full_scr[0, CONV_PAD - 1:CONV_PAD, :] = full_scr[0, CONV_PAD + tt - 1:CONV_PAD + tt, :]

        rw = rw_ref[0]
        full_scr[0, CONV_PAD:CONV_PAD + tt, :] = rw
        prev = full_scr[0, CONV_PAD - 1:CONV_PAD - 1 + tt, :]
    else:
        full_scr[:, CONV_PAD - 1:CONV_PAD, :] = sh0_ref[...]
        full_scr[:, CONV_PAD:CONV_PAD + l, :] = rw_ref[...]
        rw = rw_ref[...].reshape(tt, RWKV_PROJ)
        prev = full_scr[:, CONV_PAD - 1:CONV_PAD - 1 + l, :].reshape(tt, RWKV_PROJ)
    u = rw + (prev - rw) * mu_ref[...]
    r = u[:, :wd]
    k = u[:, wd:2 * wd]
    v = u[:, 2 * wd:3 * wd]
    w_lo = u[:, 3 * wd:3 * wd + DECAY_LORA]
    a_lo = u[:, 3 * wd + DECAY_LORA:3 * wd + DECAY_LORA + AAA_LORA]
    g_lo = u[:, 3 * wd + DECAY_LORA + AAA_LORA:]

    w_log = -_softplus(-(w0_ref[...] + _bdot(jnp.tanh(w_lo), w2_ref[...]))) - 0.5
    lw = -jnp.exp(w_log)
    a = _sigmoid(a0_ref[...] + _bdot(a_lo, a2_ref[...]))
    gate = _bdot(_sigmoid(g_lo), g2_ref[...])

    kk = k * kk_ref[...]
    kk = kk / jnp.maximum(jnp.sqrt(_head_sums(kk * kk)), 1e-12)
    kf = k * (1.0 + (a - 1.0) * ka_ref[...])
    kb = kk * a
    bonus = _head_sums(r * kf * rk_ref[...]) * v

    blk = gate_out.shape
    gate_out[...] = gate.reshape(blk)
    bonus_out[...] = bonus.reshape(blk)
    if chunk == 0:
        for ref, val in zip(outs, (r, jnp.exp(lw), kf, v, kk, kb)):
            ref[...] = val.reshape(blk)
    else:
        kkt_out, rt_out, kfh_out, nbh_out, vb_out, kbg_out, kfg_out, gend_out = outs[:8]
        tri = (lax.broadcasted_iota(jnp.int32, (chunk, chunk), 0)
               >= lax.broadcasted_iota(jnp.int32, (chunk, chunk), 1)).astype(F32)
        for ci in range(tt // chunk):
            rs = slice(ci * chunk, (ci + 1) * chunk)
            lw_c = lw[rs]
            cum = _dot01(tri, lw_c, exact_side="rhs")
            cum_end = cum[chunk - 1:chunk, :]
            g_inv = jnp.exp(-cum)
            g_tail = jnp.exp(cum_end - cum)
            kkt_out[0, rs, :] = (kk[rs] * jnp.exp(cum - lw_c)).astype(BF16)
            rt_out[0, rs, :] = (r[rs] * jnp.exp(cum)).astype(BF16)
            kfh_out[0, rs, :] = (kf[rs] * g_tail).astype(BF16)
            nbh_out[0, rs, :] = (-kb[rs] * g_tail).astype(BF16)
            vb_out[0, rs, :] = v[rs].astype(BF16)
            kbg_out[0, rs, :] = (kb[rs] * g_inv).astype(BF16)
            kfg_out[0, rs, :] = (kf[rs] * g_inv).astype(BF16)
            gend_out[0, ci] = jnp.broadcast_to(jnp.exp(cum_end), (8, wd))

    if seqs == 1:
        @pl.when(c == pl.num_programs(1) - 1)
        def _():
            shn_ref[0] = full_scr[0, CONV_PAD + tt - 1:CONV_PAD + tt, :]
    else:
        shn_ref[...] = rw_ref[:, l - 1:l, :]


def rwkv_prep(rw, shift0, w, *, tt, chunk):
    b, l, _ = rw.shape
    seqs = max(1, tt // l)
    assert chunk == 0 or (seqs == 1 and tt % chunk == 0)
    rowv = lambda t: t.reshape(1, -1)
    consts = [rowv(w["shift_mu"]), rowv(w["w0"]), w["w2"].astype(BF16), rowv(w["a0"]),
              w["a2"].astype(BF16), w["g2"].astype(BF16), rowv(w["k_k"]), rowv(w["k_a"]),
              rowv(w["r_k"])]
    rows = tt // seqs
    grid = (b // seqs, l // rows)
    seq = lambda wd: pl.BlockSpec((seqs, rows, wd), lambda i, c: (i, c, 0))
    one = pl.BlockSpec((seqs, 1, RWKV_PROJ), lambda i, c: (i, 0, 0))
    sds = jax.ShapeDtypeStruct
    f32_seq = sds((b, l, RWKV_WIDTH), F32)
    if chunk == 0:
        op_specs = [seq(RWKV_WIDTH)] * 6
        op_shapes = [f32_seq] * 6
    else:
        per_tile = tt // chunk
        op_specs = [seq(RWKV_WIDTH)] * 7 + [
            pl.BlockSpec((1, per_tile, 8, RWKV_WIDTH), lambda i, c: (i, c, 0, 0))]
        op_shapes = [sds((b, l, RWKV_WIDTH), BF16)] * 7 + [sds((b, l // chunk, 8, RWKV_WIDTH), F32)]
    outs = pl.pallas_call(
        functools.partial(_rwkv_prep_kernel, tt=tt, seqs=seqs, chunk=chunk),
        grid=grid,
        in_specs=[seq(RWKV_PROJ), one] + [_const_spec(t.shape) for t in consts],
        out_specs=op_specs + [seq(RWKV_WIDTH)] * 2 + [one],
        out_shape=op_shapes + [f32_seq] * 2 + [sds((b, 1, RWKV_PROJ), F32)],
        scratch_shapes=[pltpu.VMEM((seqs, CONV_PAD + rows, RWKV_PROJ), F32)],
        compiler_params=pltpu.CompilerParams(
            dimension_semantics=("parallel", "arbitrary"), vmem_limit_bytes=V7X_VMEM_LIMIT),
        name="rwkv_prep",
    )(rw, shift0, *consts)
    return outs


def _wkv_kernel(r_ref, w_ref, k_ref, v_ref, kk_ref, kka_ref, s0_ref,
                o_ref, sfin_ref, s_scr, vt_scr, ot_scr, *, steps):
    c = pl.program_id(1)
    n = RWKV_HEAD_DIM
    lanes = WKV_BATCH_BLOCK * RWKV_HEADS

    @pl.when(c == 0)
    def _():
        s_scr[...] = s0_ref[...].reshape(lanes, n * n).T.reshape(n, n, lanes)

    def to_pairs(ref, t):
        return ref[:, t].reshape(lanes, n).T

    def step(t, carry):
        r_t = to_pairs(r_ref, t)
        w_t = to_pairs(w_ref, t)
        k_t = to_pairs(k_ref, t)
        kk_t = to_pairs(kk_ref, t)
        kka_t = to_pairs(kka_ref, t)
        vt_scr[...] = to_pairs(v_ref, t)

        def per_value(vi, carry2):
            s_v = s_scr[vi]
            skk = jnp.sum(s_v * kk_t, axis=0, keepdims=True)
            v_row = vt_scr[pl.ds(vi, 1), :]
            s_new = s_v * w_t - skk * kka_t + v_row * k_t
            s_scr[vi] = s_new
            ot_scr[pl.ds(vi, 1), :] = jnp.sum(s_new * r_t, axis=0, keepdims=True)
            return carry2

        lax.fori_loop(0, n, per_value, 0, unroll=4)
        o_ref[:, t] = ot_scr[...].T.reshape(WKV_BATCH_BLOCK, RWKV_HEADS, n)
        return carry

    lax.fori_loop(0, steps, step, 0)

    @pl.when(c == pl.num_programs(1) - 1)
    def _():
        sfin_ref[...] = s_scr[...].reshape(n * n, lanes).T.reshape(
            WKV_BATCH_BLOCK, RWKV_HEADS, n, n)


def wkv_scan(r, w, k, v, kk, kka, s0, *, steps):
    b, l, _ = r.shape
    h, n = RWKV_HEADS, RWKV_HEAD_DIM
    assert b % WKV_BATCH_BLOCK == 0 and l % steps == 0
    ops = [t.reshape(b, l, h, n) for t in (r, w, k, v, kk, kka)]
    seq_spec = pl.BlockSpec((WKV_BATCH_BLOCK, steps, h, n), lambda g, c: (g, c, 0, 0))
    st_spec = pl.BlockSpec((WKV_BATCH_BLOCK, h, n, n), lambda g, c: (g, 0, 0, 0))
    o, s_fin = pl.pallas_call(
        functools.partial(_wkv_kernel, steps=steps),
        grid=(b // WKV_BATCH_BLOCK, l // steps),
        in_specs=[seq_spec] * 6 + [st_spec],
        out_specs=[seq_spec, st_spec],
        out_shape=[jax.ShapeDtypeStruct((b, l, h, n), F32),
                   jax.ShapeDtypeStruct((b, h, n, n), F32)],
        scratch_shapes=[pltpu.VMEM((n, n, WKV_BATCH_BLOCK * h), F32),
                        pltpu.VMEM((n, WKV_BATCH_BLOCK * h), F32),
                        pltpu.VMEM((n, WKV_BATCH_BLOCK * h), F32)],
        compiler_params=pltpu.CompilerParams(
            dimension_semantics=("parallel", "arbitrary"), vmem_limit_bytes=V7X_VMEM_LIMIT),
        name="wkv_scan",
    )(*ops, s0)
    return o.reshape(b, l, h * n), s_fin


WKV_LANE_BATCH = 128


def _wkv_batch_lanes_kernel(r_ref, w_ref, k_ref, v_ref, kk_ref, kka_ref, s0_ref,
                            o_ref, sfin_ref, op_scr, ot_scr, *, steps):
    n = RWKV_HEAD_DIM
    nb = WKV_LANE_BATCH
    sfin_ref[...] = s0_ref[...]

    def step(t, carry):
        rows = pl.ds(t, nb, stride=steps)
        for i, ref in enumerate((r_ref, w_ref, k_ref, kk_ref, kka_ref, v_ref)):
            op_scr[i] = ref[rows, :].T
        for h2 in range(2):
            ch = slice(h2 * n, (h2 + 1) * n)

            def per_value(vi, carry2):
                s_v = sfin_ref[h2, vi]
                skk = jnp.sum(s_v * op_scr[3, ch, :], axis=0, keepdims=True)
                v_row = op_scr[5, pl.ds(h2 * n + vi, 1), :]
                s_new = s_v * op_scr[1, ch, :] - skk * op_scr[4, ch, :] + v_row * op_scr[2, ch, :]
                sfin_ref[h2, vi] = s_new
                ot_scr[pl.ds(h2 * n + vi, 1), :] = jnp.sum(s_new * op_scr[0, ch, :], axis=0, keepdims=True)
                return carry2

            lax.fori_loop(0, n, per_value, 0, unroll=4)
        o_ref[rows, :] = ot_scr[...].T
        return carry

    lax.fori_loop(0, steps, step, 0)


def wkv_scan_batch_lanes(r, w, k, v, kk, kka, s0):
    b, l, wd = r.shape
    h, n, nb = RWKV_HEADS, RWKV_HEAD_DIM, WKV_LANE_BATCH
    assert b % nb == 0
    ops = [t.reshape(b * l, wd) for t in (r, w, k, v, kk, kka)]
    s0t = jnp.transpose(s0, (1, 2, 3, 0))
    seq_spec = pl.BlockSpec((nb * l, 2 * n), lambda g, j: (g, j))
    st_spec = pl.BlockSpec((2, n, n, nb), lambda g, j: (j, 0, 0, g))
    o, s_fin = pl.pallas_call(
        functools.partial(_wkv_batch_lanes_kernel, steps=l),
        grid=(b // nb, h // 2),
        in_specs=[seq_spec] * 6 + [st_spec],
        out_specs=[seq_spec, st_spec],
        out_shape=[jax.ShapeDtypeStruct((b * l, wd), F32), jax.ShapeDtypeStruct((h, n, n, b), F32)],
        scratch_shapes=[pltpu.VMEM((6, 2 * n, nb), F32), pltpu.VMEM((2 * n, nb), F32)],
        compiler_params=pltpu.CompilerParams(
            dimension_semantics=("parallel", "parallel"), vmem_limit_bytes=V7X_VMEM_LIMIT),
        name="wkv_scan_batch_lanes",
    )(*ops, s0t)
    return o.reshape(b, l, wd), jnp.transpose(s_fin, (3, 0, 1, 2))


WKV_CHUNK = 64
WKV_PAIRS = RWKV_HEADS // 2
WKV_ROW_STRIDE = WKV_CHUNK + 8
WKV_SOLVE_ROWS = 4
WKV_SOLVE_COLS = 16


def _pair_masks():
    c = WKV_CHUNK
    row = lax.broadcasted_iota(jnp.int32, (2 * c, 2 * c), 0)
    col = lax.broadcasted_iota(jnp.int32, (2 * c, 2 * c), 1)
    t, i = row % c, col % c
    keep = i <= t - jnp.where(row < c, 1, 0)
    sign = jnp.where(row >= c, jnp.where(col < c, -1.0, 1.0), 1.0)
    block_diag = row // c == col // c
    return keep, sign, block_diag


def _wkv_prepare_kernel(kkt_ref, rt_ref, kbg_ref, kfg_ref, vb_ref,
                        lo_ref, rhs0_ref, tp_ref, abs_scr, top_scr, abt_scr, tt_scr):
    c = WKV_CHUNK
    keep, sign, _ = _pair_masks()
    lane = lax.broadcasted_iota(jnp.int32, (1, 2 * c), 1)
    head0 = lane < RWKV_HEAD_DIM
    zeros = jnp.zeros((c, 2 * c), BF16)

    def per_batch(b, carry):
        for j in range(WKV_PAIRS):
            sl = slice(j * 2 * RWKV_HEAD_DIM, (j + 1) * 2 * RWKV_HEAD_DIM)
            lhs = jnp.concatenate([kkt_ref[b, :, sl], rt_ref[b, :, sl]], axis=0)
            rhs = jnp.concatenate([kbg_ref[b, :, sl], kfg_ref[b, :, sl]], axis=0)
            for h2 in range(2):
                sel = head0 if h2 == 0 else lane >= RWKV_HEAD_DIM
                a = lax.dot_general(jnp.where(sel, lhs, jnp.zeros_like(lhs)), rhs,
                                    (((1,), (1,)), ((), ())), preferred_element_type=F32)
                a = jnp.where(keep, a, 0.0) * sign
                inst = h2 * (WKV_BATCH_BLOCK * WKV_PAIRS) + b * WKV_PAIRS + j
                abs_scr[pl.ds(inst * WKV_ROW_STRIDE, c), :c] = a[:c, :c]
                top_scr[j, :, h2 * 2 * c:(h2 + 1) * 2 * c] = a[:c].astype(BF16)
                lo_ref[b, :, (2 * j + h2) * 2 * c:(2 * j + h2 + 1) * 2 * c] = a[c:].astype(BF16)

        for j in range(WKV_PAIRS):
            sl = slice(j * 2 * RWKV_HEAD_DIM, (j + 1) * 2 * RWKV_HEAD_DIM)
            v = vb_ref[b, :, sl]
            v0 = jnp.where(head0, v, jnp.zeros_like(v))
            w_akf = jnp.concatenate([zeros, v0, zeros, v - v0], axis=0)
            rhs0_ref[b, :, sl] = jnp.dot(top_scr[j], w_akf, preferred_element_type=F32)
        return carry

    lax.fori_loop(0, WKV_BATCH_BLOCK, per_batch, 0)

    n_inst = 2 * WKV_BATCH_BLOCK * WKV_PAIRS
    n_pair_rows = WKV_BATCH_BLOCK * WKV_PAIRS

    def to_lanes(t, carry):
        abt_scr[t] = abs_scr[pl.ds(t, n_inst, stride=WKV_ROW_STRIDE), :][:, :c].T
        return carry

    lax.fori_loop(0, c, to_lanes, 0, unroll=8)

    tt_scr[...] = jnp.zeros(tt_scr.shape, F32)
    sub_iota = lax.broadcasted_iota(jnp.int32, (WKV_SOLVE_COLS, n_inst), 0)
    rows = range(WKV_SOLVE_ROWS)
    for cb in range(c // WKV_SOLVE_COLS):
        col0 = WKV_SOLVE_COLS * cb
        cols = slice(col0, col0 + WKV_SOLVE_COLS)
        first_block = col0 // WKV_SOLVE_ROWS

        def solve_rows(tb, carry, col0=col0, cols=cols, first_block=first_block):
            t0 = tb * WKV_SOLVE_ROWS

            def sub(ib, accs):
                ps = [tt_scr[ib * WKV_SOLVE_ROWS + di, cols, :] for di in rows]
                out = []
                for r in rows:
                    terms = [abt_scr[t0 + r, pl.ds(ib * WKV_SOLVE_ROWS + di, 1), :] * ps[di] for di in rows]
                    while len(terms) > 1:
                        terms = [a + b for a, b in zip(terms[::2], terms[1::2])]
                    out.append(accs[r] - terms[0])
                return tuple(out)

            unit = tuple(jnp.where(sub_iota + col0 == t0 + r, 1.0, 0.0) for r in rows)
            accs = list(lax.fori_loop(first_block, tb, sub, unit))
            for r in rows:
                for r2 in range(r):
                    accs[r] = accs[r] - abt_scr[t0 + r, pl.ds(t0 + r2, 1), :] * accs[r2]
                tt_scr[t0 + r, cols, :] = accs[r]
            return carry

        lax.fori_loop(first_block, c // WKV_SOLVE_ROWS, solve_rows, 0)

    def from_lanes(t, carry):
        m = tt_scr[t].T
        abs_scr[pl.ds(t, n_pair_rows, stride=WKV_ROW_STRIDE), :] = jnp.concatenate(
            [m[:n_pair_rows], m[n_pair_rows:]], axis=1)
        return carry

    lax.fori_loop(0, c, from_lanes, 0, unroll=8)

    def emit(b, carry):
        for j in range(WKV_PAIRS):
            row0 = (b * WKV_PAIRS + j) * WKV_ROW_STRIDE
            tp_ref[b, :, j * 2 * c:(j + 1) * 2 * c] = abs_scr[pl.ds(row0, c), :].astype(BF16)
        return carry

    lax.fori_loop(0, WKV_BATCH_BLOCK, emit, 0)


def _wkv_apply_kernel(kkt_ref, rt_ref, kfh_ref, nbh_ref, vb_ref, tp_ref, lo_ref, rhs0_ref, gend_ref, s0_ref,
                      o_ref, sfin_ref, x_scr, wp_scr, p_scr):
    c = WKV_CHUNK
    n = RWKV_HEAD_DIM
    ch = pl.program_id(1)
    _, _, block_diag = _pair_masks()
    lane = lax.broadcasted_iota(jnp.int32, (1, 2 * c), 1)
    head0 = lane < n
    eye2 = (lax.broadcasted_iota(jnp.int32, (n, 2 * n), 0)
            == lax.broadcasted_iota(jnp.int32, (n, 2 * n), 1) % n).astype(F32)

    @pl.when(ch == 0)
    def _():
        def init(b, carry):
            for j in range(WKV_PAIRS):
                sp = s0_ref[b, 2 * j:2 * j + 2].reshape(2 * n, n)
                dup = jnp.dot(sp, eye2, precision=HIGHEST, preferred_element_type=F32)
                x_scr[b, j] = jnp.where(block_diag, dup, 0.0)
            return carry
        lax.fori_loop(0, WKV_BATCH_BLOCK, init, 0)

    def per_batch(b, carry):
        for j in range(WKV_PAIRS):
            sl = slice(j * 2 * n, (j + 1) * 2 * n)
            lhs = jnp.concatenate([kkt_ref[b, :, sl], rt_ref[b, :, sl]], axis=0)
            kx = lax.dot_general(lhs, x_scr[b, j].astype(BF16), (((1,), (1,)), ((), ())),
                                 preferred_element_type=F32)
            rhs = kx[:c] + rhs0_ref[b, :, sl]
            r0 = jnp.where(head0, rhs, 0.0)
            wp_scr[j] = jnp.concatenate([r0, rhs - r0], axis=0).astype(BF16)
            o_ref[b, :, sl] = kx[c:]
        for j in range(WKV_PAIRS):
            sl = slice(j * 2 * n, (j + 1) * 2 * n)
            p = jnp.dot(tp_ref[b, :, sl], wp_scr[j], preferred_element_type=F32)
            p_scr[j] = p.astype(BF16)
        for j in range(WKV_PAIRS):
            sl = slice(j * 2 * n, (j + 1) * 2 * n)
            v = vb_ref[b, :, sl]
            pb = p_scr[j]
            zero = jnp.zeros_like(pb)
            p0, v0 = jnp.where(head0, pb, zero), jnp.where(head0, v, zero)
            w_o = jnp.concatenate([p0, v0, pb - p0, v - v0], axis=0)
            o_ref[b, :, sl] = o_ref[b, :, sl] + jnp.dot(
                lo_ref[b, :, j * 4 * c:(j + 1) * 4 * c], w_o, preferred_element_type=F32)
            vp = jnp.concatenate([v, pb], axis=0)
            kb = jnp.concatenate([kfh_ref[b, :, sl], nbh_ref[b, :, sl]], axis=0)
            upd = lax.dot_general(vp, kb, (((0,), (0,)), ((), ())), preferred_element_type=F32)
            x_scr[b, j] = jnp.where(block_diag, x_scr[b, j] * gend_ref[b, 0, 0:1, sl] + upd, 0.0)
        return carry

    lax.fori_loop(0, WKV_BATCH_BLOCK, per_batch, 0)

    @pl.when(ch == pl.num_programs(1) - 1)
    def _():
        def fin(b, carry):
            for j in range(WKV_PAIRS):
                sp = lax.dot_general(x_scr[b, j], eye2, (((1,), (1,)), ((), ())),
                                     precision=HIGHEST, preferred_element_type=F32)
                sfin_ref[b, 2 * j:2 * j + 2] = sp.reshape(2, n, n)
            return carry
        lax.fori_loop(0, WKV_BATCH_BLOCK, fin, 0)


def wkv_chunked(kkt, rt, kfh, nbh, vb, kbg, kfg, gend, s0):
    b, l, wd = kkt.shape
    c = WKV_CHUNK
    assert b % WKV_BATCH_BLOCK == 0 and l % c == 0
    gb, nc = b // WKV_BATCH_BLOCK, l // c
    lanes = 2 * WKV_BATCH_BLOCK * WKV_PAIRS
    seq = lambda w_: pl.BlockSpec((WKV_BATCH_BLOCK, c, w_), lambda g, i: (g, i, 0))
    gend_spec = pl.BlockSpec((WKV_BATCH_BLOCK, 1, 8, wd), lambda g, i: (g, i, 0, 0))
    sds = jax.ShapeDtypeStruct
    lo, rhs0, tp = pl.pallas_call(
        _wkv_prepare_kernel,
        grid=(gb, nc),
        in_specs=[seq(wd)] * 5,
        out_specs=[seq(2 * wd), seq(wd), seq(wd)],
        out_shape=[sds((b, l, 2 * wd), BF16), sds((b, l, wd), F32), sds((b, l, wd), BF16)],
        scratch_shapes=[pltpu.VMEM((lanes * WKV_ROW_STRIDE, 2 * c), F32),
                        pltpu.VMEM((WKV_PAIRS, c, 4 * c), BF16),
                        pltpu.VMEM((c, c, lanes), F32),
                        pltpu.VMEM((c, c, lanes), F32)],
        compiler_params=pltpu.CompilerParams(
            dimension_semantics=("parallel", "parallel"), vmem_limit_bytes=V7X_VMEM_LIMIT),
        name="wkv_prepare",
    )(kkt, rt, kbg, kfg, vb)
    st_spec = pl.BlockSpec((WKV_BATCH_BLOCK, RWKV_HEADS, RWKV_HEAD_DIM, RWKV_HEAD_DIM),
                           lambda g, i: (g, 0, 0, 0))
    o, s_fin = pl.pallas_call(
        _wkv_apply_kernel,
        grid=(gb, nc),
        in_specs=[seq(wd)] * 6 + [seq(2 * wd), seq(wd), gend_spec, st_spec],
        out_specs=[seq(wd), st_spec],
        out_shape=[sds((b, l, wd), F32), sds(s0.shape, F32)],
        scratch_shapes=[pltpu.VMEM((WKV_BATCH_BLOCK, WKV_PAIRS, 2 * RWKV_HEAD_DIM, 2 * RWKV_HEAD_DIM), F32),
                        pltpu.VMEM((WKV_PAIRS, 2 * c, 2 * RWKV_HEAD_DIM), BF16),
                        pltpu.VMEM((WKV_PAIRS, c, 2 * RWKV_HEAD_DIM), BF16)],
        compiler_params=pltpu.CompilerParams(
            dimension_semantics=("parallel", "arbitrary"), vmem_limit_bytes=V7X_VMEM_LIMIT),
        name="wkv_apply",
    )(kkt, rt, kfh, nbh, vb, tp, lo, rhs0, gend, s0)
    return o, s_fin


def _mix_out_kernel(x_ref, ys_ref, o_ref, gate_ref, bonus_ref, lnw_ref, lnb_ref,
                    woa_ref, wob_ref, h_ref):
    o = o_ref[...]
    inv_n = 1.0 / RWKV_HEAD_DIM
    mu = _head_sums(o) * inv_n
    d = o - mu
    var = _head_sums(d * d) * inv_n
    on = d * lax.rsqrt(var + GN_EPS) * lnw_ref[...] + lnb_ref[...]
    y_rwkv = (on + bonus_ref[...]) * gate_ref[...]
    mix = _bdot(ys_ref[...], woa_ref[...]) + _bdot(y_rwkv, wob_ref[...])
    h_ref[...] = x_ref[...] + mix


def mix_out(x, y_ssd, o, gate, bonus, lnw, lnb, woa, wob, *, tm):
    n = x.shape[0]
    row = pl.BlockSpec((tm, D_MODEL), lambda i: (i, 0))
    consts = [lnw, lnb, woa, wob]
    return pl.pallas_call(
        _mix_out_kernel,
        grid=(n // tm,),
        in_specs=[row] * 5 + [_const_spec(t.shape) for t in consts],
        out_specs=row,
        out_shape=jax.ShapeDtypeStruct((n, D_MODEL), F32),
        compiler_params=pltpu.CompilerParams(
            dimension_semantics=("parallel",), vmem_limit_bytes=V7X_VMEM_LIMIT),
        name="mix_out",
    )(x, y_ssd, o, gate, bonus, *consts)


def _ffn_kernel(h_ref, p_ref, nf_ref, wg_ref, wu_ref, wd_ref, np_ref, wpg_ref, wpp_ref,
                nl_ref, y_ref):
    h = h_ref[...]
    hf = _rms(h, nf_ref[...]).astype(BF16)
    gate = jnp.dot(hf, wg_ref[...], preferred_element_type=F32)
    up = jnp.dot(hf, wu_ref[...], preferred_element_type=F32)
    h = h + _bdot(_silu(gate) * up, wd_ref[...])
    pg = _sigmoid(_bdot(_rms(h, np_ref[...]), wpg_ref[...]))
    h = h + pg * _bdot(p_ref[...], wpp_ref[...])
    y_ref[...] = _rms(h, nl_ref[...])


def ffn_ple(h, p, nf, wg, wu, wd, npl, wpg, wpp, nl, *, tm):
    n = h.shape[0]
    consts = [nf, wg, wu, wd, npl, wpg, wpp, nl]
    return pl.pallas_call(
        _ffn_kernel,
        grid=(n // tm,),
        in_specs=[pl.BlockSpec((tm, D_MODEL), lambda i: (i, 0)),
                  pl.BlockSpec((tm, PLE_DIM), lambda i: (i, 0))]
                 + [_const_spec(t.shape) for t in consts],
        out_specs=pl.BlockSpec((tm, D_MODEL), lambda i: (i, 0)),
        out_shape=jax.ShapeDtypeStruct((n, D_MODEL), F32),
        compiler_params=pltpu.CompilerParams(
            dimension_semantics=("parallel",), vmem_limit_bytes=V7X_VMEM_LIMIT),
        name="ffn_ple",
    )(h, p, *consts)


def _prepare_weights(w):
    c0, c1, c2 = SSD_WIDTH, SSD_WIDTH + SSD_CONV_DIM, SSD_WIDTH + SSD_CONV_DIM + SSD_HEADS
    w_in = w["w_in"]
    rowv = lambda t: t.reshape(1, -1)
    return dict(
        w,
        wz=w_in[:, :c0].astype(BF16), wx=w_in[:, c0:c1].astype(BF16),
        wdt=w_in[:, c1:c2].astype(BF16), wr=w_in[:, c2:].astype(BF16),
        woa=w["w_out"][:SSD_WIDTH].astype(BF16), wob=w["w_out"][SSD_WIDTH:].astype(BF16),
        wg=w["w_gate"].astype(BF16), wu=w["w_up"].astype(BF16), wd=w["w_down"].astype(BF16),
        wpg=w["w_ple_gate"].astype(BF16), wpp=w["w_ple_proj"].astype(BF16),
        norm_mix_r=rowv(w["norm_mix"]), norm_ffn_r=rowv(w["norm_ffn"]),
        norm_ple_r=rowv(w["norm_ple"]), norm_final_r=rowv(w["norm_final"]),
        ln_x_w_r=rowv(w["ln_x_w"]), ln_x_b_r=rowv(w["ln_x_b"]),
    )


def layer_forward(x, p, conv0, shift0, ssm0, wkv0, w, *, tm, ssd_q, prep_tt, wkv_steps):
    b, l, _ = x.shape
    n = b * l
    x2 = x.reshape(n, D_MODEL)
    z, xbc, rw, dt, dtt = in_projection(x2, w["norm_mix_r"], w["wz"], w["wx"], w["wr"], w["wdt"], tm=tm)
    y_ssd, ssm_new, conv_new = ssd_mixer(
        z.reshape(b, l, -1), xbc.reshape(b, l, -1), dt.reshape(b, l, -1), dtt, conv0, ssm0, w, q=ssd_q)
    chunked = l % WKV_CHUNK == 0
    *ops, gate, bonus, shift_new = rwkv_prep(
        rw.reshape(b, l, -1), shift0, w, tt=prep_tt, chunk=WKV_CHUNK if chunked else 0)
    if chunked:
        o, wkv_new = wkv_chunked(*ops, wkv0)
    elif b % WKV_LANE_BATCH == 0:
        o, wkv_new = wkv_scan_batch_lanes(*ops, wkv0)
    else:
        o, wkv_new = wkv_scan(*ops, wkv0, steps=wkv_steps)
    flat = lambda t: t.reshape(n, -1)
    h = mix_out(x2, flat(y_ssd), flat(o), flat(gate), flat(bonus), w["ln_x_w_r"], w["ln_x_b_r"],
                w["woa"], w["wob"], tm=tm)
    y = ffn_ple(h, p.reshape(n, PLE_DIM), w["norm_ffn_r"], w["wg"], w["wu"], w["wd"],
                w["norm_ple_r"], w["wpg"], w["wpp"], w["norm_final_r"], tm=tm)
    return y.reshape(b, l, D_MODEL), ssm_new, conv_new, wkv_new, shift_new


def kernel(x_prompt, x_sample, state_ssm, state_conv, state_wkv, state_shift, p_prompt, p_sample, norm_mix, w_in, conv_w, conv_b, dt_bias, a_log, d_skip, ssd_norm, shift_mu, w0, w2, a0, a2, g2, k_k, k_a, r_k, ln_x_w, ln_x_b, w_out, norm_ffn, w_gate, w_up, w_down, norm_ple, w_ple_gate, w_ple_proj, norm_final):
    w = _prepare_weights(dict(
        norm_mix=norm_mix[0], w_in=w_in[0], conv_w=conv_w[0], conv_b=conv_b[0], dt_bias=dt_bias[0],
        a_log=a_log[0], d_skip=d_skip[0], ssd_norm=ssd_norm[0], shift_mu=shift_mu[0], w0=w0[0],
        w2=w2[0], a0=a0[0], a2=a2[0], g2=g2[0], k_k=k_k[0], k_a=k_a[0], r_k=r_k[0],
        ln_x_w=ln_x_w[0], ln_x_b=ln_x_b[0], w_out=w_out[0], norm_ffn=norm_ffn[0],
        w_gate=w_gate[0], w_up=w_up[0], w_down=w_down[0], norm_ple=norm_ple[0],
        w_ple_gate=w_ple_gate[0], w_ple_proj=w_ple_proj[0], norm_final=norm_final))
    bp = x_prompt.shape[0]
    zeros = lambda *s: jnp.zeros(s, F32)
    yp, s1, c1, k1, t1 = layer_forward(
        x_prompt, p_prompt[0], zeros(bp, SSD_CONV - 1, SSD_CONV_DIM), zeros(bp, 1, RWKV_PROJ),
        zeros(bp, SSD_HEADS, SSD_HEAD_DIM, SSD_STATE),
        zeros(bp, RWKV_HEADS, RWKV_HEAD_DIM, RWKV_HEAD_DIM), w,
        tm=256, ssd_q=min(SSD_CHUNK, x_prompt.shape[1]), prep_tt=min(128, x_prompt.shape[1]),
        wkv_steps=min(16, x_prompt.shape[1]))
    ys, s2, c2, k2, t2 = layer_forward(
        x_sample, p_sample[0], state_conv[0], state_shift[0], state_ssm[0], state_wkv[0], w,
        tm=256, ssd_q=x_sample.shape[1], prep_tt=128, wkv_steps=x_sample.shape[1])
    return (yp, ys, s1[None], c1[None], k1[None], t1[None], s2[None], c2[None], k2[None], t2[None])
```

```python
import functools

import jax
import jax.numpy as jnp
from jax import lax
from jax.experimental import pallas as pl
from jax.experimental.pallas import tpu as pltpu

F32 = jnp.float32
BF16 = jnp.bfloat16
HIGHEST = lax.Precision.HIGHEST

D_MODEL = 1024
SSD_WIDTH = 1024
SSD_HEADS = 16
SSD_HEAD_DIM = 64
SSD_GROUPS = 2
SSD_GROUP_WIDTH = SSD_WIDTH // SSD_GROUPS
SSD_STATE = 128
SSD_CONV = 4
SSD_CHUNK = 128
SSD_BC = SSD_GROUPS * SSD_STATE
SSD_CONV_DIM = SSD_WIDTH + 2 * SSD_BC
RWKV_WIDTH = 1024
RWKV_HEADS = 16
RWKV_HEAD_DIM = 64
DECAY_LORA = 64
AAA_LORA = 64
GATE_LORA = 128
RWKV_PROJ = 3 * RWKV_WIDTH + DECAY_LORA + AAA_LORA + GATE_LORA
D_FF = 2816
PLE_DIM = 256
NORM_EPS = 1e-6
GN_EPS = 64e-5

WKV_BATCH_BLOCK = 8
V7X_VMEM_LIMIT = 56 * 1024 * 1024
CONV_PAD = 8
SSD_SEQS_PER_STEP = 4


def _rms(x, g):
    return x * lax.rsqrt(jnp.mean(x * x, axis=-1, keepdims=True) + NORM_EPS) * g


def _sigmoid(x):
    return 1.0 / (1.0 + jnp.exp(-x))


def _silu(x):
    return x * _sigmoid(x)


def _softplus(x):
    return jnp.maximum(x, 0.0) + jnp.log(1.0 + jnp.exp(-jnp.abs(x)))


def _bdot(a, b):
    return jnp.dot(a.astype(BF16), b.astype(BF16), preferred_element_type=F32)


def _split3(t):
    hi = t.astype(BF16)
    r1 = t - hi.astype(F32)
    mid = r1.astype(BF16)
    lo = (r1 - mid.astype(F32)).astype(BF16)
    return hi, mid, lo


def _dot01(a, b, *, exact_side):
    if exact_side == "lhs":
        m = b.astype(BF16)
        return sum(jnp.dot(p, m, preferred_element_type=F32) for p in _split3(a))
    m = a.astype(BF16)
    return sum(jnp.dot(m, p, preferred_element_type=F32) for p in _split3(b))


def _const_spec(shape):
    return pl.BlockSpec(shape, lambda *_: (0,) * len(shape), pipeline_mode=pl.Buffered(1))


def _head_expand(rows):
    h = lax.broadcasted_iota(jnp.int32, (rows, SSD_WIDTH), 0)
    c = lax.broadcasted_iota(jnp.int32, (rows, SSD_WIDTH), 1)
    return (c // SSD_HEAD_DIM == h).astype(F32)


def _proj_kernel(x_ref, g_ref, wz_ref, wx_ref, wr_ref, wdt_ref, wdtt_ref,
                 z_ref, xbc_ref, rw_ref, dt_ref, dtt_ref):
    u = _rms(x_ref[...], g_ref[...]).astype(BF16)
    z_ref[...] = jnp.dot(u, wz_ref[...], preferred_element_type=F32)
    xbc_ref[...] = jnp.dot(u, wx_ref[...], preferred_element_type=F32)
    rw_ref[...] = jnp.dot(u, wr_ref[...], preferred_element_type=F32)
    dt_ref[...] = jnp.dot(u, wdt_ref[...], preferred_element_type=F32)
    dtt_ref[...] = lax.dot_general(wdtt_ref[...], u, (((1,), (1,)), ((), ())), preferred_element_type=F32)


def in_projection(x, g, wz, wx, wr, wdt, *, tm):
    n = x.shape[0]
    row = lambda w: pl.BlockSpec((tm, w), lambda i: (i, 0))
    wdtt = wdt.T
    return pl.pallas_call(
        _proj_kernel,
        grid=(n // tm,),
        in_specs=[row(D_MODEL), _const_spec((1, D_MODEL)), _const_spec(wz.shape),
                  _const_spec(wx.shape), _const_spec(wr.shape), _const_spec(wdt.shape),
                  _const_spec(wdtt.shape)],
        out_specs=[row(SSD_WIDTH), row(SSD_CONV_DIM), row(RWKV_PROJ), row(SSD_HEADS),
                   pl.BlockSpec((SSD_HEADS, tm), lambda i: (0, i))],
        out_shape=[jax.ShapeDtypeStruct((n, SSD_WIDTH), F32),
                   jax.ShapeDtypeStruct((n, SSD_CONV_DIM), F32),
                   jax.ShapeDtypeStruct((n, RWKV_PROJ), F32),
                   jax.ShapeDtypeStruct((n, SSD_HEADS), F32),
                   jax.ShapeDtypeStruct((SSD_HEADS, n), F32)],
        compiler_params=pltpu.CompilerParams(
            dimension_semantics=("parallel",), vmem_limit_bytes=V7X_VMEM_LIMIT),
        name="in_projection",
    )(x, g, wz, wx, wr, wdt, wdtt)


def _ssd_kernel(z_ref, xbc_ref, dt_ref, dtt_ref, hist_ref, h0_ref, cw_ref, cb_ref,
                dtb_ref, dtbt_ref, alog_ref, alogt_ref, dsk_ref, nrm_ref,
                y_ref, hfin_ref, cnew_ref, xfull_scr, h_scr, *, q, nseq, single_chunk):
    refs = (z_ref, xbc_ref, dt_ref, dtt_ref, hist_ref, h0_ref, cw_ref, cb_ref, dtb_ref, dtbt_ref, alog_ref,
            alogt_ref, dsk_ref, nrm_ref, y_ref, hfin_ref, cnew_ref, xfull_scr, h_scr)
    for s in range(nseq):
        _ssd_sequence(s, *refs, q=q, single_chunk=single_chunk)


def _ssd_sequence(s, z_ref, xbc_ref, dt_ref, dtt_ref, hist_ref, h0_ref, cw_ref, cb_ref,
                  dtb_ref, dtbt_ref, alog_ref, alogt_ref, dsk_ref, nrm_ref,
                  y_ref, hfin_ref, cnew_ref, xfull_scr, h_scr, *, q, single_chunk):
    c = pl.program_id(1)
    last = pl.num_programs(1) - 1
    gw = SSD_GROUP_WIDTH

    @pl.when(c == 0)
    def _():
        xfull_scr[s,CONV_PAD - 3:CONV_PAD, :] = hist_ref[s]
        if not single_chunk:
            for g in range(SSD_GROUPS):
                h_scr[s * SSD_GROUPS + g] = h0_ref[s,g * 8:(g + 1) * 8].reshape(gw, SSD_STATE).T

    @pl.when(c > 0)
    def _():
        xfull_scr[s,CONV_PAD - 3:CONV_PAD, :] = xfull_scr[s,CONV_PAD + q - 3:CONV_PAD + q, :]

    xfull_scr[s,CONV_PAD:CONV_PAD + q, :] = xbc_ref[s]

    conv = cb_ref[...]
    for j in range(SSD_CONV):
        lo = CONV_PAD - 3 + j
        conv = conv + xfull_scr[s,lo:lo + q, :] * cw_ref[j:j + 1, :]
    act = _silu(conv)
    xs = act[:, :SSD_WIDTH]

    dt = _softplus(dt_ref[s] + dtb_ref[...])
    dtt_raw = dtt_ref[...] if len(dtt_ref.shape) == 2 else dtt_ref[s]
    dtt = _softplus(dtt_raw + dtbt_ref[...])
    da = dt * -jnp.exp(alog_ref[...])
    dat = dtt * -jnp.exp(alogt_ref[...])
    row = lax.broadcasted_iota(jnp.int32, (q, q), 0)
    col = lax.broadcasted_iota(jnp.int32, (q, q), 1)
    causal = row >= col
    a_cum = _dot01(causal.astype(F32), da, exact_side="rhs")
    a_cumt = _dot01(dat, (row <= col).astype(F32), exact_side="lhs")

    expand = _head_expand(SSD_HEADS)
    a_cum_x = _dot01(a_cum, expand, exact_side="lhs")
    dt_x = _dot01(dt, expand, exact_side="lhs")
    a_end_x = a_cum_x[q - 1:q, :]
    decay_in_x = jnp.exp(a_cum_x)
    chunk_decay_x = jnp.exp(a_end_x)
    xd = xs * (jnp.exp(a_end_x - a_cum_x) * dt_x)

    ys = []
    for g in range(SSD_GROUPS):
        bm = act[:, SSD_WIDTH + g * SSD_STATE:SSD_WIDTH + (g + 1) * SSD_STATE]
        cm = act[:, SSD_WIDTH + SSD_BC + g * SSD_STATE:SSD_WIDTH + SSD_BC + (g + 1) * SSD_STATE]
        cb = lax.dot_general(cm.astype(BF16), bm.astype(BF16), (((1,), (1,)), ((), ())),
                             preferred_element_type=F32)
        y_heads = []
        for e in range(8):
            h = g * 8 + e
            seg = a_cum[:, h:h + 1] - a_cumt[h:h + 1, :]
            lmat = jnp.where(causal, jnp.exp(jnp.where(causal, seg, 0.0)), 0.0)
            w_qs = cb * lmat * dtt[h:h + 1, :]
            y_heads.append(_bdot(w_qs, xs[:, h * SSD_HEAD_DIM:(h + 1) * SSD_HEAD_DIM]))
        y_diag = jnp.concatenate(y_heads, axis=1)
        sl = slice(g * gw, (g + 1) * gw)
        if single_chunk:
            h_in = h0_ref[s,g * 8:(g + 1) * 8].reshape(gw, SSD_STATE)
            y_off = lax.dot_general(cm.astype(BF16), h_in.astype(BF16), (((1,), (1,)), ((), ())),
                                    preferred_element_type=F32)
            upd = lax.dot_general(xd[:, sl].astype(BF16), bm.astype(BF16), (((0,), (0,)), ((), ())),
                                  preferred_element_type=F32)
            head_decay = jnp.broadcast_to(jnp.exp(a_cumt[:, q - 1:q]), (SSD_HEADS, SSD_STATE))
            for e in range(8):
                h = g * 8 + e
                hfin_ref[s,h] = (h0_ref[s,h] * head_decay[h:h + 1, :]
                                  + upd[e * SSD_HEAD_DIM:(e + 1) * SSD_HEAD_DIM, :])
        else:
            h_in = h_scr[s * SSD_GROUPS + g]
            y_off = _bdot(cm, h_in)
            upd = lax.dot_general(bm.astype(BF16), xd[:, sl].astype(BF16), (((0,), (0,)), ((), ())),
                                  preferred_element_type=F32)
            h_scr[s * SSD_GROUPS + g] = h_in * chunk_decay_x[:, sl] + upd
        ys.append(y_diag + y_off * decay_in_x[:, sl])

    y = jnp.concatenate(ys, axis=1) + dsk_ref[...] * xs
    yg = y * _silu(z_ref[s])
    outs = []
    for g in range(SSD_GROUPS):
        t = yg[:, g * gw:(g + 1) * gw]
        outs.append(t * lax.rsqrt(jnp.mean(t * t, axis=-1, keepdims=True) + NORM_EPS))
    y_ref[s] = jnp.concatenate(outs, axis=1) * nrm_ref[...]

    @pl.when(c == last)
    def _():
        cnew_ref[s] = xfull_scr[s,CONV_PAD + q - 3:CONV_PAD + q, :]
        if not single_chunk:
            for g in range(SSD_GROUPS):
                hfin_ref[s,g * 8:(g + 1) * 8] = h_scr[s * SSD_GROUPS + g].T.reshape(8, SSD_HEAD_DIM, SSD_STATE)


def ssd_mixer(z, xbc, dt, dtt_flat, conv0, ssm0, w, *, q):
    b, l, _ = z.shape
    single_chunk = l == q
    nseq = SSD_SEQS_PER_STEP if (single_chunk and b % SSD_SEQS_PER_STEP == 0) else 1
    if q % 128 == 0 and nseq == 1:
        dtt = dtt_flat
        dtt_spec = pl.BlockSpec((SSD_HEADS, q), lambda i, c: (0, i * (l // q) + c))
    else:
        dtt = jnp.swapaxes(dt, 1, 2)
        dtt_spec = pl.BlockSpec((nseq, SSD_HEADS, q), lambda i, c: (i, 0, c))
    seq = lambda wd: pl.BlockSpec((nseq, q, wd), lambda i, c: (i, c, 0))
    per_b3 = lambda s: pl.BlockSpec((nseq,) + s, lambda i, c: (i,) + (0,) * len(s))
    col = lambda t: t.reshape(-1, 1)
    rowv = lambda t: t.reshape(1, -1)
    consts = [w["conv_w"], rowv(w["conv_b"]), rowv(w["dt_bias"]), col(w["dt_bias"]),
              rowv(w["a_log"]), col(w["a_log"]),
              rowv(jnp.repeat(w["d_skip"], SSD_HEAD_DIM)), rowv(w["ssd_norm"])]
    return pl.pallas_call(
        functools.partial(_ssd_kernel, q=q, nseq=nseq, single_chunk=single_chunk),
        grid=(b // nseq, l // q),
        in_specs=[seq(SSD_WIDTH), seq(SSD_CONV_DIM), seq(SSD_HEADS),
                  dtt_spec,
                  per_b3((SSD_CONV - 1, SSD_CONV_DIM)),
                  per_b3((SSD_HEADS, SSD_HEAD_DIM, SSD_STATE))]
                 + [_const_spec(t.shape) for t in consts],
        out_specs=[seq(SSD_WIDTH), per_b3((SSD_HEADS, SSD_HEAD_DIM, SSD_STATE)),
                   per_b3((SSD_CONV - 1, SSD_CONV_DIM))],
        out_shape=[jax.ShapeDtypeStruct((b, l, SSD_WIDTH), F32),
                   jax.ShapeDtypeStruct((b, SSD_HEADS, SSD_HEAD_DIM, SSD_STATE), F32),
                   jax.ShapeDtypeStruct((b, SSD_CONV - 1, SSD_CONV_DIM), F32)],
        scratch_shapes=[pltpu.VMEM((nseq, CONV_PAD + q, SSD_CONV_DIM), F32),
                        pltpu.VMEM((nseq * SSD_GROUPS, SSD_STATE, SSD_GROUP_WIDTH), F32)],
        compiler_params=pltpu.CompilerParams(
            dimension_semantics=("parallel", "arbitrary"), vmem_limit_bytes=V7X_VMEM_LIMIT),
        name="ssd_mixer",
    )(z, xbc, dt, dtt, conv0, ssm0, *consts)


def _head_sums(t):
    pair = 2 * RWKV_HEAD_DIM
    first = lax.broadcasted_iota(jnp.int32, (1, pair), 1) < RWKV_HEAD_DIM
    pieces = []
    for j in range(RWKV_HEADS // 2):
        x = t[:, j * pair:(j + 1) * pair]
        x0 = jnp.where(first, x, 0.0)
        s0 = jnp.sum(x0, axis=-1, keepdims=True)
        s1 = jnp.sum(x - x0, axis=-1, keepdims=True)
        pieces.append(jnp.where(first, s0, s1))
    return jnp.concatenate(pieces, axis=1)


def _rwkv_prep_kernel(rw_ref, sh0_ref, mu_ref, w0_ref, w2_ref, a0_ref, a2_ref, g2_ref,
                      kk_ref, ka_ref, rk_ref,
                      *rest, tt, seqs, chunk):
    outs, full_scr = rest[:-1], rest[-1]
    shn_ref = outs[-1]
    c = pl.program_id(1)
    l = tt // seqs

    if seqs == 1:
        @pl.when(c == 0)
        def _():
            full_scr[0, CONV_PAD - 1:CONV_PAD, :] = sh0_ref[0]

        @pl.when(c > 0)
        def _():
            full_scr[0, CONV_PAD - 1:CONV_PAD, :] = full_scr[0, CONV_PAD + tt - 1:CONV_PAD + tt, :]

        rw = rw_ref[0]
        full_scr[0, CONV_PAD:CONV_PAD + tt, :] = rw
        prev = full_scr[0, CONV_PAD - 1:CONV_PAD - 1 + tt, :]
    else:
        full_scr[:, CONV_PAD - 1:CONV_PAD, :] = sh0_ref[...]
        full_scr[:, CONV_PAD:CONV_PAD + l, :] = rw_ref[...]
        rw = rw_ref[...].reshape(tt, RWKV_PROJ)
        prev = full_scr[:, CONV_PAD - 1:CONV_PAD - 1 + l, :].reshape(tt, RWKV_PROJ)
    vals = _rwkv_mix_math(rw, prev, mu_ref, w0_ref, w2_ref, a0_ref, a2_ref, g2_ref, kk_ref, ka_ref, rk_ref)
    _emit_rwkv_outputs(outs[:-1], vals, tt=tt, chunk=chunk)

    if seqs == 1:
        @pl.when(c == pl.num_programs(1) - 1)
        def _():
            shn_ref[0] = full_scr[0, CONV_PAD + tt - 1:CONV_PAD + tt, :]
    else:
        shn_ref[...] = rw_ref[:, l - 1:l, :]


def _rwkv_mix_math(rw, prev, mu_ref, w0_ref, w2_ref, a0_ref, a2_ref, g2_ref, kk_ref, ka_ref, rk_ref):
    wd = RWKV_WIDTH
    u = rw + (prev - rw) * mu_ref[...]
    r = u[:, :wd]
    k = u[:, wd:2 * wd]
    v = u[:, 2 * wd:3 * wd]
    w_lo = u[:, 3 * wd:3 * wd + DECAY_LORA]
    a_lo = u[:, 3 * wd + DECAY_LORA:3 * wd + DECAY_LORA + AAA_LORA]
    g_lo = u[:, 3 * wd + DECAY_LORA + AAA_LORA:]

    w_log = -_softplus(-(w0_ref[...] + _bdot(jnp.tanh(w_lo), w2_ref[...]))) - 0.5
    lw = -jnp.exp(w_log)
    a = _sigmoid(a0_ref[...] + _bdot(a_lo, a2_ref[...]))
    gate = _bdot(_sigmoid(g_lo), g2_ref[...])

    kk = k * kk_ref[...]
    kk = kk / jnp.maximum(jnp.sqrt(_head_sums(kk * kk)), 1e-12)
    kf = k * (1.0 + (a - 1.0) * ka_ref[...])
    kb = kk * a
    bonus = _head_sums(r * kf * rk_ref[...]) * v
    return r, lw, kf, v, kk, kb, gate, bonus


def _emit_rwkv_outputs(outs, vals, *, tt, chunk, row0=0):
    r, lw, kf, v, kk, kb, gate, bonus = vals
    wd = RWKV_WIDTH
    gate_out, bonus_out = outs[-2:]
    if chunk == 0:
        blk = gate_out.shape
        for ref, val in zip(outs, (r, jnp.exp(lw), kf, v, kk, kb, gate, bonus)):
            ref[...] = val.reshape(blk)
    else:
        gate_out[0, row0:row0 + tt, :] = gate
        bonus_out[0, row0:row0 + tt, :] = bonus
        kkt_out, rt_out, kfh_out, nbh_out, vb_out, kbg_out, kfg_out, gend_out = outs[:8]
        tri = (lax.broadcasted_iota(jnp.int32, (chunk, chunk), 0)
               >= lax.broadcasted_iota(jnp.int32, (chunk, chunk), 1)).astype(F32)
        for ci in range(tt // chunk):
            rs = slice(ci * chunk, (ci + 1) * chunk)
            ro = slice(row0 + ci * chunk, row0 + (ci + 1) * chunk)
            lw_c = lw[rs]
            cum = _dot01(tri, lw_c, exact_side="rhs")
            cum_end = cum[chunk - 1:chunk, :]
            g_inv = jnp.exp(-cum)
            g_tail = jnp.exp(cum_end - cum)
            kkt_out[0, ro, :] = (kk[rs] * jnp.exp(cum - lw_c)).astype(BF16)
            rt_out[0, ro, :] = (r[rs] * jnp.exp(cum)).astype(BF16)
            kfh_out[0, ro, :] = (kf[rs] * g_tail).astype(BF16)
            nbh_out[0, ro, :] = (-kb[rs] * g_tail).astype(BF16)
            vb_out[0, ro, :] = v[rs].astype(BF16)
            kbg_out[0, ro, :] = (kb[rs] * g_inv).astype(BF16)
            kfg_out[0, ro, :] = (kf[rs] * g_inv).astype(BF16)
            gend_out[0, row0 // chunk + ci] = jnp.broadcast_to(jnp.exp(cum_end), (8, wd))


def rwkv_prep(rw, shift0, w, *, tt, chunk):
    b, l, _ = rw.shape
    seqs = max(1, tt // l)
    assert chunk == 0 or (seqs == 1 and tt % chunk == 0)
    rowv = lambda t: t.reshape(1, -1)
    consts = [rowv(w["shift_mu"]), rowv(w["w0"]), w["w2"].astype(BF16), rowv(w["a0"]),
              w["a2"].astype(BF16), w["g2"].astype(BF16), rowv(w["k_k"]), rowv(w["k_a"]),
              rowv(w["r_k"])]
    rows = tt // seqs
    grid = (b // seqs, l // rows)
    seq = lambda wd: pl.BlockSpec((seqs, rows, wd), lambda i, c: (i, c, 0))
    one = pl.BlockSpec((seqs, 1, RWKV_PROJ), lambda i, c: (i, 0, 0))
    sds = jax.ShapeDtypeStruct
    f32_seq = sds((b, l, RWKV_WIDTH), F32)
    if chunk == 0:
        op_specs = [seq(RWKV_WIDTH)] * 6
        op_shapes = [f32_seq] * 6
    else:
        per_tile = tt // chunk
        op_specs = [seq(RWKV_WIDTH)] * 7 + [
            pl.BlockSpec((1, per_tile, 8, RWKV_WIDTH), lambda i, c: (i, c, 0, 0))]
        op_shapes = [sds((b, l, RWKV_WIDTH), BF16)] * 7 + [sds((b, l // chunk, 8, RWKV_WIDTH), F32)]
    outs = pl.pallas_call(
        functools.partial(_rwkv_prep_kernel, tt=tt, seqs=seqs, chunk=chunk),
        grid=grid,
        in_specs=[seq(RWKV_PROJ), one] + [_const_spec(t.shape) for t in consts],
        out_specs=op_specs + [seq(RWKV_WIDTH)] * 2 + [one],
        out_shape=op_shapes + [f32_seq] * 2 + [sds((b, 1, RWKV_PROJ), F32)],
        scratch_shapes=[pltpu.VMEM((seqs, CONV_PAD + rows, RWKV_PROJ), F32)],
        compiler_params=pltpu.CompilerParams(
            dimension_semantics=("parallel", "arbitrary"), vmem_limit_bytes=V7X_VMEM_LIMIT),
        name="rwkv_prep",
    )(rw, shift0, *consts)
    return outs


def _proj_prep_kernel(x_ref, sh0_ref, g_ref, wz_ref, wx_ref, wr_ref, wdt_ref, wdtt_ref,
                      mu_ref, w0_ref, w2_ref, a0_ref, a2_ref, g2_ref, kk_ref, ka_ref, rk_ref,
                      z_ref, xbc_ref, dt_ref, dtt_ref, *rest, tm, tiles_per_seq, chunk):
    outs, new_scr, cur_scr = rest[:-2], rest[-2], rest[-1]
    shn_ref = outs[-1]
    i = pl.program_id(0)

    @pl.when(i == 0)
    def _():
        new_scr[...] = jnp.zeros(new_scr.shape, F32)
        cur_scr[...] = jnp.zeros(cur_scr.shape, F32)

    k = jnp.maximum(i - 1, 0)
    first = (k % tiles_per_seq) == 0
    cur_scr[CONV_PAD - 1:CONV_PAD, :] = jnp.where(first, sh0_ref[0], cur_scr[CONV_PAD + tm - 1:CONV_PAD + tm, :])
    cur_scr[CONV_PAD:CONV_PAD + tm, :] = new_scr[...]

    shn_ref[0] = cur_scr[CONV_PAD + tm - 1:CONV_PAD + tm, :]
    u = _rms(x_ref[...], g_ref[...]).astype(BF16)

    def project(piece):
        if piece == 0:
            z_ref[...] = jnp.dot(u, wz_ref[...], preferred_element_type=F32)
        elif piece == 1:
            xbc_ref[...] = jnp.dot(u, wx_ref[...], preferred_element_type=F32)
        elif piece == 2:
            cols = slice(0, 2 * RWKV_WIDTH)
            new_scr[:, cols] = jnp.dot(u, wr_ref[:, cols], preferred_element_type=F32)
        else:
            cols = slice(2 * RWKV_WIDTH, RWKV_PROJ)
            new_scr[:, cols] = jnp.dot(u, wr_ref[:, cols], preferred_element_type=F32)
            dt_ref[...] = jnp.dot(u, wdt_ref[...], preferred_element_type=F32)
            dtt_ref[...] = lax.dot_general(wdtt_ref[...], u, (((1,), (1,)), ((), ())),
                                           preferred_element_type=F32)

    def prepare(part, rows):
        lo = CONV_PAD + part * rows
        rw = cur_scr[lo:lo + rows, :]
        prev = cur_scr[lo - 1:lo - 1 + rows, :]
        vals = _rwkv_mix_math(rw, prev, mu_ref, w0_ref, w2_ref, a0_ref, a2_ref, g2_ref, kk_ref, ka_ref, rk_ref)
        _emit_rwkv_outputs(outs[:-1], vals, tt=rows, chunk=chunk, row0=part * rows)

    parts = 4
    for part in range(parts):
        project(part)
        prepare(part, tm // parts)


def proj_prep(x, shift0, g, wz, wx, wr, wdt, w, *, tm, chunk):
    b, l, _ = x.shape
    n = b * l
    nt, tps = n // tm, l // tm
    assert l % tm == 0 and tm % chunk == 0
    x2 = x.reshape(n, D_MODEL)
    rowv = lambda t: t.reshape(1, -1)
    wdtt = wdt.T
    consts = [g, wz, wx, wr, wdt, wdtt,
              rowv(w["shift_mu"]), rowv(w["w0"]), w["w2"].astype(BF16), rowv(w["a0"]),
              w["a2"].astype(BF16), w["g2"].astype(BF16), rowv(w["k_k"]), rowv(w["k_a"]), rowv(w["r_k"])]
    ahead = lambda i: jnp.minimum(i, nt - 1)
    behind = lambda i: jnp.maximum(i - 1, 0)
    row_a = lambda w_: pl.BlockSpec((tm, w_), lambda i: (ahead(i), 0))
    seq_b = lambda w_: pl.BlockSpec((1, tm, w_), lambda i: (behind(i) // tps, behind(i) % tps, 0))
    one_b = pl.BlockSpec((1, 1, RWKV_PROJ), lambda i: (behind(i) // tps, 0, 0))
    per_tile = tm // chunk
    gend_spec = pl.BlockSpec((1, per_tile, 8, RWKV_WIDTH), lambda i: (behind(i) // tps, behind(i) % tps, 0, 0))
    sds = jax.ShapeDtypeStruct
    outs = pl.pallas_call(
        functools.partial(_proj_prep_kernel, tm=tm, tiles_per_seq=tps, chunk=chunk),
        grid=(nt + 1,),
        in_specs=[row_a(D_MODEL), one_b] + [_const_spec(t.shape) for t in consts],
        out_specs=[row_a(SSD_WIDTH), row_a(SSD_CONV_DIM), row_a(SSD_HEADS),
                   pl.BlockSpec((SSD_HEADS, tm), lambda i: (0, ahead(i)))]
                  + [seq_b(RWKV_WIDTH)] * 7 + [gend_spec] + [seq_b(RWKV_WIDTH)] * 2 + [one_b],
        out_shape=[sds((n, SSD_WIDTH), F32), sds((n, SSD_CONV_DIM), F32), sds((n, SSD_HEADS), F32),
                   sds((SSD_HEADS, n), F32)]
                  + [sds((b, l, RWKV_WIDTH), BF16)] * 7 + [sds((b, l // chunk, 8, RWKV_WIDTH), F32)]
                  + [sds((b, l, RWKV_WIDTH), F32)] * 2 + [sds((b, 1, RWKV_PROJ), F32)],
        scratch_shapes=[pltpu.VMEM((tm, RWKV_PROJ), F32), pltpu.VMEM((CONV_PAD + tm, RWKV_PROJ), F32)],
        compiler_params=pltpu.CompilerParams(
            dimension_semantics=("arbitrary",), vmem_limit_bytes=V7X_VMEM_LIMIT),
        name="proj_prep",
    )(x2, shift0, *consts)
    return outs


def _wkv_kernel(r_ref, w_ref, k_ref, v_ref, kk_ref, kka_ref, s0_ref,
                o_ref, sfin_ref, s_scr, vt_scr, ot_scr, *, steps):
    c = pl.program_id(1)
    n = RWKV_HEAD_DIM
    lanes = WKV_BATCH_BLOCK * RWKV_HEADS

    @pl.when(c == 0)
    def _():
        s_scr[...] = s0_ref[...].reshape(lanes, n * n).T.reshape(n, n, lanes)

    def to_pairs(ref, t):
        return ref[:, t].reshape(lanes, n).T

    def step(t, carry):
        r_t = to_pairs(r_ref, t)
        w_t = to_pairs(w_ref, t)
        k_t = to_pairs(k_ref, t)
        kk_t = to_pairs(kk_ref, t)
        kka_t = to_pairs(kka_ref, t)
        vt_scr[...] = to_pairs(v_ref, t)

        def per_value(vi, carry2):
            s_v = s_scr[vi]
            skk = jnp.sum(s_v * kk_t, axis=0, keepdims=True)
            v_row = vt_scr[pl.ds(vi, 1), :]
            s_new = s_v * w_t - skk * kka_t + v_row * k_t
            s_scr[vi] = s_new
            ot_scr[pl.ds(vi, 1), :] = jnp.sum(s_new * r_t, axis=0, keepdims=True)
            return carry2

        lax.fori_loop(0, n, per_value, 0, unroll=4)
        o_ref[:, t] = ot_scr[...].T.reshape(WKV_BATCH_BLOCK, RWKV_HEADS, n)
        return carry

    lax.fori_loop(0, steps, step, 0)

    @pl.when(c == pl.num_programs(1) - 1)
    def _():
        sfin_ref[...] = s_scr[...].reshape(n * n, lanes).T.reshape(
            WKV_BATCH_BLOCK, RWKV_HEADS, n, n)


def wkv_scan(r, w, k, v, kk, kka, s0, *, steps):
    b, l, _ = r.shape
    h, n = RWKV_HEADS, RWKV_HEAD_DIM
    assert b % WKV_BATCH_BLOCK == 0 and l % steps == 0
    ops = [t.reshape(b, l, h, n) for t in (r, w, k, v, kk, kka)]
    seq_spec = pl.BlockSpec((WKV_BATCH_BLOCK, steps, h, n), lambda g, c: (g, c, 0, 0))
    st_spec = pl.BlockSpec((WKV_BATCH_BLOCK, h, n, n), lambda g, c: (g, 0, 0, 0))
    o, s_fin = pl.pallas_call(
        functools.partial(_wkv_kernel, steps=steps),
        grid=(b // WKV_BATCH_BLOCK, l // steps),
        in_specs=[seq_spec] * 6 + [st_spec],
        out_specs=[seq_spec, st_spec],
        out_shape=[jax.ShapeDtypeStruct((b, l, h, n), F32),
                   jax.ShapeDtypeStruct((b, h, n, n), F32)],
        scratch_shapes=[pltpu.VMEM((n, n, WKV_BATCH_BLOCK * h), F32),
                        pltpu.VMEM((n, WKV_BATCH_BLOCK * h), F32),
                        pltpu.VMEM((n, WKV_BATCH_BLOCK * h), F32)],
        compiler_params=pltpu.CompilerParams(
            dimension_semantics=("parallel", "arbitrary"), vmem_limit_bytes=V7X_VMEM_LIMIT),
        name="wkv_scan",
    )(*ops, s0)
    return o.reshape(b, l, h * n), s_fin


WKV_LANE_BATCH = 128


def _wkv_batch_lanes_kernel(r_ref, w_ref, k_ref, v_ref, kk_ref, kka_ref, s0_ref,
                            o_ref, sfin_ref, op_scr, ot_scr, *, steps):
    n = RWKV_HEAD_DIM
    nb = WKV_LANE_BATCH
    sfin_ref[...] = s0_ref[...]

    def step(t, carry):
        rows = pl.ds(t, nb, stride=steps)
        for i, ref in enumerate((r_ref, w_ref, k_ref, kk_ref, kka_ref, v_ref)):
            op_scr[i] = ref[rows, :].T
        for h2 in range(2):
            ch = slice(h2 * n, (h2 + 1) * n)

            def per_value(vi, carry2):
                s_v = sfin_ref[h2, vi]
                skk = jnp.sum(s_v * op_scr[3, ch, :], axis=0, keepdims=True)
                v_row = op_scr[5, pl.ds(h2 * n + vi, 1), :]
                s_new = s_v * op_scr[1, ch, :] - skk * op_scr[4, ch, :] + v_row * op_scr[2, ch, :]
                sfin_ref[h2, vi] = s_new
                ot_scr[pl.ds(h2 * n + vi, 1), :] = jnp.sum(s_new * op_scr[0, ch, :], axis=0, keepdims=True)
                return carry2

            lax.fori_loop(0, n, per_value, 0, unroll=4)
        o_ref[rows, :] = ot_scr[...].T
        return carry

    lax.fori_loop(0, steps, step, 0)


def wkv_scan_batch_lanes(r, w, k, v, kk, kka, s0):
    b, l, wd = r.shape
    h, n, nb = RWKV_HEADS, RWKV_HEAD_DIM, WKV_LANE_BATCH
    assert b % nb == 0
    ops = [t.reshape(b * l, wd) for t in (r, w, k, v, kk, kka)]
    s0t = jnp.transpose(s0, (1, 2, 3, 0))
    seq_spec = pl.BlockSpec((nb * l, 2 * n), lambda g, j: (g, j))
    st_spec = pl.BlockSpec((2, n, n, nb), lambda g, j: (j, 0, 0, g))
    o, s_fin = pl.pallas_call(
        functools.partial(_wkv_batch_lanes_kernel, steps=l),
        grid=(b // nb, h // 2),
        in_specs=[seq_spec] * 6 + [st_spec],
        out_specs=[seq_spec, st_spec],
        out_shape=[jax.ShapeDtypeStruct((b * l, wd), F32), jax.ShapeDtypeStruct((h, n, n, b), F32)],
        scratch_shapes=[pltpu.VMEM((6, 2 * n, nb), F32), pltpu.VMEM((2 * n, nb), F32)],
        compiler_params=pltpu.CompilerParams(
            dimension_semantics=("parallel", "parallel"), vmem_limit_bytes=V7X_VMEM_LIMIT),
        name="wkv_scan_batch_lanes",
    )(*ops, s0t)
    return o.reshape(b, l, wd), jnp.transpose(s_fin, (3, 0, 1, 2))


WKV_CHUNK = 64
WKV_PAIRS = RWKV_HEADS // 2
WKV_ROW_STRIDE = WKV_CHUNK + 8
WKV_SOLVE_ROWS = 4
WKV_SOLVE_COLS = 16


def _pair_masks():
    c = WKV_CHUNK
    row = lax.broadcasted_iota(jnp.int32, (2 * c, 2 * c), 0)
    col = lax.broadcasted_iota(jnp.int32, (2 * c, 2 * c), 1)
    t, i = row % c, col % c
    keep = i <= t - jnp.where(row < c, 1, 0)
    sign = jnp.where(row >= c, jnp.where(col < c, -1.0, 1.0), 1.0)
    block_diag = row // c == col // c
    return keep, sign, block_diag


def _wkv_prepare_kernel(kkt_ref, rt_ref, kbg_ref, kfg_ref, vb_ref,
                        lo_ref, rhs0_ref, tp_ref, abs_scr, top_scr, abt_scr, tt_scr):
    c = WKV_CHUNK
    keep, sign, _ = _pair_masks()
    lane = lax.broadcasted_iota(jnp.int32, (1, 2 * c), 1)
    head0 = lane < RWKV_HEAD_DIM
    zeros = jnp.zeros((c, 2 * c), BF16)

    def per_batch(b, carry):
        for j in range(WKV_PAIRS):
            sl = slice(j * 2 * RWKV_HEAD_DIM, (j + 1) * 2 * RWKV_HEAD_DIM)
            lhs = jnp.concatenate([kkt_ref[b, :, sl], rt_ref[b, :, sl]], axis=0)
            rhs = jnp.concatenate([kbg_ref[b, :, sl], kfg_ref[b, :, sl]], axis=0)
            for h2 in range(2):
                sel = head0 if h2 == 0 else lane >= RWKV_HEAD_DIM
                a = lax.dot_general(jnp.where(sel, lhs, jnp.zeros_like(lhs)), rhs,
                                    (((1,), (1,)), ((), ())), preferred_element_type=F32)
                a = jnp.where(keep, a, 0.0) * sign
                inst = h2 * (WKV_BATCH_BLOCK * WKV_PAIRS) + b * WKV_PAIRS + j
                abs_scr[pl.ds(inst * WKV_ROW_STRIDE, c), :c] = a[:c, :c]
                top_scr[j, :, h2 * 2 * c:(h2 + 1) * 2 * c] = a[:c].astype(BF16)
                lo_ref[b, :, (2 * j + h2) * 2 * c:(2 * j + h2 + 1) * 2 * c] = a[c:].astype(BF16)

        for j in range(WKV_PAIRS):
            sl = slice(j * 2 * RWKV_HEAD_DIM, (j + 1) * 2 * RWKV_HEAD_DIM)
            v = vb_ref[b, :, sl]
            v0 = jnp.where(head0, v, jnp.zeros_like(v))
            w_akf = jnp.concatenate([zeros, v0, zeros, v - v0], axis=0)
            rhs0_ref[b, :, sl] = jnp.dot(top_scr[j], w_akf, preferred_element_type=F32)
        return carry

    lax.fori_loop(0, WKV_BATCH_BLOCK, per_batch, 0)

    n_inst = 2 * WKV_BATCH_BLOCK * WKV_PAIRS
    n_pair_rows = WKV_BATCH_BLOCK * WKV_PAIRS

    def to_lanes(t, carry):
        abt_scr[t] = abs_scr[pl.ds(t, n_inst, stride=WKV_ROW_STRIDE), :][:, :c].T
        return carry

    lax.fori_loop(0, c, to_lanes, 0, unroll=8)

    tt_scr[...] = jnp.zeros(tt_scr.shape, F32)
    sub_iota = lax.broadcasted_iota(jnp.int32, (WKV_SOLVE_COLS, n_inst), 0)
    rows = range(WKV_SOLVE_ROWS)
    for cb in range(c // WKV_SOLVE_COLS):
        col0 = WKV_SOLVE_COLS * cb
        cols = slice(col0, col0 + WKV_SOLVE_COLS)
        first_block = col0 // WKV_SOLVE_ROWS

        def solve_rows(tb, carry, col0=col0, cols=cols, first_block=first_block):
            t0 = tb * WKV_SOLVE_ROWS

            def sub(ib, accs):
                ps = [tt_scr[ib * WKV_SOLVE_ROWS + di, cols, :] for di in rows]
                out = []
                for r in rows:
                    terms = [abt_scr[t0 + r, pl.ds(ib * WKV_SOLVE_ROWS + di, 1), :] * ps[di] for di in rows]
                    while len(terms) > 1:
                        terms = [a + b for a, b in zip(terms[::2], terms[1::2])]
                    out.append(accs[r] - terms[0])
                return tuple(out)

            unit = tuple(jnp.where(sub_iota + col0 == t0 + r, 1.0, 0.0) for r in rows)
            accs = list(lax.fori_loop(first_block, tb, sub, unit))
            for r in rows:
                for r2 in range(r):
                    accs[r] = accs[r] - abt_scr[t0 + r, pl.ds(t0 + r2, 1), :] * accs[r2]
                tt_scr[t0 + r, cols, :] = accs[r]
            return carry

        lax.fori_loop(first_block, c // WKV_SOLVE_ROWS, solve_rows, 0)

    def from_lanes(t, carry):
        m = tt_scr[t].T
        abs_scr[pl.ds(t, n_pair_rows, stride=WKV_ROW_STRIDE), :] = jnp.concatenate(
            [m[:n_pair_rows], m[n_pair_rows:]], axis=1)
        return carry

    lax.fori_loop(0, c, from_lanes, 0, unroll=8)

    def emit(b, carry):
        for j in range(WKV_PAIRS):
            row0 = (b * WKV_PAIRS + j) * WKV_ROW_STRIDE
            tp_ref[b, :, j * 2 * c:(j + 1) * 2 * c] = abs_scr[pl.ds(row0, c), :].astype(BF16)
        return carry

    lax.fori_loop(0, WKV_BATCH_BLOCK, emit, 0)


def _wkv_apply_kernel(kkt_ref, rt_ref, kfh_ref, nbh_ref, vb_ref, tp_ref, lo_ref, rhs0_ref, gend_ref, s0_ref,
                      o_ref, sfin_ref, x_scr, wp_scr, p_scr):
    c = WKV_CHUNK
    n = RWKV_HEAD_DIM
    ch = pl.program_id(1)
    _, _, block_diag = _pair_masks()
    lane = lax.broadcasted_iota(jnp.int32, (1, 2 * c), 1)
    head0 = lane < n
    eye2 = (lax.broadcasted_iota(jnp.int32, (n, 2 * n), 0)
            == lax.broadcasted_iota(jnp.int32, (n, 2 * n), 1) % n).astype(F32)

    @pl.when(ch == 0)
    def _():
        def init(b, carry):
            for j in range(WKV_PAIRS):
                sp = s0_ref[b, 2 * j:2 * j + 2].reshape(2 * n, n)
                dup = jnp.dot(sp, eye2, precision=HIGHEST, preferred_element_type=F32)
                x_scr[b, j] = jnp.where(block_diag, dup, 0.0)
            return carry
        lax.fori_loop(0, WKV_BATCH_BLOCK, init, 0)

    def per_batch(b, carry):
        for j in range(WKV_PAIRS):
            sl = slice(j * 2 * n, (j + 1) * 2 * n)
            lhs = jnp.concatenate([kkt_ref[b, :, sl], rt_ref[b, :, sl]], axis=0)
            kx = lax.dot_general(lhs, x_scr[b, j].astype(BF16), (((1,), (1,)), ((), ())),
                                 preferred_element_type=F32)
            rhs = kx[:c] + rhs0_ref[b, :, sl]
            r0 = jnp.where(head0, rhs, 0.0)
            wp_scr[j] = jnp.concatenate([r0, rhs - r0], axis=0).astype(BF16)
            o_ref[b, :, sl] = kx[c:]
        for j in range(WKV_PAIRS):
            sl = slice(j * 2 * n, (j + 1) * 2 * n)
            p = jnp.dot(tp_ref[b, :, sl], wp_scr[j], preferred_element_type=F32)
            p_scr[j] = p.astype(BF16)
        for j in range(WKV_PAIRS):
            sl = slice(j * 2 * n, (j + 1) * 2 * n)
            v = vb_ref[b, :, sl]
            pb = p_scr[j]
            zero = jnp.zeros_like(pb)
            p0, v0 = jnp.where(head0, pb, zero), jnp.where(head0, v, zero)
            w_o = jnp.concatenate([p0, v0, pb - p0, v - v0], axis=0)
            o_ref[b, :, sl] = o_ref[b, :, sl] + jnp.dot(
                lo_ref[b, :, j * 4 * c:(j + 1) * 4 * c], w_o, preferred_element_type=F32)
            vp = jnp.concatenate([v, pb], axis=0)
            kb = jnp.concatenate([kfh_ref[b, :, sl], nbh_ref[b, :, sl]], axis=0)
            upd = lax.dot_general(vp, kb, (((0,), (0,)), ((), ())), preferred_element_type=F32)
            x_scr[b, j] = jnp.where(block_diag, x_scr[b, j] * gend_ref[b, 0, 0:1, sl] + upd, 0.0)
        return carry

    lax.fori_loop(0, WKV_BATCH_BLOCK, per_batch, 0)

    @pl.when(ch == pl.num_programs(1) - 1)
    def _():
        def fin(b, carry):
            for j in range(WKV_PAIRS):
                sp = lax.dot_general(x_scr[b, j], eye2, (((1,), (1,)), ((), ())),
                                     precision=HIGHEST, preferred_element_type=F32)
                sfin_ref[b, 2 * j:2 * j + 2] = sp.reshape(2, n, n)
            return carry
        lax.fori_loop(0, WKV_BATCH_BLOCK, fin, 0)


def wkv_chunked(kkt, rt, kfh, nbh, vb, kbg, kfg, gend, s0):
    b, l, wd = kkt.shape
    c = WKV_CHUNK
    assert b % WKV_BATCH_BLOCK == 0 and l % c == 0
    gb, nc = b // WKV_BATCH_BLOCK, l // c
    lanes = 2 * WKV_BATCH_BLOCK * WKV_PAIRS
    seq = lambda w_: pl.BlockSpec((WKV_BATCH_BLOCK, c, w_), lambda g, i: (g, i, 0))
    gend_spec = pl.BlockSpec((WKV_BATCH_BLOCK, 1, 8, wd), lambda g, i: (g, i, 0, 0))
    sds = jax.ShapeDtypeStruct
    lo, rhs0, tp = pl.pallas_call(
        _wkv_prepare_kernel,
        grid=(gb, nc),
        in_specs=[seq(wd)] * 5,
        out_specs=[seq(2 * wd), seq(wd), seq(wd)],
        out_shape=[sds((b, l, 2 * wd), BF16), sds((b, l, wd), F32), sds((b, l, wd), BF16)],
        scratch_shapes=[pltpu.VMEM((lanes * WKV_ROW_STRIDE, 2 * c), F32),
                        pltpu.VMEM((WKV_PAIRS, c, 4 * c), BF16),
                        pltpu.VMEM((c, c, lanes), F32),
                        pltpu.VMEM((c, c, lanes), F32)],
        compiler_params=pltpu.CompilerParams(
            dimension_semantics=("parallel", "parallel"), vmem_limit_bytes=V7X_VMEM_LIMIT),
        name="wkv_prepare",
    )(kkt, rt, kbg, kfg, vb)
    st_spec = pl.BlockSpec((WKV_BATCH_BLOCK, RWKV_HEADS, RWKV_HEAD_DIM, RWKV_HEAD_DIM),
                           lambda g, i: (g, 0, 0, 0))
    o, s_fin = pl.pallas_call(
        _wkv_apply_kernel,
        grid=(gb, nc),
        in_specs=[seq(wd)] * 6 + [seq(2 * wd), seq(wd), gend_spec, st_spec],
        out_specs=[seq(wd), st_spec],
        out_shape=[sds((b, l, wd), F32), sds(s0.shape, F32)],
        scratch_shapes=[pltpu.VMEM((WKV_BATCH_BLOCK, WKV_PAIRS, 2 * RWKV_HEAD_DIM, 2 * RWKV_HEAD_DIM), F32),
                        pltpu.VMEM((WKV_PAIRS, 2 * c, 2 * RWKV_HEAD_DIM), BF16),
                        pltpu.VMEM((WKV_PAIRS, c, 2 * RWKV_HEAD_DIM), BF16)],
        compiler_params=pltpu.CompilerParams(
            dimension_semantics=("parallel", "arbitrary"), vmem_limit_bytes=V7X_VMEM_LIMIT),
        name="wkv_apply",
    )(kkt, rt, kfh, nbh, vb, tp, lo, rhs0, gend, s0)
    return o, s_fin


def _mix_out_kernel(x_ref, ys_ref, o_ref, gate_ref, bonus_ref, lnw_ref, lnb_ref,
                    woa_ref, wob_ref, h_ref):
    o = o_ref[...]
    inv_n = 1.0 / RWKV_HEAD_DIM
    mu = _head_sums(o) * inv_n
    d = o - mu
    var = _head_sums(d * d) * inv_n
    on = d * lax.rsqrt(var + GN_EPS) * lnw_ref[...] + lnb_ref[...]
    y_rwkv = (on + bonus_ref[...]) * gate_ref[...]
    mix = _bdot(ys_ref[...], woa_ref[...]) + _bdot(y_rwkv, wob_ref[...])
    h_ref[...] = x_ref[...] + mix


def mix_out(x, y_ssd, o, gate, bonus, lnw, lnb, woa, wob, *, tm):
    n = x.shape[0]
    row = pl.BlockSpec((tm, D_MODEL), lambda i: (i, 0))
    consts = [lnw, lnb, woa, wob]
    return pl.pallas_call(
        _mix_out_kernel,
        grid=(n // tm,),
        in_specs=[row] * 5 + [_const_spec(t.shape) for t in consts],
        out_specs=row,
        out_shape=jax.ShapeDtypeStruct((n, D_MODEL), F32),
        compiler_params=pltpu.CompilerParams(
            dimension_semantics=("parallel",), vmem_limit_bytes=V7X_VMEM_LIMIT),
        name="mix_out",
    )(x, y_ssd, o, gate, bonus, *consts)


def _ffn_kernel(h_ref, p_ref, nf_ref, wg_ref, wu_ref, wd_ref, np_ref, wpg_ref, wpp_ref,
                nl_ref, y_ref):
    h = h_ref[...]
    hf = _rms(h, nf_ref[...]).astype(BF16)
    gate = jnp.dot(hf, wg_ref[...], preferred_element_type=F32)
    up = jnp.dot(hf, wu_ref[...], preferred_element_type=F32)
    h = h + _bdot(_silu(gate) * up, wd_ref[...])
    pg = _sigmoid(_bdot(_rms(h, np_ref[...]), wpg_ref[...]))
    h = h + pg * _bdot(p_ref[...], wpp_ref[...])
    y_ref[...] = _rms(h, nl_ref[...])


def ffn_ple(h, p, nf, wg, wu, wd, npl, wpg, wpp, nl, *, tm):
    n = h.shape[0]
    consts = [nf, wg, wu, wd, npl, wpg, wpp, nl]
    return pl.pallas_call(
        _ffn_kernel,
        grid=(n // tm,),
        in_specs=[pl.BlockSpec((tm, D_MODEL), lambda i: (i, 0)),
                  pl.BlockSpec((tm, PLE_DIM), lambda i: (i, 0))]
                 + [_const_spec(t.shape) for t in consts],
        out_specs=pl.BlockSpec((tm, D_MODEL), lambda i: (i, 0)),
        out_shape=jax.ShapeDtypeStruct((n, D_MODEL), F32),
        compiler_params=pltpu.CompilerParams(
            dimension_semantics=("parallel",), vmem_limit_bytes=V7X_VMEM_LIMIT),
        name="ffn_ple",
    )(h, p, *consts)


def _tail_kernel(o_ref, gate_ref, bonus_ref, x_ref, ys_ref, p_ref, lnw_ref, lnb_ref, woa_ref, wob_ref,
                 nf_ref, wg_ref, wu_ref, wd_ref, np_ref, wpg_ref, wpp_ref, nl_ref, y_ref, new_scr, cur_scr):
    i = pl.program_id(0)

    @pl.when(i == 0)
    def _():
        new_scr[...] = jnp.zeros(new_scr.shape, BF16)

    cur_scr[...] = new_scr[...]

    parts = 4
    rows = o_ref.shape[0] // parts
    inv_n = 1.0 / RWKV_HEAD_DIM

    def vector_half(part):
        rs = slice(part * rows, (part + 1) * rows)
        o = o_ref[rs, :]
        mu = _head_sums(o) * inv_n
        d = o - mu
        var = _head_sums(d * d) * inv_n
        on = d * lax.rsqrt(var + GN_EPS) * lnw_ref[...] + lnb_ref[...]
        new_scr[rs, :] = ((on + bonus_ref[rs, :]) * gate_ref[rs, :]).astype(BF16)

    vector_half(0)
    y_rwkv = cur_scr[...]
    h = x_ref[...] + _bdot(ys_ref[...], woa_ref[...]) + jnp.dot(y_rwkv, wob_ref[...],
                                                                preferred_element_type=F32)
    hf = _rms(h, nf_ref[...]).astype(BF16)
    vector_half(1)
    gate = jnp.dot(hf, wg_ref[...], preferred_element_type=F32)
    up = jnp.dot(hf, wu_ref[...], preferred_element_type=F32)
    vector_half(2)
    h = h + _bdot(_silu(gate) * up, wd_ref[...])
    vector_half(3)
    pg = _sigmoid(_bdot(_rms(h, np_ref[...]), wpg_ref[...]))
    h = h + pg * _bdot(p_ref[...], wpp_ref[...])
    y_ref[...] = _rms(h, nl_ref[...])


def layer_tail(o, gate, bonus, x, y_ssd, p, consts, *, tm):
    n = x.shape[0]
    nt = n // tm
    ahead = lambda w_: pl.BlockSpec((tm, w_), lambda i: (jnp.minimum(i, nt - 1), 0))
    behind = lambda w_: pl.BlockSpec((tm, w_), lambda i: (jnp.maximum(i - 1, 0), 0))
    return pl.pallas_call(
        _tail_kernel,
        grid=(nt + 1,),
        in_specs=[ahead(D_MODEL)] * 3 + [behind(D_MODEL)] * 2 + [behind(PLE_DIM)]
                 + [_const_spec(t.shape) for t in consts],
        out_specs=behind(D_MODEL),
        out_shape=jax.ShapeDtypeStruct((n, D_MODEL), F32),
        scratch_shapes=[pltpu.VMEM((tm, D_MODEL), BF16), pltpu.VMEM((tm, D_MODEL), BF16)],
        compiler_params=pltpu.CompilerParams(
            dimension_semantics=("arbitrary",), vmem_limit_bytes=V7X_VMEM_LIMIT),
        name="layer_tail",
    )(o, gate, bonus, x, y_ssd, p, *consts)


def _prepare_weights(w):
    c0, c1, c2 = SSD_WIDTH, SSD_WIDTH + SSD_CONV_DIM, SSD_WIDTH + SSD_CONV_DIM + SSD_HEADS
    w_in = w["w_in"]
    rowv = lambda t: t.reshape(1, -1)
    return dict(
        w,
        wz=w_in[:, :c0].astype(BF16), wx=w_in[:, c0:c1].astype(BF16),
        wdt=w_in[:, c1:c2].astype(BF16), wr=w_in[:, c2:].astype(BF16),
        woa=w["w_out"][:SSD_WIDTH].astype(BF16), wob=w["w_out"][SSD_WIDTH:].astype(BF16),
        wg=w["w_gate"].astype(BF16), wu=w["w_up"].astype(BF16), wd=w["w_down"].astype(BF16),
        wpg=w["w_ple_gate"].astype(BF16), wpp=w["w_ple_proj"].astype(BF16),
        norm_mix_r=rowv(w["norm_mix"]), norm_ffn_r=rowv(w["norm_ffn"]),
        norm_ple_r=rowv(w["norm_ple"]), norm_final_r=rowv(w["norm_final"]),
        ln_x_w_r=rowv(w["ln_x_w"]), ln_x_b_r=rowv(w["ln_x_b"]),
    )


def layer_forward(x, p, conv0, shift0, ssm0, wkv0, w, *, tm, ssd_q, prep_tt, wkv_steps):
    b, l, _ = x.shape
    n = b * l
    x2 = x.reshape(n, D_MODEL)
    chunked = l % WKV_CHUNK == 0 and l % tm == 0
    if chunked:
        z, xbc, dt, dtt, *ops, gate, bonus, shift_new = proj_prep(
            x, shift0, w["norm_mix_r"], w["wz"], w["wx"], w["wr"], w["wdt"], w, tm=tm, chunk=WKV_CHUNK)
    else:
        z, xbc, rw, dt, dtt = in_projection(x2, w["norm_mix_r"], w["wz"], w["wx"], w["wr"], w["wdt"], tm=tm)
        *ops, gate, bonus, shift_new = rwkv_prep(rw.reshape(b, l, -1), shift0, w, tt=prep_tt, chunk=0)
    y_ssd, ssm_new, conv_new = ssd_mixer(
        z.reshape(b, l, -1), xbc.reshape(b, l, -1), dt.reshape(b, l, -1), dtt, conv0, ssm0, w, q=ssd_q)
    if chunked:
        o, wkv_new = wkv_chunked(*ops, wkv0)
    elif b % WKV_LANE_BATCH == 0:
        o, wkv_new = wkv_scan_batch_lanes(*ops, wkv0)
    else:
        o, wkv_new = wkv_scan(*ops, wkv0, steps=wkv_steps)
    flat = lambda t: t.reshape(n, -1)
    tail_consts = [w["ln_x_w_r"], w["ln_x_b_r"], w["woa"], w["wob"], w["norm_ffn_r"], w["wg"], w["wu"],
                   w["wd"], w["norm_ple_r"], w["wpg"], w["wpp"], w["norm_final_r"]]
    y = layer_tail(flat(o), flat(gate), flat(bonus), x2, flat(y_ssd), p.reshape(n, PLE_DIM),
                   tail_consts, tm=tm)
    return y.reshape(b, l, D_MODEL), ssm_new, conv_new, wkv_new, shift_new


def kernel(x_prompt, x_sample, state_ssm, state_conv, state_wkv, state_shift, p_prompt, p_sample, norm_mix, w_in, conv_w, conv_b, dt_bias, a_log, d_skip, ssd_norm, shift_mu, w0, w2, a0, a2, g2, k_k, k_a, r_k, ln_x_w, ln_x_b, w_out, norm_ffn, w_gate, w_up, w_down, norm_ple, w_ple_gate, w_ple_proj, norm_final):
    w = _prepare_weights(dict(
        norm_mix=norm_mix[0], w_in=w_in[0], conv_w=conv_w[0], conv_b=conv_b[0], dt_bias=dt_bias[0],
        a_log=a_log[0], d_skip=d_skip[0], ssd_norm=ssd_norm[0], shift_mu=shift_mu[0], w0=w0[0],
        w2=w2[0], a0=a0[0], a2=a2[0], g2=g2[0], k_k=k_k[0], k_a=k_a[0], r_k=r_k[0],
        ln_x_w=ln_x_w[0], ln_x_b=ln_x_b[0], w_out=w_out[0], norm_ffn=norm_ffn[0],
        w_gate=w_gate[0], w_up=w_up[0], w_down=w_down[0], norm_ple=norm_ple[0],
        w_ple_gate=w_ple_gate[0], w_ple_proj=w_ple_proj[0], norm_final=norm_final))
    bp = x_prompt.shape[0]
    zeros = lambda *s: jnp.zeros(s, F32)
    yp, s1, c1, k1, t1 = layer_forward(
        x_prompt, p_prompt[0], zeros(bp, SSD_CONV - 1, SSD_CONV_DIM), zeros(bp, 1, RWKV_PROJ),
        zeros(bp, SSD_HEADS, SSD_HEAD_DIM, SSD_STATE),
        zeros(bp, RWKV_HEADS, RWKV_HEAD_DIM, RWKV_HEAD_DIM), w,
        tm=256, ssd_q=min(SSD_CHUNK, x_prompt.shape[1]), prep_tt=min(128, x_prompt.shape[1]),
        wkv_steps=min(16, x_prompt.shape[1]))
    ys, s2, c2, k2, t2 = layer_forward(
        x_sample, p_sample[0], state_conv[0], state_shift[0], state_ssm[0], state_wkv[0], w,
        tm=256, ssd_q=x_sample.shape[1], prep_tt=128, wkv_steps=x_sample.shape[1])
    return (yp, ys, s1[None], c1[None], k1[None], t1[None], s2[None], c2[None], k2[None], t2[None])
```

```python
import functools

import jax
import jax.numpy as jnp
from jax import lax
from jax.experimental import pallas as pl
from jax.experimental.pallas import tpu as pltpu

F32 = jnp.float32
BF16 = jnp.bfloat16
HIGHEST = lax.Precision.HIGHEST

D_MODEL = 1024
SSD_WIDTH = 1024
SSD_HEADS = 16
SSD_HEAD_DIM = 64
SSD_GROUPS = 2
SSD_GROUP_WIDTH = SSD_WIDTH // SSD_GROUPS
SSD_STATE = 128
SSD_CONV = 4
SSD_CHUNK = 128
SSD_BC = SSD_GROUPS * SSD_STATE
SSD_CONV_DIM = SSD_WIDTH + 2 * SSD_BC
RWKV_WIDTH = 1024
RWKV_HEADS = 16
RWKV_HEAD_DIM = 64
DECAY_LORA = 64
AAA_LORA = 64
GATE_LORA = 128
RWKV_PROJ = 3 * RWKV_WIDTH + DECAY_LORA + AAA_LORA + GATE_LORA
D_FF = 2816
PLE_DIM = 256
NORM_EPS = 1e-6
GN_EPS = 64e-5

WKV_BATCH_BLOCK = 8
V7X_VMEM_LIMIT = 56 * 1024 * 1024
CONV_PAD = 8
SSD_SEQS_PER_STEP = 4


def _rms(x, g):
    return x * lax.rsqrt(jnp.mean(x * x, axis=-1, keepdims=True) + NORM_EPS) * g


def _sigmoid(x):
    return 1.0 / (1.0 + jnp.exp(-x))


def _silu(x):
    return x * _sigmoid(x)


def _softplus(x):
    return jnp.maximum(x, 0.0) + jnp.log(1.0 + jnp.exp(-jnp.abs(x)))


def _bdot(a, b):
    return jnp.dot(a.astype(BF16), b.astype(BF16), preferred_element_type=F32)


def _split3(t):
    hi = t.astype(BF16)
    r1 = t - hi.astype(F32)
    mid = r1.astype(BF16)
    lo = (r1 - mid.astype(F32)).astype(BF16)
    return hi, mid, lo


def _dot01(a, b, *, exact_side):
    if exact_side == "lhs":
        m = b.astype(BF16)
        return sum(jnp.dot(p, m, preferred_element_type=F32) for p in _split3(a))
    m = a.astype(BF16)
    return sum(jnp.dot(m, p, preferred_element_type=F32) for p in _split3(b))


def _const_spec(shape):
    return pl.BlockSpec(shape, lambda *_: (0,) * len(shape), pipeline_mode=pl.Buffered(1))


def _head_expand(rows):
    h = lax.broadcasted_iota(jnp.int32, (rows, SSD_WIDTH), 0)
    c = lax.broadcasted_iota(jnp.int32, (rows, SSD_WIDTH), 1)
    return (c // SSD_HEAD_DIM == h).astype(F32)


def _proj_kernel(x_ref, g_ref, wz_ref, wx_ref, wr_ref, wdt_ref, wdtt_ref,
                 z_ref, xbc_ref, rw_ref, dt_ref, dtt_ref):
    u = _rms(x_ref[...], g_ref[...]).astype(BF16)
    z_ref[...] = jnp.dot(u, wz_ref[...], preferred_element_type=F32)
    xbc_ref[...] = jnp.dot(u, wx_ref[...], preferred_element_type=F32)
    rw_ref[...] = jnp.dot(u, wr_ref[...], preferred_element_type=F32)
    dt_ref[...] = jnp.dot(u, wdt_ref[...], preferred_element_type=F32)
    dtt_ref[...] = lax.dot_general(wdtt_ref[...], u, (((1,), (1,)), ((), ())), preferred_element_type=F32)


def in_projection(x, g, wz, wx, wr, wdt, *, tm):
    n = x.shape[0]
    row = lambda w: pl.BlockSpec((tm, w), lambda i: (i, 0))
    wdtt = wdt.T
    return pl.pallas_call(
        _proj_kernel,
        grid=(n // tm,),
        in_specs=[row(D_MODEL), _const_spec((1, D_MODEL)), _const_spec(wz.shape),
                  _const_spec(wx.shape), _const_spec(wr.shape), _const_spec(wdt.shape),
                  _const_spec(wdtt.shape)],
        out_specs=[row(SSD_WIDTH), row(SSD_CONV_DIM), row(RWKV_PROJ), row(SSD_HEADS),
                   pl.BlockSpec((SSD_HEADS, tm), lambda i: (0, i))],
        out_shape=[jax.ShapeDtypeStruct((n, SSD_WIDTH), F32),
                   jax.ShapeDtypeStruct((n, SSD_CONV_DIM), F32),
                   jax.ShapeDtypeStruct((n, RWKV_PROJ), F32),
                   jax.ShapeDtypeStruct((n, SSD_HEADS), F32),
                   jax.ShapeDtypeStruct((SSD_HEADS, n), F32)],
        compiler_params=pltpu.CompilerParams(
            dimension_semantics=("parallel",), vmem_limit_bytes=V7X_VMEM_LIMIT),
        name="in_projection",
    )(x, g, wz, wx, wr, wdt, wdtt)


def _ssd_kernel(z_ref, xbc_ref, dt_ref, dtt_ref, hist_ref, h0_ref, cw_ref, cb_ref,
                dtb_ref, dtbt_ref, alog_ref, alogt_ref, dsk_ref, nrm_ref,
                y_ref, hfin_ref, cnew_ref, xfull_scr, h_scr, *, q, nseq, single_chunk):
    refs = (z_ref, xbc_ref, dt_ref, dtt_ref, hist_ref, h0_ref, cw_ref, cb_ref, dtb_ref, dtbt_ref, alog_ref,
            alogt_ref, dsk_ref, nrm_ref, y_ref, hfin_ref, cnew_ref, xfull_scr, h_scr)
    for s in range(nseq):
        _ssd_sequence(s, *refs, q=q, single_chunk=single_chunk)


def _ssd_sequence(s, z_ref, xbc_ref, dt_ref, dtt_ref, hist_ref, h0_ref, cw_ref, cb_ref,
                  dtb_ref, dtbt_ref, alog_ref, alogt_ref, dsk_ref, nrm_ref,
                  y_ref, hfin_ref, cnew_ref, xfull_scr, h_scr, *, q, single_chunk):
    c = pl.program_id(1)
    last = pl.num_programs(1) - 1
    gw = SSD_GROUP_WIDTH

    @pl.when(c == 0)
    def _():
        xfull_scr[s,CONV_PAD - 3:CONV_PAD, :] = hist_ref[s]
        if not single_chunk:
            for g in range(SSD_GROUPS):
                h_scr[s * SSD_GROUPS + g] = h0_ref[s,g * 8:(g + 1) * 8].reshape(gw, SSD_STATE).T

    @pl.when(c > 0)
    def _():
        xfull_scr[s,CONV_PAD - 3:CONV_PAD, :] = xfull_scr[s,CONV_PAD + q - 3:CONV_PAD + q, :]

    xfull_scr[s,CONV_PAD:CONV_PAD + q, :] = xbc_ref[s]

    conv = cb_ref[...]
    for j in range(SSD_CONV):
        lo = CONV_PAD - 3 + j
        conv = conv + xfull_scr[s,lo:lo + q, :] * cw_ref[j:j + 1, :]
    act = _silu(conv)
    xs = act[:, :SSD_WIDTH]

    dt = _softplus(dt_ref[s] + dtb_ref[...])
    dtt_raw = dtt_ref[...] if len(dtt_ref.shape) == 2 else dtt_ref[s]
    dtt = _softplus(dtt_raw + dtbt_ref[...])
    da = dt * -jnp.exp(alog_ref[...])
    dat = dtt * -jnp.exp(alogt_ref[...])
    row = lax.broadcasted_iota(jnp.int32, (q, q), 0)
    col = lax.broadcasted_iota(jnp.int32, (q, q), 1)
    causal = row >= col
    a_cum = _dot01(causal.astype(F32), da, exact_side="rhs")
    a_cumt = _dot01(dat, (row <= col).astype(F32), exact_side="lhs")

    expand = _head_expand(SSD_HEADS)
    a_cum_x = _dot01(a_cum, expand, exact_side="lhs")
    dt_x = _dot01(dt, expand, exact_side="lhs")
    a_end_x = a_cum_x[q - 1:q, :]
    decay_in_x = jnp.exp(a_cum_x)
    chunk_decay_x = jnp.exp(a_end_x)
    xd = xs * (jnp.exp(a_end_x - a_cum_x) * dt_x)

    ys = []
    for g in range(SSD_GROUPS):
        bm = act[:, SSD_WIDTH + g * SSD_STATE:SSD_WIDTH + (g + 1) * SSD_STATE]
        cm = act[:, SSD_WIDTH + SSD_BC + g * SSD_STATE:SSD_WIDTH + SSD_BC + (g + 1) * SSD_STATE]
        cb = lax.dot_general(cm.astype(BF16), bm.astype(BF16), (((1,), (1,)), ((), ())),
                             preferred_element_type=F32)
        y_heads = []
        for e in range(8):
            h = g * 8 + e
            seg = a_cum[:, h:h + 1] - a_cumt[h:h + 1, :]
            lmat = jnp.where(causal, jnp.exp(jnp.where(causal, seg, 0.0)), 0.0)
            w_qs = cb * lmat * dtt[h:h + 1, :]
            y_heads.append(_bdot(w_qs, xs[:, h * SSD_HEAD_DIM:(h + 1) * SSD_HEAD_DIM]))
        y_diag = jnp.concatenate(y_heads, axis=1)
        sl = slice(g * gw, (g + 1) * gw)
        if single_chunk:
            h_in = h0_ref[s,g * 8:(g + 1) * 8].reshape(gw, SSD_STATE)
            y_off = lax.dot_general(cm.astype(BF16), h_in.astype(BF16), (((1,), (1,)), ((), ())),
                                    preferred_element_type=F32)
            upd = lax.dot_general(xd[:, sl].astype(BF16), bm.astype(BF16), (((0,), (0,)), ((), ())),
                                  preferred_element_type=F32)
            head_decay = jnp.broadcast_to(jnp.exp(a_cumt[:, q - 1:q]), (SSD_HEADS, SSD_STATE))
            for e in range(8):
                h = g * 8 + e
                hfin_ref[s,h] = (h0_ref[s,h] * head_decay[h:h + 1, :]
                                  + upd[e * SSD_HEAD_DIM:(e + 1) * SSD_HEAD_DIM, :])
        else:
            h_in = h_scr[s * SSD_GROUPS + g]
            y_off = _bdot(cm, h_in)
            upd = lax.dot_general(bm.astype(BF16), xd[:, sl].astype(BF16), (((0,), (0,)), ((), ())),
                                  preferred_element_type=F32)
            h_scr[s * SSD_GROUPS + g] = h_in * chunk_decay_x[:, sl] + upd
        ys.append(y_diag + y_off * decay_in_x[:, sl])

    y = jnp.concatenate(ys, axis=1) + dsk_ref[...] * xs
    yg = y * _silu(z_ref[s])
    outs = []
    for g in range(SSD_GROUPS):
        t = yg[:, g * gw:(g + 1) * gw]
        outs.append(t * lax.rsqrt(jnp.mean(t * t, axis=-1, keepdims=True) + NORM_EPS))
    y_ref[s] = jnp.concatenate(outs, axis=1) * nrm_ref[...]

    @pl.when(c == last)
    def _():
        cnew_ref[s] = xfull_scr[s,CONV_PAD + q - 3:CONV_PAD + q, :]
        if not single_chunk:
            for g in range(SSD_GROUPS):
                hfin_ref[s,g * 8:(g + 1) * 8] = h_scr[s * SSD_GROUPS + g].T.reshape(8, SSD_HEAD_DIM, SSD_STATE)


def ssd_mixer(z, xbc, dt, dtt_flat, conv0, ssm0, w, *, q):
    b, l, _ = z.shape
    single_chunk = l == q
    nseq = SSD_SEQS_PER_STEP if (single_chunk and b % SSD_SEQS_PER_STEP == 0) else 1
    if q % 128 == 0 and nseq == 1:
        dtt = dtt_flat
        dtt_spec = pl.BlockSpec((SSD_HEADS, q), lambda i, c: (0, i * (l // q) + c))
    else:
        dtt = jnp.swapaxes(dt, 1, 2)
        dtt_spec = pl.BlockSpec((nseq, SSD_HEADS, q), lambda i, c: (i, 0, c))
    seq = lambda wd: pl.BlockSpec((nseq, q, wd), lambda i, c: (i, c, 0))
    per_b3 = lambda s: pl.BlockSpec((nseq,) + s, lambda i, c: (i,) + (0,) * len(s))
    col = lambda t: t.reshape(-1, 1)
    rowv = lambda t: t.reshape(1, -1)
    consts = [w["conv_w"], rowv(w["conv_b"]), rowv(w["dt_bias"]), col(w["dt_bias"]),
              rowv(w["a_log"]), col(w["a_log"]),
              rowv(jnp.repeat(w["d_skip"], SSD_HEAD_DIM)), rowv(w["ssd_norm"])]
    return pl.pallas_call(
        functools.partial(_ssd_kernel, q=q, nseq=nseq, single_chunk=single_chunk),
        grid=(b // nseq, l // q),
        in_specs=[seq(SSD_WIDTH), seq(SSD_CONV_DIM), seq(SSD_HEADS),
                  dtt_spec,
                  per_b3((SSD_CONV - 1, SSD_CONV_DIM)),
                  per_b3((SSD_HEADS, SSD_HEAD_DIM, SSD_STATE))]
                 + [_const_spec(t.shape) for t in consts],
        out_specs=[seq(SSD_WIDTH), per_b3((SSD_HEADS, SSD_HEAD_DIM, SSD_STATE)),
                   per_b3((SSD_CONV - 1, SSD_CONV_DIM))],
        out_shape=[jax.ShapeDtypeStruct((b, l, SSD_WIDTH), F32),
                   jax.ShapeDtypeStruct((b, SSD_HEADS, SSD_HEAD_DIM, SSD_STATE), F32),
                   jax.ShapeDtypeStruct((b, SSD_CONV - 1, SSD_CONV_DIM), F32)],
        scratch_shapes=[pltpu.VMEM((nseq, CONV_PAD + q, SSD_CONV_DIM), F32),
                        pltpu.VMEM((nseq * SSD_GROUPS, SSD_STATE, SSD_GROUP_WIDTH), F32)],
        compiler_params=pltpu.CompilerParams(
            dimension_semantics=("parallel", "arbitrary"), vmem_limit_bytes=V7X_VMEM_LIMIT),
        name="ssd_mixer",
    )(z, xbc, dt, dtt, conv0, ssm0, *consts)


def _head_sums(t):
    pair = 2 * RWKV_HEAD_DIM
    first = lax.broadcasted_iota(jnp.int32, (1, pair), 1) < RWKV_HEAD_DIM
    pieces = []
    for j in range(RWKV_HEADS // 2):
        x = t[:, j * pair:(j + 1) * pair]
        x0 = jnp.where(first, x, 0.0)
        s0 = jnp.sum(x0, axis=-1, keepdims=True)
        s1 = jnp.sum(x - x0, axis=-1, keepdims=True)
        pieces.append(jnp.where(first, s0, s1))
    return jnp.concatenate(pieces, axis=1)


def _rwkv_prep_kernel(rw_ref, sh0_ref, mu_ref, w0_ref, w2_ref, a0_ref, a2_ref, g2_ref,
                      kk_ref, ka_ref, rk_ref,
                      *rest, tt, seqs, chunk):
    outs, full_scr = rest[:-1], rest[-1]
    shn_ref = outs[-1]
    c = pl.program_id(1)
    l = tt // seqs

    if seqs == 1:
        @pl.when(c == 0)
        def _():
            full_scr[0, CONV_PAD - 1:CONV_PAD, :] = sh0_ref[0]

        @pl.when(c > 0)
        def _():
            full_scr[0, CONV_PAD - 1:CONV_PAD, :] = full_scr[0, CONV_PAD + tt - 1:CONV_PAD + tt, :]

        rw = rw_ref[0]
        full_scr[0, CONV_PAD:CONV_PAD + tt, :] = rw
        prev = full_scr[0, CONV_PAD - 1:CONV_PAD - 1 + tt, :]
    else:
        full_scr[:, CONV_PAD - 1:CONV_PAD, :] = sh0_ref[...]
        full_scr[:, CONV_PAD:CONV_PAD + l, :] = rw_ref[...]
        rw = rw_ref[...].reshape(tt, RWKV_PROJ)
        prev = full_scr[:, CONV_PAD - 1:CONV_PAD - 1 + l, :].reshape(tt, RWKV_PROJ)
    vals = _rwkv_mix_math(rw, prev, mu_ref, w0_ref, w2_ref, a0_ref, a2_ref, g2_ref, kk_ref, ka_ref, rk_ref)
    _emit_rwkv_outputs(outs[:-1], vals, tt=tt, chunk=chunk)

    if seqs == 1:
        @pl.when(c == pl.num_programs(1) - 1)
        def _():
            shn_ref[0] = full_scr[0, CONV_PAD + tt - 1:CONV_PAD + tt, :]
    else:
        shn_ref[...] = rw_ref[:, l - 1:l, :]


def _rwkv_mix_math(rw, prev, mu_ref, w0_ref, w2_ref, a0_ref, a2_ref, g2_ref, kk_ref, ka_ref, rk_ref):
    wd = RWKV_WIDTH
    u = rw + (prev - rw) * mu_ref[...]
    r = u[:, :wd]
    k = u[:, wd:2 * wd]
    v = u[:, 2 * wd:3 * wd]
    w_lo = u[:, 3 * wd:3 * wd + DECAY_LORA]
    a_lo = u[:, 3 * wd + DECAY_LORA:3 * wd + DECAY_LORA + AAA_LORA]
    g_lo = u[:, 3 * wd + DECAY_LORA + AAA_LORA:]

    w_log = -_softplus(-(w0_ref[...] + _bdot(jnp.tanh(w_lo), w2_ref[...]))) - 0.5
    lw = -jnp.exp(w_log)
    a = _sigmoid(a0_ref[...] + _bdot(a_lo, a2_ref[...]))
    gate = _bdot(_sigmoid(g_lo), g2_ref[...])

    kk = k * kk_ref[...]
    kk = kk / jnp.maximum(jnp.sqrt(_head_sums(kk * kk)), 1e-12)
    kf = k * (1.0 + (a - 1.0) * ka_ref[...])
    kb = kk * a
    bonus = _head_sums(r * kf * rk_ref[...]) * v
    return r, lw, kf, v, kk, kb, gate, bonus


def _emit_rwkv_outputs(outs, vals, *, tt, chunk, row0=0):
    r, lw, kf, v, kk, kb, gate, bonus = vals
    wd = RWKV_WIDTH
    gate_out, bonus_out = outs[-2:]
    if chunk == 0:
        blk = gate_out.shape
        for ref, val in zip(outs, (r, jnp.exp(lw), kf, v, kk, kb, gate, bonus)):
            ref[...] = val.reshape(blk)
    else:
        gate_out[0, row0:row0 + tt, :] = gate
        bonus_out[0, row0:row0 + tt, :] = bonus
        kkt_out, rt_out, kfh_out, nbh_out, vb_out, kbg_out, kfg_out, gend_out = outs[:8]
        tri = (lax.broadcasted_iota(jnp.int32, (chunk, chunk), 0)
               >= lax.broadcasted_iota(jnp.int32, (chunk, chunk), 1)).astype(F32)
        for ci in range(tt // chunk):
            rs = slice(ci * chunk, (ci + 1) * chunk)
            ro = slice(row0 + ci * chunk, row0 + (ci + 1) * chunk)
            lw_c = lw[rs]
            cum = _dot01(tri, lw_c, exact_side="rhs")
            cum_end = cum[chunk - 1:chunk, :]
            g_inv = jnp.exp(-cum)
            g_tail = jnp.exp(cum_end - cum)
            kkt_out[0, ro, :] = (kk[rs] * jnp.exp(cum - lw_c)).astype(BF16)
            rt_out[0, ro, :] = (r[rs] * jnp.exp(cum)).astype(BF16)
            kfh_out[0, ro, :] = (kf[rs] * g_tail).astype(BF16)
            nbh_out[0, ro, :] = (-kb[rs] * g_tail).astype(BF16)
            vb_out[0, ro, :] = v[rs].astype(BF16)
            kbg_out[0, ro, :] = (kb[rs] * g_inv).astype(BF16)
            kfg_out[0, ro, :] = (kf[rs] * g_inv).astype(BF16)
            gend_out[0, row0 // chunk + ci] = jnp.broadcast_to(jnp.exp(cum_end), (8, wd))


def rwkv_prep(rw, shift0, w, *, tt, chunk):
    b, l, _ = rw.shape
    seqs = max(1, tt // l)
    assert chunk == 0 or (seqs == 1 and tt % chunk == 0)
    rowv = lambda t: t.reshape(1, -1)
    consts = [rowv(w["shift_mu"]), rowv(w["w0"]), w["w2"].astype(BF16), rowv(w["a0"]),
              w["a2"].astype(BF16), w["g2"].astype(BF16), rowv(w["k_k"]), rowv(w["k_a"]),
              rowv(w["r_k"])]
    rows = tt // seqs
    grid = (b // seqs, l // rows)
    seq = lambda wd: pl.BlockSpec((seqs, rows, wd), lambda i, c: (i, c, 0))
    one = pl.BlockSpec((seqs, 1, RWKV_PROJ), lambda i, c: (i, 0, 0))
    sds = jax.ShapeDtypeStruct
    f32_seq = sds((b, l, RWKV_WIDTH), F32)
    if chunk == 0:
        op_specs = [seq(RWKV_WIDTH)] * 6
        op_shapes = [f32_seq] * 6
    else:
        per_tile = tt // chunk
        op_specs = [seq(RWKV_WIDTH)] * 7 + [
            pl.BlockSpec((1, per_tile, 8, RWKV_WIDTH), lambda i, c: (i, c, 0, 0))]
        op_shapes = [sds((b, l, RWKV_WIDTH), BF16)] * 7 + [sds((b, l // chunk, 8, RWKV_WIDTH), F32)]
    outs = pl.pallas_call(
        functools.partial(_rwkv_prep_kernel, tt=tt, seqs=seqs, chunk=chunk),
        grid=grid,
        in_specs=[seq(RWKV_PROJ), one] + [_const_spec(t.shape) for t in consts],
        out_specs=op_specs + [seq(RWKV_WIDTH)] * 2 + [one],
        out_shape=op_shapes + [f32_seq] * 2 + [sds((b, 1, RWKV_PROJ), F32)],
        scratch_shapes=[pltpu.VMEM((seqs, CONV_PAD + rows, RWKV_PROJ), F32)],
        compiler_params=pltpu.CompilerParams(
            dimension_semantics=("parallel", "arbitrary"), vmem_limit_bytes=V7X_VMEM_LIMIT),
        name="rwkv_prep",
    )(rw, shift0, *consts)
    return outs


def _proj_prep_kernel(x_ref, sh0_ref, g_ref, wz_ref, wx_ref, wr_ref, wdt_ref, wdtt_ref,
                      mu_ref, w0_ref, w2_ref, a0_ref, a2_ref, g2_ref, kk_ref, ka_ref, rk_ref,
                      z_ref, xbc_ref, dt_ref, dtt_ref, *rest, tm, tiles_per_seq, chunk):
    outs, (new_scr, cur_scr) = rest[:-2], rest[-2:]
    shn_ref = outs[-1]
    i = pl.program_id(0)

    @pl.when(i == 0)
    def _():
        new_scr[...] = jnp.zeros(new_scr.shape, F32)
        cur_scr[...] = jnp.zeros(cur_scr.shape, F32)

    k = jnp.maximum(i - 1, 0)
    first = (k % tiles_per_seq) == 0
    cur_scr[CONV_PAD - 1:CONV_PAD, :] = jnp.where(first, sh0_ref[0], cur_scr[CONV_PAD + tm - 1:CONV_PAD + tm, :])
    cur_scr[CONV_PAD:CONV_PAD + tm, :] = new_scr[...]

    shn_ref[0] = cur_scr[CONV_PAD + tm - 1:CONV_PAD + tm, :]
    u = _rms(x_ref[...], g_ref[...]).astype(BF16)

    def project(piece):
        if piece == 0:
            z_ref[...] = jnp.dot(u, wz_ref[...], preferred_element_type=F32)
        elif piece == 1:
            xbc_ref[...] = jnp.dot(u, wx_ref[...], preferred_element_type=F32)
        elif piece == 2:
            cols = slice(0, 2 * RWKV_WIDTH)
            new_scr[:, cols] = jnp.dot(u, wr_ref[:, cols], preferred_element_type=F32)
        else:
            cols = slice(2 * RWKV_WIDTH, RWKV_PROJ)
            new_scr[:, cols] = jnp.dot(u, wr_ref[:, cols], preferred_element_type=F32)
            dt_ref[...] = jnp.dot(u, wdt_ref[...], preferred_element_type=F32)
            dtt_ref[...] = lax.dot_general(wdtt_ref[...], u, (((1,), (1,)), ((), ())),
                                           preferred_element_type=F32)

    def prepare(part, rows):
        lo = CONV_PAD + part * rows
        rw = cur_scr[lo:lo + rows, :]
        prev = cur_scr[lo - 1:lo - 1 + rows, :]
        vals = _rwkv_mix_math(rw, prev, mu_ref, w0_ref, w2_ref, a0_ref, a2_ref, g2_ref, kk_ref, ka_ref, rk_ref)
        _emit_rwkv_outputs(outs[:-1], vals, tt=rows, chunk=chunk, row0=part * rows)

    parts = 4
    for part in range(parts):
        project(part)
        prepare(part, tm // parts)


def proj_prep(x, shift0, g, wz, wx, wr, wdt, w, *, tm, chunk):
    b, l, _ = x.shape
    n = b * l
    nt, tps = n // tm, l // tm
    assert l % tm == 0 and tm % chunk == 0
    x2 = x.reshape(n, D_MODEL)
    rowv = lambda t: t.reshape(1, -1)
    wdtt = wdt.T
    consts = [g, wz, wx, wr, wdt, wdtt,
              rowv(w["shift_mu"]), rowv(w["w0"]), w["w2"].astype(BF16), rowv(w["a0"]),
              w["a2"].astype(BF16), w["g2"].astype(BF16), rowv(w["k_k"]), rowv(w["k_a"]), rowv(w["r_k"])]
    ahead = lambda i: jnp.minimum(i, nt - 1)
    behind = lambda i: jnp.maximum(i - 1, 0)
    row_a = lambda w_: pl.BlockSpec((tm, w_), lambda i: (ahead(i), 0))
    seq_b = lambda w_: pl.BlockSpec((1, tm, w_), lambda i: (behind(i) // tps, behind(i) % tps, 0))
    one_b = pl.BlockSpec((1, 1, RWKV_PROJ), lambda i: (behind(i) // tps, 0, 0))
    per_tile = tm // chunk
    gend_spec = pl.BlockSpec((1, per_tile, 8, RWKV_WIDTH), lambda i: (behind(i) // tps, behind(i) % tps, 0, 0))
    sds = jax.ShapeDtypeStruct
    outs = pl.pallas_call(
        functools.partial(_proj_prep_kernel, tm=tm, tiles_per_seq=tps, chunk=chunk),
        grid=(nt + 1,),
        in_specs=[row_a(D_MODEL), one_b] + [_const_spec(t.shape) for t in consts],
        out_specs=[row_a(SSD_WIDTH), row_a(SSD_CONV_DIM), row_a(SSD_HEADS),
                   pl.BlockSpec((SSD_HEADS, tm), lambda i: (0, ahead(i)))]
                  + [seq_b(RWKV_WIDTH)] * 7 + [gend_spec] + [seq_b(RWKV_WIDTH)] * 2 + [one_b],
        out_shape=[sds((n, SSD_WIDTH), F32), sds((n, SSD_CONV_DIM), F32), sds((n, SSD_HEADS), F32),
                   sds((SSD_HEADS, n), F32)]
                  + [sds((b, l, RWKV_WIDTH), BF16)] * 7 + [sds((b, l // chunk, 8, RWKV_WIDTH), F32)]
                  + [sds((b, l, RWKV_WIDTH), F32)] * 2 + [sds((b, 1, RWKV_PROJ), F32)],
        scratch_shapes=[pltpu.VMEM((tm, RWKV_PROJ), F32), pltpu.VMEM((CONV_PAD + tm, RWKV_PROJ), F32)],
        compiler_params=pltpu.CompilerParams(
            dimension_semantics=("arbitrary",), vmem_limit_bytes=V7X_VMEM_LIMIT),
        name="proj_prep",
    )(x2, shift0, *consts)
    return outs


def _wkv_kernel(r_ref, w_ref, k_ref, v_ref, kk_ref, kka_ref, s0_ref,
                o_ref, sfin_ref, s_scr, vt_scr, ot_scr, *, steps):
    c = pl.program_id(1)
    n = RWKV_HEAD_DIM
    lanes = WKV_BATCH_BLOCK * RWKV_HEADS

    @pl.when(c == 0)
    def _():
        s_scr[...] = s0_ref[...].reshape(lanes, n * n).T.reshape(n, n, lanes)

    def to_pairs(ref, t):
        return ref[:, t].reshape(lanes, n).T

    def step(t, carry):
        r_t = to_pairs(r_ref, t)
        w_t = to_pairs(w_ref, t)
        k_t = to_pairs(k_ref, t)
        kk_t = to_pairs(kk_ref, t)
        kka_t = to_pairs(kka_ref, t)
        vt_scr[...] = to_pairs(v_ref, t)

        def per_value(vi, carry2):
            s_v = s_scr[vi]
            skk = jnp.sum(s_v * kk_t, axis=0, keepdims=True)
            v_row = vt_scr[pl.ds(vi, 1), :]
            s_new = s_v * w_t - skk * kka_t + v_row * k_t
            s_scr[vi] = s_new
            ot_scr[pl.ds(vi, 1), :] = jnp.sum(s_new * r_t, axis=0, keepdims=True)
            return carry2

        lax.fori_loop(0, n, per_value, 0, unroll=4)
        o_ref[:, t] = ot_scr[...].T.reshape(WKV_BATCH_BLOCK, RWKV_HEADS, n)
        return carry

    lax.fori_loop(0, steps, step, 0)

    @pl.when(c == pl.num_programs(1) - 1)
    def _():
        sfin_ref[...] = s_scr[...].reshape(n * n, lanes).T.reshape(
            WKV_BATCH_BLOCK, RWKV_HEADS, n, n)


def wkv_scan(r, w, k, v, kk, kka, s0, *, steps):
    b, l, _ = r.shape
    h, n = RWKV_HEADS, RWKV_HEAD_DIM
    assert b % WKV_BATCH_BLOCK == 0 and l % steps == 0
    ops = [t.reshape(b, l, h, n) for t in (r, w, k, v, kk, kka)]
    seq_spec = pl.BlockSpec((WKV_BATCH_BLOCK, steps, h, n), lambda g, c: (g, c, 0, 0))
    st_spec = pl.BlockSpec((WKV_BATCH_BLOCK, h, n, n), lambda g, c: (g, 0, 0, 0))
    o, s_fin = pl.pallas_call(
        functools.partial(_wkv_kernel, steps=steps),
        grid=(b // WKV_BATCH_BLOCK, l // steps),
        in_specs=[seq_spec] * 6 + [st_spec],
        out_specs=[seq_spec, st_spec],
        out_shape=[jax.ShapeDtypeStruct((b, l, h, n), F32),
                   jax.ShapeDtypeStruct((b, h, n, n), F32)],
        scratch_shapes=[pltpu.VMEM((n, n, WKV_BATCH_BLOCK * h), F32),
                        pltpu.VMEM((n, WKV_BATCH_BLOCK * h), F32),
                        pltpu.VMEM((n, WKV_BATCH_BLOCK * h), F32)],
        compiler_params=pltpu.CompilerParams(
            dimension_semantics=("parallel", "arbitrary"), vmem_limit_bytes=V7X_VMEM_LIMIT),
        name="wkv_scan",
    )(*ops, s0)
    return o.reshape(b, l, h * n), s_fin


WKV_LANE_BATCH = 128


def _wkv_batch_lanes_kernel(r_ref, w_ref, k_ref, v_ref, kk_ref, kka_ref, s0_ref,
                            o_ref, sfin_ref, op_scr, ot_scr, *, steps):
    n = RWKV_HEAD_DIM
    nb = WKV_LANE_BATCH
    sfin_ref[...] = s0_ref[...]

    def step(t, carry):
        rows = pl.ds(t, nb, stride=steps)
        for i, ref in enumerate((r_ref, w_ref, k_ref, kk_ref, kka_ref, v_ref)):
            op_scr[i] = ref[rows, :].T
        for h2 in range(2):
            ch = slice(h2 * n, (h2 + 1) * n)
            r_t = op_scr[0, ch, :]
            kka_r = jnp.sum(op_scr[4, ch, :] * r_t, axis=0, keepdims=True)
            k_r = jnp.sum(op_scr[2, ch, :] * r_t, axis=0, keepdims=True)
            op_scr[0, ch, :] = op_scr[1, ch, :] * r_t

            def per_value(vi, carry2, ch=ch, h2=h2, kka_r=kka_r, k_r=k_r):
                s_v = sfin_ref[h2, vi]
                skk = jnp.sum(s_v * op_scr[3, ch, :], axis=0, keepdims=True)
                out = jnp.sum(s_v * op_scr[0, ch, :], axis=0, keepdims=True)
                v_row = op_scr[5, pl.ds(h2 * n + vi, 1), :]
                sfin_ref[h2, vi] = s_v * op_scr[1, ch, :] - skk * op_scr[4, ch, :] + v_row * op_scr[2, ch, :]
                ot_scr[pl.ds(h2 * n + vi, 1), :] = out - skk * kka_r + v_row * k_r
                return carry2

            lax.fori_loop(0, n, per_value, 0, unroll=4)
        o_ref[rows, :] = ot_scr[...].T
        return carry

    lax.fori_loop(0, steps, step, 0)


def wkv_scan_batch_lanes(r, w, k, v, kk, kka, s0):
    b, l, wd = r.shape
    h, n, nb = RWKV_HEADS, RWKV_HEAD_DIM, WKV_LANE_BATCH
    assert b % nb == 0
    ops = [t.reshape(b * l, wd) for t in (r, w, k, v, kk, kka)]
    s0t = jnp.transpose(s0, (1, 2, 3, 0))
    seq_spec = pl.BlockSpec((nb * l, 2 * n), lambda g, j: (g, j))
    st_spec = pl.BlockSpec((2, n, n, nb), lambda g, j: (j, 0, 0, g))
    o, s_fin = pl.pallas_call(
        functools.partial(_wkv_batch_lanes_kernel, steps=l),
        grid=(b // nb, h // 2),
        in_specs=[seq_spec] * 6 + [st_spec],
        out_specs=[seq_spec, st_spec],
        out_shape=[jax.ShapeDtypeStruct((b * l, wd), F32), jax.ShapeDtypeStruct((h, n, n, b), F32)],
        scratch_shapes=[pltpu.VMEM((6, 2 * n, nb), F32), pltpu.VMEM((2 * n, nb), F32)],
        compiler_params=pltpu.CompilerParams(
            dimension_semantics=("parallel", "parallel"), vmem_limit_bytes=V7X_VMEM_LIMIT),
        name="wkv_scan_batch_lanes",
    )(*ops, s0t)
    return o.reshape(b, l, wd), jnp.transpose(s_fin, (3, 0, 1, 2))


WKV_CHUNK = 64
WKV_PAIRS = RWKV_HEADS // 2
WKV_ROW_STRIDE = WKV_CHUNK + 8
WKV_SOLVE_ROWS = 4
WKV_SOLVE_COLS = 16


def _pair_masks():
    c = WKV_CHUNK
    row = lax.broadcasted_iota(jnp.int32, (2 * c, 2 * c), 0)
    col = lax.broadcasted_iota(jnp.int32, (2 * c, 2 * c), 1)
    t, i = row % c, col % c
    keep = i <= t - jnp.where(row < c, 1, 0)
    sign = jnp.where(row >= c, jnp.where(col < c, -1.0, 1.0), 1.0)
    block_diag = row // c == col // c
    return keep, sign, block_diag


def _wkv_prepare_kernel(kkt_ref, rt_ref, kbg_ref, kfg_ref, vb_ref,
                        lo_ref, rhs0_ref, tp_ref, abs_scr, top_scr, abt_scr, tt_scr):
    c = WKV_CHUNK
    keep, sign, _ = _pair_masks()
    lane = lax.broadcasted_iota(jnp.int32, (1, 2 * c), 1)
    head0 = lane < RWKV_HEAD_DIM
    zeros = jnp.zeros((c, 2 * c), BF16)

    def per_batch(b, carry):
        for j in range(WKV_PAIRS):
            sl = slice(j * 2 * RWKV_HEAD_DIM, (j + 1) * 2 * RWKV_HEAD_DIM)
            lhs = jnp.concatenate([kkt_ref[b, :, sl], rt_ref[b, :, sl]], axis=0)
            rhs = jnp.concatenate([kbg_ref[b, :, sl], kfg_ref[b, :, sl]], axis=0)
            for h2 in range(2):
                sel = head0 if h2 == 0 else lane >= RWKV_HEAD_DIM
                a = lax.dot_general(jnp.where(sel, lhs, jnp.zeros_like(lhs)), rhs,
                                    (((1,), (1,)), ((), ())), preferred_element_type=F32)
                a = jnp.where(keep, a, 0.0) * sign
                inst = h2 * (WKV_BATCH_BLOCK * WKV_PAIRS) + b * WKV_PAIRS + j
                abs_scr[pl.ds(inst * WKV_ROW_STRIDE, c), :c] = a[:c, :c]
                top_scr[j, :, h2 * 2 * c:(h2 + 1) * 2 * c] = a[:c].astype(BF16)
                lo_ref[b, :, (2 * j + h2) * 2 * c:(2 * j + h2 + 1) * 2 * c] = a[c:].astype(BF16)

        for j in range(WKV_PAIRS):
            sl = slice(j * 2 * RWKV_HEAD_DIM, (j + 1) * 2 * RWKV_HEAD_DIM)
            v = vb_ref[b, :, sl]
            v0 = jnp.where(head0, v, jnp.zeros_like(v))
            w_akf = jnp.concatenate([zeros, v0, zeros, v - v0], axis=0)
            rhs0_ref[b, :, sl] = jnp.dot(top_scr[j], w_akf, preferred_element_type=F32)
        return carry

    lax.fori_loop(0, WKV_BATCH_BLOCK, per_batch, 0)

    n_inst = 2 * WKV_BATCH_BLOCK * WKV_PAIRS
    n_pair_rows = WKV_BATCH_BLOCK * WKV_PAIRS

    def to_lanes(t, carry):
        abt_scr[t] = abs_scr[pl.ds(t, n_inst, stride=WKV_ROW_STRIDE), :][:, :c].T
        return carry

    lax.fori_loop(0, c, to_lanes, 0, unroll=8)

    tt_scr[...] = jnp.zeros(tt_scr.shape, F32)
    sub_iota = lax.broadcasted_iota(jnp.int32, (WKV_SOLVE_COLS, n_inst), 0)
    rows = range(WKV_SOLVE_ROWS)
    for cb in range(c // WKV_SOLVE_COLS):
        col0 = WKV_SOLVE_COLS * cb
        cols = slice(col0, col0 + WKV_SOLVE_COLS)
        first_block = col0 // WKV_SOLVE_ROWS

        def solve_rows(tb, carry, col0=col0, cols=cols, first_block=first_block):
            t0 = tb * WKV_SOLVE_ROWS

            def sub(ib, accs):
                ps = [tt_scr[ib * WKV_SOLVE_ROWS + di, cols, :] for di in rows]
                out = []
                for r in rows:
                    terms = [abt_scr[t0 + r, pl.ds(ib * WKV_SOLVE_ROWS + di, 1), :] * ps[di] for di in rows]
                    while len(terms) > 1:
                        terms = [a + b for a, b in zip(terms[::2], terms[1::2])]
                    out.append(accs[r] - terms[0])
                return tuple(out)

            unit = tuple(jnp.where(sub_iota + col0 == t0 + r, 1.0, 0.0) for r in rows)
            accs = list(lax.fori_loop(first_block, tb, sub, unit))
            for r in rows:
                for r2 in range(r):
                    accs[r] = accs[r] - abt_scr[t0 + r, pl.ds(t0 + r2, 1), :] * accs[r2]
                tt_scr[t0 + r, cols, :] = accs[r]
            return carry

        lax.fori_loop(first_block, c // WKV_SOLVE_ROWS, solve_rows, 0)

    def from_lanes(t, carry):
        m = tt_scr[t].T
        abs_scr[pl.ds(t, n_pair_rows, stride=WKV_ROW_STRIDE), :] = jnp.concatenate(
            [m[:n_pair_rows], m[n_pair_rows:]], axis=1)
        return carry

    lax.fori_loop(0, c, from_lanes, 0, unroll=8)

    def emit(b, carry):
        for j in range(WKV_PAIRS):
            row0 = (b * WKV_PAIRS + j) * WKV_ROW_STRIDE
            tp_ref[b, :, j * 2 * c:(j + 1) * 2 * c] = abs_scr[pl.ds(row0, c), :].astype(BF16)
        return carry

    lax.fori_loop(0, WKV_BATCH_BLOCK, emit, 0)


def _wkv_apply_kernel(kkt_ref, rt_ref, kfh_ref, nbh_ref, vb_ref, tp_ref, lo_ref, rhs0_ref, gend_ref, s0_ref,
                      o_ref, sfin_ref, x_scr, wp_scr, p_scr):
    c = WKV_CHUNK
    n = RWKV_HEAD_DIM
    ch = pl.program_id(1)
    _, _, block_diag = _pair_masks()
    lane = lax.broadcasted_iota(jnp.int32, (1, 2 * c), 1)
    head0 = lane < n
    eye2 = (lax.broadcasted_iota(jnp.int32, (n, 2 * n), 0)
            == lax.broadcasted_iota(jnp.int32, (n, 2 * n), 1) % n).astype(F32)

    @pl.when(ch == 0)
    def _():
        def init(b, carry):
            for j in range(WKV_PAIRS):
                sp = s0_ref[b, 2 * j:2 * j + 2].reshape(2 * n, n)
                dup = jnp.dot(sp, eye2, precision=HIGHEST, preferred_element_type=F32)
                x_scr[b, j] = jnp.where(block_diag, dup, 0.0)
            return carry
        lax.fori_loop(0, WKV_BATCH_BLOCK, init, 0)

    def per_batch(b, carry):
        for j in range(WKV_PAIRS):
            sl = slice(j * 2 * n, (j + 1) * 2 * n)
            lhs = jnp.concatenate([kkt_ref[b, :, sl], rt_ref[b, :, sl]], axis=0)
            kx = lax.dot_general(lhs, x_scr[b, j].astype(BF16), (((1,), (1,)), ((), ())),
                                 preferred_element_type=F32)
            rhs = kx[:c] + rhs0_ref[b, :, sl]
            r0 = jnp.where(head0, rhs, 0.0)
            wp_scr[j] = jnp.concatenate([r0, rhs - r0], axis=0).astype(BF16)
            o_ref[b, :, sl] = kx[c:]
        for j in range(WKV_PAIRS):
            sl = slice(j * 2 * n, (j + 1) * 2 * n)
            p = jnp.dot(tp_ref[b, :, sl], wp_scr[j], preferred_element_type=F32)
            p_scr[j] = p.astype(BF16)
        for j in range(WKV_PAIRS):
            sl = slice(j * 2 * n, (j + 1) * 2 * n)
            v = vb_ref[b, :, sl]
            pb = p_scr[j]
            zero = jnp.zeros_like(pb)
            p0, v0 = jnp.where(head0, pb, zero), jnp.where(head0, v, zero)
            w_o = jnp.concatenate([p0, v0, pb - p0, v - v0], axis=0)
            o_ref[b, :, sl] = o_ref[b, :, sl] + jnp.dot(
                lo_ref[b, :, j * 4 * c:(j + 1) * 4 * c], w_o, preferred_element_type=F32)
            vp = jnp.concatenate([v, pb], axis=0)
            kb = jnp.concatenate([kfh_ref[b, :, sl], nbh_ref[b, :, sl]], axis=0)
            upd = lax.dot_general(vp, kb, (((0,), (0,)), ((), ())), preferred_element_type=F32)
            x_scr[b, j] = jnp.where(block_diag, x_scr[b, j] * gend_ref[b, 0, 0:1, sl] + upd, 0.0)
        return carry

    lax.fori_loop(0, WKV_BATCH_BLOCK, per_batch, 0)

    @pl.when(ch == pl.num_programs(1) - 1)
    def _():
        def fin(b, carry):
            for j in range(WKV_PAIRS):
                sp = lax.dot_general(x_scr[b, j], eye2, (((1,), (1,)), ((), ())),
                                     precision=HIGHEST, preferred_element_type=F32)
                sfin_ref[b, 2 * j:2 * j + 2] = sp.reshape(2, n, n)
            return carry
        lax.fori_loop(0, WKV_BATCH_BLOCK, fin, 0)


def wkv_chunked(kkt, rt, kfh, nbh, vb, kbg, kfg, gend, s0):
    b, l, wd = kkt.shape
    c = WKV_CHUNK
    assert b % WKV_BATCH_BLOCK == 0 and l % c == 0
    gb, nc = b // WKV_BATCH_BLOCK, l // c
    lanes = 2 * WKV_BATCH_BLOCK * WKV_PAIRS
    seq = lambda w_: pl.BlockSpec((WKV_BATCH_BLOCK, c, w_), lambda g, i: (g, i, 0))
    gend_spec = pl.BlockSpec((WKV_BATCH_BLOCK, 1, 8, wd), lambda g, i: (g, i, 0, 0))
    sds = jax.ShapeDtypeStruct
    lo, rhs0, tp = pl.pallas_call(
        _wkv_prepare_kernel,
        grid=(gb, nc),
        in_specs=[seq(wd)] * 5,
        out_specs=[seq(2 * wd), seq(wd), seq(wd)],
        out_shape=[sds((b, l, 2 * wd), BF16), sds((b, l, wd), F32), sds((b, l, wd), BF16)],
        scratch_shapes=[pltpu.VMEM((lanes * WKV_ROW_STRIDE, 2 * c), F32),
                        pltpu.VMEM((WKV_PAIRS, c, 4 * c), BF16),
                        pltpu.VMEM((c, c, lanes), F32),
                        pltpu.VMEM((c, c, lanes), F32)],
        compiler_params=pltpu.CompilerParams(
            dimension_semantics=("parallel", "parallel"), vmem_limit_bytes=V7X_VMEM_LIMIT),
        name="wkv_prepare",
    )(kkt, rt, kbg, kfg, vb)
    st_spec = pl.BlockSpec((WKV_BATCH_BLOCK, RWKV_HEADS, RWKV_HEAD_DIM, RWKV_HEAD_DIM),
                           lambda g, i: (g, 0, 0, 0))
    o, s_fin = pl.pallas_call(
        _wkv_apply_kernel,
        grid=(gb, nc),
        in_specs=[seq(wd)] * 6 + [seq(2 * wd), seq(wd), gend_spec, st_spec],
        out_specs=[seq(wd), st_spec],
        out_shape=[sds((b, l, wd), F32), sds(s0.shape, F32)],
        scratch_shapes=[pltpu.VMEM((WKV_BATCH_BLOCK, WKV_PAIRS, 2 * RWKV_HEAD_DIM, 2 * RWKV_HEAD_DIM), F32),
                        pltpu.VMEM((WKV_PAIRS, 2 * c, 2 * RWKV_HEAD_DIM), BF16),
                        pltpu.VMEM((WKV_PAIRS, c, 2 * RWKV_HEAD_DIM), BF16)],
        compiler_params=pltpu.CompilerParams(
            dimension_semantics=("parallel", "arbitrary"), vmem_limit_bytes=V7X_VMEM_LIMIT),
        name="wkv_apply",
    )(kkt, rt, kfh, nbh, vb, tp, lo, rhs0, gend, s0)
    return o, s_fin


def _tail_kernel(o_ref, gate_ref, bonus_ref, x_ref, ys_ref, p_ref, lnw_ref, lnb_ref, woa_ref, wob_ref,
                 nf_ref, wg_ref, wu_ref, wd_ref, np_ref, wpg_ref, wpp_ref, nl_ref, y_ref, new_scr, cur_scr):
    i = pl.program_id(0)

    @pl.when(i == 0)
    def _():
        new_scr[...] = jnp.zeros(new_scr.shape, BF16)

    cur_scr[...] = new_scr[...]

    parts = 4
    rows = o_ref.shape[0] // parts
    inv_n = 1.0 / RWKV_HEAD_DIM

    def vector_half(part):
        rs = slice(part * rows, (part + 1) * rows)
        o = o_ref[rs, :]
        mu = _head_sums(o) * inv_n
        d = o - mu
        var = _head_sums(d * d) * inv_n
        on = d * lax.rsqrt(var + GN_EPS) * lnw_ref[...] + lnb_ref[...]
        new_scr[rs, :] = ((on + bonus_ref[rs, :]) * gate_ref[rs, :]).astype(BF16)

    vector_half(0)
    y_rwkv = cur_scr[...]
    h = x_ref[...] + _bdot(ys_ref[...], woa_ref[...]) + jnp.dot(y_rwkv, wob_ref[...],
                                                                preferred_element_type=F32)
    hf = _rms(h, nf_ref[...]).astype(BF16)
    vector_half(1)
    gate = jnp.dot(hf, wg_ref[...], preferred_element_type=F32)
    up = jnp.dot(hf, wu_ref[...], preferred_element_type=F32)
    vector_half(2)
    h = h + _bdot(_silu(gate) * up, wd_ref[...])
    vector_half(3)
    pg = _sigmoid(_bdot(_rms(h, np_ref[...]), wpg_ref[...]))
    h = h + pg * _bdot(p_ref[...], wpp_ref[...])
    y_ref[...] = _rms(h, nl_ref[...])


def layer_tail(o, gate, bonus, x, y_ssd, p, consts, *, tm):
    n = x.shape[0]
    nt = n // tm
    ahead = lambda w_: pl.BlockSpec((tm, w_), lambda i: (jnp.minimum(i, nt - 1), 0))
    behind = lambda w_: pl.BlockSpec((tm, w_), lambda i: (jnp.maximum(i - 1, 0), 0))
    return pl.pallas_call(
        _tail_kernel,
        grid=(nt + 1,),
        in_specs=[ahead(D_MODEL)] * 3 + [behind(D_MODEL)] * 2 + [behind(PLE_DIM)]
                 + [_const_spec(t.shape) for t in consts],
        out_specs=behind(D_MODEL),
        out_shape=jax.ShapeDtypeStruct((n, D_MODEL), F32),
        scratch_shapes=[pltpu.VMEM((tm, D_MODEL), BF16), pltpu.VMEM((tm, D_MODEL), BF16)],
        compiler_params=pltpu.CompilerParams(
            dimension_semantics=("arbitrary",), vmem_limit_bytes=V7X_VMEM_LIMIT),
        name="layer_tail",
    )(o, gate, bonus, x, y_ssd, p, *consts)


def _prepare_weights(w):
    c0, c1, c2 = SSD_WIDTH, SSD_WIDTH + SSD_CONV_DIM, SSD_WIDTH + SSD_CONV_DIM + SSD_HEADS
    w_in = w["w_in"]
    rowv = lambda t: t.reshape(1, -1)
    return dict(
        w,
        wz=w_in[:, :c0].astype(BF16), wx=w_in[:, c0:c1].astype(BF16),
        wdt=w_in[:, c1:c2].astype(BF16), wr=w_in[:, c2:].astype(BF16),
        woa=w["w_out"][:SSD_WIDTH].astype(BF16), wob=w["w_out"][SSD_WIDTH:].astype(BF16),
        wg=w["w_gate"].astype(BF16), wu=w["w_up"].astype(BF16), wd=w["w_down"].astype(BF16),
        wpg=w["w_ple_gate"].astype(BF16), wpp=w["w_ple_proj"].astype(BF16),
        norm_mix_r=rowv(w["norm_mix"]), norm_ffn_r=rowv(w["norm_ffn"]),
        norm_ple_r=rowv(w["norm_ple"]), norm_final_r=rowv(w["norm_final"]),
        ln_x_w_r=rowv(w["ln_x_w"]), ln_x_b_r=rowv(w["ln_x_b"]),
    )


def layer_forward(x, p, conv0, shift0, ssm0, wkv0, w, *, tm, ssd_q, prep_tt, wkv_steps):
    b, l, _ = x.shape
    n = b * l
    x2 = x.reshape(n, D_MODEL)
    chunked = l % WKV_CHUNK == 0 and l % tm == 0
    if chunked:
        z, xbc, dt, dtt, *ops, gate, bonus, shift_new = proj_prep(
            x, shift0, w["norm_mix_r"], w["wz"], w["wx"], w["wr"], w["wdt"], w, tm=tm, chunk=WKV_CHUNK)
    else:
        z, xbc, rw, dt, dtt = in_projection(x2, w["norm_mix_r"], w["wz"], w["wx"], w["wr"], w["wdt"], tm=tm)
        *ops, gate, bonus, shift_new = rwkv_prep(rw.reshape(b, l, -1), shift0, w, tt=prep_tt, chunk=0)
    y_ssd, ssm_new, conv_new = ssd_mixer(
        z.reshape(b, l, -1), xbc.reshape(b, l, -1), dt.reshape(b, l, -1), dtt, conv0, ssm0, w, q=ssd_q)
    if chunked:
        o, wkv_new = wkv_chunked(*ops, wkv0)
    elif b % WKV_LANE_BATCH == 0:
        o, wkv_new = wkv_scan_batch_lanes(*ops, wkv0)
    else:
        o, wkv_new = wkv_scan(*ops, wkv0, steps=wkv_steps)
    flat = lambda t: t.reshape(n, -1)
    tail_consts = [w["ln_x_w_r"], w["ln_x_b_r"], w["woa"], w["wob"], w["norm_ffn_r"], w["wg"], w["wu"],
                   w["wd"], w["norm_ple_r"], w["wpg"], w["wpp"], w["norm_final_r"]]
    y = layer_tail(flat(o), flat(gate), flat(bonus), x2, flat(y_ssd), p.reshape(n, PLE_DIM),
                   tail_consts, tm=tm)
    return y.reshape(b, l, D_MODEL), ssm_new, conv_new, wkv_new, shift_new


def kernel(x_prompt, x_sample, state_ssm, state_conv, state_wkv, state_shift, p_prompt, p_sample, norm_mix, w_in, conv_w, conv_b, dt_bias, a_log, d_skip, ssd_norm, shift_mu, w0, w2, a0, a2, g2, k_k, k_a, r_k, ln_x_w, ln_x_b, w_out, norm_ffn, w_gate, w_up, w_down, norm_ple, w_ple_gate, w_ple_proj, norm_final):
    w = _prepare_weights(dict(
        norm_mix=norm_mix[0], w_in=w_in[0], conv_w=conv_w[0], conv_b=conv_b[0], dt_bias=dt_bias[0],
        a_log=a_log[0], d_skip=d_skip[0], ssd_norm=ssd_norm[0], shift_mu=shift_mu[0], w0=w0[0],
        w2=w2[0], a0=a0[0], a2=a2[0], g2=g2[0], k_k=k_k[0], k_a=k_a[0], r_k=r_k[0],
        ln_x_w=ln_x_w[0], ln_x_b=ln_x_b[0], w_out=w_out[0], norm_ffn=norm_ffn[0],
        w_gate=w_gate[0], w_up=w_up[0], w_down=w_down[0], norm_ple=norm_ple[0],
        w_ple_gate=w_ple_gate[0], w_ple_proj=w_ple_proj[0], norm_final=norm_final))
    bp = x_prompt.shape[0]
    zeros = lambda *s: jnp.zeros(s, F32)
    yp, s1, c1, k1, t1 = layer_forward(
        x_prompt, p_prompt[0], zeros(bp, SSD_CONV - 1, SSD_CONV_DIM), zeros(bp, 1, RWKV_PROJ),
        zeros(bp, SSD_HEADS, SSD_HEAD_DIM, SSD_STATE),
        zeros(bp, RWKV_HEADS, RWKV_HEAD_DIM, RWKV_HEAD_DIM), w,
        tm=256, ssd_q=min(SSD_CHUNK, x_prompt.shape[1]), prep_tt=min(128, x_prompt.shape[1]),
        wkv_steps=min(16, x_prompt.shape[1]))
    ys, s2, c2, k2, t2 = layer_forward(
        x_sample, p_sample[0], state_conv[0], state_shift[0], state_ssm[0], state_wkv[0], w,
        tm=256, ssd_q=x_sample.shape[1], prep_tt=128, wkv_steps=x_sample.shape[1])
    return (yp, ys, s1[None], c1[None], k1[None], t1[None], s2[None], c2[None], k2[None], t2[None])
```

```python
import functools

import jax
import jax.numpy as jnp
from jax import lax
from jax.experimental import pallas as pl
from jax.experimental.pallas import tpu as pltpu

F32 = jnp.float32
BF16 = jnp.bfloat16
HIGHEST = lax.Precision.HIGHEST

D_MODEL = 1024
SSD_WIDTH = 1024
SSD_HEADS = 16
SSD_HEAD_DIM = 64
SSD_GROUPS = 2
SSD_GROUP_WIDTH = SSD_WIDTH // SSD_GROUPS
SSD_STATE = 128
SSD_CONV = 4
SSD_CHUNK = 128
SSD_BC = SSD_GROUPS * SSD_STATE
SSD_CONV_DIM = SSD_WIDTH + 2 * SSD_BC
RWKV_WIDTH = 1024
RWKV_HEADS = 16
RWKV_HEAD_DIM = 64
DECAY_LORA = 64
AAA_LORA = 64
GATE_LORA = 128
RWKV_PROJ = 3 * RWKV_WIDTH + DECAY_LORA + AAA_LORA + GATE_LORA
D_FF = 2816
PLE_DIM = 256
NORM_EPS = 1e-6
GN_EPS = 64e-5

WKV_BATCH_BLOCK = 8
V7X_VMEM_LIMIT = 56 * 1024 * 1024
CONV_PAD = 8
SSD_SEQS_PER_STEP = 4


def _rms(x, g):
    return x * lax.rsqrt(jnp.mean(x * x, axis=-1, keepdims=True) + NORM_EPS) * g


def _sigmoid(x):
    return 1.0 / (1.0 + jnp.exp(-x))


def _silu(x):
    return x * _sigmoid(x)


def _softplus(x):
    return jnp.maximum(x, 0.0) + jnp.log(1.0 + jnp.exp(-jnp.abs(x)))


def _bdot(a, b):
    return jnp.dot(a.astype(BF16), b.astype(BF16), preferred_element_type=F32)


def _split3(t):
    hi = t.astype(BF16)
    r1 = t - hi.astype(F32)
    mid = r1.astype(BF16)
    lo = (r1 - mid.astype(F32)).astype(BF16)
    return hi, mid, lo


def _dot01(a, b, *, exact_side):
    if exact_side == "lhs":
        m = b.astype(BF16)
        return sum(jnp.dot(p, m, preferred_element_type=F32) for p in _split3(a))
    m = a.astype(BF16)
    return sum(jnp.dot(m, p, preferred_element_type=F32) for p in _split3(b))


def _const_spec(shape):
    return pl.BlockSpec(shape, lambda *_: (0,) * len(shape), pipeline_mode=pl.Buffered(1))


def _head_expand(rows):
    h = lax.broadcasted_iota(jnp.int32, (rows, SSD_WIDTH), 0)
    c = lax.broadcasted_iota(jnp.int32, (rows, SSD_WIDTH), 1)
    return (c // SSD_HEAD_DIM == h).astype(F32)


def _proj_kernel(x_ref, g_ref, wz_ref, wx_ref, wr_ref, wdt_ref, wdtt_ref,
                 z_ref, xbc_ref, rw_ref, dt_ref, dtt_ref):
    u = _rms(x_ref[...], g_ref[...]).astype(BF16)
    z_ref[...] = jnp.dot(u, wz_ref[...], preferred_element_type=F32)
    xbc_ref[...] = jnp.dot(u, wx_ref[...], preferred_element_type=F32)
    rw_ref[...] = jnp.dot(u, wr_ref[...], preferred_element_type=F32)
    dt_ref[...] = jnp.dot(u, wdt_ref[...], preferred_element_type=F32)
    dtt_ref[...] = lax.dot_general(wdtt_ref[...], u, (((1,), (1,)), ((), ())), preferred_element_type=F32)


def in_projection(x, g, wz, wx, wr, wdt, *, tm):
    n = x.shape[0]
    row = lambda w: pl.BlockSpec((tm, w), lambda i: (i, 0))
    wdtt = wdt.T
    return pl.pallas_call(
        _proj_kernel,
        grid=(n // tm,),
        in_specs=[row(D_MODEL), _const_spec((1, D_MODEL)), _const_spec(wz.shape),
                  _const_spec(wx.shape), _const_spec(wr.shape), _const_spec(wdt.shape),
                  _const_spec(wdtt.shape)],
        out_specs=[row(SSD_WIDTH), row(SSD_CONV_DIM), row(RWKV_PROJ), row(SSD_HEADS),
                   pl.BlockSpec((SSD_HEADS, tm), lambda i: (0, i))],
        out_shape=[jax.ShapeDtypeStruct((n, SSD_WIDTH), F32),
                   jax.ShapeDtypeStruct((n, SSD_CONV_DIM), F32),
                   jax.ShapeDtypeStruct((n, RWKV_PROJ), F32),
                   jax.ShapeDtypeStruct((n, SSD_HEADS), F32),
                   jax.ShapeDtypeStruct((SSD_HEADS, n), F32)],
        compiler_params=pltpu.CompilerParams(
            dimension_semantics=("parallel",), vmem_limit_bytes=V7X_VMEM_LIMIT),
        name="in_projection",
    )(x, g, wz, wx, wr, wdt, wdtt)


def _ssd_kernel(z_ref, xbc_ref, dt_ref, dtt_ref, hist_ref, h0_ref, cw_ref, cb_ref,
                dtb_ref, dtbt_ref, alog_ref, alogt_ref, dsk_ref, nrm_ref,
                y_ref, hfin_ref, cnew_ref, xfull_scr, h_scr, *, q, nseq, single_chunk):
    refs = (z_ref, xbc_ref, dt_ref, dtt_ref, hist_ref, h0_ref, cw_ref, cb_ref, dtb_ref, dtbt_ref, alog_ref,
            alogt_ref, dsk_ref, nrm_ref, y_ref, hfin_ref, cnew_ref, xfull_scr, h_scr)
    for s in range(nseq):
        _ssd_sequence(s, *refs, q=q, single_chunk=single_chunk)


def _ssd_sequence(s, z_ref, xbc_ref, dt_ref, dtt_ref, hist_ref, h0_ref, cw_ref, cb_ref,
                  dtb_ref, dtbt_ref, alog_ref, alogt_ref, dsk_ref, nrm_ref,
                  y_ref, hfin_ref, cnew_ref, xfull_scr, h_scr, *, q, single_chunk):
    c = pl.program_id(1)
    last = pl.num_programs(1) - 1
    gw = SSD_GROUP_WIDTH

    @pl.when(c == 0)
    def _():
        xfull_scr[s,CONV_PAD - 3:CONV_PAD, :] = hist_ref[s]
        if not single_chunk:
            for g in range(SSD_GROUPS):
                h_scr[s * SSD_GROUPS + g] = h0_ref[s,g * 8:(g + 1) * 8].reshape(gw, SSD_STATE).T

    @pl.when(c > 0)
    def _():
        xfull_scr[s,CONV_PAD - 3:CONV_PAD, :] = xfull_scr[s,CONV_PAD + q - 3:CONV_PAD + q, :]

    xfull_scr[s,CONV_PAD:CONV_PAD + q, :] = xbc_ref[s]

    conv = cb_ref[...]
    for j in range(SSD_CONV):
        lo = CONV_PAD - 3 + j
        conv = conv + xfull_scr[s,lo:lo + q, :] * cw_ref[j:j + 1, :]
    act = _silu(conv)
    xs = act[:, :SSD_WIDTH]

    dt = _softplus(dt_ref[s] + dtb_ref[...])
    dtt_raw = dtt_ref[...] if len(dtt_ref.shape) == 2 else dtt_ref[s]
    dtt = _softplus(dtt_raw + dtbt_ref[...])
    da = dt * -jnp.exp(alog_ref[...])
    dat = dtt * -jnp.exp(alogt_ref[...])
    row = lax.broadcasted_iota(jnp.int32, (q, q), 0)
    col = lax.broadcasted_iota(jnp.int32, (q, q), 1)
    causal = row >= col
    a_cum = _dot01(causal.astype(F32), da, exact_side="rhs")
    a_cumt = _dot01(dat, (row <= col).astype(F32), exact_side="lhs")

    expand = _head_expand(SSD_HEADS)
    a_cum_x = _dot01(a_cum, expand, exact_side="lhs")
    dt_x = _dot01(dt, expand, exact_side="lhs")
    a_end_x = a_cum_x[q - 1:q, :]
    decay_in_x = jnp.exp(a_cum_x)
    chunk_decay_x = jnp.exp(a_end_x)
    xd = xs * (jnp.exp(a_end_x - a_cum_x) * dt_x)

    ys = []
    for g in range(SSD_GROUPS):
        bm = act[:, SSD_WIDTH + g * SSD_STATE:SSD_WIDTH + (g + 1) * SSD_STATE]
        cm = act[:, SSD_WIDTH + SSD_BC + g * SSD_STATE:SSD_WIDTH + SSD_BC + (g + 1) * SSD_STATE]
        cb = lax.dot_general(cm.astype(BF16), bm.astype(BF16), (((1,), (1,)), ((), ())),
                             preferred_element_type=F32)
        y_heads = []
        for e in range(8):
            h = g * 8 + e
            seg = a_cum[:, h:h + 1] - a_cumt[h:h + 1, :]
            lmat = jnp.where(causal, jnp.exp(jnp.where(causal, seg, 0.0)), 0.0)
            w_qs = cb * lmat * dtt[h:h + 1, :]
            y_heads.append(_bdot(w_qs, xs[:, h * SSD_HEAD_DIM:(h + 1) * SSD_HEAD_DIM]))
        y_diag = jnp.concatenate(y_heads, axis=1)
        sl = slice(g * gw, (g + 1) * gw)
        if single_chunk:
            h_in = h0_ref[s,g * 8:(g + 1) * 8].reshape(gw, SSD_STATE)
            y_off = lax.dot_general(cm.astype(BF16), h_in.astype(BF16), (((1,), (1,)), ((), ())),
                                    preferred_element_type=F32)
            upd = lax.dot_general(xd[:, sl].astype(BF16), bm.astype(BF16), (((0,), (0,)), ((), ())),
                                  preferred_element_type=F32)
            head_decay = jnp.broadcast_to(jnp.exp(a_cumt[:, q - 1:q]), (SSD_HEADS, SSD_STATE))
            for e in range(8):
                h = g * 8 + e
                hfin_ref[s,h] = (h0_ref[s,h] * head_decay[h:h + 1, :]
                                  + upd[e * SSD_HEAD_DIM:(e + 1) * SSD_HEAD_DIM, :])
        else:
            h_in = h_scr[s * SSD_GROUPS + g]
            y_off = _bdot(cm, h_in)
            upd = lax.dot_general(bm.astype(BF16), xd[:, sl].astype(BF16), (((0,), (0,)), ((), ())),
                                  preferred_element_type=F32)
            h_scr[s * SSD_GROUPS + g] = h_in * chunk_decay_x[:, sl] + upd
        ys.append(y_diag + y_off * decay_in_x[:, sl])

    y = jnp.concatenate(ys, axis=1) + dsk_ref[...] * xs
    yg = y * _silu(z_ref[s])
    outs = []
    for g in range(SSD_GROUPS):
        t = yg[:, g * gw:(g + 1) * gw]
        outs.append(t * lax.rsqrt(jnp.mean(t * t, axis=-1, keepdims=True) + NORM_EPS))
    y_ref[s] = jnp.concatenate(outs, axis=1) * nrm_ref[...]

    @pl.when(c == last)
    def _():
        cnew_ref[s] = xfull_scr[s,CONV_PAD + q - 3:CONV_PAD + q, :]
        if not single_chunk:
            for g in range(SSD_GROUPS):
                hfin_ref[s,g * 8:(g + 1) * 8] = h_scr[s * SSD_GROUPS + g].T.reshape(8, SSD_HEAD_DIM, SSD_STATE)


def ssd_mixer(z, xbc, dt, dtt_flat, conv0, ssm0, w, *, q):
    b, l, _ = z.shape
    single_chunk = l == q
    nseq = SSD_SEQS_PER_STEP if (single_chunk and b % SSD_SEQS_PER_STEP == 0) else 1
    if q % 128 == 0 and nseq == 1:
        dtt = dtt_flat
        dtt_spec = pl.BlockSpec((SSD_HEADS, q), lambda i, c: (0, i * (l // q) + c))
    else:
        dtt = jnp.swapaxes(dt, 1, 2)
        dtt_spec = pl.BlockSpec((nseq, SSD_HEADS, q), lambda i, c: (i, 0, c))
    seq = lambda wd: pl.BlockSpec((nseq, q, wd), lambda i, c: (i, c, 0))
    per_b3 = lambda s: pl.BlockSpec((nseq,) + s, lambda i, c: (i,) + (0,) * len(s))
    col = lambda t: t.reshape(-1, 1)
    rowv = lambda t: t.reshape(1, -1)
    consts = [w["conv_w"], rowv(w["conv_b"]), rowv(w["dt_bias"]), col(w["dt_bias"]),
              rowv(w["a_log"]), col(w["a_log"]),
              rowv(jnp.repeat(w["d_skip"], SSD_HEAD_DIM)), rowv(w["ssd_norm"])]
    return pl.pallas_call(
        functools.partial(_ssd_kernel, q=q, nseq=nseq, single_chunk=single_chunk),
        grid=(b // nseq, l // q),
        in_specs=[seq(SSD_WIDTH), seq(SSD_CONV_DIM), seq(SSD_HEADS),
                  dtt_spec,
                  per_b3((SSD_CONV - 1, SSD_CONV_DIM)),
                  per_b3((SSD_HEADS, SSD_HEAD_DIM, SSD_STATE))]
                 + [_const_spec(t.shape) for t in consts],
        out_specs=[seq(SSD_WIDTH), per_b3((SSD_HEADS, SSD_HEAD_DIM, SSD_STATE)),
                   per_b3((SSD_CONV - 1, SSD_CONV_DIM))],
        out_shape=[jax.ShapeDtypeStruct((b, l, SSD_WIDTH), F32),
                   jax.ShapeDtypeStruct((b, SSD_HEADS, SSD_HEAD_DIM, SSD_STATE), F32),
                   jax.ShapeDtypeStruct((b, SSD_CONV - 1, SSD_CONV_DIM), F32)],
        scratch_shapes=[pltpu.VMEM((nseq, CONV_PAD + q, SSD_CONV_DIM), F32),
                        pltpu.VMEM((nseq * SSD_GROUPS, SSD_STATE, SSD_GROUP_WIDTH), F32)],
        compiler_params=pltpu.CompilerParams(
            dimension_semantics=("parallel", "arbitrary"), vmem_limit_bytes=V7X_VMEM_LIMIT),
        name="ssd_mixer",
    )(z, xbc, dt, dtt, conv0, ssm0, *consts)


def _head_sums(t):
    pair = 2 * RWKV_HEAD_DIM
    first = lax.broadcasted_iota(jnp.int32, (1, pair), 1) < RWKV_HEAD_DIM
    pieces = []
    for j in range(RWKV_HEADS // 2):
        x = t[:, j * pair:(j + 1) * pair]
        x0 = jnp.where(first, x, 0.0)
        s0 = jnp.sum(x0, axis=-1, keepdims=True)
        s1 = jnp.sum(x - x0, axis=-1, keepdims=True)
        pieces.append(jnp.where(first, s0, s1))
    return jnp.concatenate(pieces, axis=1)


def _rwkv_prep_kernel(rw_ref, sh0_ref, mu_ref, w0_ref, w2_ref, a0_ref, a2_ref, g2_ref,
                      kk_ref, ka_ref, rk_ref,
                      *rest, tt, seqs, chunk):
    outs, full_scr = rest[:-1], rest[-1]
    shn_ref = outs[-1]
    c = pl.program_id(1)
    l = tt // seqs

    if seqs == 1:
        @pl.when(c == 0)
        def _():
            full_scr[0, CONV_PAD - 1:CONV_PAD, :] = sh0_ref[0]

        @pl.when(c > 0)
        def _():
            full_scr[0, CONV_PAD - 1:CONV_PAD, :] = full_scr[0, CONV_PAD + tt - 1:CONV_PAD + tt, :]

        rw = rw_ref[0]
        full_scr[0, CONV_PAD:CONV_PAD + tt, :] = rw
        prev = full_scr[0, CONV_PAD - 1:CONV_PAD - 1 + tt, :]
    else:
        full_scr[:, CONV_PAD - 1:CONV_PAD, :] = sh0_ref[...]
        full_scr[:, CONV_PAD:CONV_PAD + l, :] = rw_ref[...]
        rw = rw_ref[...].reshape(tt, RWKV_PROJ)
        prev = full_scr[:, CONV_PAD - 1:CONV_PAD - 1 + l, :].reshape(tt, RWKV_PROJ)
    vals = _rwkv_mix_math(rw, prev, mu_ref, w0_ref, w2_ref, a0_ref, a2_ref, g2_ref, kk_ref, ka_ref, rk_ref)
    _emit_rwkv_outputs(outs[:-1], vals, tt=tt, chunk=chunk)

    if seqs == 1:
        @pl.when(c == pl.num_programs(1) - 1)
        def _():
            shn_ref[0] = full_scr[0, CONV_PAD + tt - 1:CONV_PAD + tt, :]
    else:
        shn_ref[...] = rw_ref[:, l - 1:l, :]


def _rwkv_mix_math(rw, prev, mu_ref, w0_ref, w2_ref, a0_ref, a2_ref, g2_ref, kk_ref, ka_ref, rk_ref):
    wd = RWKV_WIDTH
    u = rw + (prev - rw) * mu_ref[...]
    r = u[:, :wd]
    k = u[:, wd:2 * wd]
    v = u[:, 2 * wd:3 * wd]
    w_lo = u[:, 3 * wd:3 * wd + DECAY_LORA]
    a_lo = u[:, 3 * wd + DECAY_LORA:3 * wd + DECAY_LORA + AAA_LORA]
    g_lo = u[:, 3 * wd + DECAY_LORA + AAA_LORA:]

    w_log = -_softplus(-(w0_ref[...] + _bdot(jnp.tanh(w_lo), w2_ref[...]))) - 0.5
    lw = -jnp.exp(w_log)
    a = _sigmoid(a0_ref[...] + _bdot(a_lo, a2_ref[...]))
    gate = _bdot(_sigmoid(g_lo), g2_ref[...])

    kk = k * kk_ref[...]
    kk = kk / jnp.maximum(jnp.sqrt(_head_sums(kk * kk)), 1e-12)
    kf = k * (1.0 + (a - 1.0) * ka_ref[...])
    kb = kk * a
    bonus = _head_sums(r * kf * rk_ref[...]) * v
    return r, lw, kf, v, kk, kb, gate, bonus


def _emit_rwkv_outputs(outs, vals, *, tt, chunk, row0=0):
    r, lw, kf, v, kk, kb, gate, bonus = vals
    wd = RWKV_WIDTH
    gate_out, bonus_out = outs[-2:]
    if chunk == 0:
        blk = gate_out.shape
        for ref, val in zip(outs, (r, jnp.exp(lw), kf, v, kk, kb, gate, bonus)):
            ref[...] = val.reshape(blk)
    else:
        gate_out[0, row0:row0 + tt, :] = gate
        bonus_out[0, row0:row0 + tt, :] = bonus
        kkt_out, rt_out, kfh_out, nbh_out, vb_out, kbg_out, kfg_out, gend_out = outs[:8]
        tri = (lax.broadcasted_iota(jnp.int32, (chunk, chunk), 0)
               >= lax.broadcasted_iota(jnp.int32, (chunk, chunk), 1)).astype(F32)
        for ci in range(tt // chunk):
            rs = slice(ci * chunk, (ci + 1) * chunk)
            ro = slice(row0 + ci * chunk, row0 + (ci + 1) * chunk)
            lw_c = lw[rs]
            cum = _dot01(tri, lw_c, exact_side="rhs")
            cum_end = cum[chunk - 1:chunk, :]
            g_inv = jnp.exp(-cum)
            g_tail = jnp.exp(cum_end - cum)
            kkt_out[0, ro, :] = (kk[rs] * jnp.exp(cum - lw_c)).astype(BF16)
            rt_out[0, ro, :] = (r[rs] * jnp.exp(cum)).astype(BF16)
            kfh_out[0, ro, :] = (kf[rs] * g_tail).astype(BF16)
            nbh_out[0, ro, :] = (-kb[rs] * g_tail).astype(BF16)
            vb_out[0, ro, :] = v[rs].astype(BF16)
            kbg_out[0, ro, :] = (kb[rs] * g_inv).astype(BF16)
            kfg_out[0, ro, :] = (kf[rs] * g_inv).astype(BF16)
            gend_out[0, row0 // chunk + ci] = jnp.broadcast_to(jnp.exp(cum_end), (8, wd))


def rwkv_prep(rw, shift0, w, *, tt, chunk):
    b, l, _ = rw.shape
    seqs = max(1, tt // l)
    assert chunk == 0 or (seqs == 1 and tt % chunk == 0)
    rowv = lambda t: t.reshape(1, -1)
    consts = [rowv(w["shift_mu"]), rowv(w["w0"]), w["w2"].astype(BF16), rowv(w["a0"]),
              w["a2"].astype(BF16), w["g2"].astype(BF16), rowv(w["k_k"]), rowv(w["k_a"]),
              rowv(w["r_k"])]
    rows = tt // seqs
    grid = (b // seqs, l // rows)
    seq = lambda wd: pl.BlockSpec((seqs, rows, wd), lambda i, c: (i, c, 0))
    one = pl.BlockSpec((seqs, 1, RWKV_PROJ), lambda i, c: (i, 0, 0))
    sds = jax.ShapeDtypeStruct
    f32_seq = sds((b, l, RWKV_WIDTH), F32)
    if chunk == 0:
        op_specs = [seq(RWKV_WIDTH)] * 6
        op_shapes = [f32_seq] * 6
    else:
        per_tile = tt // chunk
        op_specs = [seq(RWKV_WIDTH)] * 7 + [
            pl.BlockSpec((1, per_tile, 8, RWKV_WIDTH), lambda i, c: (i, c, 0, 0))]
        op_shapes = [sds((b, l, RWKV_WIDTH), BF16)] * 7 + [sds((b, l // chunk, 8, RWKV_WIDTH), F32)]
    outs = pl.pallas_call(
        functools.partial(_rwkv_prep_kernel, tt=tt, seqs=seqs, chunk=chunk),
        grid=grid,
        in_specs=[seq(RWKV_PROJ), one] + [_const_spec(t.shape) for t in consts],
        out_specs=op_specs + [seq(RWKV_WIDTH)] * 2 + [one],
        out_shape=op_shapes + [f32_seq] * 2 + [sds((b, 1, RWKV_PROJ), F32)],
        scratch_shapes=[pltpu.VMEM((seqs, CONV_PAD + rows, RWKV_PROJ), F32)],
        compiler_params=pltpu.CompilerParams(
            dimension_semantics=("parallel", "arbitrary"), vmem_limit_bytes=V7X_VMEM_LIMIT),
        name="rwkv_prep",
    )(rw, shift0, *consts)
    return outs


def _proj_prep_kernel(x_ref, sh0_ref, g_ref, wz_ref, wx_ref, wr_ref, wdt_ref, wdtt_ref,
                      mu_ref, w0_ref, w2_ref, a0_ref, a2_ref, g2_ref, kk_ref, ka_ref, rk_ref,
                      z_ref, xbc_ref, dt_ref, dtt_ref, *rest, tm, tiles_per_seq, chunk):
    outs, (new_scr, cur_scr) = rest[:-2], rest[-2:]
    shn_ref = outs[-1]
    i = pl.program_id(0)

    @pl.when(i == 0)
    def _():
        new_scr[...] = jnp.zeros(new_scr.shape, F32)
        cur_scr[...] = jnp.zeros(cur_scr.shape, F32)

    k = jnp.maximum(i - 1, 0)
    first = (k % tiles_per_seq) == 0
    cur_scr[CONV_PAD - 1:CONV_PAD, :] = jnp.where(first, sh0_ref[0], cur_scr[CONV_PAD + tm - 1:CONV_PAD + tm, :])
    cur_scr[CONV_PAD:CONV_PAD + tm, :] = new_scr[...]

    shn_ref[0] = cur_scr[CONV_PAD + tm - 1:CONV_PAD + tm, :]
    u = _rms(x_ref[...], g_ref[...]).astype(BF16)

    def project(piece):
        if piece == 0:
            z_ref[...] = jnp.dot(u, wz_ref[...], preferred_element_type=F32)
        elif piece == 1:
            xbc_ref[...] = jnp.dot(u, wx_ref[...], preferred_element_type=F32)
        elif piece == 2:
            cols = slice(0, 2 * RWKV_WIDTH)
            new_scr[:, cols] = jnp.dot(u, wr_ref[:, cols], preferred_element_type=F32)
        else:
            cols = slice(2 * RWKV_WIDTH, RWKV_PROJ)
            new_scr[:, cols] = jnp.dot(u, wr_ref[:, cols], preferred_element_type=F32)
            dt_ref[...] = jnp.dot(u, wdt_ref[...], preferred_element_type=F32)
            dtt_ref[...] = lax.dot_general(wdtt_ref[...], u, (((1,), (1,)), ((), ())),
                                           preferred_element_type=F32)

    def prepare(part, rows):
        lo = CONV_PAD + part * rows
        rw = cur_scr[lo:lo + rows, :]
        prev = cur_scr[lo - 1:lo - 1 + rows, :]
        vals = _rwkv_mix_math(rw, prev, mu_ref, w0_ref, w2_ref, a0_ref, a2_ref, g2_ref, kk_ref, ka_ref, rk_ref)
        _emit_rwkv_outputs(outs[:-1], vals, tt=rows, chunk=chunk, row0=part * rows)

    pieces = 4
    parts = min(pieces, tm // chunk)
    for part in range(parts):
        for piece in range(part * pieces // parts, (part + 1) * pieces // parts):
            project(piece)
        prepare(part, tm // parts)


def proj_prep(x, shift0, g, wz, wx, wr, wdt, w, *, tm, chunk):
    b, l, _ = x.shape
    n = b * l
    nt, tps = n // tm, l // tm
    assert l % tm == 0 and tm % chunk == 0
    x2 = x.reshape(n, D_MODEL)
    rowv = lambda t: t.reshape(1, -1)
    wdtt = wdt.T
    consts = [g, wz, wx, wr, wdt, wdtt,
              rowv(w["shift_mu"]), rowv(w["w0"]), w["w2"].astype(BF16), rowv(w["a0"]),
              w["a2"].astype(BF16), w["g2"].astype(BF16), rowv(w["k_k"]), rowv(w["k_a"]), rowv(w["r_k"])]
    ahead = lambda i: jnp.minimum(i, nt - 1)
    behind = lambda i: jnp.maximum(i - 1, 0)
    row_a = lambda w_: pl.BlockSpec((tm, w_), lambda i: (ahead(i), 0))
    seq_b = lambda w_: pl.BlockSpec((1, tm, w_), lambda i: (behind(i) // tps, behind(i) % tps, 0))
    one_b = pl.BlockSpec((1, 1, RWKV_PROJ), lambda i: (behind(i) // tps, 0, 0))
    per_tile = tm // chunk
    gend_spec = pl.BlockSpec((1, per_tile, 8, RWKV_WIDTH), lambda i: (behind(i) // tps, behind(i) % tps, 0, 0))
    sds = jax.ShapeDtypeStruct
    outs = pl.pallas_call(
        functools.partial(_proj_prep_kernel, tm=tm, tiles_per_seq=tps, chunk=chunk),
        grid=(nt + 1,),
        in_specs=[row_a(D_MODEL), one_b] + [_const_spec(t.shape) for t in consts],
        out_specs=[row_a(SSD_WIDTH), row_a(SSD_CONV_DIM), row_a(SSD_HEADS),
                   pl.BlockSpec((SSD_HEADS, tm), lambda i: (0, ahead(i)))]
                  + [seq_b(RWKV_WIDTH)] * 7 + [gend_spec] + [seq_b(RWKV_WIDTH)] * 2 + [one_b],
        out_shape=[sds((n, SSD_WIDTH), F32), sds((n, SSD_CONV_DIM), F32), sds((n, SSD_HEADS), F32),
                   sds((SSD_HEADS, n), F32)]
                  + [sds((b, l, RWKV_WIDTH), BF16)] * 7 + [sds((b, l // chunk, 8, RWKV_WIDTH), F32)]
                  + [sds((b, l, RWKV_WIDTH), F32)] * 2 + [sds((b, 1, RWKV_PROJ), F32)],
        scratch_shapes=[pltpu.VMEM((tm, RWKV_PROJ), F32), pltpu.VMEM((CONV_PAD + tm, RWKV_PROJ), F32)],
        compiler_params=pltpu.CompilerParams(
            dimension_semantics=("arbitrary",), vmem_limit_bytes=V7X_VMEM_LIMIT),
        name="proj_prep",
    )(x2, shift0, *consts)
    return outs


def _wkv_kernel(r_ref, w_ref, k_ref, v_ref, kk_ref, kka_ref, s0_ref,
                o_ref, sfin_ref, s_scr, vt_scr, ot_scr, *, steps):
    c = pl.program_id(1)
    n = RWKV_HEAD_DIM
    lanes = WKV_BATCH_BLOCK * RWKV_HEADS

    @pl.when(c == 0)
    def _():
        s_scr[...] = s0_ref[...].reshape(lanes, n * n).T.reshape(n, n, lanes)

    def to_pairs(ref, t):
        return ref[:, t].reshape(lanes, n).T

    def step(t, carry):
        r_t = to_pairs(r_ref, t)
        w_t = to_pairs(w_ref, t)
        k_t = to_pairs(k_ref, t)
        kk_t = to_pairs(kk_ref, t)
        kka_t = to_pairs(kka_ref, t)
        vt_scr[...] = to_pairs(v_ref, t)

        def per_value(vi, carry2):
            s_v = s_scr[vi]
            skk = jnp.sum(s_v * kk_t, axis=0, keepdims=True)
            v_row = vt_scr[pl.ds(vi, 1), :]
            s_new = s_v * w_t - skk * kka_t + v_row * k_t
            s_scr[vi] = s_new
            ot_scr[pl.ds(vi, 1), :] = jnp.sum(s_new * r_t, axis=0, keepdims=True)
            return carry2

        lax.fori_loop(0, n, per_value, 0, unroll=4)
        o_ref[:, t] = ot_scr[...].T.reshape(WKV_BATCH_BLOCK, RWKV_HEADS, n)
        return carry

    lax.fori_loop(0, steps, step, 0)

    @pl.when(c == pl.num_programs(1) - 1)
    def _():
        sfin_ref[...] = s_scr[...].reshape(n * n, lanes).T.reshape(
            WKV_BATCH_BLOCK, RWKV_HEADS, n, n)


def wkv_scan(r, w, k, v, kk, kka, s0, *, steps):
    b, l, _ = r.shape
    h, n = RWKV_HEADS, RWKV_HEAD_DIM
    assert b % WKV_BATCH_BLOCK == 0 and l % steps == 0
    ops = [t.reshape(b, l, h, n) for t in (r, w, k, v, kk, kka)]
    seq_spec = pl.BlockSpec((WKV_BATCH_BLOCK, steps, h, n), lambda g, c: (g, c, 0, 0))
    st_spec = pl.BlockSpec((WKV_BATCH_BLOCK, h, n, n), lambda g, c: (g, 0, 0, 0))
    o, s_fin = pl.pallas_call(
        functools.partial(_wkv_kernel, steps=steps),
        grid=(b // WKV_BATCH_BLOCK, l // steps),
        in_specs=[seq_spec] * 6 + [st_spec],
        out_specs=[seq_spec, st_spec],
        out_shape=[jax.ShapeDtypeStruct((b, l, h, n), F32),
                   jax.ShapeDtypeStruct((b, h, n, n), F32)],
        scratch_shapes=[pltpu.VMEM((n, n, WKV_BATCH_BLOCK * h), F32),
                        pltpu.VMEM((n, WKV_BATCH_BLOCK * h), F32),
                        pltpu.VMEM((n, WKV_BATCH_BLOCK * h), F32)],
        compiler_params=pltpu.CompilerParams(
            dimension_semantics=("parallel", "arbitrary"), vmem_limit_bytes=V7X_VMEM_LIMIT),
        name="wkv_scan",
    )(*ops, s0)
    return o.reshape(b, l, h * n), s_fin


WKV_LANE_BATCH = 128


def _wkv_batch_lanes_kernel(r_ref, w_ref, k_ref, v_ref, kk_ref, kka_ref, s0_ref,
                            o_ref, sfin_ref, op_scr, ot_scr, *, steps):
    n = RWKV_HEAD_DIM
    nb = WKV_LANE_BATCH
    sfin_ref[...] = s0_ref[...]

    def step(t, carry):
        rows = pl.ds(t, nb, stride=steps)
        for i, ref in enumerate((r_ref, w_ref, k_ref, kk_ref, kka_ref, v_ref)):
            op_scr[i] = ref[rows, :].T
        for h2 in range(2):
            ch = slice(h2 * n, (h2 + 1) * n)
            r_t = op_scr[0, ch, :]
            kka_r = jnp.sum(op_scr[4, ch, :] * r_t, axis=0, keepdims=True)
            k_r = jnp.sum(op_scr[2, ch, :] * r_t, axis=0, keepdims=True)
            op_scr[0, ch, :] = op_scr[1, ch, :] * r_t

            def per_value(vi, carry2, ch=ch, h2=h2, kka_r=kka_r, k_r=k_r):
                s_v = sfin_ref[h2, vi]
                skk = jnp.sum(s_v * op_scr[3, ch, :], axis=0, keepdims=True)
                out = jnp.sum(s_v * op_scr[0, ch, :], axis=0, keepdims=True)
                v_row = op_scr[5, pl.ds(h2 * n + vi, 1), :]
                sfin_ref[h2, vi] = s_v * op_scr[1, ch, :] - skk * op_scr[4, ch, :] + v_row * op_scr[2, ch, :]
                ot_scr[pl.ds(h2 * n + vi, 1), :] = out - skk * kka_r + v_row * k_r
                return carry2

            lax.fori_loop(0, n, per_value, 0, unroll=4)
        o_ref[rows, :] = ot_scr[...].T
        return carry

    lax.fori_loop(0, steps, step, 0)


def wkv_scan_batch_lanes(r, w, k, v, kk, kka, s0):
    b, l, wd = r.shape
    h, n, nb = RWKV_HEADS, RWKV_HEAD_DIM, WKV_LANE_BATCH
    assert b % nb == 0
    ops = [t.reshape(b * l, wd) for t in (r, w, k, v, kk, kka)]
    s0t = jnp.transpose(s0, (1, 2, 3, 0))
    seq_spec = pl.BlockSpec((nb * l, 2 * n), lambda g, j: (g, j))
    st_spec = pl.BlockSpec((2, n, n, nb), lambda g, j: (j, 0, 0, g))
    o, s_fin = pl.pallas_call(
        functools.partial(_wkv_batch_lanes_kernel, steps=l),
        grid=(b // nb, h // 2),
        in_specs=[seq_spec] * 6 + [st_spec],
        out_specs=[seq_spec, st_spec],
        out_shape=[jax.ShapeDtypeStruct((b * l, wd), F32), jax.ShapeDtypeStruct((h, n, n, b), F32)],
        scratch_shapes=[pltpu.VMEM((6, 2 * n, nb), F32), pltpu.VMEM((2 * n, nb), F32)],
        compiler_params=pltpu.CompilerParams(
            dimension_semantics=("parallel", "parallel"), vmem_limit_bytes=V7X_VMEM_LIMIT),
        name="wkv_scan_batch_lanes",
    )(*ops, s0t)
    return o.reshape(b, l, wd), jnp.transpose(s_fin, (3, 0, 1, 2))


WKV_CHUNK = 64
WKV_PAIRS = RWKV_HEADS // 2
WKV_ROW_STRIDE = WKV_CHUNK + 8
WKV_SOLVE_ROWS = 8
WKV_SOLVE_COLS = 16


def _pair_masks():
    c = WKV_CHUNK
    row = lax.broadcasted_iota(jnp.int32, (2 * c, 2 * c), 0)
    col = lax.broadcasted_iota(jnp.int32, (2 * c, 2 * c), 1)
    t, i = row % c, col % c
    keep = i <= t - jnp.where(row < c, 1, 0)
    sign = jnp.where(row >= c, jnp.where(col < c, -1.0, 1.0), 1.0)
    block_diag = row // c == col // c
    return keep, sign, block_diag


def _wkv_prepare_kernel(kkt_ref, rt_ref, kbg_ref, kfg_ref, vb_ref,
                        lo_ref, rhs0_ref, tp_ref, abs_scr, top_scr, abt_scr, tt_scr):
    c = WKV_CHUNK
    keep, sign, _ = _pair_masks()
    lane = lax.broadcasted_iota(jnp.int32, (1, 2 * c), 1)
    head0 = lane < RWKV_HEAD_DIM
    row_head0 = lax.broadcasted_iota(jnp.int32, (2 * RWKV_HEAD_DIM, 1), 0) < RWKV_HEAD_DIM
    zeros = jnp.zeros((c, 2 * c), BF16)

    def per_batch(b, carry):
        for j in range(WKV_PAIRS):
            sl = slice(j * 2 * RWKV_HEAD_DIM, (j + 1) * 2 * RWKV_HEAD_DIM)
            lhs = jnp.concatenate([kkt_ref[b, :, sl], rt_ref[b, :, sl]], axis=0)
            rhs = jnp.concatenate([kbg_ref[b, :, sl], kfg_ref[b, :, sl]], axis=0)
            rhs_t = rhs.astype(F32).T
            top = jnp.where(row_head0, rhs_t, 0.0)
            w_a = jnp.concatenate([top, rhs_t - top], axis=1).astype(BF16)
            a_both = jnp.dot(lhs, w_a, preferred_element_type=F32)
            for h2 in range(2):
                a = a_both[:, h2 * 2 * c:(h2 + 1) * 2 * c]
                a = jnp.where(keep, a, 0.0) * sign
                inst = h2 * (WKV_BATCH_BLOCK * WKV_PAIRS) + b * WKV_PAIRS + j
                abs_scr[pl.ds(inst * WKV_ROW_STRIDE, c), :c] = a[:c, :c]
                top_scr[j, :, h2 * 2 * c:(h2 + 1) * 2 * c] = a[:c].astype(BF16)
                lo_ref[b, :, (2 * j + h2) * 2 * c:(2 * j + h2 + 1) * 2 * c] = a[c:].astype(BF16)

        for j in range(WKV_PAIRS):
            sl = slice(j * 2 * RWKV_HEAD_DIM, (j + 1) * 2 * RWKV_HEAD_DIM)
            v = vb_ref[b, :, sl]
            v0 = jnp.where(head0, v, jnp.zeros_like(v))
            w_akf = jnp.concatenate([zeros, v0, zeros, v - v0], axis=0)
            rhs0_ref[b, :, sl] = jnp.dot(top_scr[j], w_akf, preferred_element_type=F32)
        return carry

    lax.fori_loop(0, WKV_BATCH_BLOCK, per_batch, 0)

    n_inst = 2 * WKV_BATCH_BLOCK * WKV_PAIRS
    n_pair_rows = WKV_BATCH_BLOCK * WKV_PAIRS

    def to_lanes(t, carry):
        abt_scr[t] = abs_scr[pl.ds(t, n_inst, stride=WKV_ROW_STRIDE), :][:, :c].T
        return carry

    lax.fori_loop(0, c, to_lanes, 0, unroll=8)

    tt_scr[...] = jnp.zeros(tt_scr.shape, F32)
    sub_iota = lax.broadcasted_iota(jnp.int32, (WKV_SOLVE_COLS, n_inst), 0)
    rows = range(WKV_SOLVE_ROWS)
    for cb in range(c // WKV_SOLVE_COLS):
        col0 = WKV_SOLVE_COLS * cb
        cols = slice(col0, col0 + WKV_SOLVE_COLS)
        first_block = col0 // WKV_SOLVE_ROWS

        def solve_rows(tb, carry, col0=col0, cols=cols, first_block=first_block):
            t0 = tb * WKV_SOLVE_ROWS

            def sub(ib, accs):
                ps = [tt_scr[ib * WKV_SOLVE_ROWS + di, cols, :] for di in rows]
                out = []
                for r in rows:
                    terms = [abt_scr[t0 + r, pl.ds(ib * WKV_SOLVE_ROWS + di, 1), :] * ps[di] for di in rows]
                    while len(terms) > 1:
                        terms = [a + b for a, b in zip(terms[::2], terms[1::2])]
                    out.append(accs[r] - terms[0])
                return tuple(out)

            unit = tuple(jnp.where(sub_iota + col0 == t0 + r, 1.0, 0.0) for r in rows)
            accs = list(lax.fori_loop(first_block, tb, sub, unit))
            for r in rows:
                for r2 in range(r):
                    accs[r] = accs[r] - abt_scr[t0 + r, pl.ds(t0 + r2, 1), :] * accs[r2]
                tt_scr[t0 + r, cols, :] = accs[r]
            return carry

        lax.fori_loop(first_block, c // WKV_SOLVE_ROWS, solve_rows, 0)

    def from_lanes(t, carry):
        m = tt_scr[t].T
        abs_scr[pl.ds(t, n_pair_rows, stride=WKV_ROW_STRIDE), :] = jnp.concatenate(
            [m[:n_pair_rows], m[n_pair_rows:]], axis=1)
        return carry

    lax.fori_loop(0, c, from_lanes, 0, unroll=8)

    def emit(b, carry):
        for j in range(WKV_PAIRS):
            row0 = (b * WKV_PAIRS + j) * WKV_ROW_STRIDE
            tp_ref[b, :, j * 2 * c:(j + 1) * 2 * c] = abs_scr[pl.ds(row0, c), :].astype(BF16)
        return carry

    lax.fori_loop(0, WKV_BATCH_BLOCK, emit, 0)


def _wkv_apply_kernel(kkt_ref, rt_ref, kfh_ref, nbh_ref, vb_ref, tp_ref, lo_ref, rhs0_ref, gend_ref, s0_ref,
                      o_ref, sfin_ref, x_scr, wp_scr, p_scr):
    c = WKV_CHUNK
    n = RWKV_HEAD_DIM
    ch = pl.program_id(1)
    _, _, block_diag = _pair_masks()
    lane = lax.broadcasted_iota(jnp.int32, (1, 2 * c), 1)
    head0 = lane < n
    eye2 = (lax.broadcasted_iota(jnp.int32, (n, 2 * n), 0)
            == lax.broadcasted_iota(jnp.int32, (n, 2 * n), 1) % n).astype(F32)

    @pl.when(ch == 0)
    def _():
        def init(b, carry):
            for j in range(WKV_PAIRS):
                sp = s0_ref[b, 2 * j:2 * j + 2].reshape(2 * n, n)
                dup = jnp.dot(sp, eye2, precision=HIGHEST, preferred_element_type=F32)
                x_scr[b, j] = jnp.where(block_diag, dup, 0.0)
            return carry
        lax.fori_loop(0, WKV_BATCH_BLOCK, init, 0)

    def per_batch(b, carry):
        for j in range(WKV_PAIRS):
            sl = slice(j * 2 * n, (j + 1) * 2 * n)
            lhs = jnp.concatenate([kkt_ref[b, :, sl], rt_ref[b, :, sl]], axis=0)
            kx = lax.dot_general(lhs, x_scr[b, j].astype(BF16), (((1,), (1,)), ((), ())),
                                 preferred_element_type=F32)
            rhs = kx[:c] + rhs0_ref[b, :, sl]
            r0 = jnp.where(head0, rhs, 0.0)
            wp_scr[j] = jnp.concatenate([r0, rhs - r0], axis=0).astype(BF16)
            o_ref[b, :, sl] = kx[c:]
        for j in range(WKV_PAIRS):
            sl = slice(j * 2 * n, (j + 1) * 2 * n)
            p = jnp.dot(tp_ref[b, :, sl], wp_scr[j], preferred_element_type=F32)
            p_scr[j] = p.astype(BF16)
        for j in range(WKV_PAIRS):
            sl = slice(j * 2 * n, (j + 1) * 2 * n)
            v = vb_ref[b, :, sl]
            pb = p_scr[j]
            zero = jnp.zeros_like(pb)
            p0, v0 = jnp.where(head0, pb, zero), jnp.where(head0, v, zero)
            w_o = jnp.concatenate([p0, v0, pb - p0, v - v0], axis=0)
            o_ref[b, :, sl] = o_ref[b, :, sl] + jnp.dot(
                lo_ref[b, :, j * 4 * c:(j + 1) * 4 * c], w_o, preferred_element_type=F32)
            vp = jnp.concatenate([v, pb], axis=0)
            kb = jnp.concatenate([kfh_ref[b, :, sl], nbh_ref[b, :, sl]], axis=0)
            upd = lax.dot_general(vp, kb, (((0,), (0,)), ((), ())), preferred_element_type=F32)
            x_scr[b, j] = jnp.where(block_diag, x_scr[b, j] * gend_ref[b, 0, 0:1, sl] + upd, 0.0)
        return carry

    lax.fori_loop(0, WKV_BATCH_BLOCK, per_batch, 0)

    @pl.when(ch == pl.num_programs(1) - 1)
    def _():
        def fin(b, carry):
            for j in range(WKV_PAIRS):
                sp = lax.dot_general(x_scr[b, j], eye2, (((1,), (1,)), ((), ())),
                                     precision=HIGHEST, preferred_element_type=F32)
                sfin_ref[b, 2 * j:2 * j + 2] = sp.reshape(2, n, n)
            return carry
        lax.fori_loop(0, WKV_BATCH_BLOCK, fin, 0)


def wkv_chunked(kkt, rt, kfh, nbh, vb, kbg, kfg, gend, s0):
    b, l, wd = kkt.shape
    c = WKV_CHUNK
    assert b % WKV_BATCH_BLOCK == 0 and l % c == 0
    gb, nc = b // WKV_BATCH_BLOCK, l // c
    lanes = 2 * WKV_BATCH_BLOCK * WKV_PAIRS
    seq = lambda w_: pl.BlockSpec((WKV_BATCH_BLOCK, c, w_), lambda g, i: (g, i, 0))
    gend_spec = pl.BlockSpec((WKV_BATCH_BLOCK, 1, 8, wd), lambda g, i: (g, i, 0, 0))
    sds = jax.ShapeDtypeStruct
    lo, rhs0, tp = pl.pallas_call(
        _wkv_prepare_kernel,
        grid=(gb, nc),
        in_specs=[seq(wd)] * 5,
        out_specs=[seq(2 * wd), seq(wd), seq(wd)],
        out_shape=[sds((b, l, 2 * wd), BF16), sds((b, l, wd), F32), sds((b, l, wd), BF16)],
        scratch_shapes=[pltpu.VMEM((lanes * WKV_ROW_STRIDE, 2 * c), F32),
                        pltpu.VMEM((WKV_PAIRS, c, 4 * c), BF16),
                        pltpu.VMEM((c, c, lanes), F32),
                        pltpu.VMEM((c, c, lanes), F32)],
        compiler_params=pltpu.CompilerParams(
            dimension_semantics=("parallel", "parallel"), vmem_limit_bytes=V7X_VMEM_LIMIT),
        name="wkv_prepare",
    )(kkt, rt, kbg, kfg, vb)
    st_spec = pl.BlockSpec((WKV_BATCH_BLOCK, RWKV_HEADS, RWKV_HEAD_DIM, RWKV_HEAD_DIM),
                           lambda g, i: (g, 0, 0, 0))
    o, s_fin = pl.pallas_call(
        _wkv_apply_kernel,
        grid=(gb, nc),
        in_specs=[seq(wd)] * 6 + [seq(2 * wd), seq(wd), gend_spec, st_spec],
        out_specs=[seq(wd), st_spec],
        out_shape=[sds((b, l, wd), F32), sds(s0.shape, F32)],
        scratch_shapes=[pltpu.VMEM((WKV_BATCH_BLOCK, WKV_PAIRS, 2 * RWKV_HEAD_DIM, 2 * RWKV_HEAD_DIM), F32),
                        pltpu.VMEM((WKV_PAIRS, 2 * c, 2 * RWKV_HEAD_DIM), BF16),
                        pltpu.VMEM((WKV_PAIRS, c, 2 * RWKV_HEAD_DIM), BF16)],
        compiler_params=pltpu.CompilerParams(
            dimension_semantics=("parallel", "arbitrary"), vmem_limit_bytes=V7X_VMEM_LIMIT),
        name="wkv_apply",
    )(kkt, rt, kfh, nbh, vb, tp, lo, rhs0, gend, s0)
    return o, s_fin


def _tail_kernel(o_ref, gate_ref, bonus_ref, x_ref, ys_ref, p_ref, lnw_ref, lnb_ref, woa_ref, wob_ref,
                 nf_ref, wg_ref, wu_ref, wd_ref, np_ref, wpg_ref, wpp_ref, nl_ref, y_ref, new_scr, cur_scr):
    i = pl.program_id(0)

    @pl.when(i == 0)
    def _():
        new_scr[...] = jnp.zeros(new_scr.shape, BF16)

    cur_scr[...] = new_scr[...]

    parts = 4
    rows = o_ref.shape[0] // parts
    inv_n = 1.0 / RWKV_HEAD_DIM

    def vector_half(part):
        rs = slice(part * rows, (part + 1) * rows)
        o = o_ref[rs, :]
        mu = _head_sums(o) * inv_n
        d = o - mu
        var = _head_sums(d * d) * inv_n
        on = d * lax.rsqrt(var + GN_EPS) * lnw_ref[...] + lnb_ref[...]
        new_scr[rs, :] = ((on + bonus_ref[rs, :]) * gate_ref[rs, :]).astype(BF16)

    vector_half(0)
    y_rwkv = cur_scr[...]
    h = x_ref[...] + _bdot(ys_ref[...], woa_ref[...]) + jnp.dot(y_rwkv, wob_ref[...],
                                                                preferred_element_type=F32)
    hf = _rms(h, nf_ref[...]).astype(BF16)
    vector_half(1)
    gate = jnp.dot(hf, wg_ref[...], preferred_element_type=F32)
    up = jnp.dot(hf, wu_ref[...], preferred_element_type=F32)
    vector_half(2)
    h = h + _bdot(_silu(gate) * up, wd_ref[...])
    vector_half(3)
    pg = _sigmoid(_bdot(_rms(h, np_ref[...]), wpg_ref[...]))
    h = h + pg * _bdot(p_ref[...], wpp_ref[...])
    y_ref[...] = _rms(h, nl_ref[...])


def layer_tail(o, gate, bonus, x, y_ssd, p, consts, *, tm):
    n = x.shape[0]
    nt = n // tm
    ahead = lambda w_: pl.BlockSpec((tm, w_), lambda i: (jnp.minimum(i, nt - 1), 0))
    behind = lambda w_: pl.BlockSpec((tm, w_), lambda i: (jnp.maximum(i - 1, 0), 0))
    return pl.pallas_call(
        _tail_kernel,
        grid=(nt + 1,),
        in_specs=[ahead(D_MODEL)] * 3 + [behind(D_MODEL)] * 2 + [behind(PLE_DIM)]
                 + [_const_spec(t.shape) for t in consts],
        out_specs=behind(D_MODEL),
        out_shape=jax.ShapeDtypeStruct((n, D_MODEL), F32),
        scratch_shapes=[pltpu.VMEM((tm, D_MODEL), BF16), pltpu.VMEM((tm, D_MODEL), BF16)],
        compiler_params=pltpu.CompilerParams(
            dimension_semantics=("arbitrary",), vmem_limit_bytes=V7X_VMEM_LIMIT),
        name="layer_tail",
    )(o, gate, bonus, x, y_ssd, p, *consts)


def _prepare_weights(w):
    c0, c1, c2 = SSD_WIDTH, SSD_WIDTH + SSD_CONV_DIM, SSD_WIDTH + SSD_CONV_DIM + SSD_HEADS
    w_in = w["w_in"]
    rowv = lambda t: t.reshape(1, -1)
    return dict(
        w,
        wz=w_in[:, :c0].astype(BF16), wx=w_in[:, c0:c1].astype(BF16),
        wdt=w_in[:, c1:c2].astype(BF16), wr=w_in[:, c2:].astype(BF16),
        woa=w["w_out"][:SSD_WIDTH].astype(BF16), wob=w["w_out"][SSD_WIDTH:].astype(BF16),
        wg=w["w_gate"].astype(BF16), wu=w["w_up"].astype(BF16), wd=w["w_down"].astype(BF16),
        wpg=w["w_ple_gate"].astype(BF16), wpp=w["w_ple_proj"].astype(BF16),
        norm_mix_r=rowv(w["norm_mix"]), norm_ffn_r=rowv(w["norm_ffn"]),
        norm_ple_r=rowv(w["norm_ple"]), norm_final_r=rowv(w["norm_final"]),
        ln_x_w_r=rowv(w["ln_x_w"]), ln_x_b_r=rowv(w["ln_x_b"]),
    )


def layer_forward(x, p, conv0, shift0, ssm0, wkv0, w, *, tm, ssd_q, prep_tt, wkv_steps):
    b, l, _ = x.shape
    n = b * l
    x2 = x.reshape(n, D_MODEL)
    chunked = l % WKV_CHUNK == 0 and l % tm == 0
    if chunked:
        z, xbc, dt, dtt, *ops, gate, bonus, shift_new = proj_prep(
            x, shift0, w["norm_mix_r"], w["wz"], w["wx"], w["wr"], w["wdt"], w, tm=tm, chunk=WKV_CHUNK)
    else:
        z, xbc, rw, dt, dtt = in_projection(x2, w["norm_mix_r"], w["wz"], w["wx"], w["wr"], w["wdt"], tm=tm)
        *ops, gate, bonus, shift_new = rwkv_prep(rw.reshape(b, l, -1), shift0, w, tt=prep_tt, chunk=0)
    y_ssd, ssm_new, conv_new = ssd_mixer(
        z.reshape(b, l, -1), xbc.reshape(b, l, -1), dt.reshape(b, l, -1), dtt, conv0, ssm0, w, q=ssd_q)
    if chunked:
        o, wkv_new = wkv_chunked(*ops, wkv0)
    elif b % WKV_LANE_BATCH == 0:
        o, wkv_new = wkv_scan_batch_lanes(*ops, wkv0)
    else:
        o, wkv_new = wkv_scan(*ops, wkv0, steps=wkv_steps)
    flat = lambda t: t.reshape(n, -1)
    tail_consts = [w["ln_x_w_r"], w["ln_x_b_r"], w["woa"], w["wob"], w["norm_ffn_r"], w["wg"], w["wu"],
                   w["wd"], w["norm_ple_r"], w["wpg"], w["wpp"], w["norm_final_r"]]
    y = layer_tail(flat(o), flat(gate), flat(bonus), x2, flat(y_ssd), p.reshape(n, PLE_DIM),
                   tail_consts, tm=tm)
    return y.reshape(b, l, D_MODEL), ssm_new, conv_new, wkv_new, shift_new


def kernel(x_prompt, x_sample, state_ssm, state_conv, state_wkv, state_shift, p_prompt, p_sample, norm_mix, w_in, conv_w, conv_b, dt_bias, a_log, d_skip, ssd_norm, shift_mu, w0, w2, a0, a2, g2, k_k, k_a, r_k, ln_x_w, ln_x_b, w_out, norm_ffn, w_gate, w_up, w_down, norm_ple, w_ple_gate, w_ple_proj, norm_final):
    w = _prepare_weights(dict(
        norm_mix=norm_mix[0], w_in=w_in[0], conv_w=conv_w[0], conv_b=conv_b[0], dt_bias=dt_bias[0],
        a_log=a_log[0], d_skip=d_skip[0], ssd_norm=ssd_norm[0], shift_mu=shift_mu[0], w0=w0[0],
        w2=w2[0], a0=a0[0], a2=a2[0], g2=g2[0], k_k=k_k[0], k_a=k_a[0], r_k=r_k[0],
        ln_x_w=ln_x_w[0], ln_x_b=ln_x_b[0], w_out=w_out[0], norm_ffn=norm_ffn[0],
        w_gate=w_gate[0], w_up=w_up[0], w_down=w_down[0], norm_ple=norm_ple[0],
        w_ple_gate=w_ple_gate[0], w_ple_proj=w_ple_proj[0], norm_final=norm_final))
    bp = x_prompt.shape[0]
    zeros = lambda *s: jnp.zeros(s, F32)
    yp, s1, c1, k1, t1 = layer_forward(
        x_prompt, p_prompt[0], zeros(bp, SSD_CONV - 1, SSD_CONV_DIM), zeros(bp, 1, RWKV_PROJ),
        zeros(bp, SSD_HEADS, SSD_HEAD_DIM, SSD_STATE),
        zeros(bp, RWKV_HEADS, RWKV_HEAD_DIM, RWKV_HEAD_DIM), w,
        tm=256, ssd_q=min(SSD_CHUNK, x_prompt.shape[1]), prep_tt=min(128, x_prompt.shape[1]),
        wkv_steps=min(16, x_prompt.shape[1]))
    ys, s2, c2, k2, t2 = layer_forward(
        x_sample, p_sample[0], state_conv[0], state_shift[0], state_ssm[0], state_wkv[0], w,
        tm=256, ssd_q=x_sample.shape[1], prep_tt=128, wkv_steps=x_sample.shape[1])
    return (yp, ys, s1[None], c1[None], k1[None], t1[None], s2[None], c2[None], k2[None], t2[None])
```

```python
import functools

import jax
import jax.numpy as jnp
from jax import lax
from jax.experimental import pallas as pl
from jax.experimental.pallas import tpu as pltpu

F32 = jnp.float32
BF16 = jnp.bfloat16
HIGHEST = lax.Precision.HIGHEST

D_MODEL = 1024
SSD_WIDTH = 1024
SSD_HEADS = 16
SSD_HEAD_DIM = 64
SSD_GROUPS = 2
SSD_GROUP_WIDTH = SSD_WIDTH // SSD_GROUPS
SSD_STATE = 128
SSD_CONV = 4
SSD_CHUNK = 128
SSD_BC = SSD_GROUPS * SSD_STATE
SSD_CONV_DIM = SSD_WIDTH + 2 * SSD_BC
RWKV_WIDTH = 1024
RWKV_HEADS = 16
RWKV_HEAD_DIM = 64
DECAY_LORA = 64
AAA_LORA = 64
GATE_LORA = 128
RWKV_PROJ = 3 * RWKV_WIDTH + DECAY_LORA + AAA_LORA + GATE_LORA
D_FF = 2816
PLE_DIM = 256
NORM_EPS = 1e-6
GN_EPS = 64e-5

WKV_BATCH_BLOCK = 8
V7X_VMEM_LIMIT = 56 * 1024 * 1024
CONV_PAD = 8
SSD_SEQS_PER_STEP = 4


def _rms(x, g):
    return x * lax.rsqrt(jnp.mean(x * x, axis=-1, keepdims=True) + NORM_EPS) * g


def _sigmoid(x):
    return 1.0 / (1.0 + jnp.exp(-x))


def _silu(x):
    return x * _sigmoid(x)


def _softplus(x):
    return jnp.maximum(x, 0.0) + jnp.log(1.0 + jnp.exp(-jnp.abs(x)))


def _bdot(a, b):
    return jnp.dot(a.astype(BF16), b.astype(BF16), preferred_element_type=F32)


def _split3(t):
    hi = t.astype(BF16)
    r1 = t - hi.astype(F32)
    mid = r1.astype(BF16)
    lo = (r1 - mid.astype(F32)).astype(BF16)
    return hi, mid, lo


def _dot01(a, b, *, exact_side):
    if exact_side == "lhs":
        m = b.astype(BF16)
        return sum(jnp.dot(p, m, preferred_element_type=F32) for p in _split3(a))
    m = a.astype(BF16)
    return sum(jnp.dot(m, p, preferred_element_type=F32) for p in _split3(b))


def _const_spec(shape):
    return pl.BlockSpec(shape, lambda *_: (0,) * len(shape), pipeline_mode=pl.Buffered(1))


def _head_expand(rows):
    h = lax.broadcasted_iota(jnp.int32, (rows, SSD_WIDTH), 0)
    c = lax.broadcasted_iota(jnp.int32, (rows, SSD_WIDTH), 1)
    return (c // SSD_HEAD_DIM == h).astype(F32)


def _proj_kernel(x_ref, g_ref, wz_ref, wx_ref, wr_ref, wdt_ref, wdtt_ref,
                 z_ref, xbc_ref, rw_ref, dt_ref, dtt_ref):
    u = _rms(x_ref[...], g_ref[...]).astype(BF16)
    z_ref[...] = jnp.dot(u, wz_ref[...], preferred_element_type=F32)
    xbc_ref[...] = jnp.dot(u, wx_ref[...], preferred_element_type=F32)
    rw_ref[...] = jnp.dot(u, wr_ref[...], preferred_element_type=F32)
    dt_ref[...] = jnp.dot(u, wdt_ref[...], preferred_element_type=F32)
    dtt_ref[...] = lax.dot_general(wdtt_ref[...], u, (((1,), (1,)), ((), ())), preferred_element_type=F32)


def in_projection(x, g, wz, wx, wr, wdt, *, tm):
    n = x.shape[0]
    row = lambda w: pl.BlockSpec((tm, w), lambda i: (i, 0))
    wdtt = wdt.T
    return pl.pallas_call(
        _proj_kernel,
        grid=(n // tm,),
        in_specs=[row(D_MODEL), _const_spec((1, D_MODEL)), _const_spec(wz.shape),
                  _const_spec(wx.shape), _const_spec(wr.shape), _const_spec(wdt.shape),
                  _const_spec(wdtt.shape)],
        out_specs=[row(SSD_WIDTH), row(SSD_CONV_DIM), row(RWKV_PROJ), row(SSD_HEADS),
                   pl.BlockSpec((SSD_HEADS, tm), lambda i: (0, i))],
        out_shape=[jax.ShapeDtypeStruct((n, SSD_WIDTH), F32),
                   jax.ShapeDtypeStruct((n, SSD_CONV_DIM), F32),
                   jax.ShapeDtypeStruct((n, RWKV_PROJ), F32),
                   jax.ShapeDtypeStruct((n, SSD_HEADS), F32),
                   jax.ShapeDtypeStruct((SSD_HEADS, n), F32)],
        compiler_params=pltpu.CompilerParams(
            dimension_semantics=("parallel",), vmem_limit_bytes=V7X_VMEM_LIMIT),
        name="in_projection",
    )(x, g, wz, wx, wr, wdt, wdtt)


def _ssd_kernel(z_ref, xbc_ref, dt_ref, dtt_ref, hist_ref, h0_ref, cw_ref, cb_ref,
                dtb_ref, dtbt_ref, alog_ref, alogt_ref, dsk_ref, nrm_ref,
                y_ref, hfin_ref, cnew_ref, xfull_scr, h_scr, *, q, nseq, single_chunk):
    refs = (z_ref, xbc_ref, dt_ref, dtt_ref, hist_ref, h0_ref, cw_ref, cb_ref, dtb_ref, dtbt_ref, alog_ref,
            alogt_ref, dsk_ref, nrm_ref, y_ref, hfin_ref, cnew_ref, xfull_scr, h_scr)
    for s in range(nseq):
        _ssd_sequence(s, *refs, q=q, single_chunk=single_chunk)


def _ssd_sequence(s, z_ref, xbc_ref, dt_ref, dtt_ref, hist_ref, h0_ref, cw_ref, cb_ref,
                  dtb_ref, dtbt_ref, alog_ref, alogt_ref, dsk_ref, nrm_ref,
                  y_ref, hfin_ref, cnew_ref, xfull_scr, h_scr, *, q, single_chunk):
    c = pl.program_id(1)
    last = pl.num_programs(1) - 1
    gw = SSD_GROUP_WIDTH

    @pl.when(c == 0)
    def _():
        xfull_scr[s,CONV_PAD - 3:CONV_PAD, :] = hist_ref[s]
        if not single_chunk:
            for g in range(SSD_GROUPS):
                h_scr[s * SSD_GROUPS + g] = h0_ref[s,g * 8:(g + 1) * 8].reshape(gw, SSD_STATE).T

    @pl.when(c > 0)
    def _():
        xfull_scr[s,CONV_PAD - 3:CONV_PAD, :] = xfull_scr[s,CONV_PAD + q - 3:CONV_PAD + q, :]

    xfull_scr[s,CONV_PAD:CONV_PAD + q, :] = xbc_ref[s]

    conv = cb_ref[...]
    for j in range(SSD_CONV):
        lo = CONV_PAD - 3 + j
        conv = conv + xfull_scr[s,lo:lo + q, :] * cw_ref[j:j + 1, :]
    act = _silu(conv)
    xs = act[:, :SSD_WIDTH]

    dt = _softplus(dt_ref[s] + dtb_ref[...])
    dtt_raw = dtt_ref[...] if len(dtt_ref.shape) == 2 else dtt_ref[s]
    dtt = _softplus(dtt_raw + dtbt_ref[...])
    da = dt * -jnp.exp(alog_ref[...])
    dat = dtt * -jnp.exp(alogt_ref[...])
    row = lax.broadcasted_iota(jnp.int32, (q, q), 0)
    col = lax.broadcasted_iota(jnp.int32, (q, q), 1)
    causal = row >= col
    a_cum = _dot01(causal.astype(F32), da, exact_side="rhs")
    a_cumt = _dot01(dat, (row <= col).astype(F32), exact_side="lhs")

    expand = _head_expand(SSD_HEADS)
    a_cum_x = _dot01(a_cum, expand, exact_side="lhs")
    dt_x = _dot01(dt, expand, exact_side="lhs")
    a_end_x = a_cum_x[q - 1:q, :]
    decay_in_x = jnp.exp(a_cum_x)
    chunk_decay_x = jnp.exp(a_end_x)
    xd = xs * (jnp.exp(a_end_x - a_cum_x) * dt_x)

    ys = []
    for g in range(SSD_GROUPS):
        bm = act[:, SSD_WIDTH + g * SSD_STATE:SSD_WIDTH + (g + 1) * SSD_STATE]
        cm = act[:, SSD_WIDTH + SSD_BC + g * SSD_STATE:SSD_WIDTH + SSD_BC + (g + 1) * SSD_STATE]
        cb = lax.dot_general(cm.astype(BF16), bm.astype(BF16), (((1,), (1,)), ((), ())),
                             preferred_element_type=F32)
        y_heads = []
        for e in range(8):
            h = g * 8 + e
            seg = a_cum[:, h:h + 1] - a_cumt[h:h + 1, :]
            lmat = jnp.where(causal, jnp.exp(jnp.where(causal, seg, 0.0)), 0.0)
            w_qs = cb * lmat * dtt[h:h + 1, :]
            y_heads.append(_bdot(w_qs, xs[:, h * SSD_HEAD_DIM:(h + 1) * SSD_HEAD_DIM]))
        y_diag = jnp.concatenate(y_heads, axis=1)
        sl = slice(g * gw, (g + 1) * gw)
        if single_chunk:
            h_in = h0_ref[s,g * 8:(g + 1) * 8].reshape(gw, SSD_STATE)
            y_off = lax.dot_general(cm.astype(BF16), h_in.astype(BF16), (((1,), (1,)), ((), ())),
                                    preferred_element_type=F32)
            upd = lax.dot_general(xd[:, sl].astype(BF16), bm.astype(BF16), (((0,), (0,)), ((), ())),
                                  preferred_element_type=F32)
            head_decay = jnp.broadcast_to(jnp.exp(a_cumt[:, q - 1:q]), (SSD_HEADS, SSD_STATE))
            for e in range(8):
                h = g * 8 + e
                hfin_ref[s,h] = (h0_ref[s,h] * head_decay[h:h + 1, :]
                                  + upd[e * SSD_HEAD_DIM:(e + 1) * SSD_HEAD_DIM, :])
        else:
            h_in = h_scr[s * SSD_GROUPS + g]
            y_off = _bdot(cm, h_in)
            upd = lax.dot_general(bm.astype(BF16), xd[:, sl].astype(BF16), (((0,), (0,)), ((), ())),
                                  preferred_element_type=F32)
            h_scr[s * SSD_GROUPS + g] = h_in * chunk_decay_x[:, sl] + upd
        ys.append(y_diag + y_off * decay_in_x[:, sl])

    y = jnp.concatenate(ys, axis=1) + dsk_ref[...] * xs
    yg = y * _silu(z_ref[s])
    outs = []
    for g in range(SSD_GROUPS):
        t = yg[:, g * gw:(g + 1) * gw]
        outs.append(t * lax.rsqrt(jnp.mean(t * t, axis=-1, keepdims=True) + NORM_EPS))
    y_ref[s] = jnp.concatenate(outs, axis=1) * nrm_ref[...]

    @pl.when(c == last)
    def _():
        cnew_ref[s] = xfull_scr[s,CONV_PAD + q - 3:CONV_PAD + q, :]
        if not single_chunk:
            for g in range(SSD_GROUPS):
                hfin_ref[s,g * 8:(g + 1) * 8] = h_scr[s * SSD_GROUPS + g].T.reshape(8, SSD_HEAD_DIM, SSD_STATE)


def ssd_mixer(z, xbc, dt, dtt_flat, conv0, ssm0, w, *, q):
    b, l, _ = z.shape
    single_chunk = l == q
    nseq = SSD_SEQS_PER_STEP if (single_chunk and b % SSD_SEQS_PER_STEP == 0) else 1
    if q % 128 == 0 and nseq == 1:
        dtt = dtt_flat
        dtt_spec = pl.BlockSpec((SSD_HEADS, q), lambda i, c: (0, i * (l // q) + c))
    else:
        dtt = jnp.swapaxes(dt, 1, 2)
        dtt_spec = pl.BlockSpec((nseq, SSD_HEADS, q), lambda i, c: (i, 0, c))
    seq = lambda wd: pl.BlockSpec((nseq, q, wd), lambda i, c: (i, c, 0))
    per_b3 = lambda s: pl.BlockSpec((nseq,) + s, lambda i, c: (i,) + (0,) * len(s))
    col = lambda t: t.reshape(-1, 1)
    rowv = lambda t: t.reshape(1, -1)
    consts = [w["conv_w"], rowv(w["conv_b"]), rowv(w["dt_bias"]), col(w["dt_bias"]),
              rowv(w["a_log"]), col(w["a_log"]),
              rowv(jnp.repeat(w["d_skip"], SSD_HEAD_DIM)), rowv(w["ssd_norm"])]
    return pl.pallas_call(
        functools.partial(_ssd_kernel, q=q, nseq=nseq, single_chunk=single_chunk),
        grid=(b // nseq, l // q),
        in_specs=[seq(SSD_WIDTH), seq(SSD_CONV_DIM), seq(SSD_HEADS),
                  dtt_spec,
                  per_b3((SSD_CONV - 1, SSD_CONV_DIM)),
                  per_b3((SSD_HEADS, SSD_HEAD_DIM, SSD_STATE))]
                 + [_const_spec(t.shape) for t in consts],
        out_specs=[seq(SSD_WIDTH), per_b3((SSD_HEADS, SSD_HEAD_DIM, SSD_STATE)),
                   per_b3((SSD_CONV - 1, SSD_CONV_DIM))],
        out_shape=[jax.ShapeDtypeStruct((b, l, SSD_WIDTH), F32),
                   jax.ShapeDtypeStruct((b, SSD_HEADS, SSD_HEAD_DIM, SSD_STATE), F32),
                   jax.ShapeDtypeStruct((b, SSD_CONV - 1, SSD_CONV_DIM), F32)],
        scratch_shapes=[pltpu.VMEM((nseq, CONV_PAD + q, SSD_CONV_DIM), F32),
                        pltpu.VMEM((nseq * SSD_GROUPS, SSD_STATE, SSD_GROUP_WIDTH), F32)],
        compiler_params=pltpu.CompilerParams(
            dimension_semantics=("parallel", "arbitrary"), vmem_limit_bytes=V7X_VMEM_LIMIT),
        name="ssd_mixer",
    )(z, xbc, dt, dtt, conv0, ssm0, *consts)


def _head_sums(t):
    pair = 2 * RWKV_HEAD_DIM
    first = lax.broadcasted_iota(jnp.int32, (1, pair), 1) < RWKV_HEAD_DIM
    pieces = []
    for j in range(RWKV_HEADS // 2):
        x = t[:, j * pair:(j + 1) * pair]
        x0 = jnp.where(first, x, 0.0)
        s0 = jnp.sum(x0, axis=-1, keepdims=True)
        s1 = jnp.sum(x - x0, axis=-1, keepdims=True)
        pieces.append(jnp.where(first, s0, s1))
    return jnp.concatenate(pieces, axis=1)


def _rwkv_prep_kernel(rw_ref, sh0_ref, mu_ref, w0_ref, w2_ref, a0_ref, a2_ref, g2_ref,
                      kk_ref, ka_ref, rk_ref,
                      *rest, tt, seqs, chunk):
    outs, full_scr = rest[:-1], rest[-1]
    shn_ref = outs[-1]
    c = pl.program_id(1)
    l = tt // seqs

    if seqs == 1:
        @pl.when(c == 0)
        def _():
            full_scr[0, CONV_PAD - 1:CONV_PAD, :] = sh0_ref[0]

        @pl.when(c > 0)
        def _():
            full_scr[0, CONV_PAD - 1:CONV_PAD, :] = full_scr[0, CONV_PAD + tt - 1:CONV_PAD + tt, :]

        rw = rw_ref[0]
        full_scr[0, CONV_PAD:CONV_PAD + tt, :] = rw
        prev = full_scr[0, CONV_PAD - 1:CONV_PAD - 1 + tt, :]
    else:
        full_scr[:, CONV_PAD - 1:CONV_PAD, :] = sh0_ref[...]
        full_scr[:, CONV_PAD:CONV_PAD + l, :] = rw_ref[...]
        rw = rw_ref[...].reshape(tt, RWKV_PROJ)
        prev = full_scr[:, CONV_PAD - 1:CONV_PAD - 1 + l, :].reshape(tt, RWKV_PROJ)
    vals = _rwkv_mix_math(rw, prev, mu_ref, w0_ref, w2_ref, a0_ref, a2_ref, g2_ref, kk_ref, ka_ref, rk_ref)
    _emit_rwkv_outputs(outs[:-1], vals, tt=tt, chunk=chunk)

    if seqs == 1:
        @pl.when(c == pl.num_programs(1) - 1)
        def _():
            shn_ref[0] = full_scr[0, CONV_PAD + tt - 1:CONV_PAD + tt, :]
    else:
        shn_ref[...] = rw_ref[:, l - 1:l, :]


def _rwkv_mix_math(rw, prev, mu_ref, w0_ref, w2_ref, a0_ref, a2_ref, g2_ref, kk_ref, ka_ref, rk_ref):
    wd = RWKV_WIDTH
    u = rw + (prev - rw) * mu_ref[...]
    r = u[:, :wd]
    k = u[:, wd:2 * wd]
    v = u[:, 2 * wd:3 * wd]
    w_lo = u[:, 3 * wd:3 * wd + DECAY_LORA]
    a_lo = u[:, 3 * wd + DECAY_LORA:3 * wd + DECAY_LORA + AAA_LORA]
    g_lo = u[:, 3 * wd + DECAY_LORA + AAA_LORA:]

    w_log = -_softplus(-(w0_ref[...] + _bdot(jnp.tanh(w_lo), w2_ref[...]))) - 0.5
    lw = -jnp.exp(w_log)
    a = _sigmoid(a0_ref[...] + _bdot(a_lo, a2_ref[...]))
    gate = _bdot(_sigmoid(g_lo), g2_ref[...])

    kk = k * kk_ref[...]
    kk = kk / jnp.maximum(jnp.sqrt(_head_sums(kk * kk)), 1e-12)
    kf = k * (1.0 + (a - 1.0) * ka_ref[...])
    kb = kk * a
    bonus = _head_sums(r * kf * rk_ref[...]) * v
    return r, lw, kf, v, kk, kb, gate, bonus


def _emit_rwkv_outputs(outs, vals, *, tt, chunk, row0=0):
    r, lw, kf, v, kk, kb, gate, bonus = vals
    wd = RWKV_WIDTH
    gate_out, bonus_out = outs[-2:]
    if chunk == 0:
        blk = gate_out.shape
        for ref, val in zip(outs, (r, jnp.exp(lw), kf, v, kk, kb, gate, bonus)):
            ref[...] = val.reshape(blk)
    else:
        gate_out[0, row0:row0 + tt, :] = gate
        bonus_out[0, row0:row0 + tt, :] = bonus
        kkt_out, rt_out, kfh_out, nbh_out, vb_out, kbg_out, kfg_out, gend_out = outs[:8]
        tri = (lax.broadcasted_iota(jnp.int32, (chunk, chunk), 0)
               >= lax.broadcasted_iota(jnp.int32, (chunk, chunk), 1)).astype(F32)
        for ci in range(tt // chunk):
            rs = slice(ci * chunk, (ci + 1) * chunk)
            ro = slice(row0 + ci * chunk, row0 + (ci + 1) * chunk)
            lw_c = lw[rs]
            cum = _dot01(tri, lw_c, exact_side="rhs")
            cum_end = cum[chunk - 1:chunk, :]
            g_inv = jnp.exp(-cum)
            g_tail = jnp.exp(cum_end - cum)
            kkt_out[0, ro, :] = (kk[rs] * jnp.exp(cum - lw_c)).astype(BF16)
            rt_out[0, ro, :] = (r[rs] * jnp.exp(cum)).astype(BF16)
            kfh_out[0, ro, :] = (kf[rs] * g_tail).astype(BF16)
            nbh_out[0, ro, :] = (-kb[rs] * g_tail).astype(BF16)
            vb_out[0, ro, :] = v[rs].astype(BF16)
            kbg_out[0, ro, :] = (kb[rs] * g_inv).astype(BF16)
            kfg_out[0, ro, :] = (kf[rs] * g_inv).astype(BF16)
            gend_out[0, row0 // chunk + ci] = jnp.broadcast_to(jnp.exp(cum_end), (8, wd))


def rwkv_prep(rw, shift0, w, *, tt, chunk):
    b, l, _ = rw.shape
    seqs = max(1, tt // l)
    assert chunk == 0 or (seqs == 1 and tt % chunk == 0)
    rowv = lambda t: t.reshape(1, -1)
    consts = [rowv(w["shift_mu"]), rowv(w["w0"]), w["w2"].astype(BF16), rowv(w["a0"]),
              w["a2"].astype(BF16), w["g2"].astype(BF16), rowv(w["k_k"]), rowv(w["k_a"]),
              rowv(w["r_k"])]
    rows = tt // seqs
    grid = (b // seqs, l // rows)
    seq = lambda wd: pl.BlockSpec((seqs, rows, wd), lambda i, c: (i, c, 0))
    one = pl.BlockSpec((seqs, 1, RWKV_PROJ), lambda i, c: (i, 0, 0))
    sds = jax.ShapeDtypeStruct
    f32_seq = sds((b, l, RWKV_WIDTH), F32)
    if chunk == 0:
        op_specs = [seq(RWKV_WIDTH)] * 6
        op_shapes = [f32_seq] * 6
    else:
        per_tile = tt // chunk
        op_specs = [seq(RWKV_WIDTH)] * 7 + [
            pl.BlockSpec((1, per_tile, 8, RWKV_WIDTH), lambda i, c: (i, c, 0, 0))]
        op_shapes = [sds((b, l, RWKV_WIDTH), BF16)] * 7 + [sds((b, l // chunk, 8, RWKV_WIDTH), F32)]
    outs = pl.pallas_call(
        functools.partial(_rwkv_prep_kernel, tt=tt, seqs=seqs, chunk=chunk),
        grid=grid,
        in_specs=[seq(RWKV_PROJ), one] + [_const_spec(t.shape) for t in consts],
        out_specs=op_specs + [seq(RWKV_WIDTH)] * 2 + [one],
        out_shape=op_shapes + [f32_seq] * 2 + [sds((b, 1, RWKV_PROJ), F32)],
        scratch_shapes=[pltpu.VMEM((seqs, CONV_PAD + rows, RWKV_PROJ), F32)],
        compiler_params=pltpu.CompilerParams(
            dimension_semantics=("parallel", "arbitrary"), vmem_limit_bytes=V7X_VMEM_LIMIT),
        name="rwkv_prep",
    )(rw, shift0, *consts)
    return outs


def _proj_prep_kernel(x_ref, sh0_ref, g_ref, wz_ref, wx_ref, wr_ref, wdt_ref, wdtt_ref,
                      mu_ref, w0_ref, w2_ref, a0_ref, a2_ref, g2_ref, kk_ref, ka_ref, rk_ref,
                      z_ref, xbc_ref, dt_ref, dtt_ref, *rest, tm, tiles_per_seq, chunk):
    outs, (new_scr, cur_scr) = rest[:-2], rest[-2:]
    shn_ref = outs[-1]
    i = pl.program_id(0)

    @pl.when(i == 0)
    def _():
        new_scr[...] = jnp.zeros(new_scr.shape, F32)
        cur_scr[...] = jnp.zeros(cur_scr.shape, F32)

    k = jnp.maximum(i - 1, 0)
    first = (k % tiles_per_seq) == 0
    cur_scr[CONV_PAD - 1:CONV_PAD, :] = jnp.where(first, sh0_ref[0], cur_scr[CONV_PAD + tm - 1:CONV_PAD + tm, :])
    cur_scr[CONV_PAD:CONV_PAD + tm, :] = new_scr[...]

    shn_ref[0] = cur_scr[CONV_PAD + tm - 1:CONV_PAD + tm, :]
    u = _rms(x_ref[...], g_ref[...]).astype(BF16)

    def project(piece):
        if piece == 0:
            z_ref[...] = jnp.dot(u, wz_ref[...], preferred_element_type=F32)
        elif piece == 1:
            xbc_ref[...] = jnp.dot(u, wx_ref[...], preferred_element_type=F32)
        elif piece == 2:
            cols = slice(0, 2 * RWKV_WIDTH)
            new_scr[:, cols] = jnp.dot(u, wr_ref[:, cols], preferred_element_type=F32)
        else:
            cols = slice(2 * RWKV_WIDTH, RWKV_PROJ)
            new_scr[:, cols] = jnp.dot(u, wr_ref[:, cols], preferred_element_type=F32)
            dt_ref[...] = jnp.dot(u, wdt_ref[...], preferred_element_type=F32)
            dtt_ref[...] = lax.dot_general(wdtt_ref[...], u, (((1,), (1,)), ((), ())),
                                           preferred_element_type=F32)

    def prepare(part, rows):
        lo = CONV_PAD + part * rows
        rw = cur_scr[lo:lo + rows, :]
        prev = cur_scr[lo - 1:lo - 1 + rows, :]
        vals = _rwkv_mix_math(rw, prev, mu_ref, w0_ref, w2_ref, a0_ref, a2_ref, g2_ref, kk_ref, ka_ref, rk_ref)
        _emit_rwkv_outputs(outs[:-1], vals, tt=rows, chunk=chunk, row0=part * rows)

    pieces = 4
    parts = min(pieces, tm // chunk)
    for part in range(parts):
        for piece in range(part * pieces // parts, (part + 1) * pieces // parts):
            project(piece)
        prepare(part, tm // parts)


def proj_prep(x, shift0, g, wz, wx, wr, wdt, w, *, tm, chunk):
    b, l, _ = x.shape
    n = b * l
    nt, tps = n // tm, l // tm
    assert l % tm == 0 and tm % chunk == 0
    x2 = x.reshape(n, D_MODEL)
    rowv = lambda t: t.reshape(1, -1)
    wdtt = wdt.T
    consts = [g, wz, wx, wr, wdt, wdtt,
              rowv(w["shift_mu"]), rowv(w["w0"]), w["w2"].astype(BF16), rowv(w["a0"]),
              w["a2"].astype(BF16), w["g2"].astype(BF16), rowv(w["k_k"]), rowv(w["k_a"]), rowv(w["r_k"])]
    ahead = lambda i: jnp.minimum(i, nt - 1)
    behind = lambda i: jnp.maximum(i - 1, 0)
    row_a = lambda w_: pl.BlockSpec((tm, w_), lambda i: (ahead(i), 0))
    seq_b = lambda w_: pl.BlockSpec((1, tm, w_), lambda i: (behind(i) // tps, behind(i) % tps, 0))
    one_b = pl.BlockSpec((1, 1, RWKV_PROJ), lambda i: (behind(i) // tps, 0, 0))
    per_tile = tm // chunk
    gend_spec = pl.BlockSpec((1, per_tile, 8, RWKV_WIDTH), lambda i: (behind(i) // tps, behind(i) % tps, 0, 0))
    sds = jax.ShapeDtypeStruct
    outs = pl.pallas_call(
        functools.partial(_proj_prep_kernel, tm=tm, tiles_per_seq=tps, chunk=chunk),
        grid=(nt + 1,),
        in_specs=[row_a(D_MODEL), one_b] + [_const_spec(t.shape) for t in consts],
        out_specs=[row_a(SSD_WIDTH), row_a(SSD_CONV_DIM), row_a(SSD_HEADS),
                   pl.BlockSpec((SSD_HEADS, tm), lambda i: (0, ahead(i)))]
                  + [seq_b(RWKV_WIDTH)] * 7 + [gend_spec] + [seq_b(RWKV_WIDTH)] * 2 + [one_b],
        out_shape=[sds((n, SSD_WIDTH), F32), sds((n, SSD_CONV_DIM), F32), sds((n, SSD_HEADS), F32),
                   sds((SSD_HEADS, n), F32)]
                  + [sds((b, l, RWKV_WIDTH), BF16)] * 7 + [sds((b, l // chunk, 8, RWKV_WIDTH), F32)]
                  + [sds((b, l, RWKV_WIDTH), F32)] * 2 + [sds((b, 1, RWKV_PROJ), F32)],
        scratch_shapes=[pltpu.VMEM((tm, RWKV_PROJ), F32), pltpu.VMEM((CONV_PAD + tm, RWKV_PROJ), F32)],
        compiler_params=pltpu.CompilerParams(
            dimension_semantics=("arbitrary",), vmem_limit_bytes=V7X_VMEM_LIMIT),
        name="proj_prep",
    )(x2, shift0, *consts)
    return outs


def _wkv_kernel(r_ref, w_ref, k_ref, v_ref, kk_ref, kka_ref, s0_ref,
                o_ref, sfin_ref, s_scr, vt_scr, ot_scr, *, steps):
    c = pl.program_id(1)
    n = RWKV_HEAD_DIM
    lanes = WKV_BATCH_BLOCK * RWKV_HEADS

    @pl.when(c == 0)
    def _():
        s_scr[...] = s0_ref[...].reshape(lanes, n * n).T.reshape(n, n, lanes)

    def to_pairs(ref, t):
        return ref[:, t].reshape(lanes, n).T

    def step(t, carry):
        r_t = to_pairs(r_ref, t)
        w_t = to_pairs(w_ref, t)
        k_t = to_pairs(k_ref, t)
        kk_t = to_pairs(kk_ref, t)
        kka_t = to_pairs(kka_ref, t)
        vt_scr[...] = to_pairs(v_ref, t)

        def per_value(vi, carry2):
            s_v = s_scr[vi]
            skk = jnp.sum(s_v * kk_t, axis=0, keepdims=True)
            v_row = vt_scr[pl.ds(vi, 1), :]
            s_new = s_v * w_t - skk * kka_t + v_row * k_t
            s_scr[vi] = s_new
            ot_scr[pl.ds(vi, 1), :] = jnp.sum(s_new * r_t, axis=0, keepdims=True)
            return carry2

        lax.fori_loop(0, n, per_value, 0, unroll=4)
        o_ref[:, t] = ot_scr[...].T.reshape(WKV_BATCH_BLOCK, RWKV_HEADS, n)
        return carry

    lax.fori_loop(0, steps, step, 0)

    @pl.when(c == pl.num_programs(1) - 1)
    def _():
        sfin_ref[...] = s_scr[...].reshape(n * n, lanes).T.reshape(
            WKV_BATCH_BLOCK, RWKV_HEADS, n, n)


def wkv_scan(r, w, k, v, kk, kka, s0, *, steps):
    b, l, _ = r.shape
    h, n = RWKV_HEADS, RWKV_HEAD_DIM
    assert b % WKV_BATCH_BLOCK == 0 and l % steps == 0
    ops = [t.reshape(b, l, h, n) for t in (r, w, k, v, kk, kka)]
    seq_spec = pl.BlockSpec((WKV_BATCH_BLOCK, steps, h, n), lambda g, c: (g, c, 0, 0))
    st_spec = pl.BlockSpec((WKV_BATCH_BLOCK, h, n, n), lambda g, c: (g, 0, 0, 0))
    o, s_fin = pl.pallas_call(
        functools.partial(_wkv_kernel, steps=steps),
        grid=(b // WKV_BATCH_BLOCK, l // steps),
        in_specs=[seq_spec] * 6 + [st_spec],
        out_specs=[seq_spec, st_spec],
        out_shape=[jax.ShapeDtypeStruct((b, l, h, n), F32),
                   jax.ShapeDtypeStruct((b, h, n, n), F32)],
        scratch_shapes=[pltpu.VMEM((n, n, WKV_BATCH_BLOCK * h), F32),
                        pltpu.VMEM((n, WKV_BATCH_BLOCK * h), F32),
                        pltpu.VMEM((n, WKV_BATCH_BLOCK * h), F32)],
        compiler_params=pltpu.CompilerParams(
            dimension_semantics=("parallel", "arbitrary"), vmem_limit_bytes=V7X_VMEM_LIMIT),
        name="wkv_scan",
    )(*ops, s0)
    return o.reshape(b, l, h * n), s_fin


WKV_LANE_BATCH = 128


def _wkv_batch_lanes_kernel(r_ref, w_ref, k_ref, v_ref, kk_ref, kka_ref, s0_ref,
                            o_ref, sfin_ref, op_scr, ot_scr, *, steps):
    n = RWKV_HEAD_DIM
    nb = WKV_LANE_BATCH
    sfin_ref[...] = s0_ref[...]

    def step(t, carry):
        rows = pl.ds(t, nb, stride=steps)
        for i, ref in enumerate((r_ref, w_ref, k_ref, kk_ref, kka_ref, v_ref)):
            op_scr[i] = ref[rows, :].T
        for h2 in range(2):
            ch = slice(h2 * n, (h2 + 1) * n)
            r_t = op_scr[0, ch, :]
            kka_r = jnp.sum(op_scr[4, ch, :] * r_t, axis=0, keepdims=True)
            k_r = jnp.sum(op_scr[2, ch, :] * r_t, axis=0, keepdims=True)
            op_scr[0, ch, :] = op_scr[1, ch, :] * r_t

            def per_value(vi, carry2, ch=ch, h2=h2, kka_r=kka_r, k_r=k_r):
                s_v = sfin_ref[h2, vi]
                skk = jnp.sum(s_v * op_scr[3, ch, :], axis=0, keepdims=True)
                out = jnp.sum(s_v * op_scr[0, ch, :], axis=0, keepdims=True)
                v_row = op_scr[5, pl.ds(h2 * n + vi, 1), :]
                sfin_ref[h2, vi] = s_v * op_scr[1, ch, :] - skk * op_scr[4, ch, :] + v_row * op_scr[2, ch, :]
                ot_scr[pl.ds(h2 * n + vi, 1), :] = out - skk * kka_r + v_row * k_r
                return carry2

            lax.fori_loop(0, n, per_value, 0, unroll=4)
        o_ref[rows, :] = ot_scr[...].T
        return carry

    lax.fori_loop(0, steps, step, 0)


def wkv_scan_batch_lanes(r, w, k, v, kk, kka, s0):
    b, l, wd = r.shape
    h, n, nb = RWKV_HEADS, RWKV_HEAD_DIM, WKV_LANE_BATCH
    assert b % nb == 0
    ops = [t.reshape(b * l, wd) for t in (r, w, k, v, kk, kka)]
    s0t = jnp.transpose(s0, (1, 2, 3, 0))
    seq_spec = pl.BlockSpec((nb * l, 2 * n), lambda g, j: (g, j))
    st_spec = pl.BlockSpec((2, n, n, nb), lambda g, j: (j, 0, 0, g))
    o, s_fin = pl.pallas_call(
        functools.partial(_wkv_batch_lanes_kernel, steps=l),
        grid=(b // nb, h // 2),
        in_specs=[seq_spec] * 6 + [st_spec],
        out_specs=[seq_spec, st_spec],
        out_shape=[jax.ShapeDtypeStruct((b * l, wd), F32), jax.ShapeDtypeStruct((h, n, n, b), F32)],
        scratch_shapes=[pltpu.VMEM((6, 2 * n, nb), F32), pltpu.VMEM((2 * n, nb), F32)],
        compiler_params=pltpu.CompilerParams(
            dimension_semantics=("parallel", "parallel"), vmem_limit_bytes=V7X_VMEM_LIMIT),
        name="wkv_scan_batch_lanes",
    )(*ops, s0t)
    return o.reshape(b, l, wd), jnp.transpose(s_fin, (3, 0, 1, 2))


WKV_CHUNK = 64
WKV_PAIRS = RWKV_HEADS // 2
WKV_ROW_STRIDE = WKV_CHUNK + 8
WKV_SOLVE_ROWS = 8
WKV_SOLVE_COLS = 16


def _pair_masks():
    c = WKV_CHUNK
    row = lax.broadcasted_iota(jnp.int32, (2 * c, 2 * c), 0)
    col = lax.broadcasted_iota(jnp.int32, (2 * c, 2 * c), 1)
    t, i = row % c, col % c
    keep = i <= t - jnp.where(row < c, 1, 0)
    sign = jnp.where(row >= c, jnp.where(col < c, -1.0, 1.0), 1.0)
    block_diag = row // c == col // c
    return keep, sign, block_diag


def _wkv_prepare_kernel(kkt_ref, rt_ref, kbg_ref, kfg_ref, vb_ref,
                        lo_ref, rhs0_ref, tp_ref, abs_scr, top_scr, abt_scr, tt_scr):
    c = WKV_CHUNK
    keep, sign, _ = _pair_masks()
    lane = lax.broadcasted_iota(jnp.int32, (1, 2 * c), 1)
    head0 = lane < RWKV_HEAD_DIM
    row_head0 = lax.broadcasted_iota(jnp.int32, (2 * RWKV_HEAD_DIM, 1), 0) < RWKV_HEAD_DIM
    zeros = jnp.zeros((c, 2 * c), BF16)

    def per_batch(b, carry):
        for j in range(WKV_PAIRS):
            sl = slice(j * 2 * RWKV_HEAD_DIM, (j + 1) * 2 * RWKV_HEAD_DIM)
            lhs = jnp.concatenate([kkt_ref[b, :, sl], rt_ref[b, :, sl]], axis=0)
            rhs = jnp.concatenate([kbg_ref[b, :, sl], kfg_ref[b, :, sl]], axis=0)
            rhs_t = rhs.astype(F32).T
            top = jnp.where(row_head0, rhs_t, 0.0)
            w_a = jnp.concatenate([top, rhs_t - top], axis=1).astype(BF16)
            a_both = jnp.dot(lhs, w_a, preferred_element_type=F32)
            for h2 in range(2):
                a = a_both[:, h2 * 2 * c:(h2 + 1) * 2 * c]
                a = jnp.where(keep, a, 0.0) * sign
                inst = h2 * (WKV_BATCH_BLOCK * WKV_PAIRS) + b * WKV_PAIRS + j
                abs_scr[pl.ds(inst * WKV_ROW_STRIDE, c), :c] = a[:c, :c]
                top_scr[j, :, h2 * 2 * c:(h2 + 1) * 2 * c] = a[:c].astype(BF16)
                lo_ref[b, :, (2 * j + h2) * 2 * c:(2 * j + h2 + 1) * 2 * c] = a[c:].astype(BF16)

        for j in range(WKV_PAIRS):
            sl = slice(j * 2 * RWKV_HEAD_DIM, (j + 1) * 2 * RWKV_HEAD_DIM)
            v = vb_ref[b, :, sl]
            v0 = jnp.where(head0, v, jnp.zeros_like(v))
            w_akf = jnp.concatenate([zeros, v0, zeros, v - v0], axis=0)
            rhs0_ref[b, :, sl] = jnp.dot(top_scr[j], w_akf, preferred_element_type=F32)
        return carry

    lax.fori_loop(0, WKV_BATCH_BLOCK, per_batch, 0)

    n_inst = 2 * WKV_BATCH_BLOCK * WKV_PAIRS
    n_pair_rows = WKV_BATCH_BLOCK * WKV_PAIRS

    def to_lanes(t, carry):
        abt_scr[t] = abs_scr[pl.ds(t, n_inst, stride=WKV_ROW_STRIDE), :][:, :c].T
        return carry

    lax.fori_loop(0, c, to_lanes, 0, unroll=8)

    tt_scr[...] = jnp.zeros(tt_scr.shape, F32)
    sub_iota = lax.broadcasted_iota(jnp.int32, (WKV_SOLVE_COLS, n_inst), 0)
    rows = range(WKV_SOLVE_ROWS)
    for cb in range(c // WKV_SOLVE_COLS):
        col0 = WKV_SOLVE_COLS * cb
        cols = slice(col0, col0 + WKV_SOLVE_COLS)
        first_block = col0 // WKV_SOLVE_ROWS

        def solve_rows(tb, carry, col0=col0, cols=cols, first_block=first_block):
            t0 = tb * WKV_SOLVE_ROWS

            def sub(ib, accs):
                ps = [tt_scr[ib * WKV_SOLVE_ROWS + di, cols, :] for di in rows]
                out = []
                for r in rows:
                    terms = [abt_scr[t0 + r, pl.ds(ib * WKV_SOLVE_ROWS + di, 1), :] * ps[di] for di in rows]
                    while len(terms) > 1:
                        terms = [a + b for a, b in zip(terms[::2], terms[1::2])]
                    out.append(accs[r] - terms[0])
                return tuple(out)

            unit = tuple(jnp.where(sub_iota + col0 == t0 + r, 1.0, 0.0) for r in rows)
            accs = list(lax.fori_loop(first_block, tb, sub, unit))
            for r in rows:
                for r2 in range(r):
                    accs[r] = accs[r] - abt_scr[t0 + r, pl.ds(t0 + r2, 1), :] * accs[r2]
                tt_scr[t0 + r, cols, :] = accs[r]
            return carry

        lax.fori_loop(first_block, c // WKV_SOLVE_ROWS, solve_rows, 0)

    def from_lanes(t, carry):
        m = tt_scr[t].T
        abs_scr[pl.ds(t, n_pair_rows, stride=WKV_ROW_STRIDE), :] = jnp.concatenate(
            [m[:n_pair_rows], m[n_pair_rows:]], axis=1)
        return carry

    lax.fori_loop(0, c, from_lanes, 0, unroll=8)

    def emit(b, carry):
        for j in range(WKV_PAIRS):
            row0 = (b * WKV_PAIRS + j) * WKV_ROW_STRIDE
            tp_ref[b, :, j * 2 * c:(j + 1) * 2 * c] = abs_scr[pl.ds(row0, c), :].astype(BF16)
        return carry

    lax.fori_loop(0, WKV_BATCH_BLOCK, emit, 0)


def _wkv_apply_kernel(kkt_ref, rt_ref, kfh_ref, nbh_ref, vb_ref, tp_ref, lo_ref, rhs0_ref, gend_ref, s0_ref,
                      o_ref, sfin_ref, x_scr, wp_scr, p_scr):
    c = WKV_CHUNK
    n = RWKV_HEAD_DIM
    ch = pl.program_id(1)
    _, _, block_diag = _pair_masks()
    lane = lax.broadcasted_iota(jnp.int32, (1, 2 * c), 1)
    head0 = lane < n
    eye2 = (lax.broadcasted_iota(jnp.int32, (n, 2 * n), 0)
            == lax.broadcasted_iota(jnp.int32, (n, 2 * n), 1) % n).astype(F32)

    @pl.when(ch == 0)
    def _():
        def init(b, carry):
            for j in range(WKV_PAIRS):
                sp = s0_ref[b, 2 * j:2 * j + 2].reshape(2 * n, n)
                dup = jnp.dot(sp, eye2, precision=HIGHEST, preferred_element_type=F32)
                x_scr[b, j] = jnp.where(block_diag, dup, 0.0)
            return carry
        lax.fori_loop(0, WKV_BATCH_BLOCK, init, 0)

    def per_batch(b, carry):
        for j in range(WKV_PAIRS):
            sl = slice(j * 2 * n, (j + 1) * 2 * n)
            lhs = jnp.concatenate([kkt_ref[b, :, sl], rt_ref[b, :, sl]], axis=0)
            kx = lax.dot_general(lhs, x_scr[b, j].astype(BF16), (((1,), (1,)), ((), ())),
                                 preferred_element_type=F32)
            rhs = kx[:c] + rhs0_ref[b, :, sl]
            r0 = jnp.where(head0, rhs, 0.0)
            wp_scr[j] = jnp.concatenate([r0, rhs - r0], axis=0).astype(BF16)
            o_ref[b, :, sl] = kx[c:]
        for j in range(WKV_PAIRS):
            sl = slice(j * 2 * n, (j + 1) * 2 * n)
            p = jnp.dot(tp_ref[b, :, sl], wp_scr[j], preferred_element_type=F32)
            p_scr[j] = p.astype(BF16)
        for j in range(WKV_PAIRS):
            sl = slice(j * 2 * n, (j + 1) * 2 * n)
            v = vb_ref[b, :, sl]
            pb = p_scr[j]
            zero = jnp.zeros_like(pb)
            p0, v0 = jnp.where(head0, pb, zero), jnp.where(head0, v, zero)
            w_o = jnp.concatenate([p0, v0, pb - p0, v - v0], axis=0)
            o_ref[b, :, sl] = o_ref[b, :, sl] + jnp.dot(
                lo_ref[b, :, j * 4 * c:(j + 1) * 4 * c], w_o, preferred_element_type=F32)
            vp = jnp.concatenate([v, pb], axis=0)
            kb = jnp.concatenate([kfh_ref[b, :, sl], nbh_ref[b, :, sl]], axis=0)
            upd = lax.dot_general(vp, kb, (((0,), (0,)), ((), ())), preferred_element_type=F32)
            x_scr[b, j] = jnp.where(block_diag, x_scr[b, j] * gend_ref[b, 0, 0:1, sl] + upd, 0.0)
        return carry

    lax.fori_loop(0, WKV_BATCH_BLOCK, per_batch, 0)

    @pl.when(ch == pl.num_programs(1) - 1)
    def _():
        def fin(b, carry):
            for j in range(WKV_PAIRS):
                sp = lax.dot_general(x_scr[b, j], eye2, (((1,), (1,)), ((), ())),
                                     precision=HIGHEST, preferred_element_type=F32)
                sfin_ref[b, 2 * j:2 * j + 2] = sp.reshape(2, n, n)
            return carry
        lax.fori_loop(0, WKV_BATCH_BLOCK, fin, 0)


def wkv_chunked(kkt, rt, kfh, nbh, vb, kbg, kfg, gend, s0):
    b, l, wd = kkt.shape
    c = WKV_CHUNK
    assert b % WKV_BATCH_BLOCK == 0 and l % c == 0
    gb, nc = b // WKV_BATCH_BLOCK, l // c
    lanes = 2 * WKV_BATCH_BLOCK * WKV_PAIRS
    seq = lambda w_: pl.BlockSpec((WKV_BATCH_BLOCK, c, w_), lambda g, i: (g, i, 0))
    gend_spec = pl.BlockSpec((WKV_BATCH_BLOCK, 1, 8, wd), lambda g, i: (g, i, 0, 0))
    sds = jax.ShapeDtypeStruct
    st_spec = pl.BlockSpec((WKV_BATCH_BLOCK, RWKV_HEADS, RWKV_HEAD_DIM, RWKV_HEAD_DIM),
                           lambda g, i: (g, 0, 0, 0))
    blk = (WKV_BATCH_BLOCK, c, wd)
    o, s_fin = pl.pallas_call(
        _wkv_chunk_kernel,
        grid=(gb, nc),
        in_specs=[seq(wd)] * 7 + [gend_spec, st_spec],
        out_specs=[seq(wd), st_spec],
        out_shape=[sds((b, l, wd), F32), sds(s0.shape, F32)],
        scratch_shapes=[
            pltpu.VMEM((lanes * WKV_ROW_STRIDE, 2 * c), F32),
            pltpu.VMEM((WKV_PAIRS, c, 4 * c), BF16),
            pltpu.VMEM((c, c, lanes), F32),
            pltpu.VMEM((c, c, lanes), F32),
            pltpu.VMEM((WKV_BATCH_BLOCK, c, 2 * wd), BF16),
            pltpu.VMEM(blk, F32),
            pltpu.VMEM(blk, BF16),
            pltpu.VMEM((WKV_BATCH_BLOCK, WKV_PAIRS, 2 * RWKV_HEAD_DIM, 2 * RWKV_HEAD_DIM), F32),
            pltpu.VMEM((WKV_PAIRS, 2 * c, 2 * RWKV_HEAD_DIM), BF16),
            pltpu.VMEM((WKV_PAIRS, c, 2 * RWKV_HEAD_DIM), BF16)],
        compiler_params=pltpu.CompilerParams(
            dimension_semantics=("parallel", "arbitrary"), vmem_limit_bytes=V7X_VMEM_LIMIT),
        name="wkv_chunked",
    )(kkt, rt, kfh, nbh, vb, kbg, kfg, gend, s0)
    return o, s_fin


def _wkv_chunk_kernel(kkt_ref, rt_ref, kfh_ref, nbh_ref, vb_ref, kbg_ref, kfg_ref, gend_ref, s0_ref,
                      o_ref, sfin_ref, abs_scr, top_scr, abt_scr, tt_scr, lo_scr, rhs0_scr, tp_scr,
                      x_scr, wp_scr, p_scr):
    _wkv_prepare_kernel(kkt_ref, rt_ref, kbg_ref, kfg_ref, vb_ref, lo_scr, rhs0_scr, tp_scr,
                        abs_scr, top_scr, abt_scr, tt_scr)
    _wkv_apply_kernel(kkt_ref, rt_ref, kfh_ref, nbh_ref, vb_ref, tp_scr, lo_scr, rhs0_scr, gend_ref, s0_ref,
                      o_ref, sfin_ref, x_scr, wp_scr, p_scr)


def _tail_kernel(o_ref, gate_ref, bonus_ref, x_ref, ys_ref, p_ref, lnw_ref, lnb_ref, woa_ref, wob_ref,
                 nf_ref, wg_ref, wu_ref, wd_ref, np_ref, wpg_ref, wpp_ref, nl_ref, y_ref, new_scr, cur_scr):
    i = pl.program_id(0)

    @pl.when(i == 0)
    def _():
        new_scr[...] = jnp.zeros(new_scr.shape, BF16)

    cur_scr[...] = new_scr[...]

    parts = 4
    rows = o_ref.shape[0] // parts
    inv_n = 1.0 / RWKV_HEAD_DIM

    def vector_half(part):
        rs = slice(part * rows, (part + 1) * rows)
        o = o_ref[rs, :]
        mu = _head_sums(o) * inv_n
        d = o - mu
        var = _head_sums(d * d) * inv_n
        on = d * lax.rsqrt(var + GN_EPS) * lnw_ref[...] + lnb_ref[...]
        new_scr[rs, :] = ((on + bonus_ref[rs, :]) * gate_ref[rs, :]).astype(BF16)

    vector_half(0)
    y_rwkv = cur_scr[...]
    h = x_ref[...] + _bdot(ys_ref[...], woa_ref[...]) + jnp.dot(y_rwkv, wob_ref[...],
                                                                preferred_element_type=F32)
    hf = _rms(h, nf_ref[...]).astype(BF16)
    vector_half(1)
    gate = jnp.dot(hf, wg_ref[...], preferred_element_type=F32)
    up = jnp.dot(hf, wu_ref[...], preferred_element_type=F32)
    vector_half(2)
    h = h + _bdot(_silu(gate) * up, wd_ref[...])
    vector_half(3)
    pg = _sigmoid(_bdot(_rms(h, np_ref[...]), wpg_ref[...]))
    h = h + pg * _bdot(p_ref[...], wpp_ref[...])
    y_ref[...] = _rms(h, nl_ref[...])


def layer_tail(o, gate, bonus, x, y_ssd, p, consts, *, tm):
    n = x.shape[0]
    nt = n // tm
    ahead = lambda w_: pl.BlockSpec((tm, w_), lambda i: (jnp.minimum(i, nt - 1), 0))
    behind = lambda w_: pl.BlockSpec((tm, w_), lambda i: (jnp.maximum(i - 1, 0), 0))
    return pl.pallas_call(
        _tail_kernel,
        grid=(nt + 1,),
        in_specs=[ahead(D_MODEL)] * 3 + [behind(D_MODEL)] * 2 + [behind(PLE_DIM)]
                 + [_const_spec(t.shape) for t in consts],
        out_specs=behind(D_MODEL),
        out_shape=jax.ShapeDtypeStruct((n, D_MODEL), F32),
        scratch_shapes=[pltpu.VMEM((tm, D_MODEL), BF16), pltpu.VMEM((tm, D_MODEL), BF16)],
        compiler_params=pltpu.CompilerParams(
            dimension_semantics=("arbitrary",), vmem_limit_bytes=V7X_VMEM_LIMIT),
        name="layer_tail",
    )(o, gate, bonus, x, y_ssd, p, *consts)


def _prepare_weights(w):
    c0, c1, c2 = SSD_WIDTH, SSD_WIDTH + SSD_CONV_DIM, SSD_WIDTH + SSD_CONV_DIM + SSD_HEADS
    w_in = w["w_in"]
    rowv = lambda t: t.reshape(1, -1)
    return dict(
        w,
        wz=w_in[:, :c0].astype(BF16), wx=w_in[:, c0:c1].astype(BF16),
        wdt=w_in[:, c1:c2].astype(BF16), wr=w_in[:, c2:].astype(BF16),
        woa=w["w_out"][:SSD_WIDTH].astype(BF16), wob=w["w_out"][SSD_WIDTH:].astype(BF16),
        wg=w["w_gate"].astype(BF16), wu=w["w_up"].astype(BF16), wd=w["w_down"].astype(BF16),
        wpg=w["w_ple_gate"].astype(BF16), wpp=w["w_ple_proj"].astype(BF16),
        norm_mix_r=rowv(w["norm_mix"]), norm_ffn_r=rowv(w["norm_ffn"]),
        norm_ple_r=rowv(w["norm_ple"]), norm_final_r=rowv(w["norm_final"]),
        ln_x_w_r=rowv(w["ln_x_w"]), ln_x_b_r=rowv(w["ln_x_b"]),
    )


def layer_forward(x, p, conv0, shift0, ssm0, wkv0, w, *, tm, ssd_q, prep_tt, wkv_steps):
    b, l, _ = x.shape
    n = b * l
    x2 = x.reshape(n, D_MODEL)
    chunked = l % WKV_CHUNK == 0 and l % tm == 0
    if chunked:
        z, xbc, dt, dtt, *ops, gate, bonus, shift_new = proj_prep(
            x, shift0, w["norm_mix_r"], w["wz"], w["wx"], w["wr"], w["wdt"], w, tm=tm, chunk=WKV_CHUNK)
    else:
        z, xbc, rw, dt, dtt = in_projection(x2, w["norm_mix_r"], w["wz"], w["wx"], w["wr"], w["wdt"], tm=tm)
        *ops, gate, bonus, shift_new = rwkv_prep(rw.reshape(b, l, -1), shift0, w, tt=prep_tt, chunk=0)
    y_ssd, ssm_new, conv_new = ssd_mixer(
        z.reshape(b, l, -1), xbc.reshape(b, l, -1), dt.reshape(b, l, -1), dtt, conv0, ssm0, w, q=ssd_q)
    if chunked:
        o, wkv_new = wkv_chunked(*ops, wkv0)
    elif b % WKV_LANE_BATCH == 0:
        o, wkv_new = wkv_scan_batch_lanes(*ops, wkv0)
    else:
        o, wkv_new = wkv_scan(*ops, wkv0, steps=wkv_steps)
    flat = lambda t: t.reshape(n, -1)
    tail_consts = [w["ln_x_w_r"], w["ln_x_b_r"], w["woa"], w["wob"], w["norm_ffn_r"], w["wg"], w["wu"],
                   w["wd"], w["norm_ple_r"], w["wpg"], w["wpp"], w["norm_final_r"]]
    y = layer_tail(flat(o), flat(gate), flat(bonus), x2, flat(y_ssd), p.reshape(n, PLE_DIM),
                   tail_consts, tm=tm)
    return y.reshape(b, l, D_MODEL), ssm_new, conv_new, wkv_new, shift_new


def kernel(x_prompt, x_sample, state_ssm, state_conv, state_wkv, state_shift, p_prompt, p_sample, norm_mix, w_in, conv_w, conv_b, dt_bias, a_log, d_skip, ssd_norm, shift_mu, w0, w2, a0, a2, g2, k_k, k_a, r_k, ln_x_w, ln_x_b, w_out, norm_ffn, w_gate, w_up, w_down, norm_ple, w_ple_gate, w_ple_proj, norm_final):
    w = _prepare_weights(dict(
        norm_mix=norm_mix[0], w_in=w_in[0], conv_w=conv_w[0], conv_b=conv_b[0], dt_bias=dt_bias[0],
        a_log=a_log[0], d_skip=d_skip[0], ssd_norm=ssd_norm[0], shift_mu=shift_mu[0], w0=w0[0],
        w2=w2[0], a0=a0[0], a2=a2[0], g2=g2[0], k_k=k_k[0], k_a=k_a[0], r_k=r_k[0],
        ln_x_w=ln_x_w[0], ln_x_b=ln_x_b[0], w_out=w_out[0], norm_ffn=norm_ffn[0],
        w_gate=w_gate[0], w_up=w_up[0], w_down=w_down[0], norm_ple=norm_ple[0],
        w_ple_gate=w_ple_gate[0], w_ple_proj=w_ple_proj[0], norm_final=norm_final))
    bp = x_prompt.shape[0]
    zeros = lambda *s: jnp.zeros(s, F32)
    yp, s1, c1, k1, t1 = layer_forward(
        x_prompt, p_prompt[0], zeros(bp, SSD_CONV - 1, SSD_CONV_DIM), zeros(bp, 1, RWKV_PROJ),
        zeros(bp, SSD_HEADS, SSD_HEAD_DIM, SSD_STATE),
        zeros(bp, RWKV_HEADS, RWKV_HEAD_DIM, RWKV_HEAD_DIM), w,
        tm=256, ssd_q=min(SSD_CHUNK, x_prompt.shape[1]), prep_tt=min(128, x_prompt.shape[1]),
        wkv_steps=min(16, x_prompt.shape[1]))
    ys, s2, c2, k2, t2 = layer_forward(
        x_sample, p_sample[0], state_conv[0], state_shift[0], state_ssm[0], state_wkv[0], w,
        tm=256, ssd_q=x_sample.shape[1], prep_tt=128, wkv_steps=x_sample.shape[1])
    return (yp, ys, s1[None], c1[None], k1[None], t1[None], s2[None], c2[None], k2[None], t2[None])
```

```python
import functools

import jax
import jax.numpy as jnp
from jax import lax
from jax.experimental import pallas as pl
from jax.experimental.pallas import tpu as pltpu

F32 = jnp.float32
BF16 = jnp.bfloat16
HIGHEST = lax.Precision.HIGHEST

D_MODEL = 1024
SSD_WIDTH = 1024
SSD_HEADS = 16
SSD_HEAD_DIM = 64
SSD_GROUPS = 2
SSD_GROUP_WIDTH = SSD_WIDTH // SSD_GROUPS
SSD_STATE = 128
SSD_CONV = 4
SSD_CHUNK = 128
SSD_BC = SSD_GROUPS * SSD_STATE
SSD_CONV_DIM = SSD_WIDTH + 2 * SSD_BC
RWKV_WIDTH = 1024
RWKV_HEADS = 16
RWKV_HEAD_DIM = 64
DECAY_LORA = 64
AAA_LORA = 64
GATE_LORA = 128
RWKV_PROJ = 3 * RWKV_WIDTH + DECAY_LORA + AAA_LORA + GATE_LORA
D_FF = 2816
PLE_DIM = 256
NORM_EPS = 1e-6
GN_EPS = 64e-5

WKV_BATCH_BLOCK = 8
V7X_VMEM_LIMIT = 56 * 1024 * 1024
CONV_PAD = 8
SSD_SEQS_PER_STEP = 8
SSD_LONG_SEQS_PER_STEP = 1


def _rms(x, g):
    return x * lax.rsqrt(jnp.mean(x * x, axis=-1, keepdims=True) + NORM_EPS) * g


def _sigmoid(x):
    return 1.0 / (1.0 + jnp.exp(-x))


def _silu(x):
    return x * _sigmoid(x)


def _softplus(x):
    return jnp.maximum(x, 0.0) + jnp.log(1.0 + jnp.exp(-jnp.abs(x)))


def _bdot(a, b):
    return jnp.dot(a.astype(BF16), b.astype(BF16), preferred_element_type=F32)


def _split3(t):
    hi = t.astype(BF16)
    r1 = t - hi.astype(F32)
    mid = r1.astype(BF16)
    lo = (r1 - mid.astype(F32)).astype(BF16)
    return hi, mid, lo


def _dot01(a, b, *, exact_side):
    if exact_side == "lhs":
        m = b.astype(BF16)
        return sum(jnp.dot(p, m, preferred_element_type=F32) for p in _split3(a))
    m = a.astype(BF16)
    return sum(jnp.dot(m, p, preferred_element_type=F32) for p in _split3(b))


def _const_spec(shape):
    return pl.BlockSpec(shape, lambda *_: (0,) * len(shape), pipeline_mode=pl.Buffered(1))


def _head_expand(rows):
    h = lax.broadcasted_iota(jnp.int32, (rows, SSD_WIDTH), 0)
    c = lax.broadcasted_iota(jnp.int32, (rows, SSD_WIDTH), 1)
    return (c // SSD_HEAD_DIM == h).astype(F32)


def _proj_kernel(x_ref, g_ref, wz_ref, wx_ref, wr_ref, wdt_ref, wdtt_ref,
                 z_ref, xbc_ref, rw_ref, dt_ref, dtt_ref):
    u = _rms(x_ref[...], g_ref[...]).astype(BF16)
    z_ref[...] = jnp.dot(u, wz_ref[...], preferred_element_type=F32)
    xbc_ref[...] = jnp.dot(u, wx_ref[...], preferred_element_type=F32)
    rw_ref[...] = jnp.dot(u, wr_ref[...], preferred_element_type=F32)
    dt_ref[...] = jnp.dot(u, wdt_ref[...], preferred_element_type=F32)
    dtt_ref[...] = lax.dot_general(wdtt_ref[...], u, (((1,), (1,)), ((), ())), preferred_element_type=F32)


def in_projection(x, g, wz, wx, wr, wdt, *, tm):
    n = x.shape[0]
    row = lambda w: pl.BlockSpec((tm, w), lambda i: (i, 0))
    wdtt = wdt.T
    return pl.pallas_call(
        _proj_kernel,
        grid=(n // tm,),
        in_specs=[row(D_MODEL), _const_spec((1, D_MODEL)), _const_spec(wz.shape),
                  _const_spec(wx.shape), _const_spec(wr.shape), _const_spec(wdt.shape),
                  _const_spec(wdtt.shape)],
        out_specs=[row(SSD_WIDTH), row(SSD_CONV_DIM), row(RWKV_PROJ), row(SSD_HEADS),
                   pl.BlockSpec((SSD_HEADS, tm), lambda i: (0, i))],
        out_shape=[jax.ShapeDtypeStruct((n, SSD_WIDTH), F32),
                   jax.ShapeDtypeStruct((n, SSD_CONV_DIM), F32),
                   jax.ShapeDtypeStruct((n, RWKV_PROJ), F32),
                   jax.ShapeDtypeStruct((n, SSD_HEADS), F32),
                   jax.ShapeDtypeStruct((SSD_HEADS, n), F32)],
        compiler_params=pltpu.CompilerParams(
            dimension_semantics=("parallel",), vmem_limit_bytes=V7X_VMEM_LIMIT),
        name="in_projection",
    )(x, g, wz, wx, wr, wdt, wdtt)


def _ssd_kernel(z_ref, xbc_ref, dt_ref, dtt_ref, hist_ref, h0_ref, cw_ref, cb_ref,
                dtb_ref, dtbt_ref, alog_ref, alogt_ref, dsk_ref, nrm_ref,
                y_ref, hfin_ref, cnew_ref, xfull_scr, h_scr, *, q, nseq, single_chunk):
    refs = (z_ref, xbc_ref, dt_ref, dtt_ref, hist_ref, h0_ref, cw_ref, cb_ref, dtb_ref, dtbt_ref, alog_ref,
            alogt_ref, dsk_ref, nrm_ref, y_ref, hfin_ref, cnew_ref, xfull_scr, h_scr)
    stages = [_ssd_sequence(s, *refs, q=q, single_chunk=single_chunk) for s in range(nseq)]
    for _ in zip(*stages):
        pass


def _ssd_sequence(s, z_ref, xbc_ref, dt_ref, dtt_ref, hist_ref, h0_ref, cw_ref, cb_ref,
                  dtb_ref, dtbt_ref, alog_ref, alogt_ref, dsk_ref, nrm_ref,
                  y_ref, hfin_ref, cnew_ref, xfull_scr, h_scr, *, q, single_chunk):
    c = pl.program_id(1)
    last = pl.num_programs(1) - 1
    gw = SSD_GROUP_WIDTH

    @pl.when(c == 0)
    def _():
        xfull_scr[s,CONV_PAD - 3:CONV_PAD, :] = hist_ref[s]
        if not single_chunk:
            for g in range(SSD_GROUPS):
                h_scr[s * SSD_GROUPS + g] = h0_ref[s,g * 8:(g + 1) * 8].reshape(gw, SSD_STATE).T

    @pl.when(c > 0)
    def _():
        xfull_scr[s,CONV_PAD - 3:CONV_PAD, :] = xfull_scr[s,CONV_PAD + q - 3:CONV_PAD + q, :]

    xfull_scr[s,CONV_PAD:CONV_PAD + q, :] = xbc_ref[s]

    conv = cb_ref[...]
    for j in range(SSD_CONV):
        lo = CONV_PAD - 3 + j
        conv = conv + xfull_scr[s,lo:lo + q, :] * cw_ref[j:j + 1, :]
    act = _silu(conv)
    xs = act[:, :SSD_WIDTH]
    yield

    dt = _softplus(dt_ref[s] + dtb_ref[...])
    dtt_raw = dtt_ref[...] if len(dtt_ref.shape) == 2 else dtt_ref[s]
    dtt = _softplus(dtt_raw + dtbt_ref[...])
    da = dt * -jnp.exp(alog_ref[...])
    dat = dtt * -jnp.exp(alogt_ref[...])
    row = lax.broadcasted_iota(jnp.int32, (q, q), 0)
    col = lax.broadcasted_iota(jnp.int32, (q, q), 1)
    causal = row >= col
    a_cum = _dot01(causal.astype(F32), da, exact_side="rhs")
    a_cumt = _dot01(dat, (row <= col).astype(F32), exact_side="lhs")
    yield

    expand = _head_expand(SSD_HEADS)
    a_cum_x = _dot01(a_cum, expand, exact_side="lhs")
    dt_x = _dot01(dt, expand, exact_side="lhs")
    a_end_x = a_cum_x[q - 1:q, :]
    decay_in_x = jnp.exp(a_cum_x)
    chunk_decay_x = jnp.exp(a_end_x)
    xd = xs * (jnp.exp(a_end_x - a_cum_x) * dt_x)
    yield

    ys = []
    for g in range(SSD_GROUPS):
        bm = act[:, SSD_WIDTH + g * SSD_STATE:SSD_WIDTH + (g + 1) * SSD_STATE]
        cm = act[:, SSD_WIDTH + SSD_BC + g * SSD_STATE:SSD_WIDTH + SSD_BC + (g + 1) * SSD_STATE]
        cb = lax.dot_general(cm.astype(BF16), bm.astype(BF16), (((1,), (1,)), ((), ())),
                             preferred_element_type=F32)
        yield
        y_heads = []
        for e in range(8):
            h = g * 8 + e
            seg = a_cum[:, h:h + 1] - a_cumt[h:h + 1, :]
            lmat = jnp.where(causal, jnp.exp(jnp.where(causal, seg, 0.0)), 0.0)
            w_qs = cb * lmat * dtt[h:h + 1, :]
            y_heads.append(_bdot(w_qs, xs[:, h * SSD_HEAD_DIM:(h + 1) * SSD_HEAD_DIM]))
            yield
        y_diag = jnp.concatenate(y_heads, axis=1)
        sl = slice(g * gw, (g + 1) * gw)
        if single_chunk:
            h_in = h0_ref[s,g * 8:(g + 1) * 8].reshape(gw, SSD_STATE)
            y_off = lax.dot_general(cm.astype(BF16), h_in.astype(BF16), (((1,), (1,)), ((), ())),
                                    preferred_element_type=F32)
            upd = lax.dot_general(xd[:, sl].astype(BF16), bm.astype(BF16), (((0,), (0,)), ((), ())),
                                  preferred_element_type=F32)
            head_decay = jnp.broadcast_to(jnp.exp(a_cumt[:, q - 1:q]), (SSD_HEADS, SSD_STATE))
            for e in range(8):
                h = g * 8 + e
                hfin_ref[s,h] = (h0_ref[s,h] * head_decay[h:h + 1, :]
                                  + upd[e * SSD_HEAD_DIM:(e + 1) * SSD_HEAD_DIM, :])
        else:
            h_in = h_scr[s * SSD_GROUPS + g]
            y_off = _bdot(cm, h_in)
            upd = lax.dot_general(bm.astype(BF16), xd[:, sl].astype(BF16), (((0,), (0,)), ((), ())),
                                  preferred_element_type=F32)
            h_scr[s * SSD_GROUPS + g] = h_in * chunk_decay_x[:, sl] + upd
        ys.append(y_diag + y_off * decay_in_x[:, sl])
        yield

    y = jnp.concatenate(ys, axis=1) + dsk_ref[...] * xs
    yg = y * _silu(z_ref[s])
    outs = []
    for g in range(SSD_GROUPS):
        t = yg[:, g * gw:(g + 1) * gw]
        outs.append(t * lax.rsqrt(jnp.mean(t * t, axis=-1, keepdims=True) + NORM_EPS))
    y_ref[s] = jnp.concatenate(outs, axis=1) * nrm_ref[...]

    @pl.when(c == last)
    def _():
        cnew_ref[s] = xfull_scr[s,CONV_PAD + q - 3:CONV_PAD + q, :]
        if not single_chunk:
            for g in range(SSD_GROUPS):
                hfin_ref[s,g * 8:(g + 1) * 8] = h_scr[s * SSD_GROUPS + g].T.reshape(8, SSD_HEAD_DIM, SSD_STATE)

    yield


def ssd_mixer(z, xbc, dt, dtt_flat, conv0, ssm0, w, *, q):
    b, l, _ = z.shape
    single_chunk = l == q
    want = SSD_SEQS_PER_STEP if single_chunk else SSD_LONG_SEQS_PER_STEP
    nseq = want if b % want == 0 else 1
    if q % 128 == 0 and nseq == 1:
        dtt = dtt_flat
        dtt_spec = pl.BlockSpec((SSD_HEADS, q), lambda i, c: (0, i * (l // q) + c))
    else:
        dtt = jnp.swapaxes(dt, 1, 2)
        dtt_spec = pl.BlockSpec((nseq, SSD_HEADS, q), lambda i, c: (i, 0, c))
    seq = lambda wd: pl.BlockSpec((nseq, q, wd), lambda i, c: (i, c, 0))
    per_b3 = lambda s: pl.BlockSpec((nseq,) + s, lambda i, c: (i,) + (0,) * len(s))
    col = lambda t: t.reshape(-1, 1)
    rowv = lambda t: t.reshape(1, -1)
    consts = [w["conv_w"], rowv(w["conv_b"]), rowv(w["dt_bias"]), col(w["dt_bias"]),
              rowv(w["a_log"]), col(w["a_log"]),
              rowv(jnp.repeat(w["d_skip"], SSD_HEAD_DIM)), rowv(w["ssd_norm"])]
    return pl.pallas_call(
        functools.partial(_ssd_kernel, q=q, nseq=nseq, single_chunk=single_chunk),
        grid=(b // nseq, l // q),
        in_specs=[seq(SSD_WIDTH), seq(SSD_CONV_DIM), seq(SSD_HEADS),
                  dtt_spec,
                  per_b3((SSD_CONV - 1, SSD_CONV_DIM)),
                  per_b3((SSD_HEADS, SSD_HEAD_DIM, SSD_STATE))]
                 + [_const_spec(t.shape) for t in consts],
        out_specs=[seq(SSD_WIDTH), per_b3((SSD_HEADS, SSD_HEAD_DIM, SSD_STATE)),
                   per_b3((SSD_CONV - 1, SSD_CONV_DIM))],
        out_shape=[jax.ShapeDtypeStruct((b, l, SSD_WIDTH), F32),
                   jax.ShapeDtypeStruct((b, SSD_HEADS, SSD_HEAD_DIM, SSD_STATE), F32),
                   jax.ShapeDtypeStruct((b, SSD_CONV - 1, SSD_CONV_DIM), F32)],
        scratch_shapes=[pltpu.VMEM((nseq, CONV_PAD + q, SSD_CONV_DIM), F32),
                        pltpu.VMEM((nseq * SSD_GROUPS, SSD_STATE, SSD_GROUP_WIDTH), F32)],
        compiler_params=pltpu.CompilerParams(
            dimension_semantics=("parallel", "arbitrary"), vmem_limit_bytes=V7X_VMEM_LIMIT),
        name="ssd_mixer",
    )(z, xbc, dt, dtt, conv0, ssm0, *consts)


def _head_sums(t):
    pair = 2 * RWKV_HEAD_DIM
    first = lax.broadcasted_iota(jnp.int32, (1, pair), 1) < RWKV_HEAD_DIM
    pieces = []
    for j in range(RWKV_HEADS // 2):
        x = t[:, j * pair:(j + 1) * pair]
        x0 = jnp.where(first, x, 0.0)
        s0 = jnp.sum(x0, axis=-1, keepdims=True)
        s1 = jnp.sum(x - x0, axis=-1, keepdims=True)
        pieces.append(jnp.where(first, s0, s1))
    return jnp.concatenate(pieces, axis=1)


def _rwkv_prep_kernel(rw_ref, sh0_ref, mu_ref, w0_ref, w2_ref, a0_ref, a2_ref, g2_ref,
                      kk_ref, ka_ref, rk_ref,
                      *rest, tt, seqs, chunk):
    outs, full_scr = rest[:-1], rest[-1]
    shn_ref = outs[-1]
    c = pl.program_id(1)
    l = tt // seqs

    if seqs == 1:
        @pl.when(c == 0)
        def _():
            full_scr[0, CONV_PAD - 1:CONV_PAD, :] = sh0_ref[0]

        @pl.when(c > 0)
        def _():
            full_scr[0, CONV_PAD - 1:CONV_PAD, :] = full_scr[0, CONV_PAD + tt - 1:CONV_PAD + tt, :]

        rw = rw_ref[0]
        full_scr[0, CONV_PAD:CONV_PAD + tt, :] = rw
        prev = full_scr[0, CONV_PAD - 1:CONV_PAD - 1 + tt, :]
    else:
        full_scr[:, CONV_PAD - 1:CONV_PAD, :] = sh0_ref[...]
        full_scr[:, CONV_PAD:CONV_PAD + l, :] = rw_ref[...]
        rw = rw_ref[...].reshape(tt, RWKV_PROJ)
        prev = full_scr[:, CONV_PAD - 1:CONV_PAD - 1 + l, :].reshape(tt, RWKV_PROJ)
    vals = _rwkv_mix_math(rw, prev, mu_ref, w0_ref, w2_ref, a0_ref, a2_ref, g2_ref, kk_ref, ka_ref, rk_ref)
    _emit_rwkv_outputs(outs[:-1], vals, tt=tt, chunk=chunk)

    if seqs == 1:
        @pl.when(c == pl.num_programs(1) - 1)
        def _():
            shn_ref[0] = full_scr[0, CONV_PAD + tt - 1:CONV_PAD + tt, :]
    else:
        shn_ref[...] = rw_ref[:, l - 1:l, :]


def _rwkv_mix_math(rw, prev, mu_ref, w0_ref, w2_ref, a0_ref, a2_ref, g2_ref, kk_ref, ka_ref, rk_ref):
    wd = RWKV_WIDTH
    u = rw + (prev - rw) * mu_ref[...]
    r = u[:, :wd]
    k = u[:, wd:2 * wd]
    v = u[:, 2 * wd:3 * wd]
    w_lo = u[:, 3 * wd:3 * wd + DECAY_LORA]
    a_lo = u[:, 3 * wd + DECAY_LORA:3 * wd + DECAY_LORA + AAA_LORA]
    g_lo = u[:, 3 * wd + DECAY_LORA + AAA_LORA:]

    w_log = -_softplus(-(w0_ref[...] + _bdot(jnp.tanh(w_lo), w2_ref[...]))) - 0.5
    lw = -jnp.exp(w_log)
    a = _sigmoid(a0_ref[...] + _bdot(a_lo, a2_ref[...]))
    gate = _bdot(_sigmoid(g_lo), g2_ref[...])

    kk = k * kk_ref[...]
    kk = kk / jnp.maximum(jnp.sqrt(_head_sums(kk * kk)), 1e-12)
    kf = k * (1.0 + (a - 1.0) * ka_ref[...])
    kb = kk * a
    bonus = _head_sums(r * kf * rk_ref[...]) * v
    return r, lw, kf, v, kk, kb, gate, bonus


def _emit_rwkv_outputs(outs, vals, *, tt, chunk, row0=0):
    r, lw, kf, v, kk, kb, gate, bonus = vals
    wd = RWKV_WIDTH
    gate_out, bonus_out = outs[-2:]
    if chunk == 0:
        blk = gate_out.shape
        for ref, val in zip(outs, (r, jnp.exp(lw), kf, v, kk, kb, gate, bonus)):
            ref[...] = val.reshape(blk)
    else:
        gate_out[0, row0:row0 + tt, :] = gate
        bonus_out[0, row0:row0 + tt, :] = bonus
        kkt_out, rt_out, kfh_out, nbh_out, vb_out, kbg_out, kfg_out, gend_out = outs[:8]
        tri = (lax.broadcasted_iota(jnp.int32, (chunk, chunk), 0)
               >= lax.broadcasted_iota(jnp.int32, (chunk, chunk), 1)).astype(F32)
        for ci in range(tt // chunk):
            rs = slice(ci * chunk, (ci + 1) * chunk)
            ro = slice(row0 + ci * chunk, row0 + (ci + 1) * chunk)
            lw_c = lw[rs]
            cum = _dot01(tri, lw_c, exact_side="rhs")
            cum_end = cum[chunk - 1:chunk, :]
            g_inv = jnp.exp(-cum)
            g_tail = jnp.exp(cum_end - cum)
            kkt_out[0, ro, :] = (kk[rs] * jnp.exp(cum - lw_c)).astype(BF16)
            rt_out[0, ro, :] = (r[rs] * jnp.exp(cum)).astype(BF16)
            kfh_out[0, ro, :] = (kf[rs] * g_tail).astype(BF16)
            nbh_out[0, ro, :] = (-kb[rs] * g_tail).astype(BF16)
            vb_out[0, ro, :] = v[rs].astype(BF16)
            kbg_out[0, ro, :] = (kb[rs] * g_inv).astype(BF16)
            kfg_out[0, ro, :] = (kf[rs] * g_inv).astype(BF16)
            gend_out[0, row0 // chunk + ci] = jnp.broadcast_to(jnp.exp(cum_end), (8, wd))


def rwkv_prep(rw, shift0, w, *, tt, chunk):
    b, l, _ = rw.shape
    seqs = max(1, tt // l)
    assert chunk == 0 or (seqs == 1 and tt % chunk == 0)
    rowv = lambda t: t.reshape(1, -1)
    consts = [rowv(w["shift_mu"]), rowv(w["w0"]), w["w2"].astype(BF16), rowv(w["a0"]),
              w["a2"].astype(BF16), w["g2"].astype(BF16), rowv(w["k_k"]), rowv(w["k_a"]),
              rowv(w["r_k"])]
    rows = tt // seqs
    grid = (b // seqs, l // rows)
    seq = lambda wd: pl.BlockSpec((seqs, rows, wd), lambda i, c: (i, c, 0))
    one = pl.BlockSpec((seqs, 1, RWKV_PROJ), lambda i, c: (i, 0, 0))
    sds = jax.ShapeDtypeStruct
    f32_seq = sds((b, l, RWKV_WIDTH), F32)
    if chunk == 0:
        op_specs = [seq(RWKV_WIDTH)] * 6
        op_shapes = [f32_seq] * 6
    else:
        per_tile = tt // chunk
        op_specs = [seq(RWKV_WIDTH)] * 7 + [
            pl.BlockSpec((1, per_tile, 8, RWKV_WIDTH), lambda i, c: (i, c, 0, 0))]
        op_shapes = [sds((b, l, RWKV_WIDTH), BF16)] * 7 + [sds((b, l // chunk, 8, RWKV_WIDTH), F32)]
    outs = pl.pallas_call(
        functools.partial(_rwkv_prep_kernel, tt=tt, seqs=seqs, chunk=chunk),
        grid=grid,
        in_specs=[seq(RWKV_PROJ), one] + [_const_spec(t.shape) for t in consts],
        out_specs=op_specs + [seq(RWKV_WIDTH)] * 2 + [one],
        out_shape=op_shapes + [f32_seq] * 2 + [sds((b, 1, RWKV_PROJ), F32)],
        scratch_shapes=[pltpu.VMEM((seqs, CONV_PAD + rows, RWKV_PROJ), F32)],
        compiler_params=pltpu.CompilerParams(
            dimension_semantics=("parallel", "arbitrary"), vmem_limit_bytes=V7X_VMEM_LIMIT),
        name="rwkv_prep",
    )(rw, shift0, *consts)
    return outs


def _proj_prep_kernel(x_ref, sh0_ref, g_ref, wz_ref, wx_ref, wr_ref, wdt_ref, wdtt_ref,
                      mu_ref, w0_ref, w2_ref, a0_ref, a2_ref, g2_ref, kk_ref, ka_ref, rk_ref,
                      z_ref, xbc_ref, dt_ref, dtt_ref, *rest, tm, tiles_per_seq, chunk):
    outs, (new_scr, cur_scr) = rest[:-2], rest[-2:]
    shn_ref = outs[-1]
    i = pl.program_id(0)

    @pl.when(i == 0)
    def _():
        new_scr[...] = jnp.zeros(new_scr.shape, F32)
        cur_scr[...] = jnp.zeros(cur_scr.shape, F32)

    k = jnp.maximum(i - 1, 0)
    first = (k % tiles_per_seq) == 0
    cur_scr[CONV_PAD - 1:CONV_PAD, :] = jnp.where(first, sh0_ref[0], cur_scr[CONV_PAD + tm - 1:CONV_PAD + tm, :])
    cur_scr[CONV_PAD:CONV_PAD + tm, :] = new_scr[...]

    shn_ref[0] = cur_scr[CONV_PAD + tm - 1:CONV_PAD + tm, :]
    u = _rms(x_ref[...], g_ref[...]).astype(BF16)

    def project(piece):
        if piece == 0:
            z_ref[...] = jnp.dot(u, wz_ref[...], preferred_element_type=F32)
        elif piece == 1:
            xbc_ref[...] = jnp.dot(u, wx_ref[...], preferred_element_type=F32)
        elif piece == 2:
            cols = slice(0, 2 * RWKV_WIDTH)
            new_scr[:, cols] = jnp.dot(u, wr_ref[:, cols], preferred_element_type=F32)
        else:
            cols = slice(2 * RWKV_WIDTH, RWKV_PROJ)
            new_scr[:, cols] = jnp.dot(u, wr_ref[:, cols], preferred_element_type=F32)
            dt_ref[...] = jnp.dot(u, wdt_ref[...], preferred_element_type=F32)
            dtt_ref[...] = lax.dot_general(wdtt_ref[...], u, (((1,), (1,)), ((), ())),
                                           preferred_element_type=F32)

    def prepare(part, rows):
        lo = CONV_PAD + part * rows
        rw = cur_scr[lo:lo + rows, :]
        prev = cur_scr[lo - 1:lo - 1 + rows, :]
        vals = _rwkv_mix_math(rw, prev, mu_ref, w0_ref, w2_ref, a0_ref, a2_ref, g2_ref, kk_ref, ka_ref, rk_ref)
        _emit_rwkv_outputs(outs[:-1], vals, tt=rows, chunk=chunk, row0=part * rows)

    pieces = 4
    parts = min(pieces, tm // chunk)
    for part in range(parts):
        for piece in range(part * pieces // parts, (part + 1) * pieces // parts):
            project(piece)
        prepare(part, tm // parts)


def proj_prep(x, shift0, g, wz, wx, wr, wdt, w, *, tm, chunk):
    b, l, _ = x.shape
    n = b * l
    nt, tps = n // tm, l // tm
    assert l % tm == 0 and tm % chunk == 0
    x2 = x.reshape(n, D_MODEL)
    rowv = lambda t: t.reshape(1, -1)
    wdtt = wdt.T
    consts = [g, wz, wx, wr, wdt, wdtt,
              rowv(w["shift_mu"]), rowv(w["w0"]), w["w2"].astype(BF16), rowv(w["a0"]),
              w["a2"].astype(BF16), w["g2"].astype(BF16), rowv(w["k_k"]), rowv(w["k_a"]), rowv(w["r_k"])]
    ahead = lambda i: jnp.minimum(i, nt - 1)
    behind = lambda i: jnp.maximum(i - 1, 0)
    row_a = lambda w_: pl.BlockSpec((tm, w_), lambda i: (ahead(i), 0))
    seq_b = lambda w_: pl.BlockSpec((1, tm, w_), lambda i: (behind(i) // tps, behind(i) % tps, 0))
    one_b = pl.BlockSpec((1, 1, RWKV_PROJ), lambda i: (behind(i) // tps, 0, 0))
    per_tile = tm // chunk
    gend_spec = pl.BlockSpec((1, per_tile, 8, RWKV_WIDTH), lambda i: (behind(i) // tps, behind(i) % tps, 0, 0))
    sds = jax.ShapeDtypeStruct
    outs = pl.pallas_call(
        functools.partial(_proj_prep_kernel, tm=tm, tiles_per_seq=tps, chunk=chunk),
        grid=(nt + 1,),
        in_specs=[row_a(D_MODEL), one_b] + [_const_spec(t.shape) for t in consts],
        out_specs=[row_a(SSD_WIDTH), row_a(SSD_CONV_DIM), row_a(SSD_HEADS),
                   pl.BlockSpec((SSD_HEADS, tm), lambda i: (0, ahead(i)))]
                  + [seq_b(RWKV_WIDTH)] * 7 + [gend_spec] + [seq_b(RWKV_WIDTH)] * 2 + [one_b],
        out_shape=[sds((n, SSD_WIDTH), F32), sds((n, SSD_CONV_DIM), F32), sds((n, SSD_HEADS), F32),
                   sds((SSD_HEADS, n), F32)]
                  + [sds((b, l, RWKV_WIDTH), BF16)] * 7 + [sds((b, l // chunk, 8, RWKV_WIDTH), F32)]
                  + [sds((b, l, RWKV_WIDTH), F32)] * 2 + [sds((b, 1, RWKV_PROJ), F32)],
        scratch_shapes=[pltpu.VMEM((tm, RWKV_PROJ), F32), pltpu.VMEM((CONV_PAD + tm, RWKV_PROJ), F32)],
        compiler_params=pltpu.CompilerParams(
            dimension_semantics=("arbitrary",), vmem_limit_bytes=V7X_VMEM_LIMIT),
        name="proj_prep",
    )(x2, shift0, *consts)
    return outs


def _wkv_kernel(r_ref, w_ref, k_ref, v_ref, kk_ref, kka_ref, s0_ref,
                o_ref, sfin_ref, s_scr, vt_scr, ot_scr, *, steps):
    c = pl.program_id(1)
    n = RWKV_HEAD_DIM
    lanes = WKV_BATCH_BLOCK * RWKV_HEADS

    @pl.when(c == 0)
    def _():
        s_scr[...] = s0_ref[...].reshape(lanes, n * n).T.reshape(n, n, lanes)

    def to_pairs(ref, t):
        return ref[:, t].reshape(lanes, n).T

    def step(t, carry):
        r_t = to_pairs(r_ref, t)
        w_t = to_pairs(w_ref, t)
        k_t = to_pairs(k_ref, t)
        kk_t = to_pairs(kk_ref, t)
        kka_t = to_pairs(kka_ref, t)
        vt_scr[...] = to_pairs(v_ref, t)

        def per_value(vi, carry2):
            s_v = s_scr[vi]
            skk = jnp.sum(s_v * kk_t, axis=0, keepdims=True)
            v_row = vt_scr[pl.ds(vi, 1), :]
            s_new = s_v * w_t - skk * kka_t + v_row * k_t
            s_scr[vi] = s_new
            ot_scr[pl.ds(vi, 1), :] = jnp.sum(s_new * r_t, axis=0, keepdims=True)
            return carry2

        lax.fori_loop(0, n, per_value, 0, unroll=4)
        o_ref[:, t] = ot_scr[...].T.reshape(WKV_BATCH_BLOCK, RWKV_HEADS, n)
        return carry

    lax.fori_loop(0, steps, step, 0)

    @pl.when(c == pl.num_programs(1) - 1)
    def _():
        sfin_ref[...] = s_scr[...].reshape(n * n, lanes).T.reshape(
            WKV_BATCH_BLOCK, RWKV_HEADS, n, n)


def wkv_scan(r, w, k, v, kk, kka, s0, *, steps):
    b, l, _ = r.shape
    h, n = RWKV_HEADS, RWKV_HEAD_DIM
    assert b % WKV_BATCH_BLOCK == 0 and l % steps == 0
    ops = [t.reshape(b, l, h, n) for t in (r, w, k, v, kk, kka)]
    seq_spec = pl.BlockSpec((WKV_BATCH_BLOCK, steps, h, n), lambda g, c: (g, c, 0, 0))
    st_spec = pl.BlockSpec((WKV_BATCH_BLOCK, h, n, n), lambda g, c: (g, 0, 0, 0))
    o, s_fin = pl.pallas_call(
        functools.partial(_wkv_kernel, steps=steps),
        grid=(b // WKV_BATCH_BLOCK, l // steps),
        in_specs=[seq_spec] * 6 + [st_spec],
        out_specs=[seq_spec, st_spec],
        out_shape=[jax.ShapeDtypeStruct((b, l, h, n), F32),
                   jax.ShapeDtypeStruct((b, h, n, n), F32)],
        scratch_shapes=[pltpu.VMEM((n, n, WKV_BATCH_BLOCK * h), F32),
                        pltpu.VMEM((n, WKV_BATCH_BLOCK * h), F32),
                        pltpu.VMEM((n, WKV_BATCH_BLOCK * h), F32)],
        compiler_params=pltpu.CompilerParams(
            dimension_semantics=("parallel", "arbitrary"), vmem_limit_bytes=V7X_VMEM_LIMIT),
        name="wkv_scan",
    )(*ops, s0)
    return o.reshape(b, l, h * n), s_fin


WKV_LANE_BATCH = 128


def _wkv_batch_lanes_kernel(r_ref, w_ref, k_ref, v_ref, kk_ref, kka_ref, s0_ref,
                            o_ref, sfin_ref, op_scr, ot_scr, *, steps):
    n = RWKV_HEAD_DIM
    nb = WKV_LANE_BATCH
    sfin_ref[...] = s0_ref[...]

    def step(t, carry):
        rows = pl.ds(t, nb, stride=steps)
        for i, ref in enumerate((r_ref, w_ref, k_ref, kk_ref, kka_ref, v_ref)):
            op_scr[i] = ref[rows, :].T
        for h2 in range(2):
            ch = slice(h2 * n, (h2 + 1) * n)
            r_t = op_scr[0, ch, :]
            kka_r = jnp.sum(op_scr[4, ch, :] * r_t, axis=0, keepdims=True)
            k_r = jnp.sum(op_scr[2, ch, :] * r_t, axis=0, keepdims=True)
            op_scr[0, ch, :] = op_scr[1, ch, :] * r_t

            def per_value(vi, carry2, ch=ch, h2=h2, kka_r=kka_r, k_r=k_r):
                s_v = sfin_ref[h2, vi]
                skk = jnp.sum(s_v * op_scr[3, ch, :], axis=0, keepdims=True)
                out = jnp.sum(s_v * op_scr[0, ch, :], axis=0, keepdims=True)
                v_row = op_scr[5, pl.ds(h2 * n + vi, 1), :]
                sfin_ref[h2, vi] = s_v * op_scr[1, ch, :] - skk * op_scr[4, ch, :] + v_row * op_scr[2, ch, :]
                ot_scr[pl.ds(h2 * n + vi, 1), :] = out - skk * kka_r + v_row * k_r
                return carry2

            lax.fori_loop(0, n, per_value, 0, unroll=4)
        o_ref[rows, :] = ot_scr[...].T
        return carry

    lax.fori_loop(0, steps, step, 0)


def wkv_scan_batch_lanes(r, w, k, v, kk, kka, s0):
    b, l, wd = r.shape
    h, n, nb = RWKV_HEADS, RWKV_HEAD_DIM, WKV_LANE_BATCH
    assert b % nb == 0
    ops = [t.reshape(b * l, wd) for t in (r, w, k, v, kk, kka)]
    s0t = jnp.transpose(s0, (1, 2, 3, 0))
    seq_spec = pl.BlockSpec((nb * l, 2 * n), lambda g, j: (g, j))
    st_spec = pl.BlockSpec((2, n, n, nb), lambda g, j: (j, 0, 0, g))
    o, s_fin = pl.pallas_call(
        functools.partial(_wkv_batch_lanes_kernel, steps=l),
        grid=(b // nb, h // 2),
        in_specs=[seq_spec] * 6 + [st_spec],
        out_specs=[seq_spec, st_spec],
        out_shape=[jax.ShapeDtypeStruct((b * l, wd), F32), jax.ShapeDtypeStruct((h, n, n, b), F32)],
        scratch_shapes=[pltpu.VMEM((6, 2 * n, nb), F32), pltpu.VMEM((2 * n, nb), F32)],
        compiler_params=pltpu.CompilerParams(
            dimension_semantics=("parallel", "parallel"), vmem_limit_bytes=V7X_VMEM_LIMIT),
        name="wkv_scan_batch_lanes",
    )(*ops, s0t)
    return o.reshape(b, l, wd), jnp.transpose(s_fin, (3, 0, 1, 2))


WKV_CHUNK = 64
WKV_PAIRS = RWKV_HEADS // 2
WKV_ROW_STRIDE = WKV_CHUNK + 8
WKV_SOLVE_ROWS = 8
WKV_SOLVE_COLS = 16


def _pair_masks():
    c = WKV_CHUNK
    row = lax.broadcasted_iota(jnp.int32, (2 * c, 2 * c), 0)
    col = lax.broadcasted_iota(jnp.int32, (2 * c, 2 * c), 1)
    t, i = row % c, col % c
    keep = i <= t - jnp.where(row < c, 1, 0)
    sign = jnp.where(row >= c, jnp.where(col < c, -1.0, 1.0), 1.0)
    block_diag = row // c == col // c
    return keep, sign, block_diag


def _wkv_prepare_kernel(kkt_ref, rt_ref, kbg_ref, kfg_ref, vb_ref,
                        lo_ref, rhs0_ref, tp_ref, abs_scr, top_scr, abt_scr, tt_scr):
    c = WKV_CHUNK
    keep, sign, _ = _pair_masks()
    lane = lax.broadcasted_iota(jnp.int32, (1, 2 * c), 1)
    head0 = lane < RWKV_HEAD_DIM
    row_head0 = lax.broadcasted_iota(jnp.int32, (2 * RWKV_HEAD_DIM, 1), 0) < RWKV_HEAD_DIM
    zeros = jnp.zeros((c, 2 * c), BF16)

    def per_batch(b, carry):
        for j in range(WKV_PAIRS):
            sl = slice(j * 2 * RWKV_HEAD_DIM, (j + 1) * 2 * RWKV_HEAD_DIM)
            lhs = jnp.concatenate([kkt_ref[b, :, sl], rt_ref[b, :, sl]], axis=0)
            rhs = jnp.concatenate([kbg_ref[b, :, sl], kfg_ref[b, :, sl]], axis=0)
            rhs_t = rhs.astype(F32).T
            top = jnp.where(row_head0, rhs_t, 0.0)
            w_a = jnp.concatenate([top, rhs_t - top], axis=1).astype(BF16)
            a_both = jnp.dot(lhs, w_a, preferred_element_type=F32)
            for h2 in range(2):
                a = a_both[:, h2 * 2 * c:(h2 + 1) * 2 * c]
                a = jnp.where(keep, a, 0.0) * sign
                inst = h2 * (WKV_BATCH_BLOCK * WKV_PAIRS) + b * WKV_PAIRS + j
                abs_scr[pl.ds(inst * WKV_ROW_STRIDE, c), :c] = a[:c, :c]
                top_scr[j, :, h2 * 2 * c:(h2 + 1) * 2 * c] = a[:c].astype(BF16)
                lo_ref[b, :, (2 * j + h2) * 2 * c:(2 * j + h2 + 1) * 2 * c] = a[c:].astype(BF16)

        for j in range(WKV_PAIRS):
            sl = slice(j * 2 * RWKV_HEAD_DIM, (j + 1) * 2 * RWKV_HEAD_DIM)
            v = vb_ref[b, :, sl]
            v0 = jnp.where(head0, v, jnp.zeros_like(v))
            w_akf = jnp.concatenate([zeros, v0, zeros, v - v0], axis=0)
            rhs0_ref[b, :, sl] = jnp.dot(top_scr[j], w_akf, preferred_element_type=F32)
        return carry

    lax.fori_loop(0, WKV_BATCH_BLOCK, per_batch, 0)

    n_inst = 2 * WKV_BATCH_BLOCK * WKV_PAIRS
    n_pair_rows = WKV_BATCH_BLOCK * WKV_PAIRS

    def to_lanes(t, carry):
        abt_scr[t] = abs_scr[pl.ds(t, n_inst, stride=WKV_ROW_STRIDE), :][:, :c].T
        return carry

    lax.fori_loop(0, c, to_lanes, 0, unroll=8)

    tt_scr[...] = jnp.zeros(tt_scr.shape, F32)
    sub_iota = lax.broadcasted_iota(jnp.int32, (WKV_SOLVE_COLS, n_inst), 0)
    rows = range(WKV_SOLVE_ROWS)
    for cb in range(c // WKV_SOLVE_COLS):
        col0 = WKV_SOLVE_COLS * cb
        cols = slice(col0, col0 + WKV_SOLVE_COLS)
        first_block = col0 // WKV_SOLVE_ROWS

        def solve_rows(tb, carry, col0=col0, cols=cols, first_block=first_block):
            t0 = tb * WKV_SOLVE_ROWS

            def sub(ib, accs):
                ps = [tt_scr[ib * WKV_SOLVE_ROWS + di, cols, :] for di in rows]
                out = []
                for r in rows:
                    terms = [abt_scr[t0 + r, pl.ds(ib * WKV_SOLVE_ROWS + di, 1), :] * ps[di] for di in rows]
                    while len(terms) > 1:
                        terms = [a + b for a, b in zip(terms[::2], terms[1::2])]
                    out.append(accs[r] - terms[0])
                return tuple(out)

            unit = tuple(jnp.where(sub_iota + col0 == t0 + r, 1.0, 0.0) for r in rows)
            accs = list(lax.fori_loop(first_block, tb, sub, unit))
            for r in rows:
                for r2 in range(r):
                    accs[r] = accs[r] - abt_scr[t0 + r, pl.ds(t0 + r2, 1), :] * accs[r2]
                tt_scr[t0 + r, cols, :] = accs[r]
            return carry

        lax.fori_loop(first_block, c // WKV_SOLVE_ROWS, solve_rows, 0)

    def from_lanes(t, carry):
        m = tt_scr[t].T
        abs_scr[pl.ds(t, n_pair_rows, stride=WKV_ROW_STRIDE), :] = jnp.concatenate(
            [m[:n_pair_rows], m[n_pair_rows:]], axis=1)
        return carry

    lax.fori_loop(0, c, from_lanes, 0, unroll=8)

    def emit(b, carry):
        for j in range(WKV_PAIRS):
            row0 = (b * WKV_PAIRS + j) * WKV_ROW_STRIDE
            tp_ref[b, :, j * 2 * c:(j + 1) * 2 * c] = abs_scr[pl.ds(row0, c), :].astype(BF16)
        return carry

    lax.fori_loop(0, WKV_BATCH_BLOCK, emit, 0)


def _wkv_apply_kernel(kkt_ref, rt_ref, kfh_ref, nbh_ref, vb_ref, tp_ref, lo_ref, rhs0_ref, gend_ref, s0_ref,
                      o_ref, sfin_ref, x_scr, wp_scr, p_scr):
    c = WKV_CHUNK
    n = RWKV_HEAD_DIM
    ch = pl.program_id(1)
    _, _, block_diag = _pair_masks()
    lane = lax.broadcasted_iota(jnp.int32, (1, 2 * c), 1)
    head0 = lane < n
    eye2 = (lax.broadcasted_iota(jnp.int32, (n, 2 * n), 0)
            == lax.broadcasted_iota(jnp.int32, (n, 2 * n), 1) % n).astype(F32)

    @pl.when(ch == 0)
    def _():
        def init(b, carry):
            for j in range(WKV_PAIRS):
                sp = s0_ref[b, 2 * j:2 * j + 2].reshape(2 * n, n)
                dup = jnp.dot(sp, eye2, precision=HIGHEST, preferred_element_type=F32)
                x_scr[b, j] = jnp.where(block_diag, dup, 0.0)
            return carry
        lax.fori_loop(0, WKV_BATCH_BLOCK, init, 0)

    def per_batch(b, carry):
        for j in range(WKV_PAIRS):
            sl = slice(j * 2 * n, (j + 1) * 2 * n)
            lhs = jnp.concatenate([kkt_ref[b, :, sl], rt_ref[b, :, sl]], axis=0)
            kx = lax.dot_general(lhs, x_scr[b, j].astype(BF16), (((1,), (1,)), ((), ())),
                                 preferred_element_type=F32)
            rhs = kx[:c] + rhs0_ref[b, :, sl]
            r0 = jnp.where(head0, rhs, 0.0)
            wp_scr[j] = jnp.concatenate([r0, rhs - r0], axis=0).astype(BF16)
            o_ref[b, :, sl] = kx[c:]
        for j in range(WKV_PAIRS):
            sl = slice(j * 2 * n, (j + 1) * 2 * n)
            p = jnp.dot(tp_ref[b, :, sl], wp_scr[j], preferred_element_type=F32)
            p_scr[j] = p.astype(BF16)
        for j in range(WKV_PAIRS):
            sl = slice(j * 2 * n, (j + 1) * 2 * n)
            v = vb_ref[b, :, sl]
            pb = p_scr[j]
            zero = jnp.zeros_like(pb)
            p0, v0 = jnp.where(head0, pb, zero), jnp.where(head0, v, zero)
            w_o = jnp.concatenate([p0, v0, pb - p0, v - v0], axis=0)
            o_ref[b, :, sl] = o_ref[b, :, sl] + jnp.dot(
                lo_ref[b, :, j * 4 * c:(j + 1) * 4 * c], w_o, preferred_element_type=F32)
            vp = jnp.concatenate([v, pb], axis=0)
            kb = jnp.concatenate([kfh_ref[b, :, sl], nbh_ref[b, :, sl]], axis=0)
            upd = lax.dot_general(vp, kb, (((0,), (0,)), ((), ())), preferred_element_type=F32)
            x_scr[b, j] = jnp.where(block_diag, x_scr[b, j] * gend_ref[b, 0, 0:1, sl] + upd, 0.0)
        return carry

    lax.fori_loop(0, WKV_BATCH_BLOCK, per_batch, 0)

    @pl.when(ch == pl.num_programs(1) - 1)
    def _():
        def fin(b, carry):
            for j in range(WKV_PAIRS):
                sp = lax.dot_general(x_scr[b, j], eye2, (((1,), (1,)), ((), ())),
                                     precision=HIGHEST, preferred_element_type=F32)
                sfin_ref[b, 2 * j:2 * j + 2] = sp.reshape(2, n, n)
            return carry
        lax.fori_loop(0, WKV_BATCH_BLOCK, fin, 0)


def wkv_chunked(kkt, rt, kfh, nbh, vb, kbg, kfg, gend, s0):
    b, l, wd = kkt.shape
    c = WKV_CHUNK
    assert b % WKV_BATCH_BLOCK == 0 and l % c == 0
    gb, nc = b // WKV_BATCH_BLOCK, l // c
    lanes = 2 * WKV_BATCH_BLOCK * WKV_PAIRS
    seq = lambda w_: pl.BlockSpec((WKV_BATCH_BLOCK, c, w_), lambda g, i: (g, i, 0))
    gend_spec = pl.BlockSpec((WKV_BATCH_BLOCK, 1, 8, wd), lambda g, i: (g, i, 0, 0))
    sds = jax.ShapeDtypeStruct
    st_spec = pl.BlockSpec((WKV_BATCH_BLOCK, RWKV_HEADS, RWKV_HEAD_DIM, RWKV_HEAD_DIM),
                           lambda g, i: (g, 0, 0, 0))
    blk = (WKV_BATCH_BLOCK, c, wd)
    o, s_fin = pl.pallas_call(
        _wkv_chunk_kernel,
        grid=(gb, nc),
        in_specs=[seq(wd)] * 7 + [gend_spec, st_spec],
        out_specs=[seq(wd), st_spec],
        out_shape=[sds((b, l, wd), F32), sds(s0.shape, F32)],
        scratch_shapes=[
            pltpu.VMEM((lanes * WKV_ROW_STRIDE, 2 * c), F32),
            pltpu.VMEM((WKV_PAIRS, c, 4 * c), BF16),
            pltpu.VMEM((c, c, lanes), F32),
            pltpu.VMEM((c, c, lanes), F32),
            pltpu.VMEM((WKV_BATCH_BLOCK, c, 2 * wd), BF16),
            pltpu.VMEM(blk, F32),
            pltpu.VMEM(blk, BF16),
            pltpu.VMEM((WKV_BATCH_BLOCK, WKV_PAIRS, 2 * RWKV_HEAD_DIM, 2 * RWKV_HEAD_DIM), F32),
            pltpu.VMEM((WKV_PAIRS, 2 * c, 2 * RWKV_HEAD_DIM), BF16),
            pltpu.VMEM((WKV_PAIRS, c, 2 * RWKV_HEAD_DIM), BF16)],
        compiler_params=pltpu.CompilerParams(
            dimension_semantics=("parallel", "arbitrary"), vmem_limit_bytes=V7X_VMEM_LIMIT),
        name="wkv_chunked",
    )(kkt, rt, kfh, nbh, vb, kbg, kfg, gend, s0)
    return o, s_fin


def _wkv_chunk_kernel(kkt_ref, rt_ref, kfh_ref, nbh_ref, vb_ref, kbg_ref, kfg_ref, gend_ref, s0_ref,
                      o_ref, sfin_ref, abs_scr, top_scr, abt_scr, tt_scr, lo_scr, rhs0_scr, tp_scr,
                      x_scr, wp_scr, p_scr):
    _wkv_prepare_kernel(kkt_ref, rt_ref, kbg_ref, kfg_ref, vb_ref, lo_scr, rhs0_scr, tp_scr,
                        abs_scr, top_scr, abt_scr, tt_scr)
    _wkv_apply_kernel(kkt_ref, rt_ref, kfh_ref, nbh_ref, vb_ref, tp_scr, lo_scr, rhs0_scr, gend_ref, s0_ref,
                      o_ref, sfin_ref, x_scr, wp_scr, p_scr)


def _tail_kernel(o_ref, gate_ref, bonus_ref, x_ref, ys_ref, p_ref, lnw_ref, lnb_ref, woa_ref, wob_ref,
                 nf_ref, wg_ref, wu_ref, wd_ref, np_ref, wpg_ref, wpp_ref, nl_ref, y_ref, new_scr, cur_scr):
    i = pl.program_id(0)

    @pl.when(i == 0)
    def _():
        new_scr[...] = jnp.zeros(new_scr.shape, BF16)

    cur_scr[...] = new_scr[...]

    parts = 4
    rows = o_ref.shape[0] // parts
    inv_n = 1.0 / RWKV_HEAD_DIM

    def vector_half(part):
        rs = slice(part * rows, (part + 1) * rows)
        o = o_ref[rs, :]
        mu = _head_sums(o) * inv_n
        d = o - mu
        var = _head_sums(d * d) * inv_n
        on = d * lax.rsqrt(var + GN_EPS) * lnw_ref[...] + lnb_ref[...]
        new_scr[rs, :] = ((on + bonus_ref[rs, :]) * gate_ref[rs, :]).astype(BF16)

    vector_half(0)
    y_rwkv = cur_scr[...]
    h = x_ref[...] + _bdot(ys_ref[...], woa_ref[...]) + jnp.dot(y_rwkv, wob_ref[...],
                                                                preferred_element_type=F32)
    hf = _rms(h, nf_ref[...]).astype(BF16)
    vector_half(1)
    gate = jnp.dot(hf, wg_ref[...], preferred_element_type=F32)
    up = jnp.dot(hf, wu_ref[...], preferred_element_type=F32)
    vector_half(2)
    h = h + _bdot(_silu(gate) * up, wd_ref[...])
    vector_half(3)
    pg = _sigmoid(_bdot(_rms(h, np_ref[...]), wpg_ref[...]))
    h = h + pg * _bdot(p_ref[...], wpp_ref[...])
    y_ref[...] = _rms(h, nl_ref[...])


def layer_tail(o, gate, bonus, x, y_ssd, p, consts, *, tm):
    n = x.shape[0]
    nt = n // tm
    ahead = lambda w_: pl.BlockSpec((tm, w_), lambda i: (jnp.minimum(i, nt - 1), 0))
    behind = lambda w_: pl.BlockSpec((tm, w_), lambda i: (jnp.maximum(i - 1, 0), 0))
    return pl.pallas_call(
        _tail_kernel,
        grid=(nt + 1,),
        in_specs=[ahead(D_MODEL)] * 3 + [behind(D_MODEL)] * 2 + [behind(PLE_DIM)]
                 + [_const_spec(t.shape) for t in consts],
        out_specs=behind(D_MODEL),
        out_shape=jax.ShapeDtypeStruct((n, D_MODEL), F32),
        scratch_shapes=[pltpu.VMEM((tm, D_MODEL), BF16), pltpu.VMEM((tm, D_MODEL), BF16)],
        compiler_params=pltpu.CompilerParams(
            dimension_semantics=("arbitrary",), vmem_limit_bytes=V7X_VMEM_LIMIT),
        name="layer_tail",
    )(o, gate, bonus, x, y_ssd, p, *consts)


def _prepare_weights(w):
    c0, c1, c2 = SSD_WIDTH, SSD_WIDTH + SSD_CONV_DIM, SSD_WIDTH + SSD_CONV_DIM + SSD_HEADS
    w_in = w["w_in"]
    rowv = lambda t: t.reshape(1, -1)
    return dict(
        w,
        wz=w_in[:, :c0].astype(BF16), wx=w_in[:, c0:c1].astype(BF16),
        wdt=w_in[:, c1:c2].astype(BF16), wr=w_in[:, c2:].astype(BF16),
        woa=w["w_out"][:SSD_WIDTH].astype(BF16), wob=w["w_out"][SSD_WIDTH:].astype(BF16),
        wg=w["w_gate"].astype(BF16), wu=w["w_up"].astype(BF16), wd=w["w_down"].astype(BF16),
        wpg=w["w_ple_gate"].astype(BF16), wpp=w["w_ple_proj"].astype(BF16),
        norm_mix_r=rowv(w["norm_mix"]), norm_ffn_r=rowv(w["norm_ffn"]),
        norm_ple_r=rowv(w["norm_ple"]), norm_final_r=rowv(w["norm_final"]),
        ln_x_w_r=rowv(w["ln_x_w"]), ln_x_b_r=rowv(w["ln_x_b"]),
    )


def layer_forward(x, p, conv0, shift0, ssm0, wkv0, w, *, tm, ssd_q, prep_tt, wkv_steps):
    b, l, _ = x.shape
    n = b * l
    x2 = x.reshape(n, D_MODEL)
    chunked = l % WKV_CHUNK == 0 and l % tm == 0
    if chunked:
        z, xbc, dt, dtt, *ops, gate, bonus, shift_new = proj_prep(
            x, shift0, w["norm_mix_r"], w["wz"], w["wx"], w["wr"], w["wdt"], w, tm=tm, chunk=WKV_CHUNK)
    else:
        z, xbc, rw, dt, dtt = in_projection(x2, w["norm_mix_r"], w["wz"], w["wx"], w["wr"], w["wdt"], tm=tm)
        *ops, gate, bonus, shift_new = rwkv_prep(rw.reshape(b, l, -1), shift0, w, tt=prep_tt, chunk=0)
    y_ssd, ssm_new, conv_new = ssd_mixer(
        z.reshape(b, l, -1), xbc.reshape(b, l, -1), dt.reshape(b, l, -1), dtt, conv0, ssm0, w, q=ssd_q)
    if chunked:
        o, wkv_new = wkv_chunked(*ops, wkv0)
    elif b % WKV_LANE_BATCH == 0:
        o, wkv_new = wkv_scan_batch_lanes(*ops, wkv0)
    else:
        o, wkv_new = wkv_scan(*ops, wkv0, steps=wkv_steps)
    flat = lambda t: t.reshape(n, -1)
    tail_consts = [w["ln_x_w_r"], w["ln_x_b_r"], w["woa"], w["wob"], w["norm_ffn_r"], w["wg"], w["wu"],
                   w["wd"], w["norm_ple_r"], w["wpg"], w["wpp"], w["norm_final_r"]]
    y = layer_tail(flat(o), flat(gate), flat(bonus), x2, flat(y_ssd), p.reshape(n, PLE_DIM),
                   tail_consts, tm=tm)
    return y.reshape(b, l, D_MODEL), ssm_new, conv_new, wkv_new, shift_new


def kernel(x_prompt, x_sample, state_ssm, state_conv, state_wkv, state_shift, p_prompt, p_sample, norm_mix, w_in, conv_w, conv_b, dt_bias, a_log, d_skip, ssd_norm, shift_mu, w0, w2, a0, a2, g2, k_k, k_a, r_k, ln_x_w, ln_x_b, w_out, norm_ffn, w_gate, w_up, w_down, norm_ple, w_ple_gate, w_ple_proj, norm_final):
    w = _prepare_weights(dict(
        norm_mix=norm_mix[0], w_in=w_in[0], conv_w=conv_w[0], conv_b=conv_b[0], dt_bias=dt_bias[0],
        a_log=a_log[0], d_skip=d_skip[0], ssd_norm=ssd_norm[0], shift_mu=shift_mu[0], w0=w0[0],
        w2=w2[0], a0=a0[0], a2=a2[0], g2=g2[0], k_k=k_k[0], k_a=k_a[0], r_k=r_k[0],
        ln_x_w=ln_x_w[0], ln_x_b=ln_x_b[0], w_out=w_out[0], norm_ffn=norm_ffn[0],
        w_gate=w_gate[0], w_up=w_up[0], w_down=w_down[0], norm_ple=norm_ple[0],
        w_ple_gate=w_ple_gate[0], w_ple_proj=w_ple_proj[0], norm_final=norm_final))
    bp = x_prompt.shape[0]
    zeros = lambda *s: jnp.zeros(s, F32)
    yp, s1, c1, k1, t1 = layer_forward(
        x_prompt, p_prompt[0], zeros(bp, SSD_CONV - 1, SSD_CONV_DIM), zeros(bp, 1, RWKV_PROJ),
        zeros(bp, SSD_HEADS, SSD_HEAD_DIM, SSD_STATE),
        zeros(bp, RWKV_HEADS, RWKV_HEAD_DIM, RWKV_HEAD_DIM), w,
        tm=256, ssd_q=min(SSD_CHUNK, x_prompt.shape[1]), prep_tt=min(128, x_prompt.shape[1]),
        wkv_steps=min(16, x_prompt.shape[1]))
    ys, s2, c2, k2, t2 = layer_forward(
        x_sample, p_sample[0], state_conv[0], state_shift[0], state_ssm[0], state_wkv[0], w,
        tm=256, ssd_q=x_sample.shape[1], prep_tt=128, wkv_steps=x_sample.shape[1])
    return (yp, ys, s1[None], c1[None], k1[None], t1[None], s2[None], c2[None], k2[None], t2[None])
```

```python
import functools

import jax
import jax.numpy as jnp
from jax import lax
from jax.experimental import pallas as pl
from jax.experimental.pallas import tpu as pltpu

F32 = jnp.float32
BF16 = jnp.bfloat16
HIGHEST = lax.Precision.HIGHEST

D_MODEL = 1024
SSD_WIDTH = 1024
SSD_HEADS = 16
SSD_HEAD_DIM = 64
SSD_GROUPS = 2
SSD_GROUP_WIDTH = SSD_WIDTH // SSD_GROUPS
SSD_STATE = 128
SSD_CONV = 4
SSD_CHUNK = 128
SSD_BC = SSD_GROUPS * SSD_STATE
SSD_CONV_DIM = SSD_WIDTH + 2 * SSD_BC
RWKV_WIDTH = 1024
RWKV_HEADS = 16
RWKV_HEAD_DIM = 64
DECAY_LORA = 64
AAA_LORA = 64
GATE_LORA = 128
RWKV_PROJ = 3 * RWKV_WIDTH + DECAY_LORA + AAA_LORA + GATE_LORA
D_FF = 2816
PLE_DIM = 256
NORM_EPS = 1e-6
GN_EPS = 64e-5

WKV_BATCH_BLOCK = 8
V7X_VMEM_LIMIT = 56 * 1024 * 1024
CONV_PAD = 8
SSD_SEQS_PER_STEP = 8
SSD_LONG_SEQS_PER_STEP = 1


def _rms(x, g):
    return x * lax.rsqrt(jnp.mean(x * x, axis=-1, keepdims=True) + NORM_EPS) * g


def _sigmoid(x):
    return 1.0 / (1.0 + jnp.exp(-x))


def _silu(x):
    return x * _sigmoid(x)


def _softplus(x):
    return jnp.maximum(x, 0.0) + jnp.log(1.0 + jnp.exp(-jnp.abs(x)))


def _bdot(a, b):
    return jnp.dot(a.astype(BF16), b.astype(BF16), preferred_element_type=F32)


def _split3(t):
    hi = t.astype(BF16)
    r1 = t - hi.astype(F32)
    mid = r1.astype(BF16)
    lo = (r1 - mid.astype(F32)).astype(BF16)
    return hi, mid, lo


def _dot01(a, b, *, exact_side):
    if exact_side == "lhs":
        m = b.astype(BF16)
        return sum(jnp.dot(p, m, preferred_element_type=F32) for p in _split3(a))
    m = a.astype(BF16)
    return sum(jnp.dot(m, p, preferred_element_type=F32) for p in _split3(b))


def _const_spec(shape):
    return pl.BlockSpec(shape, lambda *_: (0,) * len(shape), pipeline_mode=pl.Buffered(1))


def _head_expand(rows):
    h = lax.broadcasted_iota(jnp.int32, (rows, SSD_WIDTH), 0)
    c = lax.broadcasted_iota(jnp.int32, (rows, SSD_WIDTH), 1)
    return (c // SSD_HEAD_DIM == h).astype(F32)


def _proj_kernel(x_ref, g_ref, wz_ref, wx_ref, wr_ref, wdt_ref, wdtt_ref,
                 z_ref, xbc_ref, rw_ref, dt_ref, dtt_ref):
    u = _rms(x_ref[...], g_ref[...]).astype(BF16)
    z_ref[...] = jnp.dot(u, wz_ref[...], preferred_element_type=F32)
    xbc_ref[...] = jnp.dot(u, wx_ref[...], preferred_element_type=F32)
    rw_ref[...] = jnp.dot(u, wr_ref[...], preferred_element_type=F32)
    dt_ref[...] = jnp.dot(u, wdt_ref[...], preferred_element_type=F32)
    dtt_ref[...] = lax.dot_general(wdtt_ref[...], u, (((1,), (1,)), ((), ())), preferred_element_type=F32)


def in_projection(x, g, wz, wx, wr, wdt, *, tm):
    n = x.shape[0]
    row = lambda w: pl.BlockSpec((tm, w), lambda i: (i, 0))
    wdtt = wdt.T
    return pl.pallas_call(
        _proj_kernel,
        grid=(n // tm,),
        in_specs=[row(D_MODEL), _const_spec((1, D_MODEL)), _const_spec(wz.shape),
                  _const_spec(wx.shape), _const_spec(wr.shape), _const_spec(wdt.shape),
                  _const_spec(wdtt.shape)],
        out_specs=[row(SSD_WIDTH), row(SSD_CONV_DIM), row(RWKV_PROJ), row(SSD_HEADS),
                   pl.BlockSpec((SSD_HEADS, tm), lambda i: (0, i))],
        out_shape=[jax.ShapeDtypeStruct((n, SSD_WIDTH), F32),
                   jax.ShapeDtypeStruct((n, SSD_CONV_DIM), F32),
                   jax.ShapeDtypeStruct((n, RWKV_PROJ), F32),
                   jax.ShapeDtypeStruct((n, SSD_HEADS), F32),
                   jax.ShapeDtypeStruct((SSD_HEADS, n), F32)],
        compiler_params=pltpu.CompilerParams(
            dimension_semantics=("parallel",), vmem_limit_bytes=V7X_VMEM_LIMIT),
        name="in_projection",
    )(x, g, wz, wx, wr, wdt, wdtt)


def _ssd_kernel(z_ref, xbc_ref, dt_ref, dtt_ref, hist_ref, h0_ref, cw_ref, cb_ref,
                dtb_ref, dtbt_ref, alog_ref, alogt_ref, dsk_ref, nrm_ref,
                y_ref, hfin_ref, cnew_ref, xfull_scr, h_scr, *, q, nseq, single_chunk):
    refs = (z_ref, xbc_ref, dt_ref, dtt_ref, hist_ref, h0_ref, cw_ref, cb_ref, dtb_ref, dtbt_ref, alog_ref,
            alogt_ref, dsk_ref, nrm_ref, y_ref, hfin_ref, cnew_ref, xfull_scr, h_scr)
    stages = [_ssd_sequence(s, *refs, q=q, single_chunk=single_chunk) for s in range(nseq)]
    for _ in zip(*stages):
        pass


def _ssd_sequence(s, z_ref, xbc_ref, dt_ref, dtt_ref, hist_ref, h0_ref, cw_ref, cb_ref,
                  dtb_ref, dtbt_ref, alog_ref, alogt_ref, dsk_ref, nrm_ref,
                  y_ref, hfin_ref, cnew_ref, xfull_scr, h_scr, *, q, single_chunk):
    c = pl.program_id(1)
    last = pl.num_programs(1) - 1
    gw = SSD_GROUP_WIDTH

    @pl.when(c == 0)
    def _():
        xfull_scr[s,CONV_PAD - 3:CONV_PAD, :] = hist_ref[s]
        if not single_chunk:
            for g in range(SSD_GROUPS):
                h_scr[s * SSD_GROUPS + g] = h0_ref[s,g * 8:(g + 1) * 8].reshape(gw, SSD_STATE).T

    @pl.when(c > 0)
    def _():
        xfull_scr[s,CONV_PAD - 3:CONV_PAD, :] = xfull_scr[s,CONV_PAD + q - 3:CONV_PAD + q, :]

    xfull_scr[s,CONV_PAD:CONV_PAD + q, :] = xbc_ref[s]

    conv = cb_ref[...]
    for j in range(SSD_CONV):
        lo = CONV_PAD - 3 + j
        conv = conv + xfull_scr[s,lo:lo + q, :] * cw_ref[j:j + 1, :]
    act = _silu(conv)
    xs = act[:, :SSD_WIDTH]
    yield

    dt = _softplus(dt_ref[s] + dtb_ref[...])
    dtt_raw = dtt_ref[...] if len(dtt_ref.shape) == 2 else dtt_ref[s]
    dtt = _softplus(dtt_raw + dtbt_ref[...])
    da = dt * -jnp.exp(alog_ref[...])
    dat = dtt * -jnp.exp(alogt_ref[...])
    row = lax.broadcasted_iota(jnp.int32, (q, q), 0)
    col = lax.broadcasted_iota(jnp.int32, (q, q), 1)
    causal = row >= col
    a_cum = _dot01(causal.astype(F32), da, exact_side="rhs")
    a_cumt = _dot01(dat, (row <= col).astype(F32), exact_side="lhs")
    yield

    expand = _head_expand(SSD_HEADS)
    a_cum_x = _dot01(a_cum, expand, exact_side="lhs")
    dt_x = _dot01(dt, expand, exact_side="lhs")
    a_end_x = a_cum_x[q - 1:q, :]
    decay_in_x = jnp.exp(a_cum_x)
    chunk_decay_x = jnp.exp(a_end_x)
    xd = xs * (jnp.exp(a_end_x - a_cum_x) * dt_x)
    yield

    ys = []
    for g in range(SSD_GROUPS):
        bm = act[:, SSD_WIDTH + g * SSD_STATE:SSD_WIDTH + (g + 1) * SSD_STATE]
        cm = act[:, SSD_WIDTH + SSD_BC + g * SSD_STATE:SSD_WIDTH + SSD_BC + (g + 1) * SSD_STATE]
        cb = lax.dot_general(cm.astype(BF16), bm.astype(BF16), (((1,), (1,)), ((), ())),
                             preferred_element_type=F32)
        yield
        y_heads = []
        for e in range(8):
            h = g * 8 + e
            seg = a_cum[:, h:h + 1] - a_cumt[h:h + 1, :]
            lmat = jnp.where(causal, jnp.exp(jnp.where(causal, seg, 0.0)), 0.0)
            w_qs = cb * lmat * dtt[h:h + 1, :]
            y_heads.append(_bdot(w_qs, xs[:, h * SSD_HEAD_DIM:(h + 1) * SSD_HEAD_DIM]))
            yield
        y_diag = jnp.concatenate(y_heads, axis=1)
        sl = slice(g * gw, (g + 1) * gw)
        if single_chunk:
            h_in = h0_ref[s,g * 8:(g + 1) * 8].reshape(gw, SSD_STATE)
            y_off = lax.dot_general(cm.astype(BF16), h_in.astype(BF16), (((1,), (1,)), ((), ())),
                                    preferred_element_type=F32)
            upd = lax.dot_general(xd[:, sl].astype(BF16), bm.astype(BF16), (((0,), (0,)), ((), ())),
                                  preferred_element_type=F32)
            head_decay = jnp.broadcast_to(jnp.exp(a_cumt[:, q - 1:q]), (SSD_HEADS, SSD_STATE))
            for e in range(8):
                h = g * 8 + e
                hfin_ref[s,h] = (h0_ref[s,h] * head_decay[h:h + 1, :]
                                  + upd[e * SSD_HEAD_DIM:(e + 1) * SSD_HEAD_DIM, :])
        else:
            h_in = h_scr[s * SSD_GROUPS + g]
            y_off = _bdot(cm, h_in)
            upd = lax.dot_general(bm.astype(BF16), xd[:, sl].astype(BF16), (((0,), (0,)), ((), ())),
                                  preferred_element_type=F32)
            h_scr[s * SSD_GROUPS + g] = h_in * chunk_decay_x[:, sl] + upd
        ys.append(y_diag + y_off * decay_in_x[:, sl])
        yield

    y = jnp.concatenate(ys, axis=1) + dsk_ref[...] * xs
    yg = y * _silu(z_ref[s])
    outs = []
    for g in range(SSD_GROUPS):
        t = yg[:, g * gw:(g + 1) * gw]
        outs.append(t * lax.rsqrt(jnp.mean(t * t, axis=-1, keepdims=True) + NORM_EPS))
    y_ref[s] = jnp.concatenate(outs, axis=1) * nrm_ref[...]

    @pl.when(c == last)
    def _():
        cnew_ref[s] = xfull_scr[s,CONV_PAD + q - 3:CONV_PAD + q, :]
        if not single_chunk:
            for g in range(SSD_GROUPS):
                hfin_ref[s,g * 8:(g + 1) * 8] = h_scr[s * SSD_GROUPS + g].T.reshape(8, SSD_HEAD_DIM, SSD_STATE)

    yield


def ssd_mixer(z, xbc, dt, dtt_flat, conv0, ssm0, w, *, q):
    b, l, _ = z.shape
    single_chunk = l == q
    want = SSD_SEQS_PER_STEP if single_chunk else SSD_LONG_SEQS_PER_STEP
    nseq = want if b % want == 0 else 1
    if q % 128 == 0 and nseq == 1:
        dtt = dtt_flat
        dtt_spec = pl.BlockSpec((SSD_HEADS, q), lambda i, c: (0, i * (l // q) + c))
    else:
        dtt = jnp.swapaxes(dt, 1, 2)
        dtt_spec = pl.BlockSpec((nseq, SSD_HEADS, q), lambda i, c: (i, 0, c))
    seq = lambda wd: pl.BlockSpec((nseq, q, wd), lambda i, c: (i, c, 0))
    per_b3 = lambda s: pl.BlockSpec((nseq,) + s, lambda i, c: (i,) + (0,) * len(s))
    col = lambda t: t.reshape(-1, 1)
    rowv = lambda t: t.reshape(1, -1)
    consts = [w["conv_w"], rowv(w["conv_b"]), rowv(w["dt_bias"]), col(w["dt_bias"]),
              rowv(w["a_log"]), col(w["a_log"]),
              rowv(jnp.repeat(w["d_skip"], SSD_HEAD_DIM)), rowv(w["ssd_norm"])]
    return pl.pallas_call(
        functools.partial(_ssd_kernel, q=q, nseq=nseq, single_chunk=single_chunk),
        grid=(b // nseq, l // q),
        in_specs=[seq(SSD_WIDTH), seq(SSD_CONV_DIM), seq(SSD_HEADS),
                  dtt_spec,
                  per_b3((SSD_CONV - 1, SSD_CONV_DIM)),
                  per_b3((SSD_HEADS, SSD_HEAD_DIM, SSD_STATE))]
                 + [_const_spec(t.shape) for t in consts],
        out_specs=[seq(SSD_WIDTH), per_b3((SSD_HEADS, SSD_HEAD_DIM, SSD_STATE)),
                   per_b3((SSD_CONV - 1, SSD_CONV_DIM))],
        out_shape=[jax.ShapeDtypeStruct((b, l, SSD_WIDTH), F32),
                   jax.ShapeDtypeStruct((b, SSD_HEADS, SSD_HEAD_DIM, SSD_STATE), F32),
                   jax.ShapeDtypeStruct((b, SSD_CONV - 1, SSD_CONV_DIM), F32)],
        scratch_shapes=[pltpu.VMEM((nseq, CONV_PAD + q, SSD_CONV_DIM), F32),
                        pltpu.VMEM((nseq * SSD_GROUPS, SSD_STATE, SSD_GROUP_WIDTH), F32)],
        compiler_params=pltpu.CompilerParams(
            dimension_semantics=("parallel", "arbitrary"), vmem_limit_bytes=V7X_VMEM_LIMIT),
        name="ssd_mixer",
    )(z, xbc, dt, dtt, conv0, ssm0, *consts)


def _head_sums(t):
    pair = 2 * RWKV_HEAD_DIM
    first = lax.broadcasted_iota(jnp.int32, (1, pair), 1) < RWKV_HEAD_DIM
    pieces = []
    for j in range(RWKV_HEADS // 2):
        x = t[:, j * pair:(j + 1) * pair]
        x0 = jnp.where(first, x, 0.0)
        s0 = jnp.sum(x0, axis=-1, keepdims=True)
        s1 = jnp.sum(x - x0, axis=-1, keepdims=True)
        pieces.append(jnp.where(first, s0, s1))
    return jnp.concatenate(pieces, axis=1)


def _rwkv_prep_kernel(rw_ref, sh0_ref, mu_ref, w0_ref, w2_ref, a0_ref, a2_ref, g2_ref,
                      kk_ref, ka_ref, rk_ref,
                      *rest, tt, seqs, chunk):
    outs, full_scr = rest[:-1], rest[-1]
    shn_ref = outs[-1]
    c = pl.program_id(1)
    l = tt // seqs

    if seqs == 1:
        @pl.when(c == 0)
        def _():
            full_scr[0, CONV_PAD - 1:CONV_PAD, :] = sh0_ref[0]

        @pl.when(c > 0)
        def _():
            full_scr[0, CONV_PAD - 1:CONV_PAD, :] = full_scr[0, CONV_PAD + tt - 1:CONV_PAD + tt, :]

        rw = rw_ref[0]
        full_scr[0, CONV_PAD:CONV_PAD + tt, :] = rw
        prev = full_scr[0, CONV_PAD - 1:CONV_PAD - 1 + tt, :]
    else:
        full_scr[:, CONV_PAD - 1:CONV_PAD, :] = sh0_ref[...]
        full_scr[:, CONV_PAD:CONV_PAD + l, :] = rw_ref[...]
        rw = rw_ref[...].reshape(tt, RWKV_PROJ)
        prev = full_scr[:, CONV_PAD - 1:CONV_PAD - 1 + l, :].reshape(tt, RWKV_PROJ)
    vals = _rwkv_mix_math(rw, prev, mu_ref, w0_ref, w2_ref, a0_ref, a2_ref, g2_ref, kk_ref, ka_ref, rk_ref)
    _emit_rwkv_outputs(outs[:-1], vals, tt=tt, chunk=chunk)

    if seqs == 1:
        @pl.when(c == pl.num_programs(1) - 1)
        def _():
            shn_ref[0] = full_scr[0, CONV_PAD + tt - 1:CONV_PAD + tt, :]
    else:
        shn_ref[...] = rw_ref[:, l - 1:l, :]


def _rwkv_mix_math(rw, prev, mu_ref, w0_ref, w2_ref, a0_ref, a2_ref, g2_ref, kk_ref, ka_ref, rk_ref):
    wd = RWKV_WIDTH
    u = rw + (prev - rw) * mu_ref[...]
    r = u[:, :wd]
    k = u[:, wd:2 * wd]
    v = u[:, 2 * wd:3 * wd]
    w_lo = u[:, 3 * wd:3 * wd + DECAY_LORA]
    a_lo = u[:, 3 * wd + DECAY_LORA:3 * wd + DECAY_LORA + AAA_LORA]
    g_lo = u[:, 3 * wd + DECAY_LORA + AAA_LORA:]

    w_log = -_softplus(-(w0_ref[...] + _bdot(jnp.tanh(w_lo), w2_ref[...]))) - 0.5
    lw = -jnp.exp(w_log)
    a = _sigmoid(a0_ref[...] + _bdot(a_lo, a2_ref[...]))
    gate = _bdot(_sigmoid(g_lo), g2_ref[...])

    kk = k * kk_ref[...]
    kk = kk / jnp.maximum(jnp.sqrt(_head_sums(kk * kk)), 1e-12)
    kf = k * (1.0 + (a - 1.0) * ka_ref[...])
    kb = kk * a
    bonus = _head_sums(r * kf * rk_ref[...]) * v
    return r, lw, kf, v, kk, kb, gate, bonus


def _emit_rwkv_outputs(outs, vals, *, tt, chunk, row0=0):
    r, lw, kf, v, kk, kb, gate, bonus = vals
    wd = RWKV_WIDTH
    gate_out, bonus_out = outs[-2:]
    if chunk == 0:
        blk = gate_out.shape
        for ref, val in zip(outs, (r, jnp.exp(lw), kf, v, kk, kb, gate, bonus)):
            ref[...] = val.reshape(blk)
    else:
        gate_out[0, row0:row0 + tt, :] = gate
        bonus_out[0, row0:row0 + tt, :] = bonus
        kkt_out, rt_out, kfh_out, nbh_out, vb_out, kbg_out, kfg_out, gend_out = outs[:8]
        tri = (lax.broadcasted_iota(jnp.int32, (chunk, chunk), 0)
               >= lax.broadcasted_iota(jnp.int32, (chunk, chunk), 1)).astype(F32)
        for ci in range(tt // chunk):
            rs = slice(ci * chunk, (ci + 1) * chunk)
            ro = slice(row0 + ci * chunk, row0 + (ci + 1) * chunk)
            lw_c = lw[rs]
            cum = _dot01(tri, lw_c, exact_side="rhs")
            cum_end = cum[chunk - 1:chunk, :]
            g_inv = jnp.exp(-cum)
            g_tail = jnp.exp(cum_end - cum)
            kkt_out[0, ro, :] = (kk[rs] * jnp.exp(cum - lw_c)).astype(BF16)
            rt_out[0, ro, :] = (r[rs] * jnp.exp(cum)).astype(BF16)
            kfh_out[0, ro, :] = (kf[rs] * g_tail).astype(BF16)
            nbh_out[0, ro, :] = (-kb[rs] * g_tail).astype(BF16)
            vb_out[0, ro, :] = v[rs].astype(BF16)
            kbg_out[0, ro, :] = (kb[rs] * g_inv).astype(BF16)
            kfg_out[0, ro, :] = (kf[rs] * g_inv).astype(BF16)
            gend_out[0, row0 // chunk + ci] = jnp.broadcast_to(jnp.exp(cum_end), (8, wd))


def rwkv_prep(rw, shift0, w, *, tt, chunk):
    b, l, _ = rw.shape
    seqs = max(1, tt // l)
    assert chunk == 0 or (seqs == 1 and tt % chunk == 0)
    rowv = lambda t: t.reshape(1, -1)
    consts = [rowv(w["shift_mu"]), rowv(w["w0"]), w["w2"].astype(BF16), rowv(w["a0"]),
              w["a2"].astype(BF16), w["g2"].astype(BF16), rowv(w["k_k"]), rowv(w["k_a"]),
              rowv(w["r_k"])]
    rows = tt // seqs
    grid = (b // seqs, l // rows)
    seq = lambda wd: pl.BlockSpec((seqs, rows, wd), lambda i, c: (i, c, 0))
    one = pl.BlockSpec((seqs, 1, RWKV_PROJ), lambda i, c: (i, 0, 0))
    sds = jax.ShapeDtypeStruct
    f32_seq = sds((b, l, RWKV_WIDTH), F32)
    if chunk == 0:
        op_specs = [seq(RWKV_WIDTH)] * 6
        op_shapes = [f32_seq] * 6
    else:
        per_tile = tt // chunk
        op_specs = [seq(RWKV_WIDTH)] * 7 + [
            pl.BlockSpec((1, per_tile, 8, RWKV_WIDTH), lambda i, c: (i, c, 0, 0))]
        op_shapes = [sds((b, l, RWKV_WIDTH), BF16)] * 7 + [sds((b, l // chunk, 8, RWKV_WIDTH), F32)]
    outs = pl.pallas_call(
        functools.partial(_rwkv_prep_kernel, tt=tt, seqs=seqs, chunk=chunk),
        grid=grid,
        in_specs=[seq(RWKV_PROJ), one] + [_const_spec(t.shape) for t in consts],
        out_specs=op_specs + [seq(RWKV_WIDTH)] * 2 + [one],
        out_shape=op_shapes + [f32_seq] * 2 + [sds((b, 1, RWKV_PROJ), F32)],
        scratch_shapes=[pltpu.VMEM((seqs, CONV_PAD + rows, RWKV_PROJ), F32)],
        compiler_params=pltpu.CompilerParams(
            dimension_semantics=("parallel", "arbitrary"), vmem_limit_bytes=V7X_VMEM_LIMIT),
        name="rwkv_prep",
    )(rw, shift0, *consts)
    return outs


def _proj_prep_kernel(x_ref, sh0_ref, g_ref, wz_ref, wx_ref, wr_ref, wdt_ref, wdtt_ref,
                      mu_ref, w0_ref, w2_ref, a0_ref, a2_ref, g2_ref, kk_ref, ka_ref, rk_ref,
                      z_ref, xbc_ref, dt_ref, dtt_ref, *rest, tm, tiles_per_seq, chunk):
    outs, (new_scr, cur_scr) = rest[:-2], rest[-2:]
    shn_ref = outs[-1]
    i = pl.program_id(0)

    @pl.when(i == 0)
    def _():
        new_scr[...] = jnp.zeros(new_scr.shape, F32)
        cur_scr[...] = jnp.zeros(cur_scr.shape, F32)

    k = jnp.maximum(i - 1, 0)
    first = (k % tiles_per_seq) == 0
    cur_scr[CONV_PAD - 1:CONV_PAD, :] = jnp.where(first, sh0_ref[0], cur_scr[CONV_PAD + tm - 1:CONV_PAD + tm, :])
    cur_scr[CONV_PAD:CONV_PAD + tm, :] = new_scr[...]

    shn_ref[0] = cur_scr[CONV_PAD + tm - 1:CONV_PAD + tm, :]
    u = _rms(x_ref[...], g_ref[...]).astype(BF16)

    def project(piece):
        if piece == 0:
            z_ref[...] = jnp.dot(u, wz_ref[...], preferred_element_type=F32)
        elif piece == 1:
            xbc_ref[...] = jnp.dot(u, wx_ref[...], preferred_element_type=F32)
        elif piece == 2:
            cols = slice(0, 2 * RWKV_WIDTH)
            new_scr[:, cols] = jnp.dot(u, wr_ref[:, cols], preferred_element_type=F32)
        else:
            cols = slice(2 * RWKV_WIDTH, RWKV_PROJ)
            new_scr[:, cols] = jnp.dot(u, wr_ref[:, cols], preferred_element_type=F32)
            dt_ref[...] = jnp.dot(u, wdt_ref[...], preferred_element_type=F32)
            dtt_ref[...] = lax.dot_general(wdtt_ref[...], u, (((1,), (1,)), ((), ())),
                                           preferred_element_type=F32)

    def prepare(part, rows):
        lo = CONV_PAD + part * rows
        rw = cur_scr[lo:lo + rows, :]
        prev = cur_scr[lo - 1:lo - 1 + rows, :]
        vals = _rwkv_mix_math(rw, prev, mu_ref, w0_ref, w2_ref, a0_ref, a2_ref, g2_ref, kk_ref, ka_ref, rk_ref)
        _emit_rwkv_outputs(outs[:-1], vals, tt=rows, chunk=chunk, row0=part * rows)

    pieces = 4
    parts = min(pieces, tm // chunk)
    for part in range(parts):
        for piece in range(part * pieces // parts, (part + 1) * pieces // parts):
            project(piece)
        prepare(part, tm // parts)


def proj_prep(x, shift0, g, wz, wx, wr, wdt, w, *, tm, chunk):
    b, l, _ = x.shape
    n = b * l
    nt, tps = n // tm, l // tm
    assert l % tm == 0 and tm % chunk == 0
    x2 = x.reshape(n, D_MODEL)
    rowv = lambda t: t.reshape(1, -1)
    wdtt = wdt.T
    consts = [g, wz, wx, wr, wdt, wdtt,
              rowv(w["shift_mu"]), rowv(w["w0"]), w["w2"].astype(BF16), rowv(w["a0"]),
              w["a2"].astype(BF16), w["g2"].astype(BF16), rowv(w["k_k"]), rowv(w["k_a"]), rowv(w["r_k"])]
    ahead = lambda i: jnp.minimum(i, nt - 1)
    behind = lambda i: jnp.maximum(i - 1, 0)
    row_a = lambda w_: pl.BlockSpec((tm, w_), lambda i: (ahead(i), 0))
    seq_b = lambda w_: pl.BlockSpec((1, tm, w_), lambda i: (behind(i) // tps, behind(i) % tps, 0))
    one_b = pl.BlockSpec((1, 1, RWKV_PROJ), lambda i: (behind(i) // tps, 0, 0))
    per_tile = tm // chunk
    gend_spec = pl.BlockSpec((1, per_tile, 8, RWKV_WIDTH), lambda i: (behind(i) // tps, behind(i) % tps, 0, 0))
    sds = jax.ShapeDtypeStruct
    outs = pl.pallas_call(
        functools.partial(_proj_prep_kernel, tm=tm, tiles_per_seq=tps, chunk=chunk),
        grid=(nt + 1,),
        in_specs=[row_a(D_MODEL), one_b] + [_const_spec(t.shape) for t in consts],
        out_specs=[row_a(SSD_WIDTH), row_a(SSD_CONV_DIM), row_a(SSD_HEADS),
                   pl.BlockSpec((SSD_HEADS, tm), lambda i: (0, ahead(i)))]
                  + [seq_b(RWKV_WIDTH)] * 7 + [gend_spec] + [seq_b(RWKV_WIDTH)] * 2 + [one_b],
        out_shape=[sds((n, SSD_WIDTH), F32), sds((n, SSD_CONV_DIM), F32), sds((n, SSD_HEADS), F32),
                   sds((SSD_HEADS, n), F32)]
                  + [sds((b, l, RWKV_WIDTH), BF16)] * 7 + [sds((b, l // chunk, 8, RWKV_WIDTH), F32)]
                  + [sds((b, l, RWKV_WIDTH), F32)] * 2 + [sds((b, 1, RWKV_PROJ), F32)],
        scratch_shapes=[pltpu.VMEM((tm, RWKV_PROJ), F32), pltpu.VMEM((CONV_PAD + tm, RWKV_PROJ), F32)],
        compiler_params=pltpu.CompilerParams(
            dimension_semantics=("arbitrary",), vmem_limit_bytes=V7X_VMEM_LIMIT),
        name="proj_prep",
    )(x2, shift0, *consts)
    return outs


def _wkv_kernel(r_ref, w_ref, k_ref, v_ref, kk_ref, kka_ref, s0_ref,
                o_ref, sfin_ref, s_scr, vt_scr, ot_scr, *, steps):
    c = pl.program_id(1)
    n = RWKV_HEAD_DIM
    lanes = WKV_BATCH_BLOCK * RWKV_HEADS

    @pl.when(c == 0)
    def _():
        s_scr[...] = s0_ref[...].reshape(lanes, n * n).T.reshape(n, n, lanes)

    def to_pairs(ref, t):
        return ref[:, t].reshape(lanes, n).T

    def step(t, carry):
        r_t = to_pairs(r_ref, t)
        w_t = to_pairs(w_ref, t)
        k_t = to_pairs(k_ref, t)
        kk_t = to_pairs(kk_ref, t)
        kka_t = to_pairs(kka_ref, t)
        vt_scr[...] = to_pairs(v_ref, t)

        def per_value(vi, carry2):
            s_v = s_scr[vi]
            skk = jnp.sum(s_v * kk_t, axis=0, keepdims=True)
            v_row = vt_scr[pl.ds(vi, 1), :]
            s_new = s_v * w_t - skk * kka_t + v_row * k_t
            s_scr[vi] = s_new
            ot_scr[pl.ds(vi, 1), :] = jnp.sum(s_new * r_t, axis=0, keepdims=True)
            return carry2

        lax.fori_loop(0, n, per_value, 0, unroll=4)
        o_ref[:, t] = ot_scr[...].T.reshape(WKV_BATCH_BLOCK, RWKV_HEADS, n)
        return carry

    lax.fori_loop(0, steps, step, 0)

    @pl.when(c == pl.num_programs(1) - 1)
    def _():
        sfin_ref[...] = s_scr[...].reshape(n * n, lanes).T.reshape(
            WKV_BATCH_BLOCK, RWKV_HEADS, n, n)


def wkv_scan(r, w, k, v, kk, kka, s0, *, steps):
    b, l, _ = r.shape
    h, n = RWKV_HEADS, RWKV_HEAD_DIM
    assert b % WKV_BATCH_BLOCK == 0 and l % steps == 0
    ops = [t.reshape(b, l, h, n) for t in (r, w, k, v, kk, kka)]
    seq_spec = pl.BlockSpec((WKV_BATCH_BLOCK, steps, h, n), lambda g, c: (g, c, 0, 0))
    st_spec = pl.BlockSpec((WKV_BATCH_BLOCK, h, n, n), lambda g, c: (g, 0, 0, 0))
    o, s_fin = pl.pallas_call(
        functools.partial(_wkv_kernel, steps=steps),
        grid=(b // WKV_BATCH_BLOCK, l // steps),
        in_specs=[seq_spec] * 6 + [st_spec],
        out_specs=[seq_spec, st_spec],
        out_shape=[jax.ShapeDtypeStruct((b, l, h, n), F32),
                   jax.ShapeDtypeStruct((b, h, n, n), F32)],
        scratch_shapes=[pltpu.VMEM((n, n, WKV_BATCH_BLOCK * h), F32),
                        pltpu.VMEM((n, WKV_BATCH_BLOCK * h), F32),
                        pltpu.VMEM((n, WKV_BATCH_BLOCK * h), F32)],
        compiler_params=pltpu.CompilerParams(
            dimension_semantics=("parallel", "arbitrary"), vmem_limit_bytes=V7X_VMEM_LIMIT),
        name="wkv_scan",
    )(*ops, s0)
    return o.reshape(b, l, h * n), s_fin


WKV_LANE_BATCH = 128


def _wkv_batch_lanes_kernel(r_ref, w_ref, k_ref, v_ref, kk_ref, kka_ref, s0_ref,
                            o_ref, sfin_ref, op_scr, ot_scr, *, steps):
    n = RWKV_HEAD_DIM
    nb = WKV_LANE_BATCH
    sfin_ref[...] = s0_ref[...]

    def step(t, carry):
        rows = pl.ds(t, nb, stride=steps)
        for i, ref in enumerate((r_ref, w_ref, k_ref, kk_ref, kka_ref, v_ref)):
            op_scr[i] = ref[rows, :].T
        for h2 in range(2):
            ch = slice(h2 * n, (h2 + 1) * n)
            r_t = op_scr[0, ch, :]
            kka_r = jnp.sum(op_scr[4, ch, :] * r_t, axis=0, keepdims=True)
            k_r = jnp.sum(op_scr[2, ch, :] * r_t, axis=0, keepdims=True)
            op_scr[0, ch, :] = op_scr[1, ch, :] * r_t

            def per_value(vi, carry2, ch=ch, h2=h2, kka_r=kka_r, k_r=k_r):
                s_v = sfin_ref[h2, vi]
                skk = jnp.sum(s_v * op_scr[3, ch, :], axis=0, keepdims=True)
                out = jnp.sum(s_v * op_scr[0, ch, :], axis=0, keepdims=True)
                v_row = op_scr[5, pl.ds(h2 * n + vi, 1), :]
                sfin_ref[h2, vi] = s_v * op_scr[1, ch, :] - skk * op_scr[4, ch, :] + v_row * op_scr[2, ch, :]
                ot_scr[pl.ds(h2 * n + vi, 1), :] = out - skk * kka_r + v_row * k_r
                return carry2

            lax.fori_loop(0, n, per_value, 0, unroll=8)
        o_ref[rows, :] = ot_scr[...].T
        return carry

    lax.fori_loop(0, steps, step, 0)


def wkv_scan_batch_lanes(r, w, k, v, kk, kka, s0):
    b, l, wd = r.shape
    h, n, nb = RWKV_HEADS, RWKV_HEAD_DIM, WKV_LANE_BATCH
    assert b % nb == 0
    ops = [t.reshape(b * l, wd) for t in (r, w, k, v, kk, kka)]
    s0t = jnp.transpose(s0, (1, 2, 3, 0))
    seq_spec = pl.BlockSpec((nb * l, 2 * n), lambda g, j: (g, j))
    st_spec = pl.BlockSpec((2, n, n, nb), lambda g, j: (j, 0, 0, g))
    o, s_fin = pl.pallas_call(
        functools.partial(_wkv_batch_lanes_kernel, steps=l),
        grid=(b // nb, h // 2),
        in_specs=[seq_spec] * 6 + [st_spec],
        out_specs=[seq_spec, st_spec],
        out_shape=[jax.ShapeDtypeStruct((b * l, wd), F32), jax.ShapeDtypeStruct((h, n, n, b), F32)],
        scratch_shapes=[pltpu.VMEM((6, 2 * n, nb), F32), pltpu.VMEM((2 * n, nb), F32)],
        compiler_params=pltpu.CompilerParams(
            dimension_semantics=("parallel", "parallel"), vmem_limit_bytes=V7X_VMEM_LIMIT),
        name="wkv_scan_batch_lanes",
    )(*ops, s0t)
    return o.reshape(b, l, wd), jnp.transpose(s_fin, (3, 0, 1, 2))


WKV_CHUNK = 64
WKV_PAIRS = RWKV_HEADS // 2
WKV_ROW_STRIDE = WKV_CHUNK + 8
WKV_SOLVE_ROWS = 8
WKV_SOLVE_COLS = 16


def _pair_masks():
    c = WKV_CHUNK
    row = lax.broadcasted_iota(jnp.int32, (2 * c, 2 * c), 0)
    col = lax.broadcasted_iota(jnp.int32, (2 * c, 2 * c), 1)
    t, i = row % c, col % c
    keep = i <= t - jnp.where(row < c, 1, 0)
    sign = jnp.where(row >= c, jnp.where(col < c, -1.0, 1.0), 1.0)
    block_diag = row // c == col // c
    return keep, sign, block_diag


def _wkv_prepare_kernel(kkt_ref, rt_ref, kbg_ref, kfg_ref, vb_ref,
                        lo_ref, rhs0_ref, tp_ref, abs_scr, top_scr, abt_scr, tt_scr):
    c = WKV_CHUNK
    keep, sign, _ = _pair_masks()
    lane = lax.broadcasted_iota(jnp.int32, (1, 2 * c), 1)
    head0 = lane < RWKV_HEAD_DIM
    row_head0 = lax.broadcasted_iota(jnp.int32, (2 * RWKV_HEAD_DIM, 1), 0) < RWKV_HEAD_DIM
    zeros = jnp.zeros((c, 2 * c), BF16)

    def per_batch(b, carry):
        for j in range(WKV_PAIRS):
            sl = slice(j * 2 * RWKV_HEAD_DIM, (j + 1) * 2 * RWKV_HEAD_DIM)
            lhs = jnp.concatenate([kkt_ref[b, :, sl], rt_ref[b, :, sl]], axis=0)
            rhs = jnp.concatenate([kbg_ref[b, :, sl], kfg_ref[b, :, sl]], axis=0)
            rhs_t = rhs.astype(F32).T
            top = jnp.where(row_head0, rhs_t, 0.0)
            w_a = jnp.concatenate([top, rhs_t - top], axis=1).astype(BF16)
            a_both = jnp.dot(lhs, w_a, preferred_element_type=F32)
            for h2 in range(2):
                a = a_both[:, h2 * 2 * c:(h2 + 1) * 2 * c]
                a = jnp.where(keep, a, 0.0) * sign
                inst = h2 * (WKV_BATCH_BLOCK * WKV_PAIRS) + b * WKV_PAIRS + j
                abs_scr[pl.ds(inst * WKV_ROW_STRIDE, c), :c] = a[:c, :c]
                top_scr[j, :, h2 * 2 * c:(h2 + 1) * 2 * c] = a[:c].astype(BF16)
                lo_ref[b, :, (2 * j + h2) * 2 * c:(2 * j + h2 + 1) * 2 * c] = a[c:].astype(BF16)

        for j in range(WKV_PAIRS):
            sl = slice(j * 2 * RWKV_HEAD_DIM, (j + 1) * 2 * RWKV_HEAD_DIM)
            v = vb_ref[b, :, sl]
            v0 = jnp.where(head0, v, jnp.zeros_like(v))
            w_akf = jnp.concatenate([zeros, v0, zeros, v - v0], axis=0)
            rhs0_ref[b, :, sl] = jnp.dot(top_scr[j], w_akf, preferred_element_type=F32)
        return carry

    lax.fori_loop(0, WKV_BATCH_BLOCK, per_batch, 0)

    n_inst = 2 * WKV_BATCH_BLOCK * WKV_PAIRS
    n_pair_rows = WKV_BATCH_BLOCK * WKV_PAIRS

    def to_lanes(t, carry):
        abt_scr[t] = abs_scr[pl.ds(t, n_inst, stride=WKV_ROW_STRIDE), :][:, :c].T
        return carry

    lax.fori_loop(0, c, to_lanes, 0, unroll=8)

    tt_scr[...] = jnp.zeros(tt_scr.shape, F32)
    sub_iota = lax.broadcasted_iota(jnp.int32, (WKV_SOLVE_COLS, n_inst), 0)
    rows = range(WKV_SOLVE_ROWS)
    for cb in range(c // WKV_SOLVE_COLS):
        col0 = WKV_SOLVE_COLS * cb
        cols = slice(col0, col0 + WKV_SOLVE_COLS)
        first_block = col0 // WKV_SOLVE_ROWS

        def solve_rows(tb, carry, col0=col0, cols=cols, first_block=first_block):
            t0 = tb * WKV_SOLVE_ROWS

            def sub(ib, accs):
                ps = [tt_scr[ib * WKV_SOLVE_ROWS + di, cols, :] for di in rows]
                out = []
                for r in rows:
                    terms = [abt_scr[t0 + r, pl.ds(ib * WKV_SOLVE_ROWS + di, 1), :] * ps[di] for di in rows]
                    while len(terms) > 1:
                        terms = [a + b for a, b in zip(terms[::2], terms[1::2])]
                    out.append(accs[r] - terms[0])
                return tuple(out)

            unit = tuple(jnp.where(sub_iota + col0 == t0 + r, 1.0, 0.0) for r in rows)
            accs = list(lax.fori_loop(first_block, tb, sub, unit))
            for r in rows:
                for r2 in range(r):
                    accs[r] = accs[r] - abt_scr[t0 + r, pl.ds(t0 + r2, 1), :] * accs[r2]
                tt_scr[t0 + r, cols, :] = accs[r]
            return carry

        lax.fori_loop(first_block, c // WKV_SOLVE_ROWS, solve_rows, 0)

    def from_lanes(t, carry):
        m = tt_scr[t].T
        abs_scr[pl.ds(t, n_pair_rows, stride=WKV_ROW_STRIDE), :] = jnp.concatenate(
            [m[:n_pair_rows], m[n_pair_rows:]], axis=1)
        return carry

    lax.fori_loop(0, c, from_lanes, 0, unroll=8)

    def emit(b, carry):
        for j in range(WKV_PAIRS):
            row0 = (b * WKV_PAIRS + j) * WKV_ROW_STRIDE
            tp_ref[b, :, j * 2 * c:(j + 1) * 2 * c] = abs_scr[pl.ds(row0, c), :].astype(BF16)
        return carry

    lax.fori_loop(0, WKV_BATCH_BLOCK, emit, 0)


def _wkv_apply_kernel(kkt_ref, rt_ref, kfh_ref, nbh_ref, vb_ref, tp_ref, lo_ref, rhs0_ref, gend_ref, s0_ref,
                      o_ref, sfin_ref, x_scr, wp_scr, p_scr):
    c = WKV_CHUNK
    n = RWKV_HEAD_DIM
    ch = pl.program_id(1)
    _, _, block_diag = _pair_masks()
    lane = lax.broadcasted_iota(jnp.int32, (1, 2 * c), 1)
    head0 = lane < n
    eye2 = (lax.broadcasted_iota(jnp.int32, (n, 2 * n), 0)
            == lax.broadcasted_iota(jnp.int32, (n, 2 * n), 1) % n).astype(F32)

    @pl.when(ch == 0)
    def _():
        def init(b, carry):
            for j in range(WKV_PAIRS):
                sp = s0_ref[b, 2 * j:2 * j + 2].reshape(2 * n, n)
                dup = jnp.dot(sp, eye2, precision=HIGHEST, preferred_element_type=F32)
                x_scr[b, j] = jnp.where(block_diag, dup, 0.0)
            return carry
        lax.fori_loop(0, WKV_BATCH_BLOCK, init, 0)

    def per_batch(b, carry):
        for j in range(WKV_PAIRS):
            sl = slice(j * 2 * n, (j + 1) * 2 * n)
            lhs = jnp.concatenate([kkt_ref[b, :, sl], rt_ref[b, :, sl]], axis=0)
            kx = lax.dot_general(lhs, x_scr[b, j].astype(BF16), (((1,), (1,)), ((), ())),
                                 preferred_element_type=F32)
            rhs = kx[:c] + rhs0_ref[b, :, sl]
            r0 = jnp.where(head0, rhs, 0.0)
            wp_scr[j] = jnp.concatenate([r0, rhs - r0], axis=0).astype(BF16)
            o_ref[b, :, sl] = kx[c:]
        for j in range(WKV_PAIRS):
            sl = slice(j * 2 * n, (j + 1) * 2 * n)
            p = jnp.dot(tp_ref[b, :, sl], wp_scr[j], preferred_element_type=F32)
            p_scr[j] = p.astype(BF16)
        for j in range(WKV_PAIRS):
            sl = slice(j * 2 * n, (j + 1) * 2 * n)
            v = vb_ref[b, :, sl]
            pb = p_scr[j]
            zero = jnp.zeros_like(pb)
            p0, v0 = jnp.where(head0, pb, zero), jnp.where(head0, v, zero)
            w_o = jnp.concatenate([p0, v0, pb - p0, v - v0], axis=0)
            o_ref[b, :, sl] = o_ref[b, :, sl] + jnp.dot(
                lo_ref[b, :, j * 4 * c:(j + 1) * 4 * c], w_o, preferred_element_type=F32)
            vp = jnp.concatenate([v, pb], axis=0)
            kb = jnp.concatenate([kfh_ref[b, :, sl], nbh_ref[b, :, sl]], axis=0)
            upd = lax.dot_general(vp, kb, (((0,), (0,)), ((), ())), preferred_element_type=F32)
            x_scr[b, j] = jnp.where(block_diag, x_scr[b, j] * gend_ref[b, 0, 0:1, sl] + upd, 0.0)
        return carry

    lax.fori_loop(0, WKV_BATCH_BLOCK, per_batch, 0)

    @pl.when(ch == pl.num_programs(1) - 1)
    def _():
        def fin(b, carry):
            for j in range(WKV_PAIRS):
                sp = lax.dot_general(x_scr[b, j], eye2, (((1,), (1,)), ((), ())),
                                     precision=HIGHEST, preferred_element_type=F32)
                sfin_ref[b, 2 * j:2 * j + 2] = sp.reshape(2, n, n)
            return carry
        lax.fori_loop(0, WKV_BATCH_BLOCK, fin, 0)


def wkv_chunked(kkt, rt, kfh, nbh, vb, kbg, kfg, gend, s0):
    b, l, wd = kkt.shape
    c = WKV_CHUNK
    assert b % WKV_BATCH_BLOCK == 0 and l % c == 0
    gb, nc = b // WKV_BATCH_BLOCK, l // c
    lanes = 2 * WKV_BATCH_BLOCK * WKV_PAIRS
    seq = lambda w_: pl.BlockSpec((WKV_BATCH_BLOCK, c, w_), lambda g, i: (g, i, 0))
    gend_spec = pl.BlockSpec((WKV_BATCH_BLOCK, 1, 8, wd), lambda g, i: (g, i, 0, 0))
    sds = jax.ShapeDtypeStruct
    st_spec = pl.BlockSpec((WKV_BATCH_BLOCK, RWKV_HEADS, RWKV_HEAD_DIM, RWKV_HEAD_DIM),
                           lambda g, i: (g, 0, 0, 0))
    blk = (WKV_BATCH_BLOCK, c, wd)
    o, s_fin = pl.pallas_call(
        _wkv_chunk_kernel,
        grid=(gb, nc),
        in_specs=[seq(wd)] * 7 + [gend_spec, st_spec],
        out_specs=[seq(wd), st_spec],
        out_shape=[sds((b, l, wd), F32), sds(s0.shape, F32)],
        scratch_shapes=[
            pltpu.VMEM((lanes * WKV_ROW_STRIDE, 2 * c), F32),
            pltpu.VMEM((WKV_PAIRS, c, 4 * c), BF16),
            pltpu.VMEM((c, c, lanes), F32),
            pltpu.VMEM((c, c, lanes), F32),
            pltpu.VMEM((WKV_BATCH_BLOCK, c, 2 * wd), BF16),
            pltpu.VMEM(blk, F32),
            pltpu.VMEM(blk, BF16),
            pltpu.VMEM((WKV_BATCH_BLOCK, WKV_PAIRS, 2 * RWKV_HEAD_DIM, 2 * RWKV_HEAD_DIM), F32),
            pltpu.VMEM((WKV_PAIRS, 2 * c, 2 * RWKV_HEAD_DIM), BF16),
            pltpu.VMEM((WKV_PAIRS, c, 2 * RWKV_HEAD_DIM), BF16)],
        compiler_params=pltpu.CompilerParams(
            dimension_semantics=("parallel", "arbitrary"), vmem_limit_bytes=V7X_VMEM_LIMIT),
        name="wkv_chunked",
    )(kkt, rt, kfh, nbh, vb, kbg, kfg, gend, s0)
    return o, s_fin


def _wkv_chunk_kernel(kkt_ref, rt_ref, kfh_ref, nbh_ref, vb_ref, kbg_ref, kfg_ref, gend_ref, s0_ref,
                      o_ref, sfin_ref, abs_scr, top_scr, abt_scr, tt_scr, lo_scr, rhs0_scr, tp_scr,
                      x_scr, wp_scr, p_scr):
    _wkv_prepare_kernel(kkt_ref, rt_ref, kbg_ref, kfg_ref, vb_ref, lo_scr, rhs0_scr, tp_scr,
                        abs_scr, top_scr, abt_scr, tt_scr)
    _wkv_apply_kernel(kkt_ref, rt_ref, kfh_ref, nbh_ref, vb_ref, tp_scr, lo_scr, rhs0_scr, gend_ref, s0_ref,
                      o_ref, sfin_ref, x_scr, wp_scr, p_scr)


def _tail_kernel(o_ref, gate_ref, bonus_ref, x_ref, ys_ref, p_ref, lnw_ref, lnb_ref, woa_ref, wob_ref,
                 nf_ref, wg_ref, wu_ref, wd_ref, np_ref, wpg_ref, wpp_ref, nl_ref, y_ref, new_scr, cur_scr):
    i = pl.program_id(0)

    @pl.when(i == 0)
    def _():
        new_scr[...] = jnp.zeros(new_scr.shape, BF16)

    cur_scr[...] = new_scr[...]

    parts = 4
    rows = o_ref.shape[0] // parts
    inv_n = 1.0 / RWKV_HEAD_DIM

    def vector_half(part):
        rs = slice(part * rows, (part + 1) * rows)
        o = o_ref[rs, :]
        mu = _head_sums(o) * inv_n
        d = o - mu
        var = _head_sums(d * d) * inv_n
        on = d * lax.rsqrt(var + GN_EPS) * lnw_ref[...] + lnb_ref[...]
        new_scr[rs, :] = ((on + bonus_ref[rs, :]) * gate_ref[rs, :]).astype(BF16)

    vector_half(0)
    y_rwkv = cur_scr[...]
    h = x_ref[...] + _bdot(ys_ref[...], woa_ref[...]) + jnp.dot(y_rwkv, wob_ref[...],
                                                                preferred_element_type=F32)
    hf = _rms(h, nf_ref[...]).astype(BF16)
    vector_half(1)
    gate = jnp.dot(hf, wg_ref[...], preferred_element_type=F32)
    up = jnp.dot(hf, wu_ref[...], preferred_element_type=F32)
    vector_half(2)
    h = h + _bdot(_silu(gate) * up, wd_ref[...])
    vector_half(3)
    pg = _sigmoid(_bdot(_rms(h, np_ref[...]), wpg_ref[...]))
    h = h + pg * _bdot(p_ref[...], wpp_ref[...])
    y_ref[...] = _rms(h, nl_ref[...])


def layer_tail(o, gate, bonus, x, y_ssd, p, consts, *, tm):
    n = x.shape[0]
    nt = n // tm
    ahead = lambda w_: pl.BlockSpec((tm, w_), lambda i: (jnp.minimum(i, nt - 1), 0))
    behind = lambda w_: pl.BlockSpec((tm, w_), lambda i: (jnp.maximum(i - 1, 0), 0))
    return pl.pallas_call(
        _tail_kernel,
        grid=(nt + 1,),
        in_specs=[ahead(D_MODEL)] * 3 + [behind(D_MODEL)] * 2 + [behind(PLE_DIM)]
                 + [_const_spec(t.shape) for t in consts],
        out_specs=behind(D_MODEL),
        out_shape=jax.ShapeDtypeStruct((n, D_MODEL), F32),
        scratch_shapes=[pltpu.VMEM((tm, D_MODEL), BF16), pltpu.VMEM((tm, D_MODEL), BF16)],
        compiler_params=pltpu.CompilerParams(
            dimension_semantics=("arbitrary",), vmem_limit_bytes=V7X_VMEM_LIMIT),
        name="layer_tail",
    )(o, gate, bonus, x, y_ssd, p, *consts)


def _prepare_weights(w):
    c0, c1, c2 = SSD_WIDTH, SSD_WIDTH + SSD_CONV_DIM, SSD_WIDTH + SSD_CONV_DIM + SSD_HEADS
    w_in = w["w_in"]
    rowv = lambda t: t.reshape(1, -1)
    return dict(
        w,
        wz=w_in[:, :c0].astype(BF16), wx=w_in[:, c0:c1].astype(BF16),
        wdt=w_in[:, c1:c2].astype(BF16), wr=w_in[:, c2:].astype(BF16),
        woa=w["w_out"][:SSD_WIDTH].astype(BF16), wob=w["w_out"][SSD_WIDTH:].astype(BF16),
        wg=w["w_gate"].astype(BF16), wu=w["w_up"].astype(BF16), wd=w["w_down"].astype(BF16),
        wpg=w["w_ple_gate"].astype(BF16), wpp=w["w_ple_proj"].astype(BF16),
        norm_mix_r=rowv(w["norm_mix"]), norm_ffn_r=rowv(w["norm_ffn"]),
        norm_ple_r=rowv(w["norm_ple"]), norm_final_r=rowv(w["norm_final"]),
        ln_x_w_r=rowv(w["ln_x_w"]), ln_x_b_r=rowv(w["ln_x_b"]),
    )


def layer_forward(x, p, conv0, shift0, ssm0, wkv0, w, *, tm, ssd_q, prep_tt, wkv_steps):
    b, l, _ = x.shape
    n = b * l
    x2 = x.reshape(n, D_MODEL)
    chunked = l % WKV_CHUNK == 0 and l % tm == 0
    if chunked:
        z, xbc, dt, dtt, *ops, gate, bonus, shift_new = proj_prep(
            x, shift0, w["norm_mix_r"], w["wz"], w["wx"], w["wr"], w["wdt"], w, tm=tm, chunk=WKV_CHUNK)
    else:
        z, xbc, rw, dt, dtt = in_projection(x2, w["norm_mix_r"], w["wz"], w["wx"], w["wr"], w["wdt"], tm=tm)
        *ops, gate, bonus, shift_new = rwkv_prep(rw.reshape(b, l, -1), shift0, w, tt=prep_tt, chunk=0)
    y_ssd, ssm_new, conv_new = ssd_mixer(
        z.reshape(b, l, -1), xbc.reshape(b, l, -1), dt.reshape(b, l, -1), dtt, conv0, ssm0, w, q=ssd_q)
    if chunked:
        o, wkv_new = wkv_chunked(*ops, wkv0)
    elif b % WKV_LANE_BATCH == 0:
        o, wkv_new = wkv_scan_batch_lanes(*ops, wkv0)
    else:
        o, wkv_new = wkv_scan(*ops, wkv0, steps=wkv_steps)
    flat = lambda t: t.reshape(n, -1)
    tail_consts = [w["ln_x_w_r"], w["ln_x_b_r"], w["woa"], w["wob"], w["norm_ffn_r"], w["wg"], w["wu"],
                   w["wd"], w["norm_ple_r"], w["wpg"], w["wpp"], w["norm_final_r"]]
    y = layer_tail(flat(o), flat(gate), flat(bonus), x2, flat(y_ssd), p.reshape(n, PLE_DIM),
                   tail_consts, tm=tm)
    return y.reshape(b, l, D_MODEL), ssm_new, conv_new, wkv_new, shift_new


def kernel(x_prompt, x_sample, state_ssm, state_conv, state_wkv, state_shift, p_prompt, p_sample, norm_mix, w_in, conv_w, conv_b, dt_bias, a_log, d_skip, ssd_norm, shift_mu, w0, w2, a0, a2, g2, k_k, k_a, r_k, ln_x_w, ln_x_b, w_out, norm_ffn, w_gate, w_up, w_down, norm_ple, w_ple_gate, w_ple_proj, norm_final):
    w = _prepare_weights(dict(
        norm_mix=norm_mix[0], w_in=w_in[0], conv_w=conv_w[0], conv_b=conv_b[0], dt_bias=dt_bias[0],
        a_log=a_log[0], d_skip=d_skip[0], ssd_norm=ssd_norm[0], shift_mu=shift_mu[0], w0=w0[0],
        w2=w2[0], a0=a0[0], a2=a2[0], g2=g2[0], k_k=k_k[0], k_a=k_a[0], r_k=r_k[0],
        ln_x_w=ln_x_w[0], ln_x_b=ln_x_b[0], w_out=w_out[0], norm_ffn=norm_ffn[0],
        w_gate=w_gate[0], w_up=w_up[0], w_down=w_down[0], norm_ple=norm_ple[0],
        w_ple_gate=w_ple_gate[0], w_ple_proj=w_ple_proj[0], norm_final=norm_final))
    bp = x_prompt.shape[0]
    zeros = lambda *s: jnp.zeros(s, F32)
    yp, s1, c1, k1, t1 = layer_forward(
        x_prompt, p_prompt[0], zeros(bp, SSD_CONV - 1, SSD_CONV_DIM), zeros(bp, 1, RWKV_PROJ),
        zeros(bp, SSD_HEADS, SSD_HEAD_DIM, SSD_STATE),
        zeros(bp, RWKV_HEADS, RWKV_HEAD_DIM, RWKV_HEAD_DIM), w,
        tm=256, ssd_q=min(SSD_CHUNK, x_prompt.shape[1]), prep_tt=min(128, x_prompt.shape[1]),
        wkv_steps=min(16, x_prompt.shape[1]))
    ys, s2, c2, k2, t2 = layer_forward(
        x_sample, p_sample[0], state_conv[0], state_shift[0], state_ssm[0], state_wkv[0], w,
        tm=256, ssd_q=x_sample.shape[1], prep_tt=128, wkv_steps=x_sample.shape[1])
    return (yp, ys, s1[None], c1[None], k1[None], t1[None], s2[None], c2[None], k2[None], t2[None])
```

```python
import functools

import jax
import jax.numpy as jnp
from jax import lax
from jax.experimental import pallas as pl
from jax.experimental.pallas import tpu as pltpu

F32 = jnp.float32
BF16 = jnp.bfloat16
HIGHEST = lax.Precision.HIGHEST

D_MODEL = 1024
SSD_WIDTH = 1024
SSD_HEADS = 16
SSD_HEAD_DIM = 64
SSD_GROUPS = 2
SSD_GROUP_WIDTH = SSD_WIDTH // SSD_GROUPS
SSD_STATE = 128
SSD_CONV = 4
SSD_CHUNK = 128
SSD_BC = SSD_GROUPS * SSD_STATE
SSD_CONV_DIM = SSD_WIDTH + 2 * SSD_BC
RWKV_WIDTH = 1024
RWKV_HEADS = 16
RWKV_HEAD_DIM = 64
DECAY_LORA = 64
AAA_LORA = 64
GATE_LORA = 128
RWKV_PROJ = 3 * RWKV_WIDTH + DECAY_LORA + AAA_LORA + GATE_LORA
D_FF = 2816
PLE_DIM = 256
NORM_EPS = 1e-6
GN_EPS = 64e-5

WKV_BATCH_BLOCK = 8
V7X_VMEM_LIMIT = 56 * 1024 * 1024
CONV_PAD = 8
SSD_SEQS_PER_STEP = 8
SSD_LONG_SEQS_PER_STEP = 1


def _rms(x, g):
    return x * lax.rsqrt(jnp.mean(x * x, axis=-1, keepdims=True) + NORM_EPS) * g


def _sigmoid(x):
    return 1.0 / (1.0 + jnp.exp(-x))


def _silu(x):
    return x * _sigmoid(x)


def _softplus(x):
    return jnp.maximum(x, 0.0) + jnp.log(1.0 + jnp.exp(-jnp.abs(x)))


def _bdot(a, b):
    return jnp.dot(a.astype(BF16), b.astype(BF16), preferred_element_type=F32)


def _split3(t):
    hi = t.astype(BF16)
    r1 = t - hi.astype(F32)
    mid = r1.astype(BF16)
    lo = (r1 - mid.astype(F32)).astype(BF16)
    return hi, mid, lo


def _dot01(a, b, *, exact_side):
    if exact_side == "lhs":
        m = b.astype(BF16)
        return sum(jnp.dot(p, m, preferred_element_type=F32) for p in _split3(a))
    m = a.astype(BF16)
    return sum(jnp.dot(m, p, preferred_element_type=F32) for p in _split3(b))


def _const_spec(shape):
    return pl.BlockSpec(shape, lambda *_: (0,) * len(shape), pipeline_mode=pl.Buffered(1))


def _head_expand(rows):
    h = lax.broadcasted_iota(jnp.int32, (rows, SSD_WIDTH), 0)
    c = lax.broadcasted_iota(jnp.int32, (rows, SSD_WIDTH), 1)
    return (c // SSD_HEAD_DIM == h).astype(F32)


def _proj_kernel(x_ref, g_ref, wz_ref, wx_ref, wr_ref, wdt_ref, wdtt_ref,
                 z_ref, xbc_ref, rw_ref, dt_ref, dtt_ref):
    u = _rms(x_ref[...], g_ref[...]).astype(BF16)
    z_ref[...] = jnp.dot(u, wz_ref[...], preferred_element_type=F32)
    xbc_ref[...] = jnp.dot(u, wx_ref[...], preferred_element_type=F32)
    rw_ref[...] = jnp.dot(u, wr_ref[...], preferred_element_type=F32)
    dt_ref[...] = jnp.dot(u, wdt_ref[...], preferred_element_type=F32)
    dtt_ref[...] = lax.dot_general(wdtt_ref[...], u, (((1,), (1,)), ((), ())), preferred_element_type=F32)


def in_projection(x, g, wz, wx, wr, wdt, *, tm):
    n = x.shape[0]
    row = lambda w: pl.BlockSpec((tm, w), lambda i: (i, 0))
    wdtt = wdt.T
    return pl.pallas_call(
        _proj_kernel,
        grid=(n // tm,),
        in_specs=[row(D_MODEL), _const_spec((1, D_MODEL)), _const_spec(wz.shape),
                  _const_spec(wx.shape), _const_spec(wr.shape), _const_spec(wdt.shape),
                  _const_spec(wdtt.shape)],
        out_specs=[row(SSD_WIDTH), row(SSD_CONV_DIM), row(RWKV_PROJ), row(SSD_HEADS),
                   pl.BlockSpec((SSD_HEADS, tm), lambda i: (0, i))],
        out_shape=[jax.ShapeDtypeStruct((n, SSD_WIDTH), F32),
                   jax.ShapeDtypeStruct((n, SSD_CONV_DIM), F32),
                   jax.ShapeDtypeStruct((n, RWKV_PROJ), F32),
                   jax.ShapeDtypeStruct((n, SSD_HEADS), F32),
                   jax.ShapeDtypeStruct((SSD_HEADS, n), F32)],
        compiler_params=pltpu.CompilerParams(
            dimension_semantics=("parallel",), vmem_limit_bytes=V7X_VMEM_LIMIT),
        name="in_projection",
    )(x, g, wz, wx, wr, wdt, wdtt)


def _ssd_kernel(z_ref, xbc_ref, dt_ref, dtt_ref, hist_ref, h0_ref, cw_ref, cb_ref,
                dtb_ref, dtbt_ref, alog_ref, alogt_ref, dsk_ref, nrm_ref,
                y_ref, hfin_ref, cnew_ref, xfull_scr, h_scr, *, q, nseq, single_chunk):
    refs = (z_ref, xbc_ref, dt_ref, dtt_ref, hist_ref, h0_ref, cw_ref, cb_ref, dtb_ref, dtbt_ref, alog_ref,
            alogt_ref, dsk_ref, nrm_ref, y_ref, hfin_ref, cnew_ref, xfull_scr, h_scr)
    stages = [_ssd_sequence(s, *refs, q=q, single_chunk=single_chunk) for s in range(nseq)]
    for _ in zip(*stages):
        pass


def _ssd_sequence(s, z_ref, xbc_ref, dt_ref, dtt_ref, hist_ref, h0_ref, cw_ref, cb_ref,
                  dtb_ref, dtbt_ref, alog_ref, alogt_ref, dsk_ref, nrm_ref,
                  y_ref, hfin_ref, cnew_ref, xfull_scr, h_scr, *, q, single_chunk):
    c = pl.program_id(1)
    last = pl.num_programs(1) - 1
    gw = SSD_GROUP_WIDTH

    @pl.when(c == 0)
    def _():
        xfull_scr[s,CONV_PAD - 3:CONV_PAD, :] = hist_ref[s]
        if not single_chunk:
            for g in range(SSD_GROUPS):
                h_scr[s * SSD_GROUPS + g] = h0_ref[s,g * 8:(g + 1) * 8].reshape(gw, SSD_STATE).T

    @pl.when(c > 0)
    def _():
        xfull_scr[s,CONV_PAD - 3:CONV_PAD, :] = xfull_scr[s,CONV_PAD + q - 3:CONV_PAD + q, :]

    xfull_scr[s,CONV_PAD:CONV_PAD + q, :] = xbc_ref[s]

    rows = CONV_PAD + q
    padded = xfull_scr[s]
    conv = cb_ref[...] + padded[CONV_PAD:] * cw_ref[SSD_CONV - 1:SSD_CONV, :]
    for j in range(SSD_CONV - 1):
        lo = CONV_PAD - 3 + j
        conv = conv + pltpu.roll(padded, rows - lo, axis=0)[:q] * cw_ref[j:j + 1, :]
    act = _silu(conv)
    xs = act[:, :SSD_WIDTH]
    yield

    dt = _softplus(dt_ref[s] + dtb_ref[...])
    dtt_raw = dtt_ref[...] if len(dtt_ref.shape) == 2 else dtt_ref[s]
    dtt = _softplus(dtt_raw + dtbt_ref[...])
    da = dt * -jnp.exp(alog_ref[...])
    dat = dtt * -jnp.exp(alogt_ref[...])
    row = lax.broadcasted_iota(jnp.int32, (q, q), 0)
    col = lax.broadcasted_iota(jnp.int32, (q, q), 1)
    causal = row >= col
    a_cum = _dot01(causal.astype(F32), da, exact_side="rhs")
    a_cumt = _dot01(dat, (row <= col).astype(F32), exact_side="lhs")
    yield

    expand = _head_expand(SSD_HEADS)
    decay_in_x = _dot01(jnp.exp(a_cum), expand, exact_side="lhs")
    chunk_decay_x = decay_in_x[q - 1:q, :]
    xd = xs * _dot01(jnp.exp(a_cum[q - 1:q, :] - a_cum) * dt, expand, exact_side="lhs")
    yield

    ys = []
    for g in range(SSD_GROUPS):
        bm = act[:, SSD_WIDTH + g * SSD_STATE:SSD_WIDTH + (g + 1) * SSD_STATE]
        cm = act[:, SSD_WIDTH + SSD_BC + g * SSD_STATE:SSD_WIDTH + SSD_BC + (g + 1) * SSD_STATE]
        cb = lax.dot_general(cm.astype(BF16), bm.astype(BF16), (((1,), (1,)), ((), ())),
                             preferred_element_type=F32)
        yield
        def head_weights(h):
            seg = a_cum[:, h:h + 1] - a_cumt[h:h + 1, :]
            lmat = jnp.where(causal, jnp.exp(jnp.where(causal, seg, 0.0)), 0.0)
            return (cb * lmat * dtt[h:h + 1, :]).astype(BF16)

        y_parts = []
        for e in range(8):
            h = g * 8 + e
            y_parts.append(jnp.dot(head_weights(h), xs[:, h * SSD_HEAD_DIM:(h + 1) * SSD_HEAD_DIM].astype(BF16),
                                   preferred_element_type=F32))
            yield
        y_diag = jnp.concatenate(y_parts, axis=1)
        sl = slice(g * gw, (g + 1) * gw)
        if single_chunk:
            h_in = h0_ref[s,g * 8:(g + 1) * 8].reshape(gw, SSD_STATE)
            y_off = lax.dot_general(cm.astype(BF16), h_in.astype(BF16), (((1,), (1,)), ((), ())),
                                    preferred_element_type=F32)
            upd = lax.dot_general(xd[:, sl].astype(BF16), bm.astype(BF16), (((0,), (0,)), ((), ())),
                                  preferred_element_type=F32)
            head_decay = jnp.broadcast_to(jnp.exp(a_cumt[:, q - 1:q]), (SSD_HEADS, SSD_STATE))
            for e in range(8):
                h = g * 8 + e
                hfin_ref[s,h] = (h0_ref[s,h] * head_decay[h:h + 1, :]
                                  + upd[e * SSD_HEAD_DIM:(e + 1) * SSD_HEAD_DIM, :])
        else:
            h_in = h_scr[s * SSD_GROUPS + g]
            y_off = _bdot(cm, h_in)
            upd = lax.dot_general(bm.astype(BF16), xd[:, sl].astype(BF16), (((0,), (0,)), ((), ())),
                                  preferred_element_type=F32)
            h_scr[s * SSD_GROUPS + g] = h_in * chunk_decay_x[:, sl] + upd
        ys.append(y_diag + y_off * decay_in_x[:, sl])
        yield

    y = jnp.concatenate(ys, axis=1) + dsk_ref[...] * xs
    yg = y * _silu(z_ref[s])
    outs = []
    for g in range(SSD_GROUPS):
        t = yg[:, g * gw:(g + 1) * gw]
        outs.append(t * lax.rsqrt(jnp.mean(t * t, axis=-1, keepdims=True) + NORM_EPS))
    y_ref[s] = jnp.concatenate(outs, axis=1) * nrm_ref[...]

    @pl.when(c == last)
    def _():
        cnew_ref[s] = xfull_scr[s,CONV_PAD + q - 3:CONV_PAD + q, :]
        if not single_chunk:
            for g in range(SSD_GROUPS):
                hfin_ref[s,g * 8:(g + 1) * 8] = h_scr[s * SSD_GROUPS + g].T.reshape(8, SSD_HEAD_DIM, SSD_STATE)

    yield


def ssd_mixer(z, xbc, dt, dtt_flat, conv0, ssm0, w, *, q):
    b, l, _ = z.shape
    single_chunk = l == q
    want = SSD_SEQS_PER_STEP if single_chunk else SSD_LONG_SEQS_PER_STEP
    nseq = want if b % want == 0 else 1
    if q % 128 == 0 and nseq == 1:
        dtt = dtt_flat
        dtt_spec = pl.BlockSpec((SSD_HEADS, q), lambda i, c: (0, i * (l // q) + c))
    else:
        dtt = jnp.swapaxes(dt, 1, 2)
        dtt_spec = pl.BlockSpec((nseq, SSD_HEADS, q), lambda i, c: (i, 0, c))
    seq = lambda wd: pl.BlockSpec((nseq, q, wd), lambda i, c: (i, c, 0))
    per_b3 = lambda s: pl.BlockSpec((nseq,) + s, lambda i, c: (i,) + (0,) * len(s))
    col = lambda t: t.reshape(-1, 1)
    rowv = lambda t: t.reshape(1, -1)
    consts = [w["conv_w"], rowv(w["conv_b"]), rowv(w["dt_bias"]), col(w["dt_bias"]),
              rowv(w["a_log"]), col(w["a_log"]),
              rowv(jnp.repeat(w["d_skip"], SSD_HEAD_DIM)), rowv(w["ssd_norm"])]
    return pl.pallas_call(
        functools.partial(_ssd_kernel, q=q, nseq=nseq, single_chunk=single_chunk),
        grid=(b // nseq, l // q),
        in_specs=[seq(SSD_WIDTH), seq(SSD_CONV_DIM), seq(SSD_HEADS),
                  dtt_spec,
                  per_b3((SSD_CONV - 1, SSD_CONV_DIM)),
                  per_b3((SSD_HEADS, SSD_HEAD_DIM, SSD_STATE))]
                 + [_const_spec(t.shape) for t in consts],
        out_specs=[seq(SSD_WIDTH), per_b3((SSD_HEADS, SSD_HEAD_DIM, SSD_STATE)),
                   per_b3((SSD_CONV - 1, SSD_CONV_DIM))],
        out_shape=[jax.ShapeDtypeStruct((b, l, SSD_WIDTH), F32),
                   jax.ShapeDtypeStruct((b, SSD_HEADS, SSD_HEAD_DIM, SSD_STATE), F32),
                   jax.ShapeDtypeStruct((b, SSD_CONV - 1, SSD_CONV_DIM), F32)],
        scratch_shapes=[pltpu.VMEM((nseq, CONV_PAD + q, SSD_CONV_DIM), F32),
                        pltpu.VMEM((nseq * SSD_GROUPS, SSD_STATE, SSD_GROUP_WIDTH), F32)],
        compiler_params=pltpu.CompilerParams(
            dimension_semantics=("parallel", "arbitrary"), vmem_limit_bytes=V7X_VMEM_LIMIT),
        name="ssd_mixer",
    )(z, xbc, dt, dtt, conv0, ssm0, *consts)


def _head_sums(t):
    pair = 2 * RWKV_HEAD_DIM
    first = lax.broadcasted_iota(jnp.int32, (1, pair), 1) < RWKV_HEAD_DIM
    pieces = []
    for j in range(RWKV_HEADS // 2):
        x = t[:, j * pair:(j + 1) * pair]
        x0 = jnp.where(first, x, 0.0)
        s0 = jnp.sum(x0, axis=-1, keepdims=True)
        s1 = jnp.sum(x - x0, axis=-1, keepdims=True)
        pieces.append(jnp.where(first, s0, s1))
    return jnp.concatenate(pieces, axis=1)


def _rwkv_prep_kernel(rw_ref, sh0_ref, mu_ref, w0_ref, w2_ref, a0_ref, a2_ref, g2_ref,
                      kk_ref, ka_ref, rk_ref,
                      *rest, tt, seqs, chunk):
    outs, full_scr = rest[:-1], rest[-1]
    shn_ref = outs[-1]
    c = pl.program_id(1)
    l = tt // seqs

    if seqs == 1:
        @pl.when(c == 0)
        def _():
            full_scr[0, CONV_PAD - 1:CONV_PAD, :] = sh0_ref[0]

        @pl.when(c > 0)
        def _():
            full_scr[0, CONV_PAD - 1:CONV_PAD, :] = full_scr[0, CONV_PAD + tt - 1:CONV_PAD + tt, :]

        rw = rw_ref[0]
        full_scr[0, CONV_PAD:CONV_PAD + tt, :] = rw
        prev = full_scr[0, CONV_PAD - 1:CONV_PAD - 1 + tt, :]
    else:
        full_scr[:, CONV_PAD - 1:CONV_PAD, :] = sh0_ref[...]
        full_scr[:, CONV_PAD:CONV_PAD + l, :] = rw_ref[...]
        rw = rw_ref[...].reshape(tt, RWKV_PROJ)
        prev = full_scr[:, CONV_PAD - 1:CONV_PAD - 1 + l, :].reshape(tt, RWKV_PROJ)
    vals = _rwkv_mix_math(rw, prev, mu_ref, w0_ref, w2_ref, a0_ref, a2_ref, g2_ref, kk_ref, ka_ref, rk_ref)
    _emit_rwkv_outputs(outs[:-1], vals, tt=tt, chunk=chunk)

    if seqs == 1:
        @pl.when(c == pl.num_programs(1) - 1)
        def _():
            shn_ref[0] = full_scr[0, CONV_PAD + tt - 1:CONV_PAD + tt, :]
    else:
        shn_ref[...] = rw_ref[:, l - 1:l, :]


def _rwkv_mix_math(rw, prev, mu_ref, w0_ref, w2_ref, a0_ref, a2_ref, g2_ref, kk_ref, ka_ref, rk_ref):
    wd = RWKV_WIDTH
    u = rw + (prev - rw) * mu_ref[...]
    r = u[:, :wd]
    k = u[:, wd:2 * wd]
    v = u[:, 2 * wd:3 * wd]
    w_lo = u[:, 3 * wd:3 * wd + DECAY_LORA]
    a_lo = u[:, 3 * wd + DECAY_LORA:3 * wd + DECAY_LORA + AAA_LORA]
    g_lo = u[:, 3 * wd + DECAY_LORA + AAA_LORA:]

    w_log = -_softplus(-(w0_ref[...] + _bdot(jnp.tanh(w_lo), w2_ref[...]))) - 0.5
    lw = -jnp.exp(w_log)
    a = _sigmoid(a0_ref[...] + _bdot(a_lo, a2_ref[...]))
    gate = _bdot(_sigmoid(g_lo), g2_ref[...])

    kk = k * kk_ref[...]
    kk = kk / jnp.maximum(jnp.sqrt(_head_sums(kk * kk)), 1e-12)
    kf = k * (1.0 + (a - 1.0) * ka_ref[...])
    kb = kk * a
    bonus = _head_sums(r * kf * rk_ref[...]) * v
    return r, lw, kf, v, kk, kb, gate, bonus


def _emit_rwkv_outputs(outs, vals, *, tt, chunk, row0=0):
    r, lw, kf, v, kk, kb, gate, bonus = vals
    wd = RWKV_WIDTH
    gate_out, bonus_out = outs[-2:]
    if chunk == 0:
        blk = gate_out.shape
        for ref, val in zip(outs, (r, jnp.exp(lw), kf, v, kk, kb, gate, bonus)):
            ref[...] = val.reshape(blk)
    else:
        gate_out[0, row0:row0 + tt, :] = gate
        bonus_out[0, row0:row0 + tt, :] = bonus
        kkt_out, rt_out, kfh_out, nbh_out, vb_out, kbg_out, kfg_out, gend_out = outs[:8]
        tri = (lax.broadcasted_iota(jnp.int32, (chunk, chunk), 0)
               >= lax.broadcasted_iota(jnp.int32, (chunk, chunk), 1)).astype(F32)
        for ci in range(tt // chunk):
            rs = slice(ci * chunk, (ci + 1) * chunk)
            ro = slice(row0 + ci * chunk, row0 + (ci + 1) * chunk)
            lw_c = lw[rs]
            cum = _dot01(tri, lw_c, exact_side="rhs")
            cum_end = cum[chunk - 1:chunk, :]
            g_inv = jnp.exp(-cum)
            g_tail = jnp.exp(cum_end - cum)
            kkt_out[0, ro, :] = (kk[rs] * jnp.exp(cum - lw_c)).astype(BF16)
            rt_out[0, ro, :] = (r[rs] * jnp.exp(cum)).astype(BF16)
            kfh_out[0, ro, :] = (kf[rs] * g_tail).astype(BF16)
            nbh_out[0, ro, :] = (-kb[rs] * g_tail).astype(BF16)
            vb_out[0, ro, :] = v[rs].astype(BF16)
            kbg_out[0, ro, :] = (kb[rs] * g_inv).astype(BF16)
            kfg_out[0, ro, :] = (kf[rs] * g_inv).astype(BF16)
            gend_out[0, row0 // chunk + ci] = jnp.broadcast_to(jnp.exp(cum_end), (8, wd))


def rwkv_prep(rw, shift0, w, *, tt, chunk):
    b, l, _ = rw.shape
    seqs = max(1, tt // l)
    assert chunk == 0 or (seqs == 1 and tt % chunk == 0)
    rowv = lambda t: t.reshape(1, -1)
    consts = [rowv(w["shift_mu"]), rowv(w["w0"]), w["w2"].astype(BF16), rowv(w["a0"]),
              w["a2"].astype(BF16), w["g2"].astype(BF16), rowv(w["k_k"]), rowv(w["k_a"]),
              rowv(w["r_k"])]
    rows = tt // seqs
    grid = (b // seqs, l // rows)
    seq = lambda wd: pl.BlockSpec((seqs, rows, wd), lambda i, c: (i, c, 0))
    one = pl.BlockSpec((seqs, 1, RWKV_PROJ), lambda i, c: (i, 0, 0))
    sds = jax.ShapeDtypeStruct
    f32_seq = sds((b, l, RWKV_WIDTH), F32)
    if chunk == 0:
        op_specs = [seq(RWKV_WIDTH)] * 6
        op_shapes = [f32_seq] * 6
    else:
        per_tile = tt // chunk
        op_specs = [seq(RWKV_WIDTH)] * 7 + [
            pl.BlockSpec((1, per_tile, 8, RWKV_WIDTH), lambda i, c: (i, c, 0, 0))]
        op_shapes = [sds((b, l, RWKV_WIDTH), BF16)] * 7 + [sds((b, l // chunk, 8, RWKV_WIDTH), F32)]
    outs = pl.pallas_call(
        functools.partial(_rwkv_prep_kernel, tt=tt, seqs=seqs, chunk=chunk),
        grid=grid,
        in_specs=[seq(RWKV_PROJ), one] + [_const_spec(t.shape) for t in consts],
        out_specs=op_specs + [seq(RWKV_WIDTH)] * 2 + [one],
        out_shape=op_shapes + [f32_seq] * 2 + [sds((b, 1, RWKV_PROJ), F32)],
        scratch_shapes=[pltpu.VMEM((seqs, CONV_PAD + rows, RWKV_PROJ), F32)],
        compiler_params=pltpu.CompilerParams(
            dimension_semantics=("parallel", "arbitrary"), vmem_limit_bytes=V7X_VMEM_LIMIT),
        name="rwkv_prep",
    )(rw, shift0, *consts)
    return outs


def _proj_prep_kernel(x_ref, sh0_ref, g_ref, wz_ref, wx_ref, wr_ref, wdt_ref, wdtt_ref,
                      mu_ref, w0_ref, w2_ref, a0_ref, a2_ref, g2_ref, kk_ref, ka_ref, rk_ref,
                      z_ref, xbc_ref, dt_ref, dtt_ref, *rest, tm, tiles_per_seq, chunk):
    outs, (new_scr, cur_scr) = rest[:-2], rest[-2:]
    shn_ref = outs[-1]
    i = pl.program_id(0)

    @pl.when(i == 0)
    def _():
        new_scr[...] = jnp.zeros(new_scr.shape, F32)
        cur_scr[...] = jnp.zeros(cur_scr.shape, F32)

    k = jnp.maximum(i - 1, 0)
    first = (k % tiles_per_seq) == 0
    cur_scr[CONV_PAD - 1:CONV_PAD, :] = jnp.where(first, sh0_ref[0], cur_scr[CONV_PAD + tm - 1:CONV_PAD + tm, :])
    cur_scr[CONV_PAD:CONV_PAD + tm, :] = new_scr[...]

    shn_ref[0] = cur_scr[CONV_PAD + tm - 1:CONV_PAD + tm, :]
    u = _rms(x_ref[...], g_ref[...]).astype(BF16)

    def project(piece):
        if piece == 0:
            z_ref[...] = jnp.dot(u, wz_ref[...], preferred_element_type=F32)
        elif piece == 1:
            xbc_ref[...] = jnp.dot(u, wx_ref[...], preferred_element_type=F32)
        elif piece == 2:
            cols = slice(0, 2 * RWKV_WIDTH)
            new_scr[:, cols] = jnp.dot(u, wr_ref[:, cols], preferred_element_type=F32)
        else:
            cols = slice(2 * RWKV_WIDTH, RWKV_PROJ)
            new_scr[:, cols] = jnp.dot(u, wr_ref[:, cols], preferred_element_type=F32)
            dt_ref[...] = jnp.dot(u, wdt_ref[...], preferred_element_type=F32)
            dtt_ref[...] = lax.dot_general(wdtt_ref[...], u, (((1,), (1,)), ((), ())),
                                           preferred_element_type=F32)

    def prepare(part, rows):
        lo = CONV_PAD + part * rows
        padded = cur_scr[lo - CONV_PAD:lo + rows, :]
        rw = padded[CONV_PAD:]
        prev = pltpu.roll(padded, 1, axis=0)[CONV_PAD:]
        vals = _rwkv_mix_math(rw, prev, mu_ref, w0_ref, w2_ref, a0_ref, a2_ref, g2_ref, kk_ref, ka_ref, rk_ref)
        _emit_rwkv_outputs(outs[:-1], vals, tt=rows, chunk=chunk, row0=part * rows)

    pieces = 4
    parts = min(pieces, tm // chunk)
    for part in range(parts):
        for piece in range(part * pieces // parts, (part + 1) * pieces // parts):
            project(piece)
        prepare(part, tm // parts)


def proj_prep(x, shift0, g, wz, wx, wr, wdt, w, *, tm, chunk):
    b, l, _ = x.shape
    n = b * l
    nt, tps = n // tm, l // tm
    assert l % tm == 0 and tm % chunk == 0
    x2 = x.reshape(n, D_MODEL)
    rowv = lambda t: t.reshape(1, -1)
    wdtt = wdt.T
    consts = [g, wz, wx, wr, wdt, wdtt,
              rowv(w["shift_mu"]), rowv(w["w0"]), w["w2"].astype(BF16), rowv(w["a0"]),
              w["a2"].astype(BF16), w["g2"].astype(BF16), rowv(w["k_k"]), rowv(w["k_a"]), rowv(w["r_k"])]
    ahead = lambda i: jnp.minimum(i, nt - 1)
    behind = lambda i: jnp.maximum(i - 1, 0)
    row_a = lambda w_: pl.BlockSpec((tm, w_), lambda i: (ahead(i), 0))
    seq_b = lambda w_: pl.BlockSpec((1, tm, w_), lambda i: (behind(i) // tps, behind(i) % tps, 0))
    one_b = pl.BlockSpec((1, 1, RWKV_PROJ), lambda i: (behind(i) // tps, 0, 0))
    per_tile = tm // chunk
    gend_spec = pl.BlockSpec((1, per_tile, 8, RWKV_WIDTH), lambda i: (behind(i) // tps, behind(i) % tps, 0, 0))
    sds = jax.ShapeDtypeStruct
    outs = pl.pallas_call(
        functools.partial(_proj_prep_kernel, tm=tm, tiles_per_seq=tps, chunk=chunk),
        grid=(nt + 1,),
        in_specs=[row_a(D_MODEL), one_b] + [_const_spec(t.shape) for t in consts],
        out_specs=[row_a(SSD_WIDTH), row_a(SSD_CONV_DIM), row_a(SSD_HEADS),
                   pl.BlockSpec((SSD_HEADS, tm), lambda i: (0, ahead(i)))]
                  + [seq_b(RWKV_WIDTH)] * 7 + [gend_spec] + [seq_b(RWKV_WIDTH)] * 2 + [one_b],
        out_shape=[sds((n, SSD_WIDTH), F32), sds((n, SSD_CONV_DIM), F32), sds((n, SSD_HEADS), F32),
                   sds((SSD_HEADS, n), F32)]
                  + [sds((b, l, RWKV_WIDTH), BF16)] * 7 + [sds((b, l // chunk, 8, RWKV_WIDTH), F32)]
                  + [sds((b, l, RWKV_WIDTH), F32)] * 2 + [sds((b, 1, RWKV_PROJ), F32)],
        scratch_shapes=[pltpu.VMEM((tm, RWKV_PROJ), F32), pltpu.VMEM((CONV_PAD + tm, RWKV_PROJ), F32)],
        compiler_params=pltpu.CompilerParams(
            dimension_semantics=("arbitrary",), vmem_limit_bytes=V7X_VMEM_LIMIT),
        name="proj_prep",
    )(x2, shift0, *consts)
    return outs


def _wkv_kernel(r_ref, w_ref, k_ref, v_ref, kk_ref, kka_ref, s0_ref,
                o_ref, sfin_ref, s_scr, vt_scr, ot_scr, *, steps):
    c = pl.program_id(1)
    n = RWKV_HEAD_DIM
    lanes = WKV_BATCH_BLOCK * RWKV_HEADS

    @pl.when(c == 0)
    def _():
        s_scr[...] = s0_ref[...].reshape(lanes, n * n).T.reshape(n, n, lanes)

    def to_pairs(ref, t):
        return ref[:, t].reshape(lanes, n).T

    def step(t, carry):
        r_t = to_pairs(r_ref, t)
        w_t = to_pairs(w_ref, t)
        k_t = to_pairs(k_ref, t)
        kk_t = to_pairs(kk_ref, t)
        kka_t = to_pairs(kka_ref, t)
        vt_scr[...] = to_pairs(v_ref, t)

        def per_value(vi, carry2):
            s_v = s_scr[vi]
            skk = jnp.sum(s_v * kk_t, axis=0, keepdims=True)
            v_row = vt_scr[pl.ds(vi, 1), :]
            s_new = s_v * w_t - skk * kka_t + v_row * k_t
            s_scr[vi] = s_new
            ot_scr[pl.ds(vi, 1), :] = jnp.sum(s_new * r_t, axis=0, keepdims=True)
            return carry2

        lax.fori_loop(0, n, per_value, 0, unroll=4)
        o_ref[:, t] = ot_scr[...].T.reshape(WKV_BATCH_BLOCK, RWKV_HEADS, n)
        return carry

    lax.fori_loop(0, steps, step, 0)

    @pl.when(c == pl.num_programs(1) - 1)
    def _():
        sfin_ref[...] = s_scr[...].reshape(n * n, lanes).T.reshape(
            WKV_BATCH_BLOCK, RWKV_HEADS, n, n)


def wkv_scan(r, w, k, v, kk, kka, s0, *, steps):
    b, l, _ = r.shape
    h, n = RWKV_HEADS, RWKV_HEAD_DIM
    assert b % WKV_BATCH_BLOCK == 0 and l % steps == 0
    ops = [t.reshape(b, l, h, n) for t in (r, w, k, v, kk, kka)]
    seq_spec = pl.BlockSpec((WKV_BATCH_BLOCK, steps, h, n), lambda g, c: (g, c, 0, 0))
    st_spec = pl.BlockSpec((WKV_BATCH_BLOCK, h, n, n), lambda g, c: (g, 0, 0, 0))
    o, s_fin = pl.pallas_call(
        functools.partial(_wkv_kernel, steps=steps),
        grid=(b // WKV_BATCH_BLOCK, l // steps),
        in_specs=[seq_spec] * 6 + [st_spec],
        out_specs=[seq_spec, st_spec],
        out_shape=[jax.ShapeDtypeStruct((b, l, h, n), F32),
                   jax.ShapeDtypeStruct((b, h, n, n), F32)],
        scratch_shapes=[pltpu.VMEM((n, n, WKV_BATCH_BLOCK * h), F32),
                        pltpu.VMEM((n, WKV_BATCH_BLOCK * h), F32),
                        pltpu.VMEM((n, WKV_BATCH_BLOCK * h), F32)],
        compiler_params=pltpu.CompilerParams(
            dimension_semantics=("parallel", "arbitrary"), vmem_limit_bytes=V7X_VMEM_LIMIT),
        name="wkv_scan",
    )(*ops, s0)
    return o.reshape(b, l, h * n), s_fin


WKV_LANE_BATCH = 128


def _wkv_batch_lanes_kernel(r_ref, w_ref, k_ref, v_ref, kk_ref, kka_ref, s0_ref,
                            o_ref, sfin_ref, op_scr, ot_scr, *, steps):
    n = RWKV_HEAD_DIM
    nb = WKV_LANE_BATCH
    sfin_ref[...] = s0_ref[...]

    def step(t, carry):
        rows = pl.ds(t, nb, stride=steps)
        for i, ref in enumerate((r_ref, w_ref, k_ref, kk_ref, kka_ref, v_ref)):
            op_scr[i] = ref[rows, :].T
        for h2 in range(2):
            ch = slice(h2 * n, (h2 + 1) * n)
            r_t = op_scr[0, ch, :]
            kka_r = jnp.sum(op_scr[4, ch, :] * r_t, axis=0, keepdims=True)
            k_r = jnp.sum(op_scr[2, ch, :] * r_t, axis=0, keepdims=True)
            op_scr[0, ch, :] = op_scr[1, ch, :] * r_t

            def per_value(vi, carry2, ch=ch, h2=h2, kka_r=kka_r, k_r=k_r):
                s_v = sfin_ref[h2, vi]
                skk = jnp.sum(s_v * op_scr[3, ch, :], axis=0, keepdims=True)
                out = jnp.sum(s_v * op_scr[0, ch, :], axis=0, keepdims=True)
                v_row = op_scr[5, pl.ds(h2 * n + vi, 1), :]
                sfin_ref[h2, vi] = s_v * op_scr[1, ch, :] - skk * op_scr[4, ch, :] + v_row * op_scr[2, ch, :]
                ot_scr[pl.ds(h2 * n + vi, 1), :] = out - skk * kka_r + v_row * k_r
                return carry2

            lax.fori_loop(0, n, per_value, 0, unroll=8)
        o_ref[rows, :] = ot_scr[...].T
        return carry

    lax.fori_loop(0, steps, step, 0)


def wkv_scan_batch_lanes(r, w, k, v, kk, kka, s0):
    b, l, wd = r.shape
    h, n, nb = RWKV_HEADS, RWKV_HEAD_DIM, WKV_LANE_BATCH
    assert b % nb == 0
    ops = [t.reshape(b * l, wd) for t in (r, w, k, v, kk, kka)]
    s0t = jnp.transpose(s0, (1, 2, 3, 0))
    seq_spec = pl.BlockSpec((nb * l, 2 * n), lambda g, j: (g, j))
    st_spec = pl.BlockSpec((2, n, n, nb), lambda g, j: (j, 0, 0, g))
    o, s_fin = pl.pallas_call(
        functools.partial(_wkv_batch_lanes_kernel, steps=l),
        grid=(b // nb, h // 2),
        in_specs=[seq_spec] * 6 + [st_spec],
        out_specs=[seq_spec, st_spec],
        out_shape=[jax.ShapeDtypeStruct((b * l, wd), F32), jax.ShapeDtypeStruct((h, n, n, b), F32)],
        scratch_shapes=[pltpu.VMEM((6, 2 * n, nb), F32), pltpu.VMEM((2 * n, nb), F32)],
        compiler_params=pltpu.CompilerParams(
            dimension_semantics=("parallel", "parallel"), vmem_limit_bytes=V7X_VMEM_LIMIT),
        name="wkv_scan_batch_lanes",
    )(*ops, s0t)
    return o.reshape(b, l, wd), jnp.transpose(s_fin, (3, 0, 1, 2))


WKV_CHUNK = 64
WKV_PAIRS = RWKV_HEADS // 2
WKV_ROW_STRIDE = WKV_CHUNK + 8
WKV_SOLVE_ROWS = 8
WKV_SOLVE_COLS = 16


def _pair_masks():
    c = WKV_CHUNK
    row = lax.broadcasted_iota(jnp.int32, (2 * c, 2 * c), 0)
    col = lax.broadcasted_iota(jnp.int32, (2 * c, 2 * c), 1)
    t, i = row % c, col % c
    keep = i <= t - jnp.where(row < c, 1, 0)
    sign = jnp.where(row >= c, jnp.where(col < c, -1.0, 1.0), 1.0)
    block_diag = row // c == col // c
    return keep, sign, block_diag


def _wkv_prepare_kernel(kkt_ref, rt_ref, kbg_ref, kfg_ref, vb_ref,
                        lo_ref, rhs0_ref, tp_ref, abs_scr, top_scr, abt_scr, tt_scr):
    c = WKV_CHUNK
    keep, sign, _ = _pair_masks()
    lane = lax.broadcasted_iota(jnp.int32, (1, 2 * c), 1)
    head0 = lane < RWKV_HEAD_DIM
    row_head0 = lax.broadcasted_iota(jnp.int32, (2 * RWKV_HEAD_DIM, 1), 0) < RWKV_HEAD_DIM
    zeros = jnp.zeros((c, 2 * c), BF16)

    def per_batch(b, carry):
        for j in range(WKV_PAIRS):
            sl = slice(j * 2 * RWKV_HEAD_DIM, (j + 1) * 2 * RWKV_HEAD_DIM)
            lhs = jnp.concatenate([kkt_ref[b, :, sl], rt_ref[b, :, sl]], axis=0)
            rhs = jnp.concatenate([kbg_ref[b, :, sl], kfg_ref[b, :, sl]], axis=0)
            rhs_t = rhs.astype(F32).T
            top = jnp.where(row_head0, rhs_t, 0.0)
            w_a = jnp.concatenate([top, rhs_t - top], axis=1).astype(BF16)
            a_both = jnp.dot(lhs, w_a, preferred_element_type=F32)
            for h2 in range(2):
                a = a_both[:, h2 * 2 * c:(h2 + 1) * 2 * c]
                a = jnp.where(keep, a, 0.0) * sign
                inst = h2 * (WKV_BATCH_BLOCK * WKV_PAIRS) + b * WKV_PAIRS + j
                abs_scr[pl.ds(inst * WKV_ROW_STRIDE, c), :c] = a[:c, :c]
                top_scr[j, :, h2 * 2 * c:(h2 + 1) * 2 * c] = a[:c].astype(BF16)
                lo_ref[b, :, (2 * j + h2) * 2 * c:(2 * j + h2 + 1) * 2 * c] = a[c:].astype(BF16)

        for j in range(WKV_PAIRS):
            sl = slice(j * 2 * RWKV_HEAD_DIM, (j + 1) * 2 * RWKV_HEAD_DIM)
            v = vb_ref[b, :, sl]
            v0 = jnp.where(head0, v, jnp.zeros_like(v))
            w_akf = jnp.concatenate([zeros, v0, zeros, v - v0], axis=0)
            rhs0_ref[b, :, sl] = jnp.dot(top_scr[j], w_akf, preferred_element_type=F32)
        return carry

    lax.fori_loop(0, WKV_BATCH_BLOCK, per_batch, 0)

    n_inst = 2 * WKV_BATCH_BLOCK * WKV_PAIRS
    n_pair_rows = WKV_BATCH_BLOCK * WKV_PAIRS

    def to_lanes(t, carry):
        abt_scr[t] = abs_scr[pl.ds(t, n_inst, stride=WKV_ROW_STRIDE), :][:, :c].T
        return carry

    lax.fori_loop(0, c, to_lanes, 0, unroll=8)

    tt_scr[...] = jnp.zeros(tt_scr.shape, F32)
    sub_iota = lax.broadcasted_iota(jnp.int32, (WKV_SOLVE_COLS, n_inst), 0)
    rows = range(WKV_SOLVE_ROWS)
    for cb in range(c // WKV_SOLVE_COLS):
        col0 = WKV_SOLVE_COLS * cb
        cols = slice(col0, col0 + WKV_SOLVE_COLS)
        first_block = col0 // WKV_SOLVE_ROWS

        def solve_rows(tb, carry, col0=col0, cols=cols, first_block=first_block):
            t0 = tb * WKV_SOLVE_ROWS

            def sub(ib, accs):
                ps = [tt_scr[ib * WKV_SOLVE_ROWS + di, cols, :] for di in rows]
                out = []
                for r in rows:
                    terms = [abt_scr[t0 + r, pl.ds(ib * WKV_SOLVE_ROWS + di, 1), :] * ps[di] for di in rows]
                    while len(terms) > 1:
                        terms = [a + b for a, b in zip(terms[::2], terms[1::2])]
                    out.append(accs[r] - terms[0])
                return tuple(out)

            unit = tuple(jnp.where(sub_iota + col0 == t0 + r, 1.0, 0.0) for r in rows)
            accs = list(lax.fori_loop(first_block, tb, sub, unit))
            for r in rows:
                for r2 in range(r):
                    accs[r] = accs[r] - abt_scr[t0 + r, pl.ds(t0 + r2, 1), :] * accs[r2]
                tt_scr[t0 + r, cols, :] = accs[r]
            return carry

        lax.fori_loop(first_block, c // WKV_SOLVE_ROWS, solve_rows, 0)

    def from_lanes(t, carry):
        m = tt_scr[t].T
        abs_scr[pl.ds(t, n_pair_rows, stride=WKV_ROW_STRIDE), :] = jnp.concatenate(
            [m[:n_pair_rows], m[n_pair_rows:]], axis=1)
        return carry

    lax.fori_loop(0, c, from_lanes, 0, unroll=8)

    def emit(b, carry):
        for j in range(WKV_PAIRS):
            row0 = (b * WKV_PAIRS + j) * WKV_ROW_STRIDE
            tp_ref[b, :, j * 2 * c:(j + 1) * 2 * c] = abs_scr[pl.ds(row0, c), :].astype(BF16)
        return carry

    lax.fori_loop(0, WKV_BATCH_BLOCK, emit, 0)


def _wkv_apply_kernel(kkt_ref, rt_ref, kfh_ref, nbh_ref, vb_ref, tp_ref, lo_ref, rhs0_ref, gend_ref, s0_ref,
                      o_ref, sfin_ref, x_scr, wp_scr, p_scr):
    c = WKV_CHUNK
    n = RWKV_HEAD_DIM
    ch = pl.program_id(1)
    _, _, block_diag = _pair_masks()
    lane = lax.broadcasted_iota(jnp.int32, (1, 2 * c), 1)
    head0 = lane < n
    eye2 = (lax.broadcasted_iota(jnp.int32, (n, 2 * n), 0)
            == lax.broadcasted_iota(jnp.int32, (n, 2 * n), 1) % n).astype(F32)

    @pl.when(ch == 0)
    def _():
        def init(b, carry):
            for j in range(WKV_PAIRS):
                sp = s0_ref[b, 2 * j:2 * j + 2].reshape(2 * n, n)
                dup = jnp.dot(sp, eye2, precision=HIGHEST, preferred_element_type=F32)
                x_scr[b, j] = jnp.where(block_diag, dup, 0.0)
            return carry
        lax.fori_loop(0, WKV_BATCH_BLOCK, init, 0)

    def per_batch(b, carry):
        for j in range(WKV_PAIRS):
            sl = slice(j * 2 * n, (j + 1) * 2 * n)
            lhs = jnp.concatenate([kkt_ref[b, :, sl], rt_ref[b, :, sl]], axis=0)
            kx = lax.dot_general(lhs, x_scr[b, j].astype(BF16), (((1,), (1,)), ((), ())),
                                 preferred_element_type=F32)
            rhs = kx[:c] + rhs0_ref[b, :, sl]
            r0 = jnp.where(head0, rhs, 0.0)
            wp_scr[j] = jnp.concatenate([r0, rhs - r0], axis=0).astype(BF16)
            o_ref[b, :, sl] = kx[c:]
        for j in range(WKV_PAIRS):
            sl = slice(j * 2 * n, (j + 1) * 2 * n)
            p = jnp.dot(tp_ref[b, :, sl], wp_scr[j], preferred_element_type=F32)
            p_scr[j] = p.astype(BF16)
        for j in range(WKV_PAIRS):
            sl = slice(j * 2 * n, (j + 1) * 2 * n)
            v = vb_ref[b, :, sl]
            pb = p_scr[j]
            zero = jnp.zeros_like(pb)
            p0, v0 = jnp.where(head0, pb, zero), jnp.where(head0, v, zero)
            w_o = jnp.concatenate([p0, v0, pb - p0, v - v0], axis=0)
            o_ref[b, :, sl] = o_ref[b, :, sl] + jnp.dot(
                lo_ref[b, :, j * 4 * c:(j + 1) * 4 * c], w_o, preferred_element_type=F32)
            vp = jnp.concatenate([v, pb], axis=0)
            kb = jnp.concatenate([kfh_ref[b, :, sl], nbh_ref[b, :, sl]], axis=0)
            upd = lax.dot_general(vp, kb, (((0,), (0,)), ((), ())), preferred_element_type=F32)
            x_scr[b, j] = jnp.where(block_diag, x_scr[b, j] * gend_ref[b, 0, 0:1, sl] + upd, 0.0)
        return carry

    lax.fori_loop(0, WKV_BATCH_BLOCK, per_batch, 0)

    @pl.when(ch == pl.num_programs(1) - 1)
    def _():
        def fin(b, carry):
            for j in range(WKV_PAIRS):
                sp = lax.dot_general(x_scr[b, j], eye2, (((1,), (1,)), ((), ())),
                                     precision=HIGHEST, preferred_element_type=F32)
                sfin_ref[b, 2 * j:2 * j + 2] = sp.reshape(2, n, n)
            return carry
        lax.fori_loop(0, WKV_BATCH_BLOCK, fin, 0)


def wkv_chunked(kkt, rt, kfh, nbh, vb, kbg, kfg, gend, s0):
    b, l, wd = kkt.shape
    c = WKV_CHUNK
    assert b % WKV_BATCH_BLOCK == 0 and l % c == 0
    gb, nc = b // WKV_BATCH_BLOCK, l // c
    lanes = 2 * WKV_BATCH_BLOCK * WKV_PAIRS
    seq = lambda w_: pl.BlockSpec((WKV_BATCH_BLOCK, c, w_), lambda g, i: (g, i, 0))
    gend_spec = pl.BlockSpec((WKV_BATCH_BLOCK, 1, 8, wd), lambda g, i: (g, i, 0, 0))
    sds = jax.ShapeDtypeStruct
    st_spec = pl.BlockSpec((WKV_BATCH_BLOCK, RWKV_HEADS, RWKV_HEAD_DIM, RWKV_HEAD_DIM),
                           lambda g, i: (g, 0, 0, 0))
    blk = (WKV_BATCH_BLOCK, c, wd)
    o, s_fin = pl.pallas_call(
        _wkv_chunk_kernel,
        grid=(gb, nc),
        in_specs=[seq(wd)] * 7 + [gend_spec, st_spec],
        out_specs=[seq(wd), st_spec],
        out_shape=[sds((b, l, wd), F32), sds(s0.shape, F32)],
        scratch_shapes=[
            pltpu.VMEM((lanes * WKV_ROW_STRIDE, 2 * c), F32),
            pltpu.VMEM((WKV_PAIRS, c, 4 * c), BF16),
            pltpu.VMEM((c, c, lanes), F32),
            pltpu.VMEM((c, c, lanes), F32),
            pltpu.VMEM((WKV_BATCH_BLOCK, c, 2 * wd), BF16),
            pltpu.VMEM(blk, F32),
            pltpu.VMEM(blk, BF16),
            pltpu.VMEM((WKV_BATCH_BLOCK, WKV_PAIRS, 2 * RWKV_HEAD_DIM, 2 * RWKV_HEAD_DIM), F32),
            pltpu.VMEM((WKV_PAIRS, 2 * c, 2 * RWKV_HEAD_DIM), BF16),
            pltpu.VMEM((WKV_PAIRS, c, 2 * RWKV_HEAD_DIM), BF16)],
        compiler_params=pltpu.CompilerParams(
            dimension_semantics=("parallel", "arbitrary"), vmem_limit_bytes=V7X_VMEM_LIMIT),
        name="wkv_chunked",
    )(kkt, rt, kfh, nbh, vb, kbg, kfg, gend, s0)
    return o, s_fin


def _wkv_chunk_kernel(kkt_ref, rt_ref, kfh_ref, nbh_ref, vb_ref, kbg_ref, kfg_ref, gend_ref, s0_ref,
                      o_ref, sfin_ref, abs_scr, top_scr, abt_scr, tt_scr, lo_scr, rhs0_scr, tp_scr,
                      x_scr, wp_scr, p_scr):
    _wkv_prepare_kernel(kkt_ref, rt_ref, kbg_ref, kfg_ref, vb_ref, lo_scr, rhs0_scr, tp_scr,
                        abs_scr, top_scr, abt_scr, tt_scr)
    _wkv_apply_kernel(kkt_ref, rt_ref, kfh_ref, nbh_ref, vb_ref, tp_scr, lo_scr, rhs0_scr, gend_ref, s0_ref,
                      o_ref, sfin_ref, x_scr, wp_scr, p_scr)


def _tail_kernel(o_ref, gate_ref, bonus_ref, x_ref, ys_ref, p_ref, lnw_ref, lnb_ref, woa_ref, wob_ref,
                 nf_ref, wg_ref, wu_ref, wd_ref, np_ref, wpg_ref, wpp_ref, nl_ref, y_ref, new_scr, cur_scr):
    i = pl.program_id(0)

    @pl.when(i == 0)
    def _():
        new_scr[...] = jnp.zeros(new_scr.shape, BF16)

    cur_scr[...] = new_scr[...]

    parts = 4
    rows = o_ref.shape[0] // parts
    inv_n = 1.0 / RWKV_HEAD_DIM

    def vector_half(part):
        rs = slice(part * rows, (part + 1) * rows)
        o = o_ref[rs, :]
        mu = _head_sums(o) * inv_n
        d = o - mu
        var = _head_sums(d * d) * inv_n
        on = d * lax.rsqrt(var + GN_EPS) * lnw_ref[...] + lnb_ref[...]
        new_scr[rs, :] = ((on + bonus_ref[rs, :]) * gate_ref[rs, :]).astype(BF16)

    vector_half(0)
    y_rwkv = cur_scr[...]
    h = x_ref[...] + _bdot(ys_ref[...], woa_ref[...]) + jnp.dot(y_rwkv, wob_ref[...],
                                                                preferred_element_type=F32)
    hf = _rms(h, nf_ref[...]).astype(BF16)
    vector_half(1)
    gate = jnp.dot(hf, wg_ref[...], preferred_element_type=F32)
    up = jnp.dot(hf, wu_ref[...], preferred_element_type=F32)
    vector_half(2)
    h = h + _bdot(_silu(gate) * up, wd_ref[...])
    vector_half(3)
    pg = _sigmoid(_bdot(_rms(h, np_ref[...]), wpg_ref[...]))
    h = h + pg * _bdot(p_ref[...], wpp_ref[...])
    y_ref[...] = _rms(h, nl_ref[...])


def layer_tail(o, gate, bonus, x, y_ssd, p, consts, *, tm):
    n = x.shape[0]
    nt = n // tm
    ahead = lambda w_: pl.BlockSpec((tm, w_), lambda i: (jnp.minimum(i, nt - 1), 0))
    behind = lambda w_: pl.BlockSpec((tm, w_), lambda i: (jnp.maximum(i - 1, 0), 0))
    return pl.pallas_call(
        _tail_kernel,
        grid=(nt + 1,),
        in_specs=[ahead(D_MODEL)] * 3 + [behind(D_MODEL)] * 2 + [behind(PLE_DIM)]
                 + [_const_spec(t.shape) for t in consts],
        out_specs=behind(D_MODEL),
        out_shape=jax.ShapeDtypeStruct((n, D_MODEL), F32),
        scratch_shapes=[pltpu.VMEM((tm, D_MODEL), BF16), pltpu.VMEM((tm, D_MODEL), BF16)],
        compiler_params=pltpu.CompilerParams(
            dimension_semantics=("arbitrary",), vmem_limit_bytes=V7X_VMEM_LIMIT),
        name="layer_tail",
    )(o, gate, bonus, x, y_ssd, p, *consts)


def _prepare_weights(w):
    c0, c1, c2 = SSD_WIDTH, SSD_WIDTH + SSD_CONV_DIM, SSD_WIDTH + SSD_CONV_DIM + SSD_HEADS
    w_in = w["w_in"]
    rowv = lambda t: t.reshape(1, -1)
    return dict(
        w,
        wz=w_in[:, :c0].astype(BF16), wx=w_in[:, c0:c1].astype(BF16),
        wdt=w_in[:, c1:c2].astype(BF16), wr=w_in[:, c2:].astype(BF16),
        woa=w["w_out"][:SSD_WIDTH].astype(BF16), wob=w["w_out"][SSD_WIDTH:].astype(BF16),
        wg=w["w_gate"].astype(BF16), wu=w["w_up"].astype(BF16), wd=w["w_down"].astype(BF16),
        wpg=w["w_ple_gate"].astype(BF16), wpp=w["w_ple_proj"].astype(BF16),
        norm_mix_r=rowv(w["norm_mix"]), norm_ffn_r=rowv(w["norm_ffn"]),
        norm_ple_r=rowv(w["norm_ple"]), norm_final_r=rowv(w["norm_final"]),
        ln_x_w_r=rowv(w["ln_x_w"]), ln_x_b_r=rowv(w["ln_x_b"]),
    )


def layer_forward(x, p, conv0, shift0, ssm0, wkv0, w, *, tm, ssd_q, prep_tt, wkv_steps):
    b, l, _ = x.shape
    n = b * l
    x2 = x.reshape(n, D_MODEL)
    chunked = l % WKV_CHUNK == 0 and l % tm == 0
    if chunked:
        z, xbc, dt, dtt, *ops, gate, bonus, shift_new = proj_prep(
            x, shift0, w["norm_mix_r"], w["wz"], w["wx"], w["wr"], w["wdt"], w, tm=tm, chunk=WKV_CHUNK)
    else:
        z, xbc, rw, dt, dtt = in_projection(x2, w["norm_mix_r"], w["wz"], w["wx"], w["wr"], w["wdt"], tm=tm)
        *ops, gate, bonus, shift_new = rwkv_prep(rw.reshape(b, l, -1), shift0, w, tt=prep_tt, chunk=0)
    y_ssd, ssm_new, conv_new = ssd_mixer(
        z.reshape(b, l, -1), xbc.reshape(b, l, -1), dt.reshape(b, l, -1), dtt, conv0, ssm0, w, q=ssd_q)
    if chunked:
        o, wkv_new = wkv_chunked(*ops, wkv0)
    elif b % WKV_LANE_BATCH == 0:
        o, wkv_new = wkv_scan_batch_lanes(*ops, wkv0)
    else:
        o, wkv_new = wkv_scan(*ops, wkv0, steps=wkv_steps)
    flat = lambda t: t.reshape(n, -1)
    tail_consts = [w["ln_x_w_r"], w["ln_x_b_r"], w["woa"], w["wob"], w["norm_ffn_r"], w["wg"], w["wu"],
                   w["wd"], w["norm_ple_r"], w["wpg"], w["wpp"], w["norm_final_r"]]
    y = layer_tail(flat(o), flat(gate), flat(bonus), x2, flat(y_ssd), p.reshape(n, PLE_DIM),
                   tail_consts, tm=tm)
    return y.reshape(b, l, D_MODEL), ssm_new, conv_new, wkv_new, shift_new


def kernel(x_prompt, x_sample, state_ssm, state_conv, state_wkv, state_shift, p_prompt, p_sample, norm_mix, w_in, conv_w, conv_b, dt_bias, a_log, d_skip, ssd_norm, shift_mu, w0, w2, a0, a2, g2, k_k, k_a, r_k, ln_x_w, ln_x_b, w_out, norm_ffn, w_gate, w_up, w_down, norm_ple, w_ple_gate, w_ple_proj, norm_final):
    w = _prepare_weights(dict(
        norm_mix=norm_mix[0], w_in=w_in[0], conv_w=conv_w[0], conv_b=conv_b[0], dt_bias=dt_bias[0],
        a_log=a_log[0], d_skip=d_skip[0], ssd_norm=ssd_norm[0], shift_mu=shift_mu[0], w0=w0[0],
        w2=w2[0], a0=a0[0], a2=a2[0], g2=g2[0], k_k=k_k[0], k_a=k_a[0], r_k=r_k[0],
        ln_x_w=ln_x_w[0], ln_x_b=ln_x_b[0], w_out=w_out[0], norm_ffn=norm_ffn[0],
        w_gate=w_gate[0], w_up=w_up[0], w_down=w_down[0], norm_ple=norm_ple[0],
        w_ple_gate=w_ple_gate[0], w_ple_proj=w_ple_proj[0], norm_final=norm_final))
    bp = x_prompt.shape[0]
    zeros = lambda *s: jnp.zeros(s, F32)
    yp, s1, c1, k1, t1 = layer_forward(
        x_prompt, p_prompt[0], zeros(bp, SSD_CONV - 1, SSD_CONV_DIM), zeros(bp, 1, RWKV_PROJ),
        zeros(bp, SSD_HEADS, SSD_HEAD_DIM, SSD_STATE),
        zeros(bp, RWKV_HEADS, RWKV_HEAD_DIM, RWKV_HEAD_DIM), w,
        tm=256, ssd_q=min(SSD_CHUNK, x_prompt.shape[1]), prep_tt=min(128, x_prompt.shape[1]),
        wkv_steps=min(16, x_prompt.shape[1]))
    ys, s2, c2, k2, t2 = layer_forward(
        x_sample, p_sample[0], state_conv[0], state_shift[0], state_ssm[0], state_wkv[0], w,
        tm=256, ssd_q=x_sample.shape[1], prep_tt=128, wkv_steps=x_sample.shape[1])
    return (yp, ys, s1[None], c1[None], k1[None], t1[None], s2[None], c2[None], k2[None], t2[None])
```

```python
import functools

import jax
import jax.numpy as jnp
from jax import lax
from jax.experimental import pallas as pl
from jax.experimental.pallas import tpu as pltpu

F32 = jnp.float32
BF16 = jnp.bfloat16
HIGHEST = lax.Precision.HIGHEST

D_MODEL = 1024
SSD_WIDTH = 1024
SSD_HEADS = 16
SSD_HEAD_DIM = 64
SSD_GROUPS = 2
SSD_GROUP_WIDTH = SSD_WIDTH // SSD_GROUPS
SSD_STATE = 128
SSD_CONV = 4
SSD_CHUNK = 128
SSD_BC = SSD_GROUPS * SSD_STATE
SSD_CONV_DIM = SSD_WIDTH + 2 * SSD_BC
RWKV_WIDTH = 1024
RWKV_HEADS = 16
RWKV_HEAD_DIM = 64
DECAY_LORA = 64
AAA_LORA = 64
GATE_LORA = 128
RWKV_PROJ = 3 * RWKV_WIDTH + DECAY_LORA + AAA_LORA + GATE_LORA
PLE_DIM = 256
NORM_EPS = 1e-6
GN_EPS = 64e-5

WKV_BATCH_BLOCK = 8
V7X_VMEM_LIMIT = 56 * 1024 * 1024
CONV_PAD = 8
SSD_SEQS_PER_STEP = 8
SSD_LONG_SEQS_PER_STEP = 1


def _rms(x, g):
    return x * lax.rsqrt(jnp.mean(x * x, axis=-1, keepdims=True) + NORM_EPS) * g


def _sigmoid(x):
    return 0.5 * jnp.tanh(0.5 * x) + 0.5


def _silu(x):
    return x * _sigmoid(x)


def _softplus(x):
    return jnp.maximum(x, 0.0) + jnp.log(1.0 + jnp.exp(-jnp.abs(x)))


def _bdot(a, b):
    return jnp.dot(a.astype(BF16), b.astype(BF16), preferred_element_type=F32)


def _split3(t):
    hi = t.astype(BF16)
    r1 = t - hi.astype(F32)
    mid = r1.astype(BF16)
    lo = (r1 - mid.astype(F32)).astype(BF16)
    return hi, mid, lo


def _dot01(a, b, *, exact_side):
    if exact_side == "lhs":
        m = b.astype(BF16)
        return sum(jnp.dot(p, m, preferred_element_type=F32) for p in _split3(a))
    m = a.astype(BF16)
    return sum(jnp.dot(m, p, preferred_element_type=F32) for p in _split3(b))


def _const_spec(shape):
    return pl.BlockSpec(shape, lambda *_: (0,) * len(shape), pipeline_mode=pl.Buffered(1))


def _head_expand(rows):
    h = lax.broadcasted_iota(jnp.int32, (rows, SSD_WIDTH), 0)
    c = lax.broadcasted_iota(jnp.int32, (rows, SSD_WIDTH), 1)
    return (c // SSD_HEAD_DIM == h).astype(F32)


def _proj_kernel(x_ref, g_ref, wz_ref, wx_ref, wr_ref, wdt_ref, wdtt_ref,
                 z_ref, xbc_ref, rw_ref, dt_ref, dtt_ref):
    u = _rms(x_ref[...], g_ref[...]).astype(BF16)
    z_ref[...] = jnp.dot(u, wz_ref[...], preferred_element_type=F32)
    xbc_ref[...] = jnp.dot(u, wx_ref[...], preferred_element_type=F32)
    rw_ref[...] = jnp.dot(u, wr_ref[...], preferred_element_type=F32)
    dt_ref[...] = jnp.dot(u, wdt_ref[...], preferred_element_type=F32)
    dtt_ref[...] = lax.dot_general(wdtt_ref[...], u, (((1,), (1,)), ((), ())), preferred_element_type=F32)


def in_projection(x, g, wz, wx, wr, wdt, *, tm):
    n = x.shape[0]
    row = lambda w: pl.BlockSpec((tm, w), lambda i: (i, 0))
    wdtt = wdt.T
    return pl.pallas_call(
        _proj_kernel,
        grid=(n // tm,),
        in_specs=[row(D_MODEL), _const_spec((1, D_MODEL)), _const_spec(wz.shape),
                  _const_spec(wx.shape), _const_spec(wr.shape), _const_spec(wdt.shape),
                  _const_spec(wdtt.shape)],
        out_specs=[row(SSD_WIDTH), row(SSD_CONV_DIM), row(RWKV_PROJ), row(SSD_HEADS),
                   pl.BlockSpec((SSD_HEADS, tm), lambda i: (0, i))],
        out_shape=[jax.ShapeDtypeStruct((n, SSD_WIDTH), F32),
                   jax.ShapeDtypeStruct((n, SSD_CONV_DIM), F32),
                   jax.ShapeDtypeStruct((n, RWKV_PROJ), F32),
                   jax.ShapeDtypeStruct((n, SSD_HEADS), F32),
                   jax.ShapeDtypeStruct((SSD_HEADS, n), F32)],
        compiler_params=pltpu.CompilerParams(
            dimension_semantics=("parallel",), vmem_limit_bytes=V7X_VMEM_LIMIT),
        name="in_projection",
    )(x, g, wz, wx, wr, wdt, wdtt)


def _ssd_kernel(z_ref, xbc_ref, dt_ref, dtt_ref, hist_ref, h0_ref, cw_ref, cb_ref,
                dtb_ref, dtbt_ref, alog_ref, alogt_ref, dsk_ref, nrm_ref,
                y_ref, hfin_ref, cnew_ref, xfull_scr, h_scr, *, q, nseq, single_chunk):
    refs = (z_ref, xbc_ref, dt_ref, dtt_ref, hist_ref, h0_ref, cw_ref, cb_ref, dtb_ref, dtbt_ref, alog_ref,
            alogt_ref, dsk_ref, nrm_ref, y_ref, hfin_ref, cnew_ref, xfull_scr, h_scr)
    stages = [_ssd_sequence(s, *refs, q=q, single_chunk=single_chunk) for s in range(nseq)]
    for _ in zip(*stages):
        pass


def _ssd_sequence(s, z_ref, xbc_ref, dt_ref, dtt_ref, hist_ref, h0_ref, cw_ref, cb_ref,
                  dtb_ref, dtbt_ref, alog_ref, alogt_ref, dsk_ref, nrm_ref,
                  y_ref, hfin_ref, cnew_ref, xfull_scr, h_scr, *, q, single_chunk):
    c = pl.program_id(1)
    last = pl.num_programs(1) - 1
    gw = SSD_GROUP_WIDTH

    @pl.when(c == 0)
    def _():
        xfull_scr[s,CONV_PAD - 3:CONV_PAD, :] = hist_ref[s]
        if not single_chunk:
            for g in range(SSD_GROUPS):
                h_scr[s * SSD_GROUPS + g] = h0_ref[s,g * 8:(g + 1) * 8].reshape(gw, SSD_STATE).T

    @pl.when(c > 0)
    def _():
        xfull_scr[s,CONV_PAD - 3:CONV_PAD, :] = xfull_scr[s,CONV_PAD + q - 3:CONV_PAD + q, :]

    xfull_scr[s,CONV_PAD:CONV_PAD + q, :] = xbc_ref[s]

    rows = CONV_PAD + q
    padded = xfull_scr[s]
    conv = cb_ref[...] + padded[CONV_PAD:] * cw_ref[SSD_CONV - 1:SSD_CONV, :]
    for j in range(SSD_CONV - 1):
        lo = CONV_PAD - 3 + j
        conv = conv + pltpu.roll(padded, rows - lo, axis=0)[:q] * cw_ref[j:j + 1, :]
    act = _silu(conv)
    xs = act[:, :SSD_WIDTH]
    yield

    dt = _softplus(dt_ref[s] + dtb_ref[...])
    dtt_raw = dtt_ref[...] if len(dtt_ref.shape) == 2 else dtt_ref[s]
    dtt = _softplus(dtt_raw + dtbt_ref[...])
    da = dt * -jnp.exp(alog_ref[...])
    dat = dtt * -jnp.exp(alogt_ref[...])
    row = lax.broadcasted_iota(jnp.int32, (q, q), 0)
    col = lax.broadcasted_iota(jnp.int32, (q, q), 1)
    causal = row >= col
    a_cum = _dot01(causal.astype(F32), da, exact_side="rhs")
    a_cumt = _dot01(dat, (row <= col).astype(F32), exact_side="lhs")
    yield

    expand = _head_expand(SSD_HEADS)
    decay_in_x = _dot01(jnp.exp(a_cum), expand, exact_side="lhs")
    chunk_decay_x = decay_in_x[q - 1:q, :]
    xd = xs * _dot01(jnp.exp(a_cum[q - 1:q, :] - a_cum) * dt, expand, exact_side="lhs")
    yield

    ys = []
    for g in range(SSD_GROUPS):
        bm = act[:, SSD_WIDTH + g * SSD_STATE:SSD_WIDTH + (g + 1) * SSD_STATE]
        cm = act[:, SSD_WIDTH + SSD_BC + g * SSD_STATE:SSD_WIDTH + SSD_BC + (g + 1) * SSD_STATE]
        cb = lax.dot_general(cm.astype(BF16), bm.astype(BF16), (((1,), (1,)), ((), ())),
                             preferred_element_type=F32)
        yield
        def head_weights(h):
            seg = a_cum[:, h:h + 1] - a_cumt[h:h + 1, :]
            lmat = jnp.where(causal, jnp.exp(jnp.where(causal, seg, 0.0)), 0.0)
            return (cb * lmat * dtt[h:h + 1, :]).astype(BF16)

        y_parts = []
        for e in range(8):
            h = g * 8 + e
            y_parts.append(jnp.dot(head_weights(h), xs[:, h * SSD_HEAD_DIM:(h + 1) * SSD_HEAD_DIM].astype(BF16),
                                   preferred_element_type=F32))
            yield
        y_diag = jnp.concatenate(y_parts, axis=1)
        sl = slice(g * gw, (g + 1) * gw)
        if single_chunk:
            h_in = h0_ref[s,g * 8:(g + 1) * 8].reshape(gw, SSD_STATE)
            y_off = lax.dot_general(cm.astype(BF16), h_in.astype(BF16), (((1,), (1,)), ((), ())),
                                    preferred_element_type=F32)
            upd = lax.dot_general(xd[:, sl].astype(BF16), bm.astype(BF16), (((0,), (0,)), ((), ())),
                                  preferred_element_type=F32)
            head_decay = jnp.broadcast_to(jnp.exp(a_cumt[:, q - 1:q]), (SSD_HEADS, SSD_STATE))
            for e in range(8):
                h = g * 8 + e
                hfin_ref[s,h] = (h0_ref[s,h] * head_decay[h:h + 1, :]
                                  + upd[e * SSD_HEAD_DIM:(e + 1) * SSD_HEAD_DIM, :])
        else:
            h_in = h_scr[s * SSD_GROUPS + g]
            y_off = _bdot(cm, h_in)
            upd = lax.dot_general(bm.astype(BF16), xd[:, sl].astype(BF16), (((0,), (0,)), ((), ())),
                                  preferred_element_type=F32)
            h_scr[s * SSD_GROUPS + g] = h_in * chunk_decay_x[:, sl] + upd
        ys.append(y_diag + y_off * decay_in_x[:, sl])
        yield

    y = jnp.concatenate(ys, axis=1) + dsk_ref[...] * xs
    yg = y * _silu(z_ref[s])
    outs = []
    for g in range(SSD_GROUPS):
        t = yg[:, g * gw:(g + 1) * gw]
        outs.append(t * lax.rsqrt(jnp.mean(t * t, axis=-1, keepdims=True) + NORM_EPS))
    y_ref[s] = jnp.concatenate(outs, axis=1) * nrm_ref[...]

    @pl.when(c == last)
    def _():
        cnew_ref[s] = xfull_scr[s,CONV_PAD + q - 3:CONV_PAD + q, :]
        if not single_chunk:
            for g in range(SSD_GROUPS):
                hfin_ref[s,g * 8:(g + 1) * 8] = h_scr[s * SSD_GROUPS + g].T.reshape(8, SSD_HEAD_DIM, SSD_STATE)

    yield


def ssd_mixer(z, xbc, dt, dtt_flat, conv0, ssm0, w, *, q):
    b, l, _ = z.shape
    single_chunk = l == q
    want = SSD_SEQS_PER_STEP if single_chunk else SSD_LONG_SEQS_PER_STEP
    nseq = want if b % want == 0 else 1
    if q % 128 == 0 and nseq == 1:
        dtt = dtt_flat
        dtt_spec = pl.BlockSpec((SSD_HEADS, q), lambda i, c: (0, i * (l // q) + c))
    else:
        dtt = jnp.swapaxes(dt, 1, 2)
        dtt_spec = pl.BlockSpec((nseq, SSD_HEADS, q), lambda i, c: (i, 0, c))
    seq = lambda wd: pl.BlockSpec((nseq, q, wd), lambda i, c: (i, c, 0))
    per_b3 = lambda s: pl.BlockSpec((nseq,) + s, lambda i, c: (i,) + (0,) * len(s))
    col = lambda t: t.reshape(-1, 1)
    rowv = lambda t: t.reshape(1, -1)
    consts = [w["conv_w"], rowv(w["conv_b"]), rowv(w["dt_bias"]), col(w["dt_bias"]),
              rowv(w["a_log"]), col(w["a_log"]),
              rowv(jnp.repeat(w["d_skip"], SSD_HEAD_DIM)), rowv(w["ssd_norm"])]
    return pl.pallas_call(
        functools.partial(_ssd_kernel, q=q, nseq=nseq, single_chunk=single_chunk),
        grid=(b // nseq, l // q),
        in_specs=[seq(SSD_WIDTH), seq(SSD_CONV_DIM), seq(SSD_HEADS),
                  dtt_spec,
                  per_b3((SSD_CONV - 1, SSD_CONV_DIM)),
                  per_b3((SSD_HEADS, SSD_HEAD_DIM, SSD_STATE))]
                 + [_const_spec(t.shape) for t in consts],
        out_specs=[seq(SSD_WIDTH), per_b3((SSD_HEADS, SSD_HEAD_DIM, SSD_STATE)),
                   per_b3((SSD_CONV - 1, SSD_CONV_DIM))],
        out_shape=[jax.ShapeDtypeStruct((b, l, SSD_WIDTH), F32),
                   jax.ShapeDtypeStruct((b, SSD_HEADS, SSD_HEAD_DIM, SSD_STATE), F32),
                   jax.ShapeDtypeStruct((b, SSD_CONV - 1, SSD_CONV_DIM), F32)],
        scratch_shapes=[pltpu.VMEM((nseq, CONV_PAD + q, SSD_CONV_DIM), F32),
                        pltpu.VMEM((nseq * SSD_GROUPS, SSD_STATE, SSD_GROUP_WIDTH), F32)],
        compiler_params=pltpu.CompilerParams(
            dimension_semantics=("parallel", "arbitrary"), vmem_limit_bytes=V7X_VMEM_LIMIT),
        name="ssd_mixer",
    )(z, xbc, dt, dtt, conv0, ssm0, *consts)


def _head_sums(t):
    pair = 2 * RWKV_HEAD_DIM
    first = lax.broadcasted_iota(jnp.int32, (1, pair), 1) < RWKV_HEAD_DIM
    pieces = []
    for j in range(RWKV_HEADS // 2):
        x = t[:, j * pair:(j + 1) * pair]
        x0 = jnp.where(first, x, 0.0)
        s0 = jnp.sum(x0, axis=-1, keepdims=True)
        s1 = jnp.sum(x - x0, axis=-1, keepdims=True)
        pieces.append(jnp.where(first, s0, s1))
    return jnp.concatenate(pieces, axis=1)


def _rwkv_prep_kernel(rw_ref, sh0_ref, mu_ref, w0_ref, w2_ref, a0_ref, a2_ref, g2_ref,
                      kk_ref, ka_ref, rk_ref,
                      *rest, tt, seqs, chunk):
    outs, full_scr = rest[:-1], rest[-1]
    shn_ref = outs[-1]
    c = pl.program_id(1)
    l = tt // seqs

    if seqs == 1:
        @pl.when(c == 0)
        def _():
            full_scr[0, CONV_PAD - 1:CONV_PAD, :] = sh0_ref[0]

        @pl.when(c > 0)
        def _():
            full_scr[0, CONV_PAD - 1:CONV_PAD, :] = full_scr[0, CONV_PAD + tt - 1:CONV_PAD + tt, :]

        rw = rw_ref[0]
        full_scr[0, CONV_PAD:CONV_PAD + tt, :] = rw
        prev = full_scr[0, CONV_PAD - 1:CONV_PAD - 1 + tt, :]
    else:
        full_scr[:, CONV_PAD - 1:CONV_PAD, :] = sh0_ref[...]
        full_scr[:, CONV_PAD:CONV_PAD + l, :] = rw_ref[...]
        rw = rw_ref[...].reshape(tt, RWKV_PROJ)
        prev = full_scr[:, CONV_PAD - 1:CONV_PAD - 1 + l, :].reshape(tt, RWKV_PROJ)
    vals = _rwkv_mix_math(rw, prev, mu_ref, w0_ref, w2_ref, a0_ref, a2_ref, g2_ref, kk_ref, ka_ref, rk_ref)
    _emit_rwkv_outputs(outs[:-1], vals, tt=tt, chunk=chunk)

    if seqs == 1:
        @pl.when(c == pl.num_programs(1) - 1)
        def _():
            shn_ref[0] = full_scr[0, CONV_PAD + tt - 1:CONV_PAD + tt, :]
    else:
        shn_ref[...] = rw_ref[:, l - 1:l, :]


def _rwkv_mix_math(rw, prev, mu_ref, w0_ref, w2_ref, a0_ref, a2_ref, g2_ref, kk_ref, ka_ref, rk_ref):
    wd = RWKV_WIDTH
    u = rw + (prev - rw) * mu_ref[...]
    r = u[:, :wd]
    k = u[:, wd:2 * wd]
    v = u[:, 2 * wd:3 * wd]
    w_lo = u[:, 3 * wd:3 * wd + DECAY_LORA]
    a_lo = u[:, 3 * wd + DECAY_LORA:3 * wd + DECAY_LORA + AAA_LORA]
    g_lo = u[:, 3 * wd + DECAY_LORA + AAA_LORA:]

    w_log = -_softplus(-(w0_ref[...] + _bdot(jnp.tanh(w_lo), w2_ref[...]))) - 0.5
    lw = -jnp.exp(w_log)
    a = _sigmoid(a0_ref[...] + _bdot(a_lo, a2_ref[...]))
    gate = _bdot(_sigmoid(g_lo), g2_ref[...])

    kk = k * kk_ref[...]
    kk = kk / jnp.maximum(jnp.sqrt(_head_sums(kk * kk)), 1e-12)
    kf = k * (1.0 + (a - 1.0) * ka_ref[...])
    kb = kk * a
    bonus = _head_sums(r * kf * rk_ref[...]) * v
    return r, lw, kf, v, kk, kb, gate, bonus


def _emit_rwkv_outputs(outs, vals, *, tt, chunk, row0=0):
    r, lw, kf, v, kk, kb, gate, bonus = vals
    wd = RWKV_WIDTH
    gate_out, bonus_out = outs[-2:]
    if chunk == 0:
        blk = gate_out.shape
        for ref, val in zip(outs, (r, jnp.exp(lw), kf, v, kk, kb, gate, bonus)):
            ref[...] = val.reshape(blk)
    else:
        gate_out[0, row0:row0 + tt, :] = gate
        bonus_out[0, row0:row0 + tt, :] = bonus
        kkt_out, rt_out, kfh_out, nbh_out, vb_out, kbg_out, kfg_out, gend_out = outs[:8]
        tri = (lax.broadcasted_iota(jnp.int32, (chunk, chunk), 0)
               >= lax.broadcasted_iota(jnp.int32, (chunk, chunk), 1)).astype(F32)
        for ci in range(tt // chunk):
            rs = slice(ci * chunk, (ci + 1) * chunk)
            ro = slice(row0 + ci * chunk, row0 + (ci + 1) * chunk)
            lw_c = lw[rs]
            cum = _dot01(tri, lw_c, exact_side="rhs")
            cum_end = cum[chunk - 1:chunk, :]
            g_inv = jnp.exp(-cum)
            g_tail = jnp.exp(cum_end - cum)
            kkt_out[0, ro, :] = (kk[rs] * jnp.exp(cum - lw_c)).astype(BF16)
            rt_out[0, ro, :] = (r[rs] * jnp.exp(cum)).astype(BF16)
            kfh_out[0, ro, :] = (kf[rs] * g_tail).astype(BF16)
            nbh_out[0, ro, :] = (-kb[rs] * g_tail).astype(BF16)
            vb_out[0, ro, :] = v[rs].astype(BF16)
            kbg_out[0, ro, :] = (kb[rs] * g_inv).astype(BF16)
            kfg_out[0, ro, :] = (kf[rs] * g_inv).astype(BF16)
            gend_out[0, row0 // chunk + ci] = jnp.broadcast_to(jnp.exp(cum_end), (8, wd))


def rwkv_prep(rw, shift0, w, *, tt, chunk):
    b, l, _ = rw.shape
    seqs = max(1, tt // l)
    assert chunk == 0 or (seqs == 1 and tt % chunk == 0)
    rowv = lambda t: t.reshape(1, -1)
    consts = [rowv(w["shift_mu"]), rowv(w["w0"]), w["w2"].astype(BF16), rowv(w["a0"]),
              w["a2"].astype(BF16), w["g2"].astype(BF16), rowv(w["k_k"]), rowv(w["k_a"]),
              rowv(w["r_k"])]
    rows = tt // seqs
    grid = (b // seqs, l // rows)
    seq = lambda wd: pl.BlockSpec((seqs, rows, wd), lambda i, c: (i, c, 0))
    one = pl.BlockSpec((seqs, 1, RWKV_PROJ), lambda i, c: (i, 0, 0))
    sds = jax.ShapeDtypeStruct
    f32_seq = sds((b, l, RWKV_WIDTH), F32)
    if chunk == 0:
        op_specs = [seq(RWKV_WIDTH)] * 6
        op_shapes = [f32_seq] * 6
    else:
        per_tile = tt // chunk
        op_specs = [seq(RWKV_WIDTH)] * 7 + [
            pl.BlockSpec((1, per_tile, 8, RWKV_WIDTH), lambda i, c: (i, c, 0, 0))]
        op_shapes = [sds((b, l, RWKV_WIDTH), BF16)] * 7 + [sds((b, l // chunk, 8, RWKV_WIDTH), F32)]
    outs = pl.pallas_call(
        functools.partial(_rwkv_prep_kernel, tt=tt, seqs=seqs, chunk=chunk),
        grid=grid,
        in_specs=[seq(RWKV_PROJ), one] + [_const_spec(t.shape) for t in consts],
        out_specs=op_specs + [seq(RWKV_WIDTH)] * 2 + [one],
        out_shape=op_shapes + [f32_seq] * 2 + [sds((b, 1, RWKV_PROJ), F32)],
        scratch_shapes=[pltpu.VMEM((seqs, CONV_PAD + rows, RWKV_PROJ), F32)],
        compiler_params=pltpu.CompilerParams(
            dimension_semantics=("parallel", "arbitrary"), vmem_limit_bytes=V7X_VMEM_LIMIT),
        name="rwkv_prep",
    )(rw, shift0, *consts)
    return outs


def _proj_prep_kernel(x_ref, sh0_ref, g_ref, wz_ref, wx_ref, wr_ref, wdt_ref, wdtt_ref,
                      mu_ref, w0_ref, w2_ref, a0_ref, a2_ref, g2_ref, kk_ref, ka_ref, rk_ref,
                      z_ref, xbc_ref, dt_ref, dtt_ref, *rest, tm, tiles_per_seq, chunk):
    outs, (new_scr, cur_scr) = rest[:-2], rest[-2:]
    shn_ref = outs[-1]
    i = pl.program_id(0)

    @pl.when(i == 0)
    def _():
        new_scr[...] = jnp.zeros(new_scr.shape, F32)
        cur_scr[...] = jnp.zeros(cur_scr.shape, F32)

    k = jnp.maximum(i - 1, 0)
    first = (k % tiles_per_seq) == 0
    cur_scr[CONV_PAD - 1:CONV_PAD, :] = jnp.where(first, sh0_ref[0], cur_scr[CONV_PAD + tm - 1:CONV_PAD + tm, :])
    cur_scr[CONV_PAD:CONV_PAD + tm, :] = new_scr[...]

    shn_ref[0] = cur_scr[CONV_PAD + tm - 1:CONV_PAD + tm, :]
    u = _rms(x_ref[...], g_ref[...]).astype(BF16)

    def project(piece):
        if piece == 0:
            z_ref[...] = jnp.dot(u, wz_ref[...], preferred_element_type=F32)
        elif piece == 1:
            xbc_ref[...] = jnp.dot(u, wx_ref[...], preferred_element_type=F32)
        elif piece == 2:
            cols = slice(0, 2 * RWKV_WIDTH)
            new_scr[:, cols] = jnp.dot(u, wr_ref[:, cols], preferred_element_type=F32)
        else:
            cols = slice(2 * RWKV_WIDTH, RWKV_PROJ)
            new_scr[:, cols] = jnp.dot(u, wr_ref[:, cols], preferred_element_type=F32)
            dt_ref[...] = jnp.dot(u, wdt_ref[...], preferred_element_type=F32)
            dtt_ref[...] = lax.dot_general(wdtt_ref[...], u, (((1,), (1,)), ((), ())),
                                           preferred_element_type=F32)

    def prepare(part, rows):
        lo = CONV_PAD + part * rows
        padded = cur_scr[lo - CONV_PAD:lo + rows, :]
        rw = padded[CONV_PAD:]
        prev = pltpu.roll(padded, 1, axis=0)[CONV_PAD:]
        vals = _rwkv_mix_math(rw, prev, mu_ref, w0_ref, w2_ref, a0_ref, a2_ref, g2_ref, kk_ref, ka_ref, rk_ref)
        _emit_rwkv_outputs(outs[:-1], vals, tt=rows, chunk=chunk, row0=part * rows)

    pieces = 4
    parts = min(pieces, tm // chunk)
    for part in range(parts):
        for piece in range(part * pieces // parts, (part + 1) * pieces // parts):
            project(piece)
        prepare(part, tm // parts)


def proj_prep(x, shift0, g, wz, wx, wr, wdt, w, *, tm, chunk):
    b, l, _ = x.shape
    n = b * l
    nt, tps = n // tm, l // tm
    assert l % tm == 0 and tm % chunk == 0
    x2 = x.reshape(n, D_MODEL)
    rowv = lambda t: t.reshape(1, -1)
    wdtt = wdt.T
    consts = [g, wz, wx, wr, wdt, wdtt,
              rowv(w["shift_mu"]), rowv(w["w0"]), w["w2"].astype(BF16), rowv(w["a0"]),
              w["a2"].astype(BF16), w["g2"].astype(BF16), rowv(w["k_k"]), rowv(w["k_a"]), rowv(w["r_k"])]
    ahead = lambda i: jnp.minimum(i, nt - 1)
    behind = lambda i: jnp.maximum(i - 1, 0)
    row_a = lambda w_: pl.BlockSpec((tm, w_), lambda i: (ahead(i), 0))
    seq_b = lambda w_: pl.BlockSpec((1, tm, w_), lambda i: (behind(i) // tps, behind(i) % tps, 0))
    one_b = pl.BlockSpec((1, 1, RWKV_PROJ), lambda i: (behind(i) // tps, 0, 0))
    per_tile = tm // chunk
    gend_spec = pl.BlockSpec((1, per_tile, 8, RWKV_WIDTH), lambda i: (behind(i) // tps, behind(i) % tps, 0, 0))
    sds = jax.ShapeDtypeStruct
    outs = pl.pallas_call(
        functools.partial(_proj_prep_kernel, tm=tm, tiles_per_seq=tps, chunk=chunk),
        grid=(nt + 1,),
        in_specs=[row_a(D_MODEL), one_b] + [_const_spec(t.shape) for t in consts],
        out_specs=[row_a(SSD_WIDTH), row_a(SSD_CONV_DIM), row_a(SSD_HEADS),
                   pl.BlockSpec((SSD_HEADS, tm), lambda i: (0, ahead(i)))]
                  + [seq_b(RWKV_WIDTH)] * 7 + [gend_spec] + [seq_b(RWKV_WIDTH)] * 2 + [one_b],
        out_shape=[sds((n, SSD_WIDTH), F32), sds((n, SSD_CONV_DIM), F32), sds((n, SSD_HEADS), F32),
                   sds((SSD_HEADS, n), F32)]
                  + [sds((b, l, RWKV_WIDTH), BF16)] * 7 + [sds((b, l // chunk, 8, RWKV_WIDTH), F32)]
                  + [sds((b, l, RWKV_WIDTH), F32)] * 2 + [sds((b, 1, RWKV_PROJ), F32)],
        scratch_shapes=[pltpu.VMEM((tm, RWKV_PROJ), F32), pltpu.VMEM((CONV_PAD + tm, RWKV_PROJ), F32)],
        compiler_params=pltpu.CompilerParams(
            dimension_semantics=("arbitrary",), vmem_limit_bytes=V7X_VMEM_LIMIT),
        name="proj_prep",
    )(x2, shift0, *consts)
    return outs


def _wkv_kernel(r_ref, w_ref, k_ref, v_ref, kk_ref, kka_ref, s0_ref,
                o_ref, sfin_ref, s_scr, vt_scr, ot_scr, *, steps):
    c = pl.program_id(1)
    n = RWKV_HEAD_DIM
    lanes = WKV_BATCH_BLOCK * RWKV_HEADS

    @pl.when(c == 0)
    def _():
        s_scr[...] = s0_ref[...].reshape(lanes, n * n).T.reshape(n, n, lanes)

    def to_pairs(ref, t):
        return ref[:, t].reshape(lanes, n).T

    def step(t, carry):
        r_t = to_pairs(r_ref, t)
        w_t = to_pairs(w_ref, t)
        k_t = to_pairs(k_ref, t)
        kk_t = to_pairs(kk_ref, t)
        kka_t = to_pairs(kka_ref, t)
        vt_scr[...] = to_pairs(v_ref, t)

        def per_value(vi, carry2):
            s_v = s_scr[vi]
            skk = jnp.sum(s_v * kk_t, axis=0, keepdims=True)
            v_row = vt_scr[pl.ds(vi, 1), :]
            s_new = s_v * w_t - skk * kka_t + v_row * k_t
            s_scr[vi] = s_new
            ot_scr[pl.ds(vi, 1), :] = jnp.sum(s_new * r_t, axis=0, keepdims=True)
            return carry2

        lax.fori_loop(0, n, per_value, 0, unroll=4)
        o_ref[:, t] = ot_scr[...].T.reshape(WKV_BATCH_BLOCK, RWKV_HEADS, n)
        return carry

    lax.fori_loop(0, steps, step, 0)

    @pl.when(c == pl.num_programs(1) - 1)
    def _():
        sfin_ref[...] = s_scr[...].reshape(n * n, lanes).T.reshape(
            WKV_BATCH_BLOCK, RWKV_HEADS, n, n)


def wkv_scan(r, w, k, v, kk, kka, s0, *, steps):
    b, l, _ = r.shape
    h, n = RWKV_HEADS, RWKV_HEAD_DIM
    assert b % WKV_BATCH_BLOCK == 0 and l % steps == 0
    ops = [t.reshape(b, l, h, n) for t in (r, w, k, v, kk, kka)]
    seq_spec = pl.BlockSpec((WKV_BATCH_BLOCK, steps, h, n), lambda g, c: (g, c, 0, 0))
    st_spec = pl.BlockSpec((WKV_BATCH_BLOCK, h, n, n), lambda g, c: (g, 0, 0, 0))
    o, s_fin = pl.pallas_call(
        functools.partial(_wkv_kernel, steps=steps),
        grid=(b // WKV_BATCH_BLOCK, l // steps),
        in_specs=[seq_spec] * 6 + [st_spec],
        out_specs=[seq_spec, st_spec],
        out_shape=[jax.ShapeDtypeStruct((b, l, h, n), F32),
                   jax.ShapeDtypeStruct((b, h, n, n), F32)],
        scratch_shapes=[pltpu.VMEM((n, n, WKV_BATCH_BLOCK * h), F32),
                        pltpu.VMEM((n, WKV_BATCH_BLOCK * h), F32),
                        pltpu.VMEM((n, WKV_BATCH_BLOCK * h), F32)],
        compiler_params=pltpu.CompilerParams(
            dimension_semantics=("parallel", "arbitrary"), vmem_limit_bytes=V7X_VMEM_LIMIT),
        name="wkv_scan",
    )(*ops, s0)
    return o.reshape(b, l, h * n), s_fin


WKV_LANE_BATCH = 128


def _wkv_batch_lanes_kernel(r_ref, w_ref, k_ref, v_ref, kk_ref, kka_ref, s0_ref,
                            o_ref, sfin_ref, op_scr, ot_scr, *, steps):
    n = RWKV_HEAD_DIM
    nb = WKV_LANE_BATCH
    sfin_ref[...] = s0_ref[...]

    def step(t, carry):
        rows = pl.ds(t, nb, stride=steps)
        for i, ref in enumerate((r_ref, w_ref, k_ref, kk_ref, kka_ref, v_ref)):
            op_scr[i] = ref[rows, :].T
        for h2 in range(2):
            ch = slice(h2 * n, (h2 + 1) * n)
            r_t = op_scr[0, ch, :]
            kka_r = jnp.sum(op_scr[4, ch, :] * r_t, axis=0, keepdims=True)
            k_r = jnp.sum(op_scr[2, ch, :] * r_t, axis=0, keepdims=True)
            op_scr[0, ch, :] = op_scr[1, ch, :] * r_t

            def per_value(vi, carry2, ch=ch, h2=h2, kka_r=kka_r, k_r=k_r):
                s_v = sfin_ref[h2, vi]
                skk = jnp.sum(s_v * op_scr[3, ch, :], axis=0, keepdims=True)
                out = jnp.sum(s_v * op_scr[0, ch, :], axis=0, keepdims=True)
                v_row = op_scr[5, pl.ds(h2 * n + vi, 1), :]
                sfin_ref[h2, vi] = s_v * op_scr[1, ch, :] - skk * op_scr[4, ch, :] + v_row * op_scr[2, ch, :]
                ot_scr[pl.ds(h2 * n + vi, 1), :] = out - skk * kka_r + v_row * k_r
                return carry2

            lax.fori_loop(0, n, per_value, 0, unroll=8)
        o_ref[rows, :] = ot_scr[...].T
        return carry

    lax.fori_loop(0, steps, step, 0)


def wkv_scan_batch_lanes(r, w, k, v, kk, kka, s0):
    b, l, wd = r.shape
    h, n, nb = RWKV_HEADS, RWKV_HEAD_DIM, WKV_LANE_BATCH
    assert b % nb == 0
    ops = [t.reshape(b * l, wd) for t in (r, w, k, v, kk, kka)]
    s0t = jnp.transpose(s0, (1, 2, 3, 0))
    seq_spec = pl.BlockSpec((nb * l, 2 * n), lambda g, j: (g, j))
    st_spec = pl.BlockSpec((2, n, n, nb), lambda g, j: (j, 0, 0, g))
    o, s_fin = pl.pallas_call(
        functools.partial(_wkv_batch_lanes_kernel, steps=l),
        grid=(b // nb, h // 2),
        in_specs=[seq_spec] * 6 + [st_spec],
        out_specs=[seq_spec, st_spec],
        out_shape=[jax.ShapeDtypeStruct((b * l, wd), F32), jax.ShapeDtypeStruct((h, n, n, b), F32)],
        scratch_shapes=[pltpu.VMEM((6, 2 * n, nb), F32), pltpu.VMEM((2 * n, nb), F32)],
        compiler_params=pltpu.CompilerParams(
            dimension_semantics=("parallel", "parallel"), vmem_limit_bytes=V7X_VMEM_LIMIT),
        name="wkv_scan_batch_lanes",
    )(*ops, s0t)
    return o.reshape(b, l, wd), jnp.transpose(s_fin, (3, 0, 1, 2))


WKV_CHUNK = 64
WKV_PAIRS = RWKV_HEADS // 2
WKV_ROW_STRIDE = WKV_CHUNK + 8
WKV_SOLVE_ROWS = 8
WKV_SOLVE_COLS = 16


def _pair_masks():
    c = WKV_CHUNK
    row = lax.broadcasted_iota(jnp.int32, (2 * c, 2 * c), 0)
    col = lax.broadcasted_iota(jnp.int32, (2 * c, 2 * c), 1)
    t, i = row % c, col % c
    keep = i <= t - jnp.where(row < c, 1, 0)
    sign = jnp.where(row >= c, jnp.where(col < c, -1.0, 1.0), 1.0)
    block_diag = row // c == col // c
    return keep, sign, block_diag


def _wkv_prepare_kernel(kkt_ref, rt_ref, kbg_ref, kfg_ref, vb_ref,
                        lo_ref, rhs0_ref, tp_ref, abs_scr, top_scr, abt_scr, tt_scr):
    c = WKV_CHUNK
    keep, sign, _ = _pair_masks()
    lane = lax.broadcasted_iota(jnp.int32, (1, 2 * c), 1)
    head0 = lane < RWKV_HEAD_DIM
    row_head0 = lax.broadcasted_iota(jnp.int32, (2 * RWKV_HEAD_DIM, 1), 0) < RWKV_HEAD_DIM
    zeros = jnp.zeros((c, 2 * c), BF16)

    def per_batch(b, carry):
        for j in range(WKV_PAIRS):
            sl = slice(j * 2 * RWKV_HEAD_DIM, (j + 1) * 2 * RWKV_HEAD_DIM)
            lhs = jnp.concatenate([kkt_ref[b, :, sl], rt_ref[b, :, sl]], axis=0)
            rhs = jnp.concatenate([kbg_ref[b, :, sl], kfg_ref[b, :, sl]], axis=0)
            rhs_t = rhs.astype(F32).T
            top = jnp.where(row_head0, rhs_t, 0.0)
            w_a = jnp.concatenate([top, rhs_t - top], axis=1).astype(BF16)
            a_both = jnp.dot(lhs, w_a, preferred_element_type=F32)
            for h2 in range(2):
                a = a_both[:, h2 * 2 * c:(h2 + 1) * 2 * c]
                a = jnp.where(keep, a, 0.0) * sign
                inst = h2 * (WKV_BATCH_BLOCK * WKV_PAIRS) + b * WKV_PAIRS + j
                abs_scr[pl.ds(inst * WKV_ROW_STRIDE, c), :c] = a[:c, :c]
                top_scr[j, :, h2 * 2 * c:(h2 + 1) * 2 * c] = a[:c].astype(BF16)
                lo_ref[b, :, (2 * j + h2) * 2 * c:(2 * j + h2 + 1) * 2 * c] = a[c:].astype(BF16)

        for j in range(WKV_PAIRS):
            sl = slice(j * 2 * RWKV_HEAD_DIM, (j + 1) * 2 * RWKV_HEAD_DIM)
            v = vb_ref[b, :, sl]
            v0 = jnp.where(head0, v, jnp.zeros_like(v))
            w_akf = jnp.concatenate([zeros, v0, zeros, v - v0], axis=0)
            rhs0_ref[b, :, sl] = jnp.dot(top_scr[j], w_akf, preferred_element_type=F32)
        return carry

    lax.fori_loop(0, WKV_BATCH_BLOCK, per_batch, 0)

    n_inst = 2 * WKV_BATCH_BLOCK * WKV_PAIRS
    n_pair_rows = WKV_BATCH_BLOCK * WKV_PAIRS

    def to_lanes(t, carry):
        abt_scr[t] = abs_scr[pl.ds(t, n_inst, stride=WKV_ROW_STRIDE), :][:, :c].T
        return carry

    lax.fori_loop(0, c, to_lanes, 0, unroll=8)

    tt_scr[...] = jnp.zeros(tt_scr.shape, F32)
    sub_iota = lax.broadcasted_iota(jnp.int32, (WKV_SOLVE_COLS, n_inst), 0)
    rows = range(WKV_SOLVE_ROWS)
    for cb in range(c // WKV_SOLVE_COLS):
        col0 = WKV_SOLVE_COLS * cb
        cols = slice(col0, col0 + WKV_SOLVE_COLS)
        first_block = col0 // WKV_SOLVE_ROWS

        def solve_rows(tb, carry, col0=col0, cols=cols, first_block=first_block):
            t0 = tb * WKV_SOLVE_ROWS

            def sub(ib, accs):
                ps = [tt_scr[ib * WKV_SOLVE_ROWS + di, cols, :] for di in rows]
                out = []
                for r in rows:
                    terms = [abt_scr[t0 + r, pl.ds(ib * WKV_SOLVE_ROWS + di, 1), :] * ps[di] for di in rows]
                    while len(terms) > 1:
                        terms = [a + b for a, b in zip(terms[::2], terms[1::2])]
                    out.append(accs[r] - terms[0])
                return tuple(out)

            unit = tuple(jnp.where(sub_iota + col0 == t0 + r, 1.0, 0.0) for r in rows)
            accs = list(lax.fori_loop(first_block, tb, sub, unit))
            for r in rows:
                for r2 in range(r):
                    accs[r] = accs[r] - abt_scr[t0 + r, pl.ds(t0 + r2, 1), :] * accs[r2]
                tt_scr[t0 + r, cols, :] = accs[r]
            return carry

        lax.fori_loop(first_block, c // WKV_SOLVE_ROWS, solve_rows, 0)

    def from_lanes(t, carry):
        m = tt_scr[t].T
        abs_scr[pl.ds(t, n_pair_rows, stride=WKV_ROW_STRIDE), :] = jnp.concatenate(
            [m[:n_pair_rows], m[n_pair_rows:]], axis=1)
        return carry

    lax.fori_loop(0, c, from_lanes, 0, unroll=8)

    def emit(b, carry):
        for j in range(WKV_PAIRS):
            row0 = (b * WKV_PAIRS + j) * WKV_ROW_STRIDE
            tp_ref[b, :, j * 2 * c:(j + 1) * 2 * c] = abs_scr[pl.ds(row0, c), :].astype(BF16)
        return carry

    lax.fori_loop(0, WKV_BATCH_BLOCK, emit, 0)


def _wkv_apply_kernel(kkt_ref, rt_ref, kfh_ref, nbh_ref, vb_ref, tp_ref, lo_ref, rhs0_ref, gend_ref, s0_ref,
                      o_ref, sfin_ref, x_scr, wp_scr, p_scr):
    c = WKV_CHUNK
    n = RWKV_HEAD_DIM
    ch = pl.program_id(1)
    _, _, block_diag = _pair_masks()
    lane = lax.broadcasted_iota(jnp.int32, (1, 2 * c), 1)
    head0 = lane < n
    eye2 = (lax.broadcasted_iota(jnp.int32, (n, 2 * n), 0)
            == lax.broadcasted_iota(jnp.int32, (n, 2 * n), 1) % n).astype(F32)

    @pl.when(ch == 0)
    def _():
        def init(b, carry):
            for j in range(WKV_PAIRS):
                sp = s0_ref[b, 2 * j:2 * j + 2].reshape(2 * n, n)
                dup = jnp.dot(sp, eye2, precision=HIGHEST, preferred_element_type=F32)
                x_scr[b, j] = jnp.where(block_diag, dup, 0.0)
            return carry
        lax.fori_loop(0, WKV_BATCH_BLOCK, init, 0)

    def per_batch(b, carry):
        for j in range(WKV_PAIRS):
            sl = slice(j * 2 * n, (j + 1) * 2 * n)
            lhs = jnp.concatenate([kkt_ref[b, :, sl], rt_ref[b, :, sl]], axis=0)
            kx = lax.dot_general(lhs, x_scr[b, j].astype(BF16), (((1,), (1,)), ((), ())),
                                 preferred_element_type=F32)
            rhs = kx[:c] + rhs0_ref[b, :, sl]
            r0 = jnp.where(head0, rhs, 0.0)
            wp_scr[j] = jnp.concatenate([r0, rhs - r0], axis=0).astype(BF16)
            o_ref[b, :, sl] = kx[c:]
        for j in range(WKV_PAIRS):
            sl = slice(j * 2 * n, (j + 1) * 2 * n)
            p = jnp.dot(tp_ref[b, :, sl], wp_scr[j], preferred_element_type=F32)
            p_scr[j] = p.astype(BF16)
        for j in range(WKV_PAIRS):
            sl = slice(j * 2 * n, (j + 1) * 2 * n)
            v = vb_ref[b, :, sl]
            pb = p_scr[j]
            zero = jnp.zeros_like(pb)
            p0, v0 = jnp.where(head0, pb, zero), jnp.where(head0, v, zero)
            w_o = jnp.concatenate([p0, v0, pb - p0, v - v0], axis=0)
            o_ref[b, :, sl] = o_ref[b, :, sl] + jnp.dot(
                lo_ref[b, :, j * 4 * c:(j + 1) * 4 * c], w_o, preferred_element_type=F32)
            vp = jnp.concatenate([v, pb], axis=0)
            kb = jnp.concatenate([kfh_ref[b, :, sl], nbh_ref[b, :, sl]], axis=0)
            upd = lax.dot_general(vp, kb, (((0,), (0,)), ((), ())), preferred_element_type=F32)
            x_scr[b, j] = jnp.where(block_diag, x_scr[b, j] * gend_ref[b, 0, 0:1, sl] + upd, 0.0)
        return carry

    lax.fori_loop(0, WKV_BATCH_BLOCK, per_batch, 0)

    @pl.when(ch == pl.num_programs(1) - 1)
    def _():
        def fin(b, carry):
            for j in range(WKV_PAIRS):
                sp = lax.dot_general(x_scr[b, j], eye2, (((1,), (1,)), ((), ())),
                                     precision=HIGHEST, preferred_element_type=F32)
                sfin_ref[b, 2 * j:2 * j + 2] = sp.reshape(2, n, n)
            return carry
        lax.fori_loop(0, WKV_BATCH_BLOCK, fin, 0)


def wkv_chunked(kkt, rt, kfh, nbh, vb, kbg, kfg, gend, s0):
    b, l, wd = kkt.shape
    c = WKV_CHUNK
    assert b % WKV_BATCH_BLOCK == 0 and l % c == 0
    gb, nc = b // WKV_BATCH_BLOCK, l // c
    lanes = 2 * WKV_BATCH_BLOCK * WKV_PAIRS
    seq = lambda w_: pl.BlockSpec((WKV_BATCH_BLOCK, c, w_), lambda g, i: (g, i, 0))
    gend_spec = pl.BlockSpec((WKV_BATCH_BLOCK, 1, 8, wd), lambda g, i: (g, i, 0, 0))
    sds = jax.ShapeDtypeStruct
    st_spec = pl.BlockSpec((WKV_BATCH_BLOCK, RWKV_HEADS, RWKV_HEAD_DIM, RWKV_HEAD_DIM),
                           lambda g, i: (g, 0, 0, 0))
    blk = (WKV_BATCH_BLOCK, c, wd)
    o, s_fin = pl.pallas_call(
        _wkv_chunk_kernel,
        grid=(gb, nc),
        in_specs=[seq(wd)] * 7 + [gend_spec, st_spec],
        out_specs=[seq(wd), st_spec],
        out_shape=[sds((b, l, wd), F32), sds(s0.shape, F32)],
        scratch_shapes=[
            pltpu.VMEM((lanes * WKV_ROW_STRIDE, 2 * c), F32),
            pltpu.VMEM((WKV_PAIRS, c, 4 * c), BF16),
            pltpu.VMEM((c, c, lanes), F32),
            pltpu.VMEM((c, c, lanes), F32),
            pltpu.VMEM((WKV_BATCH_BLOCK, c, 2 * wd), BF16),
            pltpu.VMEM(blk, F32),
            pltpu.VMEM(blk, BF16),
            pltpu.VMEM((WKV_BATCH_BLOCK, WKV_PAIRS, 2 * RWKV_HEAD_DIM, 2 * RWKV_HEAD_DIM), F32),
            pltpu.VMEM((WKV_PAIRS, 2 * c, 2 * RWKV_HEAD_DIM), BF16),
            pltpu.VMEM((WKV_PAIRS, c, 2 * RWKV_HEAD_DIM), BF16)],
        compiler_params=pltpu.CompilerParams(
            dimension_semantics=("parallel", "arbitrary"), vmem_limit_bytes=V7X_VMEM_LIMIT),
        name="wkv_chunked",
    )(kkt, rt, kfh, nbh, vb, kbg, kfg, gend, s0)
    return o, s_fin


def _wkv_chunk_kernel(kkt_ref, rt_ref, kfh_ref, nbh_ref, vb_ref, kbg_ref, kfg_ref, gend_ref, s0_ref,
                      o_ref, sfin_ref, abs_scr, top_scr, abt_scr, tt_scr, lo_scr, rhs0_scr, tp_scr,
                      x_scr, wp_scr, p_scr):
    _wkv_prepare_kernel(kkt_ref, rt_ref, kbg_ref, kfg_ref, vb_ref, lo_scr, rhs0_scr, tp_scr,
                        abs_scr, top_scr, abt_scr, tt_scr)
    _wkv_apply_kernel(kkt_ref, rt_ref, kfh_ref, nbh_ref, vb_ref, tp_scr, lo_scr, rhs0_scr, gend_ref, s0_ref,
                      o_ref, sfin_ref, x_scr, wp_scr, p_scr)


def _tail_kernel(o_ref, gate_ref, bonus_ref, x_ref, ys_ref, p_ref, lnw_ref, lnb_ref, woa_ref, wob_ref,
                 nf_ref, wg_ref, wu_ref, wd_ref, np_ref, wpg_ref, wpp_ref, nl_ref, y_ref, new_scr, cur_scr):
    i = pl.program_id(0)

    @pl.when(i == 0)
    def _():
        new_scr[...] = jnp.zeros(new_scr.shape, BF16)

    cur_scr[...] = new_scr[...]

    parts = 4
    rows = o_ref.shape[0] // parts
    inv_n = 1.0 / RWKV_HEAD_DIM

    def vector_half(part):
        rs = slice(part * rows, (part + 1) * rows)
        o = o_ref[rs, :]
        mu = _head_sums(o) * inv_n
        d = o - mu
        var = _head_sums(d * d) * inv_n
        on = d * lax.rsqrt(var + GN_EPS) * lnw_ref[...] + lnb_ref[...]
        new_scr[rs, :] = ((on + bonus_ref[rs, :]) * gate_ref[rs, :]).astype(BF16)

    vector_half(0)
    y_rwkv = cur_scr[...]
    h = x_ref[...] + _bdot(ys_ref[...], woa_ref[...]) + jnp.dot(y_rwkv, wob_ref[...],
                                                                preferred_element_type=F32)
    hf = _rms(h, nf_ref[...]).astype(BF16)
    vector_half(1)
    gate = jnp.dot(hf, wg_ref[...], preferred_element_type=F32)
    up = jnp.dot(hf, wu_ref[...], preferred_element_type=F32)
    vector_half(2)
    h = h + _bdot(_silu(gate) * up, wd_ref[...])
    vector_half(3)
    pg = _sigmoid(_bdot(_rms(h, np_ref[...]), wpg_ref[...]))
    h = h + pg * _bdot(p_ref[...], wpp_ref[...])
    y_ref[...] = _rms(h, nl_ref[...])


def layer_tail(o, gate, bonus, x, y_ssd, p, consts, *, tm):
    n = x.shape[0]
    nt = n // tm
    ahead = lambda w_: pl.BlockSpec((tm, w_), lambda i: (jnp.minimum(i, nt - 1), 0))
    behind = lambda w_: pl.BlockSpec((tm, w_), lambda i: (jnp.maximum(i - 1, 0), 0))
    return pl.pallas_call(
        _tail_kernel,
        grid=(nt + 1,),
        in_specs=[ahead(D_MODEL)] * 3 + [behind(D_MODEL)] * 2 + [behind(PLE_DIM)]
                 + [_const_spec(t.shape) for t in consts],
        out_specs=behind(D_MODEL),
        out_shape=jax.ShapeDtypeStruct((n, D_MODEL), F32),
        scratch_shapes=[pltpu.VMEM((tm, D_MODEL), BF16), pltpu.VMEM((tm, D_MODEL), BF16)],
        compiler_params=pltpu.CompilerParams(
            dimension_semantics=("arbitrary",), vmem_limit_bytes=V7X_VMEM_LIMIT),
        name="layer_tail",
    )(o, gate, bonus, x, y_ssd, p, *consts)


def _prepare_weights(w):
    c0, c1, c2 = SSD_WIDTH, SSD_WIDTH + SSD_CONV_DIM, SSD_WIDTH + SSD_CONV_DIM + SSD_HEADS
    w_in = w["w_in"]
    rowv = lambda t: t.reshape(1, -1)
    return dict(
        w,
        wz=w_in[:, :c0].astype(BF16), wx=w_in[:, c0:c1].astype(BF16),
        wdt=w_in[:, c1:c2].astype(BF16), wr=w_in[:, c2:].astype(BF16),
        woa=w["w_out"][:SSD_WIDTH].astype(BF16), wob=w["w_out"][SSD_WIDTH:].astype(BF16),
        wg=w["w_gate"].astype(BF16), wu=w["w_up"].astype(BF16), wd=w["w_down"].astype(BF16),
        wpg=w["w_ple_gate"].astype(BF16), wpp=w["w_ple_proj"].astype(BF16),
        norm_mix_r=rowv(w["norm_mix"]), norm_ffn_r=rowv(w["norm_ffn"]),
        norm_ple_r=rowv(w["norm_ple"]), norm_final_r=rowv(w["norm_final"]),
        ln_x_w_r=rowv(w["ln_x_w"]), ln_x_b_r=rowv(w["ln_x_b"]),
    )


TOKEN_TILE = 256
PREP_TILE = 128
SCAN_STEPS = 16


def layer_forward(x, p, conv0, shift0, ssm0, wkv0, w):
    b, l, _ = x.shape
    n = b * l
    tm = min(TOKEN_TILE, n)
    ssd_q = SSD_CHUNK if l % SSD_CHUNK == 0 else l
    prep_tt = min(PREP_TILE, n)
    wkv_steps = min(SCAN_STEPS, l)
    x2 = x.reshape(n, D_MODEL)
    chunked = l % WKV_CHUNK == 0 and l % tm == 0
    if chunked:
        z, xbc, dt, dtt, *ops, gate, bonus, shift_new = proj_prep(
            x, shift0, w["norm_mix_r"], w["wz"], w["wx"], w["wr"], w["wdt"], w, tm=tm, chunk=WKV_CHUNK)
    else:
        z, xbc, rw, dt, dtt = in_projection(x2, w["norm_mix_r"], w["wz"], w["wx"], w["wr"], w["wdt"], tm=tm)
        *ops, gate, bonus, shift_new = rwkv_prep(rw.reshape(b, l, -1), shift0, w, tt=prep_tt, chunk=0)
    y_ssd, ssm_new, conv_new = ssd_mixer(
        z.reshape(b, l, -1), xbc.reshape(b, l, -1), dt.reshape(b, l, -1), dtt, conv0, ssm0, w, q=ssd_q)
    if chunked:
        o, wkv_new = wkv_chunked(*ops, wkv0)
    elif b % WKV_LANE_BATCH == 0:
        o, wkv_new = wkv_scan_batch_lanes(*ops, wkv0)
    else:
        o, wkv_new = wkv_scan(*ops, wkv0, steps=wkv_steps)
    flat = lambda t: t.reshape(n, -1)
    tail_consts = [w["ln_x_w_r"], w["ln_x_b_r"], w["woa"], w["wob"], w["norm_ffn_r"], w["wg"], w["wu"],
                   w["wd"], w["norm_ple_r"], w["wpg"], w["wpp"], w["norm_final_r"]]
    y = layer_tail(flat(o), flat(gate), flat(bonus), x2, flat(y_ssd), p.reshape(n, PLE_DIM),
                   tail_consts, tm=tm)
    return y.reshape(b, l, D_MODEL), ssm_new, conv_new, wkv_new, shift_new


def kernel(x_prompt, x_sample, state_ssm, state_conv, state_wkv, state_shift, p_prompt, p_sample, norm_mix, w_in, conv_w, conv_b, dt_bias, a_log, d_skip, ssd_norm, shift_mu, w0, w2, a0, a2, g2, k_k, k_a, r_k, ln_x_w, ln_x_b, w_out, norm_ffn, w_gate, w_up, w_down, norm_ple, w_ple_gate, w_ple_proj, norm_final):
    w = _prepare_weights(dict(
        norm_mix=norm_mix[0], w_in=w_in[0], conv_w=conv_w[0], conv_b=conv_b[0], dt_bias=dt_bias[0],
        a_log=a_log[0], d_skip=d_skip[0], ssd_norm=ssd_norm[0], shift_mu=shift_mu[0], w0=w0[0],
        w2=w2[0], a0=a0[0], a2=a2[0], g2=g2[0], k_k=k_k[0], k_a=k_a[0], r_k=r_k[0],
        ln_x_w=ln_x_w[0], ln_x_b=ln_x_b[0], w_out=w_out[0], norm_ffn=norm_ffn[0],
        w_gate=w_gate[0], w_up=w_up[0], w_down=w_down[0], norm_ple=norm_ple[0],
        w_ple_gate=w_ple_gate[0], w_ple_proj=w_ple_proj[0], norm_final=norm_final))
    bp = x_prompt.shape[0]
    zeros = lambda *s: jnp.zeros(s, F32)
    yp, s1, c1, k1, t1 = layer_forward(
        x_prompt, p_prompt[0], zeros(bp, SSD_CONV - 1, SSD_CONV_DIM), zeros(bp, 1, RWKV_PROJ),
        zeros(bp, SSD_HEADS, SSD_HEAD_DIM, SSD_STATE),
        zeros(bp, RWKV_HEADS, RWKV_HEAD_DIM, RWKV_HEAD_DIM), w)
    ys, s2, c2, k2, t2 = layer_forward(
        x_sample, p_sample[0], state_conv[0], state_shift[0], state_ssm[0], state_wkv[0], w)
    return (yp, ys, s1[None], c1[None], k1[None], t1[None], s2[None], c2[None], k2[None], t2[None])
```

```python
import functools

import jax
import jax.numpy as jnp
from jax import lax
from jax.experimental import pallas as pl
from jax.experimental.pallas import tpu as pltpu

F32 = jnp.float32
BF16 = jnp.bfloat16
HIGHEST = lax.Precision.HIGHEST

D_MODEL = 1024
SSD_WIDTH = 1024
SSD_HEADS = 16
SSD_HEAD_DIM = 64
SSD_GROUPS = 2
SSD_GROUP_WIDTH = SSD_WIDTH // SSD_GROUPS
SSD_STATE = 128
SSD_CONV = 4
SSD_CHUNK = 128
SSD_BC = SSD_GROUPS * SSD_STATE
SSD_CONV_DIM = SSD_WIDTH + 2 * SSD_BC
RWKV_WIDTH = 1024
RWKV_HEADS = 16
RWKV_HEAD_DIM = 64
DECAY_LORA = 64
AAA_LORA = 64
GATE_LORA = 128
RWKV_PROJ = 3 * RWKV_WIDTH + DECAY_LORA + AAA_LORA + GATE_LORA
PLE_DIM = 256
NORM_EPS = 1e-6
GN_EPS = 64e-5

WKV_BATCH_BLOCK = 8
V7X_VMEM_LIMIT = 56 * 1024 * 1024
CONV_PAD = 8
SSD_SEQS_PER_STEP = 8
SSD_LONG_SEQS_PER_STEP = 1


def _rms(x, g):
    return x * lax.rsqrt(jnp.mean(x * x, axis=-1, keepdims=True) + NORM_EPS) * g


def _sigmoid(x):
    return 0.5 * jnp.tanh(0.5 * x) + 0.5


def _silu(x):
    return x * _sigmoid(x)


def _softplus(x):
    return jnp.maximum(x, 0.0) + jnp.log(1.0 + jnp.exp(-jnp.abs(x)))


def _bdot(a, b):
    return jnp.dot(a.astype(BF16), b.astype(BF16), preferred_element_type=F32)


def _split3(t):
    hi = t.astype(BF16)
    r1 = t - hi.astype(F32)
    mid = r1.astype(BF16)
    lo = (r1 - mid.astype(F32)).astype(BF16)
    return hi, mid, lo


def _dot01(a, b, *, exact_side):
    if exact_side == "lhs":
        m = b.astype(BF16)
        return sum(jnp.dot(p, m, preferred_element_type=F32) for p in _split3(a))
    m = a.astype(BF16)
    return sum(jnp.dot(m, p, preferred_element_type=F32) for p in _split3(b))


def _const_spec(shape):
    return pl.BlockSpec(shape, lambda *_: (0,) * len(shape), pipeline_mode=pl.Buffered(1))


def _head_expand(rows):
    h = lax.broadcasted_iota(jnp.int32, (rows, SSD_WIDTH), 0)
    c = lax.broadcasted_iota(jnp.int32, (rows, SSD_WIDTH), 1)
    return (c // SSD_HEAD_DIM == h).astype(F32)


def _proj_kernel(x_ref, g_ref, wz_ref, wx_ref, wr_ref, wdt_ref, wdtt_ref,
                 z_ref, xbc_ref, rw_ref, dt_ref, dtt_ref):
    u = _rms(x_ref[...], g_ref[...]).astype(BF16)
    z_ref[...] = jnp.dot(u, wz_ref[...], preferred_element_type=F32)
    xbc_ref[...] = jnp.dot(u, wx_ref[...], preferred_element_type=F32)
    rw_ref[...] = jnp.dot(u, wr_ref[...], preferred_element_type=F32)
    dt_ref[...] = jnp.dot(u, wdt_ref[...], preferred_element_type=F32)
    dtt_ref[...] = lax.dot_general(wdtt_ref[...], u, (((1,), (1,)), ((), ())), preferred_element_type=F32)


def in_projection(x, g, wz, wx, wr, wdt, *, tm):
    n = x.shape[0]
    row = lambda w: pl.BlockSpec((tm, w), lambda i: (i, 0))
    wdtt = wdt.T
    return pl.pallas_call(
        _proj_kernel,
        grid=(n // tm,),
        in_specs=[row(D_MODEL), _const_spec((1, D_MODEL)), _const_spec(wz.shape),
                  _const_spec(wx.shape), _const_spec(wr.shape), _const_spec(wdt.shape),
                  _const_spec(wdtt.shape)],
        out_specs=[row(SSD_WIDTH), row(SSD_CONV_DIM), row(RWKV_PROJ), row(SSD_HEADS),
                   pl.BlockSpec((SSD_HEADS, tm), lambda i: (0, i))],
        out_shape=[jax.ShapeDtypeStruct((n, SSD_WIDTH), F32),
                   jax.ShapeDtypeStruct((n, SSD_CONV_DIM), F32),
                   jax.ShapeDtypeStruct((n, RWKV_PROJ), F32),
                   jax.ShapeDtypeStruct((n, SSD_HEADS), F32),
                   jax.ShapeDtypeStruct((SSD_HEADS, n), F32)],
        compiler_params=pltpu.CompilerParams(
            dimension_semantics=("parallel",), vmem_limit_bytes=V7X_VMEM_LIMIT),
        name="in_projection",
    )(x, g, wz, wx, wr, wdt, wdtt)


def _ssd_kernel(z_ref, xbc_ref, dt_ref, dtt_ref, hist_ref, h0_ref, cw_ref, cb_ref,
                dtb_ref, dtbt_ref, alog_ref, alogt_ref, dsk_ref, nrm_ref,
                y_ref, hfin_ref, cnew_ref, xfull_scr, h_scr, *, q, nseq, single_chunk):
    refs = (z_ref, xbc_ref, dt_ref, dtt_ref, hist_ref, h0_ref, cw_ref, cb_ref, dtb_ref, dtbt_ref, alog_ref,
            alogt_ref, dsk_ref, nrm_ref, y_ref, hfin_ref, cnew_ref, xfull_scr, h_scr)
    stages = [_ssd_sequence(s, *refs, q=q, single_chunk=single_chunk) for s in range(nseq)]
    for _ in zip(*stages):
        pass


def _ssd_sequence(s, z_ref, xbc_ref, dt_ref, dtt_ref, hist_ref, h0_ref, cw_ref, cb_ref,
                  dtb_ref, dtbt_ref, alog_ref, alogt_ref, dsk_ref, nrm_ref,
                  y_ref, hfin_ref, cnew_ref, xfull_scr, h_scr, *, q, single_chunk):
    c = pl.program_id(1)
    last = pl.num_programs(1) - 1
    gw = SSD_GROUP_WIDTH

    @pl.when(c == 0)
    def _():
        xfull_scr[s,CONV_PAD - 3:CONV_PAD, :] = hist_ref[s]
        if not single_chunk:
            for g in range(SSD_GROUPS):
                h_scr[s * SSD_GROUPS + g] = h0_ref[s,g * 8:(g + 1) * 8].reshape(gw, SSD_STATE).T

    @pl.when(c > 0)
    def _():
        xfull_scr[s,CONV_PAD - 3:CONV_PAD, :] = xfull_scr[s,CONV_PAD + q - 3:CONV_PAD + q, :]

    xfull_scr[s,CONV_PAD:CONV_PAD + q, :] = xbc_ref[s]

    rows = CONV_PAD + q
    padded = xfull_scr[s]
    conv = cb_ref[...] + padded[CONV_PAD:] * cw_ref[SSD_CONV - 1:SSD_CONV, :]
    for j in range(SSD_CONV - 1):
        lo = CONV_PAD - 3 + j
        conv = conv + pltpu.roll(padded, rows - lo, axis=0)[:q] * cw_ref[j:j + 1, :]
    act = _silu(conv)
    xs = act[:, :SSD_WIDTH]
    yield

    dt = _softplus(dt_ref[s] + dtb_ref[...])
    dtt_raw = dtt_ref[...] if len(dtt_ref.shape) == 2 else dtt_ref[s]
    dtt = _softplus(dtt_raw + dtbt_ref[...])
    da = dt * -jnp.exp(alog_ref[...])
    dat = dtt * -jnp.exp(alogt_ref[...])
    row = lax.broadcasted_iota(jnp.int32, (q, q), 0)
    col = lax.broadcasted_iota(jnp.int32, (q, q), 1)
    causal = row >= col
    a_cum = _dot01(causal.astype(F32), da, exact_side="rhs")
    a_cumt = _dot01(dat, (row <= col).astype(F32), exact_side="lhs")
    yield

    expand = _head_expand(SSD_HEADS)
    decay_in_x = _dot01(jnp.exp(a_cum), expand, exact_side="lhs")
    chunk_decay_x = decay_in_x[q - 1:q, :]
    xd = xs * _dot01(jnp.exp(a_cum[q - 1:q, :] - a_cum) * dt, expand, exact_side="lhs")
    yield

    ys = []
    for g in range(SSD_GROUPS):
        bm = act[:, SSD_WIDTH + g * SSD_STATE:SSD_WIDTH + (g + 1) * SSD_STATE]
        cm = act[:, SSD_WIDTH + SSD_BC + g * SSD_STATE:SSD_WIDTH + SSD_BC + (g + 1) * SSD_STATE]
        cb = lax.dot_general(cm.astype(BF16), bm.astype(BF16), (((1,), (1,)), ((), ())),
                             preferred_element_type=F32)
        yield
        def head_weights(h):
            seg = a_cum[:, h:h + 1] - a_cumt[h:h + 1, :]
            lmat = jnp.where(causal, jnp.exp(jnp.where(causal, seg, 0.0)), 0.0)
            return (cb * lmat * dtt[h:h + 1, :]).astype(BF16)

        y_parts = []
        for e in range(8):
            h = g * 8 + e
            y_parts.append(jnp.dot(head_weights(h), xs[:, h * SSD_HEAD_DIM:(h + 1) * SSD_HEAD_DIM].astype(BF16),
                                   preferred_element_type=F32))
            yield
        y_diag = jnp.concatenate(y_parts, axis=1)
        sl = slice(g * gw, (g + 1) * gw)
        if single_chunk:
            h_in = h0_ref[s,g * 8:(g + 1) * 8].reshape(gw, SSD_STATE)
            y_off = lax.dot_general(cm.astype(BF16), h_in.astype(BF16), (((1,), (1,)), ((), ())),
                                    preferred_element_type=F32)
            upd = lax.dot_general(xd[:, sl].astype(BF16), bm.astype(BF16), (((0,), (0,)), ((), ())),
                                  preferred_element_type=F32)
            head_decay = jnp.broadcast_to(jnp.exp(a_cumt[:, q - 1:q]), (SSD_HEADS, SSD_STATE))
            for e in range(8):
                h = g * 8 + e
                hfin_ref[s,h] = (h0_ref[s,h] * head_decay[h:h + 1, :]
                                  + upd[e * SSD_HEAD_DIM:(e + 1) * SSD_HEAD_DIM, :])
        else:
            h_in = h_scr[s * SSD_GROUPS + g]
            y_off = _bdot(cm, h_in)
            upd = lax.dot_general(bm.astype(BF16), xd[:, sl].astype(BF16), (((0,), (0,)), ((), ())),
                                  preferred_element_type=F32)
            h_scr[s * SSD_GROUPS + g] = h_in * chunk_decay_x[:, sl] + upd
        ys.append(y_diag + y_off * decay_in_x[:, sl])
        yield

    y = jnp.concatenate(ys, axis=1) + dsk_ref[...] * xs
    yg = y * _silu(z_ref[s])
    outs = []
    for g in range(SSD_GROUPS):
        t = yg[:, g * gw:(g + 1) * gw]
        outs.append(t * lax.rsqrt(jnp.mean(t * t, axis=-1, keepdims=True) + NORM_EPS))
    y_ref[s] = jnp.concatenate(outs, axis=1) * nrm_ref[...]

    @pl.when(c == last)
    def _():
        cnew_ref[s] = xfull_scr[s,CONV_PAD + q - 3:CONV_PAD + q, :]
        if not single_chunk:
            for g in range(SSD_GROUPS):
                hfin_ref[s,g * 8:(g + 1) * 8] = h_scr[s * SSD_GROUPS + g].T.reshape(8, SSD_HEAD_DIM, SSD_STATE)

    yield


def ssd_mixer(z, xbc, dt, dtt_flat, conv0, ssm0, w, *, q):
    b, l, _ = z.shape
    single_chunk = l == q
    want = SSD_SEQS_PER_STEP if single_chunk else SSD_LONG_SEQS_PER_STEP
    nseq = want if b % want == 0 else 1
    if q % 128 == 0 and nseq == 1:
        dtt = dtt_flat
        dtt_spec = pl.BlockSpec((SSD_HEADS, q), lambda i, c: (0, i * (l // q) + c))
    else:
        dtt = jnp.swapaxes(dt, 1, 2)
        dtt_spec = pl.BlockSpec((nseq, SSD_HEADS, q), lambda i, c: (i, 0, c))
    seq = lambda wd: pl.BlockSpec((nseq, q, wd), lambda i, c: (i, c, 0))
    per_b3 = lambda s: pl.BlockSpec((nseq,) + s, lambda i, c: (i,) + (0,) * len(s))
    col = lambda t: t.reshape(-1, 1)
    rowv = lambda t: t.reshape(1, -1)
    consts = [w["conv_w"], rowv(w["conv_b"]), rowv(w["dt_bias"]), col(w["dt_bias"]),
              rowv(w["a_log"]), col(w["a_log"]),
              rowv(jnp.repeat(w["d_skip"], SSD_HEAD_DIM)), rowv(w["ssd_norm"])]
    return pl.pallas_call(
        functools.partial(_ssd_kernel, q=q, nseq=nseq, single_chunk=single_chunk),
        grid=(b // nseq, l // q),
        in_specs=[seq(SSD_WIDTH), seq(SSD_CONV_DIM), seq(SSD_HEADS),
                  dtt_spec,
                  per_b3((SSD_CONV - 1, SSD_CONV_DIM)),
                  per_b3((SSD_HEADS, SSD_HEAD_DIM, SSD_STATE))]
                 + [_const_spec(t.shape) for t in consts],
        out_specs=[seq(SSD_WIDTH), per_b3((SSD_HEADS, SSD_HEAD_DIM, SSD_STATE)),
                   per_b3((SSD_CONV - 1, SSD_CONV_DIM))],
        out_shape=[jax.ShapeDtypeStruct((b, l, SSD_WIDTH), F32),
                   jax.ShapeDtypeStruct((b, SSD_HEADS, SSD_HEAD_DIM, SSD_STATE), F32),
                   jax.ShapeDtypeStruct((b, SSD_CONV - 1, SSD_CONV_DIM), F32)],
        scratch_shapes=[pltpu.VMEM((nseq, CONV_PAD + q, SSD_CONV_DIM), F32),
                        pltpu.VMEM((nseq * SSD_GROUPS, SSD_STATE, SSD_GROUP_WIDTH), F32)],
        compiler_params=pltpu.CompilerParams(
            dimension_semantics=("parallel", "arbitrary"), vmem_limit_bytes=V7X_VMEM_LIMIT),
        name="ssd_mixer",
    )(z, xbc, dt, dtt, conv0, ssm0, *consts)


def _head_sums(t):
    pair = 2 * RWKV_HEAD_DIM
    first = lax.broadcasted_iota(jnp.int32, (1, pair), 1) < RWKV_HEAD_DIM
    pieces = []
    for j in range(RWKV_HEADS // 2):
        x = t[:, j * pair:(j + 1) * pair]
        x0 = jnp.where(first, x, 0.0)
        s0 = jnp.sum(x0, axis=-1, keepdims=True)
        s1 = jnp.sum(x - x0, axis=-1, keepdims=True)
        pieces.append(jnp.where(first, s0, s1))
    return jnp.concatenate(pieces, axis=1)


def _rwkv_prep_kernel(rw_ref, sh0_ref, mu_ref, w0_ref, w2_ref, a0_ref, a2_ref, g2_ref,
                      kk_ref, ka_ref, rk_ref,
                      *rest, tt, seqs, chunk):
    outs, full_scr = rest[:-1], rest[-1]
    shn_ref = outs[-1]
    c = pl.program_id(1)
    l = tt // seqs

    if seqs == 1:
        @pl.when(c == 0)
        def _():
            full_scr[0, CONV_PAD - 1:CONV_PAD, :] = sh0_ref[0]

        @pl.when(c > 0)
        def _():
            full_scr[0, CONV_PAD - 1:CONV_PAD, :] = full_scr[0, CONV_PAD + tt - 1:CONV_PAD + tt, :]

        rw = rw_ref[0]
        full_scr[0, CONV_PAD:CONV_PAD + tt, :] = rw
        prev = full_scr[0, CONV_PAD - 1:CONV_PAD - 1 + tt, :]
    else:
        full_scr[:, CONV_PAD - 1:CONV_PAD, :] = sh0_ref[...]
        full_scr[:, CONV_PAD:CONV_PAD + l, :] = rw_ref[...]
        rw = rw_ref[...].reshape(tt, RWKV_PROJ)
        prev = full_scr[:, CONV_PAD - 1:CONV_PAD - 1 + l, :].reshape(tt, RWKV_PROJ)
    vals = _rwkv_mix_math(rw, prev, mu_ref, w0_ref, w2_ref, a0_ref, a2_ref, g2_ref, kk_ref, ka_ref, rk_ref)
    _emit_rwkv_outputs(outs[:-1], vals, tt=tt, chunk=chunk)

    if seqs == 1:
        @pl.when(c == pl.num_programs(1) - 1)
        def _():
            shn_ref[0] = full_scr[0, CONV_PAD + tt - 1:CONV_PAD + tt, :]
    else:
        shn_ref[...] = rw_ref[:, l - 1:l, :]


def _rwkv_mix_math(rw, prev, mu_ref, w0_ref, w2_ref, a0_ref, a2_ref, g2_ref, kk_ref, ka_ref, rk_ref):
    wd = RWKV_WIDTH
    u = rw + (prev - rw) * mu_ref[...]
    r = u[:, :wd]
    k = u[:, wd:2 * wd]
    v = u[:, 2 * wd:3 * wd]
    w_lo = u[:, 3 * wd:3 * wd + DECAY_LORA]
    a_lo = u[:, 3 * wd + DECAY_LORA:3 * wd + DECAY_LORA + AAA_LORA]
    g_lo = u[:, 3 * wd + DECAY_LORA + AAA_LORA:]

    w_log = -_softplus(-(w0_ref[...] + _bdot(jnp.tanh(w_lo), w2_ref[...]))) - 0.5
    lw = -jnp.exp(w_log)
    a = _sigmoid(a0_ref[...] + _bdot(a_lo, a2_ref[...]))
    gate = _bdot(_sigmoid(g_lo), g2_ref[...])

    kk = k * kk_ref[...]
    kk = kk / jnp.maximum(jnp.sqrt(_head_sums(kk * kk)), 1e-12)
    kf = k * (1.0 + (a - 1.0) * ka_ref[...])
    kb = kk * a
    bonus = _head_sums(r * kf * rk_ref[...]) * v
    return r, lw, kf, v, kk, kb, gate, bonus


def _emit_rwkv_outputs(outs, vals, *, tt, chunk, row0=0):
    r, lw, kf, v, kk, kb, gate, bonus = vals
    wd = RWKV_WIDTH
    gate_out, bonus_out = outs[-2:]
    if chunk == 0:
        blk = gate_out.shape
        for ref, val in zip(outs, (r, jnp.exp(lw), kf, v, kk, kb, gate, bonus)):
            ref[...] = val.reshape(blk)
    else:
        gate_out[0, row0:row0 + tt, :] = gate
        bonus_out[0, row0:row0 + tt, :] = bonus
        kkt_out, rt_out, kfh_out, nbh_out, vb_out, kbg_out, kfg_out, gend_out = outs[:8]
        tri = (lax.broadcasted_iota(jnp.int32, (chunk, chunk), 0)
               >= lax.broadcasted_iota(jnp.int32, (chunk, chunk), 1)).astype(F32)
        for ci in range(tt // chunk):
            rs = slice(ci * chunk, (ci + 1) * chunk)
            ro = slice(row0 + ci * chunk, row0 + (ci + 1) * chunk)
            lw_c = lw[rs]
            cum = _dot01(tri, lw_c, exact_side="rhs")
            cum_end = cum[chunk - 1:chunk, :]
            g_inv = jnp.exp(-cum)
            g_tail = jnp.exp(cum_end - cum)
            kkt_out[0, ro, :] = (kk[rs] * jnp.exp(cum - lw_c)).astype(BF16)
            rt_out[0, ro, :] = (r[rs] * jnp.exp(cum)).astype(BF16)
            kfh_out[0, ro, :] = (kf[rs] * g_tail).astype(BF16)
            nbh_out[0, ro, :] = (-kb[rs] * g_tail).astype(BF16)
            vb_out[0, ro, :] = v[rs].astype(BF16)
            kbg_out[0, ro, :] = (kb[rs] * g_inv).astype(BF16)
            kfg_out[0, ro, :] = (kf[rs] * g_inv).astype(BF16)
            gend_out[0, row0 // chunk + ci] = jnp.broadcast_to(jnp.exp(cum_end), (8, wd))


def rwkv_prep(rw, shift0, w, *, tt, chunk):
    b, l, _ = rw.shape
    seqs = max(1, tt // l)
    assert chunk == 0 or (seqs == 1 and tt % chunk == 0)
    rowv = lambda t: t.reshape(1, -1)
    consts = [rowv(w["shift_mu"]), rowv(w["w0"]), w["w2"].astype(BF16), rowv(w["a0"]),
              w["a2"].astype(BF16), w["g2"].astype(BF16), rowv(w["k_k"]), rowv(w["k_a"]),
              rowv(w["r_k"])]
    rows = tt // seqs
    grid = (b // seqs, l // rows)
    seq = lambda wd: pl.BlockSpec((seqs, rows, wd), lambda i, c: (i, c, 0))
    one = pl.BlockSpec((seqs, 1, RWKV_PROJ), lambda i, c: (i, 0, 0))
    sds = jax.ShapeDtypeStruct
    f32_seq = sds((b, l, RWKV_WIDTH), F32)
    if chunk == 0:
        op_specs = [seq(RWKV_WIDTH)] * 6
        op_shapes = [f32_seq] * 6
    else:
        per_tile = tt // chunk
        op_specs = [seq(RWKV_WIDTH)] * 7 + [
            pl.BlockSpec((1, per_tile, 8, RWKV_WIDTH), lambda i, c: (i, c, 0, 0))]
        op_shapes = [sds((b, l, RWKV_WIDTH), BF16)] * 7 + [sds((b, l // chunk, 8, RWKV_WIDTH), F32)]
    outs = pl.pallas_call(
        functools.partial(_rwkv_prep_kernel, tt=tt, seqs=seqs, chunk=chunk),
        grid=grid,
        in_specs=[seq(RWKV_PROJ), one] + [_const_spec(t.shape) for t in consts],
        out_specs=op_specs + [seq(RWKV_WIDTH)] * 2 + [one],
        out_shape=op_shapes + [f32_seq] * 2 + [sds((b, 1, RWKV_PROJ), F32)],
        scratch_shapes=[pltpu.VMEM((seqs, CONV_PAD + rows, RWKV_PROJ), F32)],
        compiler_params=pltpu.CompilerParams(
            dimension_semantics=("parallel", "arbitrary"), vmem_limit_bytes=V7X_VMEM_LIMIT),
        name="rwkv_prep",
    )(rw, shift0, *consts)
    return outs


def _proj_prep_kernel(x_ref, sh0_ref, g_ref, wz_ref, wx_ref, wr_ref, wdt_ref, wdtt_ref,
                      mu_ref, w0_ref, w2_ref, a0_ref, a2_ref, g2_ref, kk_ref, ka_ref, rk_ref,
                      z_ref, xbc_ref, dt_ref, dtt_ref, *rest, tm, tiles_per_seq, chunk):
    outs, (new_scr, cur_scr) = rest[:-2], rest[-2:]
    shn_ref = outs[-1]
    i = pl.program_id(0)

    @pl.when(i == 0)
    def _():
        new_scr[...] = jnp.zeros(new_scr.shape, F32)
        cur_scr[...] = jnp.zeros(cur_scr.shape, F32)

    k = jnp.maximum(i - 1, 0)
    first = (k % tiles_per_seq) == 0
    cur_scr[CONV_PAD - 1:CONV_PAD, :] = jnp.where(first, sh0_ref[0], cur_scr[CONV_PAD + tm - 1:CONV_PAD + tm, :])
    cur_scr[CONV_PAD:CONV_PAD + tm, :] = new_scr[...]

    shn_ref[0] = cur_scr[CONV_PAD + tm - 1:CONV_PAD + tm, :]
    u = _rms(x_ref[...], g_ref[...]).astype(BF16)

    def project(piece):
        if piece == 0:
            z_ref[...] = jnp.dot(u, wz_ref[...], preferred_element_type=F32)
        elif piece == 1:
            xbc_ref[...] = jnp.dot(u, wx_ref[...], preferred_element_type=F32)
        elif piece == 2:
            cols = slice(0, 2 * RWKV_WIDTH)
            new_scr[:, cols] = jnp.dot(u, wr_ref[:, cols], preferred_element_type=F32)
        else:
            cols = slice(2 * RWKV_WIDTH, RWKV_PROJ)
            new_scr[:, cols] = jnp.dot(u, wr_ref[:, cols], preferred_element_type=F32)
            dt_ref[...] = jnp.dot(u, wdt_ref[...], preferred_element_type=F32)
            dtt_ref[...] = lax.dot_general(wdtt_ref[...], u, (((1,), (1,)), ((), ())),
                                           preferred_element_type=F32)

    def prepare(part, rows):
        lo = CONV_PAD + part * rows
        padded = cur_scr[lo - CONV_PAD:lo + rows, :]
        rw = padded[CONV_PAD:]
        prev = pltpu.roll(padded, 1, axis=0)[CONV_PAD:]
        vals = _rwkv_mix_math(rw, prev, mu_ref, w0_ref, w2_ref, a0_ref, a2_ref, g2_ref, kk_ref, ka_ref, rk_ref)
        _emit_rwkv_outputs(outs[:-1], vals, tt=rows, chunk=chunk, row0=part * rows)

    pieces = 4
    parts = min(pieces, tm // chunk)
    for part in range(parts):
        for piece in range(part * pieces // parts, (part + 1) * pieces // parts):
            project(piece)
        prepare(part, tm // parts)


def proj_prep(x, shift0, g, wz, wx, wr, wdt, w, *, tm, chunk):
    b, l, _ = x.shape
    n = b * l
    nt, tps = n // tm, l // tm
    assert l % tm == 0 and tm % chunk == 0
    x2 = x.reshape(n, D_MODEL)
    rowv = lambda t: t.reshape(1, -1)
    wdtt = wdt.T
    consts = [g, wz, wx, wr, wdt, wdtt,
              rowv(w["shift_mu"]), rowv(w["w0"]), w["w2"].astype(BF16), rowv(w["a0"]),
              w["a2"].astype(BF16), w["g2"].astype(BF16), rowv(w["k_k"]), rowv(w["k_a"]), rowv(w["r_k"])]
    ahead = lambda i: jnp.minimum(i, nt - 1)
    behind = lambda i: jnp.maximum(i - 1, 0)
    row_a = lambda w_: pl.BlockSpec((tm, w_), lambda i: (ahead(i), 0))
    seq_b = lambda w_: pl.BlockSpec((1, tm, w_), lambda i: (behind(i) // tps, behind(i) % tps, 0))
    one_b = pl.BlockSpec((1, 1, RWKV_PROJ), lambda i: (behind(i) // tps, 0, 0))
    per_tile = tm // chunk
    gend_spec = pl.BlockSpec((1, per_tile, 8, RWKV_WIDTH), lambda i: (behind(i) // tps, behind(i) % tps, 0, 0))
    sds = jax.ShapeDtypeStruct
    outs = pl.pallas_call(
        functools.partial(_proj_prep_kernel, tm=tm, tiles_per_seq=tps, chunk=chunk),
        grid=(nt + 1,),
        in_specs=[row_a(D_MODEL), one_b] + [_const_spec(t.shape) for t in consts],
        out_specs=[row_a(SSD_WIDTH), row_a(SSD_CONV_DIM), row_a(SSD_HEADS),
                   pl.BlockSpec((SSD_HEADS, tm), lambda i: (0, ahead(i)))]
                  + [seq_b(RWKV_WIDTH)] * 7 + [gend_spec] + [seq_b(RWKV_WIDTH)] * 2 + [one_b],
        out_shape=[sds((n, SSD_WIDTH), F32), sds((n, SSD_CONV_DIM), F32), sds((n, SSD_HEADS), F32),
                   sds((SSD_HEADS, n), F32)]
                  + [sds((b, l, RWKV_WIDTH), BF16)] * 7 + [sds((b, l // chunk, 8, RWKV_WIDTH), F32)]
                  + [sds((b, l, RWKV_WIDTH), F32)] * 2 + [sds((b, 1, RWKV_PROJ), F32)],
        scratch_shapes=[pltpu.VMEM((tm, RWKV_PROJ), F32), pltpu.VMEM((CONV_PAD + tm, RWKV_PROJ), F32)],
        compiler_params=pltpu.CompilerParams(
            dimension_semantics=("arbitrary",), vmem_limit_bytes=V7X_VMEM_LIMIT),
        name="proj_prep",
    )(x2, shift0, *consts)
    return outs


def _wkv_kernel(r_ref, w_ref, k_ref, v_ref, kk_ref, kka_ref, s0_ref,
                o_ref, sfin_ref, s_scr, vt_scr, ot_scr, *, steps):
    c = pl.program_id(1)
    n = RWKV_HEAD_DIM
    lanes = WKV_BATCH_BLOCK * RWKV_HEADS

    @pl.when(c == 0)
    def _():
        s_scr[...] = s0_ref[...].reshape(lanes, n * n).T.reshape(n, n, lanes)

    def to_pairs(ref, t):
        return ref[:, t].reshape(lanes, n).T

    def step(t, carry):
        r_t = to_pairs(r_ref, t)
        w_t = to_pairs(w_ref, t)
        k_t = to_pairs(k_ref, t)
        kk_t = to_pairs(kk_ref, t)
        kka_t = to_pairs(kka_ref, t)
        vt_scr[...] = to_pairs(v_ref, t)

        def per_value(vi, carry2):
            s_v = s_scr[vi]
            skk = jnp.sum(s_v * kk_t, axis=0, keepdims=True)
            v_row = vt_scr[pl.ds(vi, 1), :]
            s_new = s_v * w_t - skk * kka_t + v_row * k_t
            s_scr[vi] = s_new
            ot_scr[pl.ds(vi, 1), :] = jnp.sum(s_new * r_t, axis=0, keepdims=True)
            return carry2

        lax.fori_loop(0, n, per_value, 0, unroll=4)
        o_ref[:, t] = ot_scr[...].T.reshape(WKV_BATCH_BLOCK, RWKV_HEADS, n)
        return carry

    lax.fori_loop(0, steps, step, 0)

    @pl.when(c == pl.num_programs(1) - 1)
    def _():
        sfin_ref[...] = s_scr[...].reshape(n * n, lanes).T.reshape(
            WKV_BATCH_BLOCK, RWKV_HEADS, n, n)


def wkv_scan(r, w, k, v, kk, kka, s0, *, steps):
    b, l, _ = r.shape
    h, n = RWKV_HEADS, RWKV_HEAD_DIM
    assert b % WKV_BATCH_BLOCK == 0 and l % steps == 0
    ops = [t.reshape(b, l, h, n) for t in (r, w, k, v, kk, kka)]
    seq_spec = pl.BlockSpec((WKV_BATCH_BLOCK, steps, h, n), lambda g, c: (g, c, 0, 0))
    st_spec = pl.BlockSpec((WKV_BATCH_BLOCK, h, n, n), lambda g, c: (g, 0, 0, 0))
    o, s_fin = pl.pallas_call(
        functools.partial(_wkv_kernel, steps=steps),
        grid=(b // WKV_BATCH_BLOCK, l // steps),
        in_specs=[seq_spec] * 6 + [st_spec],
        out_specs=[seq_spec, st_spec],
        out_shape=[jax.ShapeDtypeStruct((b, l, h, n), F32),
                   jax.ShapeDtypeStruct((b, h, n, n), F32)],
        scratch_shapes=[pltpu.VMEM((n, n, WKV_BATCH_BLOCK * h), F32),
                        pltpu.VMEM((n, WKV_BATCH_BLOCK * h), F32),
                        pltpu.VMEM((n, WKV_BATCH_BLOCK * h), F32)],
        compiler_params=pltpu.CompilerParams(
            dimension_semantics=("parallel", "arbitrary"), vmem_limit_bytes=V7X_VMEM_LIMIT),
        name="wkv_scan",
    )(*ops, s0)
    return o.reshape(b, l, h * n), s_fin


WKV_LANE_BATCH = 128


def _wkv_batch_lanes_kernel(r_ref, w_ref, k_ref, v_ref, kk_ref, kka_ref, s0_ref,
                            o_ref, sfin_ref, op_scr, ot_scr, *, steps):
    n = RWKV_HEAD_DIM
    nb = WKV_LANE_BATCH
    sfin_ref[...] = s0_ref[...]

    def step(t, carry):
        rows = pl.ds(t, nb, stride=steps)
        for i, ref in enumerate((r_ref, w_ref, k_ref, kk_ref, kka_ref, v_ref)):
            op_scr[i] = ref[rows, :].T
        for h2 in range(2):
            ch = slice(h2 * n, (h2 + 1) * n)
            r_t = op_scr[0, ch, :]
            kka_r = jnp.sum(op_scr[4, ch, :] * r_t, axis=0, keepdims=True)
            k_r = jnp.sum(op_scr[2, ch, :] * r_t, axis=0, keepdims=True)
            op_scr[0, ch, :] = op_scr[1, ch, :] * r_t

            def per_value(vi, carry2, ch=ch, h2=h2, kka_r=kka_r, k_r=k_r):
                s_v = sfin_ref[h2, vi]
                skk = jnp.sum(s_v * op_scr[3, ch, :], axis=0, keepdims=True)
                out = jnp.sum(s_v * op_scr[0, ch, :], axis=0, keepdims=True)
                v_row = op_scr[5, pl.ds(h2 * n + vi, 1), :]
                sfin_ref[h2, vi] = s_v * op_scr[1, ch, :] - skk * op_scr[4, ch, :] + v_row * op_scr[2, ch, :]
                ot_scr[pl.ds(h2 * n + vi, 1), :] = out - skk * kka_r + v_row * k_r
                return carry2

            lax.fori_loop(0, n, per_value, 0, unroll=8)
        o_ref[rows, :] = ot_scr[...].T
        return carry

    lax.fori_loop(0, steps, step, 0)


def wkv_scan_batch_lanes(r, w, k, v, kk, kka, s0):
    b, l, wd = r.shape
    h, n, nb = RWKV_HEADS, RWKV_HEAD_DIM, WKV_LANE_BATCH
    assert b % nb == 0
    ops = [t.reshape(b * l, wd) for t in (r, w, k, v, kk, kka)]
    s0t = jnp.transpose(s0, (1, 2, 3, 0))
    seq_spec = pl.BlockSpec((nb * l, 2 * n), lambda g, j: (g, j))
    st_spec = pl.BlockSpec((2, n, n, nb), lambda g, j: (j, 0, 0, g))
    o, s_fin = pl.pallas_call(
        functools.partial(_wkv_batch_lanes_kernel, steps=l),
        grid=(b // nb, h // 2),
        in_specs=[seq_spec] * 6 + [st_spec],
        out_specs=[seq_spec, st_spec],
        out_shape=[jax.ShapeDtypeStruct((b * l, wd), F32), jax.ShapeDtypeStruct((h, n, n, b), F32)],
        scratch_shapes=[pltpu.VMEM((6, 2 * n, nb), F32), pltpu.VMEM((2 * n, nb), F32)],
        compiler_params=pltpu.CompilerParams(
            dimension_semantics=("parallel", "parallel"), vmem_limit_bytes=V7X_VMEM_LIMIT),
        name="wkv_scan_batch_lanes",
    )(*ops, s0t)
    return o.reshape(b, l, wd), jnp.transpose(s_fin, (3, 0, 1, 2))


WKV_CHUNK = 64
WKV_PAIRS = RWKV_HEADS // 2
WKV_ROW_STRIDE = WKV_CHUNK + 8
WKV_LOCKSTEP = 8
WKV_SOLVE_ROWS = 8
WKV_SOLVE_COLS = 16


def _pair_masks():
    c = WKV_CHUNK
    row = lax.broadcasted_iota(jnp.int32, (2 * c, 2 * c), 0)
    col = lax.broadcasted_iota(jnp.int32, (2 * c, 2 * c), 1)
    t, i = row % c, col % c
    keep = i <= t - jnp.where(row < c, 1, 0)
    sign = jnp.where(row >= c, jnp.where(col < c, -1.0, 1.0), 1.0)
    block_diag = row // c == col // c
    return keep, sign, block_diag


def _wkv_prepare_kernel(kkt_ref, rt_ref, kbg_ref, kfg_ref, vb_ref,
                        lo_ref, rhs0_ref, tp_ref, abs_scr, top_scr, abt_scr, tt_scr):
    c = WKV_CHUNK
    keep, sign, _ = _pair_masks()
    lane = lax.broadcasted_iota(jnp.int32, (1, 2 * c), 1)
    head0 = lane < RWKV_HEAD_DIM
    row_head0 = lax.broadcasted_iota(jnp.int32, (2 * RWKV_HEAD_DIM, 1), 0) < RWKV_HEAD_DIM
    zeros = jnp.zeros((c, 2 * c), BF16)

    def one_batch(b, k):
        for j in range(WKV_PAIRS):
            sl = slice(j * 2 * RWKV_HEAD_DIM, (j + 1) * 2 * RWKV_HEAD_DIM)
            lhs = jnp.concatenate([kkt_ref[b, :, sl], rt_ref[b, :, sl]], axis=0)
            rhs = jnp.concatenate([kbg_ref[b, :, sl], kfg_ref[b, :, sl]], axis=0)
            rhs_t = rhs.astype(F32).T
            top = jnp.where(row_head0, rhs_t, 0.0)
            w_a = jnp.concatenate([top, rhs_t - top], axis=1).astype(BF16)
            a_both = jnp.dot(lhs, w_a, preferred_element_type=F32)
            for h2 in range(2):
                a = a_both[:, h2 * 2 * c:(h2 + 1) * 2 * c]
                a = jnp.where(keep, a, 0.0) * sign
                inst = h2 * (WKV_BATCH_BLOCK * WKV_PAIRS) + b * WKV_PAIRS + j
                abs_scr[pl.ds(inst * WKV_ROW_STRIDE, c), :c] = a[:c, :c]
                top_scr[k, j, :, h2 * 2 * c:(h2 + 1) * 2 * c] = a[:c].astype(BF16)
                lo_ref[b, :, (2 * j + h2) * 2 * c:(2 * j + h2 + 1) * 2 * c] = a[c:].astype(BF16)
        yield
        for j in range(WKV_PAIRS):
            sl = slice(j * 2 * RWKV_HEAD_DIM, (j + 1) * 2 * RWKV_HEAD_DIM)
            v = vb_ref[b, :, sl]
            v0 = jnp.where(head0, v, jnp.zeros_like(v))
            w_akf = jnp.concatenate([zeros, v0, zeros, v - v0], axis=0)
            rhs0_ref[b, :, sl] = jnp.dot(top_scr[k, j], w_akf, preferred_element_type=F32)
        yield

    def two_batches(bb, carry):
        for _ in zip(*[one_batch(WKV_LOCKSTEP * bb + k, k) for k in range(WKV_LOCKSTEP)]):
            pass
        return carry

    lax.fori_loop(0, WKV_BATCH_BLOCK // WKV_LOCKSTEP, two_batches, 0)

    n_inst = 2 * WKV_BATCH_BLOCK * WKV_PAIRS
    n_pair_rows = WKV_BATCH_BLOCK * WKV_PAIRS

    def to_lanes(t, carry):
        abt_scr[t] = abs_scr[pl.ds(t, n_inst, stride=WKV_ROW_STRIDE), :][:, :c].T
        return carry

    lax.fori_loop(0, c, to_lanes, 0, unroll=8)

    tt_scr[...] = jnp.zeros(tt_scr.shape, F32)
    sub_iota = lax.broadcasted_iota(jnp.int32, (WKV_SOLVE_COLS, n_inst), 0)
    rows = range(WKV_SOLVE_ROWS)
    for cb in range(c // WKV_SOLVE_COLS):
        col0 = WKV_SOLVE_COLS * cb
        cols = slice(col0, col0 + WKV_SOLVE_COLS)
        first_block = col0 // WKV_SOLVE_ROWS

        def solve_rows(tb, carry, col0=col0, cols=cols, first_block=first_block):
            t0 = tb * WKV_SOLVE_ROWS

            def sub(ib, accs):
                ps = [tt_scr[ib * WKV_SOLVE_ROWS + di, cols, :] for di in rows]
                out = []
                for r in rows:
                    terms = [abt_scr[t0 + r, pl.ds(ib * WKV_SOLVE_ROWS + di, 1), :] * ps[di] for di in rows]
                    while len(terms) > 1:
                        terms = [a + b for a, b in zip(terms[::2], terms[1::2])]
                    out.append(accs[r] - terms[0])
                return tuple(out)

            unit = tuple(jnp.where(sub_iota + col0 == t0 + r, 1.0, 0.0) for r in rows)
            accs = list(lax.fori_loop(first_block, tb, sub, unit))
            for r in rows:
                for r2 in range(r):
                    accs[r] = accs[r] - abt_scr[t0 + r, pl.ds(t0 + r2, 1), :] * accs[r2]
                tt_scr[t0 + r, cols, :] = accs[r]
            return carry

        lax.fori_loop(first_block, c // WKV_SOLVE_ROWS, solve_rows, 0)

    def from_lanes(t, carry):
        m = tt_scr[t].T
        abs_scr[pl.ds(t, n_pair_rows, stride=WKV_ROW_STRIDE), :] = jnp.concatenate(
            [m[:n_pair_rows], m[n_pair_rows:]], axis=1)
        return carry

    lax.fori_loop(0, c, from_lanes, 0, unroll=8)

    def emit(b, carry):
        for j in range(WKV_PAIRS):
            row0 = (b * WKV_PAIRS + j) * WKV_ROW_STRIDE
            tp_ref[b, :, j * 2 * c:(j + 1) * 2 * c] = abs_scr[pl.ds(row0, c), :].astype(BF16)
        return carry

    lax.fori_loop(0, WKV_BATCH_BLOCK, emit, 0)


def _wkv_apply_kernel(kkt_ref, rt_ref, kfh_ref, nbh_ref, vb_ref, tp_ref, lo_ref, rhs0_ref, gend_ref, s0_ref,
                      o_ref, sfin_ref, x_scr, wp_scr, p_scr):
    c = WKV_CHUNK
    n = RWKV_HEAD_DIM
    ch = pl.program_id(1)
    _, _, block_diag = _pair_masks()
    lane = lax.broadcasted_iota(jnp.int32, (1, 2 * c), 1)
    head0 = lane < n
    eye2 = (lax.broadcasted_iota(jnp.int32, (n, 2 * n), 0)
            == lax.broadcasted_iota(jnp.int32, (n, 2 * n), 1) % n).astype(F32)

    @pl.when(ch == 0)
    def _():
        def init(b, carry):
            for j in range(WKV_PAIRS):
                sp = s0_ref[b, 2 * j:2 * j + 2].reshape(2 * n, n)
                dup = jnp.dot(sp, eye2, precision=HIGHEST, preferred_element_type=F32)
                x_scr[b, j] = jnp.where(block_diag, dup, 0.0)
            return carry
        lax.fori_loop(0, WKV_BATCH_BLOCK, init, 0)

    def one_batch(b, k):
        for j in range(WKV_PAIRS):
            sl = slice(j * 2 * n, (j + 1) * 2 * n)
            lhs = jnp.concatenate([kkt_ref[b, :, sl], rt_ref[b, :, sl]], axis=0)
            kx = lax.dot_general(lhs, x_scr[b, j].astype(BF16), (((1,), (1,)), ((), ())),
                                 preferred_element_type=F32)
            rhs = kx[:c] + rhs0_ref[b, :, sl]
            r0 = jnp.where(head0, rhs, 0.0)
            wp_scr[k, j] = jnp.concatenate([r0, rhs - r0], axis=0).astype(BF16)
            o_ref[b, :, sl] = kx[c:]
        yield
        for j in range(WKV_PAIRS):
            sl = slice(j * 2 * n, (j + 1) * 2 * n)
            p = jnp.dot(tp_ref[b, :, sl], wp_scr[k, j], preferred_element_type=F32)
            p_scr[k, j] = p.astype(BF16)
        yield
        for j in range(WKV_PAIRS):
            sl = slice(j * 2 * n, (j + 1) * 2 * n)
            v = vb_ref[b, :, sl]
            pb = p_scr[k, j]
            zero = jnp.zeros_like(pb)
            p0, v0 = jnp.where(head0, pb, zero), jnp.where(head0, v, zero)
            w_o = jnp.concatenate([p0, v0, pb - p0, v - v0], axis=0)
            o_ref[b, :, sl] = o_ref[b, :, sl] + jnp.dot(
                lo_ref[b, :, j * 4 * c:(j + 1) * 4 * c], w_o, preferred_element_type=F32)
            vp = jnp.concatenate([v, pb], axis=0)
            kb = jnp.concatenate([kfh_ref[b, :, sl], nbh_ref[b, :, sl]], axis=0)
            upd = lax.dot_general(vp, kb, (((0,), (0,)), ((), ())), preferred_element_type=F32)
            x_scr[b, j] = jnp.where(block_diag, x_scr[b, j] * gend_ref[b, 0, 0:1, sl] + upd, 0.0)
        yield

    def two_batches(bb, carry):
        for _ in zip(*[one_batch(WKV_LOCKSTEP * bb + k, k) for k in range(WKV_LOCKSTEP)]):
            pass
        return carry

    lax.fori_loop(0, WKV_BATCH_BLOCK // WKV_LOCKSTEP, two_batches, 0)

    @pl.when(ch == pl.num_programs(1) - 1)
    def _():
        def fin(b, carry):
            for j in range(WKV_PAIRS):
                sp = lax.dot_general(x_scr[b, j], eye2, (((1,), (1,)), ((), ())),
                                     precision=HIGHEST, preferred_element_type=F32)
                sfin_ref[b, 2 * j:2 * j + 2] = sp.reshape(2, n, n)
            return carry
        lax.fori_loop(0, WKV_BATCH_BLOCK, fin, 0)


def wkv_chunked(kkt, rt, kfh, nbh, vb, kbg, kfg, gend, s0):
    b, l, wd = kkt.shape
    c = WKV_CHUNK
    assert b % WKV_BATCH_BLOCK == 0 and l % c == 0
    gb, nc = b // WKV_BATCH_BLOCK, l // c
    lanes = 2 * WKV_BATCH_BLOCK * WKV_PAIRS
    seq = lambda w_: pl.BlockSpec((WKV_BATCH_BLOCK, c, w_), lambda g, i: (g, i, 0))
    gend_spec = pl.BlockSpec((WKV_BATCH_BLOCK, 1, 8, wd), lambda g, i: (g, i, 0, 0))
    sds = jax.ShapeDtypeStruct
    st_spec = pl.BlockSpec((WKV_BATCH_BLOCK, RWKV_HEADS, RWKV_HEAD_DIM, RWKV_HEAD_DIM),
                           lambda g, i: (g, 0, 0, 0))
    blk = (WKV_BATCH_BLOCK, c, wd)
    o, s_fin = pl.pallas_call(
        _wkv_chunk_kernel,
        grid=(gb, nc),
        in_specs=[seq(wd)] * 7 + [gend_spec, st_spec],
        out_specs=[seq(wd), st_spec],
        out_shape=[sds((b, l, wd), F32), sds(s0.shape, F32)],
        scratch_shapes=[
            pltpu.VMEM((lanes * WKV_ROW_STRIDE, 2 * c), F32),
            pltpu.VMEM((WKV_LOCKSTEP, WKV_PAIRS, c, 4 * c), BF16),
            pltpu.VMEM((c, c, lanes), F32),
            pltpu.VMEM((c, c, lanes), F32),
            pltpu.VMEM((WKV_BATCH_BLOCK, c, 2 * wd), BF16),
            pltpu.VMEM(blk, F32),
            pltpu.VMEM(blk, BF16),
            pltpu.VMEM((WKV_BATCH_BLOCK, WKV_PAIRS, 2 * RWKV_HEAD_DIM, 2 * RWKV_HEAD_DIM), F32),
            pltpu.VMEM((WKV_LOCKSTEP, WKV_PAIRS, 2 * c, 2 * RWKV_HEAD_DIM), BF16),
            pltpu.VMEM((WKV_LOCKSTEP, WKV_PAIRS, c, 2 * RWKV_HEAD_DIM), BF16)],
        compiler_params=pltpu.CompilerParams(
            dimension_semantics=("parallel", "arbitrary"), vmem_limit_bytes=V7X_VMEM_LIMIT),
        name="wkv_chunked",
    )(kkt, rt, kfh, nbh, vb, kbg, kfg, gend, s0)
    return o, s_fin


def _wkv_chunk_kernel(kkt_ref, rt_ref, kfh_ref, nbh_ref, vb_ref, kbg_ref, kfg_ref, gend_ref, s0_ref,
                      o_ref, sfin_ref, abs_scr, top_scr, abt_scr, tt_scr, lo_scr, rhs0_scr, tp_scr,
                      x_scr, wp_scr, p_scr):
    _wkv_prepare_kernel(kkt_ref, rt_ref, kbg_ref, kfg_ref, vb_ref, lo_scr, rhs0_scr, tp_scr,
                        abs_scr, top_scr, abt_scr, tt_scr)
    _wkv_apply_kernel(kkt_ref, rt_ref, kfh_ref, nbh_ref, vb_ref, tp_scr, lo_scr, rhs0_scr, gend_ref, s0_ref,
                      o_ref, sfin_ref, x_scr, wp_scr, p_scr)


def _tail_kernel(o_ref, gate_ref, bonus_ref, x_ref, ys_ref, p_ref, lnw_ref, lnb_ref, woa_ref, wob_ref,
                 nf_ref, wg_ref, wu_ref, wd_ref, np_ref, wpg_ref, wpp_ref, nl_ref, y_ref, new_scr, cur_scr):
    i = pl.program_id(0)

    @pl.when(i == 0)
    def _():
        new_scr[...] = jnp.zeros(new_scr.shape, BF16)

    cur_scr[...] = new_scr[...]

    parts = 4
    rows = o_ref.shape[0] // parts
    inv_n = 1.0 / RWKV_HEAD_DIM

    def vector_half(part):
        rs = slice(part * rows, (part + 1) * rows)
        o = o_ref[rs, :]
        mu = _head_sums(o) * inv_n
        d = o - mu
        var = _head_sums(d * d) * inv_n
        on = d * lax.rsqrt(var + GN_EPS) * lnw_ref[...] + lnb_ref[...]
        new_scr[rs, :] = ((on + bonus_ref[rs, :]) * gate_ref[rs, :]).astype(BF16)

    vector_half(0)
    y_rwkv = cur_scr[...]
    h = x_ref[...] + _bdot(ys_ref[...], woa_ref[...]) + jnp.dot(y_rwkv, wob_ref[...],
                                                                preferred_element_type=F32)
    hf = _rms(h, nf_ref[...]).astype(BF16)
    vector_half(1)
    gate = jnp.dot(hf, wg_ref[...], preferred_element_type=F32)
    up = jnp.dot(hf, wu_ref[...], preferred_element_type=F32)
    vector_half(2)
    h = h + _bdot(_silu(gate) * up, wd_ref[...])
    vector_half(3)
    pg = _sigmoid(_bdot(_rms(h, np_ref[...]), wpg_ref[...]))
    h = h + pg * _bdot(p_ref[...], wpp_ref[...])
    y_ref[...] = _rms(h, nl_ref[...])


def layer_tail(o, gate, bonus, x, y_ssd, p, consts, *, tm):
    n = x.shape[0]
    nt = n // tm
    ahead = lambda w_: pl.BlockSpec((tm, w_), lambda i: (jnp.minimum(i, nt - 1), 0))
    behind = lambda w_: pl.BlockSpec((tm, w_), lambda i: (jnp.maximum(i - 1, 0), 0))
    return pl.pallas_call(
        _tail_kernel,
        grid=(nt + 1,),
        in_specs=[ahead(D_MODEL)] * 3 + [behind(D_MODEL)] * 2 + [behind(PLE_DIM)]
                 + [_const_spec(t.shape) for t in consts],
        out_specs=behind(D_MODEL),
        out_shape=jax.ShapeDtypeStruct((n, D_MODEL), F32),
        scratch_shapes=[pltpu.VMEM((tm, D_MODEL), BF16), pltpu.VMEM((tm, D_MODEL), BF16)],
        compiler_params=pltpu.CompilerParams(
            dimension_semantics=("arbitrary",), vmem_limit_bytes=V7X_VMEM_LIMIT),
        name="layer_tail",
    )(o, gate, bonus, x, y_ssd, p, *consts)


def _prepare_weights(w):
    c0, c1, c2 = SSD_WIDTH, SSD_WIDTH + SSD_CONV_DIM, SSD_WIDTH + SSD_CONV_DIM + SSD_HEADS
    w_in = w["w_in"]
    rowv = lambda t: t.reshape(1, -1)
    return dict(
        w,
        wz=w_in[:, :c0].astype(BF16), wx=w_in[:, c0:c1].astype(BF16),
        wdt=w_in[:, c1:c2].astype(BF16), wr=w_in[:, c2:].astype(BF16),
        woa=w["w_out"][:SSD_WIDTH].astype(BF16), wob=w["w_out"][SSD_WIDTH:].astype(BF16),
        wg=w["w_gate"].astype(BF16), wu=w["w_up"].astype(BF16), wd=w["w_down"].astype(BF16),
        wpg=w["w_ple_gate"].astype(BF16), wpp=w["w_ple_proj"].astype(BF16),
        norm_mix_r=rowv(w["norm_mix"]), norm_ffn_r=rowv(w["norm_ffn"]),
        norm_ple_r=rowv(w["norm_ple"]), norm_final_r=rowv(w["norm_final"]),
        ln_x_w_r=rowv(w["ln_x_w"]), ln_x_b_r=rowv(w["ln_x_b"]),
    )


TOKEN_TILE = 256
PREP_TILE = 128
SCAN_STEPS = 16


def layer_forward(x, p, conv0, shift0, ssm0, wkv0, w):
    b, l, _ = x.shape
    n = b * l
    tm = min(TOKEN_TILE, n)
    ssd_q = SSD_CHUNK if l % SSD_CHUNK == 0 else l
    prep_tt = min(PREP_TILE, n)
    wkv_steps = min(SCAN_STEPS, l)
    x2 = x.reshape(n, D_MODEL)
    chunked = l % WKV_CHUNK == 0 and l % tm == 0
    if chunked:
        z, xbc, dt, dtt, *ops, gate, bonus, shift_new = proj_prep(
            x, shift0, w["norm_mix_r"], w["wz"], w["wx"], w["wr"], w["wdt"], w, tm=tm, chunk=WKV_CHUNK)
    else:
        z, xbc, rw, dt, dtt = in_projection(x2, w["norm_mix_r"], w["wz"], w["wx"], w["wr"], w["wdt"], tm=tm)
        *ops, gate, bonus, shift_new = rwkv_prep(rw.reshape(b, l, -1), shift0, w, tt=prep_tt, chunk=0)
    y_ssd, ssm_new, conv_new = ssd_mixer(
        z.reshape(b, l, -1), xbc.reshape(b, l, -1), dt.reshape(b, l, -1), dtt, conv0, ssm0, w, q=ssd_q)
    if chunked:
        o, wkv_new = wkv_chunked(*ops, wkv0)
    elif b % WKV_LANE_BATCH == 0:
        o, wkv_new = wkv_scan_batch_lanes(*ops, wkv0)
    else:
        o, wkv_new = wkv_scan(*ops, wkv0, steps=wkv_steps)
    flat = lambda t: t.reshape(n, -1)
    tail_consts = [w["ln_x_w_r"], w["ln_x_b_r"], w["woa"], w["wob"], w["norm_ffn_r"], w["wg"], w["wu"],
                   w["wd"], w["norm_ple_r"], w["wpg"], w["wpp"], w["norm_final_r"]]
    y = layer_tail(flat(o), flat(gate), flat(bonus), x2, flat(y_ssd), p.reshape(n, PLE_DIM),
                   tail_consts, tm=tm)
    return y.reshape(b, l, D_MODEL), ssm_new, conv_new, wkv_new, shift_new


def kernel(x_prompt, x_sample, state_ssm, state_conv, state_wkv, state_shift, p_prompt, p_sample, norm_mix, w_in, conv_w, conv_b, dt_bias, a_log, d_skip, ssd_norm, shift_mu, w0, w2, a0, a2, g2, k_k, k_a, r_k, ln_x_w, ln_x_b, w_out, norm_ffn, w_gate, w_up, w_down, norm_ple, w_ple_gate, w_ple_proj, norm_final):
    w = _prepare_weights(dict(
        norm_mix=norm_mix[0], w_in=w_in[0], conv_w=conv_w[0], conv_b=conv_b[0], dt_bias=dt_bias[0],
        a_log=a_log[0], d_skip=d_skip[0], ssd_norm=ssd_norm[0], shift_mu=shift_mu[0], w0=w0[0],
        w2=w2[0], a0=a0[0], a2=a2[0], g2=g2[0], k_k=k_k[0], k_a=k_a[0], r_k=r_k[0],
        ln_x_w=ln_x_w[0], ln_x_b=ln_x_b[0], w_out=w_out[0], norm_ffn=norm_ffn[0],
        w_gate=w_gate[0], w_up=w_up[0], w_down=w_down[0], norm_ple=norm_ple[0],
        w_ple_gate=w_ple_gate[0], w_ple_proj=w_ple_proj[0], norm_final=norm_final))
    bp = x_prompt.shape[0]
    zeros = lambda *s: jnp.zeros(s, F32)
    yp, s1, c1, k1, t1 = layer_forward(
        x_prompt, p_prompt[0], zeros(bp, SSD_CONV - 1, SSD_CONV_DIM), zeros(bp, 1, RWKV_PROJ),
        zeros(bp, SSD_HEADS, SSD_HEAD_DIM, SSD_STATE),
        zeros(bp, RWKV_HEADS, RWKV_HEAD_DIM, RWKV_HEAD_DIM), w)
    ys, s2, c2, k2, t2 = layer_forward(
        x_sample, p_sample[0], state_conv[0], state_shift[0], state_ssm[0], state_wkv[0], w)
    return (yp, ys, s1[None], c1[None], k1[None], t1[None], s2[None], c2[None], k2[None], t2[None])
```

```python
import functools

import jax
import jax.numpy as jnp
from jax import lax
from jax.experimental import pallas as pl
from jax.experimental.pallas import tpu as pltpu

F32 = jnp.float32
BF16 = jnp.bfloat16
HIGHEST = lax.Precision.HIGHEST

D_MODEL = 1024
SSD_WIDTH = 1024
SSD_HEADS = 16
SSD_HEAD_DIM = 64
SSD_GROUPS = 2
SSD_GROUP_WIDTH = SSD_WIDTH // SSD_GROUPS
SSD_STATE = 128
SSD_CONV = 4
SSD_CHUNK = 128
SSD_BC = SSD_GROUPS * SSD_STATE
SSD_CONV_DIM = SSD_WIDTH + 2 * SSD_BC
RWKV_WIDTH = 1024
RWKV_HEADS = 16
RWKV_HEAD_DIM = 64
DECAY_LORA = 64
AAA_LORA = 64
GATE_LORA = 128
RWKV_PROJ = 3 * RWKV_WIDTH + DECAY_LORA + AAA_LORA + GATE_LORA
PLE_DIM = 256
NORM_EPS = 1e-6
GN_EPS = 64e-5

WKV_BATCH_BLOCK = 8
V7X_VMEM_LIMIT = 56 * 1024 * 1024
CONV_PAD = 8
SSD_SEQS_PER_STEP = 8
SSD_LONG_SEQS_PER_STEP = 1


def _rms(x, g):
    return x * lax.rsqrt(jnp.mean(x * x, axis=-1, keepdims=True) + NORM_EPS) * g


def _sigmoid(x):
    return 0.5 * jnp.tanh(0.5 * x) + 0.5


def _silu(x):
    return x * _sigmoid(x)


def _softplus(x):
    return jnp.maximum(x, 0.0) + jnp.log(1.0 + jnp.exp(-jnp.abs(x)))


def _bdot(a, b):
    return jnp.dot(a.astype(BF16), b.astype(BF16), preferred_element_type=F32)


def _split3(t):
    hi = t.astype(BF16)
    r1 = t - hi.astype(F32)
    mid = r1.astype(BF16)
    lo = (r1 - mid.astype(F32)).astype(BF16)
    return hi, mid, lo


def _dot01(a, b, *, exact_side):
    if exact_side == "lhs":
        m = b.astype(BF16)
        return sum(jnp.dot(p, m, preferred_element_type=F32) for p in _split3(a))
    m = a.astype(BF16)
    return sum(jnp.dot(m, p, preferred_element_type=F32) for p in _split3(b))


def _const_spec(shape):
    return pl.BlockSpec(shape, lambda *_: (0,) * len(shape), pipeline_mode=pl.Buffered(1))


def _head_expand(rows):
    h = lax.broadcasted_iota(jnp.int32, (rows, SSD_WIDTH), 0)
    c = lax.broadcasted_iota(jnp.int32, (rows, SSD_WIDTH), 1)
    return (c // SSD_HEAD_DIM == h).astype(F32)


def _proj_kernel(x_ref, g_ref, wz_ref, wx_ref, wr_ref, wdt_ref, wdtt_ref,
                 z_ref, xbc_ref, rw_ref, dt_ref, dtt_ref):
    u = _rms(x_ref[...], g_ref[...]).astype(BF16)
    z_ref[...] = jnp.dot(u, wz_ref[...], preferred_element_type=F32)
    xbc_ref[...] = jnp.dot(u, wx_ref[...], preferred_element_type=F32)
    rw_ref[...] = jnp.dot(u, wr_ref[...], preferred_element_type=F32)
    dt_ref[...] = jnp.dot(u, wdt_ref[...], preferred_element_type=F32)
    dtt_ref[...] = lax.dot_general(wdtt_ref[...], u, (((1,), (1,)), ((), ())), preferred_element_type=F32)


def in_projection(x, g, wz, wx, wr, wdt, *, tm):
    n = x.shape[0]
    row = lambda w: pl.BlockSpec((tm, w), lambda i: (i, 0))
    wdtt = wdt.T
    return pl.pallas_call(
        _proj_kernel,
        grid=(n // tm,),
        in_specs=[row(D_MODEL), _const_spec((1, D_MODEL)), _const_spec(wz.shape),
                  _const_spec(wx.shape), _const_spec(wr.shape), _const_spec(wdt.shape),
                  _const_spec(wdtt.shape)],
        out_specs=[row(SSD_WIDTH), row(SSD_CONV_DIM), row(RWKV_PROJ), row(SSD_HEADS),
                   pl.BlockSpec((SSD_HEADS, tm), lambda i: (0, i))],
        out_shape=[jax.ShapeDtypeStruct((n, SSD_WIDTH), F32),
                   jax.ShapeDtypeStruct((n, SSD_CONV_DIM), F32),
                   jax.ShapeDtypeStruct((n, RWKV_PROJ), F32),
                   jax.ShapeDtypeStruct((n, SSD_HEADS), F32),
                   jax.ShapeDtypeStruct((SSD_HEADS, n), F32)],
        compiler_params=pltpu.CompilerParams(
            dimension_semantics=("parallel",), vmem_limit_bytes=V7X_VMEM_LIMIT),
        name="in_projection",
    )(x, g, wz, wx, wr, wdt, wdtt)


def _ssd_kernel(z_ref, xbc_ref, dt_ref, dtt_ref, hist_ref, h0_ref, cw_ref, cb_ref,
                dtb_ref, dtbt_ref, alog_ref, alogt_ref, dsk_ref, nrm_ref,
                y_ref, hfin_ref, cnew_ref, xfull_scr, h_scr, *, q, nseq, single_chunk):
    refs = (z_ref, xbc_ref, dt_ref, dtt_ref, hist_ref, h0_ref, cw_ref, cb_ref, dtb_ref, dtbt_ref, alog_ref,
            alogt_ref, dsk_ref, nrm_ref, y_ref, hfin_ref, cnew_ref, xfull_scr, h_scr)
    stages = [_ssd_sequence(s, *refs, q=q, single_chunk=single_chunk) for s in range(nseq)]
    for _ in zip(*stages):
        pass


def _ssd_sequence(s, z_ref, xbc_ref, dt_ref, dtt_ref, hist_ref, h0_ref, cw_ref, cb_ref,
                  dtb_ref, dtbt_ref, alog_ref, alogt_ref, dsk_ref, nrm_ref,
                  y_ref, hfin_ref, cnew_ref, xfull_scr, h_scr, *, q, single_chunk):
    c = pl.program_id(1)
    last = pl.num_programs(1) - 1
    gw = SSD_GROUP_WIDTH

    @pl.when(c == 0)
    def _():
        xfull_scr[s,CONV_PAD - 3:CONV_PAD, :] = hist_ref[s]
        if not single_chunk:
            for g in range(SSD_GROUPS):
                h_scr[s * SSD_GROUPS + g] = h0_ref[s,g * 8:(g + 1) * 8].reshape(gw, SSD_STATE).T

    @pl.when(c > 0)
    def _():
        xfull_scr[s,CONV_PAD - 3:CONV_PAD, :] = xfull_scr[s,CONV_PAD + q - 3:CONV_PAD + q, :]

    xfull_scr[s,CONV_PAD:CONV_PAD + q, :] = xbc_ref[s]

    rows = CONV_PAD + q
    padded = xfull_scr[s]
    conv = cb_ref[...] + padded[CONV_PAD:] * cw_ref[SSD_CONV - 1:SSD_CONV, :]
    for j in range(SSD_CONV - 1):
        lo = CONV_PAD - 3 + j
        conv = conv + pltpu.roll(padded, rows - lo, axis=0)[:q] * cw_ref[j:j + 1, :]
    act = _silu(conv)
    xs = act[:, :SSD_WIDTH]
    yield

    dt = _softplus(dt_ref[s] + dtb_ref[...])
    dtt_raw = dtt_ref[...] if len(dtt_ref.shape) == 2 else dtt_ref[s]
    dtt = _softplus(dtt_raw + dtbt_ref[...])
    da = dt * -jnp.exp(alog_ref[...])
    dat = dtt * -jnp.exp(alogt_ref[...])
    row = lax.broadcasted_iota(jnp.int32, (q, q), 0)
    col = lax.broadcasted_iota(jnp.int32, (q, q), 1)
    causal = row >= col
    a_cum = _dot01(causal.astype(F32), da, exact_side="rhs")
    a_cumt = _dot01(dat, (row <= col).astype(F32), exact_side="lhs")
    yield

    expand = _head_expand(SSD_HEADS)
    decay_in_x = _dot01(jnp.exp(a_cum), expand, exact_side="lhs")
    chunk_decay_x = decay_in_x[q - 1:q, :]
    xd = xs * _dot01(jnp.exp(a_cum[q - 1:q, :] - a_cum) * dt, expand, exact_side="lhs")
    yield

    ys = []
    for g in range(SSD_GROUPS):
        bm = act[:, SSD_WIDTH + g * SSD_STATE:SSD_WIDTH + (g + 1) * SSD_STATE]
        cm = act[:, SSD_WIDTH + SSD_BC + g * SSD_STATE:SSD_WIDTH + SSD_BC + (g + 1) * SSD_STATE]
        cb = lax.dot_general(cm.astype(BF16), bm.astype(BF16), (((1,), (1,)), ((), ())),
                             preferred_element_type=F32)
        yield
        def head_weights(h):
            seg = a_cum[:, h:h + 1] - a_cumt[h:h + 1, :]
            lmat = jnp.where(causal, jnp.exp(jnp.where(causal, seg, 0.0)), 0.0)
            return (cb * lmat * dtt[h:h + 1, :]).astype(BF16)

        y_parts = []
        for e in range(8):
            h = g * 8 + e
            y_parts.append(jnp.dot(head_weights(h), xs[:, h * SSD_HEAD_DIM:(h + 1) * SSD_HEAD_DIM].astype(BF16),
                                   preferred_element_type=F32))
            yield
        y_diag = jnp.concatenate(y_parts, axis=1)
        sl = slice(g * gw, (g + 1) * gw)
        if single_chunk:
            h_in = h0_ref[s,g * 8:(g + 1) * 8].reshape(gw, SSD_STATE)
            y_off = lax.dot_general(cm.astype(BF16), h_in.astype(BF16), (((1,), (1,)), ((), ())),
                                    preferred_element_type=F32)
            upd = lax.dot_general(xd[:, sl].astype(BF16), bm.astype(BF16), (((0,), (0,)), ((), ())),
                                  preferred_element_type=F32)
            head_decay = jnp.broadcast_to(jnp.exp(a_cumt[:, q - 1:q]), (SSD_HEADS, SSD_STATE))
            for e in range(8):
                h = g * 8 + e
                hfin_ref[s,h] = (h0_ref[s,h] * head_decay[h:h + 1, :]
                                  + upd[e * SSD_HEAD_DIM:(e + 1) * SSD_HEAD_DIM, :])
        else:
            h_in = h_scr[s * SSD_GROUPS + g]
            y_off = _bdot(cm, h_in)
            upd = lax.dot_general(bm.astype(BF16), xd[:, sl].astype(BF16), (((0,), (0,)), ((), ())),
                                  preferred_element_type=F32)
            h_scr[s * SSD_GROUPS + g] = h_in * chunk_decay_x[:, sl] + upd
        ys.append(y_diag + y_off * decay_in_x[:, sl])
        yield

    y = jnp.concatenate(ys, axis=1) + dsk_ref[...] * xs
    yg = y * _silu(z_ref[s])
    outs = []
    for g in range(SSD_GROUPS):
        t = yg[:, g * gw:(g + 1) * gw]
        outs.append(t * lax.rsqrt(jnp.mean(t * t, axis=-1, keepdims=True) + NORM_EPS))
    y_ref[s] = jnp.concatenate(outs, axis=1) * nrm_ref[...]

    @pl.when(c == last)
    def _():
        cnew_ref[s] = xfull_scr[s,CONV_PAD + q - 3:CONV_PAD + q, :]
        if not single_chunk:
            for g in range(SSD_GROUPS):
                hfin_ref[s,g * 8:(g + 1) * 8] = h_scr[s * SSD_GROUPS + g].T.reshape(8, SSD_HEAD_DIM, SSD_STATE)

    yield


def ssd_mixer(z, xbc, dt, dtt_flat, conv0, ssm0, w, *, q):
    b, l, _ = z.shape
    single_chunk = l == q
    want = SSD_SEQS_PER_STEP if single_chunk else SSD_LONG_SEQS_PER_STEP
    nseq = want if b % want == 0 else 1
    if q % 128 == 0 and nseq == 1:
        dtt = dtt_flat
        dtt_spec = pl.BlockSpec((SSD_HEADS, q), lambda i, c: (0, i * (l // q) + c))
    else:
        dtt = jnp.swapaxes(dt, 1, 2)
        dtt_spec = pl.BlockSpec((nseq, SSD_HEADS, q), lambda i, c: (i, 0, c))
    seq = lambda wd: pl.BlockSpec((nseq, q, wd), lambda i, c: (i, c, 0))
    per_b3 = lambda s: pl.BlockSpec((nseq,) + s, lambda i, c: (i,) + (0,) * len(s))
    col = lambda t: t.reshape(-1, 1)
    rowv = lambda t: t.reshape(1, -1)
    consts = [w["conv_w"], rowv(w["conv_b"]), rowv(w["dt_bias"]), col(w["dt_bias"]),
              rowv(w["a_log"]), col(w["a_log"]),
              rowv(jnp.repeat(w["d_skip"], SSD_HEAD_DIM)), rowv(w["ssd_norm"])]
    return pl.pallas_call(
        functools.partial(_ssd_kernel, q=q, nseq=nseq, single_chunk=single_chunk),
        grid=(b // nseq, l // q),
        in_specs=[seq(SSD_WIDTH), seq(SSD_CONV_DIM), seq(SSD_HEADS),
                  dtt_spec,
                  per_b3((SSD_CONV - 1, SSD_CONV_DIM)),
                  per_b3((SSD_HEADS, SSD_HEAD_DIM, SSD_STATE))]
                 + [_const_spec(t.shape) for t in consts],
        out_specs=[seq(SSD_WIDTH), per_b3((SSD_HEADS, SSD_HEAD_DIM, SSD_STATE)),
                   per_b3((SSD_CONV - 1, SSD_CONV_DIM))],
        out_shape=[jax.ShapeDtypeStruct((b, l, SSD_WIDTH), F32),
                   jax.ShapeDtypeStruct((b, SSD_HEADS, SSD_HEAD_DIM, SSD_STATE), F32),
                   jax.ShapeDtypeStruct((b, SSD_CONV - 1, SSD_CONV_DIM), F32)],
        scratch_shapes=[pltpu.VMEM((nseq, CONV_PAD + q, SSD_CONV_DIM), F32),
                        pltpu.VMEM((nseq * SSD_GROUPS, SSD_STATE, SSD_GROUP_WIDTH), F32)],
        compiler_params=pltpu.CompilerParams(
            dimension_semantics=("parallel", "arbitrary"), vmem_limit_bytes=V7X_VMEM_LIMIT),
        name="ssd_mixer",
    )(z, xbc, dt, dtt, conv0, ssm0, *consts)


def _head_sums(t):
    pair = 2 * RWKV_HEAD_DIM
    first = lax.broadcasted_iota(jnp.int32, (1, pair), 1) < RWKV_HEAD_DIM
    pieces = []
    for j in range(RWKV_HEADS // 2):
        x = t[:, j * pair:(j + 1) * pair]
        x0 = jnp.where(first, x, 0.0)
        s0 = jnp.sum(x0, axis=-1, keepdims=True)
        s1 = jnp.sum(x - x0, axis=-1, keepdims=True)
        pieces.append(jnp.where(first, s0, s1))
    return jnp.concatenate(pieces, axis=1)


def _rwkv_prep_kernel(rw_ref, sh0_ref, mu_ref, w0_ref, w2_ref, a0_ref, a2_ref, g2_ref,
                      kk_ref, ka_ref, rk_ref,
                      *rest, tt, seqs, chunk):
    outs, full_scr = rest[:-1], rest[-1]
    shn_ref = outs[-1]
    c = pl.program_id(1)
    l = tt // seqs

    if seqs == 1:
        @pl.when(c == 0)
        def _():
            full_scr[0, CONV_PAD - 1:CONV_PAD, :] = sh0_ref[0]

        @pl.when(c > 0)
        def _():
            full_scr[0, CONV_PAD - 1:CONV_PAD, :] = full_scr[0, CONV_PAD + tt - 1:CONV_PAD + tt, :]

        rw = rw_ref[0]
        full_scr[0, CONV_PAD:CONV_PAD + tt, :] = rw
        prev = full_scr[0, CONV_PAD - 1:CONV_PAD - 1 + tt, :]
    else:
        full_scr[:, CONV_PAD - 1:CONV_PAD, :] = sh0_ref[...]
        full_scr[:, CONV_PAD:CONV_PAD + l, :] = rw_ref[...]
        rw = rw_ref[...].reshape(tt, RWKV_PROJ)
        prev = full_scr[:, CONV_PAD - 1:CONV_PAD - 1 + l, :].reshape(tt, RWKV_PROJ)
    vals = _rwkv_mix_math(rw, prev, mu_ref, w0_ref, w2_ref, a0_ref, a2_ref, g2_ref, kk_ref, ka_ref, rk_ref)
    _emit_rwkv_outputs(outs[:-1], vals, tt=tt, chunk=chunk)

    if seqs == 1:
        @pl.when(c == pl.num_programs(1) - 1)
        def _():
            shn_ref[0] = full_scr[0, CONV_PAD + tt - 1:CONV_PAD + tt, :]
    else:
        shn_ref[...] = rw_ref[:, l - 1:l, :]


def _rwkv_mix_math(rw, prev, mu_ref, w0_ref, w2_ref, a0_ref, a2_ref, g2_ref, kk_ref, ka_ref, rk_ref):
    wd = RWKV_WIDTH
    u = rw + (prev - rw) * mu_ref[...]
    r = u[:, :wd]
    k = u[:, wd:2 * wd]
    v = u[:, 2 * wd:3 * wd]
    w_lo = u[:, 3 * wd:3 * wd + DECAY_LORA]
    a_lo = u[:, 3 * wd + DECAY_LORA:3 * wd + DECAY_LORA + AAA_LORA]
    g_lo = u[:, 3 * wd + DECAY_LORA + AAA_LORA:]

    w_log = -_softplus(-(w0_ref[...] + _bdot(jnp.tanh(w_lo), w2_ref[...]))) - 0.5
    lw = -jnp.exp(w_log)
    a = _sigmoid(a0_ref[...] + _bdot(a_lo, a2_ref[...]))
    gate = _bdot(_sigmoid(g_lo), g2_ref[...])

    kk = k * kk_ref[...]
    kk = kk / jnp.maximum(jnp.sqrt(_head_sums(kk * kk)), 1e-12)
    kf = k * (1.0 + (a - 1.0) * ka_ref[...])
    kb = kk * a
    bonus = _head_sums(r * kf * rk_ref[...]) * v
    return r, lw, kf, v, kk, kb, gate, bonus


def _emit_rwkv_outputs(outs, vals, *, tt, chunk, row0=0):
    r, lw, kf, v, kk, kb, gate, bonus = vals
    wd = RWKV_WIDTH
    gate_out, bonus_out = outs[-2:]
    if chunk == 0:
        blk = gate_out.shape
        for ref, val in zip(outs, (r, jnp.exp(lw), kf, v, kk, kb, gate, bonus)):
            ref[...] = val.reshape(blk)
    else:
        gate_out[0, row0:row0 + tt, :] = gate
        bonus_out[0, row0:row0 + tt, :] = bonus
        kkt_out, rt_out, kfh_out, nbh_out, vb_out, kbg_out, kfg_out, gend_out = outs[:8]
        tri = (lax.broadcasted_iota(jnp.int32, (chunk, chunk), 0)
               >= lax.broadcasted_iota(jnp.int32, (chunk, chunk), 1)).astype(F32)
        for ci in range(tt // chunk):
            rs = slice(ci * chunk, (ci + 1) * chunk)
            ro = slice(row0 + ci * chunk, row0 + (ci + 1) * chunk)
            lw_c = lw[rs]
            cum = _dot01(tri, lw_c, exact_side="rhs")
            cum_end = cum[chunk - 1:chunk, :]
            g_inv = jnp.exp(-cum)
            g_tail = jnp.exp(cum_end - cum)
            kkt_out[0, ro, :] = (kk[rs] * jnp.exp(cum - lw_c)).astype(BF16)
            rt_out[0, ro, :] = (r[rs] * jnp.exp(cum)).astype(BF16)
            kfh_out[0, ro, :] = (kf[rs] * g_tail).astype(BF16)
            nbh_out[0, ro, :] = (-kb[rs] * g_tail).astype(BF16)
            vb_out[0, ro, :] = v[rs].astype(BF16)
            kbg_out[0, ro, :] = (kb[rs] * g_inv).astype(BF16)
            kfg_out[0, ro, :] = (kf[rs] * g_inv).astype(BF16)
            gend_out[0, row0 // chunk + ci] = jnp.broadcast_to(jnp.exp(cum_end), (8, wd))


def rwkv_prep(rw, shift0, w, *, tt, chunk):
    b, l, _ = rw.shape
    seqs = max(1, tt // l)
    assert chunk == 0 or (seqs == 1 and tt % chunk == 0)
    rowv = lambda t: t.reshape(1, -1)
    consts = [rowv(w["shift_mu"]), rowv(w["w0"]), w["w2"].astype(BF16), rowv(w["a0"]),
              w["a2"].astype(BF16), w["g2"].astype(BF16), rowv(w["k_k"]), rowv(w["k_a"]),
              rowv(w["r_k"])]
    rows = tt // seqs
    grid = (b // seqs, l // rows)
    seq = lambda wd: pl.BlockSpec((seqs, rows, wd), lambda i, c: (i, c, 0))
    one = pl.BlockSpec((seqs, 1, RWKV_PROJ), lambda i, c: (i, 0, 0))
    sds = jax.ShapeDtypeStruct
    f32_seq = sds((b, l, RWKV_WIDTH), F32)
    if chunk == 0:
        op_specs = [seq(RWKV_WIDTH)] * 6
        op_shapes = [f32_seq] * 6
    else:
        per_tile = tt // chunk
        op_specs = [seq(RWKV_WIDTH)] * 7 + [
            pl.BlockSpec((1, per_tile, 8, RWKV_WIDTH), lambda i, c: (i, c, 0, 0))]
        op_shapes = [sds((b, l, RWKV_WIDTH), BF16)] * 7 + [sds((b, l // chunk, 8, RWKV_WIDTH), F32)]
    outs = pl.pallas_call(
        functools.partial(_rwkv_prep_kernel, tt=tt, seqs=seqs, chunk=chunk),
        grid=grid,
        in_specs=[seq(RWKV_PROJ), one] + [_const_spec(t.shape) for t in consts],
        out_specs=op_specs + [seq(RWKV_WIDTH)] * 2 + [one],
        out_shape=op_shapes + [f32_seq] * 2 + [sds((b, 1, RWKV_PROJ), F32)],
        scratch_shapes=[pltpu.VMEM((seqs, CONV_PAD + rows, RWKV_PROJ), F32)],
        compiler_params=pltpu.CompilerParams(
            dimension_semantics=("parallel", "arbitrary"), vmem_limit_bytes=V7X_VMEM_LIMIT),
        name="rwkv_prep",
    )(rw, shift0, *consts)
    return outs


def _proj_prep_kernel(x_ref, sh0_ref, g_ref, wz_ref, wx_ref, wr_ref, wdt_ref, wdtt_ref,
                      mu_ref, w0_ref, w2_ref, a0_ref, a2_ref, g2_ref, kk_ref, ka_ref, rk_ref,
                      z_ref, xbc_ref, dt_ref, dtt_ref, *rest, tm, tiles_per_seq, chunk):
    outs, (new_scr, cur_scr) = rest[:-2], rest[-2:]
    shn_ref = outs[-1]
    i = pl.program_id(0)

    @pl.when(i == 0)
    def _():
        new_scr[...] = jnp.zeros(new_scr.shape, F32)
        cur_scr[...] = jnp.zeros(cur_scr.shape, F32)

    k = jnp.maximum(i - 1, 0)
    first = (k % tiles_per_seq) == 0
    cur_scr[CONV_PAD - 1:CONV_PAD, :] = jnp.where(first, sh0_ref[0], cur_scr[CONV_PAD + tm - 1:CONV_PAD + tm, :])
    cur_scr[CONV_PAD:CONV_PAD + tm, :] = new_scr[...]

    shn_ref[0] = cur_scr[CONV_PAD + tm - 1:CONV_PAD + tm, :]
    u = _rms(x_ref[...], g_ref[...]).astype(BF16)

    def project(piece):
        if piece == 0:
            z_ref[...] = jnp.dot(u, wz_ref[...], preferred_element_type=F32)
        elif piece == 1:
            xbc_ref[...] = jnp.dot(u, wx_ref[...], preferred_element_type=F32)
        elif piece == 2:
            cols = slice(0, 2 * RWKV_WIDTH)
            new_scr[:, cols] = jnp.dot(u, wr_ref[:, cols], preferred_element_type=F32)
        else:
            cols = slice(2 * RWKV_WIDTH, RWKV_PROJ)
            new_scr[:, cols] = jnp.dot(u, wr_ref[:, cols], preferred_element_type=F32)
            dt_ref[...] = jnp.dot(u, wdt_ref[...], preferred_element_type=F32)
            dtt_ref[...] = lax.dot_general(wdtt_ref[...], u, (((1,), (1,)), ((), ())),
                                           preferred_element_type=F32)

    def prepare(part, rows):
        lo = CONV_PAD + part * rows
        padded = cur_scr[lo - CONV_PAD:lo + rows, :]
        rw = padded[CONV_PAD:]
        prev = pltpu.roll(padded, 1, axis=0)[CONV_PAD:]
        vals = _rwkv_mix_math(rw, prev, mu_ref, w0_ref, w2_ref, a0_ref, a2_ref, g2_ref, kk_ref, ka_ref, rk_ref)
        _emit_rwkv_outputs(outs[:-1], vals, tt=rows, chunk=chunk, row0=part * rows)

    pieces = 4
    parts = min(pieces, tm // chunk)
    for part in range(parts):
        for piece in range(part * pieces // parts, (part + 1) * pieces // parts):
            project(piece)
        prepare(part, tm // parts)


def proj_prep(x, shift0, g, wz, wx, wr, wdt, w, *, tm, chunk):
    b, l, _ = x.shape
    n = b * l
    nt, tps = n // tm, l // tm
    assert l % tm == 0 and tm % chunk == 0
    x2 = x.reshape(n, D_MODEL)
    rowv = lambda t: t.reshape(1, -1)
    wdtt = wdt.T
    consts = [g, wz, wx, wr, wdt, wdtt,
              rowv(w["shift_mu"]), rowv(w["w0"]), w["w2"].astype(BF16), rowv(w["a0"]),
              w["a2"].astype(BF16), w["g2"].astype(BF16), rowv(w["k_k"]), rowv(w["k_a"]), rowv(w["r_k"])]
    ahead = lambda i: jnp.minimum(i, nt - 1)
    behind = lambda i: jnp.maximum(i - 1, 0)
    row_a = lambda w_: pl.BlockSpec((tm, w_), lambda i: (ahead(i), 0))
    seq_b = lambda w_: pl.BlockSpec((1, tm, w_), lambda i: (behind(i) // tps, behind(i) % tps, 0))
    one_b = pl.BlockSpec((1, 1, RWKV_PROJ), lambda i: (behind(i) // tps, 0, 0))
    per_tile = tm // chunk
    gend_spec = pl.BlockSpec((1, per_tile, 8, RWKV_WIDTH), lambda i: (behind(i) // tps, behind(i) % tps, 0, 0))
    sds = jax.ShapeDtypeStruct
    outs = pl.pallas_call(
        functools.partial(_proj_prep_kernel, tm=tm, tiles_per_seq=tps, chunk=chunk),
        grid=(nt + 1,),
        in_specs=[row_a(D_MODEL), one_b] + [_const_spec(t.shape) for t in consts],
        out_specs=[row_a(SSD_WIDTH), row_a(SSD_CONV_DIM), row_a(SSD_HEADS),
                   pl.BlockSpec((SSD_HEADS, tm), lambda i: (0, ahead(i)))]
                  + [seq_b(RWKV_WIDTH)] * 7 + [gend_spec] + [seq_b(RWKV_WIDTH)] * 2 + [one_b],
        out_shape=[sds((n, SSD_WIDTH), F32), sds((n, SSD_CONV_DIM), F32), sds((n, SSD_HEADS), F32),
                   sds((SSD_HEADS, n), F32)]
                  + [sds((b, l, RWKV_WIDTH), BF16)] * 7 + [sds((b, l // chunk, 8, RWKV_WIDTH), F32)]
                  + [sds((b, l, RWKV_WIDTH), F32)] * 2 + [sds((b, 1, RWKV_PROJ), F32)],
        scratch_shapes=[pltpu.VMEM((tm, RWKV_PROJ), F32), pltpu.VMEM((CONV_PAD + tm, RWKV_PROJ), F32)],
        compiler_params=pltpu.CompilerParams(
            dimension_semantics=("arbitrary",), vmem_limit_bytes=V7X_VMEM_LIMIT),
        name="proj_prep",
    )(x2, shift0, *consts)
    return outs


def _wkv_kernel(r_ref, w_ref, k_ref, v_ref, kk_ref, kka_ref, s0_ref,
                o_ref, sfin_ref, s_scr, vt_scr, ot_scr, *, steps):
    c = pl.program_id(1)
    n = RWKV_HEAD_DIM
    lanes = WKV_BATCH_BLOCK * RWKV_HEADS

    @pl.when(c == 0)
    def _():
        s_scr[...] = s0_ref[...].reshape(lanes, n * n).T.reshape(n, n, lanes)

    def to_pairs(ref, t):
        return ref[:, t].reshape(lanes, n).T

    def step(t, carry):
        r_t = to_pairs(r_ref, t)
        w_t = to_pairs(w_ref, t)
        k_t = to_pairs(k_ref, t)
        kk_t = to_pairs(kk_ref, t)
        kka_t = to_pairs(kka_ref, t)
        vt_scr[...] = to_pairs(v_ref, t)

        def per_value(vi, carry2):
            s_v = s_scr[vi]
            skk = jnp.sum(s_v * kk_t, axis=0, keepdims=True)
            v_row = vt_scr[pl.ds(vi, 1), :]
            s_new = s_v * w_t - skk * kka_t + v_row * k_t
            s_scr[vi] = s_new
            ot_scr[pl.ds(vi, 1), :] = jnp.sum(s_new * r_t, axis=0, keepdims=True)
            return carry2

        lax.fori_loop(0, n, per_value, 0, unroll=4)
        o_ref[:, t] = ot_scr[...].T.reshape(WKV_BATCH_BLOCK, RWKV_HEADS, n)
        return carry

    lax.fori_loop(0, steps, step, 0)

    @pl.when(c == pl.num_programs(1) - 1)
    def _():
        sfin_ref[...] = s_scr[...].reshape(n * n, lanes).T.reshape(
            WKV_BATCH_BLOCK, RWKV_HEADS, n, n)


def wkv_scan(r, w, k, v, kk, kka, s0, *, steps):
    b, l, _ = r.shape
    h, n = RWKV_HEADS, RWKV_HEAD_DIM
    assert b % WKV_BATCH_BLOCK == 0 and l % steps == 0
    ops = [t.reshape(b, l, h, n) for t in (r, w, k, v, kk, kka)]
    seq_spec = pl.BlockSpec((WKV_BATCH_BLOCK, steps, h, n), lambda g, c: (g, c, 0, 0))
    st_spec = pl.BlockSpec((WKV_BATCH_BLOCK, h, n, n), lambda g, c: (g, 0, 0, 0))
    o, s_fin = pl.pallas_call(
        functools.partial(_wkv_kernel, steps=steps),
        grid=(b // WKV_BATCH_BLOCK, l // steps),
        in_specs=[seq_spec] * 6 + [st_spec],
        out_specs=[seq_spec, st_spec],
        out_shape=[jax.ShapeDtypeStruct((b, l, h, n), F32),
                   jax.ShapeDtypeStruct((b, h, n, n), F32)],
        scratch_shapes=[pltpu.VMEM((n, n, WKV_BATCH_BLOCK * h), F32),
                        pltpu.VMEM((n, WKV_BATCH_BLOCK * h), F32),
                        pltpu.VMEM((n, WKV_BATCH_BLOCK * h), F32)],
        compiler_params=pltpu.CompilerParams(
            dimension_semantics=("parallel", "arbitrary"), vmem_limit_bytes=V7X_VMEM_LIMIT),
        name="wkv_scan",
    )(*ops, s0)
    return o.reshape(b, l, h * n), s_fin


WKV_LANE_BATCH = 128


def _wkv_batch_lanes_kernel(r_ref, w_ref, k_ref, v_ref, kk_ref, kka_ref, s0_ref,
                            o_ref, sfin_ref, op_scr, ot_scr, *, steps):
    n = RWKV_HEAD_DIM
    nb = WKV_LANE_BATCH
    sfin_ref[...] = s0_ref[...]

    def step(t, carry):
        rows = pl.ds(t, nb, stride=steps)
        for i, ref in enumerate((r_ref, w_ref, k_ref, kk_ref, kka_ref, v_ref)):
            op_scr[i] = ref[rows, :].T
        for h2 in range(2):
            ch = slice(h2 * n, (h2 + 1) * n)
            r_t = op_scr[0, ch, :]
            kka_r = jnp.sum(op_scr[4, ch, :] * r_t, axis=0, keepdims=True)
            k_r = jnp.sum(op_scr[2, ch, :] * r_t, axis=0, keepdims=True)
            op_scr[0, ch, :] = op_scr[1, ch, :] * r_t

            def per_value(vi, carry2, ch=ch, h2=h2, kka_r=kka_r, k_r=k_r):
                s_v = sfin_ref[h2, vi]
                skk = jnp.sum(s_v * op_scr[3, ch, :], axis=0, keepdims=True)
                out = jnp.sum(s_v * op_scr[0, ch, :], axis=0, keepdims=True)
                v_row = op_scr[5, pl.ds(h2 * n + vi, 1), :]
                sfin_ref[h2, vi] = s_v * op_scr[1, ch, :] - skk * op_scr[4, ch, :] + v_row * op_scr[2, ch, :]
                ot_scr[pl.ds(h2 * n + vi, 1), :] = out - skk * kka_r + v_row * k_r
                return carry2

            lax.fori_loop(0, n, per_value, 0, unroll=8)
        o_ref[rows, :] = ot_scr[...].T
        return carry

    lax.fori_loop(0, steps, step, 0)


def wkv_scan_batch_lanes(r, w, k, v, kk, kka, s0):
    b, l, wd = r.shape
    h, n, nb = RWKV_HEADS, RWKV_HEAD_DIM, WKV_LANE_BATCH
    assert b % nb == 0
    ops = [t.reshape(b * l, wd) for t in (r, w, k, v, kk, kka)]
    s0t = jnp.transpose(s0, (1, 2, 3, 0))
    seq_spec = pl.BlockSpec((nb * l, 2 * n), lambda g, j: (g, j))
    st_spec = pl.BlockSpec((2, n, n, nb), lambda g, j: (j, 0, 0, g))
    o, s_fin = pl.pallas_call(
        functools.partial(_wkv_batch_lanes_kernel, steps=l),
        grid=(b // nb, h // 2),
        in_specs=[seq_spec] * 6 + [st_spec],
        out_specs=[seq_spec, st_spec],
        out_shape=[jax.ShapeDtypeStruct((b * l, wd), F32), jax.ShapeDtypeStruct((h, n, n, b), F32)],
        scratch_shapes=[pltpu.VMEM((6, 2 * n, nb), F32), pltpu.VMEM((2 * n, nb), F32)],
        compiler_params=pltpu.CompilerParams(
            dimension_semantics=("parallel", "parallel"), vmem_limit_bytes=V7X_VMEM_LIMIT),
        name="wkv_scan_batch_lanes",
    )(*ops, s0t)
    return o.reshape(b, l, wd), jnp.transpose(s_fin, (3, 0, 1, 2))


WKV_CHUNK = 64
WKV_PAIRS = RWKV_HEADS // 2
WKV_ROW_STRIDE = WKV_CHUNK + 8
WKV_LOCKSTEP = 8
WKV_SOLVE_ROWS = 8
WKV_SOLVE_COLS = 16


def _pair_masks():
    c = WKV_CHUNK
    row = lax.broadcasted_iota(jnp.int32, (2 * c, 2 * c), 0)
    col = lax.broadcasted_iota(jnp.int32, (2 * c, 2 * c), 1)
    t, i = row % c, col % c
    keep = i <= t - jnp.where(row < c, 1, 0)
    sign = jnp.where(row >= c, jnp.where(col < c, -1.0, 1.0), 1.0)
    block_diag = row // c == col // c
    return keep, sign, block_diag


def _wkv_prepare_kernel(kkt_ref, rt_ref, kbg_ref, kfg_ref, vb_ref,
                        lo_ref, rhs0_ref, tp_ref, abs_scr, top_scr, abt_scr, tt_scr):
    c = WKV_CHUNK
    keep, sign, _ = _pair_masks()
    lane = lax.broadcasted_iota(jnp.int32, (1, 2 * c), 1)
    head0 = lane < RWKV_HEAD_DIM
    row_head0 = lax.broadcasted_iota(jnp.int32, (2 * RWKV_HEAD_DIM, 1), 0) < RWKV_HEAD_DIM
    zeros = jnp.zeros((c, 2 * c), BF16)

    def one_batch(b, k):
        for j in range(WKV_PAIRS):
            sl = slice(j * 2 * RWKV_HEAD_DIM, (j + 1) * 2 * RWKV_HEAD_DIM)
            lhs = jnp.concatenate([kkt_ref[b, :, sl], rt_ref[b, :, sl]], axis=0)
            rhs = jnp.concatenate([kbg_ref[b, :, sl], kfg_ref[b, :, sl]], axis=0)
            rhs_t = rhs.astype(F32).T
            top = jnp.where(row_head0, rhs_t, 0.0)
            w_a = jnp.concatenate([top, rhs_t - top], axis=1).astype(BF16)
            a_both = jnp.dot(lhs, w_a, preferred_element_type=F32)
            for h2 in range(2):
                a = a_both[:, h2 * 2 * c:(h2 + 1) * 2 * c]
                a = jnp.where(keep, a, 0.0) * sign
                inst = h2 * (WKV_BATCH_BLOCK * WKV_PAIRS) + b * WKV_PAIRS + j
                abs_scr[pl.ds(inst * WKV_ROW_STRIDE, c), :c] = a[:c, :c]
                top_scr[k, j, :, h2 * 2 * c:(h2 + 1) * 2 * c] = a[:c].astype(BF16)
                lo_ref[b, :, (2 * j + h2) * 2 * c:(2 * j + h2 + 1) * 2 * c] = a[c:].astype(BF16)
        yield
        for j in range(WKV_PAIRS):
            sl = slice(j * 2 * RWKV_HEAD_DIM, (j + 1) * 2 * RWKV_HEAD_DIM)
            v = vb_ref[b, :, sl]
            v0 = jnp.where(head0, v, jnp.zeros_like(v))
            w_akf = jnp.concatenate([zeros, v0, zeros, v - v0], axis=0)
            rhs0_ref[b, :, sl] = jnp.dot(top_scr[k, j], w_akf, preferred_element_type=F32)
        yield

    def two_batches(bb, carry):
        for _ in zip(*[one_batch(WKV_LOCKSTEP * bb + k, k) for k in range(WKV_LOCKSTEP)]):
            pass
        return carry

    lax.fori_loop(0, WKV_BATCH_BLOCK // WKV_LOCKSTEP, two_batches, 0)

    n_inst = 2 * WKV_BATCH_BLOCK * WKV_PAIRS
    n_pair_rows = WKV_BATCH_BLOCK * WKV_PAIRS

    def to_lanes(t, carry):
        abt_scr[t] = abs_scr[pl.ds(t, n_inst, stride=WKV_ROW_STRIDE), :][:, :c].T
        return carry

    lax.fori_loop(0, c, to_lanes, 0, unroll=16)

    tt_scr[...] = jnp.zeros(tt_scr.shape, F32)
    sub_iota = lax.broadcasted_iota(jnp.int32, (WKV_SOLVE_COLS, n_inst), 0)
    rows = range(WKV_SOLVE_ROWS)
    for cb in range(c // WKV_SOLVE_COLS):
        col0 = WKV_SOLVE_COLS * cb
        cols = slice(col0, col0 + WKV_SOLVE_COLS)
        first_block = col0 // WKV_SOLVE_ROWS

        def solve_rows(tb, carry, col0=col0, cols=cols, first_block=first_block):
            t0 = tb * WKV_SOLVE_ROWS

            def sub(ib, accs):
                ps = [tt_scr[ib * WKV_SOLVE_ROWS + di, cols, :] for di in rows]
                out = []
                for r in rows:
                    terms = [abt_scr[t0 + r, pl.ds(ib * WKV_SOLVE_ROWS + di, 1), :] * ps[di] for di in rows]
                    while len(terms) > 1:
                        terms = [a + b for a, b in zip(terms[::2], terms[1::2])]
                    out.append(accs[r] - terms[0])
                return tuple(out)

            unit = tuple(jnp.where(sub_iota + col0 == t0 + r, 1.0, 0.0) for r in rows)
            accs = list(lax.fori_loop(first_block, tb, sub, unit))
            for r in rows:
                for r2 in range(r):
                    accs[r] = accs[r] - abt_scr[t0 + r, pl.ds(t0 + r2, 1), :] * accs[r2]
                tt_scr[t0 + r, cols, :] = accs[r]
            return carry

        lax.fori_loop(first_block, c // WKV_SOLVE_ROWS, solve_rows, 0)

    def from_lanes(t, carry):
        m = tt_scr[t].T
        abs_scr[pl.ds(t, n_pair_rows, stride=WKV_ROW_STRIDE), :] = jnp.concatenate(
            [m[:n_pair_rows], m[n_pair_rows:]], axis=1)
        return carry

    lax.fori_loop(0, c, from_lanes, 0, unroll=32)

    def emit(b, carry):
        for j in range(WKV_PAIRS):
            row0 = (b * WKV_PAIRS + j) * WKV_ROW_STRIDE
            tp_ref[b, :, j * 2 * c:(j + 1) * 2 * c] = abs_scr[pl.ds(row0, c), :].astype(BF16)
        return carry

    lax.fori_loop(0, WKV_BATCH_BLOCK, emit, 0)


def _wkv_apply_kernel(kkt_ref, rt_ref, kfh_ref, nbh_ref, vb_ref, tp_ref, lo_ref, rhs0_ref, gend_ref, s0_ref,
                      o_ref, sfin_ref, x_scr, wp_scr, p_scr):
    c = WKV_CHUNK
    n = RWKV_HEAD_DIM
    ch = pl.program_id(1)
    _, _, block_diag = _pair_masks()
    lane = lax.broadcasted_iota(jnp.int32, (1, 2 * c), 1)
    head0 = lane < n
    eye2 = (lax.broadcasted_iota(jnp.int32, (n, 2 * n), 0)
            == lax.broadcasted_iota(jnp.int32, (n, 2 * n), 1) % n).astype(F32)

    @pl.when(ch == 0)
    def _():
        def init(b, carry):
            for j in range(WKV_PAIRS):
                sp = s0_ref[b, 2 * j:2 * j + 2].reshape(2 * n, n)
                dup = jnp.dot(sp, eye2, precision=HIGHEST, preferred_element_type=F32)
                x_scr[b, j] = jnp.where(block_diag, dup, 0.0)
            return carry
        lax.fori_loop(0, WKV_BATCH_BLOCK, init, 0)

    def one_batch(b, k):
        for j in range(WKV_PAIRS):
            sl = slice(j * 2 * n, (j + 1) * 2 * n)
            lhs = jnp.concatenate([kkt_ref[b, :, sl], rt_ref[b, :, sl]], axis=0)
            kx = lax.dot_general(lhs, x_scr[b, j].astype(BF16), (((1,), (1,)), ((), ())),
                                 preferred_element_type=F32)
            rhs = kx[:c] + rhs0_ref[b, :, sl]
            r0 = jnp.where(head0, rhs, 0.0)
            wp_scr[k, j] = jnp.concatenate([r0, rhs - r0], axis=0).astype(BF16)
            o_ref[b, :, sl] = kx[c:]
        yield
        for j in range(WKV_PAIRS):
            sl = slice(j * 2 * n, (j + 1) * 2 * n)
            p = jnp.dot(tp_ref[b, :, sl], wp_scr[k, j], preferred_element_type=F32)
            p_scr[k, j] = p.astype(BF16)
        yield
        for j in range(WKV_PAIRS):
            sl = slice(j * 2 * n, (j + 1) * 2 * n)
            v = vb_ref[b, :, sl]
            pb = p_scr[k, j]
            zero = jnp.zeros_like(pb)
            p0, v0 = jnp.where(head0, pb, zero), jnp.where(head0, v, zero)
            w_o = jnp.concatenate([p0, v0, pb - p0, v - v0], axis=0)
            o_ref[b, :, sl] = o_ref[b, :, sl] + jnp.dot(
                lo_ref[b, :, j * 4 * c:(j + 1) * 4 * c], w_o, preferred_element_type=F32)
            vp = jnp.concatenate([v, pb], axis=0)
            kb = jnp.concatenate([kfh_ref[b, :, sl], nbh_ref[b, :, sl]], axis=0)
            upd = lax.dot_general(vp, kb, (((0,), (0,)), ((), ())), preferred_element_type=F32)
            x_scr[b, j] = jnp.where(block_diag, x_scr[b, j] * gend_ref[b, 0, 0:1, sl] + upd, 0.0)
        yield

    def two_batches(bb, carry):
        for _ in zip(*[one_batch(WKV_LOCKSTEP * bb + k, k) for k in range(WKV_LOCKSTEP)]):
            pass
        return carry

    lax.fori_loop(0, WKV_BATCH_BLOCK // WKV_LOCKSTEP, two_batches, 0)

    @pl.when(ch == pl.num_programs(1) - 1)
    def _():
        def fin(b, carry):
            for j in range(WKV_PAIRS):
                sp = lax.dot_general(x_scr[b, j], eye2, (((1,), (1,)), ((), ())),
                                     precision=HIGHEST, preferred_element_type=F32)
                sfin_ref[b, 2 * j:2 * j + 2] = sp.reshape(2, n, n)
            return carry
        lax.fori_loop(0, WKV_BATCH_BLOCK, fin, 0)


def wkv_chunked(kkt, rt, kfh, nbh, vb, kbg, kfg, gend, s0):
    b, l, wd = kkt.shape
    c = WKV_CHUNK
    assert b % WKV_BATCH_BLOCK == 0 and l % c == 0
    gb, nc = b // WKV_BATCH_BLOCK, l // c
    lanes = 2 * WKV_BATCH_BLOCK * WKV_PAIRS
    seq = lambda w_: pl.BlockSpec((WKV_BATCH_BLOCK, c, w_), lambda g, i: (g, i, 0))
    gend_spec = pl.BlockSpec((WKV_BATCH_BLOCK, 1, 8, wd), lambda g, i: (g, i, 0, 0))
    sds = jax.ShapeDtypeStruct
    st_spec = pl.BlockSpec((WKV_BATCH_BLOCK, RWKV_HEADS, RWKV_HEAD_DIM, RWKV_HEAD_DIM),
                           lambda g, i: (g, 0, 0, 0))
    blk = (WKV_BATCH_BLOCK, c, wd)
    o, s_fin = pl.pallas_call(
        _wkv_chunk_kernel,
        grid=(gb, nc),
        in_specs=[seq(wd)] * 7 + [gend_spec, st_spec],
        out_specs=[seq(wd), st_spec],
        out_shape=[sds((b, l, wd), F32), sds(s0.shape, F32)],
        scratch_shapes=[
            pltpu.VMEM((lanes * WKV_ROW_STRIDE, 2 * c), F32),
            pltpu.VMEM((WKV_LOCKSTEP, WKV_PAIRS, c, 4 * c), BF16),
            pltpu.VMEM((c, c, lanes), F32),
            pltpu.VMEM((c, c, lanes), F32),
            pltpu.VMEM((WKV_BATCH_BLOCK, c, 2 * wd), BF16),
            pltpu.VMEM(blk, F32),
            pltpu.VMEM(blk, BF16),
            pltpu.VMEM((WKV_BATCH_BLOCK, WKV_PAIRS, 2 * RWKV_HEAD_DIM, 2 * RWKV_HEAD_DIM), F32),
            pltpu.VMEM((WKV_LOCKSTEP, WKV_PAIRS, 2 * c, 2 * RWKV_HEAD_DIM), BF16),
            pltpu.VMEM((WKV_LOCKSTEP, WKV_PAIRS, c, 2 * RWKV_HEAD_DIM), BF16)],
        compiler_params=pltpu.CompilerParams(
            dimension_semantics=("parallel", "arbitrary"), vmem_limit_bytes=V7X_VMEM_LIMIT),
        name="wkv_chunked",
    )(kkt, rt, kfh, nbh, vb, kbg, kfg, gend, s0)
    return o, s_fin


def _wkv_chunk_kernel(kkt_ref, rt_ref, kfh_ref, nbh_ref, vb_ref, kbg_ref, kfg_ref, gend_ref, s0_ref,
                      o_ref, sfin_ref, abs_scr, top_scr, abt_scr, tt_scr, lo_scr, rhs0_scr, tp_scr,
                      x_scr, wp_scr, p_scr):
    _wkv_prepare_kernel(kkt_ref, rt_ref, kbg_ref, kfg_ref, vb_ref, lo_scr, rhs0_scr, tp_scr,
                        abs_scr, top_scr, abt_scr, tt_scr)
    _wkv_apply_kernel(kkt_ref, rt_ref, kfh_ref, nbh_ref, vb_ref, tp_scr, lo_scr, rhs0_scr, gend_ref, s0_ref,
                      o_ref, sfin_ref, x_scr, wp_scr, p_scr)


def _tail_kernel(o_ref, gate_ref, bonus_ref, x_ref, ys_ref, p_ref, lnw_ref, lnb_ref, woa_ref, wob_ref,
                 nf_ref, wg_ref, wu_ref, wd_ref, np_ref, wpg_ref, wpp_ref, nl_ref, y_ref, new_scr, cur_scr):
    i = pl.program_id(0)

    @pl.when(i == 0)
    def _():
        new_scr[...] = jnp.zeros(new_scr.shape, BF16)

    cur_scr[...] = new_scr[...]

    parts = 4
    rows = o_ref.shape[0] // parts
    inv_n = 1.0 / RWKV_HEAD_DIM

    def vector_half(part):
        rs = slice(part * rows, (part + 1) * rows)
        o = o_ref[rs, :]
        mu = _head_sums(o) * inv_n
        d = o - mu
        var = _head_sums(d * d) * inv_n
        on = d * lax.rsqrt(var + GN_EPS) * lnw_ref[...] + lnb_ref[...]
        new_scr[rs, :] = ((on + bonus_ref[rs, :]) * gate_ref[rs, :]).astype(BF16)

    vector_half(0)
    y_rwkv = cur_scr[...]
    h = x_ref[...] + _bdot(ys_ref[...], woa_ref[...]) + jnp.dot(y_rwkv, wob_ref[...],
                                                                preferred_element_type=F32)
    hf = _rms(h, nf_ref[...]).astype(BF16)
    vector_half(1)
    gate = jnp.dot(hf, wg_ref[...], preferred_element_type=F32)
    up = jnp.dot(hf, wu_ref[...], preferred_element_type=F32)
    vector_half(2)
    h = h + _bdot(_silu(gate) * up, wd_ref[...])
    vector_half(3)
    pg = _sigmoid(_bdot(_rms(h, np_ref[...]), wpg_ref[...]))
    h = h + pg * _bdot(p_ref[...], wpp_ref[...])
    y_ref[...] = _rms(h, nl_ref[...])


def layer_tail(o, gate, bonus, x, y_ssd, p, consts, *, tm):
    n = x.shape[0]
    nt = n // tm
    ahead = lambda w_: pl.BlockSpec((tm, w_), lambda i: (jnp.minimum(i, nt - 1), 0))
    behind = lambda w_: pl.BlockSpec((tm, w_), lambda i: (jnp.maximum(i - 1, 0), 0))
    return pl.pallas_call(
        _tail_kernel,
        grid=(nt + 1,),
        in_specs=[ahead(D_MODEL)] * 3 + [behind(D_MODEL)] * 2 + [behind(PLE_DIM)]
                 + [_const_spec(t.shape) for t in consts],
        out_specs=behind(D_MODEL),
        out_shape=jax.ShapeDtypeStruct((n, D_MODEL), F32),
        scratch_shapes=[pltpu.VMEM((tm, D_MODEL), BF16), pltpu.VMEM((tm, D_MODEL), BF16)],
        compiler_params=pltpu.CompilerParams(
            dimension_semantics=("arbitrary",), vmem_limit_bytes=V7X_VMEM_LIMIT),
        name="layer_tail",
    )(o, gate, bonus, x, y_ssd, p, *consts)


def _prepare_weights(w):
    c0, c1, c2 = SSD_WIDTH, SSD_WIDTH + SSD_CONV_DIM, SSD_WIDTH + SSD_CONV_DIM + SSD_HEADS
    w_in = w["w_in"]
    rowv = lambda t: t.reshape(1, -1)
    return dict(
        w,
        wz=w_in[:, :c0].astype(BF16), wx=w_in[:, c0:c1].astype(BF16),
        wdt=w_in[:, c1:c2].astype(BF16), wr=w_in[:, c2:].astype(BF16),
        woa=w["w_out"][:SSD_WIDTH].astype(BF16), wob=w["w_out"][SSD_WIDTH:].astype(BF16),
        wg=w["w_gate"].astype(BF16), wu=w["w_up"].astype(BF16), wd=w["w_down"].astype(BF16),
        wpg=w["w_ple_gate"].astype(BF16), wpp=w["w_ple_proj"].astype(BF16),
        norm_mix_r=rowv(w["norm_mix"]), norm_ffn_r=rowv(w["norm_ffn"]),
        norm_ple_r=rowv(w["norm_ple"]), norm_final_r=rowv(w["norm_final"]),
        ln_x_w_r=rowv(w["ln_x_w"]), ln_x_b_r=rowv(w["ln_x_b"]),
    )


TOKEN_TILE = 256
PREP_TILE = 128
SCAN_STEPS = 16


def layer_forward(x, p, conv0, shift0, ssm0, wkv0, w):
    b, l, _ = x.shape
    n = b * l
    tm = min(TOKEN_TILE, n)
    ssd_q = SSD_CHUNK if l % SSD_CHUNK == 0 else l
    prep_tt = min(PREP_TILE, n)
    wkv_steps = min(SCAN_STEPS, l)
    x2 = x.reshape(n, D_MODEL)
    chunked = l % WKV_CHUNK == 0 and l % tm == 0
    if chunked:
        z, xbc, dt, dtt, *ops, gate, bonus, shift_new = proj_prep(
            x, shift0, w["norm_mix_r"], w["wz"], w["wx"], w["wr"], w["wdt"], w, tm=tm, chunk=WKV_CHUNK)
    else:
        z, xbc, rw, dt, dtt = in_projection(x2, w["norm_mix_r"], w["wz"], w["wx"], w["wr"], w["wdt"], tm=tm)
        *ops, gate, bonus, shift_new = rwkv_prep(rw.reshape(b, l, -1), shift0, w, tt=prep_tt, chunk=0)
    y_ssd, ssm_new, conv_new = ssd_mixer(
        z.reshape(b, l, -1), xbc.reshape(b, l, -1), dt.reshape(b, l, -1), dtt, conv0, ssm0, w, q=ssd_q)
    if chunked:
        o, wkv_new = wkv_chunked(*ops, wkv0)
    elif b % WKV_LANE_BATCH == 0:
        o, wkv_new = wkv_scan_batch_lanes(*ops, wkv0)
    else:
        o, wkv_new = wkv_scan(*ops, wkv0, steps=wkv_steps)
    flat = lambda t: t.reshape(n, -1)
    tail_consts = [w["ln_x_w_r"], w["ln_x_b_r"], w["woa"], w["wob"], w["norm_ffn_r"], w["wg"], w["wu"],
                   w["wd"], w["norm_ple_r"], w["wpg"], w["wpp"], w["norm_final_r"]]
    y = layer_tail(flat(o), flat(gate), flat(bonus), x2, flat(y_ssd), p.reshape(n, PLE_DIM),
                   tail_consts, tm=tm)
    return y.reshape(b, l, D_MODEL), ssm_new, conv_new, wkv_new, shift_new


def kernel(x_prompt, x_sample, state_ssm, state_conv, state_wkv, state_shift, p_prompt, p_sample, norm_mix, w_in, conv_w, conv_b, dt_bias, a_log, d_skip, ssd_norm, shift_mu, w0, w2, a0, a2, g2, k_k, k_a, r_k, ln_x_w, ln_x_b, w_out, norm_ffn, w_gate, w_up, w_down, norm_ple, w_ple_gate, w_ple_proj, norm_final):
    w = _prepare_weights(dict(
        norm_mix=norm_mix[0], w_in=w_in[0], conv_w=conv_w[0], conv_b=conv_b[0], dt_bias=dt_bias[0],
        a_log=a_log[0], d_skip=d_skip[0], ssd_norm=ssd_norm[0], shift_mu=shift_mu[0], w0=w0[0],
        w2=w2[0], a0=a0[0], a2=a2[0], g2=g2[0], k_k=k_k[0], k_a=k_a[0], r_k=r_k[0],
        ln_x_w=ln_x_w[0], ln_x_b=ln_x_b[0], w_out=w_out[0], norm_ffn=norm_ffn[0],
        w_gate=w_gate[0], w_up=w_up[0], w_down=w_down[0], norm_ple=norm_ple[0],
        w_ple_gate=w_ple_gate[0], w_ple_proj=w_ple_proj[0], norm_final=norm_final))
    bp = x_prompt.shape[0]
    zeros = lambda *s: jnp.zeros(s, F32)
    yp, s1, c1, k1, t1 = layer_forward(
        x_prompt, p_prompt[0], zeros(bp, SSD_CONV - 1, SSD_CONV_DIM), zeros(bp, 1, RWKV_PROJ),
        zeros(bp, SSD_HEADS, SSD_HEAD_DIM, SSD_STATE),
        zeros(bp, RWKV_HEADS, RWKV_HEAD_DIM, RWKV_HEAD_DIM), w)
    ys, s2, c2, k2, t2 = layer_forward(
        x_sample, p_sample[0], state_conv[0], state_shift[0], state_ssm[0], state_wkv[0], w)
    return (yp, ys, s1[None], c1[None], k1[None], t1[None], s2[None], c2[None], k2[None], t2[None])
```

```python
import functools

import jax
import jax.numpy as jnp
from jax import lax
from jax.experimental import pallas as pl
from jax.experimental.pallas import tpu as pltpu

F32 = jnp.float32
BF16 = jnp.bfloat16
HIGHEST = lax.Precision.HIGHEST

D_MODEL = 1024
SSD_WIDTH = 1024
SSD_HEADS = 16
SSD_HEAD_DIM = 64
SSD_GROUPS = 2
SSD_GROUP_WIDTH = SSD_WIDTH // SSD_GROUPS
SSD_STATE = 128
SSD_CONV = 4
SSD_CHUNK = 128
SSD_BC = SSD_GROUPS * SSD_STATE
SSD_CONV_DIM = SSD_WIDTH + 2 * SSD_BC
RWKV_WIDTH = 1024
RWKV_HEADS = 16
RWKV_HEAD_DIM = 64
DECAY_LORA = 64
AAA_LORA = 64
GATE_LORA = 128
RWKV_PROJ = 3 * RWKV_WIDTH + DECAY_LORA + AAA_LORA + GATE_LORA
PLE_DIM = 256
NORM_EPS = 1e-6
GN_EPS = 64e-5

WKV_BATCH_BLOCK = 8
V7X_VMEM_LIMIT = 56 * 1024 * 1024
CONV_PAD = 8
SSD_SEQS_PER_STEP = 8
SSD_LONG_SEQS_PER_STEP = 1


def _rms(x, g):
    return x * lax.rsqrt(jnp.mean(x * x, axis=-1, keepdims=True) + NORM_EPS) * g


def _sigmoid(x):
    return 0.5 * jnp.tanh(0.5 * x) + 0.5


def _silu(x):
    return x * _sigmoid(x)


def _softplus(x):
    return jnp.maximum(x, 0.0) + jnp.log(1.0 + jnp.exp(-jnp.abs(x)))


def _bdot(a, b):
    return jnp.dot(a.astype(BF16), b.astype(BF16), preferred_element_type=F32)


def _split3(t):
    hi = t.astype(BF16)
    r1 = t - hi.astype(F32)
    mid = r1.astype(BF16)
    lo = (r1 - mid.astype(F32)).astype(BF16)
    return hi, mid, lo


def _dot01(a, b, *, exact_side):
    if exact_side == "lhs":
        m = b.astype(BF16)
        return sum(jnp.dot(p, m, preferred_element_type=F32) for p in _split3(a))
    m = a.astype(BF16)
    return sum(jnp.dot(m, p, preferred_element_type=F32) for p in _split3(b))


def _const_spec(shape):
    return pl.BlockSpec(shape, lambda *_: (0,) * len(shape), pipeline_mode=pl.Buffered(1))


def _head_expand(rows):
    h = lax.broadcasted_iota(jnp.int32, (rows, SSD_WIDTH), 0)
    c = lax.broadcasted_iota(jnp.int32, (rows, SSD_WIDTH), 1)
    return (c // SSD_HEAD_DIM == h).astype(F32)


def _proj_kernel(x_ref, g_ref, wz_ref, wx_ref, wr_ref, wdt_ref, wdtt_ref,
                 z_ref, xbc_ref, rw_ref, dt_ref, dtt_ref):
    u = _rms(x_ref[...], g_ref[...]).astype(BF16)
    z_ref[...] = jnp.dot(u, wz_ref[...], preferred_element_type=F32)
    xbc_ref[...] = jnp.dot(u, wx_ref[...], preferred_element_type=F32)
    rw_ref[...] = jnp.dot(u, wr_ref[...], preferred_element_type=F32)
    dt_ref[...] = jnp.dot(u, wdt_ref[...], preferred_element_type=F32)
    dtt_ref[...] = lax.dot_general(wdtt_ref[...], u, (((1,), (1,)), ((), ())), preferred_element_type=F32)


def in_projection(x, g, wz, wx, wr, wdt, *, tm):
    n = x.shape[0]
    row = lambda w: pl.BlockSpec((tm, w), lambda i: (i, 0))
    wdtt = wdt.T
    return pl.pallas_call(
        _proj_kernel,
        grid=(n // tm,),
        in_specs=[row(D_MODEL), _const_spec((1, D_MODEL)), _const_spec(wz.shape),
                  _const_spec(wx.shape), _const_spec(wr.shape), _const_spec(wdt.shape),
                  _const_spec(wdtt.shape)],
        out_specs=[row(SSD_WIDTH), row(SSD_CONV_DIM), row(RWKV_PROJ), row(SSD_HEADS),
                   pl.BlockSpec((SSD_HEADS, tm), lambda i: (0, i))],
        out_shape=[jax.ShapeDtypeStruct((n, SSD_WIDTH), F32),
                   jax.ShapeDtypeStruct((n, SSD_CONV_DIM), F32),
                   jax.ShapeDtypeStruct((n, RWKV_PROJ), F32),
                   jax.ShapeDtypeStruct((n, SSD_HEADS), F32),
                   jax.ShapeDtypeStruct((SSD_HEADS, n), F32)],
        compiler_params=pltpu.CompilerParams(
            dimension_semantics=("parallel",), vmem_limit_bytes=V7X_VMEM_LIMIT),
        name="in_projection",
    )(x, g, wz, wx, wr, wdt, wdtt)


def _ssd_kernel(z_ref, xbc_ref, dt_ref, dtt_ref, hist_ref, h0_ref, cw_ref, cb_ref,
                dtb_ref, dtbt_ref, alog_ref, alogt_ref, dsk_ref, nrm_ref,
                y_ref, hfin_ref, cnew_ref, xfull_scr, h_scr, *, q, nseq, single_chunk):
    refs = (z_ref, xbc_ref, dt_ref, dtt_ref, hist_ref, h0_ref, cw_ref, cb_ref, dtb_ref, dtbt_ref, alog_ref,
            alogt_ref, dsk_ref, nrm_ref, y_ref, hfin_ref, cnew_ref, xfull_scr, h_scr)
    stages = [_ssd_sequence(s, *refs, q=q, single_chunk=single_chunk) for s in range(nseq)]
    for _ in zip(*stages):
        pass


def _ssd_sequence(s, z_ref, xbc_ref, dt_ref, dtt_ref, hist_ref, h0_ref, cw_ref, cb_ref,
                  dtb_ref, dtbt_ref, alog_ref, alogt_ref, dsk_ref, nrm_ref,
                  y_ref, hfin_ref, cnew_ref, xfull_scr, h_scr, *, q, single_chunk):
    c = pl.program_id(1)
    last = pl.num_programs(1) - 1
    gw = SSD_GROUP_WIDTH

    @pl.when(c == 0)
    def _():
        xfull_scr[s,CONV_PAD - 3:CONV_PAD, :] = hist_ref[s]
        if not single_chunk:
            for g in range(SSD_GROUPS):
                h_scr[s * SSD_GROUPS + g] = h0_ref[s,g * 8:(g + 1) * 8].reshape(gw, SSD_STATE).T

    @pl.when(c > 0)
    def _():
        xfull_scr[s,CONV_PAD - 3:CONV_PAD, :] = xfull_scr[s,CONV_PAD + q - 3:CONV_PAD + q, :]

    xfull_scr[s,CONV_PAD:CONV_PAD + q, :] = xbc_ref[s]

    rows = CONV_PAD + q
    padded = xfull_scr[s]
    conv = cb_ref[...] + padded[CONV_PAD:] * cw_ref[SSD_CONV - 1:SSD_CONV, :]
    for j in range(SSD_CONV - 1):
        lo = CONV_PAD - 3 + j
        conv = conv + pltpu.roll(padded, rows - lo, axis=0)[:q] * cw_ref[j:j + 1, :]
    act = _silu(conv)
    xs = act[:, :SSD_WIDTH]
    yield

    dt = _softplus(dt_ref[s] + dtb_ref[...])
    dtt_raw = dtt_ref[...] if len(dtt_ref.shape) == 2 else dtt_ref[s]
    dtt = _softplus(dtt_raw + dtbt_ref[...])
    da = dt * -jnp.exp(alog_ref[...])
    dat = dtt * -jnp.exp(alogt_ref[...])
    row = lax.broadcasted_iota(jnp.int32, (q, q), 0)
    col = lax.broadcasted_iota(jnp.int32, (q, q), 1)
    causal = row >= col
    a_cum = _dot01(causal.astype(F32), da, exact_side="rhs")
    a_cumt = _dot01(dat, (row <= col).astype(F32), exact_side="lhs")
    yield

    expand = _head_expand(SSD_HEADS)
    decay_in_x = _dot01(jnp.exp(a_cum), expand, exact_side="lhs")
    chunk_decay_x = decay_in_x[q - 1:q, :]
    xd = xs * _dot01(jnp.exp(a_cum[q - 1:q, :] - a_cum) * dt, expand, exact_side="lhs")
    yield

    ys = []
    for g in range(SSD_GROUPS):
        bm = act[:, SSD_WIDTH + g * SSD_STATE:SSD_WIDTH + (g + 1) * SSD_STATE]
        cm = act[:, SSD_WIDTH + SSD_BC + g * SSD_STATE:SSD_WIDTH + SSD_BC + (g + 1) * SSD_STATE]
        cb = lax.dot_general(cm.astype(BF16), bm.astype(BF16), (((1,), (1,)), ((), ())),
                             preferred_element_type=F32)
        yield
        def head_weights(h):
            seg = a_cum[:, h:h + 1] - a_cumt[h:h + 1, :]
            lmat = jnp.where(causal, jnp.exp(jnp.where(causal, seg, 0.0)), 0.0)
            return (cb * lmat * dtt[h:h + 1, :]).astype(BF16)

        y_parts = []
        for e in range(8):
            h = g * 8 + e
            y_parts.append(jnp.dot(head_weights(h), xs[:, h * SSD_HEAD_DIM:(h + 1) * SSD_HEAD_DIM].astype(BF16),
                                   preferred_element_type=F32))
            yield
        y_diag = jnp.concatenate(y_parts, axis=1)
        sl = slice(g * gw, (g + 1) * gw)
        if single_chunk:
            h_in = h0_ref[s,g * 8:(g + 1) * 8].reshape(gw, SSD_STATE)
            y_off = lax.dot_general(cm.astype(BF16), h_in.astype(BF16), (((1,), (1,)), ((), ())),
                                    preferred_element_type=F32)
            upd = lax.dot_general(xd[:, sl].astype(BF16), bm.astype(BF16), (((0,), (0,)), ((), ())),
                                  preferred_element_type=F32)
            head_decay = jnp.broadcast_to(jnp.exp(a_cumt[:, q - 1:q]), (SSD_HEADS, SSD_STATE))
            for e in range(8):
                h = g * 8 + e
                hfin_ref[s,h] = (h0_ref[s,h] * head_decay[h:h + 1, :]
                                  + upd[e * SSD_HEAD_DIM:(e + 1) * SSD_HEAD_DIM, :])
        else:
            h_in = h_scr[s * SSD_GROUPS + g]
            y_off = _bdot(cm, h_in)
            upd = lax.dot_general(bm.astype(BF16), xd[:, sl].astype(BF16), (((0,), (0,)), ((), ())),
                                  preferred_element_type=F32)
            h_scr[s * SSD_GROUPS + g] = h_in * chunk_decay_x[:, sl] + upd
        ys.append(y_diag + y_off * decay_in_x[:, sl])
        yield

    y = jnp.concatenate(ys, axis=1) + dsk_ref[...] * xs
    yg = y * _silu(z_ref[s])
    outs = []
    for g in range(SSD_GROUPS):
        t = yg[:, g * gw:(g + 1) * gw]
        outs.append(t * lax.rsqrt(jnp.mean(t * t, axis=-1, keepdims=True) + NORM_EPS))
    y_ref[s] = jnp.concatenate(outs, axis=1) * nrm_ref[...]

    @pl.when(c == last)
    def _():
        cnew_ref[s] = xfull_scr[s,CONV_PAD + q - 3:CONV_PAD + q, :]
        if not single_chunk:
            for g in range(SSD_GROUPS):
                hfin_ref[s,g * 8:(g + 1) * 8] = h_scr[s * SSD_GROUPS + g].T.reshape(8, SSD_HEAD_DIM, SSD_STATE)

    yield


def ssd_mixer(z, xbc, dt, dtt_flat, conv0, ssm0, w, *, q):
    b, l, _ = z.shape
    single_chunk = l == q
    want = SSD_SEQS_PER_STEP if single_chunk else SSD_LONG_SEQS_PER_STEP
    nseq = want if b % want == 0 else 1
    if q % 128 == 0 and nseq == 1:
        dtt = dtt_flat
        dtt_spec = pl.BlockSpec((SSD_HEADS, q), lambda i, c: (0, i * (l // q) + c))
    else:
        dtt = jnp.swapaxes(dt, 1, 2)
        dtt_spec = pl.BlockSpec((nseq, SSD_HEADS, q), lambda i, c: (i, 0, c))
    seq = lambda wd: pl.BlockSpec((nseq, q, wd), lambda i, c: (i, c, 0))
    per_b3 = lambda s: pl.BlockSpec((nseq,) + s, lambda i, c: (i,) + (0,) * len(s))
    col = lambda t: t.reshape(-1, 1)
    rowv = lambda t: t.reshape(1, -1)
    consts = [w["conv_w"], rowv(w["conv_b"]), rowv(w["dt_bias"]), col(w["dt_bias"]),
              rowv(w["a_log"]), col(w["a_log"]),
              rowv(jnp.repeat(w["d_skip"], SSD_HEAD_DIM)), rowv(w["ssd_norm"])]
    return pl.pallas_call(
        functools.partial(_ssd_kernel, q=q, nseq=nseq, single_chunk=single_chunk),
        grid=(b // nseq, l // q),
        in_specs=[seq(SSD_WIDTH), seq(SSD_CONV_DIM), seq(SSD_HEADS),
                  dtt_spec,
                  per_b3((SSD_CONV - 1, SSD_CONV_DIM)),
                  per_b3((SSD_HEADS, SSD_HEAD_DIM, SSD_STATE))]
                 + [_const_spec(t.shape) for t in consts],
        out_specs=[seq(SSD_WIDTH), per_b3((SSD_HEADS, SSD_HEAD_DIM, SSD_STATE)),
                   per_b3((SSD_CONV - 1, SSD_CONV_DIM))],
        out_shape=[jax.ShapeDtypeStruct((b, l, SSD_WIDTH), F32),
                   jax.ShapeDtypeStruct((b, SSD_HEADS, SSD_HEAD_DIM, SSD_STATE), F32),
                   jax.ShapeDtypeStruct((b, SSD_CONV - 1, SSD_CONV_DIM), F32)],
        scratch_shapes=[pltpu.VMEM((nseq, CONV_PAD + q, SSD_CONV_DIM), F32),
                        pltpu.VMEM((nseq * SSD_GROUPS, SSD_STATE, SSD_GROUP_WIDTH), F32)],
        compiler_params=pltpu.CompilerParams(
            dimension_semantics=("parallel", "arbitrary"), vmem_limit_bytes=V7X_VMEM_LIMIT),
        name="ssd_mixer",
    )(z, xbc, dt, dtt, conv0, ssm0, *consts)


def _head_sums(t):
    pair = 2 * RWKV_HEAD_DIM
    first = lax.broadcasted_iota(jnp.int32, (1, pair), 1) < RWKV_HEAD_DIM
    pieces = []
    for j in range(RWKV_HEADS // 2):
        x = t[:, j * pair:(j + 1) * pair]
        x0 = jnp.where(first, x, 0.0)
        s0 = jnp.sum(x0, axis=-1, keepdims=True)
        s1 = jnp.sum(x - x0, axis=-1, keepdims=True)
        pieces.append(jnp.where(first, s0, s1))
    return jnp.concatenate(pieces, axis=1)


def _rwkv_prep_kernel(rw_ref, sh0_ref, mu_ref, w0_ref, w2_ref, a0_ref, a2_ref, g2_ref,
                      kk_ref, ka_ref, rk_ref,
                      *rest, tt, seqs, chunk):
    outs, full_scr = rest[:-1], rest[-1]
    shn_ref = outs[-1]
    c = pl.program_id(1)
    l = tt // seqs

    if seqs == 1:
        @pl.when(c == 0)
        def _():
            full_scr[0, CONV_PAD - 1:CONV_PAD, :] = sh0_ref[0]

        @pl.when(c > 0)
        def _():
            full_scr[0, CONV_PAD - 1:CONV_PAD, :] = full_scr[0, CONV_PAD + tt - 1:CONV_PAD + tt, :]

        rw = rw_ref[0]
        full_scr[0, CONV_PAD:CONV_PAD + tt, :] = rw
        prev = full_scr[0, CONV_PAD - 1:CONV_PAD - 1 + tt, :]
    else:
        full_scr[:, CONV_PAD - 1:CONV_PAD, :] = sh0_ref[...]
        full_scr[:, CONV_PAD:CONV_PAD + l, :] = rw_ref[...]
        rw = rw_ref[...].reshape(tt, RWKV_PROJ)
        prev = full_scr[:, CONV_PAD - 1:CONV_PAD - 1 + l, :].reshape(tt, RWKV_PROJ)
    vals = _rwkv_mix_math(rw, prev, mu_ref, w0_ref, w2_ref, a0_ref, a2_ref, g2_ref, kk_ref, ka_ref, rk_ref)
    _emit_rwkv_outputs(outs[:-1], vals, tt=tt, chunk=chunk)

    if seqs == 1:
        @pl.when(c == pl.num_programs(1) - 1)
        def _():
            shn_ref[0] = full_scr[0, CONV_PAD + tt - 1:CONV_PAD + tt, :]
    else:
        shn_ref[...] = rw_ref[:, l - 1:l, :]


def _rwkv_mix_math(*args):
    return list(_rwkv_mix_stages(*args))[-1]


def _rwkv_mix_stages(rw, prev, mu_ref, w0_ref, w2_ref, a0_ref, a2_ref, g2_ref, kk_ref, ka_ref, rk_ref):
    wd = RWKV_WIDTH
    u = rw + (prev - rw) * mu_ref[...]
    r = u[:, :wd]
    k = u[:, wd:2 * wd]
    v = u[:, 2 * wd:3 * wd]
    w_lo = u[:, 3 * wd:3 * wd + DECAY_LORA]
    a_lo = u[:, 3 * wd + DECAY_LORA:3 * wd + DECAY_LORA + AAA_LORA]
    g_lo = u[:, 3 * wd + DECAY_LORA + AAA_LORA:]
    w_pre = _bdot(jnp.tanh(w_lo), w2_ref[...])
    a_pre = _bdot(a_lo, a2_ref[...])
    gate = _bdot(_sigmoid(g_lo), g2_ref[...])
    yield None

    w_log = -_softplus(-(w0_ref[...] + w_pre)) - 0.5
    lw = -jnp.exp(w_log)
    a = _sigmoid(a0_ref[...] + a_pre)

    kk = k * kk_ref[...]
    kk = kk / jnp.maximum(jnp.sqrt(_head_sums(kk * kk)), 1e-12)
    kf = k * (1.0 + (a - 1.0) * ka_ref[...])
    kb = kk * a
    bonus = _head_sums(r * kf * rk_ref[...]) * v
    yield r, lw, kf, v, kk, kb, gate, bonus


def _emit_rwkv_outputs(outs, vals, *, tt, chunk, row0=0):
    for _ in _emit_rwkv_stages(outs, vals, tt=tt, chunk=chunk, row0=row0):
        pass


def _emit_rwkv_stages(outs, vals, *, tt, chunk, row0=0):
    r, lw, kf, v, kk, kb, gate, bonus = vals
    wd = RWKV_WIDTH
    gate_out, bonus_out = outs[-2:]
    if chunk == 0:
        blk = gate_out.shape
        for ref, val in zip(outs, (r, jnp.exp(lw), kf, v, kk, kb, gate, bonus)):
            ref[...] = val.reshape(blk)
    else:
        gate_out[0, row0:row0 + tt, :] = gate
        bonus_out[0, row0:row0 + tt, :] = bonus
        kkt_out, rt_out, kfh_out, nbh_out, vb_out, kbg_out, kfg_out, gend_out = outs[:8]
        tri = (lax.broadcasted_iota(jnp.int32, (chunk, chunk), 0)
               >= lax.broadcasted_iota(jnp.int32, (chunk, chunk), 1)).astype(F32)
        for ci in range(tt // chunk):
            rs = slice(ci * chunk, (ci + 1) * chunk)
            ro = slice(row0 + ci * chunk, row0 + (ci + 1) * chunk)
            lw_c = lw[rs]
            cum = _dot01(tri, lw_c, exact_side="rhs")
            yield
            cum_end = cum[chunk - 1:chunk, :]
            g_inv = jnp.exp(-cum)
            g_tail = jnp.exp(cum_end - cum)
            kkt_out[0, ro, :] = (kk[rs] * jnp.exp(cum - lw_c)).astype(BF16)
            rt_out[0, ro, :] = (r[rs] * jnp.exp(cum)).astype(BF16)
            kfh_out[0, ro, :] = (kf[rs] * g_tail).astype(BF16)
            nbh_out[0, ro, :] = (-kb[rs] * g_tail).astype(BF16)
            vb_out[0, ro, :] = v[rs].astype(BF16)
            kbg_out[0, ro, :] = (kb[rs] * g_inv).astype(BF16)
            kfg_out[0, ro, :] = (kf[rs] * g_inv).astype(BF16)
            gend_out[0, row0 // chunk + ci] = jnp.broadcast_to(jnp.exp(cum_end), (8, wd))
    yield


def rwkv_prep(rw, shift0, w, *, tt, chunk):
    b, l, _ = rw.shape
    seqs = max(1, tt // l)
    assert chunk == 0 or (seqs == 1 and tt % chunk == 0)
    rowv = lambda t: t.reshape(1, -1)
    consts = [rowv(w["shift_mu"]), rowv(w["w0"]), w["w2"].astype(BF16), rowv(w["a0"]),
              w["a2"].astype(BF16), w["g2"].astype(BF16), rowv(w["k_k"]), rowv(w["k_a"]),
              rowv(w["r_k"])]
    rows = tt // seqs
    grid = (b // seqs, l // rows)
    seq = lambda wd: pl.BlockSpec((seqs, rows, wd), lambda i, c: (i, c, 0))
    one = pl.BlockSpec((seqs, 1, RWKV_PROJ), lambda i, c: (i, 0, 0))
    sds = jax.ShapeDtypeStruct
    f32_seq = sds((b, l, RWKV_WIDTH), F32)
    if chunk == 0:
        op_specs = [seq(RWKV_WIDTH)] * 6
        op_shapes = [f32_seq] * 6
    else:
        per_tile = tt // chunk
        op_specs = [seq(RWKV_WIDTH)] * 7 + [
            pl.BlockSpec((1, per_tile, 8, RWKV_WIDTH), lambda i, c: (i, c, 0, 0))]
        op_shapes = [sds((b, l, RWKV_WIDTH), BF16)] * 7 + [sds((b, l // chunk, 8, RWKV_WIDTH), F32)]
    outs = pl.pallas_call(
        functools.partial(_rwkv_prep_kernel, tt=tt, seqs=seqs, chunk=chunk),
        grid=grid,
        in_specs=[seq(RWKV_PROJ), one] + [_const_spec(t.shape) for t in consts],
        out_specs=op_specs + [seq(RWKV_WIDTH)] * 2 + [one],
        out_shape=op_shapes + [f32_seq] * 2 + [sds((b, 1, RWKV_PROJ), F32)],
        scratch_shapes=[pltpu.VMEM((seqs, CONV_PAD + rows, RWKV_PROJ), F32)],
        compiler_params=pltpu.CompilerParams(
            dimension_semantics=("parallel", "arbitrary"), vmem_limit_bytes=V7X_VMEM_LIMIT),
        name="rwkv_prep",
    )(rw, shift0, *consts)
    return outs


def _proj_prep_kernel(x_ref, sh0_ref, g_ref, wz_ref, wx_ref, wr_ref, wdt_ref, wdtt_ref,
                      mu_ref, w0_ref, w2_ref, a0_ref, a2_ref, g2_ref, kk_ref, ka_ref, rk_ref,
                      z_ref, xbc_ref, dt_ref, dtt_ref, *rest, tm, tiles_per_seq, chunk):
    outs, (new_scr, cur_scr) = rest[:-2], rest[-2:]
    shn_ref = outs[-1]
    i = pl.program_id(0)

    @pl.when(i == 0)
    def _():
        new_scr[...] = jnp.zeros(new_scr.shape, F32)
        cur_scr[...] = jnp.zeros(cur_scr.shape, F32)

    k = jnp.maximum(i - 1, 0)
    first = (k % tiles_per_seq) == 0
    cur_scr[CONV_PAD - 1:CONV_PAD, :] = jnp.where(first, sh0_ref[0], cur_scr[CONV_PAD + tm - 1:CONV_PAD + tm, :])
    cur_scr[CONV_PAD:CONV_PAD + tm, :] = new_scr[...]

    shn_ref[0] = cur_scr[CONV_PAD + tm - 1:CONV_PAD + tm, :]
    u = _rms(x_ref[...], g_ref[...]).astype(BF16)

    def project(piece):
        if piece == 0:
            z_ref[...] = jnp.dot(u, wz_ref[...], preferred_element_type=F32)
        elif piece == 1:
            xbc_ref[...] = jnp.dot(u, wx_ref[...], preferred_element_type=F32)
        elif piece == 2:
            cols = slice(0, 2 * RWKV_WIDTH)
            new_scr[:, cols] = jnp.dot(u, wr_ref[:, cols], preferred_element_type=F32)
        else:
            cols = slice(2 * RWKV_WIDTH, RWKV_PROJ)
            new_scr[:, cols] = jnp.dot(u, wr_ref[:, cols], preferred_element_type=F32)
            dt_ref[...] = jnp.dot(u, wdt_ref[...], preferred_element_type=F32)
            dtt_ref[...] = lax.dot_general(wdtt_ref[...], u, (((1,), (1,)), ((), ())),
                                           preferred_element_type=F32)

    def prepare(part, rows):
        lo = CONV_PAD + part * rows
        padded = cur_scr[lo - CONV_PAD:lo + rows, :]
        rw = padded[CONV_PAD:]
        prev = pltpu.roll(padded, 1, axis=0)[CONV_PAD:]
        vals = None
        for vals in _rwkv_mix_stages(rw, prev, mu_ref, w0_ref, w2_ref, a0_ref, a2_ref, g2_ref,
                                     kk_ref, ka_ref, rk_ref):
            yield
        yield from _emit_rwkv_stages(outs[:-1], vals, tt=rows, chunk=chunk, row0=part * rows)

    pieces = 4
    parts = min(pieces, tm // chunk)
    todo = list(range(pieces))
    for _ in zip(*[prepare(part, tm // parts) for part in range(parts)]):
        if todo:
            project(todo.pop(0))
    for piece in todo:
        project(piece)


def proj_prep(x, shift0, g, wz, wx, wr, wdt, w, *, tm, chunk):
    b, l, _ = x.shape
    n = b * l
    nt, tps = n // tm, l // tm
    assert l % tm == 0 and tm % chunk == 0
    x2 = x.reshape(n, D_MODEL)
    rowv = lambda t: t.reshape(1, -1)
    wdtt = wdt.T
    consts = [g, wz, wx, wr, wdt, wdtt,
              rowv(w["shift_mu"]), rowv(w["w0"]), w["w2"].astype(BF16), rowv(w["a0"]),
              w["a2"].astype(BF16), w["g2"].astype(BF16), rowv(w["k_k"]), rowv(w["k_a"]), rowv(w["r_k"])]
    ahead = lambda i: jnp.minimum(i, nt - 1)
    behind = lambda i: jnp.maximum(i - 1, 0)
    row_a = lambda w_: pl.BlockSpec((tm, w_), lambda i: (ahead(i), 0))
    seq_b = lambda w_: pl.BlockSpec((1, tm, w_), lambda i: (behind(i) // tps, behind(i) % tps, 0))
    one_b = pl.BlockSpec((1, 1, RWKV_PROJ), lambda i: (behind(i) // tps, 0, 0))
    per_tile = tm // chunk
    gend_spec = pl.BlockSpec((1, per_tile, 8, RWKV_WIDTH), lambda i: (behind(i) // tps, behind(i) % tps, 0, 0))
    sds = jax.ShapeDtypeStruct
    outs = pl.pallas_call(
        functools.partial(_proj_prep_kernel, tm=tm, tiles_per_seq=tps, chunk=chunk),
        grid=(nt + 1,),
        in_specs=[row_a(D_MODEL), one_b] + [_const_spec(t.shape) for t in consts],
        out_specs=[row_a(SSD_WIDTH), row_a(SSD_CONV_DIM), row_a(SSD_HEADS),
                   pl.BlockSpec((SSD_HEADS, tm), lambda i: (0, ahead(i)))]
                  + [seq_b(RWKV_WIDTH)] * 7 + [gend_spec] + [seq_b(RWKV_WIDTH)] * 2 + [one_b],
        out_shape=[sds((n, SSD_WIDTH), F32), sds((n, SSD_CONV_DIM), F32), sds((n, SSD_HEADS), F32),
                   sds((SSD_HEADS, n), F32)]
                  + [sds((b, l, RWKV_WIDTH), BF16)] * 7 + [sds((b, l // chunk, 8, RWKV_WIDTH), F32)]
                  + [sds((b, l, RWKV_WIDTH), F32)] * 2 + [sds((b, 1, RWKV_PROJ), F32)],
        scratch_shapes=[pltpu.VMEM((tm, RWKV_PROJ), F32), pltpu.VMEM((CONV_PAD + tm, RWKV_PROJ), F32)],
        compiler_params=pltpu.CompilerParams(
            dimension_semantics=("arbitrary",), vmem_limit_bytes=V7X_VMEM_LIMIT),
        name="proj_prep",
    )(x2, shift0, *consts)
    return outs


def _wkv_kernel(r_ref, w_ref, k_ref, v_ref, kk_ref, kka_ref, s0_ref,
                o_ref, sfin_ref, s_scr, vt_scr, ot_scr, *, steps):
    c = pl.program_id(1)
    n = RWKV_HEAD_DIM
    lanes = WKV_BATCH_BLOCK * RWKV_HEADS

    @pl.when(c == 0)
    def _():
        s_scr[...] = s0_ref[...].reshape(lanes, n * n).T.reshape(n, n, lanes)

    def to_pairs(ref, t):
        return ref[:, t].reshape(lanes, n).T

    def step(t, carry):
        r_t = to_pairs(r_ref, t)
        w_t = to_pairs(w_ref, t)
        k_t = to_pairs(k_ref, t)
        kk_t = to_pairs(kk_ref, t)
        kka_t = to_pairs(kka_ref, t)
        vt_scr[...] = to_pairs(v_ref, t)

        def per_value(vi, carry2):
            s_v = s_scr[vi]
            skk = jnp.sum(s_v * kk_t, axis=0, keepdims=True)
            v_row = vt_scr[pl.ds(vi, 1), :]
            s_new = s_v * w_t - skk * kka_t + v_row * k_t
            s_scr[vi] = s_new
            ot_scr[pl.ds(vi, 1), :] = jnp.sum(s_new * r_t, axis=0, keepdims=True)
            return carry2

        lax.fori_loop(0, n, per_value, 0, unroll=4)
        o_ref[:, t] = ot_scr[...].T.reshape(WKV_BATCH_BLOCK, RWKV_HEADS, n)
        return carry

    lax.fori_loop(0, steps, step, 0)

    @pl.when(c == pl.num_programs(1) - 1)
    def _():
        sfin_ref[...] = s_scr[...].reshape(n * n, lanes).T.reshape(
            WKV_BATCH_BLOCK, RWKV_HEADS, n, n)


def wkv_scan(r, w, k, v, kk, kka, s0, *, steps):
    b, l, _ = r.shape
    h, n = RWKV_HEADS, RWKV_HEAD_DIM
    assert b % WKV_BATCH_BLOCK == 0 and l % steps == 0
    ops = [t.reshape(b, l, h, n) for t in (r, w, k, v, kk, kka)]
    seq_spec = pl.BlockSpec((WKV_BATCH_BLOCK, steps, h, n), lambda g, c: (g, c, 0, 0))
    st_spec = pl.BlockSpec((WKV_BATCH_BLOCK, h, n, n), lambda g, c: (g, 0, 0, 0))
    o, s_fin = pl.pallas_call(
        functools.partial(_wkv_kernel, steps=steps),
        grid=(b // WKV_BATCH_BLOCK, l // steps),
        in_specs=[seq_spec] * 6 + [st_spec],
        out_specs=[seq_spec, st_spec],
        out_shape=[jax.ShapeDtypeStruct((b, l, h, n), F32),
                   jax.ShapeDtypeStruct((b, h, n, n), F32)],
        scratch_shapes=[pltpu.VMEM((n, n, WKV_BATCH_BLOCK * h), F32),
                        pltpu.VMEM((n, WKV_BATCH_BLOCK * h), F32),
                        pltpu.VMEM((n, WKV_BATCH_BLOCK * h), F32)],
        compiler_params=pltpu.CompilerParams(
            dimension_semantics=("parallel", "arbitrary"), vmem_limit_bytes=V7X_VMEM_LIMIT),
        name="wkv_scan",
    )(*ops, s0)
    return o.reshape(b, l, h * n), s_fin


WKV_LANE_BATCH = 128


def _wkv_batch_lanes_kernel(r_ref, w_ref, k_ref, v_ref, kk_ref, kka_ref, s0_ref,
                            o_ref, sfin_ref, op_scr, ot_scr, *, steps):
    n = RWKV_HEAD_DIM
    nb = WKV_LANE_BATCH
    sfin_ref[...] = s0_ref[...]

    def step(t, carry):
        rows = pl.ds(t, nb, stride=steps)
        for i, ref in enumerate((r_ref, w_ref, k_ref, kk_ref, kka_ref, v_ref)):
            op_scr[i] = ref[rows, :].T
        for h2 in range(2):
            ch = slice(h2 * n, (h2 + 1) * n)
            r_t = op_scr[0, ch, :]
            kka_r = jnp.sum(op_scr[4, ch, :] * r_t, axis=0, keepdims=True)
            k_r = jnp.sum(op_scr[2, ch, :] * r_t, axis=0, keepdims=True)
            op_scr[0, ch, :] = op_scr[1, ch, :] * r_t

            def per_value(vi, carry2, ch=ch, h2=h2, kka_r=kka_r, k_r=k_r):
                s_v = sfin_ref[h2, vi]
                skk = jnp.sum(s_v * op_scr[3, ch, :], axis=0, keepdims=True)
                out = jnp.sum(s_v * op_scr[0, ch, :], axis=0, keepdims=True)
                v_row = op_scr[5, pl.ds(h2 * n + vi, 1), :]
                sfin_ref[h2, vi] = s_v * op_scr[1, ch, :] - skk * op_scr[4, ch, :] + v_row * op_scr[2, ch, :]
                ot_scr[pl.ds(h2 * n + vi, 1), :] = out - skk * kka_r + v_row * k_r
                return carry2

            lax.fori_loop(0, n, per_value, 0, unroll=8)
        o_ref[rows, :] = ot_scr[...].T
        return carry

    lax.fori_loop(0, steps, step, 0)


def wkv_scan_batch_lanes(r, w, k, v, kk, kka, s0):
    b, l, wd = r.shape
    h, n, nb = RWKV_HEADS, RWKV_HEAD_DIM, WKV_LANE_BATCH
    assert b % nb == 0
    ops = [t.reshape(b * l, wd) for t in (r, w, k, v, kk, kka)]
    s0t = jnp.transpose(s0, (1, 2, 3, 0))
    seq_spec = pl.BlockSpec((nb * l, 2 * n), lambda g, j: (g, j))
    st_spec = pl.BlockSpec((2, n, n, nb), lambda g, j: (j, 0, 0, g))
    o, s_fin = pl.pallas_call(
        functools.partial(_wkv_batch_lanes_kernel, steps=l),
        grid=(b // nb, h // 2),
        in_specs=[seq_spec] * 6 + [st_spec],
        out_specs=[seq_spec, st_spec],
        out_shape=[jax.ShapeDtypeStruct((b * l, wd), F32), jax.ShapeDtypeStruct((h, n, n, b), F32)],
        scratch_shapes=[pltpu.VMEM((6, 2 * n, nb), F32), pltpu.VMEM((2 * n, nb), F32)],
        compiler_params=pltpu.CompilerParams(
            dimension_semantics=("parallel", "parallel"), vmem_limit_bytes=V7X_VMEM_LIMIT),
        name="wkv_scan_batch_lanes",
    )(*ops, s0t)
    return o.reshape(b, l, wd), jnp.transpose(s_fin, (3, 0, 1, 2))


WKV_CHUNK = 64
WKV_PAIRS = RWKV_HEADS // 2
WKV_ROW_STRIDE = WKV_CHUNK + 8
WKV_LOCKSTEP = 8
WKV_SOLVE_ROWS = 8
WKV_SOLVE_COLS = 16


def _pair_masks():
    c = WKV_CHUNK
    row = lax.broadcasted_iota(jnp.int32, (2 * c, 2 * c), 0)
    col = lax.broadcasted_iota(jnp.int32, (2 * c, 2 * c), 1)
    t, i = row % c, col % c
    keep = i <= t - jnp.where(row < c, 1, 0)
    sign = jnp.where(row >= c, jnp.where(col < c, -1.0, 1.0), 1.0)
    block_diag = row // c == col // c
    return keep, sign, block_diag


def _wkv_prepare_kernel(kkt_ref, rt_ref, kbg_ref, kfg_ref, vb_ref,
                        lo_ref, rhs0_ref, tp_ref, abs_scr, top_scr, abt_scr, tt_scr):
    c = WKV_CHUNK
    keep, sign, _ = _pair_masks()
    lane = lax.broadcasted_iota(jnp.int32, (1, 2 * c), 1)
    head0 = lane < RWKV_HEAD_DIM
    row_head0 = lax.broadcasted_iota(jnp.int32, (2 * RWKV_HEAD_DIM, 1), 0) < RWKV_HEAD_DIM
    zeros = jnp.zeros((c, 2 * c), BF16)

    def one_batch(b, k):
        for j in range(WKV_PAIRS):
            sl = slice(j * 2 * RWKV_HEAD_DIM, (j + 1) * 2 * RWKV_HEAD_DIM)
            lhs = jnp.concatenate([kkt_ref[b, :, sl], rt_ref[b, :, sl]], axis=0)
            rhs = jnp.concatenate([kbg_ref[b, :, sl], kfg_ref[b, :, sl]], axis=0)
            rhs_t = rhs.astype(F32).T
            top = jnp.where(row_head0, rhs_t, 0.0)
            w_a = jnp.concatenate([top, rhs_t - top], axis=1).astype(BF16)
            a_both = jnp.dot(lhs, w_a, preferred_element_type=F32)
            for h2 in range(2):
                a = a_both[:, h2 * 2 * c:(h2 + 1) * 2 * c]
                a = jnp.where(keep, a, 0.0) * sign
                inst = h2 * (WKV_BATCH_BLOCK * WKV_PAIRS) + b * WKV_PAIRS + j
                abs_scr[pl.ds(inst * WKV_ROW_STRIDE, c), :c] = a[:c, :c]
                top_scr[k, j, :, h2 * 2 * c:(h2 + 1) * 2 * c] = a[:c].astype(BF16)
                lo_ref[b, :, (2 * j + h2) * 2 * c:(2 * j + h2 + 1) * 2 * c] = a[c:].astype(BF16)
        yield
        for j in range(WKV_PAIRS):
            sl = slice(j * 2 * RWKV_HEAD_DIM, (j + 1) * 2 * RWKV_HEAD_DIM)
            v = vb_ref[b, :, sl]
            v0 = jnp.where(head0, v, jnp.zeros_like(v))
            w_akf = jnp.concatenate([zeros, v0, zeros, v - v0], axis=0)
            rhs0_ref[b, :, sl] = jnp.dot(top_scr[k, j], w_akf, preferred_element_type=F32)
        yield

    def two_batches(bb, carry):
        for _ in zip(*[one_batch(WKV_LOCKSTEP * bb + k, k) for k in range(WKV_LOCKSTEP)]):
            pass
        return carry

    lax.fori_loop(0, WKV_BATCH_BLOCK // WKV_LOCKSTEP, two_batches, 0)

    n_inst = 2 * WKV_BATCH_BLOCK * WKV_PAIRS
    n_pair_rows = WKV_BATCH_BLOCK * WKV_PAIRS

    def to_lanes(t, carry):
        abt_scr[t] = abs_scr[pl.ds(t, n_inst, stride=WKV_ROW_STRIDE), :][:, :c].T
        return carry

    lax.fori_loop(0, c, to_lanes, 0, unroll=16)

    tt_scr[...] = jnp.zeros(tt_scr.shape, F32)
    sub_iota = lax.broadcasted_iota(jnp.int32, (WKV_SOLVE_COLS, n_inst), 0)
    rows = range(WKV_SOLVE_ROWS)
    for cb in range(c // WKV_SOLVE_COLS):
        col0 = WKV_SOLVE_COLS * cb
        cols = slice(col0, col0 + WKV_SOLVE_COLS)
        first_block = col0 // WKV_SOLVE_ROWS

        def solve_rows(tb, carry, col0=col0, cols=cols, first_block=first_block):
            t0 = tb * WKV_SOLVE_ROWS

            def sub(ib, accs):
                ps = [tt_scr[ib * WKV_SOLVE_ROWS + di, cols, :] for di in rows]
                out = []
                for r in rows:
                    terms = [abt_scr[t0 + r, pl.ds(ib * WKV_SOLVE_ROWS + di, 1), :] * ps[di] for di in rows]
                    while len(terms) > 1:
                        terms = [a + b for a, b in zip(terms[::2], terms[1::2])]
                    out.append(accs[r] - terms[0])
                return tuple(out)

            unit = tuple(jnp.where(sub_iota + col0 == t0 + r, 1.0, 0.0) for r in rows)
            accs = list(lax.fori_loop(first_block, tb, sub, unit))
            for r in rows:
                for r2 in range(r):
                    accs[r] = accs[r] - abt_scr[t0 + r, pl.ds(t0 + r2, 1), :] * accs[r2]
                tt_scr[t0 + r, cols, :] = accs[r]
            return carry

        lax.fori_loop(first_block, c // WKV_SOLVE_ROWS, solve_rows, 0)

    def from_lanes(t, carry):
        m = tt_scr[t].T
        abs_scr[pl.ds(t, n_pair_rows, stride=WKV_ROW_STRIDE), :] = jnp.concatenate(
            [m[:n_pair_rows], m[n_pair_rows:]], axis=1)
        return carry

    lax.fori_loop(0, c, from_lanes, 0, unroll=32)

    def emit(b, carry):
        for j in range(WKV_PAIRS):
            row0 = (b * WKV_PAIRS + j) * WKV_ROW_STRIDE
            tp_ref[b, :, j * 2 * c:(j + 1) * 2 * c] = abs_scr[pl.ds(row0, c), :].astype(BF16)
        return carry

    lax.fori_loop(0, WKV_BATCH_BLOCK, emit, 0)


def _wkv_apply_kernel(kkt_ref, rt_ref, kfh_ref, nbh_ref, vb_ref, tp_ref, lo_ref, rhs0_ref, gend_ref, s0_ref,
                      o_ref, sfin_ref, x_scr, wp_scr, p_scr):
    c = WKV_CHUNK
    n = RWKV_HEAD_DIM
    ch = pl.program_id(1)
    _, _, block_diag = _pair_masks()
    lane = lax.broadcasted_iota(jnp.int32, (1, 2 * c), 1)
    head0 = lane < n
    eye2 = (lax.broadcasted_iota(jnp.int32, (n, 2 * n), 0)
            == lax.broadcasted_iota(jnp.int32, (n, 2 * n), 1) % n).astype(F32)

    @pl.when(ch == 0)
    def _():
        def init(b, carry):
            for j in range(WKV_PAIRS):
                sp = s0_ref[b, 2 * j:2 * j + 2].reshape(2 * n, n)
                dup = jnp.dot(sp, eye2, precision=HIGHEST, preferred_element_type=F32)
                x_scr[b, j] = jnp.where(block_diag, dup, 0.0)
            return carry
        lax.fori_loop(0, WKV_BATCH_BLOCK, init, 0)

    def one_batch(b, k):
        for j in range(WKV_PAIRS):
            sl = slice(j * 2 * n, (j + 1) * 2 * n)
            lhs = jnp.concatenate([kkt_ref[b, :, sl], rt_ref[b, :, sl]], axis=0)
            kx = lax.dot_general(lhs, x_scr[b, j].astype(BF16), (((1,), (1,)), ((), ())),
                                 preferred_element_type=F32)
            rhs = kx[:c] + rhs0_ref[b, :, sl]
            r0 = jnp.where(head0, rhs, 0.0)
            wp_scr[k, j] = jnp.concatenate([r0, rhs - r0], axis=0).astype(BF16)
            o_ref[b, :, sl] = kx[c:]
        yield
        for j in range(WKV_PAIRS):
            sl = slice(j * 2 * n, (j + 1) * 2 * n)
            p = jnp.dot(tp_ref[b, :, sl], wp_scr[k, j], preferred_element_type=F32)
            p_scr[k, j] = p.astype(BF16)
        yield
        for j in range(WKV_PAIRS):
            sl = slice(j * 2 * n, (j + 1) * 2 * n)
            v = vb_ref[b, :, sl]
            pb = p_scr[k, j]
            zero = jnp.zeros_like(pb)
            p0, v0 = jnp.where(head0, pb, zero), jnp.where(head0, v, zero)
            w_o = jnp.concatenate([p0, v0, pb - p0, v - v0], axis=0)
            o_ref[b, :, sl] = o_ref[b, :, sl] + jnp.dot(
                lo_ref[b, :, j * 4 * c:(j + 1) * 4 * c], w_o, preferred_element_type=F32)
            vp = jnp.concatenate([v, pb], axis=0)
            kb = jnp.concatenate([kfh_ref[b, :, sl], nbh_ref[b, :, sl]], axis=0)
            upd = lax.dot_general(vp, kb, (((0,), (0,)), ((), ())), preferred_element_type=F32)
            x_scr[b, j] = jnp.where(block_diag, x_scr[b, j] * gend_ref[b, 0, 0:1, sl] + upd, 0.0)
        yield

    def two_batches(bb, carry):
        for _ in zip(*[one_batch(WKV_LOCKSTEP * bb + k, k) for k in range(WKV_LOCKSTEP)]):
            pass
        return carry

    lax.fori_loop(0, WKV_BATCH_BLOCK // WKV_LOCKSTEP, two_batches, 0)

    @pl.when(ch == pl.num_programs(1) - 1)
    def _():
        def fin(b, carry):
            for j in range(WKV_PAIRS):
                sp = lax.dot_general(x_scr[b, j], eye2, (((1,), (1,)), ((), ())),
                                     precision=HIGHEST, preferred_element_type=F32)
                sfin_ref[b, 2 * j:2 * j + 2] = sp.reshape(2, n, n)
            return carry
        lax.fori_loop(0, WKV_BATCH_BLOCK, fin, 0)


def wkv_chunked(kkt, rt, kfh, nbh, vb, kbg, kfg, gend, s0):
    b, l, wd = kkt.shape
    c = WKV_CHUNK
    assert b % WKV_BATCH_BLOCK == 0 and l % c == 0
    gb, nc = b // WKV_BATCH_BLOCK, l // c
    lanes = 2 * WKV_BATCH_BLOCK * WKV_PAIRS
    seq = lambda w_: pl.BlockSpec((WKV_BATCH_BLOCK, c, w_), lambda g, i: (g, i, 0))
    gend_spec = pl.BlockSpec((WKV_BATCH_BLOCK, 1, 8, wd), lambda g, i: (g, i, 0, 0))
    sds = jax.ShapeDtypeStruct
    st_spec = pl.BlockSpec((WKV_BATCH_BLOCK, RWKV_HEADS, RWKV_HEAD_DIM, RWKV_HEAD_DIM),
                           lambda g, i: (g, 0, 0, 0))
    blk = (WKV_BATCH_BLOCK, c, wd)
    o, s_fin = pl.pallas_call(
        _wkv_chunk_kernel,
        grid=(gb, nc),
        in_specs=[seq(wd)] * 7 + [gend_spec, st_spec],
        out_specs=[seq(wd), st_spec],
        out_shape=[sds((b, l, wd), F32), sds(s0.shape, F32)],
        scratch_shapes=[
            pltpu.VMEM((lanes * WKV_ROW_STRIDE, 2 * c), F32),
            pltpu.VMEM((WKV_LOCKSTEP, WKV_PAIRS, c, 4 * c), BF16),
            pltpu.VMEM((c, c, lanes), F32),
            pltpu.VMEM((c, c, lanes), F32),
            pltpu.VMEM((WKV_BATCH_BLOCK, c, 2 * wd), BF16),
            pltpu.VMEM(blk, F32),
            pltpu.VMEM(blk, BF16),
            pltpu.VMEM((WKV_BATCH_BLOCK, WKV_PAIRS, 2 * RWKV_HEAD_DIM, 2 * RWKV_HEAD_DIM), F32),
            pltpu.VMEM((WKV_LOCKSTEP, WKV_PAIRS, 2 * c, 2 * RWKV_HEAD_DIM), BF16),
            pltpu.VMEM((WKV_LOCKSTEP, WKV_PAIRS, c, 2 * RWKV_HEAD_DIM), BF16)],
        compiler_params=pltpu.CompilerParams(
            dimension_semantics=("parallel", "arbitrary"), vmem_limit_bytes=V7X_VMEM_LIMIT),
        name="wkv_chunked",
    )(kkt, rt, kfh, nbh, vb, kbg, kfg, gend, s0)
    return o, s_fin


def _wkv_chunk_kernel(kkt_ref, rt_ref, kfh_ref, nbh_ref, vb_ref, kbg_ref, kfg_ref, gend_ref, s0_ref,
                      o_ref, sfin_ref, abs_scr, top_scr, abt_scr, tt_scr, lo_scr, rhs0_scr, tp_scr,
                      x_scr, wp_scr, p_scr):
    _wkv_prepare_kernel(kkt_ref, rt_ref, kbg_ref, kfg_ref, vb_ref, lo_scr, rhs0_scr, tp_scr,
                        abs_scr, top_scr, abt_scr, tt_scr)
    _wkv_apply_kernel(kkt_ref, rt_ref, kfh_ref, nbh_ref, vb_ref, tp_scr, lo_scr, rhs0_scr, gend_ref, s0_ref,
                      o_ref, sfin_ref, x_scr, wp_scr, p_scr)


def _tail_kernel(o_ref, gate_ref, bonus_ref, x_ref, ys_ref, p_ref, lnw_ref, lnb_ref, woa_ref, wob_ref,
                 nf_ref, wg_ref, wu_ref, wd_ref, np_ref, wpg_ref, wpp_ref, nl_ref, y_ref, new_scr, cur_scr):
    i = pl.program_id(0)

    @pl.when(i == 0)
    def _():
        new_scr[...] = jnp.zeros(new_scr.shape, BF16)

    cur_scr[...] = new_scr[...]

    parts = 4
    rows = o_ref.shape[0] // parts
    inv_n = 1.0 / RWKV_HEAD_DIM

    def vector_half(part):
        rs = slice(part * rows, (part + 1) * rows)
        o = o_ref[rs, :]
        mu = _head_sums(o) * inv_n
        d = o - mu
        var = _head_sums(d * d) * inv_n
        on = d * lax.rsqrt(var + GN_EPS) * lnw_ref[...] + lnb_ref[...]
        new_scr[rs, :] = ((on + bonus_ref[rs, :]) * gate_ref[rs, :]).astype(BF16)

    vector_half(0)
    y_rwkv = cur_scr[...]
    h = x_ref[...] + _bdot(ys_ref[...], woa_ref[...]) + jnp.dot(y_rwkv, wob_ref[...],
                                                                preferred_element_type=F32)
    hf = _rms(h, nf_ref[...]).astype(BF16)
    vector_half(1)
    gate = jnp.dot(hf, wg_ref[...], preferred_element_type=F32)
    up = jnp.dot(hf, wu_ref[...], preferred_element_type=F32)
    vector_half(2)
    h = h + _bdot(_silu(gate) * up, wd_ref[...])
    vector_half(3)
    pg = _sigmoid(_bdot(_rms(h, np_ref[...]), wpg_ref[...]))
    h = h + pg * _bdot(p_ref[...], wpp_ref[...])
    y_ref[...] = _rms(h, nl_ref[...])


def layer_tail(o, gate, bonus, x, y_ssd, p, consts, *, tm):
    n = x.shape[0]
    nt = n // tm
    ahead = lambda w_: pl.BlockSpec((tm, w_), lambda i: (jnp.minimum(i, nt - 1), 0))
    behind = lambda w_: pl.BlockSpec((tm, w_), lambda i: (jnp.maximum(i - 1, 0), 0))
    return pl.pallas_call(
        _tail_kernel,
        grid=(nt + 1,),
        in_specs=[ahead(D_MODEL)] * 3 + [behind(D_MODEL)] * 2 + [behind(PLE_DIM)]
                 + [_const_spec(t.shape) for t in consts],
        out_specs=behind(D_MODEL),
        out_shape=jax.ShapeDtypeStruct((n, D_MODEL), F32),
        scratch_shapes=[pltpu.VMEM((tm, D_MODEL), BF16), pltpu.VMEM((tm, D_MODEL), BF16)],
        compiler_params=pltpu.CompilerParams(
            dimension_semantics=("arbitrary",), vmem_limit_bytes=V7X_VMEM_LIMIT),
        name="layer_tail",
    )(o, gate, bonus, x, y_ssd, p, *consts)


def _prepare_weights(w):
    c0, c1, c2 = SSD_WIDTH, SSD_WIDTH + SSD_CONV_DIM, SSD_WIDTH + SSD_CONV_DIM + SSD_HEADS
    w_in = w["w_in"]
    rowv = lambda t: t.reshape(1, -1)
    return dict(
        w,
        wz=w_in[:, :c0].astype(BF16), wx=w_in[:, c0:c1].astype(BF16),
        wdt=w_in[:, c1:c2].astype(BF16), wr=w_in[:, c2:].astype(BF16),
        woa=w["w_out"][:SSD_WIDTH].astype(BF16), wob=w["w_out"][SSD_WIDTH:].astype(BF16),
        wg=w["w_gate"].astype(BF16), wu=w["w_up"].astype(BF16), wd=w["w_down"].astype(BF16),
        wpg=w["w_ple_gate"].astype(BF16), wpp=w["w_ple_proj"].astype(BF16),
        norm_mix_r=rowv(w["norm_mix"]), norm_ffn_r=rowv(w["norm_ffn"]),
        norm_ple_r=rowv(w["norm_ple"]), norm_final_r=rowv(w["norm_final"]),
        ln_x_w_r=rowv(w["ln_x_w"]), ln_x_b_r=rowv(w["ln_x_b"]),
    )


TOKEN_TILE = 256
PREP_TILE = 128
SCAN_STEPS = 16


def layer_forward(x, p, conv0, shift0, ssm0, wkv0, w):
    b, l, _ = x.shape
    n = b * l
    tm = min(TOKEN_TILE, n)
    ssd_q = SSD_CHUNK if l % SSD_CHUNK == 0 else l
    prep_tt = min(PREP_TILE, n)
    wkv_steps = min(SCAN_STEPS, l)
    x2 = x.reshape(n, D_MODEL)
    chunked = l % WKV_CHUNK == 0 and l % tm == 0
    if chunked:
        z, xbc, dt, dtt, *ops, gate, bonus, shift_new = proj_prep(
            x, shift0, w["norm_mix_r"], w["wz"], w["wx"], w["wr"], w["wdt"], w, tm=tm, chunk=WKV_CHUNK)
    else:
        z, xbc, rw, dt, dtt = in_projection(x2, w["norm_mix_r"], w["wz"], w["wx"], w["wr"], w["wdt"], tm=tm)
        *ops, gate, bonus, shift_new = rwkv_prep(rw.reshape(b, l, -1), shift0, w, tt=prep_tt, chunk=0)
    y_ssd, ssm_new, conv_new = ssd_mixer(
        z.reshape(b, l, -1), xbc.reshape(b, l, -1), dt.reshape(b, l, -1), dtt, conv0, ssm0, w, q=ssd_q)
    if chunked:
        o, wkv_new = wkv_chunked(*ops, wkv0)
    elif b % WKV_LANE_BATCH == 0:
        o, wkv_new = wkv_scan_batch_lanes(*ops, wkv0)
    else:
        o, wkv_new = wkv_scan(*ops, wkv0, steps=wkv_steps)
    flat = lambda t: t.reshape(n, -1)
    tail_consts = [w["ln_x_w_r"], w["ln_x_b_r"], w["woa"], w["wob"], w["norm_ffn_r"], w["wg"], w["wu"],
                   w["wd"], w["norm_ple_r"], w["wpg"], w["wpp"], w["norm_final_r"]]
    y = layer_tail(flat(o), flat(gate), flat(bonus), x2, flat(y_ssd), p.reshape(n, PLE_DIM),
                   tail_consts, tm=tm)
    return y.reshape(b, l, D_MODEL), ssm_new, conv_new, wkv_new, shift_new


def kernel(x_prompt, x_sample, state_ssm, state_conv, state_wkv, state_shift, p_prompt, p_sample, norm_mix, w_in, conv_w, conv_b, dt_bias, a_log, d_skip, ssd_norm, shift_mu, w0, w2, a0, a2, g2, k_k, k_a, r_k, ln_x_w, ln_x_b, w_out, norm_ffn, w_gate, w_up, w_down, norm_ple, w_ple_gate, w_ple_proj, norm_final):
    w = _prepare_weights(dict(
        norm_mix=norm_mix[0], w_in=w_in[0], conv_w=conv_w[0], conv_b=conv_b[0], dt_bias=dt_bias[0],
        a_log=a_log[0], d_skip=d_skip[0], ssd_norm=ssd_norm[0], shift_mu=shift_mu[0], w0=w0[0],
        w2=w2[0], a0=a0[0], a2=a2[0], g2=g2[0], k_k=k_k[0], k_a=k_a[0], r_k=r_k[0],
        ln_x_w=ln_x_w[0], ln_x_b=ln_x_b[0], w_out=w_out[0], norm_ffn=norm_ffn[0],
        w_gate=w_gate[0], w_up=w_up[0], w_down=w_down[0], norm_ple=norm_ple[0],
        w_ple_gate=w_ple_gate[0], w_ple_proj=w_ple_proj[0], norm_final=norm_final))
    bp = x_prompt.shape[0]
    zeros = lambda *s: jnp.zeros(s, F32)
    yp, s1, c1, k1, t1 = layer_forward(
        x_prompt, p_prompt[0], zeros(bp, SSD_CONV - 1, SSD_CONV_DIM), zeros(bp, 1, RWKV_PROJ),
        zeros(bp, SSD_HEADS, SSD_HEAD_DIM, SSD_STATE),
        zeros(bp, RWKV_HEADS, RWKV_HEAD_DIM, RWKV_HEAD_DIM), w)
    ys, s2, c2, k2, t2 = layer_forward(
        x_sample, p_sample[0], state_conv[0], state_shift[0], state_ssm[0], state_wkv[0], w)
    return (yp, ys, s1[None], c1[None], k1[None], t1[None], s2[None], c2[None], k2[None], t2[None])
```

```python
import functools

import jax
import jax.numpy as jnp
from jax import lax
from jax.experimental import pallas as pl
from jax.experimental.pallas import tpu as pltpu

F32 = jnp.float32
BF16 = jnp.bfloat16
HIGHEST = lax.Precision.HIGHEST

D_MODEL = 1024
SSD_WIDTH = 1024
SSD_HEADS = 16
SSD_HEAD_DIM = 64
SSD_GROUPS = 2
SSD_GROUP_WIDTH = SSD_WIDTH // SSD_GROUPS
SSD_STATE = 128
SSD_CONV = 4
SSD_CHUNK = 128
SSD_BC = SSD_GROUPS * SSD_STATE
SSD_CONV_DIM = SSD_WIDTH + 2 * SSD_BC
RWKV_WIDTH = 1024
RWKV_HEADS = 16
RWKV_HEAD_DIM = 64
DECAY_LORA = 64
AAA_LORA = 64
GATE_LORA = 128
RWKV_PROJ = 3 * RWKV_WIDTH + DECAY_LORA + AAA_LORA + GATE_LORA
PLE_DIM = 256
NORM_EPS = 1e-6
GN_EPS = 64e-5

WKV_BATCH_BLOCK = 8
V7X_VMEM_LIMIT = 56 * 1024 * 1024
CONV_PAD = 8
SSD_SEQS_PER_STEP = 8
SSD_TAIL_ROWS = 32
SSD_LONG_SEQS_PER_STEP = 1


def _rms(x, g):
    return x * lax.rsqrt(jnp.mean(x * x, axis=-1, keepdims=True) + NORM_EPS) * g


def _sigmoid(x):
    return 0.5 * jnp.tanh(0.5 * x) + 0.5


def _silu(x):
    return x * _sigmoid(x)


def _softplus(x):
    return jnp.maximum(x, 0.0) + jnp.log(1.0 + jnp.exp(-jnp.abs(x)))


def _bdot(a, b):
    return jnp.dot(a.astype(BF16), b.astype(BF16), preferred_element_type=F32)


def _split3(t):
    hi = t.astype(BF16)
    r1 = t - hi.astype(F32)
    mid = r1.astype(BF16)
    lo = (r1 - mid.astype(F32)).astype(BF16)
    return hi, mid, lo


def _dot01(a, b, *, exact_side):
    if exact_side == "lhs":
        m = b.astype(BF16)
        return sum(jnp.dot(p, m, preferred_element_type=F32) for p in _split3(a))
    m = a.astype(BF16)
    return sum(jnp.dot(m, p, preferred_element_type=F32) for p in _split3(b))


def _const_spec(shape):
    return pl.BlockSpec(shape, lambda *_: (0,) * len(shape), pipeline_mode=pl.Buffered(1))


def _head_expand(rows):
    h = lax.broadcasted_iota(jnp.int32, (rows, SSD_WIDTH), 0)
    c = lax.broadcasted_iota(jnp.int32, (rows, SSD_WIDTH), 1)
    return (c // SSD_HEAD_DIM == h).astype(F32)


def _proj_kernel(x_ref, g_ref, wz_ref, wx_ref, wr_ref, wdt_ref, wdtt_ref,
                 z_ref, xbc_ref, rw_ref, dt_ref, dtt_ref):
    u = _rms(x_ref[...], g_ref[...]).astype(BF16)
    z_ref[...] = jnp.dot(u, wz_ref[...], preferred_element_type=F32)
    xbc_ref[...] = jnp.dot(u, wx_ref[...], preferred_element_type=F32)
    rw_ref[...] = jnp.dot(u, wr_ref[...], preferred_element_type=F32)
    dt_ref[...] = jnp.dot(u, wdt_ref[...], preferred_element_type=F32)
    dtt_ref[...] = lax.dot_general(wdtt_ref[...], u, (((1,), (1,)), ((), ())), preferred_element_type=F32)


def in_projection(x, g, wz, wx, wr, wdt, *, tm):
    n = x.shape[0]
    row = lambda w: pl.BlockSpec((tm, w), lambda i: (i, 0))
    wdtt = wdt.T
    return pl.pallas_call(
        _proj_kernel,
        grid=(n // tm,),
        in_specs=[row(D_MODEL), _const_spec((1, D_MODEL)), _const_spec(wz.shape),
                  _const_spec(wx.shape), _const_spec(wr.shape), _const_spec(wdt.shape),
                  _const_spec(wdtt.shape)],
        out_specs=[row(SSD_WIDTH), row(SSD_CONV_DIM), row(RWKV_PROJ), row(SSD_HEADS),
                   pl.BlockSpec((SSD_HEADS, tm), lambda i: (0, i))],
        out_shape=[jax.ShapeDtypeStruct((n, SSD_WIDTH), F32),
                   jax.ShapeDtypeStruct((n, SSD_CONV_DIM), F32),
                   jax.ShapeDtypeStruct((n, RWKV_PROJ), F32),
                   jax.ShapeDtypeStruct((n, SSD_HEADS), F32),
                   jax.ShapeDtypeStruct((SSD_HEADS, n), F32)],
        compiler_params=pltpu.CompilerParams(
            dimension_semantics=("parallel",), vmem_limit_bytes=V7X_VMEM_LIMIT),
        name="in_projection",
    )(x, g, wz, wx, wr, wdt, wdtt)


def _ssd_kernel(z_ref, xbc_ref, dt_ref, dtt_ref, hist_ref, h0_ref, cw_ref, cb_ref,
                dtb_ref, dtbt_ref, alog_ref, alogt_ref, dsk_ref, nrm_ref,
                y_ref, hfin_ref, cnew_ref, xfull_scr, h_scr, act_scr, xsb_scr, *, q, nseq, single_chunk):
    refs = (z_ref, xbc_ref, dt_ref, dtt_ref, hist_ref, h0_ref, cw_ref, cb_ref, dtb_ref, dtbt_ref, alog_ref,
            alogt_ref, dsk_ref, nrm_ref, y_ref, hfin_ref, cnew_ref, xfull_scr, h_scr, act_scr, xsb_scr)
    stages = [_ssd_sequence(s, *refs, q=q, single_chunk=single_chunk) for s in range(nseq)]
    for _ in zip(*stages):
        pass


def _ssd_sequence(s, z_ref, xbc_ref, dt_ref, dtt_ref, hist_ref, h0_ref, cw_ref, cb_ref,
                  dtb_ref, dtbt_ref, alog_ref, alogt_ref, dsk_ref, nrm_ref,
                  y_ref, hfin_ref, cnew_ref, xfull_scr, h_scr, act_scr, xsb_scr, *, q, single_chunk):
    c = pl.program_id(1)
    last = pl.num_programs(1) - 1
    gw = SSD_GROUP_WIDTH

    @pl.when(c == 0)
    def _():
        xfull_scr[s,CONV_PAD - 3:CONV_PAD, :] = hist_ref[s]
        if not single_chunk:
            for g in range(SSD_GROUPS):
                h_scr[s * SSD_GROUPS + g] = h0_ref[s,g * 8:(g + 1) * 8].reshape(gw, SSD_STATE).T

    @pl.when(c > 0)
    def _():
        xfull_scr[s,CONV_PAD - 3:CONV_PAD, :] = xfull_scr[s,CONV_PAD + q - 3:CONV_PAD + q, :]

    xfull_scr[s,CONV_PAD:CONV_PAD + q, :] = xbc_ref[s]

    rows = CONV_PAD + q
    padded = xfull_scr[s]
    conv = cb_ref[...] + padded[CONV_PAD:] * cw_ref[SSD_CONV - 1:SSD_CONV, :]
    for j in range(SSD_CONV - 1):
        lo = CONV_PAD - 3 + j
        conv = conv + pltpu.roll(padded, rows - lo, axis=0)[:q] * cw_ref[j:j + 1, :]
    act_scr[s] = _silu(conv)
    xsb_scr[s] = act_scr[s, :, :SSD_WIDTH].astype(BF16)
    yield

    dt = _softplus(dt_ref[s] + dtb_ref[...])
    dtt_raw = dtt_ref[...] if len(dtt_ref.shape) == 2 else dtt_ref[s]
    dtt = _softplus(dtt_raw + dtbt_ref[...])
    da = dt * -jnp.exp(alog_ref[...])
    dat = dtt * -jnp.exp(alogt_ref[...])
    row = lax.broadcasted_iota(jnp.int32, (q, q), 0)
    col = lax.broadcasted_iota(jnp.int32, (q, q), 1)
    causal = row >= col
    a_cum = _dot01(causal.astype(F32), da, exact_side="rhs")
    a_cumt = _dot01(dat, (row <= col).astype(F32), exact_side="lhs")
    yield

    expand = _head_expand(SSD_HEADS)
    decay_in_x = _dot01(jnp.exp(a_cum), expand, exact_side="lhs")
    chunk_decay_x = decay_in_x[q - 1:q, :]
    xd = act_scr[s, :, :SSD_WIDTH] * _dot01(jnp.exp(a_cum[q - 1:q, :] - a_cum) * dt, expand, exact_side="lhs")
    yield

    ys = []
    for g in range(SSD_GROUPS):
        bm = act_scr[s, :, SSD_WIDTH + g * SSD_STATE:SSD_WIDTH + (g + 1) * SSD_STATE]
        cm = act_scr[s, :, SSD_WIDTH + SSD_BC + g * SSD_STATE:SSD_WIDTH + SSD_BC + (g + 1) * SSD_STATE]
        cb = lax.dot_general(cm.astype(BF16), bm.astype(BF16), (((1,), (1,)), ((), ())),
                             preferred_element_type=F32)
        yield
        def head_weights(h):
            seg = a_cum[:, h:h + 1] - a_cumt[h:h + 1, :]
            lmat = jnp.where(causal, jnp.exp(jnp.where(causal, seg, 0.0)), 0.0)
            return (cb * lmat * dtt[h:h + 1, :]).astype(BF16)

        y_parts = []
        for e in range(8):
            h = g * 8 + e
            y_parts.append(jnp.dot(head_weights(h), xsb_scr[s, :, h * SSD_HEAD_DIM:(h + 1) * SSD_HEAD_DIM],
                                   preferred_element_type=F32))
            yield
        y_diag = jnp.concatenate(y_parts, axis=1)
        sl = slice(g * gw, (g + 1) * gw)
        if single_chunk:
            h_in = h0_ref[s,g * 8:(g + 1) * 8].reshape(gw, SSD_STATE)
            y_off = lax.dot_general(cm.astype(BF16), h_in.astype(BF16), (((1,), (1,)), ((), ())),
                                    preferred_element_type=F32)
            upd = lax.dot_general(xd[:, sl].astype(BF16), bm.astype(BF16), (((0,), (0,)), ((), ())),
                                  preferred_element_type=F32)
            head_decay = jnp.broadcast_to(jnp.exp(a_cumt[:, q - 1:q]), (SSD_HEADS, SSD_STATE))
            for e in range(8):
                h = g * 8 + e
                hfin_ref[s,h] = (h0_ref[s,h] * head_decay[h:h + 1, :]
                                  + upd[e * SSD_HEAD_DIM:(e + 1) * SSD_HEAD_DIM, :])
        else:
            h_in = h_scr[s * SSD_GROUPS + g]
            y_off = _bdot(cm, h_in)
            upd = lax.dot_general(bm.astype(BF16), xd[:, sl].astype(BF16), (((0,), (0,)), ((), ())),
                                  preferred_element_type=F32)
            h_scr[s * SSD_GROUPS + g] = h_in * chunk_decay_x[:, sl] + upd
        ys.append(y_diag + y_off * decay_in_x[:, sl])
        yield

    rb = min(q, SSD_TAIL_ROWS)
    for r0 in range(0, q, rb):
        rws = slice(r0, r0 + rb)
        for g in range(SSD_GROUPS):
            cols = slice(g * gw, (g + 1) * gw)
            t = (ys[g][rws] + dsk_ref[:, cols] * act_scr[s, rws, cols]) * _silu(z_ref[s, rws, cols])
            t = t * lax.rsqrt(jnp.mean(t * t, axis=-1, keepdims=True) + NORM_EPS)
            y_ref[s, rws, cols] = t * nrm_ref[:, cols]

    @pl.when(c == last)
    def _():
        cnew_ref[s] = xfull_scr[s,CONV_PAD + q - 3:CONV_PAD + q, :]
        if not single_chunk:
            for g in range(SSD_GROUPS):
                hfin_ref[s,g * 8:(g + 1) * 8] = h_scr[s * SSD_GROUPS + g].T.reshape(8, SSD_HEAD_DIM, SSD_STATE)

    yield


def ssd_mixer(z, xbc, dt, dtt_flat, conv0, ssm0, w, *, q):
    b, l, _ = z.shape
    single_chunk = l == q
    want = SSD_SEQS_PER_STEP if single_chunk else SSD_LONG_SEQS_PER_STEP
    nseq = want if b % want == 0 else 1
    if q % 128 == 0 and nseq == 1:
        dtt = dtt_flat
        dtt_spec = pl.BlockSpec((SSD_HEADS, q), lambda i, c: (0, i * (l // q) + c))
    else:
        dtt = jnp.swapaxes(dt, 1, 2)
        dtt_spec = pl.BlockSpec((nseq, SSD_HEADS, q), lambda i, c: (i, 0, c))
    seq = lambda wd: pl.BlockSpec((nseq, q, wd), lambda i, c: (i, c, 0))
    per_b3 = lambda s: pl.BlockSpec((nseq,) + s, lambda i, c: (i,) + (0,) * len(s))
    col = lambda t: t.reshape(-1, 1)
    rowv = lambda t: t.reshape(1, -1)
    consts = [w["conv_w"], rowv(w["conv_b"]), rowv(w["dt_bias"]), col(w["dt_bias"]),
              rowv(w["a_log"]), col(w["a_log"]),
              rowv(jnp.repeat(w["d_skip"], SSD_HEAD_DIM)), rowv(w["ssd_norm"])]
    return pl.pallas_call(
        functools.partial(_ssd_kernel, q=q, nseq=nseq, single_chunk=single_chunk),
        grid=(b // nseq, l // q),
        in_specs=[seq(SSD_WIDTH), seq(SSD_CONV_DIM), seq(SSD_HEADS),
                  dtt_spec,
                  per_b3((SSD_CONV - 1, SSD_CONV_DIM)),
                  per_b3((SSD_HEADS, SSD_HEAD_DIM, SSD_STATE))]
                 + [_const_spec(t.shape) for t in consts],
        out_specs=[seq(SSD_WIDTH), per_b3((SSD_HEADS, SSD_HEAD_DIM, SSD_STATE)),
                   per_b3((SSD_CONV - 1, SSD_CONV_DIM))],
        out_shape=[jax.ShapeDtypeStruct((b, l, SSD_WIDTH), F32),
                   jax.ShapeDtypeStruct((b, SSD_HEADS, SSD_HEAD_DIM, SSD_STATE), F32),
                   jax.ShapeDtypeStruct((b, SSD_CONV - 1, SSD_CONV_DIM), F32)],
        scratch_shapes=[pltpu.VMEM((nseq, CONV_PAD + q, SSD_CONV_DIM), F32),
                        pltpu.VMEM((nseq * SSD_GROUPS, SSD_STATE, SSD_GROUP_WIDTH), F32),
                        pltpu.VMEM((nseq, q, SSD_CONV_DIM), F32),
                        pltpu.VMEM((nseq, q, SSD_WIDTH), BF16)],
        compiler_params=pltpu.CompilerParams(
            dimension_semantics=("parallel", "arbitrary"), vmem_limit_bytes=V7X_VMEM_LIMIT),
        name="ssd_mixer",
    )(z, xbc, dt, dtt, conv0, ssm0, *consts)


def _head_sums(t):
    pair = 2 * RWKV_HEAD_DIM
    first = lax.broadcasted_iota(jnp.int32, (1, pair), 1) < RWKV_HEAD_DIM
    pieces = []
    for j in range(RWKV_HEADS // 2):
        x = t[:, j * pair:(j + 1) * pair]
        x0 = jnp.where(first, x, 0.0)
        s0 = jnp.sum(x0, axis=-1, keepdims=True)
        s1 = jnp.sum(x - x0, axis=-1, keepdims=True)
        pieces.append(jnp.where(first, s0, s1))
    return jnp.concatenate(pieces, axis=1)


def _rwkv_prep_kernel(rw_ref, sh0_ref, mu_ref, w0_ref, w2_ref, a0_ref, a2_ref, g2_ref,
                      kk_ref, ka_ref, rk_ref,
                      *rest, tt, seqs, chunk):
    outs, full_scr = rest[:-1], rest[-1]
    shn_ref = outs[-1]
    c = pl.program_id(1)
    l = tt // seqs

    if seqs == 1:
        @pl.when(c == 0)
        def _():
            full_scr[0, CONV_PAD - 1:CONV_PAD, :] = sh0_ref[0]

        @pl.when(c > 0)
        def _():
            full_scr[0, CONV_PAD - 1:CONV_PAD, :] = full_scr[0, CONV_PAD + tt - 1:CONV_PAD + tt, :]

        rw = rw_ref[0]
        full_scr[0, CONV_PAD:CONV_PAD + tt, :] = rw
        prev = full_scr[0, CONV_PAD - 1:CONV_PAD - 1 + tt, :]
    else:
        full_scr[:, CONV_PAD - 1:CONV_PAD, :] = sh0_ref[...]
        full_scr[:, CONV_PAD:CONV_PAD + l, :] = rw_ref[...]
        rw = rw_ref[...].reshape(tt, RWKV_PROJ)
        prev = full_scr[:, CONV_PAD - 1:CONV_PAD - 1 + l, :].reshape(tt, RWKV_PROJ)
    vals = _rwkv_mix_math(rw, prev, mu_ref, w0_ref, w2_ref, a0_ref, a2_ref, g2_ref, kk_ref, ka_ref, rk_ref)
    _emit_rwkv_outputs(outs[:-1], vals, tt=tt, chunk=chunk)

    if seqs == 1:
        @pl.when(c == pl.num_programs(1) - 1)
        def _():
            shn_ref[0] = full_scr[0, CONV_PAD + tt - 1:CONV_PAD + tt, :]
    else:
        shn_ref[...] = rw_ref[:, l - 1:l, :]


def _rwkv_mix_math(*args):
    return list(_rwkv_mix_stages(*args))[-1]


def _rwkv_mix_stages(rw, prev, mu_ref, w0_ref, w2_ref, a0_ref, a2_ref, g2_ref, kk_ref, ka_ref, rk_ref):
    wd = RWKV_WIDTH
    u = rw + (prev - rw) * mu_ref[...]
    r = u[:, :wd]
    k = u[:, wd:2 * wd]
    v = u[:, 2 * wd:3 * wd]
    w_lo = u[:, 3 * wd:3 * wd + DECAY_LORA]
    a_lo = u[:, 3 * wd + DECAY_LORA:3 * wd + DECAY_LORA + AAA_LORA]
    g_lo = u[:, 3 * wd + DECAY_LORA + AAA_LORA:]
    w_pre = _bdot(jnp.tanh(w_lo), w2_ref[...])
    a_pre = _bdot(a_lo, a2_ref[...])
    gate = _bdot(_sigmoid(g_lo), g2_ref[...])
    yield None

    w_log = -_softplus(-(w0_ref[...] + w_pre)) - 0.5
    lw = -jnp.exp(w_log)
    a = _sigmoid(a0_ref[...] + a_pre)

    kk = k * kk_ref[...]
    kk_norm2 = _head_sums(kk * kk)
    kf = k * (1.0 + (a - 1.0) * ka_ref[...])
    rk_sum = _head_sums(r * kf * rk_ref[...])
    yield None
    kk = kk / jnp.maximum(jnp.sqrt(kk_norm2), 1e-12)
    kb = kk * a
    bonus = rk_sum * v
    yield r, lw, kf, v, kk, kb, gate, bonus


def _emit_rwkv_outputs(outs, vals, *, tt, chunk, row0=0):
    for _ in _emit_rwkv_stages(outs, vals, tt=tt, chunk=chunk, row0=row0):
        pass


def _emit_rwkv_stages(outs, vals, *, tt, chunk, row0=0):
    r, lw, kf, v, kk, kb, gate, bonus = vals
    wd = RWKV_WIDTH
    gate_out, bonus_out = outs[-2:]
    if chunk == 0:
        blk = gate_out.shape
        for ref, val in zip(outs, (r, jnp.exp(lw), kf, v, kk, kb, gate, bonus)):
            ref[...] = val.reshape(blk)
    else:
        gate_out[0, row0:row0 + tt, :] = gate
        bonus_out[0, row0:row0 + tt, :] = bonus
        kkt_out, rt_out, kfh_out, nbh_out, vb_out, kbg_out, kfg_out, gend_out = outs[:8]
        tri = (lax.broadcasted_iota(jnp.int32, (chunk, chunk), 0)
               >= lax.broadcasted_iota(jnp.int32, (chunk, chunk), 1)).astype(F32)
        for ci in range(tt // chunk):
            rs = slice(ci * chunk, (ci + 1) * chunk)
            ro = slice(row0 + ci * chunk, row0 + (ci + 1) * chunk)
            lw_c = lw[rs]
            cum = _dot01(tri, lw_c, exact_side="rhs")
            yield
            cum_end = cum[chunk - 1:chunk, :]
            g_inv = jnp.exp(-cum)
            g_tail = jnp.exp(cum_end - cum)
            kkt_out[0, ro, :] = (kk[rs] * jnp.exp(cum - lw_c)).astype(BF16)
            rt_out[0, ro, :] = (r[rs] * jnp.exp(cum)).astype(BF16)
            kfh_out[0, ro, :] = (kf[rs] * g_tail).astype(BF16)
            nbh_out[0, ro, :] = (-kb[rs] * g_tail).astype(BF16)
            vb_out[0, ro, :] = v[rs].astype(BF16)
            kbg_out[0, ro, :] = (kb[rs] * g_inv).astype(BF16)
            kfg_out[0, ro, :] = (kf[rs] * g_inv).astype(BF16)
            gend_out[0, row0 // chunk + ci] = jnp.broadcast_to(jnp.exp(cum_end), (8, wd))
    yield


def rwkv_prep(rw, shift0, w, *, tt, chunk):
    b, l, _ = rw.shape
    seqs = max(1, tt // l)
    assert chunk == 0 or (seqs == 1 and tt % chunk == 0)
    rowv = lambda t: t.reshape(1, -1)
    consts = [rowv(w["shift_mu"]), rowv(w["w0"]), w["w2"].astype(BF16), rowv(w["a0"]),
              w["a2"].astype(BF16), w["g2"].astype(BF16), rowv(w["k_k"]), rowv(w["k_a"]),
              rowv(w["r_k"])]
    rows = tt // seqs
    grid = (b // seqs, l // rows)
    seq = lambda wd: pl.BlockSpec((seqs, rows, wd), lambda i, c: (i, c, 0))
    one = pl.BlockSpec((seqs, 1, RWKV_PROJ), lambda i, c: (i, 0, 0))
    sds = jax.ShapeDtypeStruct
    f32_seq = sds((b, l, RWKV_WIDTH), F32)
    if chunk == 0:
        op_specs = [seq(RWKV_WIDTH)] * 6
        op_shapes = [f32_seq] * 6
    else:
        per_tile = tt // chunk
        op_specs = [seq(RWKV_WIDTH)] * 7 + [
            pl.BlockSpec((1, per_tile, 8, RWKV_WIDTH), lambda i, c: (i, c, 0, 0))]
        op_shapes = [sds((b, l, RWKV_WIDTH), BF16)] * 7 + [sds((b, l // chunk, 8, RWKV_WIDTH), F32)]
    outs = pl.pallas_call(
        functools.partial(_rwkv_prep_kernel, tt=tt, seqs=seqs, chunk=chunk),
        grid=grid,
        in_specs=[seq(RWKV_PROJ), one] + [_const_spec(t.shape) for t in consts],
        out_specs=op_specs + [seq(RWKV_WIDTH)] * 2 + [one],
        out_shape=op_shapes + [f32_seq] * 2 + [sds((b, 1, RWKV_PROJ), F32)],
        scratch_shapes=[pltpu.VMEM((seqs, CONV_PAD + rows, RWKV_PROJ), F32)],
        compiler_params=pltpu.CompilerParams(
            dimension_semantics=("parallel", "arbitrary"), vmem_limit_bytes=V7X_VMEM_LIMIT),
        name="rwkv_prep",
    )(rw, shift0, *consts)
    return outs


def _proj_prep_kernel(x_ref, sh0_ref, g_ref, wz_ref, wx_ref, wr_ref, wdt_ref, wdtt_ref,
                      mu_ref, w0_ref, w2_ref, a0_ref, a2_ref, g2_ref, kk_ref, ka_ref, rk_ref,
                      z_ref, xbc_ref, dt_ref, dtt_ref, *rest, tm, tiles_per_seq, chunk):
    outs, (new_scr, cur_scr) = rest[:-2], rest[-2:]
    shn_ref = outs[-1]
    i = pl.program_id(0)

    @pl.when(i == 0)
    def _():
        new_scr[...] = jnp.zeros(new_scr.shape, F32)
        cur_scr[...] = jnp.zeros(cur_scr.shape, F32)

    k = jnp.maximum(i - 1, 0)
    first = (k % tiles_per_seq) == 0
    cur_scr[CONV_PAD - 1:CONV_PAD, :] = jnp.where(first, sh0_ref[0], cur_scr[CONV_PAD + tm - 1:CONV_PAD + tm, :])
    cur_scr[CONV_PAD:CONV_PAD + tm, :] = new_scr[...]

    shn_ref[0] = cur_scr[CONV_PAD + tm - 1:CONV_PAD + tm, :]
    u = _rms(x_ref[...], g_ref[...]).astype(BF16)

    def project(piece):
        if piece == 0:
            z_ref[...] = jnp.dot(u, wz_ref[...], preferred_element_type=F32)
        elif piece == 1:
            xbc_ref[...] = jnp.dot(u, wx_ref[...], preferred_element_type=F32)
        elif piece == 2:
            cols = slice(0, 2 * RWKV_WIDTH)
            new_scr[:, cols] = jnp.dot(u, wr_ref[:, cols], preferred_element_type=F32)
        else:
            cols = slice(2 * RWKV_WIDTH, RWKV_PROJ)
            new_scr[:, cols] = jnp.dot(u, wr_ref[:, cols], preferred_element_type=F32)
            dt_ref[...] = jnp.dot(u, wdt_ref[...], preferred_element_type=F32)
            dtt_ref[...] = lax.dot_general(wdtt_ref[...], u, (((1,), (1,)), ((), ())),
                                           preferred_element_type=F32)

    def prepare(part, rows):
        lo = CONV_PAD + part * rows
        padded = cur_scr[lo - CONV_PAD:lo + rows, :]
        rw = padded[CONV_PAD:]
        prev = pltpu.roll(padded, 1, axis=0)[CONV_PAD:]
        vals = None
        for vals in _rwkv_mix_stages(rw, prev, mu_ref, w0_ref, w2_ref, a0_ref, a2_ref, g2_ref,
                                     kk_ref, ka_ref, rk_ref):
            yield
        yield from _emit_rwkv_stages(outs[:-1], vals, tt=rows, chunk=chunk, row0=part * rows)

    pieces = 4
    parts = min(pieces, tm // chunk)
    todo = list(range(pieces))
    for _ in zip(*[prepare(part, tm // parts) for part in range(parts)]):
        if todo:
            project(todo.pop(0))
    for piece in todo:
        project(piece)


def proj_prep(x, shift0, g, wz, wx, wr, wdt, w, *, tm, chunk):
    b, l, _ = x.shape
    n = b * l
    nt, tps = n // tm, l // tm
    assert l % tm == 0 and tm % chunk == 0
    x2 = x.reshape(n, D_MODEL)
    rowv = lambda t: t.reshape(1, -1)
    wdtt = wdt.T
    consts = [g, wz, wx, wr, wdt, wdtt,
              rowv(w["shift_mu"]), rowv(w["w0"]), w["w2"].astype(BF16), rowv(w["a0"]),
              w["a2"].astype(BF16), w["g2"].astype(BF16), rowv(w["k_k"]), rowv(w["k_a"]), rowv(w["r_k"])]
    ahead = lambda i: jnp.minimum(i, nt - 1)
    behind = lambda i: jnp.maximum(i - 1, 0)
    row_a = lambda w_: pl.BlockSpec((tm, w_), lambda i: (ahead(i), 0))
    seq_b = lambda w_: pl.BlockSpec((1, tm, w_), lambda i: (behind(i) // tps, behind(i) % tps, 0))
    one_b = pl.BlockSpec((1, 1, RWKV_PROJ), lambda i: (behind(i) // tps, 0, 0))
    per_tile = tm // chunk
    gend_spec = pl.BlockSpec((1, per_tile, 8, RWKV_WIDTH), lambda i: (behind(i) // tps, behind(i) % tps, 0, 0))
    sds = jax.ShapeDtypeStruct
    outs = pl.pallas_call(
        functools.partial(_proj_prep_kernel, tm=tm, tiles_per_seq=tps, chunk=chunk),
        grid=(nt + 1,),
        in_specs=[row_a(D_MODEL), one_b] + [_const_spec(t.shape) for t in consts],
        out_specs=[row_a(SSD_WIDTH), row_a(SSD_CONV_DIM), row_a(SSD_HEADS),
                   pl.BlockSpec((SSD_HEADS, tm), lambda i: (0, ahead(i)))]
                  + [seq_b(RWKV_WIDTH)] * 7 + [gend_spec] + [seq_b(RWKV_WIDTH)] * 2 + [one_b],
        out_shape=[sds((n, SSD_WIDTH), F32), sds((n, SSD_CONV_DIM), F32), sds((n, SSD_HEADS), F32),
                   sds((SSD_HEADS, n), F32)]
                  + [sds((b, l, RWKV_WIDTH), BF16)] * 7 + [sds((b, l // chunk, 8, RWKV_WIDTH), F32)]
                  + [sds((b, l, RWKV_WIDTH), F32)] * 2 + [sds((b, 1, RWKV_PROJ), F32)],
        scratch_shapes=[pltpu.VMEM((tm, RWKV_PROJ), F32), pltpu.VMEM((CONV_PAD + tm, RWKV_PROJ), F32)],
        compiler_params=pltpu.CompilerParams(
            dimension_semantics=("arbitrary",), vmem_limit_bytes=V7X_VMEM_LIMIT),
        name="proj_prep",
    )(x2, shift0, *consts)
    return outs


def _wkv_kernel(r_ref, w_ref, k_ref, v_ref, kk_ref, kka_ref, s0_ref,
                o_ref, sfin_ref, s_scr, vt_scr, ot_scr, *, steps):
    c = pl.program_id(1)
    n = RWKV_HEAD_DIM
    lanes = WKV_BATCH_BLOCK * RWKV_HEADS

    @pl.when(c == 0)
    def _():
        s_scr[...] = s0_ref[...].reshape(lanes, n * n).T.reshape(n, n, lanes)

    def to_pairs(ref, t):
        return ref[:, t].reshape(lanes, n).T

    def step(t, carry):
        r_t = to_pairs(r_ref, t)
        w_t = to_pairs(w_ref, t)
        k_t = to_pairs(k_ref, t)
        kk_t = to_pairs(kk_ref, t)
        kka_t = to_pairs(kka_ref, t)
        vt_scr[...] = to_pairs(v_ref, t)

        def per_value(vi, carry2):
            s_v = s_scr[vi]
            skk = jnp.sum(s_v * kk_t, axis=0, keepdims=True)
            v_row = vt_scr[pl.ds(vi, 1), :]
            s_new = s_v * w_t - skk * kka_t + v_row * k_t
            s_scr[vi] = s_new
            ot_scr[pl.ds(vi, 1), :] = jnp.sum(s_new * r_t, axis=0, keepdims=True)
            return carry2

        lax.fori_loop(0, n, per_value, 0, unroll=4)
        o_ref[:, t] = ot_scr[...].T.reshape(WKV_BATCH_BLOCK, RWKV_HEADS, n)
        return carry

    lax.fori_loop(0, steps, step, 0)

    @pl.when(c == pl.num_programs(1) - 1)
    def _():
        sfin_ref[...] = s_scr[...].reshape(n * n, lanes).T.reshape(
            WKV_BATCH_BLOCK, RWKV_HEADS, n, n)


def wkv_scan(r, w, k, v, kk, kka, s0, *, steps):
    b, l, _ = r.shape
    h, n = RWKV_HEADS, RWKV_HEAD_DIM
    assert b % WKV_BATCH_BLOCK == 0 and l % steps == 0
    ops = [t.reshape(b, l, h, n) for t in (r, w, k, v, kk, kka)]
    seq_spec = pl.BlockSpec((WKV_BATCH_BLOCK, steps, h, n), lambda g, c: (g, c, 0, 0))
    st_spec = pl.BlockSpec((WKV_BATCH_BLOCK, h, n, n), lambda g, c: (g, 0, 0, 0))
    o, s_fin = pl.pallas_call(
        functools.partial(_wkv_kernel, steps=steps),
        grid=(b // WKV_BATCH_BLOCK, l // steps),
        in_specs=[seq_spec] * 6 + [st_spec],
        out_specs=[seq_spec, st_spec],
        out_shape=[jax.ShapeDtypeStruct((b, l, h, n), F32),
                   jax.ShapeDtypeStruct((b, h, n, n), F32)],
        scratch_shapes=[pltpu.VMEM((n, n, WKV_BATCH_BLOCK * h), F32),
                        pltpu.VMEM((n, WKV_BATCH_BLOCK * h), F32),
                        pltpu.VMEM((n, WKV_BATCH_BLOCK * h), F32)],
        compiler_params=pltpu.CompilerParams(
            dimension_semantics=("parallel", "arbitrary"), vmem_limit_bytes=V7X_VMEM_LIMIT),
        name="wkv_scan",
    )(*ops, s0)
    return o.reshape(b, l, h * n), s_fin


WKV_LANE_BATCH = 128


def _wkv_batch_lanes_kernel(r_ref, w_ref, k_ref, v_ref, kk_ref, kka_ref, s0_ref,
                            o_ref, sfin_ref, op_scr, ot_scr, *, steps):
    n = RWKV_HEAD_DIM
    nb = WKV_LANE_BATCH
    sfin_ref[...] = s0_ref[...]

    def step(t, carry):
        rows = pl.ds(t, nb, stride=steps)
        for i, ref in enumerate((r_ref, w_ref, k_ref, kk_ref, kka_ref, v_ref)):
            op_scr[i] = ref[rows, :].T
        for h2 in range(2):
            ch = slice(h2 * n, (h2 + 1) * n)
            r_t = op_scr[0, ch, :]
            kka_r = jnp.sum(op_scr[4, ch, :] * r_t, axis=0, keepdims=True)
            k_r = jnp.sum(op_scr[2, ch, :] * r_t, axis=0, keepdims=True)
            op_scr[0, ch, :] = op_scr[1, ch, :] * r_t

            def per_value(vi, carry2, ch=ch, h2=h2, kka_r=kka_r, k_r=k_r):
                s_v = sfin_ref[h2, vi]
                skk = jnp.sum(s_v * op_scr[3, ch, :], axis=0, keepdims=True)
                out = jnp.sum(s_v * op_scr[0, ch, :], axis=0, keepdims=True)
                v_row = op_scr[5, pl.ds(h2 * n + vi, 1), :]
                sfin_ref[h2, vi] = s_v * op_scr[1, ch, :] - skk * op_scr[4, ch, :] + v_row * op_scr[2, ch, :]
                ot_scr[pl.ds(h2 * n + vi, 1), :] = out - skk * kka_r + v_row * k_r
                return carry2

            lax.fori_loop(0, n, per_value, 0, unroll=8)
        o_ref[rows, :] = ot_scr[...].T
        return carry

    lax.fori_loop(0, steps, step, 0)


def wkv_scan_batch_lanes(r, w, k, v, kk, kka, s0):
    b, l, wd = r.shape
    h, n, nb = RWKV_HEADS, RWKV_HEAD_DIM, WKV_LANE_BATCH
    assert b % nb == 0
    ops = [t.reshape(b * l, wd) for t in (r, w, k, v, kk, kka)]
    s0t = jnp.transpose(s0, (1, 2, 3, 0))
    seq_spec = pl.BlockSpec((nb * l, 2 * n), lambda g, j: (g, j))
    st_spec = pl.BlockSpec((2, n, n, nb), lambda g, j: (j, 0, 0, g))
    o, s_fin = pl.pallas_call(
        functools.partial(_wkv_batch_lanes_kernel, steps=l),
        grid=(b // nb, h // 2),
        in_specs=[seq_spec] * 6 + [st_spec],
        out_specs=[seq_spec, st_spec],
        out_shape=[jax.ShapeDtypeStruct((b * l, wd), F32), jax.ShapeDtypeStruct((h, n, n, b), F32)],
        scratch_shapes=[pltpu.VMEM((6, 2 * n, nb), F32), pltpu.VMEM((2 * n, nb), F32)],
        compiler_params=pltpu.CompilerParams(
            dimension_semantics=("parallel", "parallel"), vmem_limit_bytes=V7X_VMEM_LIMIT),
        name="wkv_scan_batch_lanes",
    )(*ops, s0t)
    return o.reshape(b, l, wd), jnp.transpose(s_fin, (3, 0, 1, 2))


WKV_CHUNK = 64
WKV_PAIRS = RWKV_HEADS // 2
WKV_ROW_STRIDE = WKV_CHUNK + 8
WKV_LOCKSTEP = 8
WKV_SOLVE_ROWS = 8
WKV_SOLVE_COLS = 16


def _pair_masks():
    c = WKV_CHUNK
    row = lax.broadcasted_iota(jnp.int32, (2 * c, 2 * c), 0)
    col = lax.broadcasted_iota(jnp.int32, (2 * c, 2 * c), 1)
    t, i = row % c, col % c
    keep = i <= t - jnp.where(row < c, 1, 0)
    sign = jnp.where(row >= c, jnp.where(col < c, -1.0, 1.0), 1.0)
    block_diag = row // c == col // c
    return keep, sign, block_diag


def _wkv_prepare_kernel(kkt_ref, rt_ref, kbg_ref, kfg_ref, vb_ref,
                        lo_ref, rhs0_ref, tp_ref, abs_scr, top_scr, abt_scr, tt_scr):
    c = WKV_CHUNK
    keep, sign, _ = _pair_masks()
    lane = lax.broadcasted_iota(jnp.int32, (1, 2 * c), 1)
    head0 = lane < RWKV_HEAD_DIM
    row_head0 = lax.broadcasted_iota(jnp.int32, (2 * RWKV_HEAD_DIM, 1), 0) < RWKV_HEAD_DIM
    zeros = jnp.zeros((c, 2 * c), BF16)

    def one_batch(b, k):
        for j in range(WKV_PAIRS):
            sl = slice(j * 2 * RWKV_HEAD_DIM, (j + 1) * 2 * RWKV_HEAD_DIM)
            lhs = jnp.concatenate([kkt_ref[b, :, sl], rt_ref[b, :, sl]], axis=0)
            rhs = jnp.concatenate([kbg_ref[b, :, sl], kfg_ref[b, :, sl]], axis=0)
            rhs_t = rhs.astype(F32).T
            top = jnp.where(row_head0, rhs_t, 0.0)
            w_a = jnp.concatenate([top, rhs_t - top], axis=1).astype(BF16)
            a_both = jnp.dot(lhs, w_a, preferred_element_type=F32)
            for h2 in range(2):
                a = a_both[:, h2 * 2 * c:(h2 + 1) * 2 * c]
                a = jnp.where(keep, a, 0.0) * sign
                inst = h2 * (WKV_BATCH_BLOCK * WKV_PAIRS) + b * WKV_PAIRS + j
                abs_scr[pl.ds(inst * WKV_ROW_STRIDE, c), :c] = a[:c, :c]
                top_scr[k, j, :, h2 * 2 * c:(h2 + 1) * 2 * c] = a[:c].astype(BF16)
                lo_ref[b, :, (2 * j + h2) * 2 * c:(2 * j + h2 + 1) * 2 * c] = a[c:].astype(BF16)
        yield
        for j in range(WKV_PAIRS):
            sl = slice(j * 2 * RWKV_HEAD_DIM, (j + 1) * 2 * RWKV_HEAD_DIM)
            v = vb_ref[b, :, sl]
            v0 = jnp.where(head0, v, jnp.zeros_like(v))
            w_akf = jnp.concatenate([zeros, v0, zeros, v - v0], axis=0)
            rhs0_ref[b, :, sl] = jnp.dot(top_scr[k, j], w_akf, preferred_element_type=F32)
        yield

    def two_batches(bb, carry):
        for _ in zip(*[one_batch(WKV_LOCKSTEP * bb + k, k) for k in range(WKV_LOCKSTEP)]):
            pass
        return carry

    lax.fori_loop(0, WKV_BATCH_BLOCK // WKV_LOCKSTEP, two_batches, 0)

    n_inst = 2 * WKV_BATCH_BLOCK * WKV_PAIRS
    n_pair_rows = WKV_BATCH_BLOCK * WKV_PAIRS

    def to_lanes(t, carry):
        abt_scr[t] = abs_scr[pl.ds(t, n_inst, stride=WKV_ROW_STRIDE), :][:, :c].T
        return carry

    lax.fori_loop(0, c, to_lanes, 0, unroll=16)

    tt_scr[...] = jnp.zeros(tt_scr.shape, F32)
    sub_iota = lax.broadcasted_iota(jnp.int32, (WKV_SOLVE_COLS, n_inst), 0)
    rows = range(WKV_SOLVE_ROWS)
    for cb in range(c // WKV_SOLVE_COLS):
        col0 = WKV_SOLVE_COLS * cb
        cols = slice(col0, col0 + WKV_SOLVE_COLS)
        first_block = col0 // WKV_SOLVE_ROWS

        def solve_rows(tb, carry, col0=col0, cols=cols, first_block=first_block):
            t0 = tb * WKV_SOLVE_ROWS

            def sub(ib, accs):
                ps = [tt_scr[ib * WKV_SOLVE_ROWS + di, cols, :] for di in rows]
                out = []
                for r in rows:
                    terms = [abt_scr[t0 + r, pl.ds(ib * WKV_SOLVE_ROWS + di, 1), :] * ps[di] for di in rows]
                    while len(terms) > 1:
                        terms = [a + b for a, b in zip(terms[::2], terms[1::2])]
                    out.append(accs[r] - terms[0])
                return tuple(out)

            unit = tuple(jnp.where(sub_iota + col0 == t0 + r, 1.0, 0.0) for r in rows)
            accs = list(lax.fori_loop(first_block, tb, sub, unit))
            for r in rows:
                for r2 in range(r):
                    accs[r] = accs[r] - abt_scr[t0 + r, pl.ds(t0 + r2, 1), :] * accs[r2]
                tt_scr[t0 + r, cols, :] = accs[r]
            return carry

        lax.fori_loop(first_block, c // WKV_SOLVE_ROWS, solve_rows, 0)

    def from_lanes(t, carry):
        m = tt_scr[t].T
        abs_scr[pl.ds(t, n_pair_rows, stride=WKV_ROW_STRIDE), :] = jnp.concatenate(
            [m[:n_pair_rows], m[n_pair_rows:]], axis=1)
        return carry

    lax.fori_loop(0, c, from_lanes, 0, unroll=32)

    def emit(b, carry):
        for j in range(WKV_PAIRS):
            row0 = (b * WKV_PAIRS + j) * WKV_ROW_STRIDE
            tp_ref[b, :, j * 2 * c:(j + 1) * 2 * c] = abs_scr[pl.ds(row0, c), :].astype(BF16)
        return carry

    lax.fori_loop(0, WKV_BATCH_BLOCK, emit, 0)


def _wkv_apply_kernel(kkt_ref, rt_ref, kfh_ref, nbh_ref, vb_ref, tp_ref, lo_ref, rhs0_ref, gend_ref, s0_ref,
                      o_ref, sfin_ref, x_scr, wp_scr, p_scr):
    c = WKV_CHUNK
    n = RWKV_HEAD_DIM
    ch = pl.program_id(1)
    _, _, block_diag = _pair_masks()
    lane = lax.broadcasted_iota(jnp.int32, (1, 2 * c), 1)
    head0 = lane < n
    eye2 = (lax.broadcasted_iota(jnp.int32, (n, 2 * n), 0)
            == lax.broadcasted_iota(jnp.int32, (n, 2 * n), 1) % n).astype(F32)

    @pl.when(ch == 0)
    def _():
        def init(b, carry):
            for j in range(WKV_PAIRS):
                sp = s0_ref[b, 2 * j:2 * j + 2].reshape(2 * n, n)
                dup = jnp.dot(sp, eye2, precision=HIGHEST, preferred_element_type=F32)
                x_scr[b, j] = jnp.where(block_diag, dup, 0.0)
            return carry
        lax.fori_loop(0, WKV_BATCH_BLOCK, init, 0)

    def one_batch(b, k):
        for j in range(WKV_PAIRS):
            sl = slice(j * 2 * n, (j + 1) * 2 * n)
            lhs = jnp.concatenate([kkt_ref[b, :, sl], rt_ref[b, :, sl]], axis=0)
            kx = lax.dot_general(lhs, x_scr[b, j].astype(BF16), (((1,), (1,)), ((), ())),
                                 preferred_element_type=F32)
            rhs = kx[:c] + rhs0_ref[b, :, sl]
            r0 = jnp.where(head0, rhs, 0.0)
            wp_scr[k, j] = jnp.concatenate([r0, rhs - r0], axis=0).astype(BF16)
            o_ref[b, :, sl] = kx[c:]
        yield
        for j in range(WKV_PAIRS):
            sl = slice(j * 2 * n, (j + 1) * 2 * n)
            p = jnp.dot(tp_ref[b, :, sl], wp_scr[k, j], preferred_element_type=F32)
            p_scr[k, j] = p.astype(BF16)
        yield
        for j in range(WKV_PAIRS):
            sl = slice(j * 2 * n, (j + 1) * 2 * n)
            v = vb_ref[b, :, sl]
            pb = p_scr[k, j]
            zero = jnp.zeros_like(pb)
            p0, v0 = jnp.where(head0, pb, zero), jnp.where(head0, v, zero)
            w_o = jnp.concatenate([p0, v0, pb - p0, v - v0], axis=0)
            o_ref[b, :, sl] = o_ref[b, :, sl] + jnp.dot(
                lo_ref[b, :, j * 4 * c:(j + 1) * 4 * c], w_o, preferred_element_type=F32)
            vp = jnp.concatenate([v, pb], axis=0)
            kb = jnp.concatenate([kfh_ref[b, :, sl], nbh_ref[b, :, sl]], axis=0)
            upd = lax.dot_general(vp, kb, (((0,), (0,)), ((), ())), preferred_element_type=F32)
            x_scr[b, j] = jnp.where(block_diag, x_scr[b, j] * gend_ref[b, 0, 0:1, sl] + upd, 0.0)
        yield

    def two_batches(bb, carry):
        for _ in zip(*[one_batch(WKV_LOCKSTEP * bb + k, k) for k in range(WKV_LOCKSTEP)]):
            pass
        return carry

    lax.fori_loop(0, WKV_BATCH_BLOCK // WKV_LOCKSTEP, two_batches, 0)

    @pl.when(ch == pl.num_programs(1) - 1)
    def _():
        def fin(b, carry):
            for j in range(WKV_PAIRS):
                sp = lax.dot_general(x_scr[b, j], eye2, (((1,), (1,)), ((), ())),
                                     precision=HIGHEST, preferred_element_type=F32)
                sfin_ref[b, 2 * j:2 * j + 2] = sp.reshape(2, n, n)
            return carry
        lax.fori_loop(0, WKV_BATCH_BLOCK, fin, 0)


def wkv_chunked(kkt, rt, kfh, nbh, vb, kbg, kfg, gend, s0):
    b, l, wd = kkt.shape
    c = WKV_CHUNK
    assert b % WKV_BATCH_BLOCK == 0 and l % c == 0
    gb, nc = b // WKV_BATCH_BLOCK, l // c
    lanes = 2 * WKV_BATCH_BLOCK * WKV_PAIRS
    seq = lambda w_: pl.BlockSpec((WKV_BATCH_BLOCK, c, w_), lambda g, i: (g, i, 0))
    gend_spec = pl.BlockSpec((WKV_BATCH_BLOCK, 1, 8, wd), lambda g, i: (g, i, 0, 0))
    sds = jax.ShapeDtypeStruct
    st_spec = pl.BlockSpec((WKV_BATCH_BLOCK, RWKV_HEADS, RWKV_HEAD_DIM, RWKV_HEAD_DIM),
                           lambda g, i: (g, 0, 0, 0))
    blk = (WKV_BATCH_BLOCK, c, wd)
    o, s_fin = pl.pallas_call(
        _wkv_chunk_kernel,
        grid=(gb, nc),
        in_specs=[seq(wd)] * 7 + [gend_spec, st_spec],
        out_specs=[seq(wd), st_spec],
        out_shape=[sds((b, l, wd), F32), sds(s0.shape, F32)],
        scratch_shapes=[
            pltpu.VMEM((lanes * WKV_ROW_STRIDE, 2 * c), F32),
            pltpu.VMEM((WKV_LOCKSTEP, WKV_PAIRS, c, 4 * c), BF16),
            pltpu.VMEM((c, c, lanes), F32),
            pltpu.VMEM((c, c, lanes), F32),
            pltpu.VMEM((WKV_BATCH_BLOCK, c, 2 * wd), BF16),
            pltpu.VMEM(blk, F32),
            pltpu.VMEM(blk, BF16),
            pltpu.VMEM((WKV_BATCH_BLOCK, WKV_PAIRS, 2 * RWKV_HEAD_DIM, 2 * RWKV_HEAD_DIM), F32),
            pltpu.VMEM((WKV_LOCKSTEP, WKV_PAIRS, 2 * c, 2 * RWKV_HEAD_DIM), BF16),
            pltpu.VMEM((WKV_LOCKSTEP, WKV_PAIRS, c, 2 * RWKV_HEAD_DIM), BF16)],
        compiler_params=pltpu.CompilerParams(
            dimension_semantics=("parallel", "arbitrary"), vmem_limit_bytes=V7X_VMEM_LIMIT),
        name="wkv_chunked",
    )(kkt, rt, kfh, nbh, vb, kbg, kfg, gend, s0)
    return o, s_fin


def _wkv_chunk_kernel(kkt_ref, rt_ref, kfh_ref, nbh_ref, vb_ref, kbg_ref, kfg_ref, gend_ref, s0_ref,
                      o_ref, sfin_ref, abs_scr, top_scr, abt_scr, tt_scr, lo_scr, rhs0_scr, tp_scr,
                      x_scr, wp_scr, p_scr):
    _wkv_prepare_kernel(kkt_ref, rt_ref, kbg_ref, kfg_ref, vb_ref, lo_scr, rhs0_scr, tp_scr,
                        abs_scr, top_scr, abt_scr, tt_scr)
    _wkv_apply_kernel(kkt_ref, rt_ref, kfh_ref, nbh_ref, vb_ref, tp_scr, lo_scr, rhs0_scr, gend_ref, s0_ref,
                      o_ref, sfin_ref, x_scr, wp_scr, p_scr)


def _tail_kernel(o_ref, gate_ref, bonus_ref, x_ref, ys_ref, p_ref, lnw_ref, lnb_ref, woa_ref, wob_ref,
                 nf_ref, wg_ref, wu_ref, wd_ref, np_ref, wpg_ref, wpp_ref, nl_ref, y_ref, new_scr, cur_scr):
    i = pl.program_id(0)

    @pl.when(i == 0)
    def _():
        new_scr[...] = jnp.zeros(new_scr.shape, BF16)

    cur_scr[...] = new_scr[...]

    parts = 4
    rows = o_ref.shape[0] // parts
    inv_n = 1.0 / RWKV_HEAD_DIM

    def vector_half(part):
        rs = slice(part * rows, (part + 1) * rows)
        o = o_ref[rs, :]
        mu = _head_sums(o) * inv_n
        d = o - mu
        var = _head_sums(d * d) * inv_n
        on = d * lax.rsqrt(var + GN_EPS) * lnw_ref[...] + lnb_ref[...]
        new_scr[rs, :] = ((on + bonus_ref[rs, :]) * gate_ref[rs, :]).astype(BF16)

    vector_half(0)
    y_rwkv = cur_scr[...]
    h = x_ref[...] + _bdot(ys_ref[...], woa_ref[...]) + jnp.dot(y_rwkv, wob_ref[...],
                                                                preferred_element_type=F32)
    hf = _rms(h, nf_ref[...]).astype(BF16)
    vector_half(1)
    gate = jnp.dot(hf, wg_ref[...], preferred_element_type=F32)
    up = jnp.dot(hf, wu_ref[...], preferred_element_type=F32)
    vector_half(2)
    h = h + _bdot(_silu(gate) * up, wd_ref[...])
    vector_half(3)
    pg = _sigmoid(_bdot(_rms(h, np_ref[...]), wpg_ref[...]))
    h = h + pg * _bdot(p_ref[...], wpp_ref[...])
    y_ref[...] = _rms(h, nl_ref[...])


def layer_tail(o, gate, bonus, x, y_ssd, p, consts, *, tm):
    n = x.shape[0]
    nt = n // tm
    ahead = lambda w_: pl.BlockSpec((tm, w_), lambda i: (jnp.minimum(i, nt - 1), 0))
    behind = lambda w_: pl.BlockSpec((tm, w_), lambda i: (jnp.maximum(i - 1, 0), 0))
    return pl.pallas_call(
        _tail_kernel,
        grid=(nt + 1,),
        in_specs=[ahead(D_MODEL)] * 3 + [behind(D_MODEL)] * 2 + [behind(PLE_DIM)]
                 + [_const_spec(t.shape) for t in consts],
        out_specs=behind(D_MODEL),
        out_shape=jax.ShapeDtypeStruct((n, D_MODEL), F32),
        scratch_shapes=[pltpu.VMEM((tm, D_MODEL), BF16), pltpu.VMEM((tm, D_MODEL), BF16)],
        compiler_params=pltpu.CompilerParams(
            dimension_semantics=("arbitrary",), vmem_limit_bytes=V7X_VMEM_LIMIT),
        name="layer_tail",
    )(o, gate, bonus, x, y_ssd, p, *consts)


def _prepare_weights(w):
    c0, c1, c2 = SSD_WIDTH, SSD_WIDTH + SSD_CONV_DIM, SSD_WIDTH + SSD_CONV_DIM + SSD_HEADS
    w_in = w["w_in"]
    rowv = lambda t: t.reshape(1, -1)
    return dict(
        w,
        wz=w_in[:, :c0].astype(BF16), wx=w_in[:, c0:c1].astype(BF16),
        wdt=w_in[:, c1:c2].astype(BF16), wr=w_in[:, c2:].astype(BF16),
        woa=w["w_out"][:SSD_WIDTH].astype(BF16), wob=w["w_out"][SSD_WIDTH:].astype(BF16),
        wg=w["w_gate"].astype(BF16), wu=w["w_up"].astype(BF16), wd=w["w_down"].astype(BF16),
        wpg=w["w_ple_gate"].astype(BF16), wpp=w["w_ple_proj"].astype(BF16),
        norm_mix_r=rowv(w["norm_mix"]), norm_ffn_r=rowv(w["norm_ffn"]),
        norm_ple_r=rowv(w["norm_ple"]), norm_final_r=rowv(w["norm_final"]),
        ln_x_w_r=rowv(w["ln_x_w"]), ln_x_b_r=rowv(w["ln_x_b"]),
    )


TOKEN_TILE = 256
PREP_TILE = 128
SCAN_STEPS = 16


def layer_forward(x, p, conv0, shift0, ssm0, wkv0, w):
    b, l, _ = x.shape
    n = b * l
    tm = min(TOKEN_TILE, n)
    ssd_q = SSD_CHUNK if l % SSD_CHUNK == 0 else l
    prep_tt = min(PREP_TILE, n)
    wkv_steps = min(SCAN_STEPS, l)
    x2 = x.reshape(n, D_MODEL)
    chunked = l % WKV_CHUNK == 0 and l % tm == 0
    if chunked:
        z, xbc, dt, dtt, *ops, gate, bonus, shift_new = proj_prep(
            x, shift0, w["norm_mix_r"], w["wz"], w["wx"], w["wr"], w["wdt"], w, tm=tm, chunk=WKV_CHUNK)
    else:
        z, xbc, rw, dt, dtt = in_projection(x2, w["norm_mix_r"], w["wz"], w["wx"], w["wr"], w["wdt"], tm=tm)
        *ops, gate, bonus, shift_new = rwkv_prep(rw.reshape(b, l, -1), shift0, w, tt=prep_tt, chunk=0)
    y_ssd, ssm_new, conv_new = ssd_mixer(
        z.reshape(b, l, -1), xbc.reshape(b, l, -1), dt.reshape(b, l, -1), dtt, conv0, ssm0, w, q=ssd_q)
    if chunked:
        o, wkv_new = wkv_chunked(*ops, wkv0)
    elif b % WKV_LANE_BATCH == 0:
        o, wkv_new = wkv_scan_batch_lanes(*ops, wkv0)
    else:
        o, wkv_new = wkv_scan(*ops, wkv0, steps=wkv_steps)
    flat = lambda t: t.reshape(n, -1)
    tail_consts = [w["ln_x_w_r"], w["ln_x_b_r"], w["woa"], w["wob"], w["norm_ffn_r"], w["wg"], w["wu"],
                   w["wd"], w["norm_ple_r"], w["wpg"], w["wpp"], w["norm_final_r"]]
    y = layer_tail(flat(o), flat(gate), flat(bonus), x2, flat(y_ssd), p.reshape(n, PLE_DIM),
                   tail_consts, tm=tm)
    return y.reshape(b, l, D_MODEL), ssm_new, conv_new, wkv_new, shift_new


def kernel(x_prompt, x_sample, state_ssm, state_conv, state_wkv, state_shift, p_prompt, p_sample, norm_mix, w_in, conv_w, conv_b, dt_bias, a_log, d_skip, ssd_norm, shift_mu, w0, w2, a0, a2, g2, k_k, k_a, r_k, ln_x_w, ln_x_b, w_out, norm_ffn, w_gate, w_up, w_down, norm_ple, w_ple_gate, w_ple_proj, norm_final):
    w = _prepare_weights(dict(
        norm_mix=norm_mix[0], w_in=w_in[0], conv_w=conv_w[0], conv_b=conv_b[0], dt_bias=dt_bias[0],
        a_log=a_log[0], d_skip=d_skip[0], ssd_norm=ssd_norm[0], shift_mu=shift_mu[0], w0=w0[0],
        w2=w2[0], a0=a0[0], a2=a2[0], g2=g2[0], k_k=k_k[0], k_a=k_a[0], r_k=r_k[0],
        ln_x_w=ln_x_w[0], ln_x_b=ln_x_b[0], w_out=w_out[0], norm_ffn=norm_ffn[0],
        w_gate=w_gate[0], w_up=w_up[0], w_down=w_down[0], norm_ple=norm_ple[0],
        w_ple_gate=w_ple_gate[0], w_ple_proj=w_ple_proj[0], norm_final=norm_final))
    bp = x_prompt.shape[0]
    zeros = lambda *s: jnp.zeros(s, F32)
    yp, s1, c1, k1, t1 = layer_forward(
        x_prompt, p_prompt[0], zeros(bp, SSD_CONV - 1, SSD_CONV_DIM), zeros(bp, 1, RWKV_PROJ),
        zeros(bp, SSD_HEADS, SSD_HEAD_DIM, SSD_STATE),
        zeros(bp, RWKV_HEADS, RWKV_HEAD_DIM, RWKV_HEAD_DIM), w)
    ys, s2, c2, k2, t2 = layer_forward(
        x_sample, p_sample[0], state_conv[0], state_shift[0], state_ssm[0], state_wkv[0], w)
    return (yp, ys, s1[None], c1[None], k1[None], t1[None], s2[None], c2[None], k2[None], t2[None])
```
